```python
import numpy as np
import jax
import jax.numpy as jnp
from jax import lax

D_MODEL = 1024
BATCH = 16
SEQ = 2048
DEPTH = 4

GRID_W = 64
CTX_LEN = 256
EPS = 1e-6
N_BRANCH = 3

M_HEADS = 4
M_DH = 128
M_WIDTH = M_HEADS * M_DH
M_CHUNK = 128
M_CONV = 3
M_FGATE_BIAS_LO = 3.0
M_FGATE_BIAS_HI = 6.0

A_HEADS = 8
A_NOPE = 64
A_ROPE = 32
A_VDIM = 64
A_QRANK = 384
A_KVRANK = 256
A_WIDTH = A_HEADS * A_VDIM
ATT_SCALE = (A_NOPE + A_ROPE) ** -0.5
Q_BLOCK = 128
ROPE_THETA = 10000.0

G_GROUPS = 4
G_CHUNK = 128
G_WIDTH = 512
G_DG = G_WIDTH // G_GROUPS

N_GROUPS = 4
EXP_PER_GROUP = 4
N_EXPERTS = N_GROUPS * EXP_PER_GROUP
TOP_K = 2
D_EXPERT = 512

IN_SPLIT = (M_WIDTH, M_WIDTH, M_WIDTH, M_WIDTH, 4 * M_HEADS, A_QRANK, A_KVRANK, A_ROPE,
            G_WIDTH, G_WIDTH, D_MODEL, D_MODEL, D_MODEL)
D_IN = sum(IN_SPLIT)

kernel_name = 'hybrid_mlstm_mla_gmlp_hmoe_diffusion_block'


def rmsnorm(x, g):
    xf = x.astype(jnp.float32)
    y = xf * lax.rsqrt(jnp.mean(xf * xf, axis=-1, keepdims=True) + EPS)
    return (y * g.astype(jnp.float32)).astype(x.dtype)


def modulate(h, shift, scale):
    return h * (1.0 + scale) + shift


def axial_rope_tables(rows):
    half = A_ROPE // 2
    r = jnp.repeat(jnp.arange(rows, dtype=jnp.float32), GRID_W)
    col = jnp.tile(jnp.arange(GRID_W, dtype=jnp.float32), rows)
    inv = ROPE_THETA ** (-jnp.arange(0, half, 2, dtype=jnp.float32) / half)
    ang = jnp.concatenate([r[:, None] * inv, col[:, None] * inv], axis=-1)
    return jnp.cos(ang), jnp.sin(ang)


def apply_rope(x, cos, sin):
    x1, x2 = jnp.split(x, 2, axis=-1)
    return jnp.concatenate([x1 * cos - x2 * sin, x2 * cos + x1 * sin], axis=-1).astype(x.dtype)


def dwconv_centered(x, w):
    pad = M_CONV // 2
    return lax.conv_general_dilated(
        x, w[:, None, :].astype(x.dtype), window_strides=(1,), padding=[(pad, pad)],
        dimension_numbers=('NWC', 'WIO', 'NWC'), feature_group_count=x.shape[-1])


def mlstm_scan(q, k, v, ig, fg, state):
    B, H, L, dh = q.shape
    nc = L // M_CHUNK
    to_chunks = lambda a: jnp.moveaxis(a.reshape(B, H, nc, M_CHUNK, *a.shape[3:]), 2, 0)
    prefix = jnp.tril(jnp.ones((M_CHUNK, M_CHUNK), dtype=bool))

    def step(carry, inp):
        C, n, m = carry
        qc, kc, vc, ic, lfc = inp
        b = jnp.cumsum(lfc, axis=-1)
        dmat = jnp.where(prefix, b[..., :, None] - b[..., None, :] + ic[..., None, :], -jnp.inf)
        inter = b + m[..., None]
        mt = jnp.maximum(inter, jnp.max(dmat, axis=-1))
        w_intra = jnp.exp(dmat - mt[..., None])
        w_inter = jnp.exp(inter - mt)
        qk = jnp.einsum('bhtd,bhsd->bhts', qc, kc) * w_intra
        num = jnp.einsum('bhts,bhse->bhte', qk, vc) + w_inter[..., None] * jnp.einsum('bhed,bhtd->bhte', C, qc)
        den = jnp.sum(qk, axis=-1) + w_inter * jnp.einsum('bhd,bhtd->bht', n, qc)
        h = num / jnp.maximum(jnp.abs(den), jnp.exp(-mt))[..., None]
        b_end = b[..., -1:]
        d_end = b_end - b + ic
        m_new = jnp.maximum(b_end[..., 0] + m, jnp.max(d_end, axis=-1))
        w_s = jnp.exp(d_end - m_new[..., None])
        w_c = jnp.exp(b_end[..., 0] + m - m_new)
        C_new = w_c[..., None, None] * C + jnp.einsum('bhs,bhse,bhsd->bhed', w_s, vc, kc)
        n_new = w_c[..., None] * n + jnp.einsum('bhs,bhsd->bhd', w_s, kc)
        return (C_new, n_new, m_new), h

    xs = (to_chunks(q), to_chunks(k), to_chunks(v), to_chunks(ig.astype(jnp.float32)),
          to_chunks(jax.nn.log_sigmoid(fg.astype(jnp.float32))))
    state, hs = lax.scan(step, state, xs)
    h = jnp.moveaxis(hs, 0, 2).reshape(B, H, L, dh)
    return h.astype(q.dtype), state


def mla_attend(q_nope, q_rope, k_nope, k_rope, v):
    s = jnp.einsum('bqhd,bkhd->bhqk', q_nope, k_nope) + jnp.einsum('bqhd,bkd->bhqk', q_rope, k_rope)
    p = jax.nn.softmax(s.astype(jnp.float32) * ATT_SCALE, axis=-1).astype(v.dtype)
    return jnp.einsum('bhqk,bkhd->bqhd', p, v)


def latent_attention_blocks(q_nope, q_rope, k_nope, k_rope, v):
    B, T, H, _ = q_nope.shape
    nb = T // Q_BLOCK
    blocks = lambda a: jnp.moveaxis(a.reshape(B, nb, Q_BLOCK, *a.shape[2:]), 1, 0)
    out = lax.map(lambda qs: mla_attend(qs[0], qs[1], k_nope, k_rope, v), (blocks(q_nope), blocks(q_rope)))
    return jnp.moveaxis(out, 0, 1).reshape(B, T, H, A_VDIM)


def chunk_spatial_gating(v, ws, bs):
    B, L, _ = v.shape
    vr = v.reshape(B, L // G_CHUNK, G_CHUNK, G_GROUPS, G_DG)
    out = jnp.einsum('gts,bnsgc->bntgc', ws, vr) + bs.T[:, :, None]
    return out.reshape(B, L, G_WIDTH)


def token_mixer(hc, hl, cos, sin, w_in, m_conv, m_gate_b, m_norm, a_qnorm, a_wuq, a_kvnorm, a_wukv,
                g_ws, g_bs, g_vnorm, w_pa, w_pb, w_pc, w_out, ctx_out):
    B, Lc, _ = hc.shape
    L = Lc + hl.shape[1]
    sl = slice(None) if ctx_out else slice(Lc, None)
    z = jnp.concatenate([hc, hl], axis=1) @ w_in
    split_at = np.cumsum(IN_SPLIT)[:-1].tolist()
    mq, mk, mv, mo, mg, aq, akv, akr, gu, gv, br_a, br_b, br_c = jnp.split(z, split_at, axis=-1)

    qk = jnp.concatenate([mq, mk], axis=-1)
    qk = jax.nn.silu(jnp.concatenate([dwconv_centered(qk[:, :Lc], m_conv), dwconv_centered(qk[:, Lc:], m_conv)], axis=1))
    heads = lambda a: a.reshape(B, L, M_HEADS, M_DH).transpose(0, 2, 1, 3)
    q = heads(qk[..., :M_WIDTH]) * (M_DH ** -0.5)
    k = heads(qk[..., M_WIDTH:])
    v = heads(mv)
    g = (mg.reshape(B, L, 4, M_HEADS) + m_gate_b.reshape(4, M_HEADS)).transpose(2, 0, 3, 1)
    st0 = (jnp.zeros((B, M_HEADS, M_DH, M_DH), jnp.float32), jnp.zeros((B, M_HEADS, M_DH), jnp.float32),
           jnp.zeros((B, M_HEADS), jnp.float32))

    def direction(ig, fg, flip):
        tf = (lambda a: jnp.flip(a, axis=2)) if flip else (lambda a: a)
        pc = lambda a: tf(a[:, :, :Lc])
        pl = lambda a: tf(a[:, :, Lc:])
        h_c, st = mlstm_scan(pc(q), pc(k), pc(v), pc(ig), pc(fg), st0)
        h_l, _ = mlstm_scan(pl(q), pl(k), pl(v), pl(ig), pl(fg), st)
        return tf(h_c), tf(h_l)

    hc_f, hl_f = direction(g[0], g[1], False)
    hc_b, hl_b = direction(g[2], g[3], True)
    h_m = jnp.concatenate([hc_f + hc_b, hl_f + hl_b], axis=2) if ctx_out else hl_f + hl_b
    y_a = rmsnorm(h_m.transpose(0, 2, 1, 3), m_norm.reshape(M_HEADS, M_DH)).reshape(B, -1, M_WIDTH)
    y_a = y_a * jax.nn.sigmoid(mo[:, sl])

    qa = (rmsnorm(aq, a_qnorm) @ a_wuq).reshape(B, L, A_HEADS, A_NOPE + A_ROPE)
    kva = (rmsnorm(akv, a_kvnorm) @ a_wukv).reshape(B, L, A_HEADS, A_NOPE + A_VDIM)
    q_nope, q_rope = qa[..., :A_NOPE], qa[..., A_NOPE:]
    k_nope, v_att = kva[..., :A_NOPE], kva[..., A_NOPE:]
    q_rope_l = apply_rope(q_rope[:, Lc:], cos[:, None, :], sin[:, None, :])
    k_rope = jnp.concatenate([akr[:, :Lc], apply_rope(akr[:, Lc:], cos, sin)], axis=1)
    o_att = latent_attention_blocks(q_nope[:, Lc:], q_rope_l, k_nope, k_rope, v_att)
    if ctx_out:
        o_c = mla_attend(q_nope[:, :Lc], q_rope[:, :Lc], k_nope[:, :Lc], k_rope[:, :Lc], v_att[:, :Lc])
        o_att = jnp.concatenate([o_c, o_att], axis=1)
    y_b = o_att.reshape(B, -1, A_WIDTH)

    gu = jax.nn.gelu(gu, approximate=False)
    gv = rmsnorm(jax.nn.gelu(gv, approximate=False).reshape(B, L, G_GROUPS, G_DG),
                 g_vnorm.reshape(G_GROUPS, G_DG)).reshape(B, L, G_WIDTH)
    s_gate = chunk_spatial_gating(gv[:, Lc:], g_ws, g_bs)
    if ctx_out:
        s_gate = jnp.concatenate([chunk_spatial_gating(gv[:, :Lc], g_ws, g_bs), s_gate], axis=1)
    y_c = gu[:, sl] * s_gate

    y = (jax.nn.sigmoid(br_a[:, sl]) * (y_a @ w_pa) + jax.nn.sigmoid(br_b[:, sl]) * (y_b @ w_pb)
         + jax.nn.sigmoid(br_c[:, sl]) * (y_c @ w_pc))
    out = y @ w_out
    if ctx_out:
        return out[:, :Lc], out[:, Lc:]
    return None, out


def hier_moe(h, r_group, r_group_b, r_expert, r_expert_b, w1, w3, w2):
    B, L, _ = h.shape
    g_logits = (h @ r_group).astype(jnp.float32) + r_group_b.astype(jnp.float32)
    g_sel = jnp.argmax(g_logits, axis=-1)
    g_prob = jnp.max(jax.nn.softmax(g_logits, axis=-1), axis=-1, keepdims=True)
    e_logits = ((h @ r_expert).astype(jnp.float32) + r_expert_b.astype(jnp.float32)).reshape(
        B, L, N_GROUPS, EXP_PER_GROUP)
    e_in_group = jnp.einsum('blg,blge->ble', jax.nn.one_hot(g_sel, N_GROUPS, dtype=jnp.float32), e_logits)
    top_v, top_i = lax.top_k(e_in_group, TOP_K)
    top_w = jax.nn.softmax(top_v, axis=-1) * g_prob
    expert_id = g_sel[..., None] * EXP_PER_GROUP + top_i
    combine = jnp.einsum('blk,blke->ble', top_w,
                         jax.nn.one_hot(expert_id, N_EXPERTS, dtype=jnp.float32)).astype(h.dtype)
    out = jnp.zeros_like(h)
    for e in range(N_EXPERTS):
        hidden = jax.nn.silu(h @ w1[e]) * (h @ w3[e])
        out = out + combine[..., e:e + 1] * (hidden @ w2[e])
    return out


def setup_inputs(seed: int = 0) -> dict:
    key = jax.random.key(seed)
    keys = jax.random.split(key, 48)
    count = [0]

    def nrm(shape, scale):
        k = keys[count[0]]
        count[0] += 1
        return jax.random.normal(k, shape, jnp.float32) * scale

    def gain(shape):
        return 1.0 + nrm(shape, 0.02)

    Ld = DEPTH
    D = D_MODEL
    f_bias = jnp.linspace(M_FGATE_BIAS_LO, M_FGATE_BIAS_HI, M_HEADS, dtype=jnp.float32)
    m_gate_b = jnp.stack([nrm((Ld, M_HEADS), 0.1), f_bias + nrm((Ld, M_HEADS), 0.1),
                          nrm((Ld, M_HEADS), 0.1), f_bias + nrm((Ld, M_HEADS), 0.1)], axis=1).reshape(Ld, 4 * M_HEADS)
    return {
        'x': nrm((BATCH, SEQ, D), 1.0),
        'c': nrm((BATCH, D), 1.0),
        'ctx': nrm((BATCH, CTX_LEN, D), 1.0),
        'c_ctx': nrm((D,), 1.0),
        'w_ada': nrm((Ld, D, 6 * D), 0.5 * D ** -0.5),
        'b_ada': nrm((Ld, 6 * D), 0.02),
        'norm1': gain((Ld, D)),
        'norm2': gain((Ld, D)),
        'final_norm': gain((D,)),
        'w_in': nrm((Ld, D, D_IN), D ** -0.5),
        'm_conv': nrm((Ld, M_CONV, 2 * M_WIDTH), M_CONV ** -0.5),
        'm_gate_b': m_gate_b,
        'm_norm': gain((Ld, M_WIDTH)),
        'a_qnorm': gain((Ld, A_QRANK)),
        'a_wuq': nrm((Ld, A_QRANK, A_HEADS * (A_NOPE + A_ROPE)), A_QRANK ** -0.5),
        'a_kvnorm': gain((Ld, A_KVRANK)),
        'a_wukv': nrm((Ld, A_KVRANK, A_HEADS * (A_NOPE + A_VDIM)), A_KVRANK ** -0.5),
        'g_ws': nrm((Ld, G_GROUPS, G_CHUNK, G_CHUNK), G_CHUNK ** -0.5),
        'g_bs': gain((Ld, G_GROUPS, G_CHUNK)),
        'g_vnorm': gain((Ld, G_WIDTH)),
        'w_pa': nrm((Ld, M_WIDTH, D), M_WIDTH ** -0.5),
        'w_pb': nrm((Ld, A_WIDTH, D), A_WIDTH ** -0.5),
        'w_pc': nrm((Ld, G_WIDTH, D), G_WIDTH ** -0.5),
        'w_out': nrm((Ld, D, D), D ** -0.5),
        'r_group': nrm((Ld, D, N_GROUPS), D ** -0.5),
        'r_group_b': nrm((Ld, N_GROUPS), 0.01),
        'r_expert': nrm((Ld, D, N_EXPERTS), D ** -0.5),
        'r_expert_b': nrm((Ld, N_EXPERTS), 0.01),
        'e_w1': nrm((Ld, N_EXPERTS, D, D_EXPERT), D ** -0.5),
        'e_w3': nrm((Ld, N_EXPERTS, D, D_EXPERT), D ** -0.5),
        'e_w2': nrm((Ld, N_EXPERTS, D_EXPERT, D), D_EXPERT ** -0.5),
    }


def reference(x, c, ctx, c_ctx, w_ada, b_ada, norm1, norm2, final_norm, w_in, m_conv, m_gate_b, m_norm,
              a_qnorm, a_wuq, a_kvnorm, a_wukv, g_ws, g_bs, g_vnorm, w_pa, w_pb, w_pc, w_out,
              r_group, r_group_b, r_expert, r_expert_b, e_w1, e_w3, e_w2):
    T = x.shape[1]
    ROWS = T // GRID_W
    cos, sin = axial_rope_tables(ROWS)
    Lc = ctx.shape[1]
    silu_c = jax.nn.silu(c)[:, None, :]
    silu_cc = jax.nn.silu(c_ctx)
    xl, xc = x, ctx
    for l in range(DEPTH):
        last = l == DEPTH - 1
        sh1, sc1, g1, sh2, sc2, g2 = jnp.split(silu_c @ w_ada[l] + b_ada[l], 6, axis=-1)
        sh1c, sc1c, g1c, sh2c, sc2c, g2c = jnp.split(silu_cc @ w_ada[l] + b_ada[l], 6, axis=-1)
        hl = modulate(rmsnorm(xl, norm1[l]), sh1, sc1)
        hc = modulate(rmsnorm(xc, norm1[l]), sh1c, sc1c)
        yc, yl = token_mixer(hc, hl, cos, sin, w_in[l], m_conv[l], m_gate_b[l], m_norm[l],
                             a_qnorm[l], a_wuq[l], a_kvnorm[l], a_wukv[l], g_ws[l], g_bs[l], g_vnorm[l],
                             w_pa[l], w_pb[l], w_pc[l], w_out[l], not last)
        xl = xl + g1 * yl
        if last:
            h2 = modulate(rmsnorm(xl, norm2[l]), sh2, sc2)
            xl = xl + g2 * hier_moe(h2, r_group[l], r_group_b[l], r_expert[l], r_expert_b[l],
                                    e_w1[l], e_w3[l], e_w2[l])
        else:
            xc = xc + g1c * yc
            h2 = jnp.concatenate([modulate(rmsnorm(xc, norm2[l]), sh2c, sc2c),
                                  modulate(rmsnorm(xl, norm2[l]), sh2, sc2)], axis=1)
            f = hier_moe(h2, r_group[l], r_group_b[l], r_expert[l], r_expert_b[l], e_w1[l], e_w3[l], e_w2[l])
            xc = xc + g2c * f[:, :Lc]
            xl = xl + g2 * f[:, Lc:]
    return rmsnorm(xl, final_norm)
```

```python
import functools

import jax
import jax.numpy as jnp
from jax import lax
from jax.experimental import pallas as pl
from jax.experimental.pallas import tpu as pltpu

F32 = jnp.float32
BF16 = jnp.bfloat16

EPS = 1e-6
GRID_W = 64
ROPE_THETA = 10000.0

M_HEADS = 4
M_DH = 128
M_WIDTH = M_HEADS * M_DH
M_CHUNK = 128

A_HEADS = 8
A_NOPE = 64
A_ROPE = 32
A_VDIM = 64
A_QRANK = 384
A_KVRANK = 256
A_WIDTH = A_HEADS * A_VDIM
A_PAD = 128
ATT_SCALE = (A_NOPE + A_ROPE) ** -0.5

G_GROUPS = 4
G_CHUNK = 128
G_WIDTH = 512
G_DG = G_WIDTH // G_GROUPS

N_GROUPS = 4
EXP_PER_GROUP = 4
N_EXPERTS = N_GROUPS * EXP_PER_GROUP
D_EXPERT = 512
R_PAD = 128

VMEM_LIMIT = 56 * 1024 * 1024


def _dot(a, b):
    return jnp.dot(a, b, preferred_element_type=F32)


def _dot_nt(a, b):
    return lax.dot_general(a, b, (((1,), (1,)), ((), ())), preferred_element_type=F32)


def _dot_tn(a, b):
    return lax.dot_general(a, b, (((0,), (0,)), ((), ())), preferred_element_type=F32)


def _split3(x):
    hi = x.astype(BF16)
    r = x - hi.astype(F32)
    mid = r.astype(BF16)
    lo = (r - mid.astype(F32)).astype(BF16)
    return hi, mid, lo


def _sigmoid(x):
    return 1.0 / (1.0 + jnp.exp(-x))


def _silu(x):
    return x * _sigmoid(x)


def _log_sigmoid(x):
    return jnp.minimum(x, 0.0) - jnp.log1p(jnp.exp(-jnp.abs(x)))


def _gelu(x):
    return 0.5 * x * (1.0 + lax.erf(x * (2.0 ** -0.5)))


def _rms(x, g):
    return x * lax.rsqrt(jnp.mean(x * x, axis=-1, keepdims=True) + EPS) * g


def _params(*sem):
    return pltpu.CompilerParams(dimension_semantics=sem, vmem_limit_bytes=VMEM_LIMIT)


def _const_spec(shape):
    nd = len(shape)
    return pl.BlockSpec(shape, lambda *_: (0,) * nd, pipeline_mode=pl.Buffered(1))


def _ada_kernel(cv_ref, w_ref, b_ref, o_ref):
    s = _silu(cv_ref[...])
    o_ref[0] = _dot(s.astype(BF16), w_ref[0].astype(BF16)) + b_ref[0]


def _ada_call(cv, w_ada, b_ada):
    depth, d, n6 = w_ada.shape
    rows = cv.shape[0]
    tn = n6 // 4
    return pl.pallas_call(
        _ada_kernel,
        grid=(depth, n6 // tn),
        in_specs=[pl.BlockSpec((rows, d), lambda l, j: (0, 0)),
                  pl.BlockSpec((1, d, tn), lambda l, j: (l, 0, j)),
                  pl.BlockSpec((1, 1, tn), lambda l, j: (l, 0, j))],
        out_specs=pl.BlockSpec((1, rows, tn), lambda l, j: (l, 0, j)),
        out_shape=jax.ShapeDtypeStruct((depth, rows, n6), F32),
        compiler_params=_params("parallel", "parallel"),
        name="ada",
    )(cv, w_ada, b_ada.reshape(depth, 1, n6))


def _inproj_kernel(x_ref, mod_ref, n1_ref, wqk_ref, wvo_ref, wgc_ref, wgr_ref, gbc_ref, gbr_ref,
                   wa_ref, wg_ref, wbr_ref, aqn_ref, akvn_ref, wuq_ref, wuk_ref, wuv_ref,
                   cos_ref, sina_ref, sinb_ref, gvn_ref, gws_ref, gbs_ref,
                   qk_o, vo_o, gc_o, gr_o, q_o, k_o, v_o, yc_o, br_o):
    tm = x_ref.shape[1]
    mod = mod_ref[0]
    h = _rms(x_ref[0], n1_ref[...]) * (1.0 + mod[1:2]) + mod[0:1]
    hb = h.astype(BF16)

    qk_o[0] = _dot(hb, wqk_ref[...])
    vo_o[0] = _dot(hb, wvo_ref[...]).astype(BF16)
    gc_o[0] = _dot(hb, wgc_ref[...]) + gbc_ref[...]
    gr_o[0] = _dot_nt(wgr_ref[...], hb) + gbr_ref[...]

    za = _dot(hb, wa_ref[...])
    aqn = _rms(za[:, :A_QRANK], aqn_ref[...]).astype(BF16)
    akvn = _rms(za[:, A_QRANK:A_QRANK + A_KVRANK], akvn_ref[...]).astype(BF16)
    cos = cos_ref[...]
    sina = sina_ref[...]
    sinb = sinb_ref[...]
    half = A_ROPE // 2

    def rope(t):
        return t * cos + pltpu.roll(t, half, 1) * sina + pltpu.roll(t, A_PAD - half, 1) * sinb

    kr = rope(za[:, A_QRANK + A_KVRANK:])
    qp = _dot(aqn, wuq_ref[...])
    kp = _dot(akvn, wuk_ref[...])
    for hh in range(A_HEADS):
        sl = slice(hh * A_PAD, (hh + 1) * A_PAD)
        q_o[0, :, sl] = (rope(qp[:, sl]) * ATT_SCALE).astype(BF16)
        k_o[0, :, sl] = (kp[:, sl] + kr).astype(BF16)
    v_o[0] = _dot(akvn, wuv_ref[...]).astype(BF16)

    zg = _dot(hb, wg_ref[...])
    gu = _gelu(zg[:, :G_WIDTH])
    gv = _gelu(zg[:, G_WIDTH:])
    gvn = gvn_ref[...]
    bias = gbs_ref[...]
    for g in range(G_GROUPS):
        sl = slice(g * G_DG, (g + 1) * G_DG)
        xn = _rms(gv[:, sl], gvn[:, sl]).astype(BF16)
        ws = gws_ref[g]
        for ci in range(tm // G_CHUNK):
            r = slice(ci * G_CHUNK, (ci + 1) * G_CHUNK)
            sg = _dot(ws, xn[r]) + bias[:, sl]
            yc_o[0, r, sl] = (gu[r, sl] * sg).astype(BF16)

    br_o[0] = _sigmoid(_dot(hb, wbr_ref[...])).astype(BF16)


def _inproj_call(xs, mod, n1, w, tabs, lc, tm):
    b, l, d = xs.shape
    nct = lc // tm
    tok = lambda width: pl.BlockSpec((1, tm, width), lambda bi, j: (bi, j, 0))
    tab = pl.BlockSpec((tm, A_PAD), lambda bi, j: (j, 0))
    consts = [n1, w["wqk"], w["wvo"], w["wgc"], w["wgr"], w["gbc"], w["gbr"], w["wa"], w["wg"], w["wbr"],
              w["aqn"], w["akvn"], w["wuq"], w["wuk"], w["wuv"]]
    consts2 = [w["gvn"], w["gws"], w["gbs"]]
    in_specs = ([tok(d), pl.BlockSpec((1, 6, d), lambda bi, j: (jnp.where(j < nct, b, bi), 0, 0))]
                + [_const_spec(a.shape) for a in consts] + [tab, tab, tab]
                + [_const_spec(a.shape) for a in consts2])
    ng = 4 * M_HEADS
    out_shape = [jax.ShapeDtypeStruct((b, l, 2 * M_WIDTH), F32),
                 jax.ShapeDtypeStruct((b, l, 2 * M_WIDTH), BF16),
                 jax.ShapeDtypeStruct((b, l, ng), F32),
                 jax.ShapeDtypeStruct((b, ng, l), F32),
                 jax.ShapeDtypeStruct((b, l, A_HEADS * A_PAD), BF16),
                 jax.ShapeDtypeStruct((b, l, A_HEADS * A_PAD), BF16),
                 jax.ShapeDtypeStruct((b, l, A_WIDTH), BF16),
                 jax.ShapeDtypeStruct((b, l, G_WIDTH), BF16),
                 jax.ShapeDtypeStruct((b, l, 3 * d), BF16)]
    out_specs = [tok(2 * M_WIDTH), tok(2 * M_WIDTH), tok(ng),
                 pl.BlockSpec((1, ng, tm), lambda bi, j: (bi, 0, j)),
                 tok(A_HEADS * A_PAD), tok(A_HEADS * A_PAD), tok(A_WIDTH), tok(G_WIDTH), tok(3 * d)]
    return pl.pallas_call(
        _inproj_kernel, grid=(b, l // tm), in_specs=in_specs, out_specs=out_specs, out_shape=out_shape,
        compiler_params=_params("parallel", "parallel"), name="inproj",
    )(xs, mod, *consts, *tabs, *consts2)


def _mlstm_kernel(qk_ref, vo_ref, gc_ref, gr_ref, conv_ref, mnorm_ref, ya_ref,
                  q_s, k_s, hf_s, hb_s, c_s, n_s, m_s, *, lc):
    l = qk_ref.shape[1]
    ch = M_CHUNK
    nc = l // ch
    ncc = lc // ch
    w = conv_ref[...]
    row = lax.broadcasted_iota(jnp.int32, (ch, 1), 0)

    def conv_body(j, carry):
        r0 = pl.multiple_of(j * ch, ch)
        cur = qk_ref[0, pl.ds(r0, ch), :]
        prev8 = qk_ref[0, pl.ds(pl.multiple_of(jnp.maximum(r0 - 8, 0), 8), 8), :]
        next8 = qk_ref[0, pl.ds(pl.multiple_of(jnp.minimum(r0 + ch, l - 8), 8), 8), :]
        seg_start = jnp.logical_or(j == 0, j == ncc)
        seg_end = jnp.logical_or(j == ncc - 1, j == nc - 1)
        pe = jnp.where(seg_start, 0.0, prev8[7:8, :])
        ne = jnp.where(seg_end, 0.0, next8[0:1, :])
        xp = jnp.where(row == 0, pe, pltpu.roll(cur, 1, 0))
        xn = jnp.where(row == ch - 1, ne, pltpu.roll(cur, ch - 1, 0))
        y = _silu(xp * w[0:1] + cur * w[1:2] + xn * w[2:3])
        q_s[pl.ds(r0, ch), :] = (y[:, :M_WIDTH] * (M_DH ** -0.5)).astype(BF16)
        k_s[pl.ds(r0, ch), :] = y[:, M_WIDTH:].astype(BF16)
        return carry

    lax.fori_loop(0, nc, conv_body, 0)

    c_s[...] = jnp.zeros_like(c_s)
    n_s[...] = jnp.zeros_like(n_s)
    m_s[...] = jnp.zeros_like(m_s)

    ri = lax.broadcasted_iota(jnp.int32, (ch, ch), 0)
    ci = lax.broadcasted_iota(jnp.int32, (ch, ch), 1)
    lower = ri >= ci
    upper = ri <= ci
    ones_lo = jnp.where(lower, 1.0, 0.0).astype(BF16)
    ones_up = jnp.where(upper, 1.0, 0.0).astype(BF16)

    def one_direction(r0, d, h_s):
        mask = upper if d else lower
        m_col = ones_up if d else ones_lo
        m_row = ones_lo if d else ones_up
        gc = gc_ref[0, pl.ds(r0, ch), :]
        gr = gr_ref[0, :, pl.ds(r0, ch)]
        lfc = _log_sigmoid(gc)
        lfr = _log_sigmoid(gr)
        bcol = sum(_dot(m_col, p) for p in _split3(lfc))
        brow = sum(_dot(p, m_row) for p in _split3(lfr))
        bend = jnp.sum(lfc, axis=0, keepdims=True)
        for hh in range(M_HEADS):
            ii = d * 2 * M_HEADS + hh
            fi = ii + M_HEADS
            sidx = d * M_HEADS + hh
            sl = slice(hh * M_DH, (hh + 1) * M_DH)
            q = q_s[pl.ds(r0, ch), sl]
            k = k_s[pl.ds(r0, ch), sl]
            v = vo_ref[0, pl.ds(r0, ch), sl]
            b_c = bcol[:, fi:fi + 1]
            b_r = brow[fi:fi + 1, :]
            i_c = gc[:, ii:ii + 1]
            i_r = gr[ii:ii + 1, :]
            b_e = bend[:, fi:fi + 1]
            c_st = c_s[sidx]
            n_st = n_s[sidx]
            m_st = m_s[sidx][:, 0:1]

            dmat = jnp.where(mask, b_c - b_r + i_r, -jnp.inf)
            inter = b_c + m_st
            mt = jnp.maximum(inter, jnp.max(dmat, axis=-1, keepdims=True))
            w_intra = jnp.exp(dmat - mt)
            w_inter = jnp.exp(inter - mt)
            qkw = _dot_nt(q, k) * w_intra
            num = _dot(qkw.astype(BF16), v) + w_inter * _dot_nt(q, c_st.astype(BF16))
            qf = q.astype(F32)
            den = jnp.sum(qkw, axis=-1, keepdims=True) + w_inter * jnp.sum(qf * n_st, axis=-1, keepdims=True)
            h_s[pl.ds(r0, ch), sl] = num / jnp.maximum(jnp.abs(den), jnp.exp(-mt))

            d_end = b_e - b_c + i_c
            m_new = jnp.maximum(b_e + m_st, jnp.max(d_end, axis=0, keepdims=True))
            w_s = jnp.exp(d_end - m_new)
            w_c = jnp.exp(b_e + m_st - m_new)
            vw = (v.astype(F32) * w_s).astype(BF16)
            c_s[sidx] = w_c * c_st + _dot_tn(vw, k)
            n_s[sidx] = w_c * n_st + jnp.sum(k.astype(F32) * w_s, axis=0, keepdims=True)
            m_s[sidx] = jnp.broadcast_to(m_new, (1, M_DH))

    def scan_body(s, carry):
        one_direction(pl.multiple_of(s * ch, ch), 0, hf_s)
        jb = jnp.where(s < ncc, ncc - 1 - s, nc - 1 - s + ncc)
        one_direction(pl.multiple_of(jb * ch, ch), 1, hb_s)
        return carry

    lax.fori_loop(0, nc, scan_body, 0)

    mnorm = mnorm_ref[...]

    def out_body(j, carry):
        r0 = pl.multiple_of(j * ch, ch)
        hsum = hf_s[pl.ds(r0, ch), :] + hb_s[pl.ds(r0, ch), :]
        og = _sigmoid(vo_ref[0, pl.ds(r0, ch), M_WIDTH:].astype(F32))
        for hh in range(M_HEADS):
            sl = slice(hh * M_DH, (hh + 1) * M_DH)
            ya_ref[0, pl.ds(r0, ch), sl] = (_rms(hsum[:, sl], mnorm[:, sl]) * og[:, sl]).astype(BF16)
        return carry

    lax.fori_loop(0, nc, out_body, 0)


def _mlstm_call(qk, vo, gc, gr, conv, mnorm, lc):
    b, l, _ = qk.shape
    ng = 4 * M_HEADS
    return pl.pallas_call(
        functools.partial(_mlstm_kernel, lc=lc),
        grid=(b,),
        in_specs=[pl.BlockSpec((1, l, 2 * M_WIDTH), lambda bi: (bi, 0, 0), pipeline_mode=pl.Buffered(1)),
                  pl.BlockSpec((1, l, 2 * M_WIDTH), lambda bi: (bi, 0, 0)),
                  pl.BlockSpec((1, l, ng), lambda bi: (bi, 0, 0)),
                  pl.BlockSpec((1, ng, l), lambda bi: (bi, 0, 0)),
                  _const_spec(conv.shape), _const_spec(mnorm.shape)],
        out_specs=pl.BlockSpec((1, l, M_WIDTH), lambda bi: (bi, 0, 0)),
        out_shape=jax.ShapeDtypeStruct((b, l, M_WIDTH), BF16),
        scratch_shapes=[pltpu.VMEM((l, M_WIDTH), BF16), pltpu.VMEM((l, M_WIDTH), BF16),
                        pltpu.VMEM((l, M_WIDTH), F32), pltpu.VMEM((l, M_WIDTH), F32),
                        pltpu.VMEM((2 * M_HEADS, M_DH, M_DH), F32),
                        pltpu.VMEM((2 * M_HEADS, 1, M_DH), F32),
                        pltpu.VMEM((2 * M_HEADS, 1, M_DH), F32)],
        compiler_params=_params("parallel"), name="mlstm",
    )(qk, vo, gc, gr, conv, mnorm)


def _attn_kernel(q_ref, k_ref, v_ref, o_ref, *, lc, ctx_out):
    tq = q_ref.shape[1]
    l = k_ref.shape[1]
    qi = pl.program_id(2)
    nct = lc // tq
    lane = lax.broadcasted_iota(jnp.int32, (tq, 2 * A_VDIM), 1)

    def run(klen):
        outs = []
        for hh in range(2):
            sl = slice(hh * A_PAD, (hh + 1) * A_PAD)
            s = _dot_nt(q_ref[0, :, sl], k_ref[0, :klen, sl])
            p = jnp.exp(s - jnp.max(s, axis=-1, keepdims=True))
            den = jnp.sum(p, axis=-1, keepdims=True)
            outs.append(_dot(p.astype(BF16), v_ref[0, :klen, :]) / den)
        o_ref[0] = jnp.where(lane < A_VDIM, outs[0], outs[1]).astype(BF16)

    @pl.when(qi >= nct)
    def _():
        run(l)

    @pl.when(qi < nct)
    def _():
        if ctx_out:
            run(lc)
        else:
            o_ref[...] = jnp.zeros_like(o_ref)


def _attn_call(q, k, v, lc, tq, ctx_out):
    b, l, _ = q.shape
    pairs = A_HEADS // 2
    return pl.pallas_call(
        functools.partial(_attn_kernel, lc=lc, ctx_out=ctx_out),
        grid=(b, pairs, l // tq),
        in_specs=[pl.BlockSpec((1, tq, 2 * A_PAD), lambda bi, p, qi: (bi, qi, p)),
                  pl.BlockSpec((1, l, 2 * A_PAD), lambda bi, p, qi: (bi, 0, p)),
                  pl.BlockSpec((1, l, 2 * A_VDIM), lambda bi, p, qi: (bi, 0, p))],
        out_specs=pl.BlockSpec((1, tq, 2 * A_VDIM), lambda bi, p, qi: (bi, qi, p)),
        out_shape=jax.ShapeDtypeStruct((b, l, A_WIDTH), BF16),
        compiler_params=_params("parallel", "parallel", "arbitrary"), name="attn",
    )(q, k, v)


def _merge_kernel(x_ref, mod_ref, ya_ref, yb_ref, yc_ref, br_ref, wpa_ref, wpb_ref, wpc_ref, wout_ref, o_ref):
    d = x_ref.shape[2]
    br = br_ref[0]
    y = (br[:, :d].astype(F32) * _dot(ya_ref[0], wpa_ref[...])
         + br[:, d:2 * d].astype(F32) * _dot(yb_ref[0], wpb_ref[...])
         + br[:, 2 * d:].astype(F32) * _dot(yc_ref[0], wpc_ref[...]))
    out = _dot(y.astype(BF16), wout_ref[...])
    o_ref[0] = x_ref[0] + mod_ref[0][2:3] * out


def _merge_call(xs, mod, ya, yb, yc, br, w, lc, tm):
    b, l, d = xs.shape
    nct = lc // tm
    tok = lambda width: pl.BlockSpec((1, tm, width), lambda bi, j: (bi, j, 0))
    consts = [w["wpa"], w["wpb"], w["wpc"], w["wout"]]
    return pl.pallas_call(
        _merge_kernel, grid=(b, l // tm),
        in_specs=[tok(d), pl.BlockSpec((1, 6, d), lambda bi, j: (jnp.where(j < nct, b, bi), 0, 0)),
                  tok(M_WIDTH), tok(A_WIDTH), tok(G_WIDTH), tok(3 * d)] + [_const_spec(a.shape) for a in consts],
        out_specs=tok(d),
        out_shape=jax.ShapeDtypeStruct((b, l, d), F32),
        compiler_params=_params("parallel", "parallel"), name="merge",
    )(xs, mod, ya, yb, yc, br, *consts)


def _router_kernel(x_ref, mod_ref, n2_ref, r_ref, rb_ref, h2_o, comb_o):
    tm = x_ref.shape[1]
    mod = mod_ref[0]
    h2 = _rms(x_ref[0], n2_ref[...]) * (1.0 + mod[4:5]) + mod[3:4]
    h2_o[0] = h2.astype(BF16)

    a = _split3(h2)
    r = [r_ref[i] for i in range(3)]
    logits = (_dot(a[0], r[0]) + (_dot(a[0], r[1]) + _dot(a[1], r[0]))
              + (_dot(a[1], r[1]) + _dot(a[0], r[2]) + _dot(a[2], r[0]))) + rb_ref[...]
    el = logits[:, :N_EXPERTS]
    gl = logits[:, N_EXPERTS:N_EXPERTS + N_GROUPS]
    big = 1e9

    lane_g = lax.broadcasted_iota(jnp.int32, (tm, N_GROUPS), 1).astype(F32)
    gmax = jnp.max(gl, axis=-1, keepdims=True)
    g_sel = jnp.min(jnp.where(gl == gmax, lane_g, big), axis=-1, keepdims=True)
    g_prob = 1.0 / jnp.sum(jnp.exp(gl - gmax), axis=-1, keepdims=True)

    lane_i = lax.broadcasted_iota(jnp.int32, (tm, N_EXPERTS), 1)
    lane_e = lane_i.astype(F32)
    lane_grp = (lane_i // EXP_PER_GROUP).astype(F32)
    v1 = jnp.where(lane_grp == g_sel, el, -jnp.inf)
    t1 = jnp.max(v1, axis=-1, keepdims=True)
    i1 = jnp.min(jnp.where(v1 == t1, lane_e, big), axis=-1, keepdims=True)
    v2 = jnp.where(lane_e == i1, -jnp.inf, v1)
    t2 = jnp.max(v2, axis=-1, keepdims=True)
    i2 = jnp.min(jnp.where(v2 == t2, lane_e, big), axis=-1, keepdims=True)
    e21 = jnp.exp(t2 - t1)
    w1 = 1.0 / (1.0 + e21)
    w2 = e21 * w1
    comb_o[0] = (jnp.where(lane_e == i1, w1, 0.0) + jnp.where(lane_e == i2, w2, 0.0)) * g_prob


def _router_call(xs, mod, n2, r3, rb, lc, tm):
    b, l, d = xs.shape
    nct = lc // tm
    tok = lambda width: pl.BlockSpec((1, tm, width), lambda bi, j: (bi, j, 0))
    return pl.pallas_call(
        _router_kernel, grid=(b, l // tm),
        in_specs=[tok(d), pl.BlockSpec((1, 6, d), lambda bi, j: (jnp.where(j < nct, b, bi), 0, 0)),
                  _const_spec(n2.shape), _const_spec(r3.shape), _const_spec(rb.shape)],
        out_specs=[tok(d), tok(N_EXPERTS)],
        out_shape=[jax.ShapeDtypeStruct((b, l, d), BF16), jax.ShapeDtypeStruct((b, l, N_EXPERTS), F32)],
        compiler_params=_params("parallel", "parallel"), name="router",
    )(xs, mod, n2, r3, rb)


def _moe_kernel(x_ref, modb_ref, modc_ref, h2_ref, comb_ref, w1_ref, w3_ref, w2_ref, o_ref, acc_ref, *, lc):
    tm = x_ref.shape[1]
    j = pl.program_id(1)
    e = pl.program_id(2)

    @pl.when(e == 0)
    def _():
        acc_ref[...] = jnp.zeros_like(acc_ref)

    h2 = h2_ref[0]
    hid = (_silu(_dot(h2, w1_ref[0])) * _dot(h2, w3_ref[0])).astype(BF16)
    lane = lax.broadcasted_iota(jnp.int32, (tm, N_EXPERTS), 1)
    ce = jnp.sum(jnp.where(lane == e, comb_ref[0], 0.0), axis=-1, keepdims=True)
    acc_ref[...] += ce * _dot(hid, w2_ref[0])

    @pl.when(e == N_EXPERTS - 1)
    def _():
        row = lax.broadcasted_iota(jnp.int32, (tm, 1), 0) + j * tm
        g2 = jnp.where(row < lc, modc_ref[0][5:6], modb_ref[0][5:6])
        o_ref[0] = x_ref[0] + g2 * acc_ref[...]


def _moe_call(xs, mod, h2, comb, w1, w3, w2, lc, tm):
    b, l, d = xs.shape
    tok = lambda width: pl.BlockSpec((1, tm, width), lambda bi, j, e: (bi, j, 0))
    return pl.pallas_call(
        functools.partial(_moe_kernel, lc=lc), grid=(b, l // tm, N_EXPERTS),
        in_specs=[tok(d), pl.BlockSpec((1, 6, d), lambda bi, j, e: (bi, 0, 0)),
                  pl.BlockSpec((1, 6, d), lambda bi, j, e: (b, 0, 0)),
                  tok(d), tok(N_EXPERTS),
                  pl.BlockSpec((1, d, D_EXPERT), lambda bi, j, e: (e, 0, 0)),
                  pl.BlockSpec((1, d, D_EXPERT), lambda bi, j, e: (e, 0, 0)),
                  pl.BlockSpec((1, D_EXPERT, d), lambda bi, j, e: (e, 0, 0))],
        out_specs=tok(d),
        out_shape=jax.ShapeDtypeStruct((b, l, d), F32),
        scratch_shapes=[pltpu.VMEM((tm, d), F32)],
        compiler_params=_params("parallel", "parallel", "arbitrary"), name="moe",
    )(xs, mod, mod, h2, comb, w1, w3, w2)


def _final_kernel(x_ref, g_ref, o_ref):
    o_ref[0] = _rms(x_ref[0], g_ref[...])


def _final_call(xs, g, lc, tm):
    b, l, d = xs.shape
    off = lc // tm
    return pl.pallas_call(
        _final_kernel, grid=(b, (l - lc) // tm),
        in_specs=[pl.BlockSpec((1, tm, d), lambda bi, j: (bi, j + off, 0)), _const_spec(g.shape)],
        out_specs=pl.BlockSpec((1, tm, d), lambda bi, j: (bi, j, 0)),
        out_shape=jax.ShapeDtypeStruct((b, l - lc, d), F32),
        compiler_params=_params("parallel", "parallel"), name="final_norm",
    )(xs, g)


def _rope_tables(t_len, lc):
    half = A_ROPE // 2
    rows = t_len // GRID_W
    r = jnp.repeat(jnp.arange(rows, dtype=F32), GRID_W)
    col = jnp.tile(jnp.arange(GRID_W, dtype=F32), rows)
    inv = ROPE_THETA ** (-jnp.arange(0, half, 2, dtype=F32) / half)
    ang = jnp.concatenate([r[:, None] * inv, col[:, None] * inv], axis=-1)
    cos = jnp.concatenate([jnp.ones((lc, half), F32), jnp.cos(ang)], axis=0)
    sin = jnp.concatenate([jnp.zeros((lc, half), F32), jnp.sin(ang)], axis=0)
    l = lc + t_len
    ones = jnp.ones((l, A_NOPE), F32)
    zeros = jnp.zeros((l, A_NOPE), F32)
    tail1 = jnp.ones((l, A_PAD - A_NOPE - A_ROPE), F32)
    tail0 = jnp.zeros((l, A_PAD - A_NOPE - A_ROPE), F32)
    zh = jnp.zeros((l, half), F32)
    cos_t = jnp.concatenate([ones, cos, cos, tail1], axis=-1)
    sina_t = jnp.concatenate([zeros, zh, sin, tail0], axis=-1)
    sinb_t = jnp.concatenate([zeros, -sin, zh, tail0], axis=-1)
    return cos_t, sina_t, sinb_t


def _layer_weights(l, w_in, m_gate_b, a_qnorm, a_wuq, a_kvnorm, a_wukv, g_ws, g_bs, g_vnorm,
                   w_pa, w_pb, w_pc, w_out):
    d = w_in.shape[1]
    wi = w_in[l]
    o = 0

    def take(n):
        nonlocal o
        s = wi[:, o:o + n]
        o += n
        return s

    mq, mk, mv, mo, mg = take(M_WIDTH), take(M_WIDTH), take(M_WIDTH), take(M_WIDTH), take(4 * M_HEADS)
    aq, akv, akr = take(A_QRANK), take(A_KVRANK), take(A_ROPE)
    gu, gv = take(G_WIDTH), take(G_WIDTH)
    br = take(3 * d)
    akr_pad = jnp.concatenate([jnp.zeros((d, A_NOPE), F32), akr,
                               jnp.zeros((d, A_PAD - A_NOPE - A_ROPE), F32)], axis=1)
    wuq = a_wuq[l].reshape(A_QRANK, A_HEADS, A_NOPE + A_ROPE)
    wuq = jnp.pad(wuq, ((0, 0), (0, 0), (0, A_PAD - A_NOPE - A_ROPE))).reshape(A_QRANK, A_HEADS * A_PAD)
    wukv = a_wukv[l].reshape(A_KVRANK, A_HEADS, A_NOPE + A_VDIM)
    wuk = jnp.pad(wukv[:, :, :A_NOPE], ((0, 0), (0, 0), (0, A_PAD - A_NOPE))).reshape(A_KVRANK, A_HEADS * A_PAD)
    wuv = wukv[:, :, A_NOPE:].reshape(A_KVRANK, A_WIDTH)
    gbs = jnp.repeat(g_bs[l].T, G_DG, axis=1)
    return dict(
        wqk=jnp.concatenate([mq, mk], 1).astype(BF16), wvo=jnp.concatenate([mv, mo], 1).astype(BF16),
        wgc=mg.astype(BF16), wgr=mg.T.astype(BF16),
        gbc=m_gate_b[l].reshape(1, -1), gbr=m_gate_b[l].reshape(-1, 1),
        wa=jnp.concatenate([aq, akv, akr_pad], 1).astype(BF16),
        wg=jnp.concatenate([gu, gv], 1).astype(BF16), wbr=br.astype(BF16),
        aqn=a_qnorm[l].reshape(1, -1), akvn=a_kvnorm[l].reshape(1, -1),
        wuq=wuq.astype(BF16), wuk=wuk.astype(BF16), wuv=wuv.astype(BF16),
        gvn=g_vnorm[l].reshape(1, -1), gws=g_ws[l].astype(BF16), gbs=gbs,
        wpa=w_pa[l].astype(BF16), wpb=w_pb[l].astype(BF16), wpc=w_pc[l].astype(BF16),
        wout=w_out[l].astype(BF16))


def _router_weights(r_group, r_group_b, r_expert, r_expert_b):
    d = r_group.shape[0]
    pad = R_PAD - N_EXPERTS - N_GROUPS
    r = jnp.concatenate([r_expert, r_group, jnp.zeros((d, pad), F32)], axis=1)
    rb = jnp.concatenate([r_expert_b, r_group_b, jnp.zeros((pad,), F32)]).reshape(1, R_PAD)
    return jnp.stack(_split3(r)), rb


def _tile(n, lc, candidates):
    for t in candidates:
        if n % t == 0 and lc % t == 0:
            return t
    raise ValueError("sequence lengths must be multiples of 128")


def kernel(x, c, ctx, c_ctx, w_ada, b_ada, norm1, norm2, final_norm, w_in, m_conv, m_gate_b, m_norm, a_qnorm, a_wuq, a_kvnorm, a_wukv, g_ws, g_bs, g_vnorm, w_pa, w_pb, w_pc, w_out, r_group, r_group_b, r_expert, r_expert_b, e_w1, e_w3, e_w2):
    b, t_len, d = x.shape
    lc = ctx.shape[1]
    l = lc + t_len
    depth = w_in.shape[0]
    tm = _tile(l, lc, (256, 128))
    tmoe = next(t for t in (768, 512, 256, 128) if l % t == 0)

    xs = jnp.concatenate([ctx, x], axis=1)
    cv = jnp.concatenate([c, c_ctx[None, :]], axis=0)
    mod_all = _ada_call(cv, w_ada, b_ada).reshape(depth, b + 1, 6, d)
    tabs = _rope_tables(t_len, lc)

    for li in range(depth):
        last = li == depth - 1
        mod = mod_all[li]
        w = _layer_weights(li, w_in, m_gate_b, a_qnorm, a_wuq, a_kvnorm, a_wukv, g_ws, g_bs, g_vnorm,
                           w_pa, w_pb, w_pc, w_out)
        qk, vo, gc, gr, q, k, v, yc, br = _inproj_call(xs, mod, norm1[li].reshape(1, d), w, tabs, lc, tm)
        ya = _mlstm_call(qk, vo, gc, gr, m_conv[li], m_norm[li].reshape(1, -1), lc)
        yb = _attn_call(q, k, v, lc, tm, not last)
        xs = _merge_call(xs, mod, ya, yb, yc, br, w, lc, tm)
        r3, rb = _router_weights(r_group[li], r_group_b[li], r_expert[li], r_expert_b[li])
        h2, comb = _router_call(xs, mod, norm2[li].reshape(1, d), r3, rb, lc, tm)
        xs = _moe_call(xs, mod, h2, comb, e_w1[li].astype(BF16), e_w3[li].astype(BF16),
                       e_w2[li].astype(BF16), lc, tmoe)
    return _final_call(xs, final_norm.reshape(1, d), lc, tm)
```

```python
import functools

import jax
import jax.numpy as jnp
from jax import lax
from jax.experimental import pallas as pl
from jax.experimental.pallas import tpu as pltpu

F32 = jnp.float32
BF16 = jnp.bfloat16

EPS = 1e-6
GRID_W = 64
ROPE_THETA = 10000.0

M_HEADS = 4
M_DH = 128
M_WIDTH = M_HEADS * M_DH
M_CHUNK = 128

A_HEADS = 8
A_NOPE = 64
A_ROPE = 32
A_VDIM = 64
A_QRANK = 384
A_KVRANK = 256
A_WIDTH = A_HEADS * A_VDIM
A_PAD = 128
ATT_SCALE = (A_NOPE + A_ROPE) ** -0.5

G_GROUPS = 4
G_CHUNK = 128
G_WIDTH = 512
G_DG = G_WIDTH // G_GROUPS

N_GROUPS = 4
EXP_PER_GROUP = 4
N_EXPERTS = N_GROUPS * EXP_PER_GROUP
D_EXPERT = 512
R_PAD = 128
MOE_CHUNK = 256

VMEM_LIMIT = 56 * 1024 * 1024


def _dot(a, b):
    return jnp.dot(a, b, preferred_element_type=F32)


def _dot_nt(a, b):
    return lax.dot_general(a, b, (((1,), (1,)), ((), ())), preferred_element_type=F32)


def _dot_tn(a, b):
    return lax.dot_general(a, b, (((0,), (0,)), ((), ())), preferred_element_type=F32)


def _split3(x):
    hi = x.astype(BF16)
    r = x - hi.astype(F32)
    mid = r.astype(BF16)
    lo = (r - mid.astype(F32)).astype(BF16)
    return hi, mid, lo


def _sigmoid(x):
    return 1.0 / (1.0 + jnp.exp(-x))


def _silu(x):
    return x * _sigmoid(x)


def _log_sigmoid(x):
    return jnp.minimum(x, 0.0) - jnp.log1p(jnp.exp(-jnp.abs(x)))


def _gelu(x):
    return 0.5 * x * (1.0 + lax.erf(x * (2.0 ** -0.5)))


def _rms(x, g):
    return x * lax.rsqrt(jnp.mean(x * x, axis=-1, keepdims=True) + EPS) * g


def _params(*sem):
    return pltpu.CompilerParams(dimension_semantics=sem, vmem_limit_bytes=VMEM_LIMIT)


def _const_spec(shape):
    nd = len(shape)
    return pl.BlockSpec(shape, lambda *_: (0,) * nd, pipeline_mode=pl.Buffered(1))


def _ada_kernel(cv_ref, w_ref, b_ref, o_ref):
    s = _silu(cv_ref[...])
    o_ref[0] = _dot(s.astype(BF16), w_ref[0].astype(BF16)) + b_ref[0]


def _ada_call(cv, w_ada, b_ada):
    depth, d, n6 = w_ada.shape
    rows = cv.shape[0]
    tn = n6 // 4
    return pl.pallas_call(
        _ada_kernel,
        grid=(depth, n6 // tn),
        in_specs=[pl.BlockSpec((rows, d), lambda l, j: (0, 0)),
                  pl.BlockSpec((1, d, tn), lambda l, j: (l, 0, j)),
                  pl.BlockSpec((1, 1, tn), lambda l, j: (l, 0, j))],
        out_specs=pl.BlockSpec((1, rows, tn), lambda l, j: (l, 0, j)),
        out_shape=jax.ShapeDtypeStruct((depth, rows, n6), F32),
        compiler_params=_params("parallel", "parallel"),
        name="ada",
    )(cv, w_ada, b_ada.reshape(depth, 1, n6))


def _inproj_kernel(x_ref, mod_ref, n1_ref, wqk_ref, wvo_ref, wgc_ref, wgr_ref, gbc_ref, gbr_ref,
                   wa_ref, wg_ref, wbr_ref, aqn_ref, akvn_ref, wuq_ref, wuk_ref, wuv_ref,
                   cos_ref, sina_ref, sinb_ref, gvn_ref, gws_ref, gbs_ref,
                   qk_o, vo_o, gc_o, gr_o, q_o, k_o, v_o, yc_o, br_o):
    tm = x_ref.shape[1]
    mod = mod_ref[0]
    h = _rms(x_ref[0], n1_ref[...]) * (1.0 + mod[1:2]) + mod[0:1]
    hb = h.astype(BF16)

    qk_o[0] = _dot(hb, wqk_ref[...])
    vo_o[0] = _dot(hb, wvo_ref[...]).astype(BF16)
    gc_o[0] = _dot(hb, wgc_ref[...]) + gbc_ref[...]
    gr_o[0] = _dot_nt(wgr_ref[...], hb) + gbr_ref[...]

    za = _dot(hb, wa_ref[...])
    aqn = _rms(za[:, :A_QRANK], aqn_ref[...]).astype(BF16)
    akvn = _rms(za[:, A_QRANK:A_QRANK + A_KVRANK], akvn_ref[...]).astype(BF16)
    cos = cos_ref[...]
    sina = sina_ref[...]
    sinb = sinb_ref[...]
    half = A_ROPE // 2

    def rope(t):
        return t * cos + pltpu.roll(t, half, 1) * sina + pltpu.roll(t, A_PAD - half, 1) * sinb

    kr = rope(za[:, A_QRANK + A_KVRANK:])
    qp = _dot(aqn, wuq_ref[...])
    kp = _dot(akvn, wuk_ref[...])
    for hh in range(A_HEADS):
        sl = slice(hh * A_PAD, (hh + 1) * A_PAD)
        q_o[0, :, sl] = (rope(qp[:, sl]) * ATT_SCALE).astype(BF16)
        k_o[0, :, sl] = (kp[:, sl] + kr).astype(BF16)
    v_o[0] = _dot(akvn, wuv_ref[...]).astype(BF16)

    zg = _dot(hb, wg_ref[...])
    gu = _gelu(zg[:, :G_WIDTH])
    gv = _gelu(zg[:, G_WIDTH:])
    gvn = gvn_ref[...]
    bias = gbs_ref[...]
    for g in range(G_GROUPS):
        sl = slice(g * G_DG, (g + 1) * G_DG)
        xn = _rms(gv[:, sl], gvn[:, sl]).astype(BF16)
        ws = gws_ref[g]
        for ci in range(tm // G_CHUNK):
            r = slice(ci * G_CHUNK, (ci + 1) * G_CHUNK)
            sg = _dot(ws, xn[r]) + bias[:, sl]
            yc_o[0, r, sl] = (gu[r, sl] * sg).astype(BF16)

    br_o[0] = _sigmoid(_dot(hb, wbr_ref[...])).astype(BF16)


def _inproj_call(xs, mod, n1, w, tabs, lc, tm):
    b, l, d = xs.shape
    nct = lc // tm
    tok = lambda width: pl.BlockSpec((1, tm, width), lambda bi, j: (bi, j, 0))
    tab = pl.BlockSpec((tm, A_PAD), lambda bi, j: (j, 0))
    consts = [n1, w["wqk"], w["wvo"], w["wgc"], w["wgr"], w["gbc"], w["gbr"], w["wa"], w["wg"], w["wbr"],
              w["aqn"], w["akvn"], w["wuq"], w["wuk"], w["wuv"]]
    consts2 = [w["gvn"], w["gws"], w["gbs"]]
    in_specs = ([tok(d), pl.BlockSpec((1, 6, d), lambda bi, j: (jnp.where(j < nct, b, bi), 0, 0))]
                + [_const_spec(a.shape) for a in consts] + [tab, tab, tab]
                + [_const_spec(a.shape) for a in consts2])
    ng = 4 * M_HEADS
    out_shape = [jax.ShapeDtypeStruct((b, l, 2 * M_WIDTH), F32),
                 jax.ShapeDtypeStruct((b, l, 2 * M_WIDTH), BF16),
                 jax.ShapeDtypeStruct((b, l, ng), F32),
                 jax.ShapeDtypeStruct((b, ng, l), F32),
                 jax.ShapeDtypeStruct((b, l, A_HEADS * A_PAD), BF16),
                 jax.ShapeDtypeStruct((b, l, A_HEADS * A_PAD), BF16),
                 jax.ShapeDtypeStruct((b, l, A_WIDTH), BF16),
                 jax.ShapeDtypeStruct((b, l, G_WIDTH), BF16),
                 jax.ShapeDtypeStruct((b, l, 3 * d), BF16)]
    out_specs = [tok(2 * M_WIDTH), tok(2 * M_WIDTH), tok(ng),
                 pl.BlockSpec((1, ng, tm), lambda bi, j: (bi, 0, j)),
                 tok(A_HEADS * A_PAD), tok(A_HEADS * A_PAD), tok(A_WIDTH), tok(G_WIDTH), tok(3 * d)]
    return pl.pallas_call(
        _inproj_kernel, grid=(b, l // tm), in_specs=in_specs, out_specs=out_specs, out_shape=out_shape,
        compiler_params=_params("parallel", "parallel"), name="inproj",
    )(xs, mod, *consts, *tabs, *consts2)


def _mlstm_kernel(qk_ref, vo_ref, gc_ref, gr_ref, conv_ref, mnorm_ref, ya_ref,
                  q_s, k_s, hf_s, hb_s, c_s, n_s, m_s, *, lc):
    l = qk_ref.shape[1]
    ch = M_CHUNK
    nc = l // ch
    ncc = lc // ch
    w = conv_ref[...]
    row = lax.broadcasted_iota(jnp.int32, (ch, 1), 0)

    def conv_body(j, carry):
        r0 = pl.multiple_of(j * ch, ch)
        cur = qk_ref[0, pl.ds(r0, ch), :]
        prev8 = qk_ref[0, pl.ds(pl.multiple_of(jnp.maximum(r0 - 8, 0), 8), 8), :]
        next8 = qk_ref[0, pl.ds(pl.multiple_of(jnp.minimum(r0 + ch, l - 8), 8), 8), :]
        seg_start = jnp.logical_or(j == 0, j == ncc)
        seg_end = jnp.logical_or(j == ncc - 1, j == nc - 1)
        pe = jnp.where(seg_start, 0.0, prev8[7:8, :])
        ne = jnp.where(seg_end, 0.0, next8[0:1, :])
        xp = jnp.where(row == 0, pe, pltpu.roll(cur, 1, 0))
        xn = jnp.where(row == ch - 1, ne, pltpu.roll(cur, ch - 1, 0))
        y = _silu(xp * w[0:1] + cur * w[1:2] + xn * w[2:3])
        q_s[pl.ds(r0, ch), :] = (y[:, :M_WIDTH] * (M_DH ** -0.5)).astype(BF16)
        k_s[pl.ds(r0, ch), :] = y[:, M_WIDTH:].astype(BF16)
        return carry

    lax.fori_loop(0, nc, conv_body, 0)

    c_s[...] = jnp.zeros_like(c_s)
    n_s[...] = jnp.zeros_like(n_s)
    m_s[...] = jnp.zeros_like(m_s)

    ri = lax.broadcasted_iota(jnp.int32, (ch, ch), 0)
    ci = lax.broadcasted_iota(jnp.int32, (ch, ch), 1)
    lower = ri >= ci
    upper = ri <= ci
    ones_lo = jnp.where(lower, 1.0, 0.0).astype(BF16)
    ones_up = jnp.where(upper, 1.0, 0.0).astype(BF16)

    def one_direction(r0, d, h_s):
        mask = upper if d else lower
        m_col = ones_up if d else ones_lo
        m_row = ones_lo if d else ones_up
        gc = gc_ref[0, pl.ds(r0, ch), :]
        gr = gr_ref[0, :, pl.ds(r0, ch)]
        lfc = _log_sigmoid(gc)
        lfr = _log_sigmoid(gr)
        bcol = sum(_dot(m_col, p) for p in _split3(lfc))
        brow = sum(_dot(p, m_row) for p in _split3(lfr))
        bend = jnp.sum(lfc, axis=0, keepdims=True)
        for hh in range(M_HEADS):
            ii = d * 2 * M_HEADS + hh
            fi = ii + M_HEADS
            sidx = d * M_HEADS + hh
            sl = slice(hh * M_DH, (hh + 1) * M_DH)
            q = q_s[pl.ds(r0, ch), sl]
            k = k_s[pl.ds(r0, ch), sl]
            v = vo_ref[0, pl.ds(r0, ch), sl]
            b_c = bcol[:, fi:fi + 1]
            b_r = brow[fi:fi + 1, :]
            i_c = gc[:, ii:ii + 1]
            i_r = gr[ii:ii + 1, :]
            b_e = bend[:, fi:fi + 1]
            c_st = c_s[sidx]
            n_st = n_s[sidx]
            m_st = m_s[sidx][:, 0:1]

            dmat = jnp.where(mask, b_c - b_r + i_r, -jnp.inf)
            inter = b_c + m_st
            mt = jnp.maximum(inter, jnp.max(dmat, axis=-1, keepdims=True))
            w_intra = jnp.exp(dmat - mt)
            w_inter = jnp.exp(inter - mt)
            qkw = _dot_nt(q, k) * w_intra
            num = _dot(qkw.astype(BF16), v) + w_inter * _dot_nt(q, c_st.astype(BF16))
            qf = q.astype(F32)
            den = jnp.sum(qkw, axis=-1, keepdims=True) + w_inter * jnp.sum(qf * n_st, axis=-1, keepdims=True)
            h_s[pl.ds(r0, ch), sl] = num / jnp.maximum(jnp.abs(den), jnp.exp(-mt))

            d_end = b_e - b_c + i_c
            m_new = jnp.maximum(b_e + m_st, jnp.max(d_end, axis=0, keepdims=True))
            w_s = jnp.exp(d_end - m_new)
            w_c = jnp.exp(b_e + m_st - m_new)
            vw = (v.astype(F32) * w_s).astype(BF16)
            c_s[sidx] = w_c * c_st + _dot_tn(vw, k)
            n_s[sidx] = w_c * n_st + jnp.sum(k.astype(F32) * w_s, axis=0, keepdims=True)
            m_s[sidx] = jnp.broadcast_to(m_new, (1, M_DH))

    def scan_body(s, carry):
        one_direction(pl.multiple_of(s * ch, ch), 0, hf_s)
        jb = jnp.where(s < ncc, ncc - 1 - s, nc - 1 - s + ncc)
        one_direction(pl.multiple_of(jb * ch, ch), 1, hb_s)
        return carry

    lax.fori_loop(0, nc, scan_body, 0)

    mnorm = mnorm_ref[...]

    def out_body(j, carry):
        r0 = pl.multiple_of(j * ch, ch)
        hsum = hf_s[pl.ds(r0, ch), :] + hb_s[pl.ds(r0, ch), :]
        og = _sigmoid(vo_ref[0, pl.ds(r0, ch), M_WIDTH:].astype(F32))
        for hh in range(M_HEADS):
            sl = slice(hh * M_DH, (hh + 1) * M_DH)
            ya_ref[0, pl.ds(r0, ch), sl] = (_rms(hsum[:, sl], mnorm[:, sl]) * og[:, sl]).astype(BF16)
        return carry

    lax.fori_loop(0, nc, out_body, 0)


def _mlstm_call(qk, vo, gc, gr, conv, mnorm, lc):
    b, l, _ = qk.shape
    ng = 4 * M_HEADS
    return pl.pallas_call(
        functools.partial(_mlstm_kernel, lc=lc),
        grid=(b,),
        in_specs=[pl.BlockSpec((1, l, 2 * M_WIDTH), lambda bi: (bi, 0, 0), pipeline_mode=pl.Buffered(1)),
                  pl.BlockSpec((1, l, 2 * M_WIDTH), lambda bi: (bi, 0, 0)),
                  pl.BlockSpec((1, l, ng), lambda bi: (bi, 0, 0)),
                  pl.BlockSpec((1, ng, l), lambda bi: (bi, 0, 0)),
                  _const_spec(conv.shape), _const_spec(mnorm.shape)],
        out_specs=pl.BlockSpec((1, l, M_WIDTH), lambda bi: (bi, 0, 0)),
        out_shape=jax.ShapeDtypeStruct((b, l, M_WIDTH), BF16),
        scratch_shapes=[pltpu.VMEM((l, M_WIDTH), BF16), pltpu.VMEM((l, M_WIDTH), BF16),
                        pltpu.VMEM((l, M_WIDTH), F32), pltpu.VMEM((l, M_WIDTH), F32),
                        pltpu.VMEM((2 * M_HEADS, M_DH, M_DH), F32),
                        pltpu.VMEM((2 * M_HEADS, 1, M_DH), F32),
                        pltpu.VMEM((2 * M_HEADS, 1, M_DH), F32)],
        compiler_params=_params("parallel"), name="mlstm",
    )(qk, vo, gc, gr, conv, mnorm)


def _attn_kernel(q_ref, k_ref, v_ref, o_ref, *, lc, ctx_out):
    tq = q_ref.shape[1]
    l = k_ref.shape[1]
    qi = pl.program_id(2)
    nct = lc // tq
    lane = lax.broadcasted_iota(jnp.int32, (tq, 2 * A_VDIM), 1)

    def run(klen):
        outs = []
        for hh in range(2):
            sl = slice(hh * A_PAD, (hh + 1) * A_PAD)
            s = _dot_nt(q_ref[0, :, sl], k_ref[0, :klen, sl])
            p = jnp.exp(s - jnp.max(s, axis=-1, keepdims=True))
            den = jnp.sum(p, axis=-1, keepdims=True)
            outs.append(_dot(p.astype(BF16), v_ref[0, :klen, :]) / den)
        o_ref[0] = jnp.where(lane < A_VDIM, outs[0], outs[1]).astype(BF16)

    @pl.when(qi >= nct)
    def _():
        run(l)

    @pl.when(qi < nct)
    def _():
        if ctx_out:
            run(lc)
        else:
            o_ref[...] = jnp.zeros_like(o_ref)


def _attn_call(q, k, v, lc, tq, ctx_out):
    b, l, _ = q.shape
    pairs = A_HEADS // 2
    return pl.pallas_call(
        functools.partial(_attn_kernel, lc=lc, ctx_out=ctx_out),
        grid=(b, pairs, l // tq),
        in_specs=[pl.BlockSpec((1, tq, 2 * A_PAD), lambda bi, p, qi: (bi, qi, p)),
                  pl.BlockSpec((1, l, 2 * A_PAD), lambda bi, p, qi: (bi, 0, p)),
                  pl.BlockSpec((1, l, 2 * A_VDIM), lambda bi, p, qi: (bi, 0, p))],
        out_specs=pl.BlockSpec((1, tq, 2 * A_VDIM), lambda bi, p, qi: (bi, qi, p)),
        out_shape=jax.ShapeDtypeStruct((b, l, A_WIDTH), BF16),
        compiler_params=_params("parallel", "parallel", "arbitrary"), name="attn",
    )(q, k, v)


def _merge_kernel(x_ref, mod_ref, ya_ref, yb_ref, yc_ref, br_ref, wpa_ref, wpb_ref, wpc_ref, wout_ref, o_ref):
    d = x_ref.shape[2]
    br = br_ref[0]
    y = (br[:, :d].astype(F32) * _dot(ya_ref[0], wpa_ref[...])
         + br[:, d:2 * d].astype(F32) * _dot(yb_ref[0], wpb_ref[...])
         + br[:, 2 * d:].astype(F32) * _dot(yc_ref[0], wpc_ref[...]))
    out = _dot(y.astype(BF16), wout_ref[...])
    o_ref[0] = x_ref[0] + mod_ref[0][2:3] * out


def _merge_call(xs, mod, ya, yb, yc, br, w, lc, tm):
    b, l, d = xs.shape
    nct = lc // tm
    tok = lambda width: pl.BlockSpec((1, tm, width), lambda bi, j: (bi, j, 0))
    consts = [w["wpa"], w["wpb"], w["wpc"], w["wout"]]
    return pl.pallas_call(
        _merge_kernel, grid=(b, l // tm),
        in_specs=[tok(d), pl.BlockSpec((1, 6, d), lambda bi, j: (jnp.where(j < nct, b, bi), 0, 0)),
                  tok(M_WIDTH), tok(A_WIDTH), tok(G_WIDTH), tok(3 * d)] + [_const_spec(a.shape) for a in consts],
        out_specs=tok(d),
        out_shape=jax.ShapeDtypeStruct((b, l, d), F32),
        compiler_params=_params("parallel", "parallel"), name="merge",
    )(xs, mod, ya, yb, yc, br, *consts)


def _router_kernel(x_ref, mod_ref, n2_ref, r_ref, rb_ref, h2_o, comb_o, idx_o, idxt_o, cnt_o):
    tm = x_ref.shape[1]
    mod = mod_ref[0]
    h2 = _rms(x_ref[0], n2_ref[...]) * (1.0 + mod[4:5]) + mod[3:4]
    h2_o[0] = h2.astype(BF16)

    a = _split3(h2)
    r = [r_ref[i] for i in range(3)]
    logits = (_dot(a[0], r[0]) + (_dot(a[0], r[1]) + _dot(a[1], r[0]))
              + (_dot(a[1], r[1]) + _dot(a[0], r[2]) + _dot(a[2], r[0]))) + rb_ref[...]
    el = logits[:, :N_EXPERTS]
    gl = logits[:, N_EXPERTS:N_EXPERTS + N_GROUPS]
    big = 1e9

    lane_g = lax.broadcasted_iota(jnp.int32, (tm, N_GROUPS), 1).astype(F32)
    gmax = jnp.max(gl, axis=-1, keepdims=True)
    g_sel = jnp.min(jnp.where(gl == gmax, lane_g, big), axis=-1, keepdims=True)
    g_prob = 1.0 / jnp.sum(jnp.exp(gl - gmax), axis=-1, keepdims=True)

    lane_i = lax.broadcasted_iota(jnp.int32, (tm, N_EXPERTS), 1)
    lane_e = lane_i.astype(F32)
    lane_grp = (lane_i // EXP_PER_GROUP).astype(F32)
    v1 = jnp.where(lane_grp == g_sel, el, -jnp.inf)
    t1 = jnp.max(v1, axis=-1, keepdims=True)
    i1 = jnp.min(jnp.where(v1 == t1, lane_e, big), axis=-1, keepdims=True)
    v2 = jnp.where(lane_e == i1, -jnp.inf, v1)
    t2 = jnp.max(v2, axis=-1, keepdims=True)
    i2 = jnp.min(jnp.where(v2 == t2, lane_e, big), axis=-1, keepdims=True)
    e21 = jnp.exp(t2 - t1)
    w1 = 1.0 / (1.0 + e21)
    w2 = e21 * w1
    comb_o[0] = (jnp.where(lane_e == i1, w1, 0.0) + jnp.where(lane_e == i2, w2, 0.0)) * g_prob

    lane_p = lax.broadcasted_iota(jnp.int32, (tm, R_PAD), 1)
    onehot = jnp.where(lane_p.astype(F32) == g_sel, 1.0, 0.0)
    ri = lax.broadcasted_iota(jnp.int32, (tm, tm), 0)
    ci = lax.broadcasted_iota(jnp.int32, (tm, tm), 1)
    before = jnp.where(ri > ci, 1.0, 0.0).astype(BF16)
    rank = jnp.sum(_dot(before, onehot.astype(BF16)) * onehot, axis=-1, keepdims=True)
    cnt_o[0, 0] = jnp.broadcast_to(jnp.sum(onehot, axis=0, keepdims=True), (8, R_PAD))
    fields = jnp.where(lane_p == 0, g_sel, jnp.where(lane_p == 1, rank, 0.0))
    idx_o[0] = fields[:, :8].astype(jnp.int32)
    idxt_o[0] = fields.T[:8, :].astype(jnp.int32)


def _router_call(xs, mod, n2, r3, rb, lc, tm):
    b, l, d = xs.shape
    nct = lc // tm
    tok = lambda width: pl.BlockSpec((1, tm, width), lambda bi, j: (bi, j, 0))
    return pl.pallas_call(
        _router_kernel, grid=(b, l // tm),
        in_specs=[tok(d), pl.BlockSpec((1, 6, d), lambda bi, j: (jnp.where(j < nct, b, bi), 0, 0)),
                  _const_spec(n2.shape), _const_spec(r3.shape), _const_spec(rb.shape)],
        out_specs=[tok(d), tok(N_EXPERTS), tok(8), pl.BlockSpec((1, 8, tm), lambda bi, j: (bi, 0, j)),
                   pl.BlockSpec((1, 1, 8, R_PAD), lambda bi, j: (bi, j, 0, 0))],
        out_shape=[jax.ShapeDtypeStruct((b, l, d), BF16), jax.ShapeDtypeStruct((b, l, N_EXPERTS), F32),
                   jax.ShapeDtypeStruct((b, l, 8), jnp.int32), jax.ShapeDtypeStruct((b, 8, l), jnp.int32),
                   jax.ShapeDtypeStruct((b, l // tm, 8, R_PAD), F32)],
        compiler_params=_params("parallel", "parallel"), name="router",
    )(xs, mod, n2, r3, rb)


def _experts_kernel(st_ref, h2_ref, comb_ref, idx_ref, idxt_ref, w1_ref, w3_ref, w2_ref, o_ref, hs_s, cs_s, *, tb):
    l, d = h2_ref.shape[1], h2_ref.shape[2]
    nblk = l // tb
    ch = MOE_CHUNK
    bi = pl.program_id(0)
    g = pl.program_id(1)

    @pl.when(g == 0)
    def _():
        o_ref[...] = jnp.zeros_like(o_ref)

    base = (bi * N_GROUPS + g) * (nblk + 1)
    cnt = st_ref[base + nblk]
    sub_iota = lax.broadcasted_iota(jnp.int32, (ch, tb), 0)
    lane_iota = lax.broadcasted_iota(jnp.int32, (tb, ch), 1)
    lane_e = lax.broadcasted_iota(jnp.int32, (ch, N_EXPERTS), 1)

    def chunk_body(c, carry):
        lo = c * ch
        hs_s[...] = jnp.zeros_like(hs_s)
        cs_s[...] = jnp.zeros_like(cs_s)
        for k in range(nblk):
            s_k = st_ref[base + k]
            e_k = st_ref[base + k + 1]
            rows = slice(k * tb, (k + 1) * tb)

            @pl.when(jnp.logical_and(s_k < lo + ch, e_k > lo))
            def _(s_k=s_k, rows=rows):
                it = idxt_ref[0, :, rows]
                pos = jnp.where(it[0:1] == g, it[1:2] + (s_k - lo), -1)
                p = jnp.where(sub_iota == pos, 1.0, 0.0).astype(BF16)
                hs_s[...] += _dot(p, h2_ref[0, rows, :])
                cs_s[...] += sum(_dot(p, piece) for piece in _split3(comb_ref[0, rows, :]))

        hsb = hs_s[...].astype(BF16)
        cs = cs_s[...]
        y = jnp.zeros((ch, d), F32)
        for e in range(EXP_PER_GROUP):
            ce = jnp.sum(jnp.where(lane_e == g * EXP_PER_GROUP + e, cs, 0.0), axis=-1, keepdims=True)
            hid = (_silu(_dot(hsb, w1_ref[e])) * _dot(hsb, w3_ref[e]) * ce).astype(BF16)
            y = y + _dot(hid, w2_ref[e])
        yb = y.astype(BF16)

        for k in range(nblk):
            s_k = st_ref[base + k]
            e_k = st_ref[base + k + 1]
            rows = slice(k * tb, (k + 1) * tb)

            @pl.when(jnp.logical_and(s_k < lo + ch, e_k > lo))
            def _(s_k=s_k, rows=rows):
                ic = idx_ref[0, rows, :]
                pos = jnp.where(ic[:, 0:1] == g, ic[:, 1:2] + (s_k - lo), -1)
                q = jnp.where(lane_iota == pos, 1.0, 0.0).astype(BF16)
                o_ref[0, rows, :] = (o_ref[0, rows, :].astype(F32) + _dot(q, yb)).astype(BF16)

        return carry

    lax.fori_loop(0, (cnt + ch - 1) // ch, chunk_body, 0)


def _experts_call(starts, h2, comb, idx, idxt, w1, w3, w2, tb):
    b, l, d = h2.shape
    whole = lambda width: pl.BlockSpec((1, l, width), lambda bi, g, st: (bi, 0, 0))
    grid_spec = pltpu.PrefetchScalarGridSpec(
        num_scalar_prefetch=1, grid=(b, N_GROUPS),
        in_specs=[whole(d), whole(N_EXPERTS), whole(8), pl.BlockSpec((1, 8, l), lambda bi, g, st: (bi, 0, 0)),
                  pl.BlockSpec((EXP_PER_GROUP, d, D_EXPERT), lambda bi, g, st: (g, 0, 0)),
                  pl.BlockSpec((EXP_PER_GROUP, d, D_EXPERT), lambda bi, g, st: (g, 0, 0)),
                  pl.BlockSpec((EXP_PER_GROUP, D_EXPERT, d), lambda bi, g, st: (g, 0, 0))],
        out_specs=whole(d),
        scratch_shapes=[pltpu.VMEM((MOE_CHUNK, d), F32), pltpu.VMEM((MOE_CHUNK, N_EXPERTS), F32)])
    return pl.pallas_call(
        functools.partial(_experts_kernel, tb=tb), grid_spec=grid_spec,
        out_shape=jax.ShapeDtypeStruct((b, l, d), BF16),
        compiler_params=_params("parallel", "arbitrary"), name="experts",
    )(starts, h2, comb, idx, idxt, w1, w3, w2)


def _group_starts(cnt):
    c = cnt[:, :, 0, :N_GROUPS].astype(jnp.int32)
    s = jnp.cumsum(c, axis=1)
    s = jnp.concatenate([jnp.zeros_like(s[:, :1]), s], axis=1)
    return jnp.transpose(s, (0, 2, 1)).reshape(-1)


def _resid_kernel(x_ref, mod_ref, f_ref, o_ref):
    o_ref[0] = x_ref[0] + mod_ref[0][5:6] * f_ref[0].astype(F32)


def _resid_call(xs, mod, f, lc, tm):
    b, l, d = xs.shape
    nct = lc // tm
    tok = pl.BlockSpec((1, tm, d), lambda bi, j: (bi, j, 0))
    return pl.pallas_call(
        _resid_kernel, grid=(b, l // tm),
        in_specs=[tok, pl.BlockSpec((1, 6, d), lambda bi, j: (jnp.where(j < nct, b, bi), 0, 0)), tok],
        out_specs=tok, out_shape=jax.ShapeDtypeStruct((b, l, d), F32),
        compiler_params=_params("parallel", "parallel"), name="resid",
    )(xs, mod, f)


def _final_kernel(x_ref, mod_ref, f_ref, g_ref, o_ref):
    o_ref[0] = _rms(x_ref[0] + mod_ref[0][5:6] * f_ref[0].astype(F32), g_ref[...])


def _final_call(xs, mod, f, g, lc, tm):
    b, l, d = xs.shape
    off = lc // tm
    lat = pl.BlockSpec((1, tm, d), lambda bi, j: (bi, j + off, 0))
    return pl.pallas_call(
        _final_kernel, grid=(b, (l - lc) // tm),
        in_specs=[lat, pl.BlockSpec((1, 6, d), lambda bi, j: (bi, 0, 0)), lat, _const_spec(g.shape)],
        out_specs=pl.BlockSpec((1, tm, d), lambda bi, j: (bi, j, 0)),
        out_shape=jax.ShapeDtypeStruct((b, l - lc, d), F32),
        compiler_params=_params("parallel", "parallel"), name="final_norm",
    )(xs, mod, f, g)


def _rope_tables(t_len, lc):
    half = A_ROPE // 2
    rows = t_len // GRID_W
    r = jnp.repeat(jnp.arange(rows, dtype=F32), GRID_W)
    col = jnp.tile(jnp.arange(GRID_W, dtype=F32), rows)
    inv = ROPE_THETA ** (-jnp.arange(0, half, 2, dtype=F32) / half)
    ang = jnp.concatenate([r[:, None] * inv, col[:, None] * inv], axis=-1)
    cos = jnp.concatenate([jnp.ones((lc, half), F32), jnp.cos(ang)], axis=0)
    sin = jnp.concatenate([jnp.zeros((lc, half), F32), jnp.sin(ang)], axis=0)
    l = lc + t_len
    ones = jnp.ones((l, A_NOPE), F32)
    zeros = jnp.zeros((l, A_NOPE), F32)
    tail1 = jnp.ones((l, A_PAD - A_NOPE - A_ROPE), F32)
    tail0 = jnp.zeros((l, A_PAD - A_NOPE - A_ROPE), F32)
    zh = jnp.zeros((l, half), F32)
    cos_t = jnp.concatenate([ones, cos, cos, tail1], axis=-1)
    sina_t = jnp.concatenate([zeros, zh, sin, tail0], axis=-1)
    sinb_t = jnp.concatenate([zeros, -sin, zh, tail0], axis=-1)
    return cos_t, sina_t, sinb_t


def _layer_weights(l, w_in, m_gate_b, a_qnorm, a_wuq, a_kvnorm, a_wukv, g_ws, g_bs, g_vnorm,
                   w_pa, w_pb, w_pc, w_out):
    d = w_in.shape[1]
    wi = w_in[l]
    o = 0

    def take(n):
        nonlocal o
        s = wi[:, o:o + n]
        o += n
        return s

    mq, mk, mv, mo, mg = take(M_WIDTH), take(M_WIDTH), take(M_WIDTH), take(M_WIDTH), take(4 * M_HEADS)
    aq, akv, akr = take(A_QRANK), take(A_KVRANK), take(A_ROPE)
    gu, gv = take(G_WIDTH), take(G_WIDTH)
    br = take(3 * d)
    akr_pad = jnp.concatenate([jnp.zeros((d, A_NOPE), F32), akr,
                               jnp.zeros((d, A_PAD - A_NOPE - A_ROPE), F32)], axis=1)
    wuq = a_wuq[l].reshape(A_QRANK, A_HEADS, A_NOPE + A_ROPE)
    wuq = jnp.pad(wuq, ((0, 0), (0, 0), (0, A_PAD - A_NOPE - A_ROPE))).reshape(A_QRANK, A_HEADS * A_PAD)
    wukv = a_wukv[l].reshape(A_KVRANK, A_HEADS, A_NOPE + A_VDIM)
    wuk = jnp.pad(wukv[:, :, :A_NOPE], ((0, 0), (0, 0), (0, A_PAD - A_NOPE))).reshape(A_KVRANK, A_HEADS * A_PAD)
    wuv = wukv[:, :, A_NOPE:].reshape(A_KVRANK, A_WIDTH)
    gbs = jnp.repeat(g_bs[l].T, G_DG, axis=1)
    return dict(
        wqk=jnp.concatenate([mq, mk], 1).astype(BF16), wvo=jnp.concatenate([mv, mo], 1).astype(BF16),
        wgc=mg.astype(BF16), wgr=mg.T.astype(BF16),
        gbc=m_gate_b[l].reshape(1, -1), gbr=m_gate_b[l].reshape(-1, 1),
        wa=jnp.concatenate([aq, akv, akr_pad], 1).astype(BF16),
        wg=jnp.concatenate([gu, gv], 1).astype(BF16), wbr=br.astype(BF16),
        aqn=a_qnorm[l].reshape(1, -1), akvn=a_kvnorm[l].reshape(1, -1),
        wuq=wuq.astype(BF16), wuk=wuk.astype(BF16), wuv=wuv.astype(BF16),
        gvn=g_vnorm[l].reshape(1, -1), gws=g_ws[l].astype(BF16), gbs=gbs,
        wpa=w_pa[l].astype(BF16), wpb=w_pb[l].astype(BF16), wpc=w_pc[l].astype(BF16),
        wout=w_out[l].astype(BF16))


def _router_weights(r_group, r_group_b, r_expert, r_expert_b):
    d = r_group.shape[0]
    pad = R_PAD - N_EXPERTS - N_GROUPS
    r = jnp.concatenate([r_expert, r_group, jnp.zeros((d, pad), F32)], axis=1)
    rb = jnp.concatenate([r_expert_b, r_group_b, jnp.zeros((pad,), F32)]).reshape(1, R_PAD)
    return jnp.stack(_split3(r)), rb


def _tile(n, lc, candidates):
    for t in candidates:
        if n % t == 0 and lc % t == 0:
            return t
    raise ValueError("sequence lengths must be multiples of 128")


def kernel(x, c, ctx, c_ctx, w_ada, b_ada, norm1, norm2, final_norm, w_in, m_conv, m_gate_b, m_norm, a_qnorm, a_wuq, a_kvnorm, a_wukv, g_ws, g_bs, g_vnorm, w_pa, w_pb, w_pc, w_out, r_group, r_group_b, r_expert, r_expert_b, e_w1, e_w3, e_w2):
    b, t_len, d = x.shape
    lc = ctx.shape[1]
    l = lc + t_len
    depth = w_in.shape[0]
    tm = _tile(l, lc, (256, 128))

    xs = jnp.concatenate([ctx, x], axis=1)
    cv = jnp.concatenate([c, c_ctx[None, :]], axis=0)
    mod_all = _ada_call(cv, w_ada, b_ada).reshape(depth, b + 1, 6, d)
    tabs = _rope_tables(t_len, lc)

    for li in range(depth):
        last = li == depth - 1
        mod = mod_all[li]
        w = _layer_weights(li, w_in, m_gate_b, a_qnorm, a_wuq, a_kvnorm, a_wukv, g_ws, g_bs, g_vnorm,
                           w_pa, w_pb, w_pc, w_out)
        qk, vo, gc, gr, q, k, v, yc, br = _inproj_call(xs, mod, norm1[li].reshape(1, d), w, tabs, lc, tm)
        ya = _mlstm_call(qk, vo, gc, gr, m_conv[li], m_norm[li].reshape(1, -1), lc)
        yb = _attn_call(q, k, v, lc, tm, not last)
        xs = _merge_call(xs, mod, ya, yb, yc, br, w, lc, tm)
        r3, rb = _router_weights(r_group[li], r_group_b[li], r_expert[li], r_expert_b[li])
        h2, comb, idx, idxt, cnt = _router_call(xs, mod, norm2[li].reshape(1, d), r3, rb, lc, tm)
        f = _experts_call(_group_starts(cnt), h2, comb, idx, idxt, e_w1[li].astype(BF16),
                          e_w3[li].astype(BF16), e_w2[li].astype(BF16), tm)
        if last:
            return _final_call(xs, mod, f, final_norm.reshape(1, d), lc, tm)
        xs = _resid_call(xs, mod, f, lc, tm)
```

```python
import functools

import jax
import jax.numpy as jnp
from jax import lax
from jax.experimental import pallas as pl
from jax.experimental.pallas import tpu as pltpu

F32 = jnp.float32
BF16 = jnp.bfloat16

EPS = 1e-6
GRID_W = 64
ROPE_THETA = 10000.0

M_HEADS = 4
M_DH = 128
M_WIDTH = M_HEADS * M_DH
M_CHUNK = 128

A_HEADS = 8
A_NOPE = 64
A_ROPE = 32
A_VDIM = 64
A_QRANK = 384
A_KVRANK = 256
A_WIDTH = A_HEADS * A_VDIM
A_PAD = 128
ATT_SCALE = (A_NOPE + A_ROPE) ** -0.5
LOG2E = 1.4426950408889634

G_GROUPS = 4
G_CHUNK = 128
G_WIDTH = 512
G_DG = G_WIDTH // G_GROUPS

N_GROUPS = 4
EXP_PER_GROUP = 4
N_EXPERTS = N_GROUPS * EXP_PER_GROUP
D_EXPERT = 512
R_PAD = 128
MOE_CHUNK = 256

VMEM_LIMIT = 56 * 1024 * 1024


def _dot(a, b):
    return jnp.dot(a, b, preferred_element_type=F32)


def _dot_nt(a, b):
    return lax.dot_general(a, b, (((1,), (1,)), ((), ())), preferred_element_type=F32)


def _dot_tn(a, b):
    return lax.dot_general(a, b, (((0,), (0,)), ((), ())), preferred_element_type=F32)


def _split3(x):
    hi = x.astype(BF16)
    r = x - hi.astype(F32)
    mid = r.astype(BF16)
    lo = (r - mid.astype(F32)).astype(BF16)
    return hi, mid, lo


def _sigmoid(x):
    return 1.0 / (1.0 + jnp.exp(-x))


def _silu(x):
    return x * _sigmoid(x)


def _log_sigmoid(x):
    return jnp.minimum(x, 0.0) - jnp.log1p(jnp.exp(-jnp.abs(x)))


def _gelu(x):
    return 0.5 * x * (1.0 + lax.erf(x * (2.0 ** -0.5)))


def _rms(x, g):
    return x * lax.rsqrt(jnp.mean(x * x, axis=-1, keepdims=True) + EPS) * g


def _params(*sem):
    return pltpu.CompilerParams(dimension_semantics=sem, vmem_limit_bytes=VMEM_LIMIT)


def _const_spec(shape):
    nd = len(shape)
    return pl.BlockSpec(shape, lambda *_: (0,) * nd, pipeline_mode=pl.Buffered(1))


def _ada_kernel(cv_ref, w_ref, b_ref, o_ref):
    s = _silu(cv_ref[...])
    o_ref[0] = _dot(s.astype(BF16), w_ref[0].astype(BF16)) + b_ref[0]


def _ada_call(cv, w_ada, b_ada):
    depth, d, n6 = w_ada.shape
    rows = cv.shape[0]
    tn = n6 // 4
    return pl.pallas_call(
        _ada_kernel,
        grid=(depth, n6 // tn),
        in_specs=[pl.BlockSpec((rows, d), lambda l, j: (0, 0)),
                  pl.BlockSpec((1, d, tn), lambda l, j: (l, 0, j)),
                  pl.BlockSpec((1, 1, tn), lambda l, j: (l, 0, j))],
        out_specs=pl.BlockSpec((1, rows, tn), lambda l, j: (l, 0, j)),
        out_shape=jax.ShapeDtypeStruct((depth, rows, n6), F32),
        compiler_params=_params("parallel", "parallel"),
        name="ada",
    )(cv, w_ada, b_ada.reshape(depth, 1, n6))


def _inproj_kernel(*refs, has_f):
    if has_f:
        f_ref, modp_ref, x_o = refs[0], refs[1], refs[-1]
        refs = refs[2:-1]
    (x_ref, mod_ref, n1_ref, wqk_ref, wvo_ref, wgc_ref, wgr_ref, gbc_ref, gbr_ref,
     wa_ref, wg_ref, wbr_ref, aqn_ref, akvn_ref, wuq_ref, wuk_ref, wuv_ref, vone_ref,
     cos_ref, sina_ref, sinb_ref, gvn_ref, gws_ref, gbs_ref,
     qk_o, vo_o, gc_o, gr_o, q_o, k_o, v_o, yc_o, br_o) = refs
    tm = x_ref.shape[1]
    mod = mod_ref[0]
    x = x_ref[0]
    if has_f:
        x = x + modp_ref[0][5:6] * f_ref[0].astype(F32)
        x_o[0] = x
    h = _rms(x, n1_ref[...]) * (1.0 + mod[1:2]) + mod[0:1]
    hb = h.astype(BF16)

    qk_o[0] = _dot(hb, wqk_ref[...])
    vo_o[0] = _dot(hb, wvo_ref[...]).astype(BF16)
    gc_o[0] = _dot(hb, wgc_ref[...]) + gbc_ref[...]
    gr_o[0] = _dot_nt(wgr_ref[...], hb) + gbr_ref[...]

    za = _dot(hb, wa_ref[...])
    aqn = _rms(za[:, :A_QRANK], aqn_ref[...]).astype(BF16)
    akvn = _rms(za[:, A_QRANK:A_QRANK + A_KVRANK], akvn_ref[...]).astype(BF16)
    cos = cos_ref[...]
    sina = sina_ref[...]
    sinb = sinb_ref[...]
    half = A_ROPE // 2

    def rope(t):
        return t * cos + pltpu.roll(t, half, 1) * sina + pltpu.roll(t, A_PAD - half, 1) * sinb

    kr = rope(za[:, A_QRANK + A_KVRANK:])
    qp = _dot(aqn, wuq_ref[...])
    kp = _dot(akvn, wuk_ref[...])
    for hh in range(A_HEADS):
        sl = slice(hh * A_PAD, (hh + 1) * A_PAD)
        q_o[0, :, sl] = (rope(qp[:, sl]) * (ATT_SCALE * LOG2E)).astype(BF16)
        k_o[0, :, sl] = (kp[:, sl] + kr).astype(BF16)
    v_o[0] = (_dot(akvn, wuv_ref[...]) + vone_ref[...]).astype(BF16)

    zg = _dot(hb, wg_ref[...])
    gu = _gelu(zg[:, :G_WIDTH])
    gv = _gelu(zg[:, G_WIDTH:])
    gvn = gvn_ref[...]
    bias = gbs_ref[...]
    for g in range(G_GROUPS):
        sl = slice(g * G_DG, (g + 1) * G_DG)
        xn = _rms(gv[:, sl], gvn[:, sl]).astype(BF16)
        ws = gws_ref[g]
        for ci in range(tm // G_CHUNK):
            r = slice(ci * G_CHUNK, (ci + 1) * G_CHUNK)
            sg = _dot(ws, xn[r]) + bias[:, sl]
            yc_o[0, r, sl] = (gu[r, sl] * sg).astype(BF16)

    br_o[0] = _sigmoid(_dot(hb, wbr_ref[...])).astype(BF16)


def _inproj_call(xs, mod, n1, w, tabs, lc, tm, f=None, mod_prev=None):
    b, l, d = xs.shape
    nct = lc // tm
    has_f = f is not None
    tok = lambda width: pl.BlockSpec((1, tm, width), lambda bi, j: (bi, j, 0))
    modspec = pl.BlockSpec((1, 6, d), lambda bi, j: (jnp.where(j < nct, b, bi), 0, 0))
    tab = pl.BlockSpec((tm, A_PAD), lambda bi, j: (j, 0))
    consts = [n1, w["wqk"], w["wvo"], w["wgc"], w["wgr"], w["gbc"], w["gbr"], w["wa"], w["wg"], w["wbr"],
              w["aqn"], w["akvn"], w["wuq"], w["wuk"], w["wuv"], w["vone"]]
    consts2 = [w["gvn"], w["gws"], w["gbs"]]
    in_specs = ([tok(d), modspec] + [_const_spec(a.shape) for a in consts] + [tab, tab, tab]
                + [_const_spec(a.shape) for a in consts2])
    args = [xs, mod, *consts, *tabs, *consts2]
    ng = 4 * M_HEADS
    out_shape = [jax.ShapeDtypeStruct((b, l, 2 * M_WIDTH), F32),
                 jax.ShapeDtypeStruct((b, l, 2 * M_WIDTH), BF16),
                 jax.ShapeDtypeStruct((b, l, ng), F32),
                 jax.ShapeDtypeStruct((b, ng, l), F32),
                 jax.ShapeDtypeStruct((b, l, A_HEADS * A_PAD), BF16),
                 jax.ShapeDtypeStruct((b, l, A_HEADS * A_PAD), BF16),
                 jax.ShapeDtypeStruct((b, l, A_HEADS * A_PAD), BF16),
                 jax.ShapeDtypeStruct((b, l, G_WIDTH), BF16),
                 jax.ShapeDtypeStruct((b, l, 3 * d), BF16)]
    out_specs = [tok(2 * M_WIDTH), tok(2 * M_WIDTH), tok(ng),
                 pl.BlockSpec((1, ng, tm), lambda bi, j: (bi, 0, j)),
                 tok(A_HEADS * A_PAD), tok(A_HEADS * A_PAD), tok(A_HEADS * A_PAD), tok(G_WIDTH), tok(3 * d)]
    if has_f:
        in_specs = [tok(d), modspec] + in_specs
        args = [f, mod_prev] + args
        out_shape.append(jax.ShapeDtypeStruct((b, l, d), F32))
        out_specs.append(tok(d))
    return pl.pallas_call(
        functools.partial(_inproj_kernel, has_f=has_f), grid=(b, l // tm), in_specs=in_specs,
        out_specs=out_specs, out_shape=out_shape,
        compiler_params=_params("parallel", "parallel"), name="inproj",
    )(*args)


def _mlstm_kernel(qk_ref, vo_ref, gc_ref, gr_ref, conv_ref, mnorm_ref, ya_ref,
                  q_s, k_s, hf_s, hb_s, c_s, n_s, m_s, *, lc):
    l = qk_ref.shape[1]
    ch = M_CHUNK
    nc = l // ch
    ncc = lc // ch
    w = conv_ref[...]
    row = lax.broadcasted_iota(jnp.int32, (ch, 1), 0)

    def conv_body(j, carry):
        r0 = pl.multiple_of(j * ch, ch)
        cur = qk_ref[0, pl.ds(r0, ch), :]
        prev8 = qk_ref[0, pl.ds(pl.multiple_of(jnp.maximum(r0 - 8, 0), 8), 8), :]
        next8 = qk_ref[0, pl.ds(pl.multiple_of(jnp.minimum(r0 + ch, l - 8), 8), 8), :]
        seg_start = jnp.logical_or(j == 0, j == ncc)
        seg_end = jnp.logical_or(j == ncc - 1, j == nc - 1)
        pe = jnp.where(seg_start, 0.0, prev8[7:8, :])
        ne = jnp.where(seg_end, 0.0, next8[0:1, :])
        xp = jnp.where(row == 0, pe, pltpu.roll(cur, 1, 0))
        xn = jnp.where(row == ch - 1, ne, pltpu.roll(cur, ch - 1, 0))
        y = _silu(xp * w[0:1] + cur * w[1:2] + xn * w[2:3])
        q_s[pl.ds(r0, ch), :] = (y[:, :M_WIDTH] * (M_DH ** -0.5)).astype(BF16)
        k_s[pl.ds(r0, ch), :] = y[:, M_WIDTH:].astype(BF16)
        return carry

    lax.fori_loop(0, nc, conv_body, 0)

    c_s[...] = jnp.zeros_like(c_s)
    n_s[...] = jnp.zeros_like(n_s)
    m_s[...] = jnp.zeros_like(m_s)

    ri = lax.broadcasted_iota(jnp.int32, (ch, ch), 0)
    ci = lax.broadcasted_iota(jnp.int32, (ch, ch), 1)
    lower = ri >= ci
    upper = ri <= ci
    ones_lo = jnp.where(lower, 1.0, 0.0).astype(BF16)
    ones_up = jnp.where(upper, 1.0, 0.0).astype(BF16)

    def one_direction(r0, d, h_s):
        mask = upper if d else lower
        m_col = ones_up if d else ones_lo
        m_row = ones_lo if d else ones_up
        gc = gc_ref[0, pl.ds(r0, ch), :]
        gr = gr_ref[0, :, pl.ds(r0, ch)]
        lfc = _log_sigmoid(gc)
        lfr = _log_sigmoid(gr)
        bcol = sum(_dot(m_col, p) for p in _split3(lfc))
        brow = sum(_dot(p, m_row) for p in _split3(lfr))
        bend = jnp.sum(lfc, axis=0, keepdims=True)
        for hh in range(M_HEADS):
            ii = d * 2 * M_HEADS + hh
            fi = ii + M_HEADS
            sidx = d * M_HEADS + hh
            sl = slice(hh * M_DH, (hh + 1) * M_DH)
            q = q_s[pl.ds(r0, ch), sl]
            k = k_s[pl.ds(r0, ch), sl]
            v = vo_ref[0, pl.ds(r0, ch), sl]
            b_c = bcol[:, fi:fi + 1]
            b_r = brow[fi:fi + 1, :]
            i_c = gc[:, ii:ii + 1]
            i_r = gr[ii:ii + 1, :]
            b_e = bend[:, fi:fi + 1]
            c_st = c_s[sidx]
            n_st = n_s[sidx]
            m_st = m_s[sidx][:, 0:1]

            dmat = jnp.where(mask, b_c - b_r + i_r, -jnp.inf)
            inter = b_c + m_st
            mt = jnp.maximum(inter, jnp.max(dmat, axis=-1, keepdims=True))
            w_intra = jnp.exp(dmat - mt)
            w_inter = jnp.exp(inter - mt)
            qkw = _dot_nt(q, k) * w_intra
            num = _dot(qkw.astype(BF16), v) + w_inter * _dot_nt(q, c_st.astype(BF16))
            qf = q.astype(F32)
            den = jnp.sum(qkw, axis=-1, keepdims=True) + w_inter * jnp.sum(qf * n_st, axis=-1, keepdims=True)
            h_s[pl.ds(r0, ch), sl] = num / jnp.maximum(jnp.abs(den), jnp.exp(-mt))

            d_end = b_e - b_c + i_c
            m_new = jnp.maximum(b_e + m_st, jnp.max(d_end, axis=0, keepdims=True))
            w_s = jnp.exp(d_end - m_new)
            w_c = jnp.exp(b_e + m_st - m_new)
            vw = (v.astype(F32) * w_s).astype(BF16)
            c_s[sidx] = w_c * c_st + _dot_tn(vw, k)
            n_s[sidx] = w_c * n_st + jnp.sum(k.astype(F32) * w_s, axis=0, keepdims=True)
            m_s[sidx] = jnp.broadcast_to(m_new, (1, M_DH))

    def scan_body(s, carry):
        one_direction(pl.multiple_of(s * ch, ch), 0, hf_s)
        jb = jnp.where(s < ncc, ncc - 1 - s, nc - 1 - s + ncc)
        one_direction(pl.multiple_of(jb * ch, ch), 1, hb_s)
        return carry

    lax.fori_loop(0, nc, scan_body, 0)

    mnorm = mnorm_ref[...]

    def out_body(j, carry):
        r0 = pl.multiple_of(j * ch, ch)
        hsum = hf_s[pl.ds(r0, ch), :] + hb_s[pl.ds(r0, ch), :]
        og = _sigmoid(vo_ref[0, pl.ds(r0, ch), M_WIDTH:].astype(F32))
        for hh in range(M_HEADS):
            sl = slice(hh * M_DH, (hh + 1) * M_DH)
            ya_ref[0, pl.ds(r0, ch), sl] = (_rms(hsum[:, sl], mnorm[:, sl]) * og[:, sl]).astype(BF16)
        return carry

    lax.fori_loop(0, nc, out_body, 0)


def _mlstm_call(qk, vo, gc, gr, conv, mnorm, lc):
    b, l, _ = qk.shape
    ng = 4 * M_HEADS
    return pl.pallas_call(
        functools.partial(_mlstm_kernel, lc=lc),
        grid=(b,),
        in_specs=[pl.BlockSpec((1, l, 2 * M_WIDTH), lambda bi: (bi, 0, 0), pipeline_mode=pl.Buffered(1)),
                  pl.BlockSpec((1, l, 2 * M_WIDTH), lambda bi: (bi, 0, 0)),
                  pl.BlockSpec((1, l, ng), lambda bi: (bi, 0, 0)),
                  pl.BlockSpec((1, ng, l), lambda bi: (bi, 0, 0)),
                  _const_spec(conv.shape), _const_spec(mnorm.shape)],
        out_specs=pl.BlockSpec((1, l, M_WIDTH), lambda bi: (bi, 0, 0)),
        out_shape=jax.ShapeDtypeStruct((b, l, M_WIDTH), BF16),
        scratch_shapes=[pltpu.VMEM((l, M_WIDTH), BF16), pltpu.VMEM((l, M_WIDTH), BF16),
                        pltpu.VMEM((l, M_WIDTH), F32), pltpu.VMEM((l, M_WIDTH), F32),
                        pltpu.VMEM((2 * M_HEADS, M_DH, M_DH), F32),
                        pltpu.VMEM((2 * M_HEADS, 1, M_DH), F32),
                        pltpu.VMEM((2 * M_HEADS, 1, M_DH), F32)],
        compiler_params=_params("parallel"), name="mlstm",
    )(qk, vo, gc, gr, conv, mnorm)


def _attn_kernel(q_ref, k_ref, v_ref, o_ref, *, lc, ctx_out):
    tq = q_ref.shape[1]
    l = k_ref.shape[1]
    qi = pl.program_id(2)
    nct = lc // tq
    lane = lax.broadcasted_iota(jnp.int32, (tq, 2 * A_VDIM), 1)

    def run(klen):
        outs = []
        for hh in range(2):
            sl = slice(hh * A_PAD, (hh + 1) * A_PAD)
            s = _dot_nt(q_ref[0, :, sl], k_ref[0, :klen, sl])
            p = jnp.exp2((s - jnp.max(s, axis=-1, keepdims=True)).astype(BF16))
            nd = _dot(p, v_ref[0, :klen, sl])
            outs.append(nd / pltpu.roll(nd, A_VDIM, 1))
        o_ref[0] = jnp.where(lane < A_VDIM, outs[0], pltpu.roll(outs[1], A_VDIM, 1)).astype(BF16)

    @pl.when(qi >= nct)
    def _():
        run(l)

    @pl.when(qi < nct)
    def _():
        if ctx_out:
            run(lc)
        else:
            o_ref[...] = jnp.zeros_like(o_ref)


def _attn_call(q, k, v, lc, tq, ctx_out):
    b, l, _ = q.shape
    pairs = A_HEADS // 2
    return pl.pallas_call(
        functools.partial(_attn_kernel, lc=lc, ctx_out=ctx_out),
        grid=(b, pairs, l // tq),
        in_specs=[pl.BlockSpec((1, tq, 2 * A_PAD), lambda bi, p, qi: (bi, qi, p)),
                  pl.BlockSpec((1, l, 2 * A_PAD), lambda bi, p, qi: (bi, 0, p)),
                  pl.BlockSpec((1, l, 2 * A_PAD), lambda bi, p, qi: (bi, 0, p))],
        out_specs=pl.BlockSpec((1, tq, 2 * A_VDIM), lambda bi, p, qi: (bi, qi, p)),
        out_shape=jax.ShapeDtypeStruct((b, l, A_WIDTH), BF16),
        compiler_params=_params("parallel", "parallel", "arbitrary"), name="attn",
    )(q, k, v)


def _merge_kernel(x_ref, mod_ref, ya_ref, yb_ref, yc_ref, br_ref, wpa_ref, wpb_ref, wpc_ref, wout_ref,
                  n2_ref, r_ref, rb_ref, o_ref, h2_o, comb_o, idx_o, idxt_o, cnt_o):
    d = x_ref.shape[2]
    br = br_ref[0]
    mod = mod_ref[0]
    y = (br[:, :d].astype(F32) * _dot(ya_ref[0], wpa_ref[...])
         + br[:, d:2 * d].astype(F32) * _dot(yb_ref[0], wpb_ref[...])
         + br[:, 2 * d:].astype(F32) * _dot(yc_ref[0], wpc_ref[...]))
    out = _dot(y.astype(BF16), wout_ref[...])
    x = x_ref[0] + mod[2:3] * out
    o_ref[0] = x
    _route(x, mod, n2_ref, r_ref, rb_ref, h2_o, comb_o, idx_o, idxt_o, cnt_o)


def _merge_call(xs, mod, ya, yb, yc, br, w, n2, r3, rb, lc, tm):
    b, l, d = xs.shape
    nct = lc // tm
    tok = lambda width: pl.BlockSpec((1, tm, width), lambda bi, j: (bi, j, 0))
    consts = [w["wpa"], w["wpb"], w["wpc"], w["wout"], n2, r3, rb]
    return pl.pallas_call(
        _merge_kernel, grid=(b, l // tm),
        in_specs=[tok(d), pl.BlockSpec((1, 6, d), lambda bi, j: (jnp.where(j < nct, b, bi), 0, 0)),
                  tok(M_WIDTH), tok(A_WIDTH), tok(G_WIDTH), tok(3 * d)] + [_const_spec(a.shape) for a in consts],
        out_specs=[tok(d), tok(d), tok(N_EXPERTS), tok(8), pl.BlockSpec((1, 8, tm), lambda bi, j: (bi, 0, j)),
                   pl.BlockSpec((1, 1, 8, R_PAD), lambda bi, j: (bi, j, 0, 0))],
        out_shape=[jax.ShapeDtypeStruct((b, l, d), F32),
                   jax.ShapeDtypeStruct((b, l, d), BF16), jax.ShapeDtypeStruct((b, l, N_EXPERTS), F32),
                   jax.ShapeDtypeStruct((b, l, 8), jnp.int32), jax.ShapeDtypeStruct((b, 8, l), jnp.int32),
                   jax.ShapeDtypeStruct((b, l // tm, 8, R_PAD), F32)],
        compiler_params=_params("parallel", "parallel"), name="merge",
    )(xs, mod, ya, yb, yc, br, *consts)


def _route(x, mod, n2_ref, r_ref, rb_ref, h2_o, comb_o, idx_o, idxt_o, cnt_o):
    tm = x.shape[0]
    h2 = _rms(x, n2_ref[...]) * (1.0 + mod[4:5]) + mod[3:4]
    h2_o[0] = h2.astype(BF16)

    a = _split3(h2)
    r = [r_ref[i] for i in range(3)]
    logits = (_dot(a[0], r[0]) + (_dot(a[0], r[1]) + _dot(a[1], r[0]))
              + (_dot(a[1], r[1]) + _dot(a[0], r[2]) + _dot(a[2], r[0]))) + rb_ref[...]
    el = logits[:, :N_EXPERTS]
    gl = logits[:, N_EXPERTS:N_EXPERTS + N_GROUPS]
    big = 1e9

    lane_g = lax.broadcasted_iota(jnp.int32, (tm, N_GROUPS), 1).astype(F32)
    gmax = jnp.max(gl, axis=-1, keepdims=True)
    g_sel = jnp.min(jnp.where(gl == gmax, lane_g, big), axis=-1, keepdims=True)
    g_prob = 1.0 / jnp.sum(jnp.exp(gl - gmax), axis=-1, keepdims=True)

    lane_i = lax.broadcasted_iota(jnp.int32, (tm, N_EXPERTS), 1)
    lane_e = lane_i.astype(F32)
    lane_grp = (lane_i // EXP_PER_GROUP).astype(F32)
    v1 = jnp.where(lane_grp == g_sel, el, -jnp.inf)
    t1 = jnp.max(v1, axis=-1, keepdims=True)
    i1 = jnp.min(jnp.where(v1 == t1, lane_e, big), axis=-1, keepdims=True)
    v2 = jnp.where(lane_e == i1, -jnp.inf, v1)
    t2 = jnp.max(v2, axis=-1, keepdims=True)
    i2 = jnp.min(jnp.where(v2 == t2, lane_e, big), axis=-1, keepdims=True)
    e21 = jnp.exp(t2 - t1)
    w1 = 1.0 / (1.0 + e21)
    w2 = e21 * w1
    comb_o[0] = (jnp.where(lane_e == i1, w1, 0.0) + jnp.where(lane_e == i2, w2, 0.0)) * g_prob

    lane_p = lax.broadcasted_iota(jnp.int32, (tm, R_PAD), 1)
    onehot = jnp.where(lane_p.astype(F32) == g_sel, 1.0, 0.0)
    ri = lax.broadcasted_iota(jnp.int32, (tm, tm), 0)
    ci = lax.broadcasted_iota(jnp.int32, (tm, tm), 1)
    before = jnp.where(ri > ci, 1.0, 0.0).astype(BF16)
    rank = jnp.sum(_dot(before, onehot.astype(BF16)) * onehot, axis=-1, keepdims=True)
    cnt_o[0, 0] = jnp.broadcast_to(jnp.sum(onehot, axis=0, keepdims=True), (8, R_PAD))
    fields = jnp.where(lane_p == 0, g_sel, jnp.where(lane_p == 1, rank, 0.0))
    idx_o[0] = fields[:, :8].astype(jnp.int32)
    idxt_o[0] = fields.T[:8, :].astype(jnp.int32)


def _experts_kernel(st_ref, h2_ref, comb_ref, idx_ref, idxt_ref, w1_ref, w3_ref, w2_ref, o_ref, hs_s, cs_s, *, tb):
    l, d = h2_ref.shape[1], h2_ref.shape[2]
    nblk = l // tb
    ch = MOE_CHUNK
    bi = pl.program_id(0)
    g = pl.program_id(1)

    @pl.when(g == 0)
    def _():
        o_ref[...] = jnp.zeros_like(o_ref)

    base = (bi * N_GROUPS + g) * (nblk + 1)
    cnt = st_ref[base + nblk]
    sub_iota = lax.broadcasted_iota(jnp.int32, (ch, tb), 0)
    lane_iota = lax.broadcasted_iota(jnp.int32, (tb, ch), 1)
    lane_e = lax.broadcasted_iota(jnp.int32, (ch, N_EXPERTS), 1)

    def chunk_body(c, carry):
        lo = c * ch
        hs_s[...] = jnp.zeros_like(hs_s)
        cs_s[...] = jnp.zeros_like(cs_s)
        for k in range(nblk):
            s_k = st_ref[base + k]
            e_k = st_ref[base + k + 1]
            rows = slice(k * tb, (k + 1) * tb)

            @pl.when(jnp.logical_and(s_k < lo + ch, e_k > lo))
            def _(s_k=s_k, rows=rows):
                it = idxt_ref[0, :, rows]
                pos = jnp.where(it[0:1] == g, it[1:2] + (s_k - lo), -1)
                p = jnp.where(sub_iota == pos, 1.0, 0.0).astype(BF16)
                hs_s[...] += _dot(p, h2_ref[0, rows, :])
                cs_s[...] += sum(_dot(p, piece) for piece in _split3(comb_ref[0, rows, :]))

        hsb = hs_s[...].astype(BF16)
        cs = cs_s[...]
        y = jnp.zeros((ch, d), F32)
        for e in range(EXP_PER_GROUP):
            ce = jnp.sum(jnp.where(lane_e == g * EXP_PER_GROUP + e, cs, 0.0), axis=-1, keepdims=True)
            hid = (_silu(_dot(hsb, w1_ref[e])) * _dot(hsb, w3_ref[e]) * ce).astype(BF16)
            y = y + _dot(hid, w2_ref[e])
        yb = y.astype(BF16)

        for k in range(nblk):
            s_k = st_ref[base + k]
            e_k = st_ref[base + k + 1]
            rows = slice(k * tb, (k + 1) * tb)

            @pl.when(jnp.logical_and(s_k < lo + ch, e_k > lo))
            def _(s_k=s_k, rows=rows):
                ic = idx_ref[0, rows, :]
                pos = jnp.where(ic[:, 0:1] == g, ic[:, 1:2] + (s_k - lo), -1)
                q = jnp.where(lane_iota == pos, 1.0, 0.0).astype(BF16)
                o_ref[0, rows, :] = (o_ref[0, rows, :].astype(F32) + _dot(q, yb)).astype(BF16)

        return carry

    lax.fori_loop(0, (cnt + ch - 1) // ch, chunk_body, 0)


def _experts_call(starts, h2, comb, idx, idxt, w1, w3, w2, tb):
    b, l, d = h2.shape
    whole = lambda width: pl.BlockSpec((1, l, width), lambda bi, g, st: (bi, 0, 0))
    grid_spec = pltpu.PrefetchScalarGridSpec(
        num_scalar_prefetch=1, grid=(b, N_GROUPS),
        in_specs=[whole(d), whole(N_EXPERTS), whole(8), pl.BlockSpec((1, 8, l), lambda bi, g, st: (bi, 0, 0)),
                  pl.BlockSpec((EXP_PER_GROUP, d, D_EXPERT), lambda bi, g, st: (g, 0, 0)),
                  pl.BlockSpec((EXP_PER_GROUP, d, D_EXPERT), lambda bi, g, st: (g, 0, 0)),
                  pl.BlockSpec((EXP_PER_GROUP, D_EXPERT, d), lambda bi, g, st: (g, 0, 0))],
        out_specs=whole(d),
        scratch_shapes=[pltpu.VMEM((MOE_CHUNK, d), F32), pltpu.VMEM((MOE_CHUNK, N_EXPERTS), F32)])
    return pl.pallas_call(
        functools.partial(_experts_kernel, tb=tb), grid_spec=grid_spec,
        out_shape=jax.ShapeDtypeStruct((b, l, d), BF16),
        compiler_params=_params("parallel", "arbitrary"), name="experts",
    )(starts, h2, comb, idx, idxt, w1, w3, w2)


def _group_starts(cnt):
    c = cnt[:, :, 0, :N_GROUPS].astype(jnp.int32)
    s = jnp.cumsum(c, axis=1)
    s = jnp.concatenate([jnp.zeros_like(s[:, :1]), s], axis=1)
    return jnp.transpose(s, (0, 2, 1)).reshape(-1)


def _final_kernel(x_ref, mod_ref, f_ref, g_ref, o_ref):
    o_ref[0] = _rms(x_ref[0] + mod_ref[0][5:6] * f_ref[0].astype(F32), g_ref[...])


def _final_call(xs, mod, f, g, lc, tm):
    b, l, d = xs.shape
    off = lc // tm
    lat = pl.BlockSpec((1, tm, d), lambda bi, j: (bi, j + off, 0))
    return pl.pallas_call(
        _final_kernel, grid=(b, (l - lc) // tm),
        in_specs=[lat, pl.BlockSpec((1, 6, d), lambda bi, j: (bi, 0, 0)), lat, _const_spec(g.shape)],
        out_specs=pl.BlockSpec((1, tm, d), lambda bi, j: (bi, j, 0)),
        out_shape=jax.ShapeDtypeStruct((b, l - lc, d), F32),
        compiler_params=_params("parallel", "parallel"), name="final_norm",
    )(xs, mod, f, g)


def _rope_tables(t_len, lc):
    half = A_ROPE // 2
    rows = t_len // GRID_W
    r = jnp.repeat(jnp.arange(rows, dtype=F32), GRID_W)
    col = jnp.tile(jnp.arange(GRID_W, dtype=F32), rows)
    inv = ROPE_THETA ** (-jnp.arange(0, half, 2, dtype=F32) / half)
    ang = jnp.concatenate([r[:, None] * inv, col[:, None] * inv], axis=-1)
    cos = jnp.concatenate([jnp.ones((lc, half), F32), jnp.cos(ang)], axis=0)
    sin = jnp.concatenate([jnp.zeros((lc, half), F32), jnp.sin(ang)], axis=0)
    l = lc + t_len
    ones = jnp.ones((l, A_NOPE), F32)
    zeros = jnp.zeros((l, A_NOPE), F32)
    tail1 = jnp.ones((l, A_PAD - A_NOPE - A_ROPE), F32)
    tail0 = jnp.zeros((l, A_PAD - A_NOPE - A_ROPE), F32)
    zh = jnp.zeros((l, half), F32)
    cos_t = jnp.concatenate([ones, cos, cos, tail1], axis=-1)
    sina_t = jnp.concatenate([zeros, zh, sin, tail0], axis=-1)
    sinb_t = jnp.concatenate([zeros, -sin, zh, tail0], axis=-1)
    return cos_t, sina_t, sinb_t


def _layer_weights(l, w_in, m_gate_b, a_qnorm, a_wuq, a_kvnorm, a_wukv, g_ws, g_bs, g_vnorm,
                   w_pa, w_pb, w_pc, w_out):
    d = w_in.shape[1]
    wi = w_in[l]
    o = 0

    def take(n):
        nonlocal o
        s = wi[:, o:o + n]
        o += n
        return s

    mq, mk, mv, mo, mg = take(M_WIDTH), take(M_WIDTH), take(M_WIDTH), take(M_WIDTH), take(4 * M_HEADS)
    aq, akv, akr = take(A_QRANK), take(A_KVRANK), take(A_ROPE)
    gu, gv = take(G_WIDTH), take(G_WIDTH)
    br = take(3 * d)
    akr_pad = jnp.concatenate([jnp.zeros((d, A_NOPE), F32), akr,
                               jnp.zeros((d, A_PAD - A_NOPE - A_ROPE), F32)], axis=1)
    wuq = a_wuq[l].reshape(A_QRANK, A_HEADS, A_NOPE + A_ROPE)
    wuq = jnp.pad(wuq, ((0, 0), (0, 0), (0, A_PAD - A_NOPE - A_ROPE))).reshape(A_QRANK, A_HEADS * A_PAD)
    wukv = a_wukv[l].reshape(A_KVRANK, A_HEADS, A_NOPE + A_VDIM)
    wuk = jnp.pad(wukv[:, :, :A_NOPE], ((0, 0), (0, 0), (0, A_PAD - A_NOPE))).reshape(A_KVRANK, A_HEADS * A_PAD)
    wuv = jnp.pad(wukv[:, :, A_NOPE:], ((0, 0), (0, 0), (0, A_PAD - A_VDIM))).reshape(A_KVRANK, A_HEADS * A_PAD)
    vone = jnp.tile(jnp.concatenate([jnp.zeros((A_VDIM,), F32), jnp.ones((A_PAD - A_VDIM,), F32)]),
                    A_HEADS).reshape(1, A_HEADS * A_PAD)
    gbs = jnp.repeat(g_bs[l].T, G_DG, axis=1)
    return dict(
        wqk=jnp.concatenate([mq, mk], 1).astype(BF16), wvo=jnp.concatenate([mv, mo], 1).astype(BF16),
        wgc=mg.astype(BF16), wgr=mg.T.astype(BF16),
        gbc=m_gate_b[l].reshape(1, -1), gbr=m_gate_b[l].reshape(-1, 1),
        wa=jnp.concatenate([aq, akv, akr_pad], 1).astype(BF16),
        wg=jnp.concatenate([gu, gv], 1).astype(BF16), wbr=br.astype(BF16),
        aqn=a_qnorm[l].reshape(1, -1), akvn=a_kvnorm[l].reshape(1, -1),
        wuq=wuq.astype(BF16), wuk=wuk.astype(BF16), wuv=wuv.astype(BF16), vone=vone,
        gvn=g_vnorm[l].reshape(1, -1), gws=g_ws[l].astype(BF16), gbs=gbs,
        wpa=w_pa[l].astype(BF16), wpb=w_pb[l].astype(BF16), wpc=w_pc[l].astype(BF16),
        wout=w_out[l].astype(BF16))


def _router_weights(r_group, r_group_b, r_expert, r_expert_b):
    d = r_group.shape[0]
    pad = R_PAD - N_EXPERTS - N_GROUPS
    r = jnp.concatenate([r_expert, r_group, jnp.zeros((d, pad), F32)], axis=1)
    rb = jnp.concatenate([r_expert_b, r_group_b, jnp.zeros((pad,), F32)]).reshape(1, R_PAD)
    return jnp.stack(_split3(r)), rb


def _tile(n, lc, candidates):
    for t in candidates:
        if n % t == 0 and lc % t == 0:
            return t
    raise ValueError("sequence lengths must be multiples of 128")


def kernel(x, c, ctx, c_ctx, w_ada, b_ada, norm1, norm2, final_norm, w_in, m_conv, m_gate_b, m_norm, a_qnorm, a_wuq, a_kvnorm, a_wukv, g_ws, g_bs, g_vnorm, w_pa, w_pb, w_pc, w_out, r_group, r_group_b, r_expert, r_expert_b, e_w1, e_w3, e_w2):
    b, t_len, d = x.shape
    lc = ctx.shape[1]
    l = lc + t_len
    depth = w_in.shape[0]
    tm = _tile(l, lc, (256, 128))

    xs = jnp.concatenate([ctx, x], axis=1)
    cv = jnp.concatenate([c, c_ctx[None, :]], axis=0)
    mod_all = _ada_call(cv, w_ada, b_ada).reshape(depth, b + 1, 6, d)
    tabs = _rope_tables(t_len, lc)

    f = None
    mod_prev = None
    for li in range(depth):
        last = li == depth - 1
        mod = mod_all[li]
        w = _layer_weights(li, w_in, m_gate_b, a_qnorm, a_wuq, a_kvnorm, a_wukv, g_ws, g_bs, g_vnorm,
                           w_pa, w_pb, w_pc, w_out)
        outs = _inproj_call(xs, mod, norm1[li].reshape(1, d), w, tabs, lc, tm, f, mod_prev)
        qk, vo, gc, gr, q, k, v, yc, br = outs[:9]
        if f is not None:
            xs = outs[9]
        ya = _mlstm_call(qk, vo, gc, gr, m_conv[li], m_norm[li].reshape(1, -1), lc)
        yb = _attn_call(q, k, v, lc, tm, not last)
        r3, rb = _router_weights(r_group[li], r_group_b[li], r_expert[li], r_expert_b[li])
        xs, h2, comb, idx, idxt, cnt = _merge_call(xs, mod, ya, yb, yc, br, w, norm2[li].reshape(1, d),
                                                   r3, rb, lc, tm)
        f = _experts_call(_group_starts(cnt), h2, comb, idx, idxt, e_w1[li].astype(BF16),
                          e_w3[li].astype(BF16), e_w2[li].astype(BF16), tm)
        mod_prev = mod
    return _final_call(xs, mod_prev, f, final_norm.reshape(1, d), lc, tm)
```

```python
import functools

import jax
import jax.numpy as jnp
from jax import lax
from jax.experimental import pallas as pl
from jax.experimental.pallas import tpu as pltpu

F32 = jnp.float32
BF16 = jnp.bfloat16

EPS = 1e-6
GRID_W = 64
ROPE_THETA = 10000.0

M_HEADS = 4
M_DH = 128
M_WIDTH = M_HEADS * M_DH
M_CHUNK = 128

A_HEADS = 8
A_NOPE = 64
A_ROPE = 32
A_VDIM = 64
A_QRANK = 384
A_KVRANK = 256
A_WIDTH = A_HEADS * A_VDIM
A_PAD = 128
ATT_SCALE = (A_NOPE + A_ROPE) ** -0.5
LOG2E = 1.4426950408889634

G_GROUPS = 4
G_CHUNK = 128
G_WIDTH = 512
G_DG = G_WIDTH // G_GROUPS

N_GROUPS = 4
EXP_PER_GROUP = 4
N_EXPERTS = N_GROUPS * EXP_PER_GROUP
D_EXPERT = 512
R_PAD = 128
MOE_CHUNK = 256

VMEM_LIMIT = 56 * 1024 * 1024


def _dot(a, b):
    return jnp.dot(a, b, preferred_element_type=F32)


def _dot_nt(a, b):
    return lax.dot_general(a, b, (((1,), (1,)), ((), ())), preferred_element_type=F32)


def _dot_tn(a, b):
    return lax.dot_general(a, b, (((0,), (0,)), ((), ())), preferred_element_type=F32)


def _split3(x):
    hi = x.astype(BF16)
    r = x - hi.astype(F32)
    mid = r.astype(BF16)
    lo = (r - mid.astype(F32)).astype(BF16)
    return hi, mid, lo


def _sigmoid(x):
    return 1.0 / (1.0 + jnp.exp(-x))


def _silu(x):
    return x * _sigmoid(x)


def _log_sigmoid(x):
    return jnp.minimum(x, 0.0) - jnp.log1p(jnp.exp(-jnp.abs(x)))


def _gelu(x):
    return 0.5 * x * (1.0 + lax.erf(x * (2.0 ** -0.5)))


def _rms(x, g):
    return x * lax.rsqrt(jnp.mean(x * x, axis=-1, keepdims=True) + EPS) * g


def _params(*sem):
    return pltpu.CompilerParams(dimension_semantics=sem, vmem_limit_bytes=VMEM_LIMIT)


def _const_spec(shape):
    nd = len(shape)
    return pl.BlockSpec(shape, lambda *_: (0,) * nd, pipeline_mode=pl.Buffered(1))


def _ada_kernel(cv_ref, w_ref, b_ref, o_ref):
    s = _silu(cv_ref[...])
    o_ref[0] = _dot(s.astype(BF16), w_ref[0].astype(BF16)) + b_ref[0]


def _ada_call(cv, w_ada, b_ada):
    depth, d, n6 = w_ada.shape
    rows = cv.shape[0]
    tn = n6 // 4
    return pl.pallas_call(
        _ada_kernel,
        grid=(depth, n6 // tn),
        in_specs=[pl.BlockSpec((rows, d), lambda l, j: (0, 0)),
                  pl.BlockSpec((1, d, tn), lambda l, j: (l, 0, j)),
                  pl.BlockSpec((1, 1, tn), lambda l, j: (l, 0, j))],
        out_specs=pl.BlockSpec((1, rows, tn), lambda l, j: (l, 0, j)),
        out_shape=jax.ShapeDtypeStruct((depth, rows, n6), F32),
        compiler_params=_params("parallel", "parallel"),
        name="ada",
    )(cv, w_ada, b_ada.reshape(depth, 1, n6))


def _inproj_kernel(*refs, has_f):
    if has_f:
        f_ref, modp_ref, x_o = refs[0], refs[1], refs[-1]
        refs = refs[2:-1]
    (x_ref, mod_ref, n1_ref, wqk_ref, wvo_ref, wgc_ref, wgr_ref, gbc_ref, gbr_ref,
     wa_ref, wg_ref, wbr_ref, aqn_ref, akvn_ref, wuq_ref, wuk_ref, wuv_ref, vone_ref,
     cos_ref, sina_ref, sinb_ref, gvn_ref, gws_ref, gbs_ref,
     qk_o, vo_o, gc_o, gr_o, q_o, k_o, v_o, yc_o, br_o) = refs
    tm = x_ref.shape[1]
    mod = mod_ref[0]
    x = x_ref[0]
    if has_f:
        x = x + modp_ref[0][5:6] * f_ref[0].astype(F32)
        x_o[0] = x
    h = _rms(x, n1_ref[...]) * (1.0 + mod[1:2]) + mod[0:1]
    hb = h.astype(BF16)

    qk_o[0] = _dot(hb, wqk_ref[...])
    vo_o[0] = _dot(hb, wvo_ref[...]).astype(BF16)
    gc_o[0] = _dot(hb, wgc_ref[...]) + gbc_ref[...]
    gr_o[0] = _dot_nt(wgr_ref[...], hb) + gbr_ref[...]

    za = _dot(hb, wa_ref[...])
    aqn = _rms(za[:, :A_QRANK], aqn_ref[...]).astype(BF16)
    akvn = _rms(za[:, A_QRANK:A_QRANK + A_KVRANK], akvn_ref[...]).astype(BF16)
    cos = cos_ref[...]
    sina = sina_ref[...]
    sinb = sinb_ref[...]
    half = A_ROPE // 2

    def rope(t):
        return t * cos + pltpu.roll(t, half, 1) * sina + pltpu.roll(t, A_PAD - half, 1) * sinb

    kr = rope(za[:, A_QRANK + A_KVRANK:])
    qp = _dot(aqn, wuq_ref[...])
    kp = _dot(akvn, wuk_ref[...])
    for hh in range(A_HEADS):
        sl = slice(hh * A_PAD, (hh + 1) * A_PAD)
        q_o[0, :, sl] = (rope(qp[:, sl]) * (ATT_SCALE * LOG2E)).astype(BF16)
        k_o[0, :, sl] = (kp[:, sl] + kr).astype(BF16)
    v_o[0] = (_dot(akvn, wuv_ref[...]) + vone_ref[...]).astype(BF16)

    zg = _dot(hb, wg_ref[...])
    gu = _gelu(zg[:, :G_WIDTH])
    gv = _gelu(zg[:, G_WIDTH:])
    gvn = gvn_ref[...]
    bias = gbs_ref[...]
    for g in range(G_GROUPS):
        sl = slice(g * G_DG, (g + 1) * G_DG)
        xn = _rms(gv[:, sl], gvn[:, sl]).astype(BF16)
        ws = gws_ref[g]
        for ci in range(tm // G_CHUNK):
            r = slice(ci * G_CHUNK, (ci + 1) * G_CHUNK)
            sg = _dot(ws, xn[r]) + bias[:, sl]
            yc_o[0, r, sl] = (gu[r, sl] * sg).astype(BF16)

    br_o[0] = _sigmoid(_dot(hb, wbr_ref[...])).astype(BF16)


def _inproj_call(xs, mod, n1, w, tabs, lc, tm, f=None, mod_prev=None):
    b, l, d = xs.shape
    nct = lc // tm
    has_f = f is not None
    tok = lambda width: pl.BlockSpec((1, tm, width), lambda bi, j: (bi, j, 0))
    modspec = pl.BlockSpec((1, 6, d), lambda bi, j: (jnp.where(j < nct, b, bi), 0, 0))
    tab = pl.BlockSpec((tm, A_PAD), lambda bi, j: (j, 0))
    consts = [n1, w["wqk"], w["wvo"], w["wgc"], w["wgr"], w["gbc"], w["gbr"], w["wa"], w["wg"], w["wbr"],
              w["aqn"], w["akvn"], w["wuq"], w["wuk"], w["wuv"], w["vone"]]
    consts2 = [w["gvn"], w["gws"], w["gbs"]]
    in_specs = ([tok(d), modspec] + [_const_spec(a.shape) for a in consts] + [tab, tab, tab]
                + [_const_spec(a.shape) for a in consts2])
    args = [xs, mod, *consts, *tabs, *consts2]
    ng = 4 * M_HEADS
    out_shape = [jax.ShapeDtypeStruct((b, l, 2 * M_WIDTH), F32),
                 jax.ShapeDtypeStruct((b, l, 2 * M_WIDTH), BF16),
                 jax.ShapeDtypeStruct((b, l, ng), F32),
                 jax.ShapeDtypeStruct((b, ng, l), F32),
                 jax.ShapeDtypeStruct((b, l, A_HEADS * A_PAD), BF16),
                 jax.ShapeDtypeStruct((b, l, A_HEADS * A_PAD), BF16),
                 jax.ShapeDtypeStruct((b, l, A_HEADS * A_PAD), BF16),
                 jax.ShapeDtypeStruct((b, l, G_WIDTH), BF16),
                 jax.ShapeDtypeStruct((b, l, 3 * d), BF16)]
    out_specs = [tok(2 * M_WIDTH), tok(2 * M_WIDTH), tok(ng),
                 pl.BlockSpec((1, ng, tm), lambda bi, j: (bi, 0, j)),
                 tok(A_HEADS * A_PAD), tok(A_HEADS * A_PAD), tok(A_HEADS * A_PAD), tok(G_WIDTH), tok(3 * d)]
    if has_f:
        in_specs = [tok(d), modspec] + in_specs
        args = [f, mod_prev] + args
        out_shape.append(jax.ShapeDtypeStruct((b, l, d), F32))
        out_specs.append(tok(d))
    return pl.pallas_call(
        functools.partial(_inproj_kernel, has_f=has_f), grid=(b, l // tm), in_specs=in_specs,
        out_specs=out_specs, out_shape=out_shape,
        compiler_params=_params("parallel", "parallel"), name="inproj",
    )(*args)


def _mlstm_kernel(qk_ref, vo_ref, gc_ref, gr_ref, conv_ref, mnorm_ref, ya_ref,
                  q_s, kt_s, h_s, bc_s, ml_s, dl_s, br_s, cn_s, m_s, *, lc):
    l = qk_ref.shape[1]
    ch = M_CHUNK
    nc = l // ch
    ncc = lc // ch
    nh = M_HEADS
    ng = 4 * nh
    w = conv_ref[...]
    row = lax.broadcasted_iota(jnp.int32, (ch, 1), 0)

    def conv_body(j, carry):
        r0 = pl.multiple_of(j * ch, ch)
        cur = qk_ref[0, pl.ds(r0, ch), :]
        prev8 = qk_ref[0, pl.ds(pl.multiple_of(jnp.maximum(r0 - 8, 0), 8), 8), :]
        next8 = qk_ref[0, pl.ds(pl.multiple_of(jnp.minimum(r0 + ch, l - 8), 8), 8), :]
        seg_start = jnp.logical_or(j == 0, j == ncc)
        seg_end = jnp.logical_or(j == ncc - 1, j == nc - 1)
        pe = jnp.where(seg_start, 0.0, prev8[7:8, :])
        ne = jnp.where(seg_end, 0.0, next8[0:1, :])
        xp = jnp.where(row == 0, pe, pltpu.roll(cur, 1, 0))
        xn = jnp.where(row == ch - 1, ne, pltpu.roll(cur, ch - 1, 0))
        y = _silu(xp * w[0:1] + cur * w[1:2] + xn * w[2:3])
        q_s[pl.ds(r0, ch), :] = (y[:, :M_WIDTH] * (M_DH ** -0.5)).astype(BF16)
        kt_s[:, pl.ds(r0, ch)] = y[:, M_WIDTH:].T.astype(BF16)
        return carry

    lax.fori_loop(0, nc, conv_body, 0)

    ri = lax.broadcasted_iota(jnp.int32, (ch, ch), 0)
    ci = lax.broadcasted_iota(jnp.int32, (ch, ch), 1)
    lower = ri >= ci
    upper = ri <= ci
    ones_lo = jnp.where(lower, 1.0, 0.0).astype(BF16)
    ones_up = jnp.where(upper, 1.0, 0.0).astype(BF16)
    ones_blk = jnp.ones((ch, M_DH), BF16)

    ml_s[...] = jnp.zeros_like(ml_s)
    dl_s[...] = jnp.zeros_like(dl_s)

    def local_body(j, carry):
        r0 = pl.multiple_of(j * ch, ch)
        rows = pl.ds(r0, ch)
        gc = gc_ref[0, rows, :]
        gr = gr_ref[0, :, rows]
        lfc = _log_sigmoid(gc)
        lfr = _log_sigmoid(gr)
        pre_c = sum(_dot(ones_lo, p) for p in _split3(lfc))
        pre_r = sum(_dot(p, ones_up) for p in _split3(lfr))
        bcol = (pre_c, jnp.sum(lfc, axis=0, keepdims=True) + lfc - pre_c)
        brow = (pre_r, jnp.sum(lfr, axis=1, keepdims=True) + lfr - pre_r)
        for d in range(2):
            bc_s[d, rows, :] = bcol[d]
            br_s[d, :, rows] = brow[d]
        for hh in range(nh):
            sl = slice(hh * M_DH, (hh + 1) * M_DH)
            v1 = jnp.concatenate([vo_ref[0, rows, sl], ones_blk], axis=1)
            s = _dot(q_s[rows, sl], kt_s[sl, rows])
            for d in range(2):
                ii = d * 2 * nh + hh
                fi = ii + nh
                dmat = jnp.where(upper if d else lower,
                                 bcol[d][:, fi:fi + 1] - brow[d][fi:fi + 1, :] + gr[ii:ii + 1, :], -jnp.inf)
                m_loc = jnp.max(dmat, axis=-1, keepdims=True)
                nd = _dot((s * jnp.exp(dmat - m_loc)).astype(BF16), v1)
                h_s[d, rows, sl] = nd[:, :M_DH]
                ml_s[d, rows, fi:fi + 1] = m_loc
                dl_s[d, rows, fi:fi + 1] = nd[:, M_DH:M_DH + 1]
        return carry

    lax.fori_loop(0, nc, local_body, 0)

    cn_s[...] = jnp.zeros_like(cn_s)
    m_s[...] = jnp.zeros_like(m_s)
    lane_r = lax.broadcasted_iota(jnp.int32, (1, ng), 1)
    lane_c = lax.broadcasted_iota(jnp.int32, (ch, ng), 1)

    def scan_direction(r0, d):
        rows = pl.ds(r0, ch)
        gr = gr_ref[0, :, rows]
        br = br_s[d, :, rows]
        tot = jnp.sum(_log_sigmoid(gr), axis=1, keepdims=True)
        m_old = [m_s[d * nh + hh][:, 0:1] for hh in range(nh)]
        m_row = jnp.zeros((1, ng), F32)
        for hh in range(nh):
            m_row = jnp.where(lane_r == d * 2 * nh + nh + hh, m_old[hh], m_row)

        inter = bc_s[d, rows, :] + m_row
        ml = ml_s[d, rows, :]
        mt = jnp.maximum(inter, ml)
        a = jnp.exp(ml - mt)
        wi = jnp.exp(inter - mt)
        qc = []
        qn = jnp.zeros((ch, ng), F32)
        for hh in range(nh):
            fi = d * 2 * nh + nh + hh
            sl = slice(hh * M_DH, (hh + 1) * M_DH)
            qcn = _dot(q_s[rows, sl], cn_s[d * nh + hh].astype(BF16))
            qc.append(qcn[:, :M_DH])
            qn = jnp.where(lane_c == fi, qcn[:, M_DH:M_DH + ng], qn)
        den = a * dl_s[d, rows, :] + wi * qn
        rinv = 1.0 / jnp.maximum(jnp.abs(den), jnp.exp(-mt))
        c_loc = a * rinv
        c_int = wi * rinv

        for hh in range(nh):
            ii = d * 2 * nh + hh
            fi = ii + nh
            sidx = d * nh + hh
            sl = slice(hh * M_DH, (hh + 1) * M_DH)
            h_s[d, rows, sl] = c_loc[:, fi:fi + 1] * h_s[d, rows, sl] + c_int[:, fi:fi + 1] * qc[hh]

            b_e = tot[fi:fi + 1, :]
            d_end = b_e - br[fi:fi + 1, :] + gr[ii:ii + 1, :]
            m_end = jnp.max(d_end, axis=-1, keepdims=True)
            m_new = jnp.maximum(b_e + m_old[hh], m_end)
            ktw = (kt_s[sl, rows].astype(F32) * jnp.exp(d_end - m_end)).astype(BF16)
            v1 = jnp.concatenate([vo_ref[0, rows, sl], ones_blk], axis=1)
            cn_s[sidx] = (jnp.exp(b_e + m_old[hh] - m_new) * cn_s[sidx]
                          + jnp.exp(m_end - m_new) * _dot(ktw, v1))
            m_s[sidx] = jnp.broadcast_to(m_new, (1, M_DH))

    def scan_body(s, carry):
        scan_direction(pl.multiple_of(s * ch, ch), 0)
        jb = jnp.where(s < ncc, ncc - 1 - s, nc - 1 - s + ncc)
        scan_direction(pl.multiple_of(jb * ch, ch), 1)
        return carry

    lax.fori_loop(0, nc, scan_body, 0)

    mnorm = mnorm_ref[...]

    def out_body(j, carry):
        r0 = pl.multiple_of(j * ch, ch)
        hsum = h_s[0, pl.ds(r0, ch), :] + h_s[1, pl.ds(r0, ch), :]
        og = _sigmoid(vo_ref[0, pl.ds(r0, ch), M_WIDTH:].astype(F32))
        for hh in range(M_HEADS):
            sl = slice(hh * M_DH, (hh + 1) * M_DH)
            ya_ref[0, pl.ds(r0, ch), sl] = (_rms(hsum[:, sl], mnorm[:, sl]) * og[:, sl]).astype(BF16)
        return carry

    lax.fori_loop(0, nc, out_body, 0)


def _mlstm_call(qk, vo, gc, gr, conv, mnorm, lc):
    b, l, _ = qk.shape
    ng = 4 * M_HEADS
    return pl.pallas_call(
        functools.partial(_mlstm_kernel, lc=lc),
        grid=(b,),
        in_specs=[pl.BlockSpec((1, l, 2 * M_WIDTH), lambda bi: (bi, 0, 0), pipeline_mode=pl.Buffered(1)),
                  pl.BlockSpec((1, l, 2 * M_WIDTH), lambda bi: (bi, 0, 0)),
                  pl.BlockSpec((1, l, ng), lambda bi: (bi, 0, 0)),
                  pl.BlockSpec((1, ng, l), lambda bi: (bi, 0, 0)),
                  _const_spec(conv.shape), _const_spec(mnorm.shape)],
        out_specs=pl.BlockSpec((1, l, M_WIDTH), lambda bi: (bi, 0, 0)),
        out_shape=jax.ShapeDtypeStruct((b, l, M_WIDTH), BF16),
        scratch_shapes=[pltpu.VMEM((l, M_WIDTH), BF16),
                        pltpu.VMEM((M_WIDTH, l), BF16),
                        pltpu.VMEM((2, l, M_WIDTH), F32),
                        pltpu.VMEM((2, l, ng), F32),
                        pltpu.VMEM((2, l, ng), F32),
                        pltpu.VMEM((2, l, ng), F32),
                        pltpu.VMEM((2, ng, l), F32),
                        pltpu.VMEM((2 * M_HEADS, M_DH, 2 * M_DH), F32),
                        pltpu.VMEM((2 * M_HEADS, 1, M_DH), F32)],
        compiler_params=_params("parallel"), name="mlstm",
    )(qk, vo, gc, gr, conv, mnorm)


def _attn_kernel(q_ref, k_ref, v_ref, o_ref, *, lc, ctx_out):
    tq = q_ref.shape[1]
    l = k_ref.shape[1]
    qi = pl.program_id(2)
    nct = lc // tq
    lane = lax.broadcasted_iota(jnp.int32, (tq, 2 * A_VDIM), 1)

    def run(klen):
        outs = []
        for hh in range(2):
            sl = slice(hh * A_PAD, (hh + 1) * A_PAD)
            s = _dot_nt(q_ref[0, :, sl], k_ref[0, :klen, sl])
            p = jnp.exp2((s - jnp.max(s, axis=-1, keepdims=True)).astype(BF16))
            nd = _dot(p, v_ref[0, :klen, sl])
            outs.append(nd / pltpu.roll(nd, A_VDIM, 1))
        o_ref[0] = jnp.where(lane < A_VDIM, outs[0], pltpu.roll(outs[1], A_VDIM, 1)).astype(BF16)

    @pl.when(qi >= nct)
    def _():
        run(l)

    @pl.when(qi < nct)
    def _():
        if ctx_out:
            run(lc)
        else:
            o_ref[...] = jnp.zeros_like(o_ref)


def _attn_call(q, k, v, lc, tq, ctx_out):
    b, l, _ = q.shape
    pairs = A_HEADS // 2
    return pl.pallas_call(
        functools.partial(_attn_kernel, lc=lc, ctx_out=ctx_out),
        grid=(b, pairs, l // tq),
        in_specs=[pl.BlockSpec((1, tq, 2 * A_PAD), lambda bi, p, qi: (bi, qi, p)),
                  pl.BlockSpec((1, l, 2 * A_PAD), lambda bi, p, qi: (bi, 0, p)),
                  pl.BlockSpec((1, l, 2 * A_PAD), lambda bi, p, qi: (bi, 0, p))],
        out_specs=pl.BlockSpec((1, tq, 2 * A_VDIM), lambda bi, p, qi: (bi, qi, p)),
        out_shape=jax.ShapeDtypeStruct((b, l, A_WIDTH), BF16),
        compiler_params=_params("parallel", "parallel", "arbitrary"), name="attn",
    )(q, k, v)


def _merge_kernel(x_ref, mod_ref, ya_ref, yb_ref, yc_ref, br_ref, wpa_ref, wpb_ref, wpc_ref, wout_ref,
                  n2_ref, r_ref, rb_ref, o_ref, h2_o, comb_o, idx_o, idxt_o, cnt_o):
    d = x_ref.shape[2]
    br = br_ref[0]
    mod = mod_ref[0]
    y = (br[:, :d].astype(F32) * _dot(ya_ref[0], wpa_ref[...])
         + br[:, d:2 * d].astype(F32) * _dot(yb_ref[0], wpb_ref[...])
         + br[:, 2 * d:].astype(F32) * _dot(yc_ref[0], wpc_ref[...]))
    out = _dot(y.astype(BF16), wout_ref[...])
    x = x_ref[0] + mod[2:3] * out
    o_ref[0] = x
    _route(x, mod, n2_ref, r_ref, rb_ref, h2_o, comb_o, idx_o, idxt_o, cnt_o)


def _merge_call(xs, mod, ya, yb, yc, br, w, n2, r3, rb, lc, tm):
    b, l, d = xs.shape
    nct = lc // tm
    tok = lambda width: pl.BlockSpec((1, tm, width), lambda bi, j: (bi, j, 0))
    consts = [w["wpa"], w["wpb"], w["wpc"], w["wout"], n2, r3, rb]
    return pl.pallas_call(
        _merge_kernel, grid=(b, l // tm),
        in_specs=[tok(d), pl.BlockSpec((1, 6, d), lambda bi, j: (jnp.where(j < nct, b, bi), 0, 0)),
                  tok(M_WIDTH), tok(A_WIDTH), tok(G_WIDTH), tok(3 * d)] + [_const_spec(a.shape) for a in consts],
        out_specs=[tok(d), tok(d), tok(N_EXPERTS), tok(8), pl.BlockSpec((1, 8, tm), lambda bi, j: (bi, 0, j)),
                   pl.BlockSpec((1, 1, 8, R_PAD), lambda bi, j: (bi, j, 0, 0))],
        out_shape=[jax.ShapeDtypeStruct((b, l, d), F32),
                   jax.ShapeDtypeStruct((b, l, d), BF16), jax.ShapeDtypeStruct((b, l, N_EXPERTS), F32),
                   jax.ShapeDtypeStruct((b, l, 8), jnp.int32), jax.ShapeDtypeStruct((b, 8, l), jnp.int32),
                   jax.ShapeDtypeStruct((b, l // tm, 8, R_PAD), F32)],
        compiler_params=_params("parallel", "parallel"), name="merge",
    )(xs, mod, ya, yb, yc, br, *consts)


def _route(x, mod, n2_ref, r_ref, rb_ref, h2_o, comb_o, idx_o, idxt_o, cnt_o):
    tm = x.shape[0]
    h2 = _rms(x, n2_ref[...]) * (1.0 + mod[4:5]) + mod[3:4]
    h2_o[0] = h2.astype(BF16)

    a = _split3(h2)
    r = [r_ref[i] for i in range(3)]
    logits = (_dot(a[0], r[0]) + (_dot(a[0], r[1]) + _dot(a[1], r[0]))
              + (_dot(a[1], r[1]) + _dot(a[0], r[2]) + _dot(a[2], r[0]))) + rb_ref[...]
    el = logits[:, :N_EXPERTS]
    gl = logits[:, N_EXPERTS:N_EXPERTS + N_GROUPS]
    big = 1e9

    lane_g = lax.broadcasted_iota(jnp.int32, (tm, N_GROUPS), 1).astype(F32)
    gmax = jnp.max(gl, axis=-1, keepdims=True)
    g_sel = jnp.min(jnp.where(gl == gmax, lane_g, big), axis=-1, keepdims=True)
    g_prob = 1.0 / jnp.sum(jnp.exp(gl - gmax), axis=-1, keepdims=True)

    lane_i = lax.broadcasted_iota(jnp.int32, (tm, N_EXPERTS), 1)
    lane_e = lane_i.astype(F32)
    lane_grp = (lane_i // EXP_PER_GROUP).astype(F32)
    v1 = jnp.where(lane_grp == g_sel, el, -jnp.inf)
    t1 = jnp.max(v1, axis=-1, keepdims=True)
    i1 = jnp.min(jnp.where(v1 == t1, lane_e, big), axis=-1, keepdims=True)
    v2 = jnp.where(lane_e == i1, -jnp.inf, v1)
    t2 = jnp.max(v2, axis=-1, keepdims=True)
    i2 = jnp.min(jnp.where(v2 == t2, lane_e, big), axis=-1, keepdims=True)
    e21 = jnp.exp(t2 - t1)
    w1 = 1.0 / (1.0 + e21)
    w2 = e21 * w1
    comb_o[0] = (jnp.where(lane_e == i1, w1, 0.0) + jnp.where(lane_e == i2, w2, 0.0)) * g_prob

    lane_p = lax.broadcasted_iota(jnp.int32, (tm, R_PAD), 1)
    onehot = jnp.where(lane_p.astype(F32) == g_sel, 1.0, 0.0)
    ri = lax.broadcasted_iota(jnp.int32, (tm, tm), 0)
    ci = lax.broadcasted_iota(jnp.int32, (tm, tm), 1)
    before = jnp.where(ri > ci, 1.0, 0.0).astype(BF16)
    rank = jnp.sum(_dot(before, onehot.astype(BF16)) * onehot, axis=-1, keepdims=True)
    cnt_o[0, 0] = jnp.broadcast_to(jnp.sum(onehot, axis=0, keepdims=True), (8, R_PAD))
    fields = jnp.where(lane_p == 0, g_sel, jnp.where(lane_p == 1, rank, 0.0))
    idx_o[0] = fields[:, :8].astype(jnp.int32)
    idxt_o[0] = fields.T[:8, :].astype(jnp.int32)


def _experts_kernel(st_ref, h2_ref, comb_ref, idx_ref, idxt_ref, w1_ref, w3_ref, w2_ref, o_ref, hs_s, cs_s, *, tb):
    l, d = h2_ref.shape[1], h2_ref.shape[2]
    nblk = l // tb
    ch = MOE_CHUNK
    bi = pl.program_id(0)
    g = pl.program_id(1)

    @pl.when(g == 0)
    def _():
        o_ref[...] = jnp.zeros_like(o_ref)

    base = (bi * N_GROUPS + g) * (nblk + 1)
    cnt = st_ref[base + nblk]
    sub_iota = lax.broadcasted_iota(jnp.int32, (ch, tb), 0)
    lane_iota = lax.broadcasted_iota(jnp.int32, (tb, ch), 1)
    lane_e = lax.broadcasted_iota(jnp.int32, (ch, N_EXPERTS), 1)

    def chunk_body(c, carry):
        lo = c * ch
        hs_s[...] = jnp.zeros_like(hs_s)
        cs_s[...] = jnp.zeros_like(cs_s)
        for k in range(nblk):
            s_k = st_ref[base + k]
            e_k = st_ref[base + k + 1]
            rows = slice(k * tb, (k + 1) * tb)

            @pl.when(jnp.logical_and(s_k < lo + ch, e_k > lo))
            def _(s_k=s_k, rows=rows):
                it = idxt_ref[0, :, rows]
                pos = jnp.where(it[0:1] == g, it[1:2] + (s_k - lo), -1)
                p = jnp.where(sub_iota == pos, 1.0, 0.0).astype(BF16)
                hs_s[...] += _dot(p, h2_ref[0, rows, :])
                cs_s[...] += sum(_dot(p, piece) for piece in _split3(comb_ref[0, rows, :]))

        hsb = hs_s[...].astype(BF16)
        cs = cs_s[...]
        y = jnp.zeros((ch, d), F32)
        for e in range(EXP_PER_GROUP):
            ce = jnp.sum(jnp.where(lane_e == g * EXP_PER_GROUP + e, cs, 0.0), axis=-1, keepdims=True)
            hid = (_silu(_dot(hsb, w1_ref[e])) * _dot(hsb, w3_ref[e]) * ce).astype(BF16)
            y = y + _dot(hid, w2_ref[e])
        yb = y.astype(BF16)

        for k in range(nblk):
            s_k = st_ref[base + k]
            e_k = st_ref[base + k + 1]
            rows = slice(k * tb, (k + 1) * tb)

            @pl.when(jnp.logical_and(s_k < lo + ch, e_k > lo))
            def _(s_k=s_k, rows=rows):
                ic = idx_ref[0, rows, :]
                pos = jnp.where(ic[:, 0:1] == g, ic[:, 1:2] + (s_k - lo), -1)
                q = jnp.where(lane_iota == pos, 1.0, 0.0).astype(BF16)
                o_ref[0, rows, :] = (o_ref[0, rows, :].astype(F32) + _dot(q, yb)).astype(BF16)

        return carry

    lax.fori_loop(0, (cnt + ch - 1) // ch, chunk_body, 0)


def _experts_call(starts, h2, comb, idx, idxt, w1, w3, w2, tb):
    b, l, d = h2.shape
    whole = lambda width: pl.BlockSpec((1, l, width), lambda bi, g, st: (bi, 0, 0))
    grid_spec = pltpu.PrefetchScalarGridSpec(
        num_scalar_prefetch=1, grid=(b, N_GROUPS),
        in_specs=[whole(d), whole(N_EXPERTS), whole(8), pl.BlockSpec((1, 8, l), lambda bi, g, st: (bi, 0, 0)),
                  pl.BlockSpec((EXP_PER_GROUP, d, D_EXPERT), lambda bi, g, st: (g, 0, 0)),
                  pl.BlockSpec((EXP_PER_GROUP, d, D_EXPERT), lambda bi, g, st: (g, 0, 0)),
                  pl.BlockSpec((EXP_PER_GROUP, D_EXPERT, d), lambda bi, g, st: (g, 0, 0))],
        out_specs=whole(d),
        scratch_shapes=[pltpu.VMEM((MOE_CHUNK, d), F32), pltpu.VMEM((MOE_CHUNK, N_EXPERTS), F32)])
    return pl.pallas_call(
        functools.partial(_experts_kernel, tb=tb), grid_spec=grid_spec,
        out_shape=jax.ShapeDtypeStruct((b, l, d), BF16),
        compiler_params=_params("parallel", "arbitrary"), name="experts",
    )(starts, h2, comb, idx, idxt, w1, w3, w2)


def _group_starts(cnt):
    c = cnt[:, :, 0, :N_GROUPS].astype(jnp.int32)
    s = jnp.cumsum(c, axis=1)
    s = jnp.concatenate([jnp.zeros_like(s[:, :1]), s], axis=1)
    return jnp.transpose(s, (0, 2, 1)).reshape(-1)


def _final_kernel(x_ref, mod_ref, f_ref, g_ref, o_ref):
    o_ref[0] = _rms(x_ref[0] + mod_ref[0][5:6] * f_ref[0].astype(F32), g_ref[...])


def _final_call(xs, mod, f, g, lc, tm):
    b, l, d = xs.shape
    off = lc // tm
    lat = pl.BlockSpec((1, tm, d), lambda bi, j: (bi, j + off, 0))
    return pl.pallas_call(
        _final_kernel, grid=(b, (l - lc) // tm),
        in_specs=[lat, pl.BlockSpec((1, 6, d), lambda bi, j: (bi, 0, 0)), lat, _const_spec(g.shape)],
        out_specs=pl.BlockSpec((1, tm, d), lambda bi, j: (bi, j, 0)),
        out_shape=jax.ShapeDtypeStruct((b, l - lc, d), F32),
        compiler_params=_params("parallel", "parallel"), name="final_norm",
    )(xs, mod, f, g)


def _rope_tables(t_len, lc):
    half = A_ROPE // 2
    rows = t_len // GRID_W
    r = jnp.repeat(jnp.arange(rows, dtype=F32), GRID_W)
    col = jnp.tile(jnp.arange(GRID_W, dtype=F32), rows)
    inv = ROPE_THETA ** (-jnp.arange(0, half, 2, dtype=F32) / half)
    ang = jnp.concatenate([r[:, None] * inv, col[:, None] * inv], axis=-1)
    cos = jnp.concatenate([jnp.ones((lc, half), F32), jnp.cos(ang)], axis=0)
    sin = jnp.concatenate([jnp.zeros((lc, half), F32), jnp.sin(ang)], axis=0)
    l = lc + t_len
    ones = jnp.ones((l, A_NOPE), F32)
    zeros = jnp.zeros((l, A_NOPE), F32)
    tail1 = jnp.ones((l, A_PAD - A_NOPE - A_ROPE), F32)
    tail0 = jnp.zeros((l, A_PAD - A_NOPE - A_ROPE), F32)
    zh = jnp.zeros((l, half), F32)
    cos_t = jnp.concatenate([ones, cos, cos, tail1], axis=-1)
    sina_t = jnp.concatenate([zeros, zh, sin, tail0], axis=-1)
    sinb_t = jnp.concatenate([zeros, -sin, zh, tail0], axis=-1)
    return cos_t, sina_t, sinb_t


def _layer_weights(l, w_in, m_gate_b, a_qnorm, a_wuq, a_kvnorm, a_wukv, g_ws, g_bs, g_vnorm,
                   w_pa, w_pb, w_pc, w_out):
    d = w_in.shape[1]
    wi = w_in[l]
    o = 0

    def take(n):
        nonlocal o
        s = wi[:, o:o + n]
        o += n
        return s

    mq, mk, mv, mo, mg = take(M_WIDTH), take(M_WIDTH), take(M_WIDTH), take(M_WIDTH), take(4 * M_HEADS)
    aq, akv, akr = take(A_QRANK), take(A_KVRANK), take(A_ROPE)
    gu, gv = take(G_WIDTH), take(G_WIDTH)
    br = take(3 * d)
    akr_pad = jnp.concatenate([jnp.zeros((d, A_NOPE), F32), akr,
                               jnp.zeros((d, A_PAD - A_NOPE - A_ROPE), F32)], axis=1)
    wuq = a_wuq[l].reshape(A_QRANK, A_HEADS, A_NOPE + A_ROPE)
    wuq = jnp.pad(wuq, ((0, 0), (0, 0), (0, A_PAD - A_NOPE - A_ROPE))).reshape(A_QRANK, A_HEADS * A_PAD)
    wukv = a_wukv[l].reshape(A_KVRANK, A_HEADS, A_NOPE + A_VDIM)
    wuk = jnp.pad(wukv[:, :, :A_NOPE], ((0, 0), (0, 0), (0, A_PAD - A_NOPE))).reshape(A_KVRANK, A_HEADS * A_PAD)
    wuv = jnp.pad(wukv[:, :, A_NOPE:], ((0, 0), (0, 0), (0, A_PAD - A_VDIM))).reshape(A_KVRANK, A_HEADS * A_PAD)
    vone = jnp.tile(jnp.concatenate([jnp.zeros((A_VDIM,), F32), jnp.ones((A_PAD - A_VDIM,), F32)]),
                    A_HEADS).reshape(1, A_HEADS * A_PAD)
    gbs = jnp.repeat(g_bs[l].T, G_DG, axis=1)
    return dict(
        wqk=jnp.concatenate([mq, mk], 1).astype(BF16), wvo=jnp.concatenate([mv, mo], 1).astype(BF16),
        wgc=mg.astype(BF16), wgr=mg.T.astype(BF16),
        gbc=m_gate_b[l].reshape(1, -1), gbr=m_gate_b[l].reshape(-1, 1),
        wa=jnp.concatenate([aq, akv, akr_pad], 1).astype(BF16),
        wg=jnp.concatenate([gu, gv], 1).astype(BF16), wbr=br.astype(BF16),
        aqn=a_qnorm[l].reshape(1, -1), akvn=a_kvnorm[l].reshape(1, -1),
        wuq=wuq.astype(BF16), wuk=wuk.astype(BF16), wuv=wuv.astype(BF16), vone=vone,
        gvn=g_vnorm[l].reshape(1, -1), gws=g_ws[l].astype(BF16), gbs=gbs,
        wpa=w_pa[l].astype(BF16), wpb=w_pb[l].astype(BF16), wpc=w_pc[l].astype(BF16),
        wout=w_out[l].astype(BF16))


def _router_weights(r_group, r_group_b, r_expert, r_expert_b):
    d = r_group.shape[0]
    pad = R_PAD - N_EXPERTS - N_GROUPS
    r = jnp.concatenate([r_expert, r_group, jnp.zeros((d, pad), F32)], axis=1)
    rb = jnp.concatenate([r_expert_b, r_group_b, jnp.zeros((pad,), F32)]).reshape(1, R_PAD)
    return jnp.stack(_split3(r)), rb


def _tile(n, lc, candidates):
    for t in candidates:
        if n % t == 0 and lc % t == 0:
            return t
    raise ValueError("sequence lengths must be multiples of 128")


def kernel(x, c, ctx, c_ctx, w_ada, b_ada, norm1, norm2, final_norm, w_in, m_conv, m_gate_b, m_norm, a_qnorm, a_wuq, a_kvnorm, a_wukv, g_ws, g_bs, g_vnorm, w_pa, w_pb, w_pc, w_out, r_group, r_group_b, r_expert, r_expert_b, e_w1, e_w3, e_w2):
    b, t_len, d = x.shape
    lc = ctx.shape[1]
    l = lc + t_len
    depth = w_in.shape[0]
    tm = _tile(l, lc, (256, 128))

    xs = jnp.concatenate([ctx, x], axis=1)
    cv = jnp.concatenate([c, c_ctx[None, :]], axis=0)
    mod_all = _ada_call(cv, w_ada, b_ada).reshape(depth, b + 1, 6, d)
    tabs = _rope_tables(t_len, lc)

    f = None
    mod_prev = None
    for li in range(depth):
        last = li == depth - 1
        mod = mod_all[li]
        w = _layer_weights(li, w_in, m_gate_b, a_qnorm, a_wuq, a_kvnorm, a_wukv, g_ws, g_bs, g_vnorm,
                           w_pa, w_pb, w_pc, w_out)
        outs = _inproj_call(xs, mod, norm1[li].reshape(1, d), w, tabs, lc, tm, f, mod_prev)
        qk, vo, gc, gr, q, k, v, yc, br = outs[:9]
        if f is not None:
            xs = outs[9]
        ya = _mlstm_call(qk, vo, gc, gr, m_conv[li], m_norm[li].reshape(1, -1), lc)
        yb = _attn_call(q, k, v, lc, tm, not last)
        r3, rb = _router_weights(r_group[li], r_group_b[li], r_expert[li], r_expert_b[li])
        xs, h2, comb, idx, idxt, cnt = _merge_call(xs, mod, ya, yb, yc, br, w, norm2[li].reshape(1, d),
                                                   r3, rb, lc, tm)
        f = _experts_call(_group_starts(cnt), h2, comb, idx, idxt, e_w1[li].astype(BF16),
                          e_w3[li].astype(BF16), e_w2[li].astype(BF16), tm)
        mod_prev = mod
    return _final_call(xs, mod_prev, f, final_norm.reshape(1, d), lc, tm)
```

```python
import functools

import jax
import jax.numpy as jnp
from jax import lax
from jax.experimental import pallas as pl
from jax.experimental.pallas import tpu as pltpu

F32 = jnp.float32
BF16 = jnp.bfloat16

EPS = 1e-6
GRID_W = 64
ROPE_THETA = 10000.0

M_HEADS = 4
M_DH = 128
M_WIDTH = M_HEADS * M_DH
M_CHUNK = 128

A_HEADS = 8
A_NOPE = 64
A_ROPE = 32
A_VDIM = 64
A_QRANK = 384
A_KVRANK = 256
A_WIDTH = A_HEADS * A_VDIM
A_PAD = 128
A_HPS = 4
ATT_SCALE = (A_NOPE + A_ROPE) ** -0.5
LOG2E = 1.4426950408889634

G_GROUPS = 4
G_CHUNK = 128
G_WIDTH = 512
G_DG = G_WIDTH // G_GROUPS

N_GROUPS = 4
EXP_PER_GROUP = 4
N_EXPERTS = N_GROUPS * EXP_PER_GROUP
D_EXPERT = 512
R_PAD = 128
R_SEG = 32
MOE_CHUNK = 256

VMEM_LIMIT = 56 * 1024 * 1024


def _dot(a, b):
    return jnp.dot(a, b, preferred_element_type=F32)


def _dot_nt(a, b):
    return lax.dot_general(a, b, (((1,), (1,)), ((), ())), preferred_element_type=F32)


def _dot_tn(a, b):
    return lax.dot_general(a, b, (((0,), (0,)), ((), ())), preferred_element_type=F32)


def _split3(x):
    hi = x.astype(BF16)
    r = x - hi.astype(F32)
    mid = r.astype(BF16)
    lo = (r - mid.astype(F32)).astype(BF16)
    return hi, mid, lo


def _sigmoid(x):
    return 1.0 / (1.0 + jnp.exp(-x))


def _silu(x):
    return x * _sigmoid(x)


def _log_sigmoid(x):
    return jnp.minimum(x, 0.0) - jnp.log1p(jnp.exp(-jnp.abs(x)))


def _gelu(x):
    return 0.5 * x * (1.0 + lax.erf(x * (2.0 ** -0.5)))


def _rms(x, g):
    return x * lax.rsqrt(jnp.mean(x * x, axis=-1, keepdims=True) + EPS) * g


def _params(*sem):
    return pltpu.CompilerParams(dimension_semantics=sem, vmem_limit_bytes=VMEM_LIMIT)


def _const_spec(shape):
    nd = len(shape)
    return pl.BlockSpec(shape, lambda *_: (0,) * nd, pipeline_mode=pl.Buffered(1))


def _ada_kernel(cv_ref, w_ref, b_ref, o_ref):
    s = _silu(cv_ref[...])
    o_ref[0] = _dot(s.astype(BF16), w_ref[0].astype(BF16)) + b_ref[0]


def _ada_call(cv, w_ada, b_ada):
    depth, d, n6 = w_ada.shape
    rows = cv.shape[0]
    tn = n6 // 4
    return pl.pallas_call(
        _ada_kernel,
        grid=(depth, n6 // tn),
        in_specs=[pl.BlockSpec((rows, d), lambda l, j: (0, 0)),
                  pl.BlockSpec((1, d, tn), lambda l, j: (l, 0, j)),
                  pl.BlockSpec((1, 1, tn), lambda l, j: (l, 0, j))],
        out_specs=pl.BlockSpec((1, rows, tn), lambda l, j: (l, 0, j)),
        out_shape=jax.ShapeDtypeStruct((depth, rows, n6), F32),
        compiler_params=_params("parallel", "parallel"),
        name="ada",
    )(cv, w_ada, b_ada.reshape(depth, 1, n6))


def _inproj_kernel(*refs, has_f):
    if has_f:
        f_ref, modp_ref, x_o = refs[0], refs[1], refs[-1]
        refs = refs[2:-1]
    (x_ref, mod_ref, n1_ref, wqk_ref, wvo_ref, wgi_ref, wgf_ref, wgr_ref, gbi_ref, gbf_ref, gbr_ref,
     wa_ref, wg_ref, wbr_ref, aqn_ref, akvn_ref, wuq_ref, wuk_ref, wuv_ref, vone_ref,
     cos_ref, sina_ref, sinb_ref, gvn_ref, gws_ref, gbs_ref,
     qk_o, vo_o, gi_o, gf_o, gr_o, q_o, k_o, v_o, yc_o, br_o) = refs
    tm = x_ref.shape[1]
    mod = mod_ref[0]
    x = x_ref[0]
    if has_f:
        x = x + modp_ref[0][5:6] * f_ref[0].astype(F32)
        x_o[0] = x
    h = _rms(x, n1_ref[...]) * (1.0 + mod[1:2]) + mod[0:1]
    hb = h.astype(BF16)

    qk_o[0] = _dot(hb, wqk_ref[...])
    vo_o[0] = _dot(hb, wvo_ref[...]).astype(BF16)
    gi_o[0] = _dot(hb, wgi_ref[...]) + gbi_ref[...]
    gf_o[0] = _dot(hb, wgf_ref[...]) + gbf_ref[...]
    gr_o[0] = _dot_nt(wgr_ref[...], hb) + gbr_ref[...]

    za = _dot(hb, wa_ref[...])
    aqn = _rms(za[:, :A_QRANK], aqn_ref[...]).astype(BF16)
    akvn = _rms(za[:, A_QRANK:A_QRANK + A_KVRANK], akvn_ref[...]).astype(BF16)
    cos = cos_ref[...]
    sina = sina_ref[...]
    sinb = sinb_ref[...]
    half = A_ROPE // 2

    def rope(t):
        return t * cos + pltpu.roll(t, half, 1) * sina + pltpu.roll(t, A_PAD - half, 1) * sinb

    kr = rope(za[:, A_QRANK + A_KVRANK:])
    qp = _dot(aqn, wuq_ref[...])
    kp = _dot(akvn, wuk_ref[...])
    for hh in range(A_HEADS):
        sl = slice(hh * A_PAD, (hh + 1) * A_PAD)
        q_o[0, :, sl] = (rope(qp[:, sl]) * (ATT_SCALE * LOG2E)).astype(BF16)
        k_o[0, :, sl] = (kp[:, sl] + kr).astype(BF16)
    v_o[0] = (_dot(akvn, wuv_ref[...]) + vone_ref[...]).astype(BF16)

    zg = _dot(hb, wg_ref[...])
    gu = _gelu(zg[:, :G_WIDTH])
    gv = _gelu(zg[:, G_WIDTH:])
    gvn = gvn_ref[...]
    bias = gbs_ref[...]
    for g in range(G_GROUPS):
        sl = slice(g * G_DG, (g + 1) * G_DG)
        xn = _rms(gv[:, sl], gvn[:, sl]).astype(BF16)
        ws = gws_ref[g]
        for ci in range(tm // G_CHUNK):
            r = slice(ci * G_CHUNK, (ci + 1) * G_CHUNK)
            sg = _dot(ws, xn[r]) + bias[:, sl]
            yc_o[0, r, sl] = (gu[r, sl] * sg).astype(BF16)

    br_o[0] = _sigmoid(_dot(hb, wbr_ref[...])).astype(BF16)


def _inproj_call(xs, mod, n1, w, tabs, lc, tm, f=None, mod_prev=None):
    b, l, d = xs.shape
    nct = lc // tm
    has_f = f is not None
    tok = lambda width: pl.BlockSpec((1, tm, width), lambda bi, j: (bi, j, 0))
    modspec = pl.BlockSpec((1, 6, d), lambda bi, j: (jnp.where(j < nct, b, bi), 0, 0))
    tab = pl.BlockSpec((tm, A_PAD), lambda bi, j: (j, 0))
    consts = [n1, w["wqk"], w["wvo"], w["wgi"], w["wgf"], w["wgr"], w["gbi"], w["gbf"], w["gbr"], w["wa"], w["wg"], w["wbr"],
              w["aqn"], w["akvn"], w["wuq"], w["wuk"], w["wuv"], w["vone"]]
    consts2 = [w["gvn"], w["gws"], w["gbs"]]
    in_specs = ([tok(d), modspec] + [_const_spec(a.shape) for a in consts] + [tab, tab, tab]
                + [_const_spec(a.shape) for a in consts2])
    args = [xs, mod, *consts, *tabs, *consts2]
    ng = 4 * M_HEADS
    out_shape = [jax.ShapeDtypeStruct((b, l, 2 * M_WIDTH), F32),
                 jax.ShapeDtypeStruct((b, l, 2 * M_WIDTH), BF16),
                 jax.ShapeDtypeStruct((b, l, ng // 2), F32),
                 jax.ShapeDtypeStruct((b, l, ng // 2), F32),
                 jax.ShapeDtypeStruct((b, ng, l), F32),
                 jax.ShapeDtypeStruct((b, l, A_HEADS * A_PAD), BF16),
                 jax.ShapeDtypeStruct((b, l, A_HEADS * A_PAD), BF16),
                 jax.ShapeDtypeStruct((b, l, A_HEADS * A_PAD), BF16),
                 jax.ShapeDtypeStruct((b, l, G_WIDTH), BF16),
                 jax.ShapeDtypeStruct((b, l, 3 * d), BF16)]
    out_specs = [tok(2 * M_WIDTH), tok(2 * M_WIDTH), tok(ng // 2), tok(ng // 2),
                 pl.BlockSpec((1, ng, tm), lambda bi, j: (bi, 0, j)),
                 tok(A_HEADS * A_PAD), tok(A_HEADS * A_PAD), tok(A_HEADS * A_PAD), tok(G_WIDTH), tok(3 * d)]
    if has_f:
        in_specs = [tok(d), modspec] + in_specs
        args = [f, mod_prev] + args
        out_shape.append(jax.ShapeDtypeStruct((b, l, d), F32))
        out_specs.append(tok(d))
    return pl.pallas_call(
        functools.partial(_inproj_kernel, has_f=has_f), grid=(b, l // tm), in_specs=in_specs,
        out_specs=out_specs, out_shape=out_shape,
        compiler_params=_params("parallel", "parallel"), name="inproj",
    )(*args)


def _scan(x, op, fill, axis, reverse):
    n = x.shape[axis]
    idx = lax.broadcasted_iota(jnp.int32, x.shape, axis)
    k = 1
    while k < n:
        if reverse:
            x = op(x, jnp.where(idx >= n - k, fill, pltpu.roll(x, n - k, axis)))
        else:
            x = op(x, jnp.where(idx < k, fill, pltpu.roll(x, k, axis)))
        k *= 2
    return x


def _mlstm_kernel(qk_ref, vo_ref, gi_ref, gf_ref, gr_ref, conv_ref, mnorm_ref, ya_ref,
                  q_s, kt_s, h_s, bc_s, ml_s, dl_s, br_s, cn_s, m_s, s_s, p_s, qcn_s, u_s, *, lc):
    l = qk_ref.shape[1]
    ch = M_CHUNK
    nc = l // ch
    ncc = lc // ch
    nh = M_HEADS
    ng = 2 * nh
    w = conv_ref[...]
    row = lax.broadcasted_iota(jnp.int32, (ch, 1), 0)

    def conv_body(j, carry):
        r0 = pl.multiple_of(j * ch, ch)
        cur = qk_ref[0, pl.ds(r0, ch), :]
        prev8 = qk_ref[0, pl.ds(pl.multiple_of(jnp.maximum(r0 - 8, 0), 8), 8), :]
        next8 = qk_ref[0, pl.ds(pl.multiple_of(jnp.minimum(r0 + ch, l - 8), 8), 8), :]
        seg_start = jnp.logical_or(j == 0, j == ncc)
        seg_end = jnp.logical_or(j == ncc - 1, j == nc - 1)
        pe = jnp.where(seg_start, 0.0, prev8[7:8, :])
        ne = jnp.where(seg_end, 0.0, next8[0:1, :])
        xp = jnp.where(row == 0, pe, pltpu.roll(cur, 1, 0))
        xn = jnp.where(row == ch - 1, ne, pltpu.roll(cur, ch - 1, 0))
        y = _silu(xp * w[0:1] + cur * w[1:2] + xn * w[2:3])
        q_s[pl.ds(r0, ch), :] = (y[:, :M_WIDTH] * (M_DH ** -0.5)).astype(BF16)
        kt_s[:, pl.ds(r0, ch)] = y[:, M_WIDTH:].T.astype(BF16)
        return carry

    lax.fori_loop(0, nc, conv_body, 0)

    ri = lax.broadcasted_iota(jnp.int32, (ch, ch), 0)
    ci = lax.broadcasted_iota(jnp.int32, (ch, ch), 1)
    lower = ri >= ci
    upper = ri <= ci
    ones_blk = jnp.ones((ch, M_DH), BF16)
    fwd_c = lax.broadcasted_iota(jnp.int32, (ch, ng), 1) < nh
    fwd_r = lax.broadcasted_iota(jnp.int32, (ng, ch), 0) < nh
    lane_c = lax.broadcasted_iota(jnp.int32, (ch, ng), 1)

    def local_body(j, carry):
        r0 = pl.multiple_of(j * ch, ch)
        rows = pl.ds(r0, ch)
        lfc = _log_sigmoid(gf_ref[0, rows, :])
        gr = gr_ref[0, :, rows]
        lfr = _log_sigmoid(gr[ng:])
        pre_c = _scan(lfc, jnp.add, 0.0, 0, False)
        pre_r = _scan(lfr, jnp.add, 0.0, 1, False)
        b_c = jnp.where(fwd_c, pre_c, jnp.sum(lfc, axis=0, keepdims=True) + lfc - pre_c)
        b_r = jnp.where(fwd_r, pre_r, jnp.sum(lfr, axis=1, keepdims=True) + lfr - pre_r)
        g_c = gi_ref[0, rows, :] - b_c
        g_r = gr[:ng] - b_r
        cg_c = jnp.where(fwd_c, _scan(g_c, jnp.maximum, -jnp.inf, 0, False),
                         _scan(g_c, jnp.maximum, -jnp.inf, 0, True))
        bc_s[rows, :] = b_c
        br_s[:, rows] = b_r
        ml_s[rows, :] = b_c + cg_c
        dl = jnp.zeros((ch, ng), F32)
        for hh in range(nh):
            sl = slice(hh * M_DH, (hh + 1) * M_DH)
            s_s[hh] = _dot(q_s[rows, sl], kt_s[sl, rows])
        for hh in range(nh):
            s = s_s[hh]
            for d in range(2):
                jj = d * nh + hh
                wgt = jnp.exp(jnp.where(upper if d else lower, g_r[jj:jj + 1, :] - cg_c[:, jj:jj + 1], -jnp.inf))
                p_s[jj] = (s * wgt).astype(BF16)
        for hh in range(nh):
            sl = slice(hh * M_DH, (hh + 1) * M_DH)
            v1 = jnp.concatenate([vo_ref[0, rows, sl], ones_blk], axis=1)
            for d in range(2):
                jj = d * nh + hh
                nd = _dot(p_s[jj], v1)
                h_s[d, rows, sl] = nd[:, :M_DH]
                dl = jnp.where(lane_c == jj, nd[:, M_DH:M_DH + ng], dl)
        dl_s[rows, :] = dl
        return carry

    lax.fori_loop(0, nc, local_body, 0)

    cn_s[...] = jnp.zeros_like(cn_s)
    m_s[...] = jnp.zeros_like(m_s)
    lane_r = lax.broadcasted_iota(jnp.int32, (1, ng), 1)

    def scan_issue(r0, d):
        rows = pl.ds(r0, ch)
        gr = gr_ref[0, :, rows]
        br = br_s[:, rows]
        tot = jnp.sum(_log_sigmoid(gr[ng:]), axis=1, keepdims=True)
        scal = []
        for hh in range(nh):
            fi = d * nh + hh
            sl = slice(hh * M_DH, (hh + 1) * M_DH)
            qcn_s[fi] = _dot(q_s[rows, sl], cn_s[fi].astype(BF16))
            m_old = m_s[fi][:, 0:1]
            b_e = tot[fi:fi + 1, :]
            d_end = b_e - br[fi:fi + 1, :] + gr[fi:fi + 1, :]
            m_end = jnp.max(d_end, axis=-1, keepdims=True)
            m_new = jnp.maximum(b_e + m_old, m_end)
            ktw = (kt_s[sl, rows].astype(F32) * jnp.exp(d_end - m_end)).astype(BF16)
            v1 = jnp.concatenate([vo_ref[0, rows, sl], ones_blk], axis=1)
            u_s[fi] = _dot(ktw, v1)
            scal.append((m_old, m_new, jnp.exp(b_e + m_old - m_new), jnp.exp(m_end - m_new)))
        return scal

    def scan_finish(r0, d, scal):
        rows = pl.ds(r0, ch)
        m_row = jnp.zeros((1, ng), F32)
        for hh in range(nh):
            m_row = jnp.where(lane_r == d * nh + hh, scal[hh][0], m_row)

        inter = bc_s[rows, :] + m_row
        ml = ml_s[rows, :]
        mt = jnp.maximum(inter, ml)
        a = jnp.exp(ml - mt)
        wi = jnp.exp(inter - mt)
        qn = jnp.zeros((ch, ng), F32)
        for hh in range(nh):
            fi = d * nh + hh
            qn = jnp.where(lane_c == fi, qcn_s[fi, :, M_DH:M_DH + ng], qn)
        den = a * dl_s[rows, :] + wi * qn
        rinv = 1.0 / jnp.maximum(jnp.abs(den), jnp.exp(-mt))
        c_loc = a * rinv
        c_int = wi * rinv

        for hh in range(nh):
            fi = d * nh + hh
            sl = slice(hh * M_DH, (hh + 1) * M_DH)
            h_s[d, rows, sl] = (c_loc[:, fi:fi + 1] * h_s[d, rows, sl]
                                + c_int[:, fi:fi + 1] * qcn_s[fi, :, :M_DH])
            cn_s[fi] = scal[hh][2] * cn_s[fi] + scal[hh][3] * u_s[fi]
            m_s[fi] = jnp.broadcast_to(scal[hh][1], (1, M_DH))

    def scan_body(s, carry):
        rf = pl.multiple_of(s * ch, ch)
        rb = pl.multiple_of(jnp.where(s < ncc, ncc - 1 - s, nc - 1 - s + ncc) * ch, ch)
        sf = scan_issue(rf, 0)
        sb = scan_issue(rb, 1)
        scan_finish(rf, 0, sf)
        scan_finish(rb, 1, sb)
        return carry

    lax.fori_loop(0, nc, scan_body, 0)

    mnorm = mnorm_ref[...]

    def out_body(j, carry):
        r0 = pl.multiple_of(j * ch, ch)
        hsum = h_s[0, pl.ds(r0, ch), :] + h_s[1, pl.ds(r0, ch), :]
        og = _sigmoid(vo_ref[0, pl.ds(r0, ch), M_WIDTH:].astype(F32))
        for hh in range(M_HEADS):
            sl = slice(hh * M_DH, (hh + 1) * M_DH)
            ya_ref[0, pl.ds(r0, ch), sl] = (_rms(hsum[:, sl], mnorm[:, sl]) * og[:, sl]).astype(BF16)
        return carry

    lax.fori_loop(0, nc, out_body, 0)


def _mlstm_call(qk, vo, gi, gf, gr, conv, mnorm, lc):
    b, l, _ = qk.shape
    ng = 2 * M_HEADS
    return pl.pallas_call(
        functools.partial(_mlstm_kernel, lc=lc),
        grid=(b,),
        in_specs=[pl.BlockSpec((1, l, 2 * M_WIDTH), lambda bi: (bi, 0, 0), pipeline_mode=pl.Buffered(1)),
                  pl.BlockSpec((1, l, 2 * M_WIDTH), lambda bi: (bi, 0, 0)),
                  pl.BlockSpec((1, l, ng), lambda bi: (bi, 0, 0)),
                  pl.BlockSpec((1, l, ng), lambda bi: (bi, 0, 0)),
                  pl.BlockSpec((1, 2 * ng, l), lambda bi: (bi, 0, 0)),
                  _const_spec(conv.shape), _const_spec(mnorm.shape)],
        out_specs=pl.BlockSpec((1, l, M_WIDTH), lambda bi: (bi, 0, 0)),
        out_shape=jax.ShapeDtypeStruct((b, l, M_WIDTH), BF16),
        scratch_shapes=[pltpu.VMEM((l, M_WIDTH), BF16),
                        pltpu.VMEM((M_WIDTH, l), BF16),
                        pltpu.VMEM((2, l, M_WIDTH), F32),
                        pltpu.VMEM((l, ng), F32),
                        pltpu.VMEM((l, ng), F32),
                        pltpu.VMEM((l, ng), F32),
                        pltpu.VMEM((ng, l), F32),
                        pltpu.VMEM((2 * M_HEADS, M_DH, 2 * M_DH), F32),
                        pltpu.VMEM((2 * M_HEADS, 1, M_DH), F32),
                        pltpu.VMEM((M_HEADS, M_CHUNK, M_CHUNK), F32),
                        pltpu.VMEM((2 * M_HEADS, M_CHUNK, M_CHUNK), BF16),
                        pltpu.VMEM((2 * M_HEADS, M_CHUNK, 2 * M_DH), F32),
                        pltpu.VMEM((2 * M_HEADS, M_DH, 2 * M_DH), F32)],
        compiler_params=_params("parallel"), name="mlstm",
    )(qk, vo, gi, gf, gr, conv, mnorm)


def _attn_kernel(q_ref, k_ref, v_ref, o_ref, *, lc, ctx_out):
    tq = q_ref.shape[1]
    l = k_ref.shape[1]
    qi = pl.program_id(2)
    nct = lc // tq
    lane = lax.broadcasted_iota(jnp.int32, (tq, 2 * A_VDIM), 1)

    def run(klen):
        outs = []
        for hh in range(A_HPS):
            sl = slice(hh * A_PAD, (hh + 1) * A_PAD)
            s = _dot_nt(q_ref[0, :, sl], k_ref[0, :klen, sl])
            p = jnp.exp2((s - jnp.max(s, axis=-1, keepdims=True)).astype(BF16))
            nd = _dot(p, v_ref[0, :klen, sl])
            outs.append(nd / pltpu.roll(nd, A_VDIM, 1))
        for pp in range(A_HPS // 2):
            o_ref[0, :, pp * A_PAD:(pp + 1) * A_PAD] = jnp.where(
                lane < A_VDIM, outs[2 * pp], pltpu.roll(outs[2 * pp + 1], A_VDIM, 1)).astype(BF16)

    @pl.when(qi >= nct)
    def _():
        run(l)

    @pl.when(qi < nct)
    def _():
        if ctx_out:
            run(lc)
        else:
            o_ref[...] = jnp.zeros_like(o_ref)


def _attn_call(q, k, v, lc, tq, ctx_out):
    b, l, _ = q.shape
    return pl.pallas_call(
        functools.partial(_attn_kernel, lc=lc, ctx_out=ctx_out),
        grid=(b, A_HEADS // A_HPS, l // tq),
        in_specs=[pl.BlockSpec((1, tq, A_HPS * A_PAD), lambda bi, p, qi: (bi, qi, p)),
                  pl.BlockSpec((1, l, A_HPS * A_PAD), lambda bi, p, qi: (bi, 0, p)),
                  pl.BlockSpec((1, l, A_HPS * A_PAD), lambda bi, p, qi: (bi, 0, p))],
        out_specs=pl.BlockSpec((1, tq, A_HPS * A_VDIM), lambda bi, p, qi: (bi, qi, p)),
        out_shape=jax.ShapeDtypeStruct((b, l, A_WIDTH), BF16),
        compiler_params=_params("parallel", "parallel", "arbitrary"), name="attn",
    )(q, k, v)


def _merge_kernel(x_ref, mod_ref, ya_ref, yb_ref, yc_ref, br_ref, wpa_ref, wpb_ref, wpc_ref, wout_ref,
                  n2_ref, r_ref, rb_ref, o_ref, h2_o, comb_o, idx_o, idxt_o, cnt_o):
    d = x_ref.shape[2]
    br = br_ref[0]
    mod = mod_ref[0]
    y = (br[:, :d].astype(F32) * _dot(ya_ref[0], wpa_ref[...])
         + br[:, d:2 * d].astype(F32) * _dot(yb_ref[0], wpb_ref[...])
         + br[:, 2 * d:].astype(F32) * _dot(yc_ref[0], wpc_ref[...]))
    out = _dot(y.astype(BF16), wout_ref[...])
    x = x_ref[0] + mod[2:3] * out
    o_ref[0] = x
    _route(x, mod, n2_ref, r_ref, rb_ref, h2_o, comb_o, idx_o, idxt_o, cnt_o)


def _merge_call(xs, mod, ya, yb, yc, br, w, n2, r3, rb, lc, tm):
    b, l, d = xs.shape
    nct = lc // tm
    tok = lambda width: pl.BlockSpec((1, tm, width), lambda bi, j: (bi, j, 0))
    consts = [w["wpa"], w["wpb"], w["wpc"], w["wout"], n2, r3, rb]
    return pl.pallas_call(
        _merge_kernel, grid=(b, l // tm),
        in_specs=[tok(d), pl.BlockSpec((1, 6, d), lambda bi, j: (jnp.where(j < nct, b, bi), 0, 0)),
                  tok(M_WIDTH), tok(A_WIDTH), tok(G_WIDTH), tok(3 * d)] + [_const_spec(a.shape) for a in consts],
        out_specs=[tok(d), tok(d), tok(N_EXPERTS), tok(8), pl.BlockSpec((1, 8, tm), lambda bi, j: (bi, 0, j)),
                   pl.BlockSpec((1, 1, 8, R_PAD), lambda bi, j: (bi, j, 0, 0))],
        out_shape=[jax.ShapeDtypeStruct((b, l, d), F32),
                   jax.ShapeDtypeStruct((b, l, d), BF16), jax.ShapeDtypeStruct((b, l, N_EXPERTS), F32),
                   jax.ShapeDtypeStruct((b, l, 8), jnp.int32), jax.ShapeDtypeStruct((b, 8, l), jnp.int32),
                   jax.ShapeDtypeStruct((b, l // tm, 8, R_PAD), F32)],
        compiler_params=_params("parallel", "parallel"), name="merge",
    )(xs, mod, ya, yb, yc, br, *consts)


def _route(x, mod, n2_ref, r_ref, rb_ref, h2_o, comb_o, idx_o, idxt_o, cnt_o):
    tm = x.shape[0]
    h2 = _rms(x, n2_ref[...]) * (1.0 + mod[4:5]) + mod[3:4]
    h2_o[0] = h2.astype(BF16)

    r = r_ref[...]
    pp = sum(_dot(piece, r) for piece in _split3(h2))
    logits = pp + pltpu.roll(pp, R_PAD - R_SEG, 1) + pltpu.roll(pp, R_PAD - 2 * R_SEG, 1) + rb_ref[...]
    el = logits[:, :N_EXPERTS]
    gl = logits[:, N_EXPERTS:N_EXPERTS + N_GROUPS]
    big = 1e9

    lane_g = lax.broadcasted_iota(jnp.int32, (tm, N_GROUPS), 1).astype(F32)
    gmax = jnp.max(gl, axis=-1, keepdims=True)
    g_sel = jnp.min(jnp.where(gl == gmax, lane_g, big), axis=-1, keepdims=True)
    g_prob = 1.0 / jnp.sum(jnp.exp(gl - gmax), axis=-1, keepdims=True)

    lane_i = lax.broadcasted_iota(jnp.int32, (tm, N_EXPERTS), 1)
    lane_e = lane_i.astype(F32)
    lane_grp = (lane_i // EXP_PER_GROUP).astype(F32)
    v1 = jnp.where(lane_grp == g_sel, el, -jnp.inf)
    t1 = jnp.max(v1, axis=-1, keepdims=True)
    i1 = jnp.min(jnp.where(v1 == t1, lane_e, big), axis=-1, keepdims=True)
    v2 = jnp.where(lane_e == i1, -jnp.inf, v1)
    t2 = jnp.max(v2, axis=-1, keepdims=True)
    i2 = jnp.min(jnp.where(v2 == t2, lane_e, big), axis=-1, keepdims=True)
    e21 = jnp.exp(t2 - t1)
    w1 = 1.0 / (1.0 + e21)
    w2 = e21 * w1
    comb_o[0] = (jnp.where(lane_e == i1, w1, 0.0) + jnp.where(lane_e == i2, w2, 0.0)) * g_prob

    lane_p = lax.broadcasted_iota(jnp.int32, (tm, R_PAD), 1)
    onehot = jnp.where(lane_p.astype(F32) == g_sel, 1.0, 0.0)
    ri = lax.broadcasted_iota(jnp.int32, (tm, tm), 0)
    ci = lax.broadcasted_iota(jnp.int32, (tm, tm), 1)
    before = jnp.where(ri > ci, 1.0, 0.0).astype(BF16)
    rank = jnp.sum(_dot(before, onehot.astype(BF16)) * onehot, axis=-1, keepdims=True)
    cnt_o[0, 0] = jnp.broadcast_to(jnp.sum(onehot, axis=0, keepdims=True), (8, R_PAD))
    fields = jnp.where(lane_p == 0, g_sel, jnp.where(lane_p == 1, rank, 0.0))
    idx_o[0] = fields[:, :8].astype(jnp.int32)
    idxt_o[0] = fields.T[:8, :].astype(jnp.int32)


def _experts_kernel(st_ref, h2_ref, comb_ref, idx_ref, idxt_ref, w1_ref, w3_ref, w2_ref, o_ref, hs_s, cs_s, *, tb):
    l, d = h2_ref.shape[1], h2_ref.shape[2]
    nblk = l // tb
    ch = MOE_CHUNK
    bi = pl.program_id(0)
    g = pl.program_id(1)

    @pl.when(g == 0)
    def _():
        o_ref[...] = jnp.zeros_like(o_ref)

    base = (bi * N_GROUPS + g) * (nblk + 1)
    cnt = st_ref[base + nblk]
    sub_iota = lax.broadcasted_iota(jnp.int32, (ch, tb), 0)
    lane_iota = lax.broadcasted_iota(jnp.int32, (tb, ch), 1)
    lane_e = lax.broadcasted_iota(jnp.int32, (ch, N_EXPERTS), 1)

    def chunk_body(c, carry):
        lo = c * ch
        hs_s[...] = jnp.zeros_like(hs_s)
        cs_s[...] = jnp.zeros_like(cs_s)
        for k in range(nblk):
            s_k = st_ref[base + k]
            e_k = st_ref[base + k + 1]
            rows = slice(k * tb, (k + 1) * tb)

            @pl.when(jnp.logical_and(s_k < lo + ch, e_k > lo))
            def _(s_k=s_k, rows=rows):
                it = idxt_ref[0, :, rows]
                pos = jnp.where(it[0:1] == g, it[1:2] + (s_k - lo), -1)
                p = jnp.where(sub_iota == pos, 1.0, 0.0).astype(BF16)
                hs_s[...] += _dot(p, h2_ref[0, rows, :])
                cs_s[...] += sum(_dot(p, piece) for piece in _split3(comb_ref[0, rows, :]))

        hsb = hs_s[...].astype(BF16)
        cs = cs_s[...]
        y = jnp.zeros((ch, d), F32)
        for e in range(EXP_PER_GROUP):
            ce = jnp.sum(jnp.where(lane_e == g * EXP_PER_GROUP + e, cs, 0.0), axis=-1, keepdims=True)
            hid = (_silu(_dot(hsb, w1_ref[e])) * _dot(hsb, w3_ref[e]) * ce).astype(BF16)
            y = y + _dot(hid, w2_ref[e])
        yb = y.astype(BF16)

        for k in range(nblk):
            s_k = st_ref[base + k]
            e_k = st_ref[base + k + 1]
            rows = slice(k * tb, (k + 1) * tb)

            @pl.when(jnp.logical_and(s_k < lo + ch, e_k > lo))
            def _(s_k=s_k, rows=rows):
                ic = idx_ref[0, rows, :]
                pos = jnp.where(ic[:, 0:1] == g, ic[:, 1:2] + (s_k - lo), -1)
                q = jnp.where(lane_iota == pos, 1.0, 0.0).astype(BF16)
                o_ref[0, rows, :] = (o_ref[0, rows, :].astype(F32) + _dot(q, yb)).astype(BF16)

        return carry

    lax.fori_loop(0, (cnt + ch - 1) // ch, chunk_body, 0)


def _experts_call(starts, h2, comb, idx, idxt, w1, w3, w2, tb):
    b, l, d = h2.shape
    whole = lambda width: pl.BlockSpec((1, l, width), lambda bi, g, st: (bi, 0, 0))
    grid_spec = pltpu.PrefetchScalarGridSpec(
        num_scalar_prefetch=1, grid=(b, N_GROUPS),
        in_specs=[whole(d), whole(N_EXPERTS), whole(8), pl.BlockSpec((1, 8, l), lambda bi, g, st: (bi, 0, 0)),
                  pl.BlockSpec((EXP_PER_GROUP, d, D_EXPERT), lambda bi, g, st: (g, 0, 0)),
                  pl.BlockSpec((EXP_PER_GROUP, d, D_EXPERT), lambda bi, g, st: (g, 0, 0)),
                  pl.BlockSpec((EXP_PER_GROUP, D_EXPERT, d), lambda bi, g, st: (g, 0, 0))],
        out_specs=whole(d),
        scratch_shapes=[pltpu.VMEM((MOE_CHUNK, d), F32), pltpu.VMEM((MOE_CHUNK, N_EXPERTS), F32)])
    return pl.pallas_call(
        functools.partial(_experts_kernel, tb=tb), grid_spec=grid_spec,
        out_shape=jax.ShapeDtypeStruct((b, l, d), BF16),
        compiler_params=_params("parallel", "arbitrary"), name="experts",
    )(starts, h2, comb, idx, idxt, w1, w3, w2)


def _group_starts(cnt):
    c = cnt[:, :, 0, :N_GROUPS].astype(jnp.int32)
    s = jnp.cumsum(c, axis=1)
    s = jnp.concatenate([jnp.zeros_like(s[:, :1]), s], axis=1)
    return jnp.transpose(s, (0, 2, 1)).reshape(-1)


def _final_kernel(x_ref, mod_ref, f_ref, g_ref, o_ref):
    o_ref[0] = _rms(x_ref[0] + mod_ref[0][5:6] * f_ref[0].astype(F32), g_ref[...])


def _final_call(xs, mod, f, g, lc, tm):
    b, l, d = xs.shape
    off = lc // tm
    lat = pl.BlockSpec((1, tm, d), lambda bi, j: (bi, j + off, 0))
    return pl.pallas_call(
        _final_kernel, grid=(b, (l - lc) // tm),
        in_specs=[lat, pl.BlockSpec((1, 6, d), lambda bi, j: (bi, 0, 0)), lat, _const_spec(g.shape)],
        out_specs=pl.BlockSpec((1, tm, d), lambda bi, j: (bi, j, 0)),
        out_shape=jax.ShapeDtypeStruct((b, l - lc, d), F32),
        compiler_params=_params("parallel", "parallel"), name="final_norm",
    )(xs, mod, f, g)


def _rope_tables(t_len, lc):
    half = A_ROPE // 2
    rows = t_len // GRID_W
    r = jnp.repeat(jnp.arange(rows, dtype=F32), GRID_W)
    col = jnp.tile(jnp.arange(GRID_W, dtype=F32), rows)
    inv = ROPE_THETA ** (-jnp.arange(0, half, 2, dtype=F32) / half)
    ang = jnp.concatenate([r[:, None] * inv, col[:, None] * inv], axis=-1)
    cos = jnp.concatenate([jnp.ones((lc, half), F32), jnp.cos(ang)], axis=0)
    sin = jnp.concatenate([jnp.zeros((lc, half), F32), jnp.sin(ang)], axis=0)
    l = lc + t_len
    ones = jnp.ones((l, A_NOPE), F32)
    zeros = jnp.zeros((l, A_NOPE), F32)
    tail1 = jnp.ones((l, A_PAD - A_NOPE - A_ROPE), F32)
    tail0 = jnp.zeros((l, A_PAD - A_NOPE - A_ROPE), F32)
    zh = jnp.zeros((l, half), F32)
    cos_t = jnp.concatenate([ones, cos, cos, tail1], axis=-1)
    sina_t = jnp.concatenate([zeros, zh, sin, tail0], axis=-1)
    sinb_t = jnp.concatenate([zeros, -sin, zh, tail0], axis=-1)
    return cos_t, sina_t, sinb_t


def _layer_weights(l, w_in, m_gate_b, a_qnorm, a_wuq, a_kvnorm, a_wukv, g_ws, g_bs, g_vnorm,
                   w_pa, w_pb, w_pc, w_out):
    d = w_in.shape[1]
    wi = w_in[l]
    o = 0

    def take(n):
        nonlocal o
        s = wi[:, o:o + n]
        o += n
        return s

    mq, mk, mv, mo, mg = take(M_WIDTH), take(M_WIDTH), take(M_WIDTH), take(M_WIDTH), take(4 * M_HEADS)
    aq, akv, akr = take(A_QRANK), take(A_KVRANK), take(A_ROPE)
    gu, gv = take(G_WIDTH), take(G_WIDTH)
    br = take(3 * d)
    nh = M_HEADS
    gb = m_gate_b[l]
    mgo = jnp.concatenate([mg[:, :nh], mg[:, 2 * nh:3 * nh], mg[:, nh:2 * nh], mg[:, 3 * nh:]], axis=1)
    gbo = jnp.concatenate([gb[:nh], gb[2 * nh:3 * nh], gb[nh:2 * nh], gb[3 * nh:]])
    akr_pad = jnp.concatenate([jnp.zeros((d, A_NOPE), F32), akr,
                               jnp.zeros((d, A_PAD - A_NOPE - A_ROPE), F32)], axis=1)
    wuq = a_wuq[l].reshape(A_QRANK, A_HEADS, A_NOPE + A_ROPE)
    wuq = jnp.pad(wuq, ((0, 0), (0, 0), (0, A_PAD - A_NOPE - A_ROPE))).reshape(A_QRANK, A_HEADS * A_PAD)
    wukv = a_wukv[l].reshape(A_KVRANK, A_HEADS, A_NOPE + A_VDIM)
    wuk = jnp.pad(wukv[:, :, :A_NOPE], ((0, 0), (0, 0), (0, A_PAD - A_NOPE))).reshape(A_KVRANK, A_HEADS * A_PAD)
    wuv = jnp.pad(wukv[:, :, A_NOPE:], ((0, 0), (0, 0), (0, A_PAD - A_VDIM))).reshape(A_KVRANK, A_HEADS * A_PAD)
    vone = jnp.tile(jnp.concatenate([jnp.zeros((A_VDIM,), F32), jnp.ones((A_PAD - A_VDIM,), F32)]),
                    A_HEADS).reshape(1, A_HEADS * A_PAD)
    gbs = jnp.repeat(g_bs[l].T, G_DG, axis=1)
    return dict(
        wqk=jnp.concatenate([mq, mk], 1).astype(BF16), wvo=jnp.concatenate([mv, mo], 1).astype(BF16),
        wgi=mgo[:, :2 * nh].astype(BF16), wgf=mgo[:, 2 * nh:].astype(BF16), wgr=mgo.T.astype(BF16),
        gbi=gbo[:2 * nh].reshape(1, -1), gbf=gbo[2 * nh:].reshape(1, -1), gbr=gbo.reshape(-1, 1),
        wa=jnp.concatenate([aq, akv, akr_pad], 1).astype(BF16),
        wg=jnp.concatenate([gu, gv], 1).astype(BF16), wbr=br.astype(BF16),
        aqn=a_qnorm[l].reshape(1, -1), akvn=a_kvnorm[l].reshape(1, -1),
        wuq=wuq.astype(BF16), wuk=wuk.astype(BF16), wuv=wuv.astype(BF16), vone=vone,
        gvn=g_vnorm[l].reshape(1, -1), gws=g_ws[l].astype(BF16), gbs=gbs,
        wpa=w_pa[l].astype(BF16), wpb=w_pb[l].astype(BF16), wpc=w_pc[l].astype(BF16),
        wout=w_out[l].astype(BF16))


def _router_weights(r_group, r_group_b, r_expert, r_expert_b):
    d = r_group.shape[0]
    pad = R_SEG - N_EXPERTS - N_GROUPS
    r = jnp.concatenate([r_expert, r_group, jnp.zeros((d, pad), F32)], axis=1)
    r3 = jnp.concatenate(list(_split3(r)) + [jnp.zeros((d, R_PAD - 3 * R_SEG), BF16)], axis=1)
    rb = jnp.concatenate([r_expert_b, r_group_b, jnp.zeros((R_PAD - N_EXPERTS - N_GROUPS,), F32)])
    return r3, rb.reshape(1, R_PAD)


def _tile(n, lc, candidates):
    for t in candidates:
        if n % t == 0 and lc % t == 0:
            return t
    raise ValueError("sequence lengths must be multiples of 128")


def kernel(x, c, ctx, c_ctx, w_ada, b_ada, norm1, norm2, final_norm, w_in, m_conv, m_gate_b, m_norm, a_qnorm, a_wuq, a_kvnorm, a_wukv, g_ws, g_bs, g_vnorm, w_pa, w_pb, w_pc, w_out, r_group, r_group_b, r_expert, r_expert_b, e_w1, e_w3, e_w2):
    b, t_len, d = x.shape
    lc = ctx.shape[1]
    l = lc + t_len
    depth = w_in.shape[0]
    tm = _tile(l, lc, (256, 128))

    xs = jnp.concatenate([ctx, x], axis=1)
    cv = jnp.concatenate([c, c_ctx[None, :]], axis=0)
    mod_all = _ada_call(cv, w_ada, b_ada).reshape(depth, b + 1, 6, d)
    tabs = _rope_tables(t_len, lc)

    f = None
    mod_prev = None
    for li in range(depth):
        last = li == depth - 1
        mod = mod_all[li]
        w = _layer_weights(li, w_in, m_gate_b, a_qnorm, a_wuq, a_kvnorm, a_wukv, g_ws, g_bs, g_vnorm,
                           w_pa, w_pb, w_pc, w_out)
        outs = _inproj_call(xs, mod, norm1[li].reshape(1, d), w, tabs, lc, tm, f, mod_prev)
        qk, vo, gi, gf, gr, q, k, v, yc, br = outs[:10]
        if f is not None:
            xs = outs[10]
        ya = _mlstm_call(qk, vo, gi, gf, gr, m_conv[li], m_norm[li].reshape(1, -1), lc)
        yb = _attn_call(q, k, v, lc, tm, not last)
        r3, rb = _router_weights(r_group[li], r_group_b[li], r_expert[li], r_expert_b[li])
        xs, h2, comb, idx, idxt, cnt = _merge_call(xs, mod, ya, yb, yc, br, w, norm2[li].reshape(1, d),
                                                   r3, rb, lc, tm)
        f = _experts_call(_group_starts(cnt), h2, comb, idx, idxt, e_w1[li].astype(BF16),
                          e_w3[li].astype(BF16), e_w2[li].astype(BF16), tm)
        mod_prev = mod
    return _final_call(xs, mod_prev, f, final_norm.reshape(1, d), lc, tm)
```

```python
import functools

import jax
import jax.numpy as jnp
from jax import lax
from jax.experimental import pallas as pl
from jax.experimental.pallas import tpu as pltpu

F32 = jnp.float32
BF16 = jnp.bfloat16

EPS = 1e-6
GRID_W = 64
ROPE_THETA = 10000.0

M_HEADS = 4
M_DH = 128
M_WIDTH = M_HEADS * M_DH
M_CHUNK = 128
G_PAD = 128

A_HEADS = 8
A_NOPE = 64
A_ROPE = 32
A_VDIM = 64
A_QRANK = 384
A_KVRANK = 256
A_WIDTH = A_HEADS * A_VDIM
A_PAD = 128
A_HPS = 4
ATT_SCALE = (A_NOPE + A_ROPE) ** -0.5
LOG2E = 1.4426950408889634

G_GROUPS = 4
G_CHUNK = 128
G_WIDTH = 512
G_DG = G_WIDTH // G_GROUPS

N_GROUPS = 4
EXP_PER_GROUP = 4
N_EXPERTS = N_GROUPS * EXP_PER_GROUP
D_EXPERT = 512
R_PAD = 128
R_SEG = 32
MOE_CHUNK = 256

VMEM_LIMIT = 56 * 1024 * 1024


def _dot(a, b):
    return jnp.dot(a, b, preferred_element_type=F32)


def _dot_nt(a, b):
    return lax.dot_general(a, b, (((1,), (1,)), ((), ())), preferred_element_type=F32)


def _dot_tn(a, b):
    return lax.dot_general(a, b, (((0,), (0,)), ((), ())), preferred_element_type=F32)


def _split3(x):
    hi = x.astype(BF16)
    r = x - hi.astype(F32)
    mid = r.astype(BF16)
    lo = (r - mid.astype(F32)).astype(BF16)
    return hi, mid, lo


def _sigmoid(x):
    return 1.0 / (1.0 + jnp.exp(-x))


def _silu(x):
    return x * _sigmoid(x)


def _log_sigmoid(x):
    return jnp.minimum(x, 0.0) - jnp.log1p(jnp.exp(-jnp.abs(x)))


def _gelu(x):
    return 0.5 * x * (1.0 + lax.erf(x * (2.0 ** -0.5)))


def _rms(x, g):
    return x * lax.rsqrt(jnp.mean(x * x, axis=-1, keepdims=True) + EPS) * g


def _params(*sem):
    return pltpu.CompilerParams(dimension_semantics=sem, vmem_limit_bytes=VMEM_LIMIT)


def _const_spec(shape):
    nd = len(shape)
    return pl.BlockSpec(shape, lambda *_: (0,) * nd, pipeline_mode=pl.Buffered(1))


def _ada_kernel(cv_ref, w_ref, b_ref, o_ref):
    s = _silu(cv_ref[...])
    o_ref[0] = _dot(s.astype(BF16), w_ref[0].astype(BF16)) + b_ref[0]


def _ada_call(cv, w_ada, b_ada):
    depth, d, n6 = w_ada.shape
    rows = cv.shape[0]
    tn = n6 // 4
    return pl.pallas_call(
        _ada_kernel,
        grid=(depth, n6 // tn),
        in_specs=[pl.BlockSpec((rows, d), lambda l, j: (0, 0)),
                  pl.BlockSpec((1, d, tn), lambda l, j: (l, 0, j)),
                  pl.BlockSpec((1, 1, tn), lambda l, j: (l, 0, j))],
        out_specs=pl.BlockSpec((1, rows, tn), lambda l, j: (l, 0, j)),
        out_shape=jax.ShapeDtypeStruct((depth, rows, n6), F32),
        compiler_params=_params("parallel", "parallel"),
        name="ada",
    )(cv, w_ada, b_ada.reshape(depth, 1, n6))


def _inproj_kernel(*refs, has_f):
    if has_f:
        f_ref, modp_ref, x_o = refs[0], refs[1], refs[-1]
        refs = refs[2:-1]
    (x_ref, mod_ref, n1_ref, wqk_ref, wvo_ref, wgt_ref, gbt_ref,
     wa_ref, wg_ref, wbr_ref, aqn_ref, akvn_ref, wuq_ref, wuk_ref, wuv_ref, vone_ref,
     cos_ref, sina_ref, sinb_ref, gvn_ref, gws_ref, gbs_ref,
     qk_o, vo_o, gi_o, gf_o, gr_o, q_o, k_o, v_o, yc_o, br_o) = refs
    tm = x_ref.shape[1]
    mod = mod_ref[0]
    x = x_ref[0]
    if has_f:
        x = x + modp_ref[0][5:6] * f_ref[0].astype(F32)
        x_o[0] = x
    h = _rms(x, n1_ref[...]) * (1.0 + mod[1:2]) + mod[0:1]
    hb = h.astype(BF16)

    qk_o[0] = _dot(hb, wqk_ref[...])
    vo_o[0] = _dot(hb, wvo_ref[...]).astype(BF16)
    ng = gi_o.shape[2]
    gates = _dot(hb, wgt_ref[...]) + gbt_ref[...]
    gi_o[0] = gates[:, :ng]
    gf_o[0] = pltpu.roll(gates, gates.shape[1] - ng, 1)[:, :ng]
    gr_o[0] = gates.T[:2 * ng, :]

    za = _dot(hb, wa_ref[...])
    aqn = _rms(za[:, :A_QRANK], aqn_ref[...]).astype(BF16)
    akvn = _rms(za[:, A_QRANK:A_QRANK + A_KVRANK], akvn_ref[...]).astype(BF16)
    cos = cos_ref[...]
    sina = sina_ref[...]
    sinb = sinb_ref[...]
    half = A_ROPE // 2

    def rope(t):
        return t * cos + pltpu.roll(t, half, 1) * sina + pltpu.roll(t, A_PAD - half, 1) * sinb

    kr = rope(za[:, A_QRANK + A_KVRANK:])
    qp = _dot(aqn, wuq_ref[...])
    kp = _dot(akvn, wuk_ref[...])
    for hh in range(A_HEADS):
        sl = slice(hh * A_PAD, (hh + 1) * A_PAD)
        q_o[0, :, sl] = (rope(qp[:, sl]) * (ATT_SCALE * LOG2E)).astype(BF16)
        k_o[0, :, sl] = (kp[:, sl] + kr).astype(BF16)
    v_o[0] = (_dot(akvn, wuv_ref[...]) + vone_ref[...]).astype(BF16)

    zg = _dot(hb, wg_ref[...])
    gu = _gelu(zg[:, :G_WIDTH])
    gv = _gelu(zg[:, G_WIDTH:])
    gvn = gvn_ref[...]
    bias = gbs_ref[...]
    for g in range(G_GROUPS):
        sl = slice(g * G_DG, (g + 1) * G_DG)
        xn = _rms(gv[:, sl], gvn[:, sl]).astype(BF16)
        ws = gws_ref[g]
        for ci in range(tm // G_CHUNK):
            r = slice(ci * G_CHUNK, (ci + 1) * G_CHUNK)
            sg = _dot(ws, xn[r]) + bias[:, sl]
            yc_o[0, r, sl] = (gu[r, sl] * sg).astype(BF16)

    br_o[0] = _sigmoid(_dot(hb, wbr_ref[...])).astype(BF16)


def _inproj_call(xs, mod, n1, w, tabs, lc, tm, f=None, mod_prev=None):
    b, l, d = xs.shape
    nct = lc // tm
    has_f = f is not None
    tok = lambda width: pl.BlockSpec((1, tm, width), lambda bi, j: (bi, j, 0))
    modspec = pl.BlockSpec((1, 6, d), lambda bi, j: (jnp.where(j < nct, b, bi), 0, 0))
    tab = pl.BlockSpec((tm, A_PAD), lambda bi, j: (j, 0))
    consts = [n1, w["wqk"], w["wvo"], w["wgt"], w["gbt"], w["wa"], w["wg"], w["wbr"],
              w["aqn"], w["akvn"], w["wuq"], w["wuk"], w["wuv"], w["vone"]]
    consts2 = [w["gvn"], w["gws"], w["gbs"]]
    in_specs = ([tok(d), modspec] + [_const_spec(a.shape) for a in consts] + [tab, tab, tab]
                + [_const_spec(a.shape) for a in consts2])
    args = [xs, mod, *consts, *tabs, *consts2]
    ng = 4 * M_HEADS
    out_shape = [jax.ShapeDtypeStruct((b, l, 2 * M_WIDTH), F32),
                 jax.ShapeDtypeStruct((b, l, 2 * M_WIDTH), BF16),
                 jax.ShapeDtypeStruct((b, l, ng // 2), F32),
                 jax.ShapeDtypeStruct((b, l, ng // 2), F32),
                 jax.ShapeDtypeStruct((b, ng, l), F32),
                 jax.ShapeDtypeStruct((b, l, A_HEADS * A_PAD), BF16),
                 jax.ShapeDtypeStruct((b, l, A_HEADS * A_PAD), BF16),
                 jax.ShapeDtypeStruct((b, l, A_HEADS * A_PAD), BF16),
                 jax.ShapeDtypeStruct((b, l, G_WIDTH), BF16),
                 jax.ShapeDtypeStruct((b, l, 3 * d), BF16)]
    out_specs = [tok(2 * M_WIDTH), tok(2 * M_WIDTH), tok(ng // 2), tok(ng // 2),
                 pl.BlockSpec((1, ng, tm), lambda bi, j: (bi, 0, j)),
                 tok(A_HEADS * A_PAD), tok(A_HEADS * A_PAD), tok(A_HEADS * A_PAD), tok(G_WIDTH), tok(3 * d)]
    if has_f:
        in_specs = [tok(d), modspec] + in_specs
        args = [f, mod_prev] + args
        out_shape.append(jax.ShapeDtypeStruct((b, l, d), F32))
        out_specs.append(tok(d))
    return pl.pallas_call(
        functools.partial(_inproj_kernel, has_f=has_f), grid=(b, l // tm), in_specs=in_specs,
        out_specs=out_specs, out_shape=out_shape,
        compiler_params=_params("parallel", "parallel"), name="inproj",
    )(*args)


def _scan(x, op, fill, axis, reverse):
    n = x.shape[axis]
    idx = lax.broadcasted_iota(jnp.int32, x.shape, axis)
    k = 1
    while k < n:
        if reverse:
            x = op(x, jnp.where(idx >= n - k, fill, pltpu.roll(x, n - k, axis)))
        else:
            x = op(x, jnp.where(idx < k, fill, pltpu.roll(x, k, axis)))
        k *= 2
    return x


def _mlstm_kernel(qk_ref, vo_ref, gi_ref, gf_ref, gr_ref, conv_ref, mnorm_ref, ya_ref,
                  q_s, kt_s, h_s, bc_s, ml_s, dl_s, br_s, cn_s, m_s, s_s, p_s, qcn_s, u_s, *, lc):
    l = qk_ref.shape[1]
    ch = M_CHUNK
    nc = l // ch
    ncc = lc // ch
    nh = M_HEADS
    ng = 2 * nh
    w = conv_ref[...]
    row = lax.broadcasted_iota(jnp.int32, (ch, 1), 0)

    def conv_body(j, carry):
        r0 = pl.multiple_of(j * ch, ch)
        cur = qk_ref[0, pl.ds(r0, ch), :]
        prev8 = qk_ref[0, pl.ds(pl.multiple_of(jnp.maximum(r0 - 8, 0), 8), 8), :]
        next8 = qk_ref[0, pl.ds(pl.multiple_of(jnp.minimum(r0 + ch, l - 8), 8), 8), :]
        seg_start = jnp.logical_or(j == 0, j == ncc)
        seg_end = jnp.logical_or(j == ncc - 1, j == nc - 1)
        pe = jnp.where(seg_start, 0.0, prev8[7:8, :])
        ne = jnp.where(seg_end, 0.0, next8[0:1, :])
        xp = jnp.where(row == 0, pe, pltpu.roll(cur, 1, 0))
        xn = jnp.where(row == ch - 1, ne, pltpu.roll(cur, ch - 1, 0))
        y = _silu(xp * w[0:1] + cur * w[1:2] + xn * w[2:3])
        q_s[pl.ds(r0, ch), :] = (y[:, :M_WIDTH] * (M_DH ** -0.5)).astype(BF16)
        kt_s[:, pl.ds(r0, ch)] = y[:, M_WIDTH:].T.astype(BF16)
        return carry

    lax.fori_loop(0, nc, conv_body, 0)

    ri = lax.broadcasted_iota(jnp.int32, (ch, ch), 0)
    ci = lax.broadcasted_iota(jnp.int32, (ch, ch), 1)
    lower = ri >= ci
    upper = ri <= ci
    ones_blk = jnp.ones((ch, M_DH), BF16)
    fwd_c = lax.broadcasted_iota(jnp.int32, (ch, ng), 1) < nh
    fwd_r = lax.broadcasted_iota(jnp.int32, (ng, ch), 0) < nh
    lane_c = lax.broadcasted_iota(jnp.int32, (ch, ng), 1)

    def local_body(j, carry):
        r0 = pl.multiple_of(j * ch, ch)
        rows = pl.ds(r0, ch)
        lfc = _log_sigmoid(gf_ref[0, rows, :])
        gr = gr_ref[0, :, rows]
        lfr = _log_sigmoid(gr[ng:])
        pre_c = _scan(lfc, jnp.add, 0.0, 0, False)
        pre_r = _scan(lfr, jnp.add, 0.0, 1, False)
        b_c = jnp.where(fwd_c, pre_c, jnp.sum(lfc, axis=0, keepdims=True) + lfc - pre_c)
        b_r = jnp.where(fwd_r, pre_r, jnp.sum(lfr, axis=1, keepdims=True) + lfr - pre_r)
        g_c = gi_ref[0, rows, :] - b_c
        g_r = gr[:ng] - b_r
        cg_c = jnp.where(fwd_c, _scan(g_c, jnp.maximum, -jnp.inf, 0, False),
                         _scan(g_c, jnp.maximum, -jnp.inf, 0, True))
        bc_s[rows, :] = b_c
        br_s[:, rows] = b_r
        ml_s[rows, :] = b_c + cg_c
        dl = jnp.zeros((ch, ng), F32)
        for hh in range(nh):
            sl = slice(hh * M_DH, (hh + 1) * M_DH)
            s_s[hh] = _dot(q_s[rows, sl], kt_s[sl, rows])
        for hh in range(nh):
            s = s_s[hh]
            for d in range(2):
                jj = d * nh + hh
                wgt = jnp.exp(jnp.where(upper if d else lower, g_r[jj:jj + 1, :] - cg_c[:, jj:jj + 1], -jnp.inf))
                p_s[jj] = (s * wgt).astype(BF16)
        for hh in range(nh):
            sl = slice(hh * M_DH, (hh + 1) * M_DH)
            v1 = jnp.concatenate([vo_ref[0, rows, sl], ones_blk], axis=1)
            for d in range(2):
                jj = d * nh + hh
                nd = _dot(p_s[jj], v1)
                h_s[d, rows, sl] = nd[:, :M_DH]
                dl = jnp.where(lane_c == jj, nd[:, M_DH:M_DH + ng], dl)
        dl_s[rows, :] = dl
        return carry

    lax.fori_loop(0, nc, local_body, 0)

    cn_s[...] = jnp.zeros_like(cn_s)
    m_s[...] = jnp.zeros_like(m_s)
    lane_r = lax.broadcasted_iota(jnp.int32, (1, ng), 1)

    def scan_issue(r0, d):
        rows = pl.ds(r0, ch)
        gr = gr_ref[0, :, rows]
        br = br_s[:, rows]
        tot = jnp.sum(_log_sigmoid(gr[ng:]), axis=1, keepdims=True)
        scal = []
        for hh in range(nh):
            fi = d * nh + hh
            sl = slice(hh * M_DH, (hh + 1) * M_DH)
            qcn_s[fi] = _dot(q_s[rows, sl], cn_s[fi].astype(BF16))
            m_old = m_s[fi][:, 0:1]
            b_e = tot[fi:fi + 1, :]
            d_end = b_e - br[fi:fi + 1, :] + gr[fi:fi + 1, :]
            m_end = jnp.max(d_end, axis=-1, keepdims=True)
            m_new = jnp.maximum(b_e + m_old, m_end)
            ktw = (kt_s[sl, rows].astype(F32) * jnp.exp(d_end - m_end)).astype(BF16)
            v1 = jnp.concatenate([vo_ref[0, rows, sl], ones_blk], axis=1)
            u_s[fi] = _dot(ktw, v1)
            scal.append((m_old, m_new, jnp.exp(b_e + m_old - m_new), jnp.exp(m_end - m_new)))
        return scal

    def scan_finish(r0, d, scal):
        rows = pl.ds(r0, ch)
        m_row = jnp.zeros((1, ng), F32)
        for hh in range(nh):
            m_row = jnp.where(lane_r == d * nh + hh, scal[hh][0], m_row)

        inter = bc_s[rows, :] + m_row
        ml = ml_s[rows, :]
        mt = jnp.maximum(inter, ml)
        a = jnp.exp(ml - mt)
        wi = jnp.exp(inter - mt)
        qn = jnp.zeros((ch, ng), F32)
        for hh in range(nh):
            fi = d * nh + hh
            qn = jnp.where(lane_c == fi, qcn_s[fi, :, M_DH:M_DH + ng], qn)
        den = a * dl_s[rows, :] + wi * qn
        rinv = 1.0 / jnp.maximum(jnp.abs(den), jnp.exp(-mt))
        c_loc = a * rinv
        c_int = wi * rinv

        for hh in range(nh):
            fi = d * nh + hh
            sl = slice(hh * M_DH, (hh + 1) * M_DH)
            h_s[d, rows, sl] = (c_loc[:, fi:fi + 1] * h_s[d, rows, sl]
                                + c_int[:, fi:fi + 1] * qcn_s[fi, :, :M_DH])
            cn_s[fi] = scal[hh][2] * cn_s[fi] + scal[hh][3] * u_s[fi]
            m_s[fi] = jnp.broadcast_to(scal[hh][1], (1, M_DH))

    def scan_body(s, carry):
        rf = pl.multiple_of(s * ch, ch)
        rb = pl.multiple_of(jnp.where(s < ncc, ncc - 1 - s, nc - 1 - s + ncc) * ch, ch)
        sf = scan_issue(rf, 0)
        sb = scan_issue(rb, 1)
        scan_finish(rf, 0, sf)
        scan_finish(rb, 1, sb)
        return carry

    lax.fori_loop(0, nc, scan_body, 0)

    mnorm = mnorm_ref[...]

    def out_body(j, carry):
        r0 = pl.multiple_of(j * ch, ch)
        hsum = h_s[0, pl.ds(r0, ch), :] + h_s[1, pl.ds(r0, ch), :]
        og = _sigmoid(vo_ref[0, pl.ds(r0, ch), M_WIDTH:].astype(F32))
        for hh in range(M_HEADS):
            sl = slice(hh * M_DH, (hh + 1) * M_DH)
            ya_ref[0, pl.ds(r0, ch), sl] = (_rms(hsum[:, sl], mnorm[:, sl]) * og[:, sl]).astype(BF16)
        return carry

    lax.fori_loop(0, nc, out_body, 0)


def _mlstm_call(qk, vo, gi, gf, gr, conv, mnorm, lc):
    b, l, _ = qk.shape
    ng = 2 * M_HEADS
    return pl.pallas_call(
        functools.partial(_mlstm_kernel, lc=lc),
        grid=(b,),
        in_specs=[pl.BlockSpec((1, l, 2 * M_WIDTH), lambda bi: (bi, 0, 0), pipeline_mode=pl.Buffered(1)),
                  pl.BlockSpec((1, l, 2 * M_WIDTH), lambda bi: (bi, 0, 0)),
                  pl.BlockSpec((1, l, ng), lambda bi: (bi, 0, 0)),
                  pl.BlockSpec((1, l, ng), lambda bi: (bi, 0, 0)),
                  pl.BlockSpec((1, 2 * ng, l), lambda bi: (bi, 0, 0)),
                  _const_spec(conv.shape), _const_spec(mnorm.shape)],
        out_specs=pl.BlockSpec((1, l, M_WIDTH), lambda bi: (bi, 0, 0)),
        out_shape=jax.ShapeDtypeStruct((b, l, M_WIDTH), BF16),
        scratch_shapes=[pltpu.VMEM((l, M_WIDTH), BF16),
                        pltpu.VMEM((M_WIDTH, l), BF16),
                        pltpu.VMEM((2, l, M_WIDTH), F32),
                        pltpu.VMEM((l, ng), F32),
                        pltpu.VMEM((l, ng), F32),
                        pltpu.VMEM((l, ng), F32),
                        pltpu.VMEM((ng, l), F32),
                        pltpu.VMEM((2 * M_HEADS, M_DH, 2 * M_DH), F32),
                        pltpu.VMEM((2 * M_HEADS, 1, M_DH), F32),
                        pltpu.VMEM((M_HEADS, M_CHUNK, M_CHUNK), F32),
                        pltpu.VMEM((2 * M_HEADS, M_CHUNK, M_CHUNK), BF16),
                        pltpu.VMEM((2 * M_HEADS, M_CHUNK, 2 * M_DH), F32),
                        pltpu.VMEM((2 * M_HEADS, M_DH, 2 * M_DH), F32)],
        compiler_params=_params("parallel"), name="mlstm",
    )(qk, vo, gi, gf, gr, conv, mnorm)


def _attn_kernel(q_ref, k_ref, v_ref, o_ref, *, lc, ctx_out):
    tq = q_ref.shape[1]
    l = k_ref.shape[1]
    qi = pl.program_id(2)
    nct = lc // tq
    lane = lax.broadcasted_iota(jnp.int32, (tq, 2 * A_VDIM), 1)

    def run(klen):
        outs = []
        for hh in range(A_HPS):
            sl = slice(hh * A_PAD, (hh + 1) * A_PAD)
            s = _dot_nt(q_ref[0, :, sl], k_ref[0, :klen, sl])
            p = jnp.exp2((s - jnp.max(s, axis=-1, keepdims=True)).astype(BF16))
            nd = _dot(p, v_ref[0, :klen, sl])
            outs.append(nd / pltpu.roll(nd, A_VDIM, 1))
        for pp in range(A_HPS // 2):
            o_ref[0, :, pp * A_PAD:(pp + 1) * A_PAD] = jnp.where(
                lane < A_VDIM, outs[2 * pp], pltpu.roll(outs[2 * pp + 1], A_VDIM, 1)).astype(BF16)

    @pl.when(qi >= nct)
    def _():
        run(l)

    @pl.when(qi < nct)
    def _():
        if ctx_out:
            run(lc)
        else:
            o_ref[...] = jnp.zeros_like(o_ref)


def _attn_call(q, k, v, lc, tq, ctx_out):
    b, l, _ = q.shape
    return pl.pallas_call(
        functools.partial(_attn_kernel, lc=lc, ctx_out=ctx_out),
        grid=(b, A_HEADS // A_HPS, l // tq),
        in_specs=[pl.BlockSpec((1, tq, A_HPS * A_PAD), lambda bi, p, qi: (bi, qi, p)),
                  pl.BlockSpec((1, l, A_HPS * A_PAD), lambda bi, p, qi: (bi, 0, p)),
                  pl.BlockSpec((1, l, A_HPS * A_PAD), lambda bi, p, qi: (bi, 0, p))],
        out_specs=pl.BlockSpec((1, tq, A_HPS * A_VDIM), lambda bi, p, qi: (bi, qi, p)),
        out_shape=jax.ShapeDtypeStruct((b, l, A_WIDTH), BF16),
        compiler_params=_params("parallel", "parallel", "arbitrary"), name="attn",
    )(q, k, v)


def _merge_kernel(x_ref, mod_ref, ya_ref, yb_ref, yc_ref, br_ref, wpa_ref, wpb_ref, wpc_ref, wout_ref,
                  n2_ref, r_ref, rb_ref, o_ref, h2_o, idx_o, idxt_o, cnt_o):
    d = x_ref.shape[2]
    br = br_ref[0]
    mod = mod_ref[0]
    y = (br[:, :d].astype(F32) * _dot(ya_ref[0], wpa_ref[...])
         + br[:, d:2 * d].astype(F32) * _dot(yb_ref[0], wpb_ref[...])
         + br[:, 2 * d:].astype(F32) * _dot(yc_ref[0], wpc_ref[...]))
    out = _dot(y.astype(BF16), wout_ref[...])
    x = x_ref[0] + mod[2:3] * out
    o_ref[0] = x
    _route(x, mod, n2_ref, r_ref, rb_ref, h2_o, idx_o, idxt_o, cnt_o)


def _merge_call(xs, mod, ya, yb, yc, br, w, n2, r3, rb, lc, tm):
    b, l, d = xs.shape
    nct = lc // tm
    tok = lambda width: pl.BlockSpec((1, tm, width), lambda bi, j: (bi, j, 0))
    consts = [w["wpa"], w["wpb"], w["wpc"], w["wout"], n2, r3, rb]
    return pl.pallas_call(
        _merge_kernel, grid=(b, l // tm),
        in_specs=[tok(d), pl.BlockSpec((1, 6, d), lambda bi, j: (jnp.where(j < nct, b, bi), 0, 0)),
                  tok(M_WIDTH), tok(A_WIDTH), tok(G_WIDTH), tok(3 * d)] + [_const_spec(a.shape) for a in consts],
        out_specs=[tok(d), tok(d + R_PAD), tok(8), pl.BlockSpec((1, 8, tm), lambda bi, j: (bi, 0, j)),
                   pl.BlockSpec((1, 1, 8, R_PAD), lambda bi, j: (bi, j, 0, 0))],
        out_shape=[jax.ShapeDtypeStruct((b, l, d), F32),
                   jax.ShapeDtypeStruct((b, l, d + R_PAD), BF16),
                   jax.ShapeDtypeStruct((b, l, 8), jnp.int32), jax.ShapeDtypeStruct((b, 8, l), jnp.int32),
                   jax.ShapeDtypeStruct((b, l // tm, 8, R_PAD), F32)],
        compiler_params=_params("parallel", "parallel"), name="merge",
    )(xs, mod, ya, yb, yc, br, *consts)


def _route(x, mod, n2_ref, r_ref, rb_ref, h2_o, idx_o, idxt_o, cnt_o):
    tm, d = x.shape
    h2 = _rms(x, n2_ref[...]) * (1.0 + mod[4:5]) + mod[3:4]
    h2_o[0, :, :d] = h2.astype(BF16)

    r = r_ref[...]
    pp = sum(_dot(piece, r) for piece in _split3(h2))
    logits = pp + pltpu.roll(pp, R_PAD - R_SEG, 1) + pltpu.roll(pp, R_PAD - 2 * R_SEG, 1) + rb_ref[...]
    el = logits[:, :N_EXPERTS]
    gl = logits[:, N_EXPERTS:N_EXPERTS + N_GROUPS]
    big = 1e9

    lane_g = lax.broadcasted_iota(jnp.int32, (tm, N_GROUPS), 1).astype(F32)
    gmax = jnp.max(gl, axis=-1, keepdims=True)
    g_sel = jnp.min(jnp.where(gl == gmax, lane_g, big), axis=-1, keepdims=True)
    g_prob = 1.0 / jnp.sum(jnp.exp(gl - gmax), axis=-1, keepdims=True)

    lane_i = lax.broadcasted_iota(jnp.int32, (tm, N_EXPERTS), 1)
    lane_e = lane_i.astype(F32)
    lane_grp = (lane_i // EXP_PER_GROUP).astype(F32)
    v1 = jnp.where(lane_grp == g_sel, el, -jnp.inf)
    t1 = jnp.max(v1, axis=-1, keepdims=True)
    i1 = jnp.min(jnp.where(v1 == t1, lane_e, big), axis=-1, keepdims=True)
    v2 = jnp.where(lane_e == i1, -jnp.inf, v1)
    t2 = jnp.max(v2, axis=-1, keepdims=True)
    i2 = jnp.min(jnp.where(v2 == t2, lane_e, big), axis=-1, keepdims=True)
    e21 = jnp.exp(t2 - t1)
    w1 = 1.0 / (1.0 + e21)
    w2 = e21 * w1
    comb = (jnp.where(lane_e == i1, w1, 0.0) + jnp.where(lane_e == i2, w2, 0.0)) * g_prob
    tail = jnp.zeros((tm, R_PAD - 3 * N_EXPERTS), BF16)
    h2_o[0, :, d:] = jnp.concatenate(list(_split3(comb)) + [tail], axis=1)

    lane_p = lax.broadcasted_iota(jnp.int32, (tm, R_PAD), 1)
    onehot = jnp.where(lane_p.astype(F32) == g_sel, 1.0, 0.0)
    ri = lax.broadcasted_iota(jnp.int32, (tm, tm), 0)
    ci = lax.broadcasted_iota(jnp.int32, (tm, tm), 1)
    before = jnp.where(ri > ci, 1.0, 0.0).astype(BF16)
    rank = jnp.sum(_dot(before, onehot.astype(BF16)) * onehot, axis=-1, keepdims=True)
    cnt_o[0, 0] = jnp.broadcast_to(jnp.sum(onehot, axis=0, keepdims=True), (8, R_PAD))
    fields = jnp.where(lane_p == 0, g_sel, jnp.where(lane_p == 1, rank, 0.0))
    idx_o[0] = fields[:, :8].astype(jnp.int32)
    idxt_o[0] = fields.T[:8, :].astype(jnp.int32)


def _experts_kernel(st_ref, h2_ref, idx_ref, idxt_ref, w1_ref, w3_ref, w2_ref, o_ref, hs_s, *, tb):
    l = h2_ref.shape[1]
    d = o_ref.shape[2]
    nblk = l // tb
    bi = pl.program_id(0)
    g = pl.program_id(1)

    @pl.when(g == 0)
    def _():
        o_ref[...] = jnp.zeros_like(o_ref)

    base = (bi * N_GROUPS + g) * (nblk + 1)
    cnt = st_ref[base + nblk]

    def chunk(lo, ch):
        sub_iota = lax.broadcasted_iota(jnp.int32, (ch, tb), 0)
        lane_iota = lax.broadcasted_iota(jnp.int32, (tb, ch), 1)
        lane_e = lax.broadcasted_iota(jnp.int32, (ch, N_EXPERTS), 1)
        hs_s[:ch] = jnp.zeros((ch, hs_s.shape[1]), F32)
        for k in range(nblk):
            s_k = st_ref[base + k]
            e_k = st_ref[base + k + 1]
            rows = slice(k * tb, (k + 1) * tb)

            @pl.when(jnp.logical_and(s_k < lo + ch, e_k > lo))
            def _(s_k=s_k, rows=rows):
                it = idxt_ref[0, :, rows]
                pos = jnp.where(it[0:1] == g, it[1:2] + (s_k - lo), -1)
                p = jnp.where(sub_iota == pos, 1.0, 0.0).astype(BF16)
                hs_s[:ch] += _dot(p, h2_ref[0, rows, :])

        hsb = hs_s[:ch, :d].astype(BF16)
        cs = (hs_s[:ch, d:d + N_EXPERTS] + hs_s[:ch, d + N_EXPERTS:d + 2 * N_EXPERTS]
              + hs_s[:ch, d + 2 * N_EXPERTS:d + 3 * N_EXPERTS])
        y = jnp.zeros((ch, d), F32)
        for e in range(EXP_PER_GROUP):
            ce = jnp.sum(jnp.where(lane_e == g * EXP_PER_GROUP + e, cs, 0.0), axis=-1, keepdims=True)
            hid = (_silu(_dot(hsb, w1_ref[e])) * _dot(hsb, w3_ref[e]) * ce).astype(BF16)
            y = y + _dot(hid, w2_ref[e])
        yb = y.astype(BF16)

        for k in range(nblk):
            s_k = st_ref[base + k]
            e_k = st_ref[base + k + 1]
            rows = slice(k * tb, (k + 1) * tb)

            @pl.when(jnp.logical_and(s_k < lo + ch, e_k > lo))
            def _(s_k=s_k, rows=rows):
                ic = idx_ref[0, rows, :]
                pos = jnp.where(ic[:, 0:1] == g, ic[:, 1:2] + (s_k - lo), -1)
                q = jnp.where(lane_iota == pos, 1.0, 0.0).astype(BF16)
                o_ref[0, rows, :] = (o_ref[0, rows, :].astype(F32) + _dot(q, yb)).astype(BF16)

    ch = MOE_CHUNK
    half = ch // 2
    nfull = cnt // ch
    rem = cnt - nfull * ch
    nloop = nfull + jnp.where(rem > half, 1, 0)

    def chunk_body(c, carry):
        chunk(c * ch, ch)
        return carry

    lax.fori_loop(0, nloop, chunk_body, 0)

    @pl.when(jnp.logical_and(rem > 0, rem <= half))
    def _():
        chunk(nfull * ch, half)


def _experts_call(starts, h2, idx, idxt, w1, w3, w2, tb):
    b, l, de = h2.shape
    d = de - R_PAD
    whole = lambda width: pl.BlockSpec((1, l, width), lambda bi, g, st: (bi, 0, 0))
    grid_spec = pltpu.PrefetchScalarGridSpec(
        num_scalar_prefetch=1, grid=(b, N_GROUPS),
        in_specs=[whole(de), whole(8), pl.BlockSpec((1, 8, l), lambda bi, g, st: (bi, 0, 0)),
                  pl.BlockSpec((EXP_PER_GROUP, d, D_EXPERT), lambda bi, g, st: (g, 0, 0)),
                  pl.BlockSpec((EXP_PER_GROUP, d, D_EXPERT), lambda bi, g, st: (g, 0, 0)),
                  pl.BlockSpec((EXP_PER_GROUP, D_EXPERT, d), lambda bi, g, st: (g, 0, 0))],
        out_specs=whole(d),
        scratch_shapes=[pltpu.VMEM((MOE_CHUNK, de), F32)])
    return pl.pallas_call(
        functools.partial(_experts_kernel, tb=tb), grid_spec=grid_spec,
        out_shape=jax.ShapeDtypeStruct((b, l, d), BF16),
        compiler_params=_params("parallel", "arbitrary"), name="experts",
    )(starts, h2, idx, idxt, w1, w3, w2)


def _group_starts(cnt):
    c = cnt[:, :, 0, :N_GROUPS].astype(jnp.int32)
    s = jnp.cumsum(c, axis=1)
    s = jnp.concatenate([jnp.zeros_like(s[:, :1]), s], axis=1)
    return jnp.transpose(s, (0, 2, 1)).reshape(-1)


def _final_kernel(x_ref, mod_ref, f_ref, g_ref, o_ref):
    o_ref[0] = _rms(x_ref[0] + mod_ref[0][5:6] * f_ref[0].astype(F32), g_ref[...])


def _final_call(xs, mod, f, g, lc, tm):
    b, l, d = xs.shape
    off = lc // tm
    lat = pl.BlockSpec((1, tm, d), lambda bi, j: (bi, j + off, 0))
    return pl.pallas_call(
        _final_kernel, grid=(b, (l - lc) // tm),
        in_specs=[lat, pl.BlockSpec((1, 6, d), lambda bi, j: (bi, 0, 0)), lat, _const_spec(g.shape)],
        out_specs=pl.BlockSpec((1, tm, d), lambda bi, j: (bi, j, 0)),
        out_shape=jax.ShapeDtypeStruct((b, l - lc, d), F32),
        compiler_params=_params("parallel", "parallel"), name="final_norm",
    )(xs, mod, f, g)


def _rope_tables(t_len, lc):
    half = A_ROPE // 2
    rows = t_len // GRID_W
    r = jnp.repeat(jnp.arange(rows, dtype=F32), GRID_W)
    col = jnp.tile(jnp.arange(GRID_W, dtype=F32), rows)
    inv = ROPE_THETA ** (-jnp.arange(0, half, 2, dtype=F32) / half)
    ang = jnp.concatenate([r[:, None] * inv, col[:, None] * inv], axis=-1)
    cos = jnp.concatenate([jnp.ones((lc, half), F32), jnp.cos(ang)], axis=0)
    sin = jnp.concatenate([jnp.zeros((lc, half), F32), jnp.sin(ang)], axis=0)
    l = lc + t_len
    ones = jnp.ones((l, A_NOPE), F32)
    zeros = jnp.zeros((l, A_NOPE), F32)
    tail1 = jnp.ones((l, A_PAD - A_NOPE - A_ROPE), F32)
    tail0 = jnp.zeros((l, A_PAD - A_NOPE - A_ROPE), F32)
    zh = jnp.zeros((l, half), F32)
    cos_t = jnp.concatenate([ones, cos, cos, tail1], axis=-1)
    sina_t = jnp.concatenate([zeros, zh, sin, tail0], axis=-1)
    sinb_t = jnp.concatenate([zeros, -sin, zh, tail0], axis=-1)
    return cos_t, sina_t, sinb_t


def _layer_weights(l, w_in, m_gate_b, a_qnorm, a_wuq, a_kvnorm, a_wukv, g_ws, g_bs, g_vnorm,
                   w_pa, w_pb, w_pc, w_out):
    d = w_in.shape[1]
    wi = w_in[l]
    o = 0

    def take(n):
        nonlocal o
        s = wi[:, o:o + n]
        o += n
        return s

    mq, mk, mv, mo, mg = take(M_WIDTH), take(M_WIDTH), take(M_WIDTH), take(M_WIDTH), take(4 * M_HEADS)
    aq, akv, akr = take(A_QRANK), take(A_KVRANK), take(A_ROPE)
    gu, gv = take(G_WIDTH), take(G_WIDTH)
    br = take(3 * d)
    nh = M_HEADS
    gb = m_gate_b[l]
    mgo = jnp.concatenate([mg[:, :nh], mg[:, 2 * nh:3 * nh], mg[:, nh:2 * nh], mg[:, 3 * nh:]], axis=1)
    gbo = jnp.concatenate([gb[:nh], gb[2 * nh:3 * nh], gb[nh:2 * nh], gb[3 * nh:]])
    akr_pad = jnp.concatenate([jnp.zeros((d, A_NOPE), F32), akr,
                               jnp.zeros((d, A_PAD - A_NOPE - A_ROPE), F32)], axis=1)
    wuq = a_wuq[l].reshape(A_QRANK, A_HEADS, A_NOPE + A_ROPE)
    wuq = jnp.pad(wuq, ((0, 0), (0, 0), (0, A_PAD - A_NOPE - A_ROPE))).reshape(A_QRANK, A_HEADS * A_PAD)
    wukv = a_wukv[l].reshape(A_KVRANK, A_HEADS, A_NOPE + A_VDIM)
    wuk = jnp.pad(wukv[:, :, :A_NOPE], ((0, 0), (0, 0), (0, A_PAD - A_NOPE))).reshape(A_KVRANK, A_HEADS * A_PAD)
    wuv = jnp.pad(wukv[:, :, A_NOPE:], ((0, 0), (0, 0), (0, A_PAD - A_VDIM))).reshape(A_KVRANK, A_HEADS * A_PAD)
    vone = jnp.tile(jnp.concatenate([jnp.zeros((A_VDIM,), F32), jnp.ones((A_PAD - A_VDIM,), F32)]),
                    A_HEADS).reshape(1, A_HEADS * A_PAD)
    gbs = jnp.repeat(g_bs[l].T, G_DG, axis=1)
    return dict(
        wqk=jnp.concatenate([mq, mk], 1).astype(BF16), wvo=jnp.concatenate([mv, mo], 1).astype(BF16),
        wgt=jnp.pad(mgo, ((0, 0), (0, G_PAD - 4 * nh))).astype(BF16),
        gbt=jnp.pad(gbo, (0, G_PAD - 4 * nh)).reshape(1, G_PAD),
        wa=jnp.concatenate([aq, akv, akr_pad], 1).astype(BF16),
        wg=jnp.concatenate([gu, gv], 1).astype(BF16), wbr=br.astype(BF16),
        aqn=a_qnorm[l].reshape(1, -1), akvn=a_kvnorm[l].reshape(1, -1),
        wuq=wuq.astype(BF16), wuk=wuk.astype(BF16), wuv=wuv.astype(BF16), vone=vone,
        gvn=g_vnorm[l].reshape(1, -1), gws=g_ws[l].astype(BF16), gbs=gbs,
        wpa=w_pa[l].astype(BF16), wpb=w_pb[l].astype(BF16), wpc=w_pc[l].astype(BF16),
        wout=w_out[l].astype(BF16))


def _router_weights(r_group, r_group_b, r_expert, r_expert_b):
    d = r_group.shape[0]
    pad = R_SEG - N_EXPERTS - N_GROUPS
    r = jnp.concatenate([r_expert, r_group, jnp.zeros((d, pad), F32)], axis=1)
    r3 = jnp.concatenate(list(_split3(r)) + [jnp.zeros((d, R_PAD - 3 * R_SEG), BF16)], axis=1)
    rb = jnp.concatenate([r_expert_b, r_group_b, jnp.zeros((R_PAD - N_EXPERTS - N_GROUPS,), F32)])
    return r3, rb.reshape(1, R_PAD)


def _tile(n, lc, candidates):
    for t in candidates:
        if n % t == 0 and lc % t == 0:
            return t
    raise ValueError("sequence lengths must be multiples of 128")


def kernel(x, c, ctx, c_ctx, w_ada, b_ada, norm1, norm2, final_norm, w_in, m_conv, m_gate_b, m_norm, a_qnorm, a_wuq, a_kvnorm, a_wukv, g_ws, g_bs, g_vnorm, w_pa, w_pb, w_pc, w_out, r_group, r_group_b, r_expert, r_expert_b, e_w1, e_w3, e_w2):
    b, t_len, d = x.shape
    lc = ctx.shape[1]
    l = lc + t_len
    depth = w_in.shape[0]
    tm = _tile(l, lc, (256, 128))

    xs = jnp.concatenate([ctx, x], axis=1)
    cv = jnp.concatenate([c, c_ctx[None, :]], axis=0)
    mod_all = _ada_call(cv, w_ada, b_ada).reshape(depth, b + 1, 6, d)
    tabs = _rope_tables(t_len, lc)

    f = None
    mod_prev = None
    for li in range(depth):
        last = li == depth - 1
        mod = mod_all[li]
        w = _layer_weights(li, w_in, m_gate_b, a_qnorm, a_wuq, a_kvnorm, a_wukv, g_ws, g_bs, g_vnorm,
                           w_pa, w_pb, w_pc, w_out)
        outs = _inproj_call(xs, mod, norm1[li].reshape(1, d), w, tabs, lc, tm, f, mod_prev)
        qk, vo, gi, gf, gr, q, k, v, yc, br = outs[:10]
        if f is not None:
            xs = outs[10]
        ya = _mlstm_call(qk, vo, gi, gf, gr, m_conv[li], m_norm[li].reshape(1, -1), lc)
        yb = _attn_call(q, k, v, lc, tm, not last)
        r3, rb = _router_weights(r_group[li], r_group_b[li], r_expert[li], r_expert_b[li])
        xs, h2, idx, idxt, cnt = _merge_call(xs, mod, ya, yb, yc, br, w, norm2[li].reshape(1, d),
                                             r3, rb, lc, tm)
        f = _experts_call(_group_starts(cnt), h2, idx, idxt, e_w1[li].astype(BF16),
                          e_w3[li].astype(BF16), e_w2[li].astype(BF16), tm)
        mod_prev = mod
    return _final_call(xs, mod_prev, f, final_norm.reshape(1, d), lc, tm)
```

```python
import functools

import jax
import jax.numpy as jnp
from jax import lax
from jax.experimental import pallas as pl
from jax.experimental.pallas import tpu as pltpu

F32 = jnp.float32
BF16 = jnp.bfloat16

EPS = 1e-6
GRID_W = 64
ROPE_THETA = 10000.0

M_HEADS = 4
M_DH = 128
M_WIDTH = M_HEADS * M_DH
M_CHUNK = 128
G_PAD = 128

A_HEADS = 8
A_NOPE = 64
A_ROPE = 32
A_VDIM = 64
A_QRANK = 384
A_KVRANK = 256
A_WIDTH = A_HEADS * A_VDIM
A_PAD = 128
A_HPS = 4
ATT_SCALE = (A_NOPE + A_ROPE) ** -0.5
LOG2E = 1.4426950408889634

G_GROUPS = 4
G_CHUNK = 128
G_WIDTH = 512
G_DG = G_WIDTH // G_GROUPS

N_GROUPS = 4
EXP_PER_GROUP = 4
N_EXPERTS = N_GROUPS * EXP_PER_GROUP
D_EXPERT = 512
R_PAD = 128
R_SEG = 32
MOE_CHUNK = 256

VMEM_LIMIT = 56 * 1024 * 1024


def _dot(a, b):
    return jnp.dot(a, b, preferred_element_type=F32)


def _dot_nt(a, b):
    return lax.dot_general(a, b, (((1,), (1,)), ((), ())), preferred_element_type=F32)


def _dot_tn(a, b):
    return lax.dot_general(a, b, (((0,), (0,)), ((), ())), preferred_element_type=F32)


def _split3(x):
    hi = x.astype(BF16)
    r = x - hi.astype(F32)
    mid = r.astype(BF16)
    lo = (r - mid.astype(F32)).astype(BF16)
    return hi, mid, lo


def _sigmoid(x):
    return 1.0 / (1.0 + jnp.exp(-x))


def _silu(x):
    return x * _sigmoid(x)


def _log_sigmoid(x):
    return jnp.minimum(x, 0.0) - jnp.log1p(jnp.exp(-jnp.abs(x)))


def _gelu(x):
    return 0.5 * x * (1.0 + lax.erf(x * (2.0 ** -0.5)))


def _rms(x, g):
    return x * lax.rsqrt(jnp.mean(x * x, axis=-1, keepdims=True) + EPS) * g


def _params(*sem):
    return pltpu.CompilerParams(dimension_semantics=sem, vmem_limit_bytes=VMEM_LIMIT)


def _const_spec(shape):
    nd = len(shape)
    return pl.BlockSpec(shape, lambda *_: (0,) * nd, pipeline_mode=pl.Buffered(1))


def _ada_kernel(cv_ref, w_ref, b_ref, o_ref):
    s = _silu(cv_ref[...])
    o_ref[0] = _dot(s.astype(BF16), w_ref[0].astype(BF16)) + b_ref[0]


def _ada_call(cv, w_ada, b_ada):
    depth, d, n6 = w_ada.shape
    rows = cv.shape[0]
    tn = n6 // 4
    return pl.pallas_call(
        _ada_kernel,
        grid=(depth, n6 // tn),
        in_specs=[pl.BlockSpec((rows, d), lambda l, j: (0, 0)),
                  pl.BlockSpec((1, d, tn), lambda l, j: (l, 0, j)),
                  pl.BlockSpec((1, 1, tn), lambda l, j: (l, 0, j))],
        out_specs=pl.BlockSpec((1, rows, tn), lambda l, j: (l, 0, j)),
        out_shape=jax.ShapeDtypeStruct((depth, rows, n6), F32),
        compiler_params=_params("parallel", "parallel"),
        name="ada",
    )(cv, w_ada, b_ada.reshape(depth, 1, n6))


def _inproj_kernel(*refs, has_f):
    if has_f:
        f_ref, modp_ref, x_o = refs[0], refs[1], refs[-1]
        refs = refs[2:-1]
    (x_ref, mod_ref, n1_ref, wqk_ref, wvo_ref, wgt_ref, gbt_ref,
     wa_ref, wg_ref, wbr_ref, aqn_ref, akvn_ref, wuq_ref, wuk_ref, wuv_ref, vone_ref,
     cos_ref, sina_ref, sinb_ref, gvn_ref, gws_ref, gbs_ref,
     qk_o, vo_o, gi_o, gf_o, gr_o, q_o, k_o, v_o, yc_o, br_o) = refs
    tm = x_ref.shape[1]
    mod = mod_ref[0]
    x = x_ref[0]
    if has_f:
        x = x + modp_ref[0][5:6] * f_ref[0].astype(F32)
        x_o[0] = x
    h = _rms(x, n1_ref[...]) * (1.0 + mod[1:2]) + mod[0:1]
    hb = h.astype(BF16)

    qk_o[0] = _dot(hb, wqk_ref[...])
    vo_o[0] = _dot(hb, wvo_ref[...]).astype(BF16)
    ng = gi_o.shape[2]
    gates = _dot(hb, wgt_ref[...]) + gbt_ref[...]
    gi_o[0] = gates[:, :ng]
    gf_o[0] = pltpu.roll(gates, gates.shape[1] - ng, 1)[:, :ng]
    gr_o[0] = gates.T[:2 * ng, :]

    za = _dot(hb, wa_ref[...])
    aqn = _rms(za[:, :A_QRANK], aqn_ref[...]).astype(BF16)
    akvn = _rms(za[:, A_QRANK:A_QRANK + A_KVRANK], akvn_ref[...]).astype(BF16)
    cos = cos_ref[...]
    sina = sina_ref[...]
    sinb = sinb_ref[...]
    half = A_ROPE // 2

    def rope(t):
        return t * cos + pltpu.roll(t, half, 1) * sina + pltpu.roll(t, A_PAD - half, 1) * sinb

    kr = rope(za[:, A_QRANK + A_KVRANK:])
    qp = _dot(aqn, wuq_ref[...])
    kp = _dot(akvn, wuk_ref[...])
    for hh in range(A_HEADS):
        sl = slice(hh * A_PAD, (hh + 1) * A_PAD)
        q_o[0, :, sl] = (rope(qp[:, sl]) * (ATT_SCALE * LOG2E)).astype(BF16)
        k_o[0, :, sl] = (kp[:, sl] + kr).astype(BF16)
    v_o[0] = (_dot(akvn, wuv_ref[...]) + vone_ref[...]).astype(BF16)

    zg = _dot(hb, wg_ref[...])
    gu = _gelu(zg[:, :G_WIDTH])
    gv = _gelu(zg[:, G_WIDTH:])
    gvn = gvn_ref[...]
    bias = gbs_ref[...]
    for g in range(G_GROUPS):
        sl = slice(g * G_DG, (g + 1) * G_DG)
        xn = _rms(gv[:, sl], gvn[:, sl]).astype(BF16)
        ws = gws_ref[g]
        for ci in range(tm // G_CHUNK):
            r = slice(ci * G_CHUNK, (ci + 1) * G_CHUNK)
            sg = _dot(ws, xn[r]) + bias[:, sl]
            yc_o[0, r, sl] = (gu[r, sl] * sg).astype(BF16)

    br_o[0] = _sigmoid(_dot(hb, wbr_ref[...])).astype(BF16)


def _inproj_call(xs, mod, n1, w, tabs, lc, tm, f=None, mod_prev=None):
    b, l, d = xs.shape
    nct = lc // tm
    has_f = f is not None
    tok = lambda width: pl.BlockSpec((1, tm, width), lambda bi, j: (bi, j, 0))
    modspec = pl.BlockSpec((1, 6, d), lambda bi, j: (jnp.where(j < nct, b, bi), 0, 0))
    tab = pl.BlockSpec((tm, A_PAD), lambda bi, j: (j, 0))
    consts = [n1, w["wqk"], w["wvo"], w["wgt"], w["gbt"], w["wa"], w["wg"], w["wbr"],
              w["aqn"], w["akvn"], w["wuq"], w["wuk"], w["wuv"], w["vone"]]
    consts2 = [w["gvn"], w["gws"], w["gbs"]]
    in_specs = ([tok(d), modspec] + [_const_spec(a.shape) for a in consts] + [tab, tab, tab]
                + [_const_spec(a.shape) for a in consts2])
    args = [xs, mod, *consts, *tabs, *consts2]
    ng = 4 * M_HEADS
    out_shape = [jax.ShapeDtypeStruct((b, l, 2 * M_WIDTH), F32),
                 jax.ShapeDtypeStruct((b, l, 2 * M_WIDTH), BF16),
                 jax.ShapeDtypeStruct((b, l, ng // 2), F32),
                 jax.ShapeDtypeStruct((b, l, ng // 2), F32),
                 jax.ShapeDtypeStruct((b, ng, l), F32),
                 jax.ShapeDtypeStruct((b, l, A_HEADS * A_PAD), BF16),
                 jax.ShapeDtypeStruct((b, l, A_HEADS * A_PAD), BF16),
                 jax.ShapeDtypeStruct((b, l, A_HEADS * A_PAD), BF16),
                 jax.ShapeDtypeStruct((b, l, G_WIDTH), BF16),
                 jax.ShapeDtypeStruct((b, l, 3 * d), BF16)]
    out_specs = [tok(2 * M_WIDTH), tok(2 * M_WIDTH), tok(ng // 2), tok(ng // 2),
                 pl.BlockSpec((1, ng, tm), lambda bi, j: (bi, 0, j)),
                 tok(A_HEADS * A_PAD), tok(A_HEADS * A_PAD), tok(A_HEADS * A_PAD), tok(G_WIDTH), tok(3 * d)]
    if has_f:
        in_specs = [tok(d), modspec] + in_specs
        args = [f, mod_prev] + args
        out_shape.append(jax.ShapeDtypeStruct((b, l, d), F32))
        out_specs.append(tok(d))
    return pl.pallas_call(
        functools.partial(_inproj_kernel, has_f=has_f), grid=(b, l // tm), in_specs=in_specs,
        out_specs=out_specs, out_shape=out_shape,
        compiler_params=_params("parallel", "parallel"), name="inproj",
    )(*args)


def _scan(x, op, fill, axis, reverse):
    n = x.shape[axis]
    idx = lax.broadcasted_iota(jnp.int32, x.shape, axis)
    k = 1
    while k < n:
        if reverse:
            x = op(x, jnp.where(idx >= n - k, fill, pltpu.roll(x, n - k, axis)))
        else:
            x = op(x, jnp.where(idx < k, fill, pltpu.roll(x, k, axis)))
        k *= 2
    return x


def _mlstm_kernel(qk_ref, vo_ref, gi_ref, gf_ref, gr_ref, conv_ref, mnorm_ref, ya_ref,
                  q_s, kt_s, h_s, bc_s, ml_s, dl_s, br_s, cn_s, m_s, s_s, p_s, qcn_s, u_s, *, lc):
    l = qk_ref.shape[1]
    ch = M_CHUNK
    nc = l // ch
    ncc = lc // ch
    nh = M_HEADS
    ng = 2 * nh
    w = conv_ref[...]
    row = lax.broadcasted_iota(jnp.int32, (ch, 1), 0)

    def conv_chunk(j):
        r0 = pl.multiple_of(j * ch, ch)
        cur = qk_ref[0, pl.ds(r0, ch), :]
        prev8 = qk_ref[0, pl.ds(pl.multiple_of(jnp.maximum(r0 - 8, 0), 8), 8), :]
        next8 = qk_ref[0, pl.ds(pl.multiple_of(jnp.minimum(r0 + ch, l - 8), 8), 8), :]
        seg_start = jnp.logical_or(j == 0, j == ncc)
        seg_end = jnp.logical_or(j == ncc - 1, j == nc - 1)
        pe = jnp.where(seg_start, 0.0, prev8[7:8, :])
        ne = jnp.where(seg_end, 0.0, next8[0:1, :])
        xp = jnp.where(row == 0, pe, pltpu.roll(cur, 1, 0))
        xn = jnp.where(row == ch - 1, ne, pltpu.roll(cur, ch - 1, 0))
        y = _silu(xp * w[0:1] + cur * w[1:2] + xn * w[2:3])
        q_s[pl.ds(r0, ch), :] = (y[:, :M_WIDTH] * (M_DH ** -0.5)).astype(BF16)
        kt_s[:, pl.ds(r0, ch)] = y[:, M_WIDTH:].T.astype(BF16)

    ri = lax.broadcasted_iota(jnp.int32, (ch, ch), 0)
    ci = lax.broadcasted_iota(jnp.int32, (ch, ch), 1)
    lower = ri >= ci
    upper = ri <= ci
    ones_blk = jnp.ones((ch, M_DH), BF16)
    fwd_c = lax.broadcasted_iota(jnp.int32, (ch, ng), 1) < nh
    fwd_r = lax.broadcasted_iota(jnp.int32, (ng, ch), 0) < nh
    lane_c = lax.broadcasted_iota(jnp.int32, (ch, ng), 1)

    def local_chunk(j):
        r0 = pl.multiple_of(j * ch, ch)
        rows = pl.ds(r0, ch)
        lfc = _log_sigmoid(gf_ref[0, rows, :])
        gr = gr_ref[0, :, rows]
        lfr = _log_sigmoid(gr[ng:])
        pre_c = _scan(lfc, jnp.add, 0.0, 0, False)
        pre_r = _scan(lfr, jnp.add, 0.0, 1, False)
        b_c = jnp.where(fwd_c, pre_c, jnp.sum(lfc, axis=0, keepdims=True) + lfc - pre_c)
        b_r = jnp.where(fwd_r, pre_r, jnp.sum(lfr, axis=1, keepdims=True) + lfr - pre_r)
        g_c = gi_ref[0, rows, :] - b_c
        g_r = gr[:ng] - b_r
        cg_c = jnp.where(fwd_c, _scan(g_c, jnp.maximum, -jnp.inf, 0, False),
                         _scan(g_c, jnp.maximum, -jnp.inf, 0, True))
        bc_s[rows, :] = b_c
        br_s[:, rows] = b_r
        ml_s[rows, :] = b_c + cg_c
        dl = jnp.zeros((ch, ng), F32)
        for hh in range(nh):
            sl = slice(hh * M_DH, (hh + 1) * M_DH)
            s_s[hh] = _dot(q_s[rows, sl], kt_s[sl, rows])
        for hh in range(nh):
            s = s_s[hh]
            for d in range(2):
                jj = d * nh + hh
                wgt = jnp.exp(jnp.where(upper if d else lower, g_r[jj:jj + 1, :] - cg_c[:, jj:jj + 1], -jnp.inf))
                p_s[jj] = (s * wgt).astype(BF16)
        for hh in range(nh):
            sl = slice(hh * M_DH, (hh + 1) * M_DH)
            v1 = jnp.concatenate([vo_ref[0, rows, sl], ones_blk], axis=1)
            for d in range(2):
                jj = d * nh + hh
                nd = _dot(p_s[jj], v1)
                h_s[d, rows, sl] = nd[:, :M_DH]
                dl = jnp.where(lane_c == jj, nd[:, M_DH:M_DH + ng], dl)
        dl_s[rows, :] = dl

    def conv_local_body(j, carry):
        conv_chunk(j + 1)
        local_chunk(j)
        return carry

    conv_chunk(jnp.int32(0))
    lax.fori_loop(0, nc - 1, conv_local_body, 0)
    local_chunk(jnp.int32(nc - 1))

    cn_s[...] = jnp.zeros_like(cn_s)
    m_s[...] = jnp.zeros_like(m_s)
    lane_r = lax.broadcasted_iota(jnp.int32, (1, ng), 1)

    def scan_issue(r0, d):
        rows = pl.ds(r0, ch)
        gr = gr_ref[0, :, rows]
        br = br_s[:, rows]
        tot = jnp.sum(_log_sigmoid(gr[ng:]), axis=1, keepdims=True)
        scal = []
        for hh in range(nh):
            fi = d * nh + hh
            sl = slice(hh * M_DH, (hh + 1) * M_DH)
            qcn_s[fi] = _dot(q_s[rows, sl], cn_s[fi].astype(BF16))
            m_old = m_s[fi][:, 0:1]
            b_e = tot[fi:fi + 1, :]
            d_end = b_e - br[fi:fi + 1, :] + gr[fi:fi + 1, :]
            m_end = jnp.max(d_end, axis=-1, keepdims=True)
            m_new = jnp.maximum(b_e + m_old, m_end)
            ktw = (kt_s[sl, rows].astype(F32) * jnp.exp(d_end - m_end)).astype(BF16)
            v1 = jnp.concatenate([vo_ref[0, rows, sl], ones_blk], axis=1)
            u_s[fi] = _dot(ktw, v1)
            scal.append((m_old, m_new, jnp.exp(b_e + m_old - m_new), jnp.exp(m_end - m_new)))
        return scal

    def scan_finish(r0, d, scal):
        rows = pl.ds(r0, ch)
        m_row = jnp.zeros((1, ng), F32)
        for hh in range(nh):
            m_row = jnp.where(lane_r == d * nh + hh, scal[hh][0], m_row)

        inter = bc_s[rows, :] + m_row
        ml = ml_s[rows, :]
        mt = jnp.maximum(inter, ml)
        a = jnp.exp(ml - mt)
        wi = jnp.exp(inter - mt)
        qn = jnp.zeros((ch, ng), F32)
        for hh in range(nh):
            fi = d * nh + hh
            qn = jnp.where(lane_c == fi, qcn_s[fi, :, M_DH:M_DH + ng], qn)
        den = a * dl_s[rows, :] + wi * qn
        rinv = 1.0 / jnp.maximum(jnp.abs(den), jnp.exp(-mt))
        c_loc = a * rinv
        c_int = wi * rinv

        for hh in range(nh):
            fi = d * nh + hh
            sl = slice(hh * M_DH, (hh + 1) * M_DH)
            h_s[d, rows, sl] = (c_loc[:, fi:fi + 1] * h_s[d, rows, sl]
                                + c_int[:, fi:fi + 1] * qcn_s[fi, :, :M_DH])
            cn_s[fi] = scal[hh][2] * cn_s[fi] + scal[hh][3] * u_s[fi]
            m_s[fi] = jnp.broadcast_to(scal[hh][1], (1, M_DH))

    def scan_body(s, carry):
        rf = pl.multiple_of(s * ch, ch)
        rb = pl.multiple_of(jnp.where(s < ncc, ncc - 1 - s, nc - 1 - s + ncc) * ch, ch)
        sf = scan_issue(rf, 0)
        sb = scan_issue(rb, 1)
        scan_finish(rf, 0, sf)
        scan_finish(rb, 1, sb)
        return carry

    lax.fori_loop(0, nc, scan_body, 0)

    mnorm = mnorm_ref[...]

    def out_body(j, carry):
        r0 = pl.multiple_of(j * ch, ch)
        hsum = h_s[0, pl.ds(r0, ch), :] + h_s[1, pl.ds(r0, ch), :]
        og = _sigmoid(vo_ref[0, pl.ds(r0, ch), M_WIDTH:].astype(F32))
        for hh in range(M_HEADS):
            sl = slice(hh * M_DH, (hh + 1) * M_DH)
            ya_ref[0, pl.ds(r0, ch), sl] = (_rms(hsum[:, sl], mnorm[:, sl]) * og[:, sl]).astype(BF16)
        return carry

    lax.fori_loop(0, nc, out_body, 0)


def _mlstm_call(qk, vo, gi, gf, gr, conv, mnorm, lc):
    b, l, _ = qk.shape
    ng = 2 * M_HEADS
    return pl.pallas_call(
        functools.partial(_mlstm_kernel, lc=lc),
        grid=(b,),
        in_specs=[pl.BlockSpec((1, l, 2 * M_WIDTH), lambda bi: (bi, 0, 0), pipeline_mode=pl.Buffered(1)),
                  pl.BlockSpec((1, l, 2 * M_WIDTH), lambda bi: (bi, 0, 0)),
                  pl.BlockSpec((1, l, ng), lambda bi: (bi, 0, 0)),
                  pl.BlockSpec((1, l, ng), lambda bi: (bi, 0, 0)),
                  pl.BlockSpec((1, 2 * ng, l), lambda bi: (bi, 0, 0)),
                  _const_spec(conv.shape), _const_spec(mnorm.shape)],
        out_specs=pl.BlockSpec((1, l, M_WIDTH), lambda bi: (bi, 0, 0)),
        out_shape=jax.ShapeDtypeStruct((b, l, M_WIDTH), BF16),
        scratch_shapes=[pltpu.VMEM((l, M_WIDTH), BF16),
                        pltpu.VMEM((M_WIDTH, l), BF16),
                        pltpu.VMEM((2, l, M_WIDTH), F32),
                        pltpu.VMEM((l, ng), F32),
                        pltpu.VMEM((l, ng), F32),
                        pltpu.VMEM((l, ng), F32),
                        pltpu.VMEM((ng, l), F32),
                        pltpu.VMEM((2 * M_HEADS, M_DH, 2 * M_DH), F32),
                        pltpu.VMEM((2 * M_HEADS, 1, M_DH), F32),
                        pltpu.VMEM((M_HEADS, M_CHUNK, M_CHUNK), F32),
                        pltpu.VMEM((2 * M_HEADS, M_CHUNK, M_CHUNK), BF16),
                        pltpu.VMEM((2 * M_HEADS, M_CHUNK, 2 * M_DH), F32),
                        pltpu.VMEM((2 * M_HEADS, M_DH, 2 * M_DH), F32)],
        compiler_params=_params("parallel"), name="mlstm",
    )(qk, vo, gi, gf, gr, conv, mnorm)


def _attn_kernel(q_ref, k_ref, v_ref, o_ref, *, lc, ctx_out):
    tq = q_ref.shape[1]
    l = k_ref.shape[1]
    qi = pl.program_id(2)
    nct = lc // tq
    lane = lax.broadcasted_iota(jnp.int32, (tq, 2 * A_VDIM), 1)

    def run(klen):
        outs = []
        for hh in range(A_HPS):
            sl = slice(hh * A_PAD, (hh + 1) * A_PAD)
            s = _dot_nt(q_ref[0, :, sl], k_ref[0, :klen, sl])
            p = jnp.exp2((s - jnp.max(s, axis=-1, keepdims=True)).astype(BF16))
            nd = _dot(p, v_ref[0, :klen, sl])
            outs.append(nd / pltpu.roll(nd, A_VDIM, 1))
        for pp in range(A_HPS // 2):
            o_ref[0, :, pp * A_PAD:(pp + 1) * A_PAD] = jnp.where(
                lane < A_VDIM, outs[2 * pp], pltpu.roll(outs[2 * pp + 1], A_VDIM, 1)).astype(BF16)

    @pl.when(qi >= nct)
    def _():
        run(l)

    @pl.when(qi < nct)
    def _():
        if ctx_out:
            run(lc)
        else:
            o_ref[...] = jnp.zeros_like(o_ref)


def _attn_call(q, k, v, lc, tq, ctx_out):
    b, l, _ = q.shape
    return pl.pallas_call(
        functools.partial(_attn_kernel, lc=lc, ctx_out=ctx_out),
        grid=(b, A_HEADS // A_HPS, l // tq),
        in_specs=[pl.BlockSpec((1, tq, A_HPS * A_PAD), lambda bi, p, qi: (bi, qi, p)),
                  pl.BlockSpec((1, l, A_HPS * A_PAD), lambda bi, p, qi: (bi, 0, p)),
                  pl.BlockSpec((1, l, A_HPS * A_PAD), lambda bi, p, qi: (bi, 0, p))],
        out_specs=pl.BlockSpec((1, tq, A_HPS * A_VDIM), lambda bi, p, qi: (bi, qi, p)),
        out_shape=jax.ShapeDtypeStruct((b, l, A_WIDTH), BF16),
        compiler_params=_params("parallel", "parallel", "arbitrary"), name="attn",
    )(q, k, v)


def _merge_kernel(x_ref, mod_ref, ya_ref, yb_ref, yc_ref, br_ref, wpa_ref, wpb_ref, wpc_ref, wout_ref,
                  n2_ref, r_ref, rb_ref, o_ref, h2_o, idx_o, idxt_o, cnt_o):
    d = x_ref.shape[2]
    br = br_ref[0]
    mod = mod_ref[0]
    y = (br[:, :d].astype(F32) * _dot(ya_ref[0], wpa_ref[...])
         + br[:, d:2 * d].astype(F32) * _dot(yb_ref[0], wpb_ref[...])
         + br[:, 2 * d:].astype(F32) * _dot(yc_ref[0], wpc_ref[...]))
    out = _dot(y.astype(BF16), wout_ref[...])
    x = x_ref[0] + mod[2:3] * out
    o_ref[0] = x
    _route(x, mod, n2_ref, r_ref, rb_ref, h2_o, idx_o, idxt_o, cnt_o)


def _merge_call(xs, mod, ya, yb, yc, br, w, n2, r3, rb, lc, tm):
    b, l, d = xs.shape
    nct = lc // tm
    tok = lambda width: pl.BlockSpec((1, tm, width), lambda bi, j: (bi, j, 0))
    consts = [w["wpa"], w["wpb"], w["wpc"], w["wout"], n2, r3, rb]
    return pl.pallas_call(
        _merge_kernel, grid=(b, l // tm),
        in_specs=[tok(d), pl.BlockSpec((1, 6, d), lambda bi, j: (jnp.where(j < nct, b, bi), 0, 0)),
                  tok(M_WIDTH), tok(A_WIDTH), tok(G_WIDTH), tok(3 * d)] + [_const_spec(a.shape) for a in consts],
        out_specs=[tok(d), tok(d + R_PAD), tok(8), pl.BlockSpec((1, 8, tm), lambda bi, j: (bi, 0, j)),
                   pl.BlockSpec((1, 1, 8, R_PAD), lambda bi, j: (bi, j, 0, 0))],
        out_shape=[jax.ShapeDtypeStruct((b, l, d), F32),
                   jax.ShapeDtypeStruct((b, l, d + R_PAD), BF16),
                   jax.ShapeDtypeStruct((b, l, 8), jnp.int32), jax.ShapeDtypeStruct((b, 8, l), jnp.int32),
                   jax.ShapeDtypeStruct((b, l // tm, 8, R_PAD), F32)],
        compiler_params=_params("parallel", "parallel"), name="merge",
    )(xs, mod, ya, yb, yc, br, *consts)


def _route(x, mod, n2_ref, r_ref, rb_ref, h2_o, idx_o, idxt_o, cnt_o):
    tm, d = x.shape
    h2 = _rms(x, n2_ref[...]) * (1.0 + mod[4:5]) + mod[3:4]
    h2_o[0, :, :d] = h2.astype(BF16)

    r = r_ref[...]
    pp = sum(_dot(piece, r) for piece in _split3(h2))
    logits = pp + pltpu.roll(pp, R_PAD - R_SEG, 1) + pltpu.roll(pp, R_PAD - 2 * R_SEG, 1) + rb_ref[...]
    el = logits[:, :N_EXPERTS]
    gl = logits[:, N_EXPERTS:N_EXPERTS + N_GROUPS]
    big = 1e9

    lane_g = lax.broadcasted_iota(jnp.int32, (tm, N_GROUPS), 1).astype(F32)
    gmax = jnp.max(gl, axis=-1, keepdims=True)
    g_sel = jnp.min(jnp.where(gl == gmax, lane_g, big), axis=-1, keepdims=True)
    g_prob = 1.0 / jnp.sum(jnp.exp(gl - gmax), axis=-1, keepdims=True)

    lane_i = lax.broadcasted_iota(jnp.int32, (tm, N_EXPERTS), 1)
    lane_e = lane_i.astype(F32)
    lane_grp = (lane_i // EXP_PER_GROUP).astype(F32)
    v1 = jnp.where(lane_grp == g_sel, el, -jnp.inf)
    t1 = jnp.max(v1, axis=-1, keepdims=True)
    i1 = jnp.min(jnp.where(v1 == t1, lane_e, big), axis=-1, keepdims=True)
    v2 = jnp.where(lane_e == i1, -jnp.inf, v1)
    t2 = jnp.max(v2, axis=-1, keepdims=True)
    i2 = jnp.min(jnp.where(v2 == t2, lane_e, big), axis=-1, keepdims=True)
    e21 = jnp.exp(t2 - t1)
    w1 = 1.0 / (1.0 + e21)
    w2 = e21 * w1
    comb = (jnp.where(lane_e == i1, w1, 0.0) + jnp.where(lane_e == i2, w2, 0.0)) * g_prob
    tail = jnp.zeros((tm, R_PAD - 3 * N_EXPERTS), BF16)
    h2_o[0, :, d:] = jnp.concatenate(list(_split3(comb)) + [tail], axis=1)

    lane_p = lax.broadcasted_iota(jnp.int32, (tm, R_PAD), 1)
    onehot = jnp.where(lane_p.astype(F32) == g_sel, 1.0, 0.0)
    ri = lax.broadcasted_iota(jnp.int32, (tm, tm), 0)
    ci = lax.broadcasted_iota(jnp.int32, (tm, tm), 1)
    before = jnp.where(ri > ci, 1.0, 0.0).astype(BF16)
    rank = jnp.sum(_dot(before, onehot.astype(BF16)) * onehot, axis=-1, keepdims=True)
    cnt_o[0, 0] = jnp.broadcast_to(jnp.sum(onehot, axis=0, keepdims=True), (8, R_PAD))
    fields = jnp.where(lane_p == 0, g_sel, jnp.where(lane_p == 1, rank, 0.0))
    idx_o[0] = fields[:, :8].astype(jnp.int32)
    idxt_o[0] = fields.T[:8, :].astype(jnp.int32)


def _experts_kernel(st_ref, h2_ref, idx_ref, idxt_ref, w1_ref, w3_ref, w2_ref, o_ref, hs_s, ys_s, *, tb):
    l = h2_ref.shape[1]
    d = o_ref.shape[2]
    nblk = l // tb
    ch = MOE_CHUNK
    bi = pl.program_id(0)
    g = pl.program_id(1)

    @pl.when(g == 0)
    def _():
        ys_s[...] = jnp.zeros_like(ys_s)

    def group_base(gg):
        return (bi * N_GROUPS + gg) * (nblk + 1)

    def group_offset(upto):
        off = 0
        for gg in range(N_GROUPS - 1):
            padded = ((st_ref[group_base(gg) + nblk] + ch - 1) // ch) * ch
            off = off + jnp.where(gg < upto, padded, 0)
        return off

    base = group_base(g)
    cnt = st_ref[base + nblk]
    goff = group_offset(g)

    def chunk(lo, ch):
        sub_iota = lax.broadcasted_iota(jnp.int32, (ch, tb), 0)
        lane_e = lax.broadcasted_iota(jnp.int32, (ch, N_EXPERTS), 1)
        hs_s[:ch] = jnp.zeros((ch, hs_s.shape[1]), F32)
        for k in range(nblk):
            s_k = st_ref[base + k]
            e_k = st_ref[base + k + 1]
            rows = slice(k * tb, (k + 1) * tb)

            @pl.when(jnp.logical_and(s_k < lo + ch, e_k > lo))
            def _(s_k=s_k, rows=rows):
                it = idxt_ref[0, :, rows]
                pos = jnp.where(it[0:1] == g, it[1:2] + (s_k - lo), -1)
                p = jnp.where(sub_iota == pos, 1.0, 0.0).astype(BF16)
                hs_s[:ch] += _dot(p, h2_ref[0, rows, :])

        hsb = hs_s[:ch, :d].astype(BF16)
        cs = (hs_s[:ch, d:d + N_EXPERTS] + hs_s[:ch, d + N_EXPERTS:d + 2 * N_EXPERTS]
              + hs_s[:ch, d + 2 * N_EXPERTS:d + 3 * N_EXPERTS])
        y = jnp.zeros((ch, d), F32)
        for e in range(EXP_PER_GROUP):
            ce = jnp.sum(jnp.where(lane_e == g * EXP_PER_GROUP + e, cs, 0.0), axis=-1, keepdims=True)
            hid = (_silu(_dot(hsb, w1_ref[e])) * _dot(hsb, w3_ref[e]) * ce).astype(BF16)
            y = y + _dot(hid, w2_ref[e])
        ys_s[pl.ds(pl.multiple_of(goff + lo, ch), ch), :] = y.astype(BF16)

    half = ch // 2
    nfull = cnt // ch
    rem = cnt - nfull * ch
    nloop = nfull + jnp.where(rem > half, 1, 0)

    def chunk_body(c, carry):
        chunk(c * ch, ch)
        return carry

    lax.fori_loop(0, nloop, chunk_body, 0)

    @pl.when(jnp.logical_and(rem > 0, rem <= half))
    def _():
        chunk(nfull * ch, half)

    @pl.when(g == N_GROUPS - 1)
    def _():
        lane_w = lax.broadcasted_iota(jnp.int32, (tb, 2 * ch), 1)
        for k in range(nblk):
            rows = slice(k * tb, (k + 1) * tb)
            ic = idx_ref[0, rows, :]
            acc = jnp.zeros((tb, d), F32)
            for gg in range(N_GROUPS):
                s_k = st_ref[group_base(gg) + k]
                win = (s_k // ch) * ch
                pos = jnp.where(ic[:, 0:1] == gg, ic[:, 1:2] + (s_k - win), -1)
                q = jnp.where(lane_w == pos, 1.0, 0.0).astype(BF16)
                start = pl.multiple_of(group_offset(gg) + win, ch)
                acc = acc + _dot(q, ys_s[pl.ds(start, 2 * ch), :])
            o_ref[0, rows, :] = acc.astype(BF16)


def _experts_call(starts, h2, idx, idxt, w1, w3, w2, tb):
    b, l, de = h2.shape
    d = de - R_PAD
    whole = lambda width: pl.BlockSpec((1, l, width), lambda bi, g, st: (bi, 0, 0))
    grid_spec = pltpu.PrefetchScalarGridSpec(
        num_scalar_prefetch=1, grid=(b, N_GROUPS),
        in_specs=[pl.BlockSpec((1, l, de), lambda bi, g, st: (bi, 0, 0), pipeline_mode=pl.Buffered(1)),
                  whole(8), pl.BlockSpec((1, 8, l), lambda bi, g, st: (bi, 0, 0)),
                  pl.BlockSpec((EXP_PER_GROUP, d, D_EXPERT), lambda bi, g, st: (g, 0, 0)),
                  pl.BlockSpec((EXP_PER_GROUP, d, D_EXPERT), lambda bi, g, st: (g, 0, 0)),
                  pl.BlockSpec((EXP_PER_GROUP, D_EXPERT, d), lambda bi, g, st: (g, 0, 0))],
        out_specs=whole(d),
        scratch_shapes=[pltpu.VMEM((MOE_CHUNK, de), F32),
                        pltpu.VMEM((l + (N_GROUPS + 1) * MOE_CHUNK, d), BF16)])
    return pl.pallas_call(
        functools.partial(_experts_kernel, tb=tb), grid_spec=grid_spec,
        out_shape=jax.ShapeDtypeStruct((b, l, d), BF16),
        compiler_params=_params("parallel", "arbitrary"), name="experts",
    )(starts, h2, idx, idxt, w1, w3, w2)


def _group_starts(cnt):
    c = cnt[:, :, 0, :N_GROUPS].astype(jnp.int32)
    s = jnp.cumsum(c, axis=1)
    s = jnp.concatenate([jnp.zeros_like(s[:, :1]), s], axis=1)
    return jnp.transpose(s, (0, 2, 1)).reshape(-1)


def _final_kernel(x_ref, mod_ref, f_ref, g_ref, o_ref):
    o_ref[0] = _rms(x_ref[0] + mod_ref[0][5:6] * f_ref[0].astype(F32), g_ref[...])


def _final_call(xs, mod, f, g, lc, tm):
    b, l, d = xs.shape
    off = lc // tm
    lat = pl.BlockSpec((1, tm, d), lambda bi, j: (bi, j + off, 0))
    return pl.pallas_call(
        _final_kernel, grid=(b, (l - lc) // tm),
        in_specs=[lat, pl.BlockSpec((1, 6, d), lambda bi, j: (bi, 0, 0)), lat, _const_spec(g.shape)],
        out_specs=pl.BlockSpec((1, tm, d), lambda bi, j: (bi, j, 0)),
        out_shape=jax.ShapeDtypeStruct((b, l - lc, d), F32),
        compiler_params=_params("parallel", "parallel"), name="final_norm",
    )(xs, mod, f, g)


def _rope_tables(t_len, lc):
    half = A_ROPE // 2
    rows = t_len // GRID_W
    r = jnp.repeat(jnp.arange(rows, dtype=F32), GRID_W)
    col = jnp.tile(jnp.arange(GRID_W, dtype=F32), rows)
    inv = ROPE_THETA ** (-jnp.arange(0, half, 2, dtype=F32) / half)
    ang = jnp.concatenate([r[:, None] * inv, col[:, None] * inv], axis=-1)
    cos = jnp.concatenate([jnp.ones((lc, half), F32), jnp.cos(ang)], axis=0)
    sin = jnp.concatenate([jnp.zeros((lc, half), F32), jnp.sin(ang)], axis=0)
    l = lc + t_len
    ones = jnp.ones((l, A_NOPE), F32)
    zeros = jnp.zeros((l, A_NOPE), F32)
    tail1 = jnp.ones((l, A_PAD - A_NOPE - A_ROPE), F32)
    tail0 = jnp.zeros((l, A_PAD - A_NOPE - A_ROPE), F32)
    zh = jnp.zeros((l, half), F32)
    cos_t = jnp.concatenate([ones, cos, cos, tail1], axis=-1)
    sina_t = jnp.concatenate([zeros, zh, sin, tail0], axis=-1)
    sinb_t = jnp.concatenate([zeros, -sin, zh, tail0], axis=-1)
    return cos_t, sina_t, sinb_t


def _layer_weights(l, w_in, m_gate_b, a_qnorm, a_wuq, a_kvnorm, a_wukv, g_ws, g_bs, g_vnorm,
                   w_pa, w_pb, w_pc, w_out):
    d = w_in.shape[1]
    wi = w_in[l]
    o = 0

    def take(n):
        nonlocal o
        s = wi[:, o:o + n]
        o += n
        return s

    mq, mk, mv, mo, mg = take(M_WIDTH), take(M_WIDTH), take(M_WIDTH), take(M_WIDTH), take(4 * M_HEADS)
    aq, akv, akr = take(A_QRANK), take(A_KVRANK), take(A_ROPE)
    gu, gv = take(G_WIDTH), take(G_WIDTH)
    br = take(3 * d)
    nh = M_HEADS
    gb = m_gate_b[l]
    mgo = jnp.concatenate([mg[:, :nh], mg[:, 2 * nh:3 * nh], mg[:, nh:2 * nh], mg[:, 3 * nh:]], axis=1)
    gbo = jnp.concatenate([gb[:nh], gb[2 * nh:3 * nh], gb[nh:2 * nh], gb[3 * nh:]])
    akr_pad = jnp.concatenate([jnp.zeros((d, A_NOPE), F32), akr,
                               jnp.zeros((d, A_PAD - A_NOPE - A_ROPE), F32)], axis=1)
    wuq = a_wuq[l].reshape(A_QRANK, A_HEADS, A_NOPE + A_ROPE)
    wuq = jnp.pad(wuq, ((0, 0), (0, 0), (0, A_PAD - A_NOPE - A_ROPE))).reshape(A_QRANK, A_HEADS * A_PAD)
    wukv = a_wukv[l].reshape(A_KVRANK, A_HEADS, A_NOPE + A_VDIM)
    wuk = jnp.pad(wukv[:, :, :A_NOPE], ((0, 0), (0, 0), (0, A_PAD - A_NOPE))).reshape(A_KVRANK, A_HEADS * A_PAD)
    wuv = jnp.pad(wukv[:, :, A_NOPE:], ((0, 0), (0, 0), (0, A_PAD - A_VDIM))).reshape(A_KVRANK, A_HEADS * A_PAD)
    vone = jnp.tile(jnp.concatenate([jnp.zeros((A_VDIM,), F32), jnp.ones((A_PAD - A_VDIM,), F32)]),
                    A_HEADS).reshape(1, A_HEADS * A_PAD)
    gbs = jnp.repeat(g_bs[l].T, G_DG, axis=1)
    return dict(
        wqk=jnp.concatenate([mq, mk], 1).astype(BF16), wvo=jnp.concatenate([mv, mo], 1).astype(BF16),
        wgt=jnp.pad(mgo, ((0, 0), (0, G_PAD - 4 * nh))).astype(BF16),
        gbt=jnp.pad(gbo, (0, G_PAD - 4 * nh)).reshape(1, G_PAD),
        wa=jnp.concatenate([aq, akv, akr_pad], 1).astype(BF16),
        wg=jnp.concatenate([gu, gv], 1).astype(BF16), wbr=br.astype(BF16),
        aqn=a_qnorm[l].reshape(1, -1), akvn=a_kvnorm[l].reshape(1, -1),
        wuq=wuq.astype(BF16), wuk=wuk.astype(BF16), wuv=wuv.astype(BF16), vone=vone,
        gvn=g_vnorm[l].reshape(1, -1), gws=g_ws[l].astype(BF16), gbs=gbs,
        wpa=w_pa[l].astype(BF16), wpb=w_pb[l].astype(BF16), wpc=w_pc[l].astype(BF16),
        wout=w_out[l].astype(BF16))


def _router_weights(r_group, r_group_b, r_expert, r_expert_b):
    d = r_group.shape[0]
    pad = R_SEG - N_EXPERTS - N_GROUPS
    r = jnp.concatenate([r_expert, r_group, jnp.zeros((d, pad), F32)], axis=1)
    r3 = jnp.concatenate(list(_split3(r)) + [jnp.zeros((d, R_PAD - 3 * R_SEG), BF16)], axis=1)
    rb = jnp.concatenate([r_expert_b, r_group_b, jnp.zeros((R_PAD - N_EXPERTS - N_GROUPS,), F32)])
    return r3, rb.reshape(1, R_PAD)


def _tile(n, lc, candidates):
    for t in candidates:
        if n % t == 0 and lc % t == 0:
            return t
    raise ValueError("sequence lengths must be multiples of 128")


def kernel(x, c, ctx, c_ctx, w_ada, b_ada, norm1, norm2, final_norm, w_in, m_conv, m_gate_b, m_norm, a_qnorm, a_wuq, a_kvnorm, a_wukv, g_ws, g_bs, g_vnorm, w_pa, w_pb, w_pc, w_out, r_group, r_group_b, r_expert, r_expert_b, e_w1, e_w3, e_w2):
    b, t_len, d = x.shape
    lc = ctx.shape[1]
    l = lc + t_len
    depth = w_in.shape[0]
    tm = _tile(l, lc, (256, 128))

    xs = jnp.concatenate([ctx, x], axis=1)
    cv = jnp.concatenate([c, c_ctx[None, :]], axis=0)
    mod_all = _ada_call(cv, w_ada, b_ada).reshape(depth, b + 1, 6, d)
    tabs = _rope_tables(t_len, lc)

    f = None
    mod_prev = None
    for li in range(depth):
        last = li == depth - 1
        mod = mod_all[li]
        w = _layer_weights(li, w_in, m_gate_b, a_qnorm, a_wuq, a_kvnorm, a_wukv, g_ws, g_bs, g_vnorm,
                           w_pa, w_pb, w_pc, w_out)
        outs = _inproj_call(xs, mod, norm1[li].reshape(1, d), w, tabs, lc, tm, f, mod_prev)
        qk, vo, gi, gf, gr, q, k, v, yc, br = outs[:10]
        if f is not None:
            xs = outs[10]
        ya = _mlstm_call(qk, vo, gi, gf, gr, m_conv[li], m_norm[li].reshape(1, -1), lc)
        yb = _attn_call(q, k, v, lc, tm, not last)
        r3, rb = _router_weights(r_group[li], r_group_b[li], r_expert[li], r_expert_b[li])
        xs, h2, idx, idxt, cnt = _merge_call(xs, mod, ya, yb, yc, br, w, norm2[li].reshape(1, d),
                                             r3, rb, lc, tm)
        f = _experts_call(_group_starts(cnt), h2, idx, idxt, e_w1[li].astype(BF16),
                          e_w3[li].astype(BF16), e_w2[li].astype(BF16), tm)
        mod_prev = mod
    return _final_call(xs, mod_prev, f, final_norm.reshape(1, d), lc, tm)
```

```python
import functools

import jax
import jax.numpy as jnp
from jax import lax
from jax.experimental import pallas as pl
from jax.experimental.pallas import tpu as pltpu

F32 = jnp.float32
BF16 = jnp.bfloat16

EPS = 1e-6
GRID_W = 64
ROPE_THETA = 10000.0

M_HEADS = 4
M_DH = 128
M_WIDTH = M_HEADS * M_DH
M_CHUNK = 128
G_PAD = 128

A_HEADS = 8
A_NOPE = 64
A_ROPE = 32
A_VDIM = 64
A_QRANK = 384
A_KVRANK = 256
A_WIDTH = A_HEADS * A_VDIM
A_PAD = 128
A_HPS = 4
ATT_SCALE = (A_NOPE + A_ROPE) ** -0.5
LOG2E = 1.4426950408889634

G_GROUPS = 4
G_CHUNK = 128
G_WIDTH = 512
G_DG = G_WIDTH // G_GROUPS

N_GROUPS = 4
EXP_PER_GROUP = 4
N_EXPERTS = N_GROUPS * EXP_PER_GROUP
D_EXPERT = 512
R_PAD = 128
R_SEG = 32
MOE_CHUNK = 256

VMEM_LIMIT = 56 * 1024 * 1024


def _dot(a, b):
    return jnp.dot(a, b, preferred_element_type=F32)


def _dot_nt(a, b):
    return lax.dot_general(a, b, (((1,), (1,)), ((), ())), preferred_element_type=F32)


def _dot_tn(a, b):
    return lax.dot_general(a, b, (((0,), (0,)), ((), ())), preferred_element_type=F32)


def _split3(x):
    hi = x.astype(BF16)
    r = x - hi.astype(F32)
    mid = r.astype(BF16)
    lo = (r - mid.astype(F32)).astype(BF16)
    return hi, mid, lo


def _sigmoid(x):
    return 1.0 / (1.0 + jnp.exp(-x))


def _silu(x):
    return x * _sigmoid(x)


def _log_sigmoid(x):
    return jnp.minimum(x, 0.0) - jnp.log1p(jnp.exp(-jnp.abs(x)))


def _gelu(x):
    return 0.5 * x * (1.0 + lax.erf(x * (2.0 ** -0.5)))


def _rms(x, g):
    return x * lax.rsqrt(jnp.mean(x * x, axis=-1, keepdims=True) + EPS) * g


def _params(*sem):
    return pltpu.CompilerParams(dimension_semantics=sem, vmem_limit_bytes=VMEM_LIMIT)


def _const_spec(shape):
    nd = len(shape)
    return pl.BlockSpec(shape, lambda *_: (0,) * nd, pipeline_mode=pl.Buffered(1))


def _ada_kernel(cv_ref, w_ref, b_ref, o_ref):
    s = _silu(cv_ref[...])
    o_ref[0] = _dot(s.astype(BF16), w_ref[0].astype(BF16)) + b_ref[0]


def _ada_call(cv, w_ada, b_ada):
    depth, d, n6 = w_ada.shape
    rows = cv.shape[0]
    tn = n6 // 4
    return pl.pallas_call(
        _ada_kernel,
        grid=(depth, n6 // tn),
        in_specs=[pl.BlockSpec((rows, d), lambda l, j: (0, 0)),
                  pl.BlockSpec((1, d, tn), lambda l, j: (l, 0, j)),
                  pl.BlockSpec((1, 1, tn), lambda l, j: (l, 0, j))],
        out_specs=pl.BlockSpec((1, rows, tn), lambda l, j: (l, 0, j)),
        out_shape=jax.ShapeDtypeStruct((depth, rows, n6), F32),
        compiler_params=_params("parallel", "parallel"),
        name="ada",
    )(cv, w_ada, b_ada.reshape(depth, 1, n6))


def _inproj_kernel(*refs, has_f):
    if has_f:
        f_ref, modp_ref, x_o = refs[0], refs[1], refs[-1]
        refs = refs[2:-1]
    (x_ref, mod_ref, n1_ref, wqk_ref, wvo_ref, wgt_ref, gbt_ref,
     wa_ref, wg_ref, wbr_ref, aqn_ref, akvn_ref, wuq_ref, wuk_ref, wuv_ref, vone_ref,
     cos_ref, sina_ref, sinb_ref, gvn_ref, gws_ref, gbs_ref,
     qk_o, vo_o, gi_o, gf_o, gr_o, q_o, k_o, v_o, yc_o, br_o) = refs
    tm = x_ref.shape[1]
    mod = mod_ref[0]
    x = x_ref[0]
    if has_f:
        x = x + modp_ref[0][5:6] * f_ref[0].astype(F32)
        x_o[0] = x
    h = _rms(x, n1_ref[...]) * (1.0 + mod[1:2]) + mod[0:1]
    hb = h.astype(BF16)

    qk_o[0] = _dot(hb, wqk_ref[...])
    vo_o[0] = _dot(hb, wvo_ref[...]).astype(BF16)
    ng = gi_o.shape[2]
    gates = _dot(hb, wgt_ref[...]) + gbt_ref[...]
    gi_o[0] = gates[:, :ng]
    gf_o[0] = pltpu.roll(gates, gates.shape[1] - ng, 1)[:, :ng]
    gr_o[0] = gates.T[:2 * ng, :]

    za = _dot(hb, wa_ref[...])
    aqn = _rms(za[:, :A_QRANK], aqn_ref[...]).astype(BF16)
    akvn = _rms(za[:, A_QRANK:A_QRANK + A_KVRANK], akvn_ref[...]).astype(BF16)
    cos = cos_ref[...]
    sina = sina_ref[...]
    sinb = sinb_ref[...]
    half = A_ROPE // 2

    def rope(t):
        return t * cos + pltpu.roll(t, half, 1) * sina + pltpu.roll(t, A_PAD - half, 1) * sinb

    kr = rope(za[:, A_QRANK + A_KVRANK:])
    qp = _dot(aqn, wuq_ref[...])
    kp = _dot(akvn, wuk_ref[...])
    for hh in range(A_HEADS):
        sl = slice(hh * A_PAD, (hh + 1) * A_PAD)
        q_o[0, :, sl] = (rope(qp[:, sl]) * (ATT_SCALE * LOG2E)).astype(BF16)
        k_o[0, :, sl] = (kp[:, sl] + kr).astype(BF16)
    v_o[0] = (_dot(akvn, wuv_ref[...]) + vone_ref[...]).astype(BF16)

    zg = _dot(hb, wg_ref[...])
    gu = _gelu(zg[:, :G_WIDTH])
    gv = _gelu(zg[:, G_WIDTH:])
    gvn = gvn_ref[...]
    bias = gbs_ref[...]
    for g in range(G_GROUPS):
        sl = slice(g * G_DG, (g + 1) * G_DG)
        xn = _rms(gv[:, sl], gvn[:, sl]).astype(BF16)
        ws = gws_ref[g]
        for ci in range(tm // G_CHUNK):
            r = slice(ci * G_CHUNK, (ci + 1) * G_CHUNK)
            sg = _dot(ws, xn[r]) + bias[:, sl]
            yc_o[0, r, sl] = (gu[r, sl] * sg).astype(BF16)

    br_o[0] = _sigmoid(_dot(hb, wbr_ref[...])).astype(BF16)


def _inproj_call(xs, mod, n1, w, tabs, lc, tm, f=None, mod_prev=None):
    b, l, d = xs.shape
    nct = lc // tm
    has_f = f is not None
    tok = lambda width: pl.BlockSpec((1, tm, width), lambda bi, j: (bi, j, 0))
    modspec = pl.BlockSpec((1, 6, d), lambda bi, j: (jnp.where(j < nct, b, bi), 0, 0))
    tab = pl.BlockSpec((tm, A_PAD), lambda bi, j: (j, 0))
    consts = [n1, w["wqk"], w["wvo"], w["wgt"], w["gbt"], w["wa"], w["wg"], w["wbr"],
              w["aqn"], w["akvn"], w["wuq"], w["wuk"], w["wuv"], w["vone"]]
    consts2 = [w["gvn"], w["gws"], w["gbs"]]
    in_specs = ([tok(d), modspec] + [_const_spec(a.shape) for a in consts] + [tab, tab, tab]
                + [_const_spec(a.shape) for a in consts2])
    args = [xs, mod, *consts, *tabs, *consts2]
    ng = 4 * M_HEADS
    out_shape = [jax.ShapeDtypeStruct((b, l, 2 * M_WIDTH), F32),
                 jax.ShapeDtypeStruct((b, l, 2 * M_WIDTH), BF16),
                 jax.ShapeDtypeStruct((b, l, ng // 2), F32),
                 jax.ShapeDtypeStruct((b, l, ng // 2), F32),
                 jax.ShapeDtypeStruct((b, ng, l), F32),
                 jax.ShapeDtypeStruct((b, l, A_HEADS * A_PAD), BF16),
                 jax.ShapeDtypeStruct((b, l, A_HEADS * A_PAD), BF16),
                 jax.ShapeDtypeStruct((b, l, A_HEADS * A_PAD), BF16),
                 jax.ShapeDtypeStruct((b, l, G_WIDTH), BF16),
                 jax.ShapeDtypeStruct((b, l, 3 * d), BF16)]
    out_specs = [tok(2 * M_WIDTH), tok(2 * M_WIDTH), tok(ng // 2), tok(ng // 2),
                 pl.BlockSpec((1, ng, tm), lambda bi, j: (bi, 0, j)),
                 tok(A_HEADS * A_PAD), tok(A_HEADS * A_PAD), tok(A_HEADS * A_PAD), tok(G_WIDTH), tok(3 * d)]
    if has_f:
        in_specs = [tok(d), modspec] + in_specs
        args = [f, mod_prev] + args
        out_shape.append(jax.ShapeDtypeStruct((b, l, d), F32))
        out_specs.append(tok(d))
    return pl.pallas_call(
        functools.partial(_inproj_kernel, has_f=has_f), grid=(b, l // tm), in_specs=in_specs,
        out_specs=out_specs, out_shape=out_shape,
        compiler_params=_params("parallel", "parallel"), name="inproj",
    )(*args)


def _scan(x, op, fill, axis, reverse):
    n = x.shape[axis]
    idx = lax.broadcasted_iota(jnp.int32, x.shape, axis)
    k = 1
    while k < n:
        if reverse:
            x = op(x, jnp.where(idx >= n - k, fill, pltpu.roll(x, n - k, axis)))
        else:
            x = op(x, jnp.where(idx < k, fill, pltpu.roll(x, k, axis)))
        k *= 2
    return x


def _mlstm_kernel(qk_ref, vo_ref, gi_ref, gf_ref, gr_ref, conv_ref, mnorm_ref, ya_ref,
                  q_s, kt_s, h_s, bc_s, ml_s, dl_s, br_s, cn_s, m_s, s_s, p_s, qcn_s, u_s, *, lc):
    l = qk_ref.shape[1]
    ch = M_CHUNK
    nc = l // ch
    ncc = lc // ch
    nh = M_HEADS
    ng = 2 * nh
    w = conv_ref[...]
    row = lax.broadcasted_iota(jnp.int32, (ch, 1), 0)

    def conv_chunk(j):
        r0 = pl.multiple_of(j * ch, ch)
        cur = qk_ref[0, pl.ds(r0, ch), :]
        prev8 = qk_ref[0, pl.ds(pl.multiple_of(jnp.maximum(r0 - 8, 0), 8), 8), :]
        next8 = qk_ref[0, pl.ds(pl.multiple_of(jnp.minimum(r0 + ch, l - 8), 8), 8), :]
        seg_start = jnp.logical_or(j == 0, j == ncc)
        seg_end = jnp.logical_or(j == ncc - 1, j == nc - 1)
        pe = jnp.where(seg_start, 0.0, prev8[7:8, :])
        ne = jnp.where(seg_end, 0.0, next8[0:1, :])
        xp = jnp.where(row == 0, pe, pltpu.roll(cur, 1, 0))
        xn = jnp.where(row == ch - 1, ne, pltpu.roll(cur, ch - 1, 0))
        y = _silu(xp * w[0:1] + cur * w[1:2] + xn * w[2:3])
        q_s[pl.ds(r0, ch), :] = (y[:, :M_WIDTH] * (M_DH ** -0.5)).astype(BF16)
        kt_s[:, pl.ds(r0, ch)] = y[:, M_WIDTH:].T.astype(BF16)

    ri = lax.broadcasted_iota(jnp.int32, (ch, ch), 0)
    ci = lax.broadcasted_iota(jnp.int32, (ch, ch), 1)
    lower = ri >= ci
    upper = ri <= ci
    ones_blk = jnp.ones((ch, M_DH), BF16)
    fwd_c = lax.broadcasted_iota(jnp.int32, (ch, ng), 1) < nh
    fwd_r = lax.broadcasted_iota(jnp.int32, (ng, ch), 0) < nh
    lane_c = lax.broadcasted_iota(jnp.int32, (ch, ng), 1)

    def local_chunk(j):
        r0 = pl.multiple_of(j * ch, ch)
        rows = pl.ds(r0, ch)
        lfc = _log_sigmoid(gf_ref[0, rows, :])
        gr = gr_ref[0, :, rows]
        lfr = _log_sigmoid(gr[ng:])
        pre_c = _scan(lfc, jnp.add, 0.0, 0, False)
        pre_r = _scan(lfr, jnp.add, 0.0, 1, False)
        b_c = jnp.where(fwd_c, pre_c, jnp.sum(lfc, axis=0, keepdims=True) + lfc - pre_c)
        b_r = jnp.where(fwd_r, pre_r, jnp.sum(lfr, axis=1, keepdims=True) + lfr - pre_r)
        g_c = gi_ref[0, rows, :] - b_c
        g_r = gr[:ng] - b_r
        cg_c = jnp.where(fwd_c, _scan(g_c, jnp.maximum, -jnp.inf, 0, False),
                         _scan(g_c, jnp.maximum, -jnp.inf, 0, True))
        bc_s[rows, :] = b_c
        br_s[:, rows] = b_r
        ml_s[rows, :] = b_c + cg_c
        dl = jnp.zeros((ch, ng), F32)
        for hh in range(nh):
            sl = slice(hh * M_DH, (hh + 1) * M_DH)
            s_s[hh] = _dot(q_s[rows, sl], kt_s[sl, rows])
        for hh in range(nh):
            s = s_s[hh]
            for d in range(2):
                jj = d * nh + hh
                wgt = jnp.exp(jnp.where(upper if d else lower, g_r[jj:jj + 1, :] - cg_c[:, jj:jj + 1], -jnp.inf))
                p_s[jj] = (s * wgt).astype(BF16)
        for hh in range(nh):
            sl = slice(hh * M_DH, (hh + 1) * M_DH)
            v1 = jnp.concatenate([vo_ref[0, rows, sl], ones_blk], axis=1)
            for d in range(2):
                jj = d * nh + hh
                nd = _dot(p_s[jj], v1)
                h_s[d, rows, sl] = nd[:, :M_DH]
                dl = jnp.where(lane_c == jj, nd[:, M_DH:M_DH + ng], dl)
        dl_s[rows, :] = dl

    def conv_local_body(j, carry):
        conv_chunk(j + 1)
        local_chunk(j)
        return carry

    conv_chunk(jnp.int32(0))
    lax.fori_loop(0, nc - 1, conv_local_body, 0)
    local_chunk(jnp.int32(nc - 1))

    cn_s[...] = jnp.zeros_like(cn_s)
    m_s[...] = jnp.zeros_like(m_s)
    lane_r = lax.broadcasted_iota(jnp.int32, (1, ng), 1)

    def scan_issue(r0, d):
        rows = pl.ds(r0, ch)
        gr = gr_ref[0, :, rows]
        br = br_s[:, rows]
        tot = jnp.sum(_log_sigmoid(gr[ng:]), axis=1, keepdims=True)
        scal = []
        for hh in range(nh):
            fi = d * nh + hh
            sl = slice(hh * M_DH, (hh + 1) * M_DH)
            qcn_s[fi] = _dot(q_s[rows, sl], cn_s[fi].astype(BF16))
            m_old = m_s[fi][:, 0:1]
            b_e = tot[fi:fi + 1, :]
            d_end = b_e - br[fi:fi + 1, :] + gr[fi:fi + 1, :]
            m_end = jnp.max(d_end, axis=-1, keepdims=True)
            m_new = jnp.maximum(b_e + m_old, m_end)
            ktw = (kt_s[sl, rows].astype(F32) * jnp.exp(d_end - m_end)).astype(BF16)
            v1 = jnp.concatenate([vo_ref[0, rows, sl], ones_blk], axis=1)
            u_s[fi] = _dot(ktw, v1)
            scal.append((m_old, m_new, jnp.exp(b_e + m_old - m_new), jnp.exp(m_end - m_new)))
        return scal

    def scan_finish(r0, d, scal):
        rows = pl.ds(r0, ch)
        m_row = jnp.zeros((1, ng), F32)
        for hh in range(nh):
            m_row = jnp.where(lane_r == d * nh + hh, scal[hh][0], m_row)

        inter = bc_s[rows, :] + m_row
        ml = ml_s[rows, :]
        mt = jnp.maximum(inter, ml)
        a = jnp.exp(ml - mt)
        wi = jnp.exp(inter - mt)
        qn = jnp.zeros((ch, ng), F32)
        for hh in range(nh):
            fi = d * nh + hh
            qn = jnp.where(lane_c == fi, qcn_s[fi, :, M_DH:M_DH + ng], qn)
        den = a * dl_s[rows, :] + wi * qn
        rinv = 1.0 / jnp.maximum(jnp.abs(den), jnp.exp(-mt))
        c_loc = a * rinv
        c_int = wi * rinv

        for hh in range(nh):
            fi = d * nh + hh
            sl = slice(hh * M_DH, (hh + 1) * M_DH)
            h_s[d, rows, sl] = (c_loc[:, fi:fi + 1] * h_s[d, rows, sl]
                                + c_int[:, fi:fi + 1] * qcn_s[fi, :, :M_DH])
            cn_s[fi] = scal[hh][2] * cn_s[fi] + scal[hh][3] * u_s[fi]
            m_s[fi] = jnp.broadcast_to(scal[hh][1], (1, M_DH))

    def scan_body(s, carry):
        rf = pl.multiple_of(s * ch, ch)
        rb = pl.multiple_of(jnp.where(s < ncc, ncc - 1 - s, nc - 1 - s + ncc) * ch, ch)
        sf = scan_issue(rf, 0)
        sb = scan_issue(rb, 1)
        scan_finish(rf, 0, sf)
        scan_finish(rb, 1, sb)
        return carry

    lax.fori_loop(0, nc, scan_body, 0)

    mnorm = mnorm_ref[...]

    def out_body(j, carry):
        r0 = pl.multiple_of(j * ch, ch)
        hsum = h_s[0, pl.ds(r0, ch), :] + h_s[1, pl.ds(r0, ch), :]
        og = _sigmoid(vo_ref[0, pl.ds(r0, ch), M_WIDTH:].astype(F32))
        for hh in range(M_HEADS):
            sl = slice(hh * M_DH, (hh + 1) * M_DH)
            ya_ref[0, pl.ds(r0, ch), sl] = (_rms(hsum[:, sl], mnorm[:, sl]) * og[:, sl]).astype(BF16)
        return carry

    lax.fori_loop(0, nc, out_body, 0)


def _mlstm_call(qk, vo, gi, gf, gr, conv, mnorm, lc):
    b, l, _ = qk.shape
    ng = 2 * M_HEADS
    return pl.pallas_call(
        functools.partial(_mlstm_kernel, lc=lc),
        grid=(b,),
        in_specs=[pl.BlockSpec((1, l, 2 * M_WIDTH), lambda bi: (bi, 0, 0), pipeline_mode=pl.Buffered(1)),
                  pl.BlockSpec((1, l, 2 * M_WIDTH), lambda bi: (bi, 0, 0)),
                  pl.BlockSpec((1, l, ng), lambda bi: (bi, 0, 0)),
                  pl.BlockSpec((1, l, ng), lambda bi: (bi, 0, 0)),
                  pl.BlockSpec((1, 2 * ng, l), lambda bi: (bi, 0, 0)),
                  _const_spec(conv.shape), _const_spec(mnorm.shape)],
        out_specs=pl.BlockSpec((1, l, M_WIDTH), lambda bi: (bi, 0, 0)),
        out_shape=jax.ShapeDtypeStruct((b, l, M_WIDTH), BF16),
        scratch_shapes=[pltpu.VMEM((l, M_WIDTH), BF16),
                        pltpu.VMEM((M_WIDTH, l), BF16),
                        pltpu.VMEM((2, l, M_WIDTH), F32),
                        pltpu.VMEM((l, ng), F32),
                        pltpu.VMEM((l, ng), F32),
                        pltpu.VMEM((l, ng), F32),
                        pltpu.VMEM((ng, l), F32),
                        pltpu.VMEM((2 * M_HEADS, M_DH, 2 * M_DH), F32),
                        pltpu.VMEM((2 * M_HEADS, 1, M_DH), F32),
                        pltpu.VMEM((M_HEADS, M_CHUNK, M_CHUNK), F32),
                        pltpu.VMEM((2 * M_HEADS, M_CHUNK, M_CHUNK), BF16),
                        pltpu.VMEM((2 * M_HEADS, M_CHUNK, 2 * M_DH), F32),
                        pltpu.VMEM((2 * M_HEADS, M_DH, 2 * M_DH), F32)],
        compiler_params=_params("parallel"), name="mlstm",
    )(qk, vo, gi, gf, gr, conv, mnorm)


def _attn_kernel(q_ref, k_ref, v_ref, o_ref, *, lc, ctx_out):
    tq = q_ref.shape[1]
    l = k_ref.shape[1]
    qi = pl.program_id(2)
    nct = lc // tq
    lane = lax.broadcasted_iota(jnp.int32, (tq, 2 * A_VDIM), 1)

    def run(klen):
        outs = []
        for hh in range(A_HPS):
            sl = slice(hh * A_PAD, (hh + 1) * A_PAD)
            s = _dot_nt(q_ref[0, :, sl], k_ref[0, :klen, sl])
            p = jnp.exp2((s - jnp.max(s, axis=-1, keepdims=True)).astype(BF16))
            nd = _dot(p, v_ref[0, :klen, sl])
            outs.append(nd / pltpu.roll(nd, A_VDIM, 1))
        for pp in range(A_HPS // 2):
            o_ref[0, :, pp * A_PAD:(pp + 1) * A_PAD] = jnp.where(
                lane < A_VDIM, outs[2 * pp], pltpu.roll(outs[2 * pp + 1], A_VDIM, 1)).astype(BF16)

    @pl.when(qi >= nct)
    def _():
        run(l)

    @pl.when(qi < nct)
    def _():
        if ctx_out:
            run(lc)
        else:
            o_ref[...] = jnp.zeros_like(o_ref)


def _attn_call(q, k, v, lc, tq, ctx_out):
    b, l, _ = q.shape
    return pl.pallas_call(
        functools.partial(_attn_kernel, lc=lc, ctx_out=ctx_out),
        grid=(b, A_HEADS // A_HPS, l // tq),
        in_specs=[pl.BlockSpec((1, tq, A_HPS * A_PAD), lambda bi, p, qi: (bi, qi, p)),
                  pl.BlockSpec((1, l, A_HPS * A_PAD), lambda bi, p, qi: (bi, 0, p)),
                  pl.BlockSpec((1, l, A_HPS * A_PAD), lambda bi, p, qi: (bi, 0, p))],
        out_specs=pl.BlockSpec((1, tq, A_HPS * A_VDIM), lambda bi, p, qi: (bi, qi, p)),
        out_shape=jax.ShapeDtypeStruct((b, l, A_WIDTH), BF16),
        compiler_params=_params("parallel", "parallel", "arbitrary"), name="attn",
    )(q, k, v)


def _merge_kernel(x_ref, mod_ref, ya_ref, yb_ref, yc_ref, br_ref, wpa_ref, wpb_ref, wpc_ref, wout_ref,
                  n2_ref, r_ref, rb_ref, o_ref, h2_o, idx_o, idxt_o, cnt_o, *, nct, skip_ctx):
    d = x_ref.shape[2]

    def compute():
        br = br_ref[0]
        mod = mod_ref[0]
        y = (br[:, :d].astype(F32) * _dot(ya_ref[0], wpa_ref[...])
             + br[:, d:2 * d].astype(F32) * _dot(yb_ref[0], wpb_ref[...])
             + br[:, 2 * d:].astype(F32) * _dot(yc_ref[0], wpc_ref[...]))
        out = _dot(y.astype(BF16), wout_ref[...])
        x = x_ref[0] + mod[2:3] * out
        o_ref[0] = x
        _route(x, mod, n2_ref, r_ref, rb_ref, h2_o, idx_o, idxt_o, cnt_o)

    if not skip_ctx:
        compute()
        return

    pl.when(pl.program_id(1) >= nct)(compute)

    @pl.when(pl.program_id(1) < nct)
    def _():
        o_ref[...] = jnp.zeros_like(o_ref)
        h2_o[...] = jnp.zeros_like(h2_o)
        idx_o[...] = jnp.full(idx_o.shape, -1, jnp.int32)
        idxt_o[...] = jnp.full(idxt_o.shape, -1, jnp.int32)
        cnt_o[...] = jnp.zeros_like(cnt_o)


def _merge_call(xs, mod, ya, yb, yc, br, w, n2, r3, rb, lc, tm, skip_ctx):
    b, l, d = xs.shape
    nct = lc // tm
    tok = lambda width: pl.BlockSpec((1, tm, width), lambda bi, j: (bi, j, 0))
    consts = [w["wpa"], w["wpb"], w["wpc"], w["wout"], n2, r3, rb]
    return pl.pallas_call(
        functools.partial(_merge_kernel, nct=nct, skip_ctx=skip_ctx), grid=(b, l // tm),
        in_specs=[tok(d), pl.BlockSpec((1, 6, d), lambda bi, j: (jnp.where(j < nct, b, bi), 0, 0)),
                  tok(M_WIDTH), tok(A_WIDTH), tok(G_WIDTH), tok(3 * d)] + [_const_spec(a.shape) for a in consts],
        out_specs=[tok(d), tok(d + R_PAD), tok(8), pl.BlockSpec((1, 8, tm), lambda bi, j: (bi, 0, j)),
                   pl.BlockSpec((1, 1, 8, R_PAD), lambda bi, j: (bi, j, 0, 0))],
        out_shape=[jax.ShapeDtypeStruct((b, l, d), F32),
                   jax.ShapeDtypeStruct((b, l, d + R_PAD), BF16),
                   jax.ShapeDtypeStruct((b, l, 8), jnp.int32), jax.ShapeDtypeStruct((b, 8, l), jnp.int32),
                   jax.ShapeDtypeStruct((b, l // tm, 8, R_PAD), F32)],
        compiler_params=_params("parallel", "parallel"), name="merge",
    )(xs, mod, ya, yb, yc, br, *consts)


def _route(x, mod, n2_ref, r_ref, rb_ref, h2_o, idx_o, idxt_o, cnt_o):
    tm, d = x.shape
    h2 = _rms(x, n2_ref[...]) * (1.0 + mod[4:5]) + mod[3:4]
    h2_o[0, :, :d] = h2.astype(BF16)

    r = r_ref[...]
    pp = sum(_dot(piece, r) for piece in _split3(h2))
    logits = pp + pltpu.roll(pp, R_PAD - R_SEG, 1) + pltpu.roll(pp, R_PAD - 2 * R_SEG, 1) + rb_ref[...]
    el = logits[:, :N_EXPERTS]
    gl = logits[:, N_EXPERTS:N_EXPERTS + N_GROUPS]
    big = 1e9

    lane_g = lax.broadcasted_iota(jnp.int32, (tm, N_GROUPS), 1).astype(F32)
    gmax = jnp.max(gl, axis=-1, keepdims=True)
    g_sel = jnp.min(jnp.where(gl == gmax, lane_g, big), axis=-1, keepdims=True)
    g_prob = 1.0 / jnp.sum(jnp.exp(gl - gmax), axis=-1, keepdims=True)

    lane_i = lax.broadcasted_iota(jnp.int32, (tm, N_EXPERTS), 1)
    lane_e = lane_i.astype(F32)
    lane_grp = (lane_i // EXP_PER_GROUP).astype(F32)
    v1 = jnp.where(lane_grp == g_sel, el, -jnp.inf)
    t1 = jnp.max(v1, axis=-1, keepdims=True)
    i1 = jnp.min(jnp.where(v1 == t1, lane_e, big), axis=-1, keepdims=True)
    v2 = jnp.where(lane_e == i1, -jnp.inf, v1)
    t2 = jnp.max(v2, axis=-1, keepdims=True)
    i2 = jnp.min(jnp.where(v2 == t2, lane_e, big), axis=-1, keepdims=True)
    e21 = jnp.exp(t2 - t1)
    w1 = 1.0 / (1.0 + e21)
    w2 = e21 * w1
    comb = (jnp.where(lane_e == i1, w1, 0.0) + jnp.where(lane_e == i2, w2, 0.0)) * g_prob
    tail = jnp.zeros((tm, R_PAD - 3 * N_EXPERTS), BF16)
    h2_o[0, :, d:] = jnp.concatenate(list(_split3(comb)) + [tail], axis=1)

    lane_p = lax.broadcasted_iota(jnp.int32, (tm, R_PAD), 1)
    onehot = jnp.where(lane_p.astype(F32) == g_sel, 1.0, 0.0)
    ri = lax.broadcasted_iota(jnp.int32, (tm, tm), 0)
    ci = lax.broadcasted_iota(jnp.int32, (tm, tm), 1)
    before = jnp.where(ri > ci, 1.0, 0.0).astype(BF16)
    rank = jnp.sum(_dot(before, onehot.astype(BF16)) * onehot, axis=-1, keepdims=True)
    cnt_o[0, 0] = jnp.broadcast_to(jnp.sum(onehot, axis=0, keepdims=True), (8, R_PAD))
    fields = jnp.where(lane_p == 0, g_sel, jnp.where(lane_p == 1, rank, 0.0))
    idx_o[0] = fields[:, :8].astype(jnp.int32)
    idxt_o[0] = fields.T[:8, :].astype(jnp.int32)


def _experts_kernel(st_ref, h2_ref, idx_ref, idxt_ref, w1_ref, w3_ref, w2_ref, o_ref, hs_s, ys_s, *, tb):
    l = h2_ref.shape[1]
    d = o_ref.shape[2]
    nblk = l // tb
    ch = MOE_CHUNK
    bi = pl.program_id(0)
    g = pl.program_id(1)

    @pl.when(g == 0)
    def _():
        ys_s[...] = jnp.zeros_like(ys_s)

    def group_base(gg):
        return (bi * N_GROUPS + gg) * (nblk + 1)

    def group_offset(upto):
        off = 0
        for gg in range(N_GROUPS - 1):
            padded = ((st_ref[group_base(gg) + nblk] + ch - 1) // ch) * ch
            off = off + jnp.where(gg < upto, padded, 0)
        return off

    base = group_base(g)
    cnt = st_ref[base + nblk]
    goff = group_offset(g)

    def chunk(lo, ch):
        sub_iota = lax.broadcasted_iota(jnp.int32, (ch, tb), 0)
        lane_e = lax.broadcasted_iota(jnp.int32, (ch, N_EXPERTS), 1)
        hs_s[:ch] = jnp.zeros((ch, hs_s.shape[1]), F32)
        for k in range(nblk):
            s_k = st_ref[base + k]
            e_k = st_ref[base + k + 1]
            rows = slice(k * tb, (k + 1) * tb)

            @pl.when(jnp.logical_and(s_k < lo + ch, e_k > lo))
            def _(s_k=s_k, rows=rows):
                it = idxt_ref[0, :, rows]
                pos = jnp.where(it[0:1] == g, it[1:2] + (s_k - lo), -1)
                p = jnp.where(sub_iota == pos, 1.0, 0.0).astype(BF16)
                hs_s[:ch] += _dot(p, h2_ref[0, rows, :])

        hsb = hs_s[:ch, :d].astype(BF16)
        cs = (hs_s[:ch, d:d + N_EXPERTS] + hs_s[:ch, d + N_EXPERTS:d + 2 * N_EXPERTS]
              + hs_s[:ch, d + 2 * N_EXPERTS:d + 3 * N_EXPERTS])
        y = jnp.zeros((ch, d), F32)
        for e in range(EXP_PER_GROUP):
            ce = jnp.sum(jnp.where(lane_e == g * EXP_PER_GROUP + e, cs, 0.0), axis=-1, keepdims=True)
            hid = (_silu(_dot(hsb, w1_ref[e])) * _dot(hsb, w3_ref[e]) * ce).astype(BF16)
            y = y + _dot(hid, w2_ref[e])
        ys_s[pl.ds(pl.multiple_of(goff + lo, ch), ch), :] = y.astype(BF16)

    half = ch // 2
    nfull = cnt // ch
    rem = cnt - nfull * ch
    nloop = nfull + jnp.where(rem > half, 1, 0)

    def chunk_body(c, carry):
        chunk(c * ch, ch)
        return carry

    lax.fori_loop(0, nloop, chunk_body, 0)

    @pl.when(jnp.logical_and(rem > 0, rem <= half))
    def _():
        chunk(nfull * ch, half)

    @pl.when(g == N_GROUPS - 1)
    def _():
        lane_w = lax.broadcasted_iota(jnp.int32, (tb, 2 * ch), 1)
        for k in range(nblk):
            rows = slice(k * tb, (k + 1) * tb)
            ic = idx_ref[0, rows, :]
            acc = jnp.zeros((tb, d), F32)
            for gg in range(N_GROUPS):
                s_k = st_ref[group_base(gg) + k]
                win = (s_k // ch) * ch
                pos = jnp.where(ic[:, 0:1] == gg, ic[:, 1:2] + (s_k - win), -1)
                q = jnp.where(lane_w == pos, 1.0, 0.0).astype(BF16)
                start = pl.multiple_of(group_offset(gg) + win, ch)
                acc = acc + _dot(q, ys_s[pl.ds(start, 2 * ch), :])
            o_ref[0, rows, :] = acc.astype(BF16)


def _experts_call(starts, h2, idx, idxt, w1, w3, w2, tb):
    b, l, de = h2.shape
    d = de - R_PAD
    whole = lambda width: pl.BlockSpec((1, l, width), lambda bi, g, st: (bi, 0, 0))
    grid_spec = pltpu.PrefetchScalarGridSpec(
        num_scalar_prefetch=1, grid=(b, N_GROUPS),
        in_specs=[pl.BlockSpec((1, l, de), lambda bi, g, st: (bi, 0, 0), pipeline_mode=pl.Buffered(1)),
                  whole(8), pl.BlockSpec((1, 8, l), lambda bi, g, st: (bi, 0, 0)),
                  pl.BlockSpec((EXP_PER_GROUP, d, D_EXPERT), lambda bi, g, st: (g, 0, 0)),
                  pl.BlockSpec((EXP_PER_GROUP, d, D_EXPERT), lambda bi, g, st: (g, 0, 0)),
                  pl.BlockSpec((EXP_PER_GROUP, D_EXPERT, d), lambda bi, g, st: (g, 0, 0))],
        out_specs=whole(d),
        scratch_shapes=[pltpu.VMEM((MOE_CHUNK, de), F32),
                        pltpu.VMEM((l + (N_GROUPS + 1) * MOE_CHUNK, d), BF16)])
    return pl.pallas_call(
        functools.partial(_experts_kernel, tb=tb), grid_spec=grid_spec,
        out_shape=jax.ShapeDtypeStruct((b, l, d), BF16),
        compiler_params=_params("parallel", "arbitrary"), name="experts",
    )(starts, h2, idx, idxt, w1, w3, w2)


def _group_starts(cnt):
    c = cnt[:, :, 0, :N_GROUPS].astype(jnp.int32)
    s = jnp.cumsum(c, axis=1)
    s = jnp.concatenate([jnp.zeros_like(s[:, :1]), s], axis=1)
    return jnp.transpose(s, (0, 2, 1)).reshape(-1)


def _final_kernel(x_ref, mod_ref, f_ref, g_ref, o_ref):
    o_ref[0] = _rms(x_ref[0] + mod_ref[0][5:6] * f_ref[0].astype(F32), g_ref[...])


def _final_call(xs, mod, f, g, lc, tm):
    b, l, d = xs.shape
    off = lc // tm
    lat = pl.BlockSpec((1, tm, d), lambda bi, j: (bi, j + off, 0))
    return pl.pallas_call(
        _final_kernel, grid=(b, (l - lc) // tm),
        in_specs=[lat, pl.BlockSpec((1, 6, d), lambda bi, j: (bi, 0, 0)), lat, _const_spec(g.shape)],
        out_specs=pl.BlockSpec((1, tm, d), lambda bi, j: (bi, j, 0)),
        out_shape=jax.ShapeDtypeStruct((b, l - lc, d), F32),
        compiler_params=_params("parallel", "parallel"), name="final_norm",
    )(xs, mod, f, g)


def _rope_tables(t_len, lc):
    half = A_ROPE // 2
    rows = t_len // GRID_W
    r = jnp.repeat(jnp.arange(rows, dtype=F32), GRID_W)
    col = jnp.tile(jnp.arange(GRID_W, dtype=F32), rows)
    inv = ROPE_THETA ** (-jnp.arange(0, half, 2, dtype=F32) / half)
    ang = jnp.concatenate([r[:, None] * inv, col[:, None] * inv], axis=-1)
    cos = jnp.concatenate([jnp.ones((lc, half), F32), jnp.cos(ang)], axis=0)
    sin = jnp.concatenate([jnp.zeros((lc, half), F32), jnp.sin(ang)], axis=0)
    l = lc + t_len
    ones = jnp.ones((l, A_NOPE), F32)
    zeros = jnp.zeros((l, A_NOPE), F32)
    tail1 = jnp.ones((l, A_PAD - A_NOPE - A_ROPE), F32)
    tail0 = jnp.zeros((l, A_PAD - A_NOPE - A_ROPE), F32)
    zh = jnp.zeros((l, half), F32)
    cos_t = jnp.concatenate([ones, cos, cos, tail1], axis=-1)
    sina_t = jnp.concatenate([zeros, zh, sin, tail0], axis=-1)
    sinb_t = jnp.concatenate([zeros, -sin, zh, tail0], axis=-1)
    return cos_t, sina_t, sinb_t


def _layer_weights(l, w_in, m_gate_b, a_qnorm, a_wuq, a_kvnorm, a_wukv, g_ws, g_bs, g_vnorm,
                   w_pa, w_pb, w_pc, w_out):
    d = w_in.shape[1]
    wi = w_in[l]
    o = 0

    def take(n):
        nonlocal o
        s = wi[:, o:o + n]
        o += n
        return s

    mq, mk, mv, mo, mg = take(M_WIDTH), take(M_WIDTH), take(M_WIDTH), take(M_WIDTH), take(4 * M_HEADS)
    aq, akv, akr = take(A_QRANK), take(A_KVRANK), take(A_ROPE)
    gu, gv = take(G_WIDTH), take(G_WIDTH)
    br = take(3 * d)
    nh = M_HEADS
    gb = m_gate_b[l]
    mgo = jnp.concatenate([mg[:, :nh], mg[:, 2 * nh:3 * nh], mg[:, nh:2 * nh], mg[:, 3 * nh:]], axis=1)
    gbo = jnp.concatenate([gb[:nh], gb[2 * nh:3 * nh], gb[nh:2 * nh], gb[3 * nh:]])
    akr_pad = jnp.concatenate([jnp.zeros((d, A_NOPE), F32), akr,
                               jnp.zeros((d, A_PAD - A_NOPE - A_ROPE), F32)], axis=1)
    wuq = a_wuq[l].reshape(A_QRANK, A_HEADS, A_NOPE + A_ROPE)
    wuq = jnp.pad(wuq, ((0, 0), (0, 0), (0, A_PAD - A_NOPE - A_ROPE))).reshape(A_QRANK, A_HEADS * A_PAD)
    wukv = a_wukv[l].reshape(A_KVRANK, A_HEADS, A_NOPE + A_VDIM)
    wuk = jnp.pad(wukv[:, :, :A_NOPE], ((0, 0), (0, 0), (0, A_PAD - A_NOPE))).reshape(A_KVRANK, A_HEADS * A_PAD)
    wuv = jnp.pad(wukv[:, :, A_NOPE:], ((0, 0), (0, 0), (0, A_PAD - A_VDIM))).reshape(A_KVRANK, A_HEADS * A_PAD)
    vone = jnp.tile(jnp.concatenate([jnp.zeros((A_VDIM,), F32), jnp.ones((A_PAD - A_VDIM,), F32)]),
                    A_HEADS).reshape(1, A_HEADS * A_PAD)
    gbs = jnp.repeat(g_bs[l].T, G_DG, axis=1)
    return dict(
        wqk=jnp.concatenate([mq, mk], 1).astype(BF16), wvo=jnp.concatenate([mv, mo], 1).astype(BF16),
        wgt=jnp.pad(mgo, ((0, 0), (0, G_PAD - 4 * nh))).astype(BF16),
        gbt=jnp.pad(gbo, (0, G_PAD - 4 * nh)).reshape(1, G_PAD),
        wa=jnp.concatenate([aq, akv, akr_pad], 1).astype(BF16),
        wg=jnp.concatenate([gu, gv], 1).astype(BF16), wbr=br.astype(BF16),
        aqn=a_qnorm[l].reshape(1, -1), akvn=a_kvnorm[l].reshape(1, -1),
        wuq=wuq.astype(BF16), wuk=wuk.astype(BF16), wuv=wuv.astype(BF16), vone=vone,
        gvn=g_vnorm[l].reshape(1, -1), gws=g_ws[l].astype(BF16), gbs=gbs,
        wpa=w_pa[l].astype(BF16), wpb=w_pb[l].astype(BF16), wpc=w_pc[l].astype(BF16),
        wout=w_out[l].astype(BF16))


def _router_weights(r_group, r_group_b, r_expert, r_expert_b):
    d = r_group.shape[0]
    pad = R_SEG - N_EXPERTS - N_GROUPS
    r = jnp.concatenate([r_expert, r_group, jnp.zeros((d, pad), F32)], axis=1)
    r3 = jnp.concatenate(list(_split3(r)) + [jnp.zeros((d, R_PAD - 3 * R_SEG), BF16)], axis=1)
    rb = jnp.concatenate([r_expert_b, r_group_b, jnp.zeros((R_PAD - N_EXPERTS - N_GROUPS,), F32)])
    return r3, rb.reshape(1, R_PAD)


def _tile(n, lc, candidates):
    for t in candidates:
        if n % t == 0 and lc % t == 0:
            return t
    raise ValueError("sequence lengths must be multiples of 128")


def kernel(x, c, ctx, c_ctx, w_ada, b_ada, norm1, norm2, final_norm, w_in, m_conv, m_gate_b, m_norm, a_qnorm, a_wuq, a_kvnorm, a_wukv, g_ws, g_bs, g_vnorm, w_pa, w_pb, w_pc, w_out, r_group, r_group_b, r_expert, r_expert_b, e_w1, e_w3, e_w2):
    b, t_len, d = x.shape
    lc = ctx.shape[1]
    l = lc + t_len
    depth = w_in.shape[0]
    tm = _tile(l, lc, (256, 128))

    xs = jnp.concatenate([ctx, x], axis=1)
    cv = jnp.concatenate([c, c_ctx[None, :]], axis=0)
    mod_all = _ada_call(cv, w_ada, b_ada).reshape(depth, b + 1, 6, d)
    tabs = _rope_tables(t_len, lc)

    f = None
    mod_prev = None
    for li in range(depth):
        last = li == depth - 1
        mod = mod_all[li]
        w = _layer_weights(li, w_in, m_gate_b, a_qnorm, a_wuq, a_kvnorm, a_wukv, g_ws, g_bs, g_vnorm,
                           w_pa, w_pb, w_pc, w_out)
        outs = _inproj_call(xs, mod, norm1[li].reshape(1, d), w, tabs, lc, tm, f, mod_prev)
        qk, vo, gi, gf, gr, q, k, v, yc, br = outs[:10]
        if f is not None:
            xs = outs[10]
        ya = _mlstm_call(qk, vo, gi, gf, gr, m_conv[li], m_norm[li].reshape(1, -1), lc)
        yb = _attn_call(q, k, v, lc, tm, not last)
        r3, rb = _router_weights(r_group[li], r_group_b[li], r_expert[li], r_expert_b[li])
        xs, h2, idx, idxt, cnt = _merge_call(xs, mod, ya, yb, yc, br, w, norm2[li].reshape(1, d),
                                             r3, rb, lc, tm, last)
        f = _experts_call(_group_starts(cnt), h2, idx, idxt, e_w1[li].astype(BF16),
                          e_w3[li].astype(BF16), e_w2[li].astype(BF16), tm)
        mod_prev = mod
    return _final_call(xs, mod_prev, f, final_norm.reshape(1, d), lc, tm)
```

```python
import functools

import jax
import jax.numpy as jnp
from jax import lax
from jax.experimental import pallas as pl
from jax.experimental.pallas import tpu as pltpu

F32 = jnp.float32
BF16 = jnp.bfloat16

EPS = 1e-6
GRID_W = 64
ROPE_THETA = 10000.0

M_HEADS = 4
M_DH = 128
M_WIDTH = M_HEADS * M_DH
M_CHUNK = 128
G_PAD = 128

A_HEADS = 8
A_NOPE = 64
A_ROPE = 32
A_VDIM = 64
A_QRANK = 384
A_KVRANK = 256
A_WIDTH = A_HEADS * A_VDIM
A_PAD = 128
A_HPS = 4
ATT_SCALE = (A_NOPE + A_ROPE) ** -0.5
LOG2E = 1.4426950408889634

G_GROUPS = 4
G_CHUNK = 128
G_WIDTH = 512
G_DG = G_WIDTH // G_GROUPS

N_GROUPS = 4
EXP_PER_GROUP = 4
N_EXPERTS = N_GROUPS * EXP_PER_GROUP
D_EXPERT = 512
R_PAD = 128
R_SEG = 32
MOE_CHUNK = 256

VMEM_LIMIT = 56 * 1024 * 1024


def _dot(a, b):
    return jnp.dot(a, b, preferred_element_type=F32)


def _dot_nt(a, b):
    return lax.dot_general(a, b, (((1,), (1,)), ((), ())), preferred_element_type=F32)


def _dot_tn(a, b):
    return lax.dot_general(a, b, (((0,), (0,)), ((), ())), preferred_element_type=F32)


def _split3(x):
    hi = x.astype(BF16)
    r = x - hi.astype(F32)
    mid = r.astype(BF16)
    lo = (r - mid.astype(F32)).astype(BF16)
    return hi, mid, lo


def _sigmoid(x):
    return 1.0 / (1.0 + jnp.exp(-x))


def _silu(x):
    return x * _sigmoid(x)


def _log_sigmoid(x):
    return jnp.minimum(x, 0.0) - jnp.log1p(jnp.exp(-jnp.abs(x)))


def _gelu(x):
    return 0.5 * x * (1.0 + lax.erf(x * (2.0 ** -0.5)))


def _rms(x, g):
    return x * lax.rsqrt(jnp.mean(x * x, axis=-1, keepdims=True) + EPS) * g


def _params(*sem):
    return pltpu.CompilerParams(dimension_semantics=sem, vmem_limit_bytes=VMEM_LIMIT)


def _const_spec(shape):
    nd = len(shape)
    return pl.BlockSpec(shape, lambda *_: (0,) * nd, pipeline_mode=pl.Buffered(1))


def _ada_kernel(cv_ref, w_ref, b_ref, o_ref):
    s = _silu(cv_ref[...])
    o_ref[0] = _dot(s.astype(BF16), w_ref[0].astype(BF16)) + b_ref[0]


def _ada_call(cv, w_ada, b_ada):
    depth, d, n6 = w_ada.shape
    rows = cv.shape[0]
    tn = n6 // 4
    return pl.pallas_call(
        _ada_kernel,
        grid=(depth, n6 // tn),
        in_specs=[pl.BlockSpec((rows, d), lambda l, j: (0, 0)),
                  pl.BlockSpec((1, d, tn), lambda l, j: (l, 0, j)),
                  pl.BlockSpec((1, 1, tn), lambda l, j: (l, 0, j))],
        out_specs=pl.BlockSpec((1, rows, tn), lambda l, j: (l, 0, j)),
        out_shape=jax.ShapeDtypeStruct((depth, rows, n6), F32),
        compiler_params=_params("parallel", "parallel"),
        name="ada",
    )(cv, w_ada, b_ada.reshape(depth, 1, n6))


def _inproj_kernel(*refs, has_f):
    if has_f:
        f_ref, modp_ref, x_o = refs[0], refs[1], refs[-1]
        refs = refs[2:-1]
    (x_ref, mod_ref, n1_ref, wqk_ref, wvo_ref, wgt_ref, gbt_ref,
     wa_ref, wg_ref, wbr_ref, aqn_ref, akvn_ref, wuq_ref, wuk_ref, wuv_ref, vone_ref,
     cos_ref, sina_ref, sinb_ref, gvn_ref, gws_ref, gbs_ref,
     qk_o, vo_o, gi_o, gf_o, gr_o, q_o, k_o, v_o, yc_o, br_o) = refs
    tm = x_ref.shape[1]
    mod = mod_ref[0]
    x = x_ref[0]
    if has_f:
        x = x + modp_ref[0][5:6] * f_ref[0].astype(F32)
        x_o[0] = x
    h = _rms(x, n1_ref[...]) * (1.0 + mod[1:2]) + mod[0:1]
    hb = h.astype(BF16)

    qk_o[0] = _dot(hb, wqk_ref[...])
    vo_o[0] = _dot(hb, wvo_ref[...]).astype(BF16)
    ng = gi_o.shape[2]
    gates = _dot(hb, wgt_ref[...]) + gbt_ref[...]
    gi_o[0] = gates[:, :ng]
    gf_o[0] = pltpu.roll(gates, gates.shape[1] - ng, 1)[:, :ng]
    gr_o[0] = gates.T[:2 * ng, :]

    za = _dot(hb, wa_ref[...])
    aqn = _rms(za[:, :A_QRANK], aqn_ref[...]).astype(BF16)
    akvn = _rms(za[:, A_QRANK:A_QRANK + A_KVRANK], akvn_ref[...]).astype(BF16)
    cos = cos_ref[...]
    sina = sina_ref[...]
    sinb = sinb_ref[...]
    half = A_ROPE // 2

    def rope(t):
        return t * cos + pltpu.roll(t, half, 1) * sina + pltpu.roll(t, A_PAD - half, 1) * sinb

    kr = rope(za[:, A_QRANK + A_KVRANK:])
    qp = _dot(aqn, wuq_ref[...])
    kp = _dot(akvn, wuk_ref[...])
    for hh in range(A_HEADS):
        sl = slice(hh * A_PAD, (hh + 1) * A_PAD)
        q_o[0, :, sl] = (rope(qp[:, sl]) * (ATT_SCALE * LOG2E)).astype(BF16)
        k_o[0, :, sl] = (kp[:, sl] + kr).astype(BF16)
    v_o[0] = (_dot(akvn, wuv_ref[...]) + vone_ref[...]).astype(BF16)

    zg = _dot(hb, wg_ref[...])
    gu = _gelu(zg[:, :G_WIDTH])
    gv = _gelu(zg[:, G_WIDTH:])
    gvn = gvn_ref[...]
    bias = gbs_ref[...]
    for g in range(G_GROUPS):
        sl = slice(g * G_DG, (g + 1) * G_DG)
        xn = _rms(gv[:, sl], gvn[:, sl]).astype(BF16)
        ws = gws_ref[g]
        for ci in range(tm // G_CHUNK):
            r = slice(ci * G_CHUNK, (ci + 1) * G_CHUNK)
            sg = _dot(ws, xn[r]) + bias[:, sl]
            yc_o[0, r, sl] = (gu[r, sl] * sg).astype(BF16)

    br_o[0] = _sigmoid(_dot(hb, wbr_ref[...])).astype(BF16)


def _inproj_call(xs, mod, n1, w, tabs, lc, tm, f=None, mod_prev=None):
    b, l, d = xs.shape
    nct = lc // tm
    has_f = f is not None
    tok = lambda width: pl.BlockSpec((1, tm, width), lambda bi, j: (bi, j, 0))
    modspec = pl.BlockSpec((1, 6, d), lambda bi, j: (jnp.where(j < nct, b, bi), 0, 0))
    tab = pl.BlockSpec((tm, A_PAD), lambda bi, j: (j, 0))
    consts = [n1, w["wqk"], w["wvo"], w["wgt"], w["gbt"], w["wa"], w["wg"], w["wbr"],
              w["aqn"], w["akvn"], w["wuq"], w["wuk"], w["wuv"], w["vone"]]
    consts2 = [w["gvn"], w["gws"], w["gbs"]]
    in_specs = ([tok(d), modspec] + [_const_spec(a.shape) for a in consts] + [tab, tab, tab]
                + [_const_spec(a.shape) for a in consts2])
    args = [xs, mod, *consts, *tabs, *consts2]
    ng = 4 * M_HEADS
    out_shape = [jax.ShapeDtypeStruct((b, l, 2 * M_WIDTH), F32),
                 jax.ShapeDtypeStruct((b, l, 2 * M_WIDTH), BF16),
                 jax.ShapeDtypeStruct((b, l, ng // 2), F32),
                 jax.ShapeDtypeStruct((b, l, ng // 2), F32),
                 jax.ShapeDtypeStruct((b, ng, l), F32),
                 jax.ShapeDtypeStruct((b, l, A_HEADS * A_PAD), BF16),
                 jax.ShapeDtypeStruct((b, l, A_HEADS * A_PAD), BF16),
                 jax.ShapeDtypeStruct((b, l, A_HEADS * A_PAD), BF16),
                 jax.ShapeDtypeStruct((b, l, G_WIDTH), BF16),
                 jax.ShapeDtypeStruct((b, l, 3 * d), BF16)]
    out_specs = [tok(2 * M_WIDTH), tok(2 * M_WIDTH), tok(ng // 2), tok(ng // 2),
                 pl.BlockSpec((1, ng, tm), lambda bi, j: (bi, 0, j)),
                 tok(A_HEADS * A_PAD), tok(A_HEADS * A_PAD), tok(A_HEADS * A_PAD), tok(G_WIDTH), tok(3 * d)]
    if has_f:
        in_specs = [tok(d), modspec] + in_specs
        args = [f, mod_prev] + args
        out_shape.append(jax.ShapeDtypeStruct((b, l, d), F32))
        out_specs.append(tok(d))
    return pl.pallas_call(
        functools.partial(_inproj_kernel, has_f=has_f), grid=(b, l // tm), in_specs=in_specs,
        out_specs=out_specs, out_shape=out_shape,
        compiler_params=_params("parallel", "parallel"), name="inproj",
    )(*args)


def _scan(x, op, fill, axis, reverse):
    n = x.shape[axis]
    idx = lax.broadcasted_iota(jnp.int32, x.shape, axis)
    k = 1
    while k < n:
        if reverse:
            x = op(x, jnp.where(idx >= n - k, fill, pltpu.roll(x, n - k, axis)))
        else:
            x = op(x, jnp.where(idx < k, fill, pltpu.roll(x, k, axis)))
        k *= 2
    return x


def _mlstm_kernel(qk_ref, vo_ref, gi_ref, gf_ref, gr_ref, conv_ref, mnorm_ref, ya_ref,
                  q_s, kt_s, h_s, bc_s, ml_s, dl_s, br_s, cn_s, m_s, s_s, p_s, qcn_s, u_s, *, lc):
    l = qk_ref.shape[1]
    ch = M_CHUNK
    nc = l // ch
    ncc = lc // ch
    nh = M_HEADS
    ng = 2 * nh
    w = conv_ref[...]
    row = lax.broadcasted_iota(jnp.int32, (ch, 1), 0)

    def conv_chunk(j):
        r0 = pl.multiple_of(j * ch, ch)
        cur = qk_ref[0, pl.ds(r0, ch), :]
        prev8 = qk_ref[0, pl.ds(pl.multiple_of(jnp.maximum(r0 - 8, 0), 8), 8), :]
        next8 = qk_ref[0, pl.ds(pl.multiple_of(jnp.minimum(r0 + ch, l - 8), 8), 8), :]
        seg_start = jnp.logical_or(j == 0, j == ncc)
        seg_end = jnp.logical_or(j == ncc - 1, j == nc - 1)
        pe = jnp.where(seg_start, 0.0, prev8[7:8, :])
        ne = jnp.where(seg_end, 0.0, next8[0:1, :])
        xp = jnp.where(row == 0, pe, pltpu.roll(cur, 1, 0))
        xn = jnp.where(row == ch - 1, ne, pltpu.roll(cur, ch - 1, 0))
        y = _silu(xp * w[0:1] + cur * w[1:2] + xn * w[2:3])
        q_s[pl.ds(r0, ch), :] = (y[:, :M_WIDTH] * (M_DH ** -0.5)).astype(BF16)
        kt_s[:, pl.ds(r0, ch)] = y[:, M_WIDTH:].T.astype(BF16)

    ri = lax.broadcasted_iota(jnp.int32, (ch, ch), 0)
    ci = lax.broadcasted_iota(jnp.int32, (ch, ch), 1)
    lower = ri >= ci
    upper = ri <= ci
    ones_blk = jnp.ones((ch, M_DH), BF16)
    fwd_c = lax.broadcasted_iota(jnp.int32, (ch, ng), 1) < nh
    fwd_r = lax.broadcasted_iota(jnp.int32, (ng, ch), 0) < nh
    lane_c = lax.broadcasted_iota(jnp.int32, (ch, ng), 1)

    def local_chunk(j):
        r0 = pl.multiple_of(j * ch, ch)
        rows = pl.ds(r0, ch)
        lfc = _log_sigmoid(gf_ref[0, rows, :])
        gr = gr_ref[0, :, rows]
        lfr = _log_sigmoid(gr[ng:])
        pre_c = _scan(lfc, jnp.add, 0.0, 0, False)
        pre_r = _scan(lfr, jnp.add, 0.0, 1, False)
        b_c = jnp.where(fwd_c, pre_c, jnp.sum(lfc, axis=0, keepdims=True) + lfc - pre_c)
        b_r = jnp.where(fwd_r, pre_r, jnp.sum(lfr, axis=1, keepdims=True) + lfr - pre_r)
        g_c = gi_ref[0, rows, :] - b_c
        g_r = gr[:ng] - b_r
        cg_c = jnp.where(fwd_c, _scan(g_c, jnp.maximum, -jnp.inf, 0, False),
                         _scan(g_c, jnp.maximum, -jnp.inf, 0, True))
        bc_s[rows, :] = b_c
        br_s[:, rows] = b_r
        ml_s[rows, :] = b_c + cg_c
        dl = jnp.zeros((ch, ng), F32)
        for hh in range(nh):
            sl = slice(hh * M_DH, (hh + 1) * M_DH)
            s_s[hh] = _dot(q_s[rows, sl], kt_s[sl, rows])
        for hh in range(nh):
            s = s_s[hh]
            for d in range(2):
                jj = d * nh + hh
                wgt = jnp.exp(jnp.where(upper if d else lower, g_r[jj:jj + 1, :] - cg_c[:, jj:jj + 1], -jnp.inf))
                p_s[jj] = (s * wgt).astype(BF16)
        for hh in range(nh):
            sl = slice(hh * M_DH, (hh + 1) * M_DH)
            v1 = jnp.concatenate([vo_ref[0, rows, sl], ones_blk], axis=1)
            for d in range(2):
                jj = d * nh + hh
                nd = _dot(p_s[jj], v1)
                h_s[d, rows, sl] = nd[:, :M_DH]
                dl = jnp.where(lane_c == jj, nd[:, M_DH:M_DH + ng], dl)
        dl_s[rows, :] = dl

    def conv_local_body(j, carry):
        conv_chunk(j + 1)
        local_chunk(j)
        return carry

    conv_chunk(jnp.int32(0))
    lax.fori_loop(0, nc - 1, conv_local_body, 0)
    local_chunk(jnp.int32(nc - 1))

    cn_s[...] = jnp.zeros_like(cn_s)
    m_s[...] = jnp.zeros_like(m_s)
    lane_r = lax.broadcasted_iota(jnp.int32, (1, ng), 1)

    def scan_issue(r0, d):
        rows = pl.ds(r0, ch)
        gr = gr_ref[0, :, rows]
        br = br_s[:, rows]
        tot = jnp.sum(_log_sigmoid(gr[ng:]), axis=1, keepdims=True)
        scal = []
        for hh in range(nh):
            fi = d * nh + hh
            sl = slice(hh * M_DH, (hh + 1) * M_DH)
            qcn_s[fi] = _dot(q_s[rows, sl], cn_s[fi].astype(BF16))
            m_old = m_s[fi][:, 0:1]
            b_e = tot[fi:fi + 1, :]
            d_end = b_e - br[fi:fi + 1, :] + gr[fi:fi + 1, :]
            m_end = jnp.max(d_end, axis=-1, keepdims=True)
            m_new = jnp.maximum(b_e + m_old, m_end)
            ktw = (kt_s[sl, rows].astype(F32) * jnp.exp(d_end - m_end)).astype(BF16)
            v1 = jnp.concatenate([vo_ref[0, rows, sl], ones_blk], axis=1)
            u_s[fi] = _dot(ktw, v1)
            scal.append((m_old, m_new, jnp.exp(b_e + m_old - m_new), jnp.exp(m_end - m_new)))
        return scal

    def scan_finish(r0, d, scal):
        rows = pl.ds(r0, ch)
        m_row = jnp.zeros((1, ng), F32)
        for hh in range(nh):
            m_row = jnp.where(lane_r == d * nh + hh, scal[hh][0], m_row)

        inter = bc_s[rows, :] + m_row
        ml = ml_s[rows, :]
        mt = jnp.maximum(inter, ml)
        a = jnp.exp(ml - mt)
        wi = jnp.exp(inter - mt)
        qn = jnp.zeros((ch, ng), F32)
        for hh in range(nh):
            fi = d * nh + hh
            qn = jnp.where(lane_c == fi, qcn_s[fi, :, M_DH:M_DH + ng], qn)
        den = a * dl_s[rows, :] + wi * qn
        rinv = 1.0 / jnp.maximum(jnp.abs(den), jnp.exp(-mt))
        c_loc = a * rinv
        c_int = wi * rinv

        for hh in range(nh):
            fi = d * nh + hh
            sl = slice(hh * M_DH, (hh + 1) * M_DH)
            h_s[d, rows, sl] = (c_loc[:, fi:fi + 1] * h_s[d, rows, sl]
                                + c_int[:, fi:fi + 1] * qcn_s[fi, :, :M_DH])
            cn_s[fi] = scal[hh][2] * cn_s[fi] + scal[hh][3] * u_s[fi]
            m_s[fi] = jnp.broadcast_to(scal[hh][1], (1, M_DH))

    def scan_body(s, carry):
        rf = pl.multiple_of(s * ch, ch)
        rb = pl.multiple_of(jnp.where(s < ncc, ncc - 1 - s, nc - 1 - s + ncc) * ch, ch)
        sf = scan_issue(rf, 0)
        sb = scan_issue(rb, 1)
        scan_finish(rf, 0, sf)
        scan_finish(rb, 1, sb)
        return carry

    lax.fori_loop(0, nc, scan_body, 0)

    mnorm = mnorm_ref[...]

    def out_body(j, carry):
        r0 = pl.multiple_of(j * ch, ch)
        hsum = h_s[0, pl.ds(r0, ch), :] + h_s[1, pl.ds(r0, ch), :]
        og = _sigmoid(vo_ref[0, pl.ds(r0, ch), M_WIDTH:].astype(F32))
        for hh in range(M_HEADS):
            sl = slice(hh * M_DH, (hh + 1) * M_DH)
            ya_ref[0, pl.ds(r0, ch), sl] = (_rms(hsum[:, sl], mnorm[:, sl]) * og[:, sl]).astype(BF16)
        return carry

    lax.fori_loop(0, nc, out_body, 0)


def _mlstm_call(qk, vo, gi, gf, gr, conv, mnorm, lc):
    b, l, _ = qk.shape
    ng = 2 * M_HEADS
    return pl.pallas_call(
        functools.partial(_mlstm_kernel, lc=lc),
        grid=(b,),
        in_specs=[pl.BlockSpec((1, l, 2 * M_WIDTH), lambda bi: (bi, 0, 0), pipeline_mode=pl.Buffered(1)),
                  pl.BlockSpec((1, l, 2 * M_WIDTH), lambda bi: (bi, 0, 0)),
                  pl.BlockSpec((1, l, ng), lambda bi: (bi, 0, 0)),
                  pl.BlockSpec((1, l, ng), lambda bi: (bi, 0, 0)),
                  pl.BlockSpec((1, 2 * ng, l), lambda bi: (bi, 0, 0)),
                  _const_spec(conv.shape), _const_spec(mnorm.shape)],
        out_specs=pl.BlockSpec((1, l, M_WIDTH), lambda bi: (bi, 0, 0)),
        out_shape=jax.ShapeDtypeStruct((b, l, M_WIDTH), BF16),
        scratch_shapes=[pltpu.VMEM((l, M_WIDTH), BF16),
                        pltpu.VMEM((M_WIDTH, l), BF16),
                        pltpu.VMEM((2, l, M_WIDTH), F32),
                        pltpu.VMEM((l, ng), F32),
                        pltpu.VMEM((l, ng), F32),
                        pltpu.VMEM((l, ng), F32),
                        pltpu.VMEM((ng, l), F32),
                        pltpu.VMEM((2 * M_HEADS, M_DH, 2 * M_DH), F32),
                        pltpu.VMEM((2 * M_HEADS, 1, M_DH), F32),
                        pltpu.VMEM((M_HEADS, M_CHUNK, M_CHUNK), F32),
                        pltpu.VMEM((2 * M_HEADS, M_CHUNK, M_CHUNK), BF16),
                        pltpu.VMEM((2 * M_HEADS, M_CHUNK, 2 * M_DH), F32),
                        pltpu.VMEM((2 * M_HEADS, M_DH, 2 * M_DH), F32)],
        compiler_params=_params("parallel"), name="mlstm",
    )(qk, vo, gi, gf, gr, conv, mnorm)


def _attn_kernel(q_ref, k_ref, v_ref, o_ref, *, lc, ctx_out):
    tq = q_ref.shape[1]
    l = k_ref.shape[1]
    qi = pl.program_id(2)
    nct = lc // tq
    lane = lax.broadcasted_iota(jnp.int32, (tq, 2 * A_VDIM), 1)

    def run(klen):
        outs = []
        for hh in range(A_HPS):
            sl = slice(hh * A_PAD, (hh + 1) * A_PAD)
            s = _dot_nt(q_ref[0, :, sl], k_ref[0, :klen, sl])
            p = jnp.exp2((s - jnp.max(s, axis=-1, keepdims=True)).astype(BF16))
            nd = _dot(p, v_ref[0, :klen, sl])
            outs.append(nd / pltpu.roll(nd, A_VDIM, 1))
        for pp in range(A_HPS // 2):
            o_ref[0, :, pp * A_PAD:(pp + 1) * A_PAD] = jnp.where(
                lane < A_VDIM, outs[2 * pp], pltpu.roll(outs[2 * pp + 1], A_VDIM, 1)).astype(BF16)

    @pl.when(qi >= nct)
    def _():
        run(l)

    @pl.when(qi < nct)
    def _():
        if ctx_out:
            run(lc)
        else:
            o_ref[...] = jnp.zeros_like(o_ref)


def _attn_call(q, k, v, lc, tq, ctx_out):
    b, l, _ = q.shape
    return pl.pallas_call(
        functools.partial(_attn_kernel, lc=lc, ctx_out=ctx_out),
        grid=(b, A_HEADS // A_HPS, l // tq),
        in_specs=[pl.BlockSpec((1, tq, A_HPS * A_PAD), lambda bi, p, qi: (bi, qi, p)),
                  pl.BlockSpec((1, l, A_HPS * A_PAD), lambda bi, p, qi: (bi, 0, p)),
                  pl.BlockSpec((1, l, A_HPS * A_PAD), lambda bi, p, qi: (bi, 0, p))],
        out_specs=pl.BlockSpec((1, tq, A_HPS * A_VDIM), lambda bi, p, qi: (bi, qi, p)),
        out_shape=jax.ShapeDtypeStruct((b, l, A_WIDTH), BF16),
        compiler_params=_params("parallel", "parallel", "arbitrary"), name="attn",
    )(q, k, v)


def _merge_kernel(x_ref, mod_ref, modc_ref, ya_ref, yb_ref, yc_ref, br_ref, wpa_ref, wpb_ref, wpc_ref, wout_ref,
                  n2_ref, r_ref, rb_ref, o_ref, h2_o, idx_o, idxt_o, cnt_o, *, tm, nct, skip_ctx):
    d = x_ref.shape[2]
    nsub = x_ref.shape[1] // tm
    modb = mod_ref[0]
    modc = modc_ref[0]
    for s in range(nsub):
        rows = slice(s * tm, (s + 1) * tm)
        is_ctx = pl.program_id(1) * nsub + s < nct
        mod = jnp.where(is_ctx, modc, modb)
        br = br_ref[0, rows, :]
        y = (br[:, :d].astype(F32) * _dot(ya_ref[0, rows, :], wpa_ref[...])
             + br[:, d:2 * d].astype(F32) * _dot(yb_ref[0, rows, :], wpb_ref[...])
             + br[:, 2 * d:].astype(F32) * _dot(yc_ref[0, rows, :], wpc_ref[...]))
        out = _dot(y.astype(BF16), wout_ref[...])
        x = x_ref[0, rows, :] + mod[2:3] * out
        o_ref[0, rows, :] = x
        left_out = is_ctx if skip_ctx else None
        _route(x, mod, left_out, n2_ref, r_ref, rb_ref, h2_o, idx_o, idxt_o, cnt_o, rows, s)


def _merge_call(xs, mod, ya, yb, yc, br, w, n2, r3, rb, lc, tm, skip_ctx):
    b, l, d = xs.shape
    nct = lc // tm
    nsub = next(n for n in (3, 2, 1) if l % (n * tm) == 0)
    tg = nsub * tm
    tok = lambda width: pl.BlockSpec((1, tg, width), lambda bi, j: (bi, j, 0))
    consts = [w["wpa"], w["wpb"], w["wpc"], w["wout"], n2, r3, rb]
    return pl.pallas_call(
        functools.partial(_merge_kernel, tm=tm, nct=nct, skip_ctx=skip_ctx), grid=(b, l // tg),
        in_specs=[tok(d), pl.BlockSpec((1, 6, d), lambda bi, j: (bi, 0, 0)),
                  pl.BlockSpec((1, 6, d), lambda bi, j: (b, 0, 0)),
                  tok(M_WIDTH), tok(A_WIDTH), tok(G_WIDTH), tok(3 * d)] + [_const_spec(a.shape) for a in consts],
        out_specs=[tok(d), tok(d + R_PAD), tok(8), pl.BlockSpec((1, 8, tg), lambda bi, j: (bi, 0, j)),
                   pl.BlockSpec((1, nsub, 8, R_PAD), lambda bi, j: (bi, j, 0, 0))],
        out_shape=[jax.ShapeDtypeStruct((b, l, d), F32),
                   jax.ShapeDtypeStruct((b, l, d + R_PAD), BF16),
                   jax.ShapeDtypeStruct((b, l, 8), jnp.int32), jax.ShapeDtypeStruct((b, 8, l), jnp.int32),
                   jax.ShapeDtypeStruct((b, l // tm, 8, R_PAD), F32)],
        compiler_params=_params("parallel", "parallel"), name="merge",
    )(xs, mod, mod, ya, yb, yc, br, *consts)


def _route(x, mod, left_out, n2_ref, r_ref, rb_ref, h2_o, idx_o, idxt_o, cnt_o, rows, s):
    tm, d = x.shape
    h2 = _rms(x, n2_ref[...]) * (1.0 + mod[4:5]) + mod[3:4]
    h2_o[0, rows, :d] = h2.astype(BF16)

    r = r_ref[...]
    pp = sum(_dot(piece, r) for piece in _split3(h2))
    logits = pp + pltpu.roll(pp, R_PAD - R_SEG, 1) + pltpu.roll(pp, R_PAD - 2 * R_SEG, 1) + rb_ref[...]
    el = logits[:, :N_EXPERTS]
    gl = logits[:, N_EXPERTS:N_EXPERTS + N_GROUPS]
    big = 1e9

    lane_g = lax.broadcasted_iota(jnp.int32, (tm, N_GROUPS), 1).astype(F32)
    gmax = jnp.max(gl, axis=-1, keepdims=True)
    g_sel = jnp.min(jnp.where(gl == gmax, lane_g, big), axis=-1, keepdims=True)
    g_prob = 1.0 / jnp.sum(jnp.exp(gl - gmax), axis=-1, keepdims=True)
    if left_out is not None:
        g_sel = jnp.where(left_out, -1.0, g_sel)

    lane_i = lax.broadcasted_iota(jnp.int32, (tm, N_EXPERTS), 1)
    lane_e = lane_i.astype(F32)
    lane_grp = (lane_i // EXP_PER_GROUP).astype(F32)
    v1 = jnp.where(lane_grp == g_sel, el, -jnp.inf)
    t1 = jnp.max(v1, axis=-1, keepdims=True)
    i1 = jnp.min(jnp.where(v1 == t1, lane_e, big), axis=-1, keepdims=True)
    v2 = jnp.where(lane_e == i1, -jnp.inf, v1)
    t2 = jnp.max(v2, axis=-1, keepdims=True)
    i2 = jnp.min(jnp.where(v2 == t2, lane_e, big), axis=-1, keepdims=True)
    e21 = jnp.exp(t2 - t1)
    w1 = 1.0 / (1.0 + e21)
    w2 = e21 * w1
    comb = (jnp.where(lane_e == i1, w1, 0.0) + jnp.where(lane_e == i2, w2, 0.0)) * g_prob
    tail = jnp.zeros((tm, R_PAD - 3 * N_EXPERTS), BF16)
    h2_o[0, rows, d:] = jnp.concatenate(list(_split3(comb)) + [tail], axis=1)

    lane_p = lax.broadcasted_iota(jnp.int32, (tm, R_PAD), 1)
    onehot = jnp.where(lane_p.astype(F32) == g_sel, 1.0, 0.0)
    ri = lax.broadcasted_iota(jnp.int32, (tm, tm), 0)
    ci = lax.broadcasted_iota(jnp.int32, (tm, tm), 1)
    before = jnp.where(ri > ci, 1.0, 0.0).astype(BF16)
    rank = jnp.sum(_dot(before, onehot.astype(BF16)) * onehot, axis=-1, keepdims=True)
    cnt_o[0, s] = jnp.broadcast_to(jnp.sum(onehot, axis=0, keepdims=True), (8, R_PAD))
    fields = jnp.where(lane_p == 0, g_sel, jnp.where(lane_p == 1, rank, 0.0))
    idx_o[0, rows, :] = fields[:, :8].astype(jnp.int32)
    idxt_o[0, :, rows] = fields.T[:8, :].astype(jnp.int32)


def _experts_kernel(st_ref, h2_ref, idx_ref, idxt_ref, w1_ref, w3_ref, w2_ref, o_ref, hs_s, ys_s, *, tb):
    l = h2_ref.shape[1]
    d = o_ref.shape[2]
    nblk = l // tb
    ch = MOE_CHUNK
    bi = pl.program_id(0)
    g = pl.program_id(1)

    @pl.when(g == 0)
    def _():
        ys_s[...] = jnp.zeros_like(ys_s)

    def group_base(gg):
        return (bi * N_GROUPS + gg) * (nblk + 1)

    def group_offset(upto):
        off = 0
        for gg in range(N_GROUPS - 1):
            padded = ((st_ref[group_base(gg) + nblk] + ch - 1) // ch) * ch
            off = off + jnp.where(gg < upto, padded, 0)
        return off

    base = group_base(g)
    cnt = st_ref[base + nblk]
    goff = group_offset(g)

    def chunk(lo, ch):
        sub_iota = lax.broadcasted_iota(jnp.int32, (ch, tb), 0)
        lane_e = lax.broadcasted_iota(jnp.int32, (ch, N_EXPERTS), 1)
        hs_s[:ch] = jnp.zeros((ch, hs_s.shape[1]), F32)
        for k in range(nblk):
            s_k = st_ref[base + k]
            e_k = st_ref[base + k + 1]
            rows = slice(k * tb, (k + 1) * tb)

            @pl.when(jnp.logical_and(s_k < lo + ch, e_k > lo))
            def _(s_k=s_k, rows=rows):
                it = idxt_ref[0, :, rows]
                pos = jnp.where(it[0:1] == g, it[1:2] + (s_k - lo), -1)
                p = jnp.where(sub_iota == pos, 1.0, 0.0).astype(BF16)
                hs_s[:ch] += _dot(p, h2_ref[0, rows, :])

        hsb = hs_s[:ch, :d].astype(BF16)
        cs = (hs_s[:ch, d:d + N_EXPERTS] + hs_s[:ch, d + N_EXPERTS:d + 2 * N_EXPERTS]
              + hs_s[:ch, d + 2 * N_EXPERTS:d + 3 * N_EXPERTS])
        y = jnp.zeros((ch, d), F32)
        for e in range(EXP_PER_GROUP):
            ce = jnp.sum(jnp.where(lane_e == g * EXP_PER_GROUP + e, cs, 0.0), axis=-1, keepdims=True)
            hid = (_silu(_dot(hsb, w1_ref[e])) * _dot(hsb, w3_ref[e]) * ce).astype(BF16)
            y = y + _dot(hid, w2_ref[e])
        ys_s[pl.ds(pl.multiple_of(goff + lo, ch), ch), :] = y.astype(BF16)

    half = ch // 2
    nfull = cnt // ch
    rem = cnt - nfull * ch
    nloop = nfull + jnp.where(rem > half, 1, 0)

    def chunk_body(c, carry):
        chunk(c * ch, ch)
        return carry

    lax.fori_loop(0, nloop, chunk_body, 0)

    @pl.when(jnp.logical_and(rem > 0, rem <= half))
    def _():
        chunk(nfull * ch, half)

    @pl.when(g == N_GROUPS - 1)
    def _():
        lane_w = lax.broadcasted_iota(jnp.int32, (tb, 2 * ch), 1)
        for k in range(nblk):
            rows = slice(k * tb, (k + 1) * tb)
            ic = idx_ref[0, rows, :]
            acc = jnp.zeros((tb, d), F32)
            for gg in range(N_GROUPS):
                s_k = st_ref[group_base(gg) + k]
                win = (s_k // ch) * ch
                pos = jnp.where(ic[:, 0:1] == gg, ic[:, 1:2] + (s_k - win), -1)
                q = jnp.where(lane_w == pos, 1.0, 0.0).astype(BF16)
                start = pl.multiple_of(group_offset(gg) + win, ch)
                acc = acc + _dot(q, ys_s[pl.ds(start, 2 * ch), :])
            o_ref[0, rows, :] = acc.astype(BF16)


def _experts_call(starts, h2, idx, idxt, w1, w3, w2, tb):
    b, l, de = h2.shape
    d = de - R_PAD
    whole = lambda width: pl.BlockSpec((1, l, width), lambda bi, g, st: (bi, 0, 0))
    grid_spec = pltpu.PrefetchScalarGridSpec(
        num_scalar_prefetch=1, grid=(b, N_GROUPS),
        in_specs=[pl.BlockSpec((1, l, de), lambda bi, g, st: (bi, 0, 0), pipeline_mode=pl.Buffered(1)),
                  whole(8), pl.BlockSpec((1, 8, l), lambda bi, g, st: (bi, 0, 0)),
                  pl.BlockSpec((EXP_PER_GROUP, d, D_EXPERT), lambda bi, g, st: (g, 0, 0)),
                  pl.BlockSpec((EXP_PER_GROUP, d, D_EXPERT), lambda bi, g, st: (g, 0, 0)),
                  pl.BlockSpec((EXP_PER_GROUP, D_EXPERT, d), lambda bi, g, st: (g, 0, 0))],
        out_specs=whole(d),
        scratch_shapes=[pltpu.VMEM((MOE_CHUNK, de), F32),
                        pltpu.VMEM((l + (N_GROUPS + 1) * MOE_CHUNK, d), BF16)])
    return pl.pallas_call(
        functools.partial(_experts_kernel, tb=tb), grid_spec=grid_spec,
        out_shape=jax.ShapeDtypeStruct((b, l, d), BF16),
        compiler_params=_params("parallel", "arbitrary"), name="experts",
    )(starts, h2, idx, idxt, w1, w3, w2)


def _group_starts(cnt):
    c = cnt[:, :, 0, :N_GROUPS].astype(jnp.int32)
    s = jnp.cumsum(c, axis=1)
    s = jnp.concatenate([jnp.zeros_like(s[:, :1]), s], axis=1)
    return jnp.transpose(s, (0, 2, 1)).reshape(-1)


def _final_kernel(x_ref, mod_ref, f_ref, g_ref, o_ref):
    o_ref[0] = _rms(x_ref[0] + mod_ref[0][5:6] * f_ref[0].astype(F32), g_ref[...])


def _final_call(xs, mod, f, g, lc, tm):
    b, l, d = xs.shape
    off = lc // tm
    lat = pl.BlockSpec((1, tm, d), lambda bi, j: (bi, j + off, 0))
    return pl.pallas_call(
        _final_kernel, grid=(b, (l - lc) // tm),
        in_specs=[lat, pl.BlockSpec((1, 6, d), lambda bi, j: (bi, 0, 0)), lat, _const_spec(g.shape)],
        out_specs=pl.BlockSpec((1, tm, d), lambda bi, j: (bi, j, 0)),
        out_shape=jax.ShapeDtypeStruct((b, l - lc, d), F32),
        compiler_params=_params("parallel", "parallel"), name="final_norm",
    )(xs, mod, f, g)


def _rope_tables(t_len, lc):
    half = A_ROPE // 2
    rows = t_len // GRID_W
    r = jnp.repeat(jnp.arange(rows, dtype=F32), GRID_W)
    col = jnp.tile(jnp.arange(GRID_W, dtype=F32), rows)
    inv = ROPE_THETA ** (-jnp.arange(0, half, 2, dtype=F32) / half)
    ang = jnp.concatenate([r[:, None] * inv, col[:, None] * inv], axis=-1)
    cos = jnp.concatenate([jnp.ones((lc, half), F32), jnp.cos(ang)], axis=0)
    sin = jnp.concatenate([jnp.zeros((lc, half), F32), jnp.sin(ang)], axis=0)
    l = lc + t_len
    ones = jnp.ones((l, A_NOPE), F32)
    zeros = jnp.zeros((l, A_NOPE), F32)
    tail1 = jnp.ones((l, A_PAD - A_NOPE - A_ROPE), F32)
    tail0 = jnp.zeros((l, A_PAD - A_NOPE - A_ROPE), F32)
    zh = jnp.zeros((l, half), F32)
    cos_t = jnp.concatenate([ones, cos, cos, tail1], axis=-1)
    sina_t = jnp.concatenate([zeros, zh, sin, tail0], axis=-1)
    sinb_t = jnp.concatenate([zeros, -sin, zh, tail0], axis=-1)
    return cos_t, sina_t, sinb_t


def _layer_weights(l, w_in, m_gate_b, a_qnorm, a_wuq, a_kvnorm, a_wukv, g_ws, g_bs, g_vnorm,
                   w_pa, w_pb, w_pc, w_out):
    d = w_in.shape[1]
    wi = w_in[l]
    o = 0

    def take(n):
        nonlocal o
        s = wi[:, o:o + n]
        o += n
        return s

    mq, mk, mv, mo, mg = take(M_WIDTH), take(M_WIDTH), take(M_WIDTH), take(M_WIDTH), take(4 * M_HEADS)
    aq, akv, akr = take(A_QRANK), take(A_KVRANK), take(A_ROPE)
    gu, gv = take(G_WIDTH), take(G_WIDTH)
    br = take(3 * d)
    nh = M_HEADS
    gb = m_gate_b[l]
    mgo = jnp.concatenate([mg[:, :nh], mg[:, 2 * nh:3 * nh], mg[:, nh:2 * nh], mg[:, 3 * nh:]], axis=1)
    gbo = jnp.concatenate([gb[:nh], gb[2 * nh:3 * nh], gb[nh:2 * nh], gb[3 * nh:]])
    akr_pad = jnp.concatenate([jnp.zeros((d, A_NOPE), F32), akr,
                               jnp.zeros((d, A_PAD - A_NOPE - A_ROPE), F32)], axis=1)
    wuq = a_wuq[l].reshape(A_QRANK, A_HEADS, A_NOPE + A_ROPE)
    wuq = jnp.pad(wuq, ((0, 0), (0, 0), (0, A_PAD - A_NOPE - A_ROPE))).reshape(A_QRANK, A_HEADS * A_PAD)
    wukv = a_wukv[l].reshape(A_KVRANK, A_HEADS, A_NOPE + A_VDIM)
    wuk = jnp.pad(wukv[:, :, :A_NOPE], ((0, 0), (0, 0), (0, A_PAD - A_NOPE))).reshape(A_KVRANK, A_HEADS * A_PAD)
    wuv = jnp.pad(wukv[:, :, A_NOPE:], ((0, 0), (0, 0), (0, A_PAD - A_VDIM))).reshape(A_KVRANK, A_HEADS * A_PAD)
    vone = jnp.tile(jnp.concatenate([jnp.zeros((A_VDIM,), F32), jnp.ones((A_PAD - A_VDIM,), F32)]),
                    A_HEADS).reshape(1, A_HEADS * A_PAD)
    gbs = jnp.repeat(g_bs[l].T, G_DG, axis=1)
    return dict(
        wqk=jnp.concatenate([mq, mk], 1).astype(BF16), wvo=jnp.concatenate([mv, mo], 1).astype(BF16),
        wgt=jnp.pad(mgo, ((0, 0), (0, G_PAD - 4 * nh))).astype(BF16),
        gbt=jnp.pad(gbo, (0, G_PAD - 4 * nh)).reshape(1, G_PAD),
        wa=jnp.concatenate([aq, akv, akr_pad], 1).astype(BF16),
        wg=jnp.concatenate([gu, gv], 1).astype(BF16), wbr=br.astype(BF16),
        aqn=a_qnorm[l].reshape(1, -1), akvn=a_kvnorm[l].reshape(1, -1),
        wuq=wuq.astype(BF16), wuk=wuk.astype(BF16), wuv=wuv.astype(BF16), vone=vone,
        gvn=g_vnorm[l].reshape(1, -1), gws=g_ws[l].astype(BF16), gbs=gbs,
        wpa=w_pa[l].astype(BF16), wpb=w_pb[l].astype(BF16), wpc=w_pc[l].astype(BF16),
        wout=w_out[l].astype(BF16))


def _router_weights(r_group, r_group_b, r_expert, r_expert_b):
    d = r_group.shape[0]
    pad = R_SEG - N_EXPERTS - N_GROUPS
    r = jnp.concatenate([r_expert, r_group, jnp.zeros((d, pad), F32)], axis=1)
    r3 = jnp.concatenate(list(_split3(r)) + [jnp.zeros((d, R_PAD - 3 * R_SEG), BF16)], axis=1)
    rb = jnp.concatenate([r_expert_b, r_group_b, jnp.zeros((R_PAD - N_EXPERTS - N_GROUPS,), F32)])
    return r3, rb.reshape(1, R_PAD)


def _tile(n, lc, candidates):
    for t in candidates:
        if n % t == 0 and lc % t == 0:
            return t
    raise ValueError("sequence lengths must be multiples of 128")


def kernel(x, c, ctx, c_ctx, w_ada, b_ada, norm1, norm2, final_norm, w_in, m_conv, m_gate_b, m_norm, a_qnorm, a_wuq, a_kvnorm, a_wukv, g_ws, g_bs, g_vnorm, w_pa, w_pb, w_pc, w_out, r_group, r_group_b, r_expert, r_expert_b, e_w1, e_w3, e_w2):
    b, t_len, d = x.shape
    lc = ctx.shape[1]
    l = lc + t_len
    depth = w_in.shape[0]
    tm = _tile(l, lc, (256, 128))

    xs = jnp.concatenate([ctx, x], axis=1)
    cv = jnp.concatenate([c, c_ctx[None, :]], axis=0)
    mod_all = _ada_call(cv, w_ada, b_ada).reshape(depth, b + 1, 6, d)
    tabs = _rope_tables(t_len, lc)

    f = None
    mod_prev = None
    for li in range(depth):
        last = li == depth - 1
        mod = mod_all[li]
        w = _layer_weights(li, w_in, m_gate_b, a_qnorm, a_wuq, a_kvnorm, a_wukv, g_ws, g_bs, g_vnorm,
                           w_pa, w_pb, w_pc, w_out)
        outs = _inproj_call(xs, mod, norm1[li].reshape(1, d), w, tabs, lc, tm, f, mod_prev)
        qk, vo, gi, gf, gr, q, k, v, yc, br = outs[:10]
        if f is not None:
            xs = outs[10]
        ya = _mlstm_call(qk, vo, gi, gf, gr, m_conv[li], m_norm[li].reshape(1, -1), lc)
        yb = _attn_call(q, k, v, lc, tm, not last)
        r3, rb = _router_weights(r_group[li], r_group_b[li], r_expert[li], r_expert_b[li])
        xs, h2, idx, idxt, cnt = _merge_call(xs, mod, ya, yb, yc, br, w, norm2[li].reshape(1, d),
                                             r3, rb, lc, tm, last)
        f = _experts_call(_group_starts(cnt), h2, idx, idxt, e_w1[li].astype(BF16),
                          e_w3[li].astype(BF16), e_w2[li].astype(BF16), tm)
        mod_prev = mod
    return _final_call(xs, mod_prev, f, final_norm.reshape(1, d), lc, tm)
```

```python
import functools

import jax
import jax.numpy as jnp
from jax import lax
from jax.experimental import pallas as pl
from jax.experimental.pallas import tpu as pltpu

F32 = jnp.float32
BF16 = jnp.bfloat16

EPS = 1e-6
GRID_W = 64
ROPE_THETA = 10000.0

M_HEADS = 4
M_DH = 128
M_WIDTH = M_HEADS * M_DH
M_CHUNK = 128
G_PAD = 128

A_HEADS = 8
A_NOPE = 64
A_ROPE = 32
A_VDIM = 64
A_QRANK = 384
A_KVRANK = 256
A_WIDTH = A_HEADS * A_VDIM
A_PAD = 128
A_HPS = 4
ATT_SCALE = (A_NOPE + A_ROPE) ** -0.5
LOG2E = 1.4426950408889634

G_GROUPS = 4
G_CHUNK = 128
G_WIDTH = 512
G_DG = G_WIDTH // G_GROUPS

N_GROUPS = 4
EXP_PER_GROUP = 4
N_EXPERTS = N_GROUPS * EXP_PER_GROUP
D_EXPERT = 512
R_PAD = 128
R_SEG = 32
MOE_CHUNK = 256

VMEM_LIMIT = 56 * 1024 * 1024


def _dot(a, b):
    return jnp.dot(a, b, preferred_element_type=F32)


def _dot_nt(a, b):
    return lax.dot_general(a, b, (((1,), (1,)), ((), ())), preferred_element_type=F32)


def _dot_tn(a, b):
    return lax.dot_general(a, b, (((0,), (0,)), ((), ())), preferred_element_type=F32)


def _split3(x):
    hi = x.astype(BF16)
    r = x - hi.astype(F32)
    mid = r.astype(BF16)
    lo = (r - mid.astype(F32)).astype(BF16)
    return hi, mid, lo


def _sigmoid(x):
    return 1.0 / (1.0 + jnp.exp(-x))


def _silu(x):
    return x * _sigmoid(x)


def _log_sigmoid(x):
    return jnp.minimum(x, 0.0) - jnp.log1p(jnp.exp(-jnp.abs(x)))


def _gelu(x):
    return 0.5 * x * (1.0 + lax.erf(x * (2.0 ** -0.5)))


def _rms(x, g):
    return x * lax.rsqrt(jnp.mean(x * x, axis=-1, keepdims=True) + EPS) * g


def _params(*sem):
    return pltpu.CompilerParams(dimension_semantics=sem, vmem_limit_bytes=VMEM_LIMIT)


def _const_spec(shape):
    nd = len(shape)
    return pl.BlockSpec(shape, lambda *_: (0,) * nd, pipeline_mode=pl.Buffered(1))


def _ada_kernel(cv_ref, w_ref, b_ref, o_ref):
    s = _silu(cv_ref[...])
    o_ref[0] = _dot(s.astype(BF16), w_ref[0].astype(BF16)) + b_ref[0]


def _ada_call(cv, w_ada, b_ada):
    depth, d, n6 = w_ada.shape
    rows = cv.shape[0]
    tn = n6 // 4
    return pl.pallas_call(
        _ada_kernel,
        grid=(depth, n6 // tn),
        in_specs=[pl.BlockSpec((rows, d), lambda l, j: (0, 0)),
                  pl.BlockSpec((1, d, tn), lambda l, j: (l, 0, j)),
                  pl.BlockSpec((1, 1, tn), lambda l, j: (l, 0, j))],
        out_specs=pl.BlockSpec((1, rows, tn), lambda l, j: (l, 0, j)),
        out_shape=jax.ShapeDtypeStruct((depth, rows, n6), F32),
        compiler_params=_params("parallel", "parallel"),
        name="ada",
    )(cv, w_ada, b_ada.reshape(depth, 1, n6))


def _inproj_kernel(*refs, has_f):
    if has_f:
        f_ref, modp_ref, x_o = refs[0], refs[1], refs[-1]
        refs = refs[2:-1]
    (x_ref, mod_ref, n1_ref, wqk_ref, wvo_ref, wgt_ref, gbt_ref,
     wa_ref, wg_ref, wbr_ref, aqn_ref, akvn_ref, wuq_ref, wuk_ref, wuv_ref, vone_ref,
     cos_ref, sina_ref, sinb_ref, gvn_ref, gws_ref, gbs_ref,
     qk_o, vo_o, gi_o, gf_o, gr_o, q_o, k_o, v_o, yc_o, br_o) = refs
    tm = x_ref.shape[1]
    mod = mod_ref[0]
    x = x_ref[0]
    if has_f:
        x = x + modp_ref[0][5:6] * f_ref[0].astype(F32)
        x_o[0] = x
    h = _rms(x, n1_ref[...]) * (1.0 + mod[1:2]) + mod[0:1]
    hb = h.astype(BF16)

    qk_o[0] = _dot(hb, wqk_ref[...])
    vo_o[0] = _dot(hb, wvo_ref[...]).astype(BF16)
    ng = gi_o.shape[2]
    gates = _dot(hb, wgt_ref[...]) + gbt_ref[...]
    gi_o[0] = gates[:, :ng]
    gf_o[0] = pltpu.roll(gates, gates.shape[1] - ng, 1)[:, :ng]
    gr_o[0] = gates.T[:2 * ng, :]

    za = _dot(hb, wa_ref[...])
    aqn = _rms(za[:, :A_QRANK], aqn_ref[...]).astype(BF16)
    akvn = _rms(za[:, A_QRANK:A_QRANK + A_KVRANK], akvn_ref[...]).astype(BF16)
    cos = cos_ref[...]
    sina = sina_ref[...]
    sinb = sinb_ref[...]
    half = A_ROPE // 2

    def rope(t):
        return t * cos + pltpu.roll(t, half, 1) * sina + pltpu.roll(t, A_PAD - half, 1) * sinb

    kr = rope(za[:, A_QRANK + A_KVRANK:])
    qp = _dot(aqn, wuq_ref[...])
    kp = _dot(akvn, wuk_ref[...])
    for hh in range(A_HEADS):
        sl = slice(hh * A_PAD, (hh + 1) * A_PAD)
        q_o[0, :, sl] = (rope(qp[:, sl]) * (ATT_SCALE * LOG2E)).astype(BF16)
        k_o[0, :, sl] = (kp[:, sl] + kr).astype(BF16)
    v_o[0] = (_dot(akvn, wuv_ref[...]) + vone_ref[...]).astype(BF16)

    zg = _dot(hb, wg_ref[...])
    gu = _gelu(zg[:, :G_WIDTH])
    gv = _gelu(zg[:, G_WIDTH:])
    gvn = gvn_ref[...]
    bias = gbs_ref[...]
    for g in range(G_GROUPS):
        sl = slice(g * G_DG, (g + 1) * G_DG)
        xn = _rms(gv[:, sl], gvn[:, sl]).astype(BF16)
        ws = gws_ref[g]
        for ci in range(tm // G_CHUNK):
            r = slice(ci * G_CHUNK, (ci + 1) * G_CHUNK)
            sg = _dot(ws, xn[r]) + bias[:, sl]
            yc_o[0, r, sl] = (gu[r, sl] * sg).astype(BF16)

    br_o[0] = _sigmoid(_dot(hb, wbr_ref[...])).astype(BF16)


def _inproj_call(xs, mod, n1, w, tabs, lc, tm, f=None, mod_prev=None):
    b, l, d = xs.shape
    nct = lc // tm
    has_f = f is not None
    tok = lambda width: pl.BlockSpec((1, tm, width), lambda bi, j: (bi, j, 0))
    modspec = pl.BlockSpec((1, 6, d), lambda bi, j: (jnp.where(j < nct, b, bi), 0, 0))
    tab = pl.BlockSpec((tm, A_PAD), lambda bi, j: (j, 0))
    consts = [n1, w["wqk"], w["wvo"], w["wgt"], w["gbt"], w["wa"], w["wg"], w["wbr"],
              w["aqn"], w["akvn"], w["wuq"], w["wuk"], w["wuv"], w["vone"]]
    consts2 = [w["gvn"], w["gws"], w["gbs"]]
    in_specs = ([tok(d), modspec] + [_const_spec(a.shape) for a in consts] + [tab, tab, tab]
                + [_const_spec(a.shape) for a in consts2])
    args = [xs, mod, *consts, *tabs, *consts2]
    ng = 4 * M_HEADS
    out_shape = [jax.ShapeDtypeStruct((b, l, 2 * M_WIDTH), F32),
                 jax.ShapeDtypeStruct((b, l, 2 * M_WIDTH), BF16),
                 jax.ShapeDtypeStruct((b, l, ng // 2), F32),
                 jax.ShapeDtypeStruct((b, l, ng // 2), F32),
                 jax.ShapeDtypeStruct((b, ng, l), F32),
                 jax.ShapeDtypeStruct((b, l, A_HEADS * A_PAD), BF16),
                 jax.ShapeDtypeStruct((b, l, A_HEADS * A_PAD), BF16),
                 jax.ShapeDtypeStruct((b, l, A_HEADS * A_PAD), BF16),
                 jax.ShapeDtypeStruct((b, l, G_WIDTH), BF16),
                 jax.ShapeDtypeStruct((b, l, 3 * d), BF16)]
    out_specs = [tok(2 * M_WIDTH), tok(2 * M_WIDTH), tok(ng // 2), tok(ng // 2),
                 pl.BlockSpec((1, ng, tm), lambda bi, j: (bi, 0, j)),
                 tok(A_HEADS * A_PAD), tok(A_HEADS * A_PAD), tok(A_HEADS * A_PAD), tok(G_WIDTH), tok(3 * d)]
    if has_f:
        in_specs = [tok(d), modspec] + in_specs
        args = [f, mod_prev] + args
        out_shape.append(jax.ShapeDtypeStruct((b, l, d), F32))
        out_specs.append(tok(d))
    return pl.pallas_call(
        functools.partial(_inproj_kernel, has_f=has_f), grid=(b, l // tm), in_specs=in_specs,
        out_specs=out_specs, out_shape=out_shape,
        compiler_params=_params("parallel", "parallel"), name="inproj",
    )(*args)


def _scan(x, op, fill, axis, reverse):
    n = x.shape[axis]
    idx = lax.broadcasted_iota(jnp.int32, x.shape, axis)
    k = 1
    while k < n:
        if reverse:
            x = op(x, jnp.where(idx >= n - k, fill, pltpu.roll(x, n - k, axis)))
        else:
            x = op(x, jnp.where(idx < k, fill, pltpu.roll(x, k, axis)))
        k *= 2
    return x


def _mlstm_kernel(qk_ref, vo_ref, gi_ref, gf_ref, gr_ref, conv_ref, mnorm_ref, ya_ref,
                  q_s, kt_s, h_s, bc_s, ml_s, dl_s, br_s, cn_s, m_s, s_s, p_s, qcn_s, u_s, *, lc):
    l = qk_ref.shape[1]
    ch = M_CHUNK
    nc = l // ch
    ncc = lc // ch
    nh = M_HEADS
    ng = 2 * nh
    w = conv_ref[...]
    row = lax.broadcasted_iota(jnp.int32, (ch, 1), 0)

    def conv_chunk(j):
        r0 = pl.multiple_of(j * ch, ch)
        cur = qk_ref[0, pl.ds(r0, ch), :]
        prev8 = qk_ref[0, pl.ds(pl.multiple_of(jnp.maximum(r0 - 8, 0), 8), 8), :]
        next8 = qk_ref[0, pl.ds(pl.multiple_of(jnp.minimum(r0 + ch, l - 8), 8), 8), :]
        seg_start = jnp.logical_or(j == 0, j == ncc)
        seg_end = jnp.logical_or(j == ncc - 1, j == nc - 1)
        pe = jnp.where(seg_start, 0.0, prev8[7:8, :])
        ne = jnp.where(seg_end, 0.0, next8[0:1, :])
        xp = jnp.where(row == 0, pe, pltpu.roll(cur, 1, 0))
        xn = jnp.where(row == ch - 1, ne, pltpu.roll(cur, ch - 1, 0))
        y = _silu(xp * w[0:1] + cur * w[1:2] + xn * w[2:3])
        q_s[pl.ds(r0, ch), :] = (y[:, :M_WIDTH] * (M_DH ** -0.5)).astype(BF16)
        kt_s[:, pl.ds(r0, ch)] = y[:, M_WIDTH:].T.astype(BF16)

    ri = lax.broadcasted_iota(jnp.int32, (ch, ch), 0)
    ci = lax.broadcasted_iota(jnp.int32, (ch, ch), 1)
    lower = ri >= ci
    upper = ri <= ci
    ones_blk = jnp.ones((ch, M_DH), BF16)
    fwd_c = lax.broadcasted_iota(jnp.int32, (ch, ng), 1) < nh
    fwd_r = lax.broadcasted_iota(jnp.int32, (ng, ch), 0) < nh
    lane_c = lax.broadcasted_iota(jnp.int32, (ch, ng), 1)

    def local_chunk(j):
        r0 = pl.multiple_of(j * ch, ch)
        rows = pl.ds(r0, ch)
        lfc = _log_sigmoid(gf_ref[0, rows, :])
        gr = gr_ref[0, :, rows]
        lfr = _log_sigmoid(gr[ng:])
        pre_c = _scan(lfc, jnp.add, 0.0, 0, False)
        pre_r = _scan(lfr, jnp.add, 0.0, 1, False)
        b_c = jnp.where(fwd_c, pre_c, jnp.sum(lfc, axis=0, keepdims=True) + lfc - pre_c)
        b_r = jnp.where(fwd_r, pre_r, jnp.sum(lfr, axis=1, keepdims=True) + lfr - pre_r)
        g_c = gi_ref[0, rows, :] - b_c
        g_r = gr[:ng] - b_r
        cg_c = jnp.where(fwd_c, _scan(g_c, jnp.maximum, -jnp.inf, 0, False),
                         _scan(g_c, jnp.maximum, -jnp.inf, 0, True))
        bc_s[rows, :] = b_c
        br_s[:, rows] = b_r
        ml_s[rows, :] = b_c + cg_c
        dl = jnp.zeros((ch, ng), F32)
        for hh in range(nh):
            sl = slice(hh * M_DH, (hh + 1) * M_DH)
            s_s[hh] = _dot(q_s[rows, sl], kt_s[sl, rows])
        for hh in range(nh):
            s = s_s[hh]
            for d in range(2):
                jj = d * nh + hh
                wgt = jnp.exp(jnp.where(upper if d else lower, g_r[jj:jj + 1, :] - cg_c[:, jj:jj + 1], -jnp.inf))
                p_s[jj] = (s * wgt).astype(BF16)
        for hh in range(nh):
            sl = slice(hh * M_DH, (hh + 1) * M_DH)
            v1 = jnp.concatenate([vo_ref[0, rows, sl], ones_blk], axis=1)
            for d in range(2):
                jj = d * nh + hh
                nd = _dot(p_s[jj], v1)
                h_s[d, rows, sl] = nd[:, :M_DH]
                dl = jnp.where(lane_c == jj, nd[:, M_DH:M_DH + ng], dl)
        dl_s[rows, :] = dl

    def conv_local_body(j, carry):
        conv_chunk(j + 1)
        local_chunk(j)
        return carry

    conv_chunk(jnp.int32(0))
    lax.fori_loop(0, nc - 1, conv_local_body, 0)
    local_chunk(jnp.int32(nc - 1))

    cn_s[...] = jnp.zeros_like(cn_s)
    m_s[...] = jnp.zeros_like(m_s)
    lane_r = lax.broadcasted_iota(jnp.int32, (1, ng), 1)

    def scan_issue(r0, d):
        rows = pl.ds(r0, ch)
        gr = gr_ref[0, :, rows]
        br = br_s[:, rows]
        tot = jnp.sum(_log_sigmoid(gr[ng:]), axis=1, keepdims=True)
        scal = []
        for hh in range(nh):
            fi = d * nh + hh
            sl = slice(hh * M_DH, (hh + 1) * M_DH)
            qcn_s[fi] = _dot(q_s[rows, sl], cn_s[fi].astype(BF16))
            m_old = m_s[fi][:, 0:1]
            b_e = tot[fi:fi + 1, :]
            d_end = b_e - br[fi:fi + 1, :] + gr[fi:fi + 1, :]
            m_end = jnp.max(d_end, axis=-1, keepdims=True)
            m_new = jnp.maximum(b_e + m_old, m_end)
            ktw = (kt_s[sl, rows].astype(F32) * jnp.exp(d_end - m_end)).astype(BF16)
            v1 = jnp.concatenate([vo_ref[0, rows, sl], ones_blk], axis=1)
            u_s[fi] = _dot(ktw, v1)
            scal.append((m_old, m_new, jnp.exp(b_e + m_old - m_new), jnp.exp(m_end - m_new)))
        return scal

    def scan_finish(r0, d, scal):
        rows = pl.ds(r0, ch)
        m_row = jnp.zeros((1, ng), F32)
        for hh in range(nh):
            m_row = jnp.where(lane_r == d * nh + hh, scal[hh][0], m_row)

        inter = bc_s[rows, :] + m_row
        ml = ml_s[rows, :]
        mt = jnp.maximum(inter, ml)
        a = jnp.exp(ml - mt)
        wi = jnp.exp(inter - mt)
        qn = jnp.zeros((ch, ng), F32)
        for hh in range(nh):
            fi = d * nh + hh
            qn = jnp.where(lane_c == fi, qcn_s[fi, :, M_DH:M_DH + ng], qn)
        den = a * dl_s[rows, :] + wi * qn
        rinv = 1.0 / jnp.maximum(jnp.abs(den), jnp.exp(-mt))
        c_loc = a * rinv
        c_int = wi * rinv

        for hh in range(nh):
            fi = d * nh + hh
            sl = slice(hh * M_DH, (hh + 1) * M_DH)
            h_s[d, rows, sl] = (c_loc[:, fi:fi + 1] * h_s[d, rows, sl]
                                + c_int[:, fi:fi + 1] * qcn_s[fi, :, :M_DH])
            cn_s[fi] = scal[hh][2] * cn_s[fi] + scal[hh][3] * u_s[fi]
            m_s[fi] = jnp.broadcast_to(scal[hh][1], (1, M_DH))

    def scan_body(s, carry):
        rf = pl.multiple_of(s * ch, ch)
        rb = pl.multiple_of(jnp.where(s < ncc, ncc - 1 - s, nc - 1 - s + ncc) * ch, ch)
        sf = scan_issue(rf, 0)
        sb = scan_issue(rb, 1)
        scan_finish(rf, 0, sf)
        scan_finish(rb, 1, sb)
        return carry

    lax.fori_loop(0, nc, scan_body, 0)

    mnorm = mnorm_ref[...]

    def out_body(j, carry):
        r0 = pl.multiple_of(j * ch, ch)
        hsum = h_s[0, pl.ds(r0, ch), :] + h_s[1, pl.ds(r0, ch), :]
        og = _sigmoid(vo_ref[0, pl.ds(r0, ch), M_WIDTH:].astype(F32))
        for hh in range(M_HEADS):
            sl = slice(hh * M_DH, (hh + 1) * M_DH)
            ya_ref[0, pl.ds(r0, ch), sl] = (_rms(hsum[:, sl], mnorm[:, sl]) * og[:, sl]).astype(BF16)
        return carry

    lax.fori_loop(0, nc, out_body, 0)


def _mlstm_call(qk, vo, gi, gf, gr, conv, mnorm, lc):
    b, l, _ = qk.shape
    ng = 2 * M_HEADS
    return pl.pallas_call(
        functools.partial(_mlstm_kernel, lc=lc),
        grid=(b,),
        in_specs=[pl.BlockSpec((1, l, 2 * M_WIDTH), lambda bi: (bi, 0, 0), pipeline_mode=pl.Buffered(1)),
                  pl.BlockSpec((1, l, 2 * M_WIDTH), lambda bi: (bi, 0, 0)),
                  pl.BlockSpec((1, l, ng), lambda bi: (bi, 0, 0)),
                  pl.BlockSpec((1, l, ng), lambda bi: (bi, 0, 0)),
                  pl.BlockSpec((1, 2 * ng, l), lambda bi: (bi, 0, 0)),
                  _const_spec(conv.shape), _const_spec(mnorm.shape)],
        out_specs=pl.BlockSpec((1, l, M_WIDTH), lambda bi: (bi, 0, 0)),
        out_shape=jax.ShapeDtypeStruct((b, l, M_WIDTH), BF16),
        scratch_shapes=[pltpu.VMEM((l, M_WIDTH), BF16),
                        pltpu.VMEM((M_WIDTH, l), BF16),
                        pltpu.VMEM((2, l, M_WIDTH), F32),
                        pltpu.VMEM((l, ng), F32),
                        pltpu.VMEM((l, ng), F32),
                        pltpu.VMEM((l, ng), F32),
                        pltpu.VMEM((ng, l), F32),
                        pltpu.VMEM((2 * M_HEADS, M_DH, 2 * M_DH), F32),
                        pltpu.VMEM((2 * M_HEADS, 1, M_DH), F32),
                        pltpu.VMEM((M_HEADS, M_CHUNK, M_CHUNK), F32),
                        pltpu.VMEM((2 * M_HEADS, M_CHUNK, M_CHUNK), BF16),
                        pltpu.VMEM((2 * M_HEADS, M_CHUNK, 2 * M_DH), F32),
                        pltpu.VMEM((2 * M_HEADS, M_DH, 2 * M_DH), F32)],
        compiler_params=_params("parallel"), name="mlstm",
    )(qk, vo, gi, gf, gr, conv, mnorm)


def _attn_kernel(q_ref, k_ref, v_ref, o_ref, s_s, p_s, *, lc, ctx_out):
    tq = q_ref.shape[1]
    l = k_ref.shape[1]
    qi = pl.program_id(2)
    nct = lc // tq
    lane = lax.broadcasted_iota(jnp.int32, (tq, 2 * A_VDIM), 1)

    def run(klen):
        outs = []
        for hh in range(A_HPS):
            sl = slice(hh * A_PAD, (hh + 1) * A_PAD)
            s_s[hh, :, :klen] = _dot_nt(q_ref[0, :, sl], k_ref[0, :klen, sl])
        row_max = [jnp.max(s_s[hh, :, :klen], axis=-1, keepdims=True) for hh in range(A_HPS)]
        for hh in range(A_HPS):
            p_s[hh, :, :klen] = jnp.exp2((s_s[hh, :, :klen] - row_max[hh]).astype(BF16))
        for hh in range(A_HPS):
            sl = slice(hh * A_PAD, (hh + 1) * A_PAD)
            nd = _dot(p_s[hh, :, :klen], v_ref[0, :klen, sl])
            outs.append(nd / pltpu.roll(nd, A_VDIM, 1))
        for pp in range(A_HPS // 2):
            o_ref[0, :, pp * A_PAD:(pp + 1) * A_PAD] = jnp.where(
                lane < A_VDIM, outs[2 * pp], pltpu.roll(outs[2 * pp + 1], A_VDIM, 1)).astype(BF16)

    @pl.when(qi >= nct)
    def _():
        run(l)

    @pl.when(qi < nct)
    def _():
        if ctx_out:
            run(lc)
        else:
            o_ref[...] = jnp.zeros_like(o_ref)


def _attn_call(q, k, v, lc, tq, ctx_out):
    b, l, _ = q.shape
    return pl.pallas_call(
        functools.partial(_attn_kernel, lc=lc, ctx_out=ctx_out),
        grid=(b, A_HEADS // A_HPS, l // tq),
        in_specs=[pl.BlockSpec((1, tq, A_HPS * A_PAD), lambda bi, p, qi: (bi, qi, p)),
                  pl.BlockSpec((1, l, A_HPS * A_PAD), lambda bi, p, qi: (bi, 0, p)),
                  pl.BlockSpec((1, l, A_HPS * A_PAD), lambda bi, p, qi: (bi, 0, p))],
        out_specs=pl.BlockSpec((1, tq, A_HPS * A_VDIM), lambda bi, p, qi: (bi, qi, p)),
        out_shape=jax.ShapeDtypeStruct((b, l, A_WIDTH), BF16),
        scratch_shapes=[pltpu.VMEM((A_HPS, tq, l), F32), pltpu.VMEM((A_HPS, tq, l), BF16)],
        compiler_params=_params("parallel", "parallel", "arbitrary"), name="attn",
    )(q, k, v)


def _merge_kernel(x_ref, mod_ref, modc_ref, ya_ref, yb_ref, yc_ref, br_ref, wpa_ref, wpb_ref, wpc_ref, wout_ref,
                  n2_ref, r_ref, rb_ref, o_ref, h2_o, idx_o, idxt_o, cnt_o, y_s, *, tm, nct, skip_ctx):
    d = x_ref.shape[2]
    nsub = x_ref.shape[1] // tm
    modb = mod_ref[0]
    modc = modc_ref[0]
    tiles = [slice(s * tm, (s + 1) * tm) for s in range(nsub)]
    ctx = [pl.program_id(1) * nsub + s < nct for s in range(nsub)]
    mods = [jnp.where(c, modc, modb) for c in ctx]
    for s, rows in enumerate(tiles):
        br = br_ref[0, rows, :]
        y = (br[:, :d].astype(F32) * _dot(ya_ref[0, rows, :], wpa_ref[...])
             + br[:, d:2 * d].astype(F32) * _dot(yb_ref[0, rows, :], wpb_ref[...])
             + br[:, 2 * d:].astype(F32) * _dot(yc_ref[0, rows, :], wpc_ref[...]))
        y_s[s] = y.astype(BF16)
    for s, rows in enumerate(tiles):
        o_ref[0, rows, :] = x_ref[0, rows, :] + mods[s][2:3] * _dot(y_s[s], wout_ref[...])
    for s, rows in enumerate(tiles):
        logits = _route_logits(o_ref[0, rows, :], mods[s], n2_ref, r_ref, rb_ref, h2_o, rows)
        left_out = ctx[s] if skip_ctx else None
        _route_assign(logits, left_out, d, h2_o, idx_o, idxt_o, cnt_o, rows, s)


def _merge_call(xs, mod, ya, yb, yc, br, w, n2, r3, rb, lc, tm, skip_ctx):
    b, l, d = xs.shape
    nct = lc // tm
    nsub = next(n for n in (3, 2, 1) if l % (n * tm) == 0)
    tg = nsub * tm
    tok = lambda width: pl.BlockSpec((1, tg, width), lambda bi, j: (bi, j, 0))
    consts = [w["wpa"], w["wpb"], w["wpc"], w["wout"], n2, r3, rb]
    return pl.pallas_call(
        functools.partial(_merge_kernel, tm=tm, nct=nct, skip_ctx=skip_ctx), grid=(b, l // tg),
        in_specs=[tok(d), pl.BlockSpec((1, 6, d), lambda bi, j: (bi, 0, 0)),
                  pl.BlockSpec((1, 6, d), lambda bi, j: (b, 0, 0)),
                  tok(M_WIDTH), tok(A_WIDTH), tok(G_WIDTH), tok(3 * d)] + [_const_spec(a.shape) for a in consts],
        out_specs=[tok(d), tok(d + R_PAD), tok(8), pl.BlockSpec((1, 8, tg), lambda bi, j: (bi, 0, j)),
                   pl.BlockSpec((1, nsub, 8, R_PAD), lambda bi, j: (bi, j, 0, 0))],
        out_shape=[jax.ShapeDtypeStruct((b, l, d), F32),
                   jax.ShapeDtypeStruct((b, l, d + R_PAD), BF16),
                   jax.ShapeDtypeStruct((b, l, 8), jnp.int32), jax.ShapeDtypeStruct((b, 8, l), jnp.int32),
                   jax.ShapeDtypeStruct((b, l // tm, 8, R_PAD), F32)],
        scratch_shapes=[pltpu.VMEM((nsub, tm, d), BF16)],
        compiler_params=_params("parallel", "parallel"), name="merge",
    )(xs, mod, mod, ya, yb, yc, br, *consts)


def _route_logits(x, mod, n2_ref, r_ref, rb_ref, h2_o, rows):
    d = x.shape[1]
    h2 = _rms(x, n2_ref[...]) * (1.0 + mod[4:5]) + mod[3:4]
    h2_o[0, rows, :d] = h2.astype(BF16)
    r = r_ref[...]
    pp = sum(_dot(piece, r) for piece in _split3(h2))
    return pp + pltpu.roll(pp, R_PAD - R_SEG, 1) + pltpu.roll(pp, R_PAD - 2 * R_SEG, 1) + rb_ref[...]


def _route_assign(logits, left_out, d, h2_o, idx_o, idxt_o, cnt_o, rows, s):
    tm = logits.shape[0]
    el = logits[:, :N_EXPERTS]
    gl = logits[:, N_EXPERTS:N_EXPERTS + N_GROUPS]
    big = 1e9

    lane_g = lax.broadcasted_iota(jnp.int32, (tm, N_GROUPS), 1).astype(F32)
    gmax = jnp.max(gl, axis=-1, keepdims=True)
    g_sel = jnp.min(jnp.where(gl == gmax, lane_g, big), axis=-1, keepdims=True)
    g_prob = 1.0 / jnp.sum(jnp.exp(gl - gmax), axis=-1, keepdims=True)
    if left_out is not None:
        g_sel = jnp.where(left_out, -1.0, g_sel)

    lane_i = lax.broadcasted_iota(jnp.int32, (tm, N_EXPERTS), 1)
    lane_e = lane_i.astype(F32)
    lane_grp = (lane_i // EXP_PER_GROUP).astype(F32)
    v1 = jnp.where(lane_grp == g_sel, el, -jnp.inf)
    t1 = jnp.max(v1, axis=-1, keepdims=True)
    i1 = jnp.min(jnp.where(v1 == t1, lane_e, big), axis=-1, keepdims=True)
    v2 = jnp.where(lane_e == i1, -jnp.inf, v1)
    t2 = jnp.max(v2, axis=-1, keepdims=True)
    i2 = jnp.min(jnp.where(v2 == t2, lane_e, big), axis=-1, keepdims=True)
    e21 = jnp.exp(t2 - t1)
    w1 = 1.0 / (1.0 + e21)
    w2 = e21 * w1
    comb = (jnp.where(lane_e == i1, w1, 0.0) + jnp.where(lane_e == i2, w2, 0.0)) * g_prob
    tail = jnp.zeros((tm, R_PAD - 3 * N_EXPERTS), BF16)
    h2_o[0, rows, d:] = jnp.concatenate(list(_split3(comb)) + [tail], axis=1)

    lane_p = lax.broadcasted_iota(jnp.int32, (tm, R_PAD), 1)
    onehot = jnp.where(lane_p.astype(F32) == g_sel, 1.0, 0.0)
    ri = lax.broadcasted_iota(jnp.int32, (tm, tm), 0)
    ci = lax.broadcasted_iota(jnp.int32, (tm, tm), 1)
    before = jnp.where(ri > ci, 1.0, 0.0).astype(BF16)
    rank = jnp.sum(_dot(before, onehot.astype(BF16)) * onehot, axis=-1, keepdims=True)
    cnt_o[0, s] = jnp.broadcast_to(jnp.sum(onehot, axis=0, keepdims=True), (8, R_PAD))
    fields = jnp.where(lane_p == 0, g_sel, jnp.where(lane_p == 1, rank, 0.0))
    idx_o[0, rows, :] = fields[:, :8].astype(jnp.int32)
    idxt_o[0, :, rows] = fields.T[:8, :].astype(jnp.int32)


def _experts_kernel(st_ref, h2_ref, idx_ref, idxt_ref, w1_ref, w3_ref, w2_ref, o_ref, hs_s, ys_s, a_s, hid_s, *, tb):
    l = h2_ref.shape[1]
    d = o_ref.shape[2]
    nblk = l // tb
    ch = MOE_CHUNK
    bi = pl.program_id(0)
    g = pl.program_id(1)

    @pl.when(g == 0)
    def _():
        ys_s[...] = jnp.zeros_like(ys_s)

    def group_base(gg):
        return (bi * N_GROUPS + gg) * (nblk + 1)

    def group_offset(upto):
        off = 0
        for gg in range(N_GROUPS - 1):
            padded = ((st_ref[group_base(gg) + nblk] + ch - 1) // ch) * ch
            off = off + jnp.where(gg < upto, padded, 0)
        return off

    base = group_base(g)
    cnt = st_ref[base + nblk]
    goff = group_offset(g)

    def chunk(lo, ch):
        sub_iota = lax.broadcasted_iota(jnp.int32, (ch, tb), 0)
        lane_e = lax.broadcasted_iota(jnp.int32, (ch, N_EXPERTS), 1)
        hs_s[:ch] = jnp.zeros((ch, hs_s.shape[1]), F32)
        for k in range(nblk):
            s_k = st_ref[base + k]
            e_k = st_ref[base + k + 1]
            rows = slice(k * tb, (k + 1) * tb)

            @pl.when(jnp.logical_and(s_k < lo + ch, e_k > lo))
            def _(s_k=s_k, rows=rows):
                it = idxt_ref[0, :, rows]
                pos = jnp.where(it[0:1] == g, it[1:2] + (s_k - lo), -1)
                p = jnp.where(sub_iota == pos, 1.0, 0.0).astype(BF16)
                hs_s[:ch] += _dot(p, h2_ref[0, rows, :])

        hsb = hs_s[:ch, :d].astype(BF16)
        cs = (hs_s[:ch, d:d + N_EXPERTS] + hs_s[:ch, d + N_EXPERTS:d + 2 * N_EXPERTS]
              + hs_s[:ch, d + 2 * N_EXPERTS:d + 3 * N_EXPERTS])
        for e in range(EXP_PER_GROUP):
            a_s[2 * e, :ch] = _dot(hsb, w1_ref[e])
            a_s[2 * e + 1, :ch] = _dot(hsb, w3_ref[e])
        for e in range(EXP_PER_GROUP):
            ce = jnp.sum(jnp.where(lane_e == g * EXP_PER_GROUP + e, cs, 0.0), axis=-1, keepdims=True)
            hid_s[e, :ch] = (_silu(a_s[2 * e, :ch]) * a_s[2 * e + 1, :ch] * ce).astype(BF16)
        y = jnp.zeros((ch, d), F32)
        for e in range(EXP_PER_GROUP):
            y = y + _dot(hid_s[e, :ch], w2_ref[e])
        ys_s[pl.ds(pl.multiple_of(goff + lo, ch), ch), :] = y.astype(BF16)

    half = ch // 2
    nfull = cnt // ch
    rem = cnt - nfull * ch
    nloop = nfull + jnp.where(rem > half, 1, 0)

    def chunk_body(c, carry):
        chunk(c * ch, ch)
        return carry

    lax.fori_loop(0, nloop, chunk_body, 0)

    @pl.when(jnp.logical_and(rem > 0, rem <= half))
    def _():
        chunk(nfull * ch, half)

    @pl.when(g == N_GROUPS - 1)
    def _():
        lane_w = lax.broadcasted_iota(jnp.int32, (tb, 2 * ch), 1)
        for k in range(nblk):
            rows = slice(k * tb, (k + 1) * tb)
            ic = idx_ref[0, rows, :]
            acc = jnp.zeros((tb, d), F32)
            for gg in range(N_GROUPS):
                s_k = st_ref[group_base(gg) + k]
                win = (s_k // ch) * ch
                pos = jnp.where(ic[:, 0:1] == gg, ic[:, 1:2] + (s_k - win), -1)
                q = jnp.where(lane_w == pos, 1.0, 0.0).astype(BF16)
                start = pl.multiple_of(group_offset(gg) + win, ch)
                acc = acc + _dot(q, ys_s[pl.ds(start, 2 * ch), :])
            o_ref[0, rows, :] = acc.astype(BF16)


def _experts_call(starts, h2, idx, idxt, w1, w3, w2, tb):
    b, l, de = h2.shape
    d = de - R_PAD
    whole = lambda width: pl.BlockSpec((1, l, width), lambda bi, g, st: (bi, 0, 0))
    grid_spec = pltpu.PrefetchScalarGridSpec(
        num_scalar_prefetch=1, grid=(b, N_GROUPS),
        in_specs=[pl.BlockSpec((1, l, de), lambda bi, g, st: (bi, 0, 0), pipeline_mode=pl.Buffered(1)),
                  whole(8), pl.BlockSpec((1, 8, l), lambda bi, g, st: (bi, 0, 0)),
                  pl.BlockSpec((EXP_PER_GROUP, d, D_EXPERT), lambda bi, g, st: (g, 0, 0)),
                  pl.BlockSpec((EXP_PER_GROUP, d, D_EXPERT), lambda bi, g, st: (g, 0, 0)),
                  pl.BlockSpec((EXP_PER_GROUP, D_EXPERT, d), lambda bi, g, st: (g, 0, 0))],
        out_specs=whole(d),
        scratch_shapes=[pltpu.VMEM((MOE_CHUNK, de), F32),
                        pltpu.VMEM((l + (N_GROUPS + 1) * MOE_CHUNK, d), BF16),
                        pltpu.VMEM((2 * EXP_PER_GROUP, MOE_CHUNK, D_EXPERT), F32),
                        pltpu.VMEM((EXP_PER_GROUP, MOE_CHUNK, D_EXPERT), BF16)])
    return pl.pallas_call(
        functools.partial(_experts_kernel, tb=tb), grid_spec=grid_spec,
        out_shape=jax.ShapeDtypeStruct((b, l, d), BF16),
        compiler_params=_params("parallel", "arbitrary"), name="experts",
    )(starts, h2, idx, idxt, w1, w3, w2)


def _group_starts(cnt):
    c = cnt[:, :, 0, :N_GROUPS].astype(jnp.int32)
    s = jnp.cumsum(c, axis=1)
    s = jnp.concatenate([jnp.zeros_like(s[:, :1]), s], axis=1)
    return jnp.transpose(s, (0, 2, 1)).reshape(-1)


def _final_kernel(x_ref, mod_ref, f_ref, g_ref, o_ref):
    o_ref[0] = _rms(x_ref[0] + mod_ref[0][5:6] * f_ref[0].astype(F32), g_ref[...])


def _final_call(xs, mod, f, g, lc, tm):
    b, l, d = xs.shape
    off = lc // tm
    lat = pl.BlockSpec((1, tm, d), lambda bi, j: (bi, j + off, 0))
    return pl.pallas_call(
        _final_kernel, grid=(b, (l - lc) // tm),
        in_specs=[lat, pl.BlockSpec((1, 6, d), lambda bi, j: (bi, 0, 0)), lat, _const_spec(g.shape)],
        out_specs=pl.BlockSpec((1, tm, d), lambda bi, j: (bi, j, 0)),
        out_shape=jax.ShapeDtypeStruct((b, l - lc, d), F32),
        compiler_params=_params("parallel", "parallel"), name="final_norm",
    )(xs, mod, f, g)


def _rope_tables(t_len, lc):
    half = A_ROPE // 2
    rows = t_len // GRID_W
    r = jnp.repeat(jnp.arange(rows, dtype=F32), GRID_W)
    col = jnp.tile(jnp.arange(GRID_W, dtype=F32), rows)
    inv = ROPE_THETA ** (-jnp.arange(0, half, 2, dtype=F32) / half)
    ang = jnp.concatenate([r[:, None] * inv, col[:, None] * inv], axis=-1)
    cos = jnp.concatenate([jnp.ones((lc, half), F32), jnp.cos(ang)], axis=0)
    sin = jnp.concatenate([jnp.zeros((lc, half), F32), jnp.sin(ang)], axis=0)
    l = lc + t_len
    ones = jnp.ones((l, A_NOPE), F32)
    zeros = jnp.zeros((l, A_NOPE), F32)
    tail1 = jnp.ones((l, A_PAD - A_NOPE - A_ROPE), F32)
    tail0 = jnp.zeros((l, A_PAD - A_NOPE - A_ROPE), F32)
    zh = jnp.zeros((l, half), F32)
    cos_t = jnp.concatenate([ones, cos, cos, tail1], axis=-1)
    sina_t = jnp.concatenate([zeros, zh, sin, tail0], axis=-1)
    sinb_t = jnp.concatenate([zeros, -sin, zh, tail0], axis=-1)
    return cos_t, sina_t, sinb_t


def _layer_weights(l, w_in, m_gate_b, a_qnorm, a_wuq, a_kvnorm, a_wukv, g_ws, g_bs, g_vnorm,
                   w_pa, w_pb, w_pc, w_out):
    d = w_in.shape[1]
    wi = w_in[l]
    o = 0

    def take(n):
        nonlocal o
        s = wi[:, o:o + n]
        o += n
        return s

    mq, mk, mv, mo, mg = take(M_WIDTH), take(M_WIDTH), take(M_WIDTH), take(M_WIDTH), take(4 * M_HEADS)
    aq, akv, akr = take(A_QRANK), take(A_KVRANK), take(A_ROPE)
    gu, gv = take(G_WIDTH), take(G_WIDTH)
    br = take(3 * d)
    nh = M_HEADS
    gb = m_gate_b[l]
    mgo = jnp.concatenate([mg[:, :nh], mg[:, 2 * nh:3 * nh], mg[:, nh:2 * nh], mg[:, 3 * nh:]], axis=1)
    gbo = jnp.concatenate([gb[:nh], gb[2 * nh:3 * nh], gb[nh:2 * nh], gb[3 * nh:]])
    akr_pad = jnp.concatenate([jnp.zeros((d, A_NOPE), F32), akr,
                               jnp.zeros((d, A_PAD - A_NOPE - A_ROPE), F32)], axis=1)
    wuq = a_wuq[l].reshape(A_QRANK, A_HEADS, A_NOPE + A_ROPE)
    wuq = jnp.pad(wuq, ((0, 0), (0, 0), (0, A_PAD - A_NOPE - A_ROPE))).reshape(A_QRANK, A_HEADS * A_PAD)
    wukv = a_wukv[l].reshape(A_KVRANK, A_HEADS, A_NOPE + A_VDIM)
    wuk = jnp.pad(wukv[:, :, :A_NOPE], ((0, 0), (0, 0), (0, A_PAD - A_NOPE))).reshape(A_KVRANK, A_HEADS * A_PAD)
    wuv = jnp.pad(wukv[:, :, A_NOPE:], ((0, 0), (0, 0), (0, A_PAD - A_VDIM))).reshape(A_KVRANK, A_HEADS * A_PAD)
    vone = jnp.tile(jnp.concatenate([jnp.zeros((A_VDIM,), F32), jnp.ones((A_PAD - A_VDIM,), F32)]),
                    A_HEADS).reshape(1, A_HEADS * A_PAD)
    gbs = jnp.repeat(g_bs[l].T, G_DG, axis=1)
    return dict(
        wqk=jnp.concatenate([mq, mk], 1).astype(BF16), wvo=jnp.concatenate([mv, mo], 1).astype(BF16),
        wgt=jnp.pad(mgo, ((0, 0), (0, G_PAD - 4 * nh))).astype(BF16),
        gbt=jnp.pad(gbo, (0, G_PAD - 4 * nh)).reshape(1, G_PAD),
        wa=jnp.concatenate([aq, akv, akr_pad], 1).astype(BF16),
        wg=jnp.concatenate([gu, gv], 1).astype(BF16), wbr=br.astype(BF16),
        aqn=a_qnorm[l].reshape(1, -1), akvn=a_kvnorm[l].reshape(1, -1),
        wuq=wuq.astype(BF16), wuk=wuk.astype(BF16), wuv=wuv.astype(BF16), vone=vone,
        gvn=g_vnorm[l].reshape(1, -1), gws=g_ws[l].astype(BF16), gbs=gbs,
        wpa=w_pa[l].astype(BF16), wpb=w_pb[l].astype(BF16), wpc=w_pc[l].astype(BF16),
        wout=w_out[l].astype(BF16))


def _router_weights(r_group, r_group_b, r_expert, r_expert_b):
    d = r_group.shape[0]
    pad = R_SEG - N_EXPERTS - N_GROUPS
    r = jnp.concatenate([r_expert, r_group, jnp.zeros((d, pad), F32)], axis=1)
    r3 = jnp.concatenate(list(_split3(r)) + [jnp.zeros((d, R_PAD - 3 * R_SEG), BF16)], axis=1)
    rb = jnp.concatenate([r_expert_b, r_group_b, jnp.zeros((R_PAD - N_EXPERTS - N_GROUPS,), F32)])
    return r3, rb.reshape(1, R_PAD)


def _tile(n, lc, candidates):
    for t in candidates:
        if n % t == 0 and lc % t == 0:
            return t
    raise ValueError("sequence lengths must be multiples of 128")


def kernel(x, c, ctx, c_ctx, w_ada, b_ada, norm1, norm2, final_norm, w_in, m_conv, m_gate_b, m_norm, a_qnorm, a_wuq, a_kvnorm, a_wukv, g_ws, g_bs, g_vnorm, w_pa, w_pb, w_pc, w_out, r_group, r_group_b, r_expert, r_expert_b, e_w1, e_w3, e_w2):
    b, t_len, d = x.shape
    lc = ctx.shape[1]
    l = lc + t_len
    depth = w_in.shape[0]
    tm = _tile(l, lc, (256, 128))

    xs = jnp.concatenate([ctx, x], axis=1)
    cv = jnp.concatenate([c, c_ctx[None, :]], axis=0)
    mod_all = _ada_call(cv, w_ada, b_ada).reshape(depth, b + 1, 6, d)
    tabs = _rope_tables(t_len, lc)

    f = None
    mod_prev = None
    for li in range(depth):
        last = li == depth - 1
        mod = mod_all[li]
        w = _layer_weights(li, w_in, m_gate_b, a_qnorm, a_wuq, a_kvnorm, a_wukv, g_ws, g_bs, g_vnorm,
                           w_pa, w_pb, w_pc, w_out)
        outs = _inproj_call(xs, mod, norm1[li].reshape(1, d), w, tabs, lc, tm, f, mod_prev)
        qk, vo, gi, gf, gr, q, k, v, yc, br = outs[:10]
        if f is not None:
            xs = outs[10]
        ya = _mlstm_call(qk, vo, gi, gf, gr, m_conv[li], m_norm[li].reshape(1, -1), lc)
        yb = _attn_call(q, k, v, lc, tm, not last)
        r3, rb = _router_weights(r_group[li], r_group_b[li], r_expert[li], r_expert_b[li])
        xs, h2, idx, idxt, cnt = _merge_call(xs, mod, ya, yb, yc, br, w, norm2[li].reshape(1, d),
                                             r3, rb, lc, tm, last)
        f = _experts_call(_group_starts(cnt), h2, idx, idxt, e_w1[li].astype(BF16),
                          e_w3[li].astype(BF16), e_w2[li].astype(BF16), tm)
        mod_prev = mod
    return _final_call(xs, mod_prev, f, final_norm.reshape(1, d), lc, tm)
```

```python
import functools

import jax
import jax.numpy as jnp
from jax import lax
from jax.experimental import pallas as pl
from jax.experimental.pallas import tpu as pltpu

F32 = jnp.float32
BF16 = jnp.bfloat16

EPS = 1e-6
GRID_W = 64
ROPE_THETA = 10000.0

M_HEADS = 4
M_DH = 128
M_WIDTH = M_HEADS * M_DH
M_CHUNK = 128
G_PAD = 128

A_HEADS = 8
A_NOPE = 64
A_ROPE = 32
A_VDIM = 64
A_QRANK = 384
A_KVRANK = 256
A_WIDTH = A_HEADS * A_VDIM
A_PAD = 128
A_HPS = 4
ATT_SCALE = (A_NOPE + A_ROPE) ** -0.5
LOG2E = 1.4426950408889634

G_GROUPS = 4
G_CHUNK = 128
G_WIDTH = 512
G_DG = G_WIDTH // G_GROUPS

N_GROUPS = 4
EXP_PER_GROUP = 4
N_EXPERTS = N_GROUPS * EXP_PER_GROUP
D_EXPERT = 512
R_PAD = 128
R_SEG = 32
MOE_CHUNK = 256

VMEM_LIMIT = 56 * 1024 * 1024


def _dot(a, b):
    return jnp.dot(a, b, preferred_element_type=F32)


def _dot_nt(a, b):
    return lax.dot_general(a, b, (((1,), (1,)), ((), ())), preferred_element_type=F32)


def _dot_tn(a, b):
    return lax.dot_general(a, b, (((0,), (0,)), ((), ())), preferred_element_type=F32)


def _split3(x):
    hi = x.astype(BF16)
    r = x - hi.astype(F32)
    mid = r.astype(BF16)
    lo = (r - mid.astype(F32)).astype(BF16)
    return hi, mid, lo


def _sigmoid(x):
    return 1.0 / (1.0 + jnp.exp(-x))


def _silu(x):
    return x * _sigmoid(x)


def _log_sigmoid(x):
    return jnp.minimum(x, 0.0) - jnp.log1p(jnp.exp(-jnp.abs(x)))


def _gelu(x):
    return 0.5 * x * (1.0 + lax.erf(x * (2.0 ** -0.5)))


def _rms(x, g):
    return x * lax.rsqrt(jnp.mean(x * x, axis=-1, keepdims=True) + EPS) * g


def _params(*sem):
    return pltpu.CompilerParams(dimension_semantics=sem, vmem_limit_bytes=VMEM_LIMIT)


def _const_spec(shape):
    nd = len(shape)
    return pl.BlockSpec(shape, lambda *_: (0,) * nd, pipeline_mode=pl.Buffered(1))


def _ada_kernel(cv_ref, w_ref, b_ref, o_ref):
    s = _silu(cv_ref[...])
    o_ref[0] = _dot(s.astype(BF16), w_ref[0].astype(BF16)) + b_ref[0]


def _ada_call(cv, w_ada, b_ada):
    depth, d, n6 = w_ada.shape
    rows = cv.shape[0]
    tn = n6 // 4
    return pl.pallas_call(
        _ada_kernel,
        grid=(depth, n6 // tn),
        in_specs=[pl.BlockSpec((rows, d), lambda l, j: (0, 0)),
                  pl.BlockSpec((1, d, tn), lambda l, j: (l, 0, j)),
                  pl.BlockSpec((1, 1, tn), lambda l, j: (l, 0, j))],
        out_specs=pl.BlockSpec((1, rows, tn), lambda l, j: (l, 0, j)),
        out_shape=jax.ShapeDtypeStruct((depth, rows, n6), F32),
        compiler_params=_params("parallel", "parallel"),
        name="ada",
    )(cv, w_ada, b_ada.reshape(depth, 1, n6))


def _inproj_kernel(*refs, has_f):
    za_s, zg_s, zbr_s = refs[-3:]
    refs = refs[:-3]
    if has_f:
        f_ref, modp_ref, x_o = refs[0], refs[1], refs[-1]
        refs = refs[2:-1]
    (x_ref, mod_ref, n1_ref, wqk_ref, wvo_ref, wgt_ref, gbt_ref,
     wa_ref, wg_ref, wbr_ref, aqn_ref, akvn_ref, wuq_ref, wuk_ref, wuv_ref, vone_ref,
     cos_ref, sina_ref, sinb_ref, gvn_ref, gws_ref, gbs_ref,
     qk_o, vo_o, gi_o, gf_o, gr_o, q_o, k_o, v_o, yc_o, br_o) = refs
    tm = x_ref.shape[1]
    mod = mod_ref[0]
    x = x_ref[0]
    if has_f:
        x = x + modp_ref[0][5:6] * f_ref[0].astype(F32)
        x_o[0] = x
    h = _rms(x, n1_ref[...]) * (1.0 + mod[1:2]) + mod[0:1]
    hb = h.astype(BF16)

    qk_o[0] = _dot(hb, wqk_ref[...])
    vo_o[0] = _dot(hb, wvo_ref[...]).astype(BF16)
    ng = gi_o.shape[2]
    gates = _dot(hb, wgt_ref[...]) + gbt_ref[...]
    gi_o[0] = gates[:, :ng]
    gf_o[0] = pltpu.roll(gates, gates.shape[1] - ng, 1)[:, :ng]
    gr_o[0] = gates.T[:2 * ng, :]

    za_s[...] = _dot(hb, wa_ref[...])
    zg_s[...] = _dot(hb, wg_ref[...])
    zbr_s[...] = _dot(hb, wbr_ref[...])

    aqn = _rms(za_s[:, :A_QRANK], aqn_ref[...]).astype(BF16)
    akvn = _rms(za_s[:, A_QRANK:A_QRANK + A_KVRANK], akvn_ref[...]).astype(BF16)
    cos = cos_ref[...]
    sina = sina_ref[...]
    sinb = sinb_ref[...]
    half = A_ROPE // 2

    def rope(t):
        return t * cos + pltpu.roll(t, half, 1) * sina + pltpu.roll(t, A_PAD - half, 1) * sinb

    kr = rope(za_s[:, A_QRANK + A_KVRANK:])
    qp = _dot(aqn, wuq_ref[...])
    kp = _dot(akvn, wuk_ref[...])
    for hh in range(A_HEADS):
        sl = slice(hh * A_PAD, (hh + 1) * A_PAD)
        q_o[0, :, sl] = (rope(qp[:, sl]) * (ATT_SCALE * LOG2E)).astype(BF16)
        k_o[0, :, sl] = (kp[:, sl] + kr).astype(BF16)
    v_o[0] = (_dot(akvn, wuv_ref[...]) + vone_ref[...]).astype(BF16)

    gu = _gelu(zg_s[:, :G_WIDTH])
    gv = _gelu(zg_s[:, G_WIDTH:])
    gvn = gvn_ref[...]
    bias = gbs_ref[...]
    for g in range(G_GROUPS):
        sl = slice(g * G_DG, (g + 1) * G_DG)
        xn = _rms(gv[:, sl], gvn[:, sl]).astype(BF16)
        ws = gws_ref[g]
        for ci in range(tm // G_CHUNK):
            r = slice(ci * G_CHUNK, (ci + 1) * G_CHUNK)
            sg = _dot(ws, xn[r]) + bias[:, sl]
            yc_o[0, r, sl] = (gu[r, sl] * sg).astype(BF16)

    br_o[0] = _sigmoid(zbr_s[...]).astype(BF16)


def _inproj_call(xs, mod, n1, w, tabs, lc, tm, f=None, mod_prev=None):
    b, l, d = xs.shape
    nct = lc // tm
    has_f = f is not None
    tok = lambda width: pl.BlockSpec((1, tm, width), lambda bi, j: (bi, j, 0))
    modspec = pl.BlockSpec((1, 6, d), lambda bi, j: (jnp.where(j < nct, b, bi), 0, 0))
    tab = pl.BlockSpec((tm, A_PAD), lambda bi, j: (j, 0))
    consts = [n1, w["wqk"], w["wvo"], w["wgt"], w["gbt"], w["wa"], w["wg"], w["wbr"],
              w["aqn"], w["akvn"], w["wuq"], w["wuk"], w["wuv"], w["vone"]]
    consts2 = [w["gvn"], w["gws"], w["gbs"]]
    in_specs = ([tok(d), modspec] + [_const_spec(a.shape) for a in consts] + [tab, tab, tab]
                + [_const_spec(a.shape) for a in consts2])
    args = [xs, mod, *consts, *tabs, *consts2]
    ng = 4 * M_HEADS
    out_shape = [jax.ShapeDtypeStruct((b, l, 2 * M_WIDTH), F32),
                 jax.ShapeDtypeStruct((b, l, 2 * M_WIDTH), BF16),
                 jax.ShapeDtypeStruct((b, l, ng // 2), F32),
                 jax.ShapeDtypeStruct((b, l, ng // 2), F32),
                 jax.ShapeDtypeStruct((b, ng, l), F32),
                 jax.ShapeDtypeStruct((b, l, A_HEADS * A_PAD), BF16),
                 jax.ShapeDtypeStruct((b, l, A_HEADS * A_PAD), BF16),
                 jax.ShapeDtypeStruct((b, l, A_HEADS * A_PAD), BF16),
                 jax.ShapeDtypeStruct((b, l, G_WIDTH), BF16),
                 jax.ShapeDtypeStruct((b, l, 3 * d), BF16)]
    out_specs = [tok(2 * M_WIDTH), tok(2 * M_WIDTH), tok(ng // 2), tok(ng // 2),
                 pl.BlockSpec((1, ng, tm), lambda bi, j: (bi, 0, j)),
                 tok(A_HEADS * A_PAD), tok(A_HEADS * A_PAD), tok(A_HEADS * A_PAD), tok(G_WIDTH), tok(3 * d)]
    if has_f:
        in_specs = [tok(d), modspec] + in_specs
        args = [f, mod_prev] + args
        out_shape.append(jax.ShapeDtypeStruct((b, l, d), F32))
        out_specs.append(tok(d))
    return pl.pallas_call(
        functools.partial(_inproj_kernel, has_f=has_f), grid=(b, l // tm), in_specs=in_specs,
        out_specs=out_specs, out_shape=out_shape,
        scratch_shapes=[pltpu.VMEM((tm, w["wa"].shape[1]), F32), pltpu.VMEM((tm, 2 * G_WIDTH), F32),
                        pltpu.VMEM((tm, 3 * d), F32)],
        compiler_params=_params("parallel", "parallel"), name="inproj",
    )(*args)


def _scan(x, op, fill, axis, reverse):
    n = x.shape[axis]
    idx = lax.broadcasted_iota(jnp.int32, x.shape, axis)
    k = 1
    while k < n:
        if reverse:
            x = op(x, jnp.where(idx >= n - k, fill, pltpu.roll(x, n - k, axis)))
        else:
            x = op(x, jnp.where(idx < k, fill, pltpu.roll(x, k, axis)))
        k *= 2
    return x


def _mlstm_kernel(qk_ref, vo_ref, gi_ref, gf_ref, gr_ref, conv_ref, mnorm_ref, ya_ref,
                  q_s, kt_s, h_s, bc_s, ml_s, dl_s, br_s, cn_s, m_s, s_s, p_s, qcn_s, u_s, *, lc):
    l = qk_ref.shape[1]
    ch = M_CHUNK
    nc = l // ch
    ncc = lc // ch
    nh = M_HEADS
    ng = 2 * nh
    w = conv_ref[...]
    row = lax.broadcasted_iota(jnp.int32, (ch, 1), 0)

    def conv_chunk(j):
        r0 = pl.multiple_of(j * ch, ch)
        cur = qk_ref[0, pl.ds(r0, ch), :]
        prev8 = qk_ref[0, pl.ds(pl.multiple_of(jnp.maximum(r0 - 8, 0), 8), 8), :]
        next8 = qk_ref[0, pl.ds(pl.multiple_of(jnp.minimum(r0 + ch, l - 8), 8), 8), :]
        seg_start = jnp.logical_or(j == 0, j == ncc)
        seg_end = jnp.logical_or(j == ncc - 1, j == nc - 1)
        pe = jnp.where(seg_start, 0.0, prev8[7:8, :])
        ne = jnp.where(seg_end, 0.0, next8[0:1, :])
        xp = jnp.where(row == 0, pe, pltpu.roll(cur, 1, 0))
        xn = jnp.where(row == ch - 1, ne, pltpu.roll(cur, ch - 1, 0))
        y = _silu(xp * w[0:1] + cur * w[1:2] + xn * w[2:3])
        q_s[pl.ds(r0, ch), :] = (y[:, :M_WIDTH] * (M_DH ** -0.5)).astype(BF16)
        kt_s[:, pl.ds(r0, ch)] = y[:, M_WIDTH:].T.astype(BF16)

    ri = lax.broadcasted_iota(jnp.int32, (ch, ch), 0)
    ci = lax.broadcasted_iota(jnp.int32, (ch, ch), 1)
    lower = ri >= ci
    upper = ri <= ci
    ones_blk = jnp.ones((ch, M_DH), BF16)
    fwd_c = lax.broadcasted_iota(jnp.int32, (ch, ng), 1) < nh
    fwd_r = lax.broadcasted_iota(jnp.int32, (ng, ch), 0) < nh
    lane_c = lax.broadcasted_iota(jnp.int32, (ch, ng), 1)

    def local_chunk(j):
        r0 = pl.multiple_of(j * ch, ch)
        rows = pl.ds(r0, ch)
        lfc = _log_sigmoid(gf_ref[0, rows, :])
        gr = gr_ref[0, :, rows]
        lfr = _log_sigmoid(gr[ng:])
        pre_c = _scan(lfc, jnp.add, 0.0, 0, False)
        pre_r = _scan(lfr, jnp.add, 0.0, 1, False)
        b_c = jnp.where(fwd_c, pre_c, jnp.sum(lfc, axis=0, keepdims=True) + lfc - pre_c)
        b_r = jnp.where(fwd_r, pre_r, jnp.sum(lfr, axis=1, keepdims=True) + lfr - pre_r)
        g_c = gi_ref[0, rows, :] - b_c
        g_r = gr[:ng] - b_r
        cg_c = jnp.where(fwd_c, _scan(g_c, jnp.maximum, -jnp.inf, 0, False),
                         _scan(g_c, jnp.maximum, -jnp.inf, 0, True))
        bc_s[rows, :] = b_c
        br_s[:, rows] = b_r
        ml_s[rows, :] = b_c + cg_c
        dl = jnp.zeros((ch, ng), F32)
        for hh in range(nh):
            sl = slice(hh * M_DH, (hh + 1) * M_DH)
            s_s[hh] = _dot(q_s[rows, sl], kt_s[sl, rows])
        for hh in range(nh):
            s = s_s[hh]
            for d in range(2):
                jj = d * nh + hh
                wgt = jnp.exp(jnp.where(upper if d else lower, g_r[jj:jj + 1, :] - cg_c[:, jj:jj + 1], -jnp.inf))
                p_s[jj] = (s * wgt).astype(BF16)
        for hh in range(nh):
            sl = slice(hh * M_DH, (hh + 1) * M_DH)
            v1 = jnp.concatenate([vo_ref[0, rows, sl], ones_blk], axis=1)
            for d in range(2):
                jj = d * nh + hh
                nd = _dot(p_s[jj], v1)
                h_s[d, rows, sl] = nd[:, :M_DH]
                dl = jnp.where(lane_c == jj, nd[:, M_DH:M_DH + ng], dl)
        dl_s[rows, :] = dl

    def conv_local_body(j, carry):
        conv_chunk(j + 1)
        local_chunk(j)
        return carry

    conv_chunk(jnp.int32(0))
    lax.fori_loop(0, nc - 1, conv_local_body, 0)
    local_chunk(jnp.int32(nc - 1))

    cn_s[...] = jnp.zeros_like(cn_s)
    m_s[...] = jnp.zeros_like(m_s)
    lane_r = lax.broadcasted_iota(jnp.int32, (1, ng), 1)

    def scan_issue(r0, d):
        rows = pl.ds(r0, ch)
        gr = gr_ref[0, :, rows]
        br = br_s[:, rows]
        tot = jnp.sum(_log_sigmoid(gr[ng:]), axis=1, keepdims=True)
        scal = []
        for hh in range(nh):
            fi = d * nh + hh
            sl = slice(hh * M_DH, (hh + 1) * M_DH)
            qcn_s[fi] = _dot(q_s[rows, sl], cn_s[fi].astype(BF16))
            m_old = m_s[fi][:, 0:1]
            b_e = tot[fi:fi + 1, :]
            d_end = b_e - br[fi:fi + 1, :] + gr[fi:fi + 1, :]
            m_end = jnp.max(d_end, axis=-1, keepdims=True)
            m_new = jnp.maximum(b_e + m_old, m_end)
            ktw = (kt_s[sl, rows].astype(F32) * jnp.exp(d_end - m_end)).astype(BF16)
            v1 = jnp.concatenate([vo_ref[0, rows, sl], ones_blk], axis=1)
            u_s[fi] = _dot(ktw, v1)
            scal.append((m_old, m_new, jnp.exp(b_e + m_old - m_new), jnp.exp(m_end - m_new)))
        return scal

    def scan_finish(r0, d, scal):
        rows = pl.ds(r0, ch)
        m_row = jnp.zeros((1, ng), F32)
        for hh in range(nh):
            m_row = jnp.where(lane_r == d * nh + hh, scal[hh][0], m_row)

        inter = bc_s[rows, :] + m_row
        ml = ml_s[rows, :]
        mt = jnp.maximum(inter, ml)
        a = jnp.exp(ml - mt)
        wi = jnp.exp(inter - mt)
        qn = jnp.zeros((ch, ng), F32)
        for hh in range(nh):
            fi = d * nh + hh
            qn = jnp.where(lane_c == fi, qcn_s[fi, :, M_DH:M_DH + ng], qn)
        den = a * dl_s[rows, :] + wi * qn
        rinv = 1.0 / jnp.maximum(jnp.abs(den), jnp.exp(-mt))
        c_loc = a * rinv
        c_int = wi * rinv

        for hh in range(nh):
            fi = d * nh + hh
            sl = slice(hh * M_DH, (hh + 1) * M_DH)
            h_s[d, rows, sl] = (c_loc[:, fi:fi + 1] * h_s[d, rows, sl]
                                + c_int[:, fi:fi + 1] * qcn_s[fi, :, :M_DH])
            cn_s[fi] = scal[hh][2] * cn_s[fi] + scal[hh][3] * u_s[fi]
            m_s[fi] = jnp.broadcast_to(scal[hh][1], (1, M_DH))

    def scan_body(s, carry):
        rf = pl.multiple_of(s * ch, ch)
        rb = pl.multiple_of(jnp.where(s < ncc, ncc - 1 - s, nc - 1 - s + ncc) * ch, ch)
        sf = scan_issue(rf, 0)
        sb = scan_issue(rb, 1)
        scan_finish(rf, 0, sf)
        scan_finish(rb, 1, sb)
        return carry

    lax.fori_loop(0, nc, scan_body, 0)

    mnorm = mnorm_ref[...]

    def out_body(j, carry):
        r0 = pl.multiple_of(j * ch, ch)
        hsum = h_s[0, pl.ds(r0, ch), :] + h_s[1, pl.ds(r0, ch), :]
        og = _sigmoid(vo_ref[0, pl.ds(r0, ch), M_WIDTH:].astype(F32))
        for hh in range(M_HEADS):
            sl = slice(hh * M_DH, (hh + 1) * M_DH)
            ya_ref[0, pl.ds(r0, ch), sl] = (_rms(hsum[:, sl], mnorm[:, sl]) * og[:, sl]).astype(BF16)
        return carry

    lax.fori_loop(0, nc, out_body, 0)


def _mlstm_call(qk, vo, gi, gf, gr, conv, mnorm, lc):
    b, l, _ = qk.shape
    ng = 2 * M_HEADS
    return pl.pallas_call(
        functools.partial(_mlstm_kernel, lc=lc),
        grid=(b,),
        in_specs=[pl.BlockSpec((1, l, 2 * M_WIDTH), lambda bi: (bi, 0, 0), pipeline_mode=pl.Buffered(1)),
                  pl.BlockSpec((1, l, 2 * M_WIDTH), lambda bi: (bi, 0, 0)),
                  pl.BlockSpec((1, l, ng), lambda bi: (bi, 0, 0)),
                  pl.BlockSpec((1, l, ng), lambda bi: (bi, 0, 0)),
                  pl.BlockSpec((1, 2 * ng, l), lambda bi: (bi, 0, 0)),
                  _const_spec(conv.shape), _const_spec(mnorm.shape)],
        out_specs=pl.BlockSpec((1, l, M_WIDTH), lambda bi: (bi, 0, 0)),
        out_shape=jax.ShapeDtypeStruct((b, l, M_WIDTH), BF16),
        scratch_shapes=[pltpu.VMEM((l, M_WIDTH), BF16),
                        pltpu.VMEM((M_WIDTH, l), BF16),
                        pltpu.VMEM((2, l, M_WIDTH), F32),
                        pltpu.VMEM((l, ng), F32),
                        pltpu.VMEM((l, ng), F32),
                        pltpu.VMEM((l, ng), F32),
                        pltpu.VMEM((ng, l), F32),
                        pltpu.VMEM((2 * M_HEADS, M_DH, 2 * M_DH), F32),
                        pltpu.VMEM((2 * M_HEADS, 1, M_DH), F32),
                        pltpu.VMEM((M_HEADS, M_CHUNK, M_CHUNK), F32),
                        pltpu.VMEM((2 * M_HEADS, M_CHUNK, M_CHUNK), BF16),
                        pltpu.VMEM((2 * M_HEADS, M_CHUNK, 2 * M_DH), F32),
                        pltpu.VMEM((2 * M_HEADS, M_DH, 2 * M_DH), F32)],
        compiler_params=_params("parallel"), name="mlstm",
    )(qk, vo, gi, gf, gr, conv, mnorm)


def _attn_kernel(q_ref, k_ref, v_ref, o_ref, s_s, p_s, *, lc, ctx_out):
    tq = q_ref.shape[1]
    l = k_ref.shape[1]
    qi = pl.program_id(2)
    nct = lc // tq
    lane = lax.broadcasted_iota(jnp.int32, (tq, 2 * A_VDIM), 1)

    def run(klen):
        outs = []
        for hh in range(A_HPS):
            sl = slice(hh * A_PAD, (hh + 1) * A_PAD)
            s_s[hh, :, :klen] = _dot_nt(q_ref[0, :, sl], k_ref[0, :klen, sl])
        row_max = [jnp.max(s_s[hh, :, :klen], axis=-1, keepdims=True) for hh in range(A_HPS)]
        for hh in range(A_HPS):
            p_s[hh, :, :klen] = jnp.exp2((s_s[hh, :, :klen] - row_max[hh]).astype(BF16))
        for hh in range(A_HPS):
            sl = slice(hh * A_PAD, (hh + 1) * A_PAD)
            nd = _dot(p_s[hh, :, :klen], v_ref[0, :klen, sl])
            outs.append(nd / pltpu.roll(nd, A_VDIM, 1))
        for pp in range(A_HPS // 2):
            o_ref[0, :, pp * A_PAD:(pp + 1) * A_PAD] = jnp.where(
                lane < A_VDIM, outs[2 * pp], pltpu.roll(outs[2 * pp + 1], A_VDIM, 1)).astype(BF16)

    @pl.when(qi >= nct)
    def _():
        run(l)

    @pl.when(qi < nct)
    def _():
        if ctx_out:
            run(lc)
        else:
            o_ref[...] = jnp.zeros_like(o_ref)


def _attn_call(q, k, v, lc, tq, ctx_out):
    b, l, _ = q.shape
    return pl.pallas_call(
        functools.partial(_attn_kernel, lc=lc, ctx_out=ctx_out),
        grid=(b, A_HEADS // A_HPS, l // tq),
        in_specs=[pl.BlockSpec((1, tq, A_HPS * A_PAD), lambda bi, p, qi: (bi, qi, p)),
                  pl.BlockSpec((1, l, A_HPS * A_PAD), lambda bi, p, qi: (bi, 0, p)),
                  pl.BlockSpec((1, l, A_HPS * A_PAD), lambda bi, p, qi: (bi, 0, p))],
        out_specs=pl.BlockSpec((1, tq, A_HPS * A_VDIM), lambda bi, p, qi: (bi, qi, p)),
        out_shape=jax.ShapeDtypeStruct((b, l, A_WIDTH), BF16),
        scratch_shapes=[pltpu.VMEM((A_HPS, tq, l), F32), pltpu.VMEM((A_HPS, tq, l), BF16)],
        compiler_params=_params("parallel", "parallel", "arbitrary"), name="attn",
    )(q, k, v)


def _merge_kernel(x_ref, mod_ref, modc_ref, ya_ref, yb_ref, yc_ref, br_ref, wpa_ref, wpb_ref, wpc_ref, wout_ref,
                  n2_ref, r_ref, rb_ref, o_ref, h2_o, idx_o, idxt_o, cnt_o, y_s, *, tm, nct, skip_ctx):
    d = x_ref.shape[2]
    nsub = x_ref.shape[1] // tm
    modb = mod_ref[0]
    modc = modc_ref[0]
    tiles = [slice(s * tm, (s + 1) * tm) for s in range(nsub)]
    ctx = [pl.program_id(1) * nsub + s < nct for s in range(nsub)]
    mods = [jnp.where(c, modc, modb) for c in ctx]
    for s, rows in enumerate(tiles):
        br = br_ref[0, rows, :]
        y = (br[:, :d].astype(F32) * _dot(ya_ref[0, rows, :], wpa_ref[...])
             + br[:, d:2 * d].astype(F32) * _dot(yb_ref[0, rows, :], wpb_ref[...])
             + br[:, 2 * d:].astype(F32) * _dot(yc_ref[0, rows, :], wpc_ref[...]))
        y_s[s] = y.astype(BF16)
    for s, rows in enumerate(tiles):
        o_ref[0, rows, :] = x_ref[0, rows, :] + mods[s][2:3] * _dot(y_s[s], wout_ref[...])
    for s, rows in enumerate(tiles):
        logits = _route_logits(o_ref[0, rows, :], mods[s], n2_ref, r_ref, rb_ref, h2_o, rows)
        left_out = ctx[s] if skip_ctx else None
        _route_assign(logits, left_out, d, h2_o, idx_o, idxt_o, cnt_o, rows, s)


def _merge_call(xs, mod, ya, yb, yc, br, w, n2, r3, rb, lc, tm, skip_ctx):
    b, l, d = xs.shape
    nct = lc // tm
    nsub = next(n for n in (3, 2, 1) if l % (n * tm) == 0)
    tg = nsub * tm
    tok = lambda width: pl.BlockSpec((1, tg, width), lambda bi, j: (bi, j, 0))
    consts = [w["wpa"], w["wpb"], w["wpc"], w["wout"], n2, r3, rb]
    return pl.pallas_call(
        functools.partial(_merge_kernel, tm=tm, nct=nct, skip_ctx=skip_ctx), grid=(b, l // tg),
        in_specs=[tok(d), pl.BlockSpec((1, 6, d), lambda bi, j: (bi, 0, 0)),
                  pl.BlockSpec((1, 6, d), lambda bi, j: (b, 0, 0)),
                  tok(M_WIDTH), tok(A_WIDTH), tok(G_WIDTH), tok(3 * d)] + [_const_spec(a.shape) for a in consts],
        out_specs=[tok(d), tok(d + R_PAD), tok(8), pl.BlockSpec((1, 8, tg), lambda bi, j: (bi, 0, j)),
                   pl.BlockSpec((1, nsub, 8, R_PAD), lambda bi, j: (bi, j, 0, 0))],
        out_shape=[jax.ShapeDtypeStruct((b, l, d), F32),
                   jax.ShapeDtypeStruct((b, l, d + R_PAD), BF16),
                   jax.ShapeDtypeStruct((b, l, 8), jnp.int32), jax.ShapeDtypeStruct((b, 8, l), jnp.int32),
                   jax.ShapeDtypeStruct((b, l // tm, 8, R_PAD), F32)],
        scratch_shapes=[pltpu.VMEM((nsub, tm, d), BF16)],
        compiler_params=_params("parallel", "parallel"), name="merge",
    )(xs, mod, mod, ya, yb, yc, br, *consts)


def _route_logits(x, mod, n2_ref, r_ref, rb_ref, h2_o, rows):
    d = x.shape[1]
    h2 = _rms(x, n2_ref[...]) * (1.0 + mod[4:5]) + mod[3:4]
    h2_o[0, rows, :d] = h2.astype(BF16)
    r = r_ref[...]
    pp = sum(_dot(piece, r) for piece in _split3(h2))
    return pp + pltpu.roll(pp, R_PAD - R_SEG, 1) + pltpu.roll(pp, R_PAD - 2 * R_SEG, 1) + rb_ref[...]


def _route_assign(logits, left_out, d, h2_o, idx_o, idxt_o, cnt_o, rows, s):
    tm = logits.shape[0]
    el = logits[:, :N_EXPERTS]
    gl = logits[:, N_EXPERTS:N_EXPERTS + N_GROUPS]
    big = 1e9

    lane_g = lax.broadcasted_iota(jnp.int32, (tm, N_GROUPS), 1).astype(F32)
    gmax = jnp.max(gl, axis=-1, keepdims=True)
    g_sel = jnp.min(jnp.where(gl == gmax, lane_g, big), axis=-1, keepdims=True)
    g_prob = 1.0 / jnp.sum(jnp.exp(gl - gmax), axis=-1, keepdims=True)
    if left_out is not None:
        g_sel = jnp.where(left_out, -1.0, g_sel)

    lane_i = lax.broadcasted_iota(jnp.int32, (tm, N_EXPERTS), 1)
    lane_e = lane_i.astype(F32)
    lane_grp = (lane_i // EXP_PER_GROUP).astype(F32)
    v1 = jnp.where(lane_grp == g_sel, el, -jnp.inf)
    t1 = jnp.max(v1, axis=-1, keepdims=True)
    i1 = jnp.min(jnp.where(v1 == t1, lane_e, big), axis=-1, keepdims=True)
    v2 = jnp.where(lane_e == i1, -jnp.inf, v1)
    t2 = jnp.max(v2, axis=-1, keepdims=True)
    i2 = jnp.min(jnp.where(v2 == t2, lane_e, big), axis=-1, keepdims=True)
    e21 = jnp.exp(t2 - t1)
    w1 = 1.0 / (1.0 + e21)
    w2 = e21 * w1
    comb = (jnp.where(lane_e == i1, w1, 0.0) + jnp.where(lane_e == i2, w2, 0.0)) * g_prob
    tail = jnp.zeros((tm, R_PAD - 3 * N_EXPERTS), BF16)
    h2_o[0, rows, d:] = jnp.concatenate(list(_split3(comb)) + [tail], axis=1)

    lane_p = lax.broadcasted_iota(jnp.int32, (tm, R_PAD), 1)
    onehot = jnp.where(lane_p.astype(F32) == g_sel, 1.0, 0.0)
    ri = lax.broadcasted_iota(jnp.int32, (tm, tm), 0)
    ci = lax.broadcasted_iota(jnp.int32, (tm, tm), 1)
    before = jnp.where(ri > ci, 1.0, 0.0).astype(BF16)
    rank = jnp.sum(_dot(before, onehot.astype(BF16)) * onehot, axis=-1, keepdims=True)
    cnt_o[0, s] = jnp.broadcast_to(jnp.sum(onehot, axis=0, keepdims=True), (8, R_PAD))
    fields = jnp.where(lane_p == 0, g_sel, jnp.where(lane_p == 1, rank, 0.0))
    idx_o[0, rows, :] = fields[:, :8].astype(jnp.int32)
    idxt_o[0, :, rows] = fields.T[:8, :].astype(jnp.int32)


def _experts_kernel(st_ref, h2_ref, idx_ref, idxt_ref, w1_ref, w3_ref, w2_ref, o_ref, hs_s, ys_s, a_s, hid_s, *, tb):
    l = h2_ref.shape[1]
    d = o_ref.shape[2]
    nblk = l // tb
    ch = MOE_CHUNK
    bi = pl.program_id(0)
    g = pl.program_id(1)

    @pl.when(g == 0)
    def _():
        ys_s[...] = jnp.zeros_like(ys_s)

    def group_base(gg):
        return (bi * N_GROUPS + gg) * (nblk + 1)

    def group_offset(upto):
        off = 0
        for gg in range(N_GROUPS - 1):
            padded = ((st_ref[group_base(gg) + nblk] + ch - 1) // ch) * ch
            off = off + jnp.where(gg < upto, padded, 0)
        return off

    base = group_base(g)
    cnt = st_ref[base + nblk]
    goff = group_offset(g)

    def chunk(lo, ch):
        sub_iota = lax.broadcasted_iota(jnp.int32, (ch, tb), 0)
        lane_e = lax.broadcasted_iota(jnp.int32, (ch, N_EXPERTS), 1)
        hs_s[:ch] = jnp.zeros((ch, hs_s.shape[1]), F32)
        for k in range(nblk):
            s_k = st_ref[base + k]
            e_k = st_ref[base + k + 1]
            rows = slice(k * tb, (k + 1) * tb)

            @pl.when(jnp.logical_and(s_k < lo + ch, e_k > lo))
            def _(s_k=s_k, rows=rows):
                it = idxt_ref[0, :, rows]
                pos = jnp.where(it[0:1] == g, it[1:2] + (s_k - lo), -1)
                p = jnp.where(sub_iota == pos, 1.0, 0.0).astype(BF16)
                hs_s[:ch] += _dot(p, h2_ref[0, rows, :])

        hsb = hs_s[:ch, :d].astype(BF16)
        cs = (hs_s[:ch, d:d + N_EXPERTS] + hs_s[:ch, d + N_EXPERTS:d + 2 * N_EXPERTS]
              + hs_s[:ch, d + 2 * N_EXPERTS:d + 3 * N_EXPERTS])
        for e in range(EXP_PER_GROUP):
            a_s[2 * e, :ch] = _dot(hsb, w1_ref[e])
            a_s[2 * e + 1, :ch] = _dot(hsb, w3_ref[e])
        for e in range(EXP_PER_GROUP):
            ce = jnp.sum(jnp.where(lane_e == g * EXP_PER_GROUP + e, cs, 0.0), axis=-1, keepdims=True)
            hid_s[e, :ch] = (_silu(a_s[2 * e, :ch]) * a_s[2 * e + 1, :ch] * ce).astype(BF16)
        y = jnp.zeros((ch, d), F32)
        for e in range(EXP_PER_GROUP):
            y = y + _dot(hid_s[e, :ch], w2_ref[e])
        ys_s[pl.ds(pl.multiple_of(goff + lo, ch), ch), :] = y.astype(BF16)

    half = ch // 2
    nfull = cnt // ch
    rem = cnt - nfull * ch
    nloop = nfull + jnp.where(rem > half, 1, 0)

    def chunk_body(c, carry):
        chunk(c * ch, ch)
        return carry

    lax.fori_loop(0, nloop, chunk_body, 0)

    @pl.when(jnp.logical_and(rem > 0, rem <= half))
    def _():
        chunk(nfull * ch, half)

    @pl.when(g == N_GROUPS - 1)
    def _():
        lane_w = lax.broadcasted_iota(jnp.int32, (tb, 2 * ch), 1)
        for k in range(nblk):
            rows = slice(k * tb, (k + 1) * tb)
            ic = idx_ref[0, rows, :]
            acc = jnp.zeros((tb, d), F32)
            for gg in range(N_GROUPS):
                s_k = st_ref[group_base(gg) + k]
                win = (s_k // ch) * ch
                pos = jnp.where(ic[:, 0:1] == gg, ic[:, 1:2] + (s_k - win), -1)
                q = jnp.where(lane_w == pos, 1.0, 0.0).astype(BF16)
                start = pl.multiple_of(group_offset(gg) + win, ch)
                acc = acc + _dot(q, ys_s[pl.ds(start, 2 * ch), :])
            o_ref[0, rows, :] = acc.astype(BF16)


def _experts_call(starts, h2, idx, idxt, w1, w3, w2, tb):
    b, l, de = h2.shape
    d = de - R_PAD
    whole = lambda width: pl.BlockSpec((1, l, width), lambda bi, g, st: (bi, 0, 0))
    grid_spec = pltpu.PrefetchScalarGridSpec(
        num_scalar_prefetch=1, grid=(b, N_GROUPS),
        in_specs=[pl.BlockSpec((1, l, de), lambda bi, g, st: (bi, 0, 0), pipeline_mode=pl.Buffered(1)),
                  whole(8), pl.BlockSpec((1, 8, l), lambda bi, g, st: (bi, 0, 0)),
                  pl.BlockSpec((EXP_PER_GROUP, d, D_EXPERT), lambda bi, g, st: (g, 0, 0)),
                  pl.BlockSpec((EXP_PER_GROUP, d, D_EXPERT), lambda bi, g, st: (g, 0, 0)),
                  pl.BlockSpec((EXP_PER_GROUP, D_EXPERT, d), lambda bi, g, st: (g, 0, 0))],
        out_specs=whole(d),
        scratch_shapes=[pltpu.VMEM((MOE_CHUNK, de), F32),
                        pltpu.VMEM((l + (N_GROUPS + 1) * MOE_CHUNK, d), BF16),
                        pltpu.VMEM((2 * EXP_PER_GROUP, MOE_CHUNK, D_EXPERT), F32),
                        pltpu.VMEM((EXP_PER_GROUP, MOE_CHUNK, D_EXPERT), BF16)])
    return pl.pallas_call(
        functools.partial(_experts_kernel, tb=tb), grid_spec=grid_spec,
        out_shape=jax.ShapeDtypeStruct((b, l, d), BF16),
        compiler_params=_params("parallel", "arbitrary"), name="experts",
    )(starts, h2, idx, idxt, w1, w3, w2)


def _group_starts(cnt):
    c = cnt[:, :, 0, :N_GROUPS].astype(jnp.int32)
    s = jnp.cumsum(c, axis=1)
    s = jnp.concatenate([jnp.zeros_like(s[:, :1]), s], axis=1)
    return jnp.transpose(s, (0, 2, 1)).reshape(-1)


def _final_kernel(x_ref, mod_ref, f_ref, g_ref, o_ref):
    o_ref[0] = _rms(x_ref[0] + mod_ref[0][5:6] * f_ref[0].astype(F32), g_ref[...])


def _final_call(xs, mod, f, g, lc, tm):
    b, l, d = xs.shape
    off = lc // tm
    lat = pl.BlockSpec((1, tm, d), lambda bi, j: (bi, j + off, 0))
    return pl.pallas_call(
        _final_kernel, grid=(b, (l - lc) // tm),
        in_specs=[lat, pl.BlockSpec((1, 6, d), lambda bi, j: (bi, 0, 0)), lat, _const_spec(g.shape)],
        out_specs=pl.BlockSpec((1, tm, d), lambda bi, j: (bi, j, 0)),
        out_shape=jax.ShapeDtypeStruct((b, l - lc, d), F32),
        compiler_params=_params("parallel", "parallel"), name="final_norm",
    )(xs, mod, f, g)


def _rope_tables(t_len, lc):
    half = A_ROPE // 2
    rows = t_len // GRID_W
    r = jnp.repeat(jnp.arange(rows, dtype=F32), GRID_W)
    col = jnp.tile(jnp.arange(GRID_W, dtype=F32), rows)
    inv = ROPE_THETA ** (-jnp.arange(0, half, 2, dtype=F32) / half)
    ang = jnp.concatenate([r[:, None] * inv, col[:, None] * inv], axis=-1)
    cos = jnp.concatenate([jnp.ones((lc, half), F32), jnp.cos(ang)], axis=0)
    sin = jnp.concatenate([jnp.zeros((lc, half), F32), jnp.sin(ang)], axis=0)
    l = lc + t_len
    ones = jnp.ones((l, A_NOPE), F32)
    zeros = jnp.zeros((l, A_NOPE), F32)
    tail1 = jnp.ones((l, A_PAD - A_NOPE - A_ROPE), F32)
    tail0 = jnp.zeros((l, A_PAD - A_NOPE - A_ROPE), F32)
    zh = jnp.zeros((l, half), F32)
    cos_t = jnp.concatenate([ones, cos, cos, tail1], axis=-1)
    sina_t = jnp.concatenate([zeros, zh, sin, tail0], axis=-1)
    sinb_t = jnp.concatenate([zeros, -sin, zh, tail0], axis=-1)
    return cos_t, sina_t, sinb_t


def _layer_weights(l, w_in, m_gate_b, a_qnorm, a_wuq, a_kvnorm, a_wukv, g_ws, g_bs, g_vnorm,
                   w_pa, w_pb, w_pc, w_out):
    d = w_in.shape[1]
    wi = w_in[l]
    o = 0

    def take(n):
        nonlocal o
        s = wi[:, o:o + n]
        o += n
        return s

    mq, mk, mv, mo, mg = take(M_WIDTH), take(M_WIDTH), take(M_WIDTH), take(M_WIDTH), take(4 * M_HEADS)
    aq, akv, akr = take(A_QRANK), take(A_KVRANK), take(A_ROPE)
    gu, gv = take(G_WIDTH), take(G_WIDTH)
    br = take(3 * d)
    nh = M_HEADS
    gb = m_gate_b[l]
    mgo = jnp.concatenate([mg[:, :nh], mg[:, 2 * nh:3 * nh], mg[:, nh:2 * nh], mg[:, 3 * nh:]], axis=1)
    gbo = jnp.concatenate([gb[:nh], gb[2 * nh:3 * nh], gb[nh:2 * nh], gb[3 * nh:]])
    akr_pad = jnp.concatenate([jnp.zeros((d, A_NOPE), F32), akr,
                               jnp.zeros((d, A_PAD - A_NOPE - A_ROPE), F32)], axis=1)
    wuq = a_wuq[l].reshape(A_QRANK, A_HEADS, A_NOPE + A_ROPE)
    wuq = jnp.pad(wuq, ((0, 0), (0, 0), (0, A_PAD - A_NOPE - A_ROPE))).reshape(A_QRANK, A_HEADS * A_PAD)
    wukv = a_wukv[l].reshape(A_KVRANK, A_HEADS, A_NOPE + A_VDIM)
    wuk = jnp.pad(wukv[:, :, :A_NOPE], ((0, 0), (0, 0), (0, A_PAD - A_NOPE))).reshape(A_KVRANK, A_HEADS * A_PAD)
    wuv = jnp.pad(wukv[:, :, A_NOPE:], ((0, 0), (0, 0), (0, A_PAD - A_VDIM))).reshape(A_KVRANK, A_HEADS * A_PAD)
    vone = jnp.tile(jnp.concatenate([jnp.zeros((A_VDIM,), F32), jnp.ones((A_PAD - A_VDIM,), F32)]),
                    A_HEADS).reshape(1, A_HEADS * A_PAD)
    gbs = jnp.repeat(g_bs[l].T, G_DG, axis=1)
    return dict(
        wqk=jnp.concatenate([mq, mk], 1).astype(BF16), wvo=jnp.concatenate([mv, mo], 1).astype(BF16),
        wgt=jnp.pad(mgo, ((0, 0), (0, G_PAD - 4 * nh))).astype(BF16),
        gbt=jnp.pad(gbo, (0, G_PAD - 4 * nh)).reshape(1, G_PAD),
        wa=jnp.concatenate([aq, akv, akr_pad], 1).astype(BF16),
        wg=jnp.concatenate([gu, gv], 1).astype(BF16), wbr=br.astype(BF16),
        aqn=a_qnorm[l].reshape(1, -1), akvn=a_kvnorm[l].reshape(1, -1),
        wuq=wuq.astype(BF16), wuk=wuk.astype(BF16), wuv=wuv.astype(BF16), vone=vone,
        gvn=g_vnorm[l].reshape(1, -1), gws=g_ws[l].astype(BF16), gbs=gbs,
        wpa=w_pa[l].astype(BF16), wpb=w_pb[l].astype(BF16), wpc=w_pc[l].astype(BF16),
        wout=w_out[l].astype(BF16))


def _router_weights(r_group, r_group_b, r_expert, r_expert_b):
    d = r_group.shape[0]
    pad = R_SEG - N_EXPERTS - N_GROUPS
    r = jnp.concatenate([r_expert, r_group, jnp.zeros((d, pad), F32)], axis=1)
    r3 = jnp.concatenate(list(_split3(r)) + [jnp.zeros((d, R_PAD - 3 * R_SEG), BF16)], axis=1)
    rb = jnp.concatenate([r_expert_b, r_group_b, jnp.zeros((R_PAD - N_EXPERTS - N_GROUPS,), F32)])
    return r3, rb.reshape(1, R_PAD)


def _tile(n, lc, candidates):
    for t in candidates:
        if n % t == 0 and lc % t == 0:
            return t
    raise ValueError("sequence lengths must be multiples of 128")


def kernel(x, c, ctx, c_ctx, w_ada, b_ada, norm1, norm2, final_norm, w_in, m_conv, m_gate_b, m_norm, a_qnorm, a_wuq, a_kvnorm, a_wukv, g_ws, g_bs, g_vnorm, w_pa, w_pb, w_pc, w_out, r_group, r_group_b, r_expert, r_expert_b, e_w1, e_w3, e_w2):
    b, t_len, d = x.shape
    lc = ctx.shape[1]
    l = lc + t_len
    depth = w_in.shape[0]
    tm = _tile(l, lc, (256, 128))

    xs = jnp.concatenate([ctx, x], axis=1)
    cv = jnp.concatenate([c, c_ctx[None, :]], axis=0)
    mod_all = _ada_call(cv, w_ada, b_ada).reshape(depth, b + 1, 6, d)
    tabs = _rope_tables(t_len, lc)

    f = None
    mod_prev = None
    for li in range(depth):
        last = li == depth - 1
        mod = mod_all[li]
        w = _layer_weights(li, w_in, m_gate_b, a_qnorm, a_wuq, a_kvnorm, a_wukv, g_ws, g_bs, g_vnorm,
                           w_pa, w_pb, w_pc, w_out)
        outs = _inproj_call(xs, mod, norm1[li].reshape(1, d), w, tabs, lc, tm, f, mod_prev)
        qk, vo, gi, gf, gr, q, k, v, yc, br = outs[:10]
        if f is not None:
            xs = outs[10]
        ya = _mlstm_call(qk, vo, gi, gf, gr, m_conv[li], m_norm[li].reshape(1, -1), lc)
        yb = _attn_call(q, k, v, lc, tm, not last)
        r3, rb = _router_weights(r_group[li], r_group_b[li], r_expert[li], r_expert_b[li])
        xs, h2, idx, idxt, cnt = _merge_call(xs, mod, ya, yb, yc, br, w, norm2[li].reshape(1, d),
                                             r3, rb, lc, tm, last)
        f = _experts_call(_group_starts(cnt), h2, idx, idxt, e_w1[li].astype(BF16),
                          e_w3[li].astype(BF16), e_w2[li].astype(BF16), tm)
        mod_prev = mod
    return _final_call(xs, mod_prev, f, final_norm.reshape(1, d), lc, tm)
```

```python
import functools

import jax
import jax.numpy as jnp
from jax import lax
from jax.experimental import pallas as pl
from jax.experimental.pallas import tpu as pltpu

F32 = jnp.float32
BF16 = jnp.bfloat16

EPS = 1e-6
GRID_W = 64
ROPE_THETA = 10000.0

M_HEADS = 4
M_DH = 128
M_WIDTH = M_HEADS * M_DH
M_CHUNK = 128
G_PAD = 128

A_HEADS = 8
A_NOPE = 64
A_ROPE = 32
A_VDIM = 64
A_QRANK = 384
A_KVRANK = 256
A_WIDTH = A_HEADS * A_VDIM
A_PAD = 128
A_HPS = 4
ATT_SCALE = (A_NOPE + A_ROPE) ** -0.5
LOG2E = 1.4426950408889634

G_GROUPS = 4
G_CHUNK = 128
G_WIDTH = 512
G_DG = G_WIDTH // G_GROUPS

N_GROUPS = 4
EXP_PER_GROUP = 4
N_EXPERTS = N_GROUPS * EXP_PER_GROUP
D_EXPERT = 512
R_PAD = 128
R_SEG = 32
MOE_CHUNK = 256

VMEM_LIMIT = 56 * 1024 * 1024


def _dot(a, b):
    return jnp.dot(a, b, preferred_element_type=F32)


def _dot_nt(a, b):
    return lax.dot_general(a, b, (((1,), (1,)), ((), ())), preferred_element_type=F32)


def _dot_tn(a, b):
    return lax.dot_general(a, b, (((0,), (0,)), ((), ())), preferred_element_type=F32)


def _split3(x):
    hi = x.astype(BF16)
    r = x - hi.astype(F32)
    mid = r.astype(BF16)
    lo = (r - mid.astype(F32)).astype(BF16)
    return hi, mid, lo


def _sigmoid(x):
    return 1.0 / (1.0 + jnp.exp(-x))


def _silu(x):
    return x * _sigmoid(x)


def _log_sigmoid(x):
    return jnp.minimum(x, 0.0) - jnp.log1p(jnp.exp(-jnp.abs(x)))


def _gelu(x):
    return 0.5 * x * (1.0 + lax.erf(x * (2.0 ** -0.5)))


def _rms(x, g):
    return x * lax.rsqrt(jnp.mean(x * x, axis=-1, keepdims=True) + EPS) * g


def _params(*sem):
    return pltpu.CompilerParams(dimension_semantics=sem, vmem_limit_bytes=VMEM_LIMIT)


def _const_spec(shape):
    nd = len(shape)
    return pl.BlockSpec(shape, lambda *_: (0,) * nd, pipeline_mode=pl.Buffered(1))


def _ada_kernel(cv_ref, w_ref, b_ref, o_ref):
    s = _silu(cv_ref[...])
    o_ref[0] = _dot(s.astype(BF16), w_ref[0].astype(BF16)) + b_ref[0]


def _ada_call(cv, w_ada, b_ada):
    depth, d, n6 = w_ada.shape
    rows = cv.shape[0]
    tn = n6 // 4
    return pl.pallas_call(
        _ada_kernel,
        grid=(depth, n6 // tn),
        in_specs=[pl.BlockSpec((rows, d), lambda l, j: (0, 0)),
                  pl.BlockSpec((1, d, tn), lambda l, j: (l, 0, j)),
                  pl.BlockSpec((1, 1, tn), lambda l, j: (l, 0, j))],
        out_specs=pl.BlockSpec((1, rows, tn), lambda l, j: (l, 0, j)),
        out_shape=jax.ShapeDtypeStruct((depth, rows, n6), F32),
        compiler_params=_params("parallel", "parallel"),
        name="ada",
    )(cv, w_ada, b_ada.reshape(depth, 1, n6))


def _inproj_kernel(*refs, has_f):
    za_s, zg_s, zbr_s = refs[-3:]
    refs = refs[:-3]
    if has_f:
        f_ref, modp_ref, x_o = refs[0], refs[1], refs[-1]
        refs = refs[2:-1]
    (x_ref, mod_ref, n1_ref, wqk_ref, wvo_ref, wgt_ref, gbt_ref,
     wa_ref, wg_ref, wbr_ref, aqn_ref, akvn_ref, wuq_ref, wuk_ref, wuv_ref, vone_ref,
     cos_ref, sina_ref, sinb_ref, gvn_ref, gws_ref, gbs_ref,
     qk_o, vo_o, gi_o, gf_o, gr_o, q_o, k_o, v_o, yc_o, br_o) = refs
    tm = x_ref.shape[1]
    mod = mod_ref[0]
    x = x_ref[0]
    if has_f:
        x = x + modp_ref[0][5:6] * f_ref[0].astype(F32)
        x_o[0] = x
    h = _rms(x, n1_ref[...]) * (1.0 + mod[1:2]) + mod[0:1]
    hb = h.astype(BF16)

    qk_o[0] = _dot(hb, wqk_ref[...])
    vo_o[0] = _dot(hb, wvo_ref[...]).astype(BF16)
    ng = gi_o.shape[2]
    gates = _dot(hb, wgt_ref[...]) + gbt_ref[...]
    gi_o[0] = gates[:, :ng]
    gf_o[0] = pltpu.roll(gates, gates.shape[1] - ng, 1)[:, :ng]
    gr_o[0] = gates.T[:2 * ng, :]

    za_s[...] = _dot(hb, wa_ref[...])
    zg_s[...] = _dot(hb, wg_ref[...])
    zbr_s[...] = _dot(hb, wbr_ref[...])

    aqn = _rms(za_s[:, :A_QRANK], aqn_ref[...]).astype(BF16)
    akvn = _rms(za_s[:, A_QRANK:A_QRANK + A_KVRANK], akvn_ref[...]).astype(BF16)
    cos = cos_ref[...]
    sina = sina_ref[...]
    sinb = sinb_ref[...]
    half = A_ROPE // 2

    def rope(t):
        return t * cos + pltpu.roll(t, half, 1) * sina + pltpu.roll(t, A_PAD - half, 1) * sinb

    kr = rope(za_s[:, A_QRANK + A_KVRANK:])
    qp = _dot(aqn, wuq_ref[...])
    kp = _dot(akvn, wuk_ref[...])
    for hh in range(A_HEADS):
        sl = slice(hh * A_PAD, (hh + 1) * A_PAD)
        q_o[0, :, sl] = (rope(qp[:, sl]) * (ATT_SCALE * LOG2E)).astype(BF16)
        k_o[0, :, sl] = (kp[:, sl] + kr).astype(BF16)
    v_o[0] = (_dot(akvn, wuv_ref[...]) + vone_ref[...]).astype(BF16)

    gu = _gelu(zg_s[:, :G_WIDTH])
    gv = _gelu(zg_s[:, G_WIDTH:])
    gvn = gvn_ref[...]
    bias = gbs_ref[...]
    for g in range(G_GROUPS):
        sl = slice(g * G_DG, (g + 1) * G_DG)
        xn = _rms(gv[:, sl], gvn[:, sl]).astype(BF16)
        ws = gws_ref[g]
        for ci in range(tm // G_CHUNK):
            r = slice(ci * G_CHUNK, (ci + 1) * G_CHUNK)
            sg = _dot(ws, xn[r]) + bias[:, sl]
            yc_o[0, r, sl] = (gu[r, sl] * sg).astype(BF16)

    br_o[0] = _sigmoid(zbr_s[...]).astype(BF16)


def _inproj_call(xs, mod, n1, w, tabs, lc, tm, f=None, mod_prev=None):
    b, l, d = xs.shape
    nct = lc // tm
    has_f = f is not None
    tok = lambda width: pl.BlockSpec((1, tm, width), lambda bi, j: (bi, j, 0))
    modspec = pl.BlockSpec((1, 6, d), lambda bi, j: (jnp.where(j < nct, b, bi), 0, 0))
    tab = pl.BlockSpec((tm, A_PAD), lambda bi, j: (j, 0))
    consts = [n1, w["wqk"], w["wvo"], w["wgt"], w["gbt"], w["wa"], w["wg"], w["wbr"],
              w["aqn"], w["akvn"], w["wuq"], w["wuk"], w["wuv"], w["vone"]]
    consts2 = [w["gvn"], w["gws"], w["gbs"]]
    in_specs = ([tok(d), modspec] + [_const_spec(a.shape) for a in consts] + [tab, tab, tab]
                + [_const_spec(a.shape) for a in consts2])
    args = [xs, mod, *consts, *tabs, *consts2]
    ng = 4 * M_HEADS
    out_shape = [jax.ShapeDtypeStruct((b, l, 2 * M_WIDTH), F32),
                 jax.ShapeDtypeStruct((b, l, 2 * M_WIDTH), BF16),
                 jax.ShapeDtypeStruct((b, l, ng // 2), F32),
                 jax.ShapeDtypeStruct((b, l, ng // 2), F32),
                 jax.ShapeDtypeStruct((b, ng, l), F32),
                 jax.ShapeDtypeStruct((b, l, A_HEADS * A_PAD), BF16),
                 jax.ShapeDtypeStruct((b, l, A_HEADS * A_PAD), BF16),
                 jax.ShapeDtypeStruct((b, l, A_HEADS * A_PAD), BF16),
                 jax.ShapeDtypeStruct((b, l, G_WIDTH), BF16),
                 jax.ShapeDtypeStruct((b, l, 3 * d), BF16)]
    out_specs = [tok(2 * M_WIDTH), tok(2 * M_WIDTH), tok(ng // 2), tok(ng // 2),
                 pl.BlockSpec((1, ng, tm), lambda bi, j: (bi, 0, j)),
                 tok(A_HEADS * A_PAD), tok(A_HEADS * A_PAD), tok(A_HEADS * A_PAD), tok(G_WIDTH), tok(3 * d)]
    if has_f:
        in_specs = [tok(d), modspec] + in_specs
        args = [f, mod_prev] + args
        out_shape.append(jax.ShapeDtypeStruct((b, l, d), F32))
        out_specs.append(tok(d))
    return pl.pallas_call(
        functools.partial(_inproj_kernel, has_f=has_f), grid=(b, l // tm), in_specs=in_specs,
        out_specs=out_specs, out_shape=out_shape,
        scratch_shapes=[pltpu.VMEM((tm, w["wa"].shape[1]), F32), pltpu.VMEM((tm, 2 * G_WIDTH), F32),
                        pltpu.VMEM((tm, 3 * d), F32)],
        compiler_params=_params("parallel", "parallel"), name="inproj",
    )(*args)


def _scan(x, op, fill, axis, reverse):
    n = x.shape[axis]
    idx = lax.broadcasted_iota(jnp.int32, x.shape, axis)
    k = 1
    while k < n:
        if reverse:
            x = op(x, jnp.where(idx >= n - k, fill, pltpu.roll(x, n - k, axis)))
        else:
            x = op(x, jnp.where(idx < k, fill, pltpu.roll(x, k, axis)))
        k *= 2
    return x


def _mlstm_kernel(qk_ref, vo_ref, gi_ref, gf_ref, gr_ref, conv_ref, mnorm_ref, ya_ref,
                  q_s, kt_s, h_s, bc_s, ml_s, dl_s, br_s, cn_s, m_s, s_s, p_s, qcn_s, u_s, *, lc):
    l = qk_ref.shape[1]
    ch = M_CHUNK
    nc = l // ch
    ncc = lc // ch
    nh = M_HEADS
    ng = 2 * nh
    w = conv_ref[...]
    row = lax.broadcasted_iota(jnp.int32, (ch, 1), 0)

    def conv_chunk(j):
        r0 = pl.multiple_of(j * ch, ch)
        cur = qk_ref[0, pl.ds(r0, ch), :]
        prev8 = qk_ref[0, pl.ds(pl.multiple_of(jnp.maximum(r0 - 8, 0), 8), 8), :]
        next8 = qk_ref[0, pl.ds(pl.multiple_of(jnp.minimum(r0 + ch, l - 8), 8), 8), :]
        seg_start = jnp.logical_or(j == 0, j == ncc)
        seg_end = jnp.logical_or(j == ncc - 1, j == nc - 1)
        pe = jnp.where(seg_start, 0.0, prev8[7:8, :])
        ne = jnp.where(seg_end, 0.0, next8[0:1, :])
        xp = jnp.where(row == 0, pe, pltpu.roll(cur, 1, 0))
        xn = jnp.where(row == ch - 1, ne, pltpu.roll(cur, ch - 1, 0))
        y = _silu(xp * w[0:1] + cur * w[1:2] + xn * w[2:3])
        q_s[pl.ds(r0, ch), :] = (y[:, :M_WIDTH] * (M_DH ** -0.5)).astype(BF16)
        kt_s[:, pl.ds(r0, ch)] = y[:, M_WIDTH:].T.astype(BF16)

    ri = lax.broadcasted_iota(jnp.int32, (ch, ch), 0)
    ci = lax.broadcasted_iota(jnp.int32, (ch, ch), 1)
    lower = ri >= ci
    upper = ri <= ci
    ones_blk = jnp.ones((ch, M_DH), BF16)
    fwd_c = lax.broadcasted_iota(jnp.int32, (ch, ng), 1) < nh
    fwd_r = lax.broadcasted_iota(jnp.int32, (ng, ch), 0) < nh
    lane_c = lax.broadcasted_iota(jnp.int32, (ch, ng), 1)

    def local_chunk(j):
        r0 = pl.multiple_of(j * ch, ch)
        rows = pl.ds(r0, ch)
        lfc = _log_sigmoid(gf_ref[0, rows, :])
        gr = gr_ref[0, :, rows]
        lfr = _log_sigmoid(gr[ng:])
        pre_c = _scan(lfc, jnp.add, 0.0, 0, False)
        pre_r = _scan(lfr, jnp.add, 0.0, 1, False)
        b_c = jnp.where(fwd_c, pre_c, jnp.sum(lfc, axis=0, keepdims=True) + lfc - pre_c)
        b_r = jnp.where(fwd_r, pre_r, jnp.sum(lfr, axis=1, keepdims=True) + lfr - pre_r)
        g_c = gi_ref[0, rows, :] - b_c
        g_r = gr[:ng] - b_r
        cg_c = jnp.where(fwd_c, _scan(g_c, jnp.maximum, -jnp.inf, 0, False),
                         _scan(g_c, jnp.maximum, -jnp.inf, 0, True))
        bc_s[rows, :] = b_c
        br_s[:, rows] = b_r
        ml_s[rows, :] = b_c + cg_c
        dl = jnp.zeros((ch, ng), F32)
        for hh in range(nh):
            sl = slice(hh * M_DH, (hh + 1) * M_DH)
            s_s[hh] = _dot(q_s[rows, sl], kt_s[sl, rows])
        for hh in range(nh):
            s = s_s[hh]
            for d in range(2):
                jj = d * nh + hh
                wgt = jnp.exp(jnp.where(upper if d else lower, g_r[jj:jj + 1, :] - cg_c[:, jj:jj + 1], -jnp.inf))
                p_s[jj] = (s * wgt).astype(BF16)
        for hh in range(nh):
            sl = slice(hh * M_DH, (hh + 1) * M_DH)
            v1 = jnp.concatenate([vo_ref[0, rows, sl], ones_blk], axis=1)
            for d in range(2):
                jj = d * nh + hh
                nd = _dot(p_s[jj], v1)
                h_s[d, rows, sl] = nd[:, :M_DH]
                dl = jnp.where(lane_c == jj, nd[:, M_DH:M_DH + ng], dl)
        dl_s[rows, :] = dl

    def conv_local_body(j, carry):
        conv_chunk(j + 1)
        local_chunk(j)
        return carry

    conv_chunk(jnp.int32(0))
    lax.fori_loop(0, nc - 1, conv_local_body, 0)
    local_chunk(jnp.int32(nc - 1))

    cn_s[...] = jnp.zeros_like(cn_s)
    m_s[...] = jnp.zeros_like(m_s)
    lane_r = lax.broadcasted_iota(jnp.int32, (1, ng), 1)

    def scan_issue(r0, d):
        rows = pl.ds(r0, ch)
        gr = gr_ref[0, :, rows]
        br = br_s[:, rows]
        tot = jnp.sum(_log_sigmoid(gr[ng:]), axis=1, keepdims=True)
        scal = []
        for hh in range(nh):
            fi = d * nh + hh
            sl = slice(hh * M_DH, (hh + 1) * M_DH)
            qcn_s[fi] = _dot(q_s[rows, sl], cn_s[fi].astype(BF16))
            m_old = m_s[fi][:, 0:1]
            b_e = tot[fi:fi + 1, :]
            d_end = b_e - br[fi:fi + 1, :] + gr[fi:fi + 1, :]
            m_end = jnp.max(d_end, axis=-1, keepdims=True)
            m_new = jnp.maximum(b_e + m_old, m_end)
            ktw = (kt_s[sl, rows].astype(F32) * jnp.exp(d_end - m_end)).astype(BF16)
            v1 = jnp.concatenate([vo_ref[0, rows, sl], ones_blk], axis=1)
            u_s[fi] = _dot(ktw, v1)
            scal.append((m_old, m_new, jnp.exp(b_e + m_old - m_new), jnp.exp(m_end - m_new)))
        return scal

    def scan_finish(r0, d, scal):
        rows = pl.ds(r0, ch)
        m_row = jnp.zeros((1, ng), F32)
        for hh in range(nh):
            m_row = jnp.where(lane_r == d * nh + hh, scal[hh][0], m_row)

        inter = bc_s[rows, :] + m_row
        ml = ml_s[rows, :]
        mt = jnp.maximum(inter, ml)
        a = jnp.exp(ml - mt)
        wi = jnp.exp(inter - mt)
        qn = jnp.zeros((ch, ng), F32)
        for hh in range(nh):
            fi = d * nh + hh
            qn = jnp.where(lane_c == fi, qcn_s[fi, :, M_DH:M_DH + ng], qn)
        den = a * dl_s[rows, :] + wi * qn
        rinv = 1.0 / jnp.maximum(jnp.abs(den), jnp.exp(-mt))
        c_loc = a * rinv
        c_int = wi * rinv

        for hh in range(nh):
            fi = d * nh + hh
            sl = slice(hh * M_DH, (hh + 1) * M_DH)
            h_s[d, rows, sl] = (c_loc[:, fi:fi + 1] * h_s[d, rows, sl]
                                + c_int[:, fi:fi + 1] * qcn_s[fi, :, :M_DH])
            cn_s[fi] = scal[hh][2] * cn_s[fi] + scal[hh][3] * u_s[fi]
            m_s[fi] = jnp.broadcast_to(scal[hh][1], (1, M_DH))

    def scan_body(s, carry):
        rf = pl.multiple_of(s * ch, ch)
        rb = pl.multiple_of(jnp.where(s < ncc, ncc - 1 - s, nc - 1 - s + ncc) * ch, ch)
        sf = scan_issue(rf, 0)
        sb = scan_issue(rb, 1)
        scan_finish(rf, 0, sf)
        scan_finish(rb, 1, sb)
        return carry

    lax.fori_loop(0, nc, scan_body, 0)

    mnorm = mnorm_ref[...]

    def out_body(j, carry):
        r0 = pl.multiple_of(j * ch, ch)
        hsum = h_s[0, pl.ds(r0, ch), :] + h_s[1, pl.ds(r0, ch), :]
        og = _sigmoid(vo_ref[0, pl.ds(r0, ch), M_WIDTH:].astype(F32))
        for hh in range(M_HEADS):
            sl = slice(hh * M_DH, (hh + 1) * M_DH)
            ya_ref[0, pl.ds(r0, ch), sl] = (_rms(hsum[:, sl], mnorm[:, sl]) * og[:, sl]).astype(BF16)
        return carry

    lax.fori_loop(0, nc, out_body, 0)


def _mlstm_call(qk, vo, gi, gf, gr, conv, mnorm, lc):
    b, l, _ = qk.shape
    ng = 2 * M_HEADS
    return pl.pallas_call(
        functools.partial(_mlstm_kernel, lc=lc),
        grid=(b,),
        in_specs=[pl.BlockSpec((1, l, 2 * M_WIDTH), lambda bi: (bi, 0, 0), pipeline_mode=pl.Buffered(1)),
                  pl.BlockSpec((1, l, 2 * M_WIDTH), lambda bi: (bi, 0, 0)),
                  pl.BlockSpec((1, l, ng), lambda bi: (bi, 0, 0)),
                  pl.BlockSpec((1, l, ng), lambda bi: (bi, 0, 0)),
                  pl.BlockSpec((1, 2 * ng, l), lambda bi: (bi, 0, 0)),
                  _const_spec(conv.shape), _const_spec(mnorm.shape)],
        out_specs=pl.BlockSpec((1, l, M_WIDTH), lambda bi: (bi, 0, 0)),
        out_shape=jax.ShapeDtypeStruct((b, l, M_WIDTH), BF16),
        scratch_shapes=[pltpu.VMEM((l, M_WIDTH), BF16),
                        pltpu.VMEM((M_WIDTH, l), BF16),
                        pltpu.VMEM((2, l, M_WIDTH), F32),
                        pltpu.VMEM((l, ng), F32),
                        pltpu.VMEM((l, ng), F32),
                        pltpu.VMEM((l, ng), F32),
                        pltpu.VMEM((ng, l), F32),
                        pltpu.VMEM((2 * M_HEADS, M_DH, 2 * M_DH), F32),
                        pltpu.VMEM((2 * M_HEADS, 1, M_DH), F32),
                        pltpu.VMEM((M_HEADS, M_CHUNK, M_CHUNK), F32),
                        pltpu.VMEM((2 * M_HEADS, M_CHUNK, M_CHUNK), BF16),
                        pltpu.VMEM((2 * M_HEADS, M_CHUNK, 2 * M_DH), F32),
                        pltpu.VMEM((2 * M_HEADS, M_DH, 2 * M_DH), F32)],
        compiler_params=_params("parallel"), name="mlstm",
    )(qk, vo, gi, gf, gr, conv, mnorm)


def _attn_kernel(q_ref, k_ref, v_ref, o_ref, s_s, p_s, *, lc, ctx_out):
    tq = q_ref.shape[1]
    l = k_ref.shape[1]
    qi = pl.program_id(2)
    nct = lc // tq
    lane = lax.broadcasted_iota(jnp.int32, (tq, 2 * A_VDIM), 1)

    def run(klen):
        outs = []
        for hh in range(A_HPS):
            sl = slice(hh * A_PAD, (hh + 1) * A_PAD)
            s_s[hh, :, :klen] = _dot_nt(q_ref[0, :, sl], k_ref[0, :klen, sl])
        row_max = [jnp.max(s_s[hh, :, :klen], axis=-1, keepdims=True) for hh in range(A_HPS)]
        for hh in range(A_HPS):
            p_s[hh, :, :klen] = jnp.exp2((s_s[hh, :, :klen] - row_max[hh]).astype(BF16))
        for hh in range(A_HPS):
            sl = slice(hh * A_PAD, (hh + 1) * A_PAD)
            nd = _dot(p_s[hh, :, :klen], v_ref[0, :klen, sl])
            outs.append(nd / pltpu.roll(nd, A_VDIM, 1))
        for pp in range(A_HPS // 2):
            o_ref[0, :, pp * A_PAD:(pp + 1) * A_PAD] = jnp.where(
                lane < A_VDIM, outs[2 * pp], pltpu.roll(outs[2 * pp + 1], A_VDIM, 1)).astype(BF16)

    @pl.when(qi >= nct)
    def _():
        run(l)

    @pl.when(qi < nct)
    def _():
        if ctx_out:
            run(lc)
        else:
            o_ref[...] = jnp.zeros_like(o_ref)


def _attn_call(q, k, v, lc, tq, ctx_out):
    b, l, _ = q.shape
    return pl.pallas_call(
        functools.partial(_attn_kernel, lc=lc, ctx_out=ctx_out),
        grid=(b, A_HEADS // A_HPS, l // tq),
        in_specs=[pl.BlockSpec((1, tq, A_HPS * A_PAD), lambda bi, p, qi: (bi, qi, p)),
                  pl.BlockSpec((1, l, A_HPS * A_PAD), lambda bi, p, qi: (bi, 0, p)),
                  pl.BlockSpec((1, l, A_HPS * A_PAD), lambda bi, p, qi: (bi, 0, p))],
        out_specs=pl.BlockSpec((1, tq, A_HPS * A_VDIM), lambda bi, p, qi: (bi, qi, p)),
        out_shape=jax.ShapeDtypeStruct((b, l, A_WIDTH), BF16),
        scratch_shapes=[pltpu.VMEM((A_HPS, tq, l), F32), pltpu.VMEM((A_HPS, tq, l), BF16)],
        compiler_params=_params("parallel", "parallel", "arbitrary"), name="attn",
    )(q, k, v)


def _merge_kernel(x_ref, mod_ref, modc_ref, ya_ref, yb_ref, yc_ref, br_ref, wpa_ref, wpb_ref, wpc_ref, wout_ref,
                  n2_ref, r_ref, rb_ref, o_ref, h2_o, idx_o, idxt_o, cnt_o, y_s, *, tm, nct, skip_ctx):
    d = x_ref.shape[2]
    nsub = x_ref.shape[1] // tm
    modb = mod_ref[0]
    modc = modc_ref[0]
    tiles = [slice(s * tm, (s + 1) * tm) for s in range(nsub)]
    ctx = [pl.program_id(1) * nsub + s < nct for s in range(nsub)]
    mods = [jnp.where(c, modc, modb) for c in ctx]
    for s, rows in enumerate(tiles):
        br = br_ref[0, rows, :]
        y = (br[:, :d].astype(F32) * _dot(ya_ref[0, rows, :], wpa_ref[...])
             + br[:, d:2 * d].astype(F32) * _dot(yb_ref[0, rows, :], wpb_ref[...])
             + br[:, 2 * d:].astype(F32) * _dot(yc_ref[0, rows, :], wpc_ref[...]))
        y_s[s] = y.astype(BF16)
    for s, rows in enumerate(tiles):
        o_ref[0, rows, :] = x_ref[0, rows, :] + mods[s][2:3] * _dot(y_s[s], wout_ref[...])
    for s, rows in enumerate(tiles):
        logits = _route_logits(o_ref[0, rows, :], mods[s], n2_ref, r_ref, rb_ref, h2_o, rows)
        left_out = ctx[s] if skip_ctx else None
        _route_assign(logits, left_out, d, h2_o, idx_o, idxt_o, cnt_o, rows, s)


def _merge_call(xs, mod, ya, yb, yc, br, w, n2, r3, rb, lc, tm, skip_ctx):
    b, l, d = xs.shape
    nct = lc // tm
    nsub = next(n for n in (3, 2, 1) if l % (n * tm) == 0)
    tg = nsub * tm
    tok = lambda width: pl.BlockSpec((1, tg, width), lambda bi, j: (bi, j, 0))
    consts = [w["wpa"], w["wpb"], w["wpc"], w["wout"], n2, r3, rb]
    return pl.pallas_call(
        functools.partial(_merge_kernel, tm=tm, nct=nct, skip_ctx=skip_ctx), grid=(b, l // tg),
        in_specs=[tok(d), pl.BlockSpec((1, 6, d), lambda bi, j: (bi, 0, 0)),
                  pl.BlockSpec((1, 6, d), lambda bi, j: (b, 0, 0)),
                  tok(M_WIDTH), tok(A_WIDTH), tok(G_WIDTH), tok(3 * d)] + [_const_spec(a.shape) for a in consts],
        out_specs=[tok(d), tok(d + R_PAD), tok(8), pl.BlockSpec((1, 8, tg), lambda bi, j: (bi, 0, j)),
                   pl.BlockSpec((1, nsub, 8, R_PAD), lambda bi, j: (bi, j, 0, 0))],
        out_shape=[jax.ShapeDtypeStruct((b, l, d), F32),
                   jax.ShapeDtypeStruct((b, l, d + R_PAD), BF16),
                   jax.ShapeDtypeStruct((b, l, 8), jnp.int32), jax.ShapeDtypeStruct((b, 8, l), jnp.int32),
                   jax.ShapeDtypeStruct((b, l // tm, 8, R_PAD), F32)],
        scratch_shapes=[pltpu.VMEM((nsub, tm, d), BF16)],
        compiler_params=_params("parallel", "parallel"), name="merge",
    )(xs, mod, mod, ya, yb, yc, br, *consts)


def _route_logits(x, mod, n2_ref, r_ref, rb_ref, h2_o, rows):
    d = x.shape[1]
    h2 = _rms(x, n2_ref[...]) * (1.0 + mod[4:5]) + mod[3:4]
    h2_o[0, rows, :d] = h2.astype(BF16)
    r = r_ref[...]
    pp = sum(_dot(piece, r) for piece in _split3(h2))
    return pp + pltpu.roll(pp, R_PAD - R_SEG, 1) + pltpu.roll(pp, R_PAD - 2 * R_SEG, 1) + rb_ref[...]


def _route_assign(logits, left_out, d, h2_o, idx_o, idxt_o, cnt_o, rows, s):
    tm = logits.shape[0]
    el = logits[:, :N_EXPERTS]
    gl = logits[:, N_EXPERTS:N_EXPERTS + N_GROUPS]
    big = 1e9

    lane_g = lax.broadcasted_iota(jnp.int32, (tm, N_GROUPS), 1).astype(F32)
    gmax = jnp.max(gl, axis=-1, keepdims=True)
    g_sel = jnp.min(jnp.where(gl == gmax, lane_g, big), axis=-1, keepdims=True)
    g_prob = 1.0 / jnp.sum(jnp.exp(gl - gmax), axis=-1, keepdims=True)
    if left_out is not None:
        g_sel = jnp.where(left_out, -1.0, g_sel)

    lane_i = lax.broadcasted_iota(jnp.int32, (tm, N_EXPERTS), 1)
    lane_e = lane_i.astype(F32)
    lane_grp = (lane_i // EXP_PER_GROUP).astype(F32)
    v1 = jnp.where(lane_grp == g_sel, el, -jnp.inf)
    t1 = jnp.max(v1, axis=-1, keepdims=True)
    i1 = jnp.min(jnp.where(v1 == t1, lane_e, big), axis=-1, keepdims=True)
    v2 = jnp.where(lane_e == i1, -jnp.inf, v1)
    t2 = jnp.max(v2, axis=-1, keepdims=True)
    i2 = jnp.min(jnp.where(v2 == t2, lane_e, big), axis=-1, keepdims=True)
    e21 = jnp.exp(t2 - t1)
    w1 = 1.0 / (1.0 + e21)
    w2 = e21 * w1
    comb = (jnp.where(lane_e == i1, w1, 0.0) + jnp.where(lane_e == i2, w2, 0.0)) * g_prob
    tail = jnp.zeros((tm, R_PAD - 3 * N_EXPERTS), BF16)
    h2_o[0, rows, d:] = jnp.concatenate(list(_split3(comb)) + [tail], axis=1)

    lane_p = lax.broadcasted_iota(jnp.int32, (tm, R_PAD), 1)
    onehot = jnp.where(lane_p.astype(F32) == g_sel, 1.0, 0.0)
    ri = lax.broadcasted_iota(jnp.int32, (tm, tm), 0)
    ci = lax.broadcasted_iota(jnp.int32, (tm, tm), 1)
    before = jnp.where(ri > ci, 1.0, 0.0).astype(BF16)
    rank = jnp.sum(_dot(before, onehot.astype(BF16)) * onehot, axis=-1, keepdims=True)
    cnt_o[0, s] = jnp.broadcast_to(jnp.sum(onehot, axis=0, keepdims=True), (8, R_PAD))
    fields = jnp.where(lane_p == 0, g_sel, jnp.where(lane_p == 1, rank, 0.0))
    idx_o[0, rows, :] = fields[:, :8].astype(jnp.int32)
    idxt_o[0, :, rows] = fields.T[:8, :].astype(jnp.int32)


def _experts_kernel(st_ref, h2_ref, idx_ref, idxt_ref, w1_ref, w3_ref, w2_ref, o_ref, hs_s, ys_s, a_s, hid_s, *, tb):
    l = h2_ref.shape[1]
    d = o_ref.shape[2]
    nblk = l // tb
    ch = MOE_CHUNK
    bi = pl.program_id(0)
    g = pl.program_id(1)

    @pl.when(g == 0)
    def _():
        ys_s[...] = jnp.zeros_like(ys_s)

    def group_base(gg):
        return (bi * N_GROUPS + gg) * (nblk + 1)

    def group_offset(upto):
        off = 0
        for gg in range(N_GROUPS - 1):
            padded = ((st_ref[group_base(gg) + nblk] + ch - 1) // ch) * ch
            off = off + jnp.where(gg < upto, padded, 0)
        return off

    base = group_base(g)
    cnt = st_ref[base + nblk]
    goff = group_offset(g)

    def chunk(lo, ch):
        sub_iota = lax.broadcasted_iota(jnp.int32, (ch, tb), 0)
        lane_e = lax.broadcasted_iota(jnp.int32, (ch, N_EXPERTS), 1)
        hs_s[:ch] = jnp.zeros((ch, hs_s.shape[1]), F32)
        for k in range(nblk):
            s_k = st_ref[base + k]
            e_k = st_ref[base + k + 1]
            rows = slice(k * tb, (k + 1) * tb)

            @pl.when(jnp.logical_and(s_k < lo + ch, e_k > lo))
            def _(s_k=s_k, rows=rows):
                it = idxt_ref[0, :, rows]
                pos = jnp.where(it[0:1] == g, it[1:2] + (s_k - lo), -1)
                p = jnp.where(sub_iota == pos, 1.0, 0.0).astype(BF16)
                hs_s[:ch] += _dot(p, h2_ref[0, rows, :])

        hsb = hs_s[:ch, :d].astype(BF16)
        cs = (hs_s[:ch, d:d + N_EXPERTS] + hs_s[:ch, d + N_EXPERTS:d + 2 * N_EXPERTS]
              + hs_s[:ch, d + 2 * N_EXPERTS:d + 3 * N_EXPERTS])
        for e in range(EXP_PER_GROUP):
            a_s[2 * e, :ch] = _dot(hsb, w1_ref[e])
            a_s[2 * e + 1, :ch] = _dot(hsb, w3_ref[e])
        for e in range(EXP_PER_GROUP):
            ce = jnp.sum(jnp.where(lane_e == g * EXP_PER_GROUP + e, cs, 0.0), axis=-1, keepdims=True)
            hid_s[e, :ch] = (_silu(a_s[2 * e, :ch]) * a_s[2 * e + 1, :ch] * ce).astype(BF16)
        y = jnp.zeros((ch, d), F32)
        for e in range(EXP_PER_GROUP):
            y = y + _dot(hid_s[e, :ch], w2_ref[e])
        ys_s[pl.ds(pl.multiple_of(goff + lo, ch), ch), :] = y.astype(BF16)

    half = ch // 2
    nfull = cnt // ch
    rem = cnt - nfull * ch
    nloop = nfull + jnp.where(rem > half, 1, 0)

    def chunk_body(c, carry):
        chunk(c * ch, ch)
        return carry

    lax.fori_loop(0, nloop, chunk_body, 0)

    @pl.when(jnp.logical_and(rem > 0, rem <= half))
    def _():
        chunk(nfull * ch, half)

    @pl.when(g == N_GROUPS - 1)
    def _():
        lane_w = lax.broadcasted_iota(jnp.int32, (tb, ch), 1)
        for k in range(nblk):
            rows = slice(k * tb, (k + 1) * tb)
            ic = idx_ref[0, rows, :]
            acc = jnp.zeros((tb, d), F32)
            spill = []
            for gg in range(N_GROUPS):
                s_k = st_ref[group_base(gg) + k]
                e_k = st_ref[group_base(gg) + k + 1]
                win = (s_k // half) * half
                pos = jnp.where(ic[:, 0:1] == gg, ic[:, 1:2] + (s_k - win), -1)
                start = pl.multiple_of(group_offset(gg) + win, half)
                q = jnp.where(lane_w == pos, 1.0, 0.0).astype(BF16)
                acc = acc + _dot(q, ys_s[pl.ds(start, ch), :])
                spill.append((e_k - win > ch, pos, start))
            o_ref[0, rows, :] = acc.astype(BF16)
            for more, pos, start in spill:
                @pl.when(more)
                def _(pos=pos, start=start):
                    q = jnp.where(lane_w == pos - ch, 1.0, 0.0).astype(BF16)
                    extra = _dot(q, ys_s[pl.ds(start + ch, ch), :])
                    o_ref[0, rows, :] = (o_ref[0, rows, :].astype(F32) + extra).astype(BF16)


def _experts_call(starts, h2, idx, idxt, w1, w3, w2, tb):
    b, l, de = h2.shape
    d = de - R_PAD
    whole = lambda width: pl.BlockSpec((1, l, width), lambda bi, g, st: (bi, 0, 0))
    grid_spec = pltpu.PrefetchScalarGridSpec(
        num_scalar_prefetch=1, grid=(b, N_GROUPS),
        in_specs=[pl.BlockSpec((1, l, de), lambda bi, g, st: (bi, 0, 0), pipeline_mode=pl.Buffered(1)),
                  whole(8), pl.BlockSpec((1, 8, l), lambda bi, g, st: (bi, 0, 0)),
                  pl.BlockSpec((EXP_PER_GROUP, d, D_EXPERT), lambda bi, g, st: (g, 0, 0)),
                  pl.BlockSpec((EXP_PER_GROUP, d, D_EXPERT), lambda bi, g, st: (g, 0, 0)),
                  pl.BlockSpec((EXP_PER_GROUP, D_EXPERT, d), lambda bi, g, st: (g, 0, 0))],
        out_specs=whole(d),
        scratch_shapes=[pltpu.VMEM((MOE_CHUNK, de), F32),
                        pltpu.VMEM((l + (N_GROUPS + 1) * MOE_CHUNK, d), BF16),
                        pltpu.VMEM((2 * EXP_PER_GROUP, MOE_CHUNK, D_EXPERT), F32),
                        pltpu.VMEM((EXP_PER_GROUP, MOE_CHUNK, D_EXPERT), BF16)])
    return pl.pallas_call(
        functools.partial(_experts_kernel, tb=tb), grid_spec=grid_spec,
        out_shape=jax.ShapeDtypeStruct((b, l, d), BF16),
        compiler_params=_params("parallel", "arbitrary"), name="experts",
    )(starts, h2, idx, idxt, w1, w3, w2)


def _group_starts(cnt):
    c = cnt[:, :, 0, :N_GROUPS].astype(jnp.int32)
    s = jnp.cumsum(c, axis=1)
    s = jnp.concatenate([jnp.zeros_like(s[:, :1]), s], axis=1)
    return jnp.transpose(s, (0, 2, 1)).reshape(-1)


def _final_kernel(x_ref, mod_ref, f_ref, g_ref, o_ref):
    o_ref[0] = _rms(x_ref[0] + mod_ref[0][5:6] * f_ref[0].astype(F32), g_ref[...])


def _final_call(xs, mod, f, g, lc, tm):
    b, l, d = xs.shape
    off = lc // tm
    lat = pl.BlockSpec((1, tm, d), lambda bi, j: (bi, j + off, 0))
    return pl.pallas_call(
        _final_kernel, grid=(b, (l - lc) // tm),
        in_specs=[lat, pl.BlockSpec((1, 6, d), lambda bi, j: (bi, 0, 0)), lat, _const_spec(g.shape)],
        out_specs=pl.BlockSpec((1, tm, d), lambda bi, j: (bi, j, 0)),
        out_shape=jax.ShapeDtypeStruct((b, l - lc, d), F32),
        compiler_params=_params("parallel", "parallel"), name="final_norm",
    )(xs, mod, f, g)


def _rope_tables(t_len, lc):
    half = A_ROPE // 2
    rows = t_len // GRID_W
    r = jnp.repeat(jnp.arange(rows, dtype=F32), GRID_W)
    col = jnp.tile(jnp.arange(GRID_W, dtype=F32), rows)
    inv = ROPE_THETA ** (-jnp.arange(0, half, 2, dtype=F32) / half)
    ang = jnp.concatenate([r[:, None] * inv, col[:, None] * inv], axis=-1)
    cos = jnp.concatenate([jnp.ones((lc, half), F32), jnp.cos(ang)], axis=0)
    sin = jnp.concatenate([jnp.zeros((lc, half), F32), jnp.sin(ang)], axis=0)
    l = lc + t_len
    ones = jnp.ones((l, A_NOPE), F32)
    zeros = jnp.zeros((l, A_NOPE), F32)
    tail1 = jnp.ones((l, A_PAD - A_NOPE - A_ROPE), F32)
    tail0 = jnp.zeros((l, A_PAD - A_NOPE - A_ROPE), F32)
    zh = jnp.zeros((l, half), F32)
    cos_t = jnp.concatenate([ones, cos, cos, tail1], axis=-1)
    sina_t = jnp.concatenate([zeros, zh, sin, tail0], axis=-1)
    sinb_t = jnp.concatenate([zeros, -sin, zh, tail0], axis=-1)
    return cos_t, sina_t, sinb_t


def _layer_weights(l, w_in, m_gate_b, a_qnorm, a_wuq, a_kvnorm, a_wukv, g_ws, g_bs, g_vnorm,
                   w_pa, w_pb, w_pc, w_out):
    d = w_in.shape[1]
    wi = w_in[l]
    o = 0

    def take(n):
        nonlocal o
        s = wi[:, o:o + n]
        o += n
        return s

    mq, mk, mv, mo, mg = take(M_WIDTH), take(M_WIDTH), take(M_WIDTH), take(M_WIDTH), take(4 * M_HEADS)
    aq, akv, akr = take(A_QRANK), take(A_KVRANK), take(A_ROPE)
    gu, gv = take(G_WIDTH), take(G_WIDTH)
    br = take(3 * d)
    nh = M_HEADS
    gb = m_gate_b[l]
    mgo = jnp.concatenate([mg[:, :nh], mg[:, 2 * nh:3 * nh], mg[:, nh:2 * nh], mg[:, 3 * nh:]], axis=1)
    gbo = jnp.concatenate([gb[:nh], gb[2 * nh:3 * nh], gb[nh:2 * nh], gb[3 * nh:]])
    akr_pad = jnp.concatenate([jnp.zeros((d, A_NOPE), F32), akr,
                               jnp.zeros((d, A_PAD - A_NOPE - A_ROPE), F32)], axis=1)
    wuq = a_wuq[l].reshape(A_QRANK, A_HEADS, A_NOPE + A_ROPE)
    wuq = jnp.pad(wuq, ((0, 0), (0, 0), (0, A_PAD - A_NOPE - A_ROPE))).reshape(A_QRANK, A_HEADS * A_PAD)
    wukv = a_wukv[l].reshape(A_KVRANK, A_HEADS, A_NOPE + A_VDIM)
    wuk = jnp.pad(wukv[:, :, :A_NOPE], ((0, 0), (0, 0), (0, A_PAD - A_NOPE))).reshape(A_KVRANK, A_HEADS * A_PAD)
    wuv = jnp.pad(wukv[:, :, A_NOPE:], ((0, 0), (0, 0), (0, A_PAD - A_VDIM))).reshape(A_KVRANK, A_HEADS * A_PAD)
    vone = jnp.tile(jnp.concatenate([jnp.zeros((A_VDIM,), F32), jnp.ones((A_PAD - A_VDIM,), F32)]),
                    A_HEADS).reshape(1, A_HEADS * A_PAD)
    gbs = jnp.repeat(g_bs[l].T, G_DG, axis=1)
    return dict(
        wqk=jnp.concatenate([mq, mk], 1).astype(BF16), wvo=jnp.concatenate([mv, mo], 1).astype(BF16),
        wgt=jnp.pad(mgo, ((0, 0), (0, G_PAD - 4 * nh))).astype(BF16),
        gbt=jnp.pad(gbo, (0, G_PAD - 4 * nh)).reshape(1, G_PAD),
        wa=jnp.concatenate([aq, akv, akr_pad], 1).astype(BF16),
        wg=jnp.concatenate([gu, gv], 1).astype(BF16), wbr=br.astype(BF16),
        aqn=a_qnorm[l].reshape(1, -1), akvn=a_kvnorm[l].reshape(1, -1),
        wuq=wuq.astype(BF16), wuk=wuk.astype(BF16), wuv=wuv.astype(BF16), vone=vone,
        gvn=g_vnorm[l].reshape(1, -1), gws=g_ws[l].astype(BF16), gbs=gbs,
        wpa=w_pa[l].astype(BF16), wpb=w_pb[l].astype(BF16), wpc=w_pc[l].astype(BF16),
        wout=w_out[l].astype(BF16))


def _router_weights(r_group, r_group_b, r_expert, r_expert_b):
    d = r_group.shape[0]
    pad = R_SEG - N_EXPERTS - N_GROUPS
    r = jnp.concatenate([r_expert, r_group, jnp.zeros((d, pad), F32)], axis=1)
    r3 = jnp.concatenate(list(_split3(r)) + [jnp.zeros((d, R_PAD - 3 * R_SEG), BF16)], axis=1)
    rb = jnp.concatenate([r_expert_b, r_group_b, jnp.zeros((R_PAD - N_EXPERTS - N_GROUPS,), F32)])
    return r3, rb.reshape(1, R_PAD)


def _tile(n, lc, candidates):
    for t in candidates:
        if n % t == 0 and lc % t == 0:
            return t
    raise ValueError("sequence lengths must be multiples of 128")


def kernel(x, c, ctx, c_ctx, w_ada, b_ada, norm1, norm2, final_norm, w_in, m_conv, m_gate_b, m_norm, a_qnorm, a_wuq, a_kvnorm, a_wukv, g_ws, g_bs, g_vnorm, w_pa, w_pb, w_pc, w_out, r_group, r_group_b, r_expert, r_expert_b, e_w1, e_w3, e_w2):
    b, t_len, d = x.shape
    lc = ctx.shape[1]
    l = lc + t_len
    depth = w_in.shape[0]
    tm = _tile(l, lc, (256, 128))

    xs = jnp.concatenate([ctx, x], axis=1)
    cv = jnp.concatenate([c, c_ctx[None, :]], axis=0)
    mod_all = _ada_call(cv, w_ada, b_ada).reshape(depth, b + 1, 6, d)
    tabs = _rope_tables(t_len, lc)

    f = None
    mod_prev = None
    for li in range(depth):
        last = li == depth - 1
        mod = mod_all[li]
        w = _layer_weights(li, w_in, m_gate_b, a_qnorm, a_wuq, a_kvnorm, a_wukv, g_ws, g_bs, g_vnorm,
                           w_pa, w_pb, w_pc, w_out)
        outs = _inproj_call(xs, mod, norm1[li].reshape(1, d), w, tabs, lc, tm, f, mod_prev)
        qk, vo, gi, gf, gr, q, k, v, yc, br = outs[:10]
        if f is not None:
            xs = outs[10]
        ya = _mlstm_call(qk, vo, gi, gf, gr, m_conv[li], m_norm[li].reshape(1, -1), lc)
        yb = _attn_call(q, k, v, lc, tm, not last)
        r3, rb = _router_weights(r_group[li], r_group_b[li], r_expert[li], r_expert_b[li])
        xs, h2, idx, idxt, cnt = _merge_call(xs, mod, ya, yb, yc, br, w, norm2[li].reshape(1, d),
                                             r3, rb, lc, tm, last)
        f = _experts_call(_group_starts(cnt), h2, idx, idxt, e_w1[li].astype(BF16),
                          e_w3[li].astype(BF16), e_w2[li].astype(BF16), tm)
        mod_prev = mod
    return _final_call(xs, mod_prev, f, final_norm.reshape(1, d), lc, tm)
```

```python
import functools

import jax
import jax.numpy as jnp
from jax import lax
from jax.experimental import pallas as pl
from jax.experimental.pallas import tpu as pltpu

F32 = jnp.float32
BF16 = jnp.bfloat16

EPS = 1e-6
GRID_W = 64
ROPE_THETA = 10000.0

M_HEADS = 4
M_DH = 128
M_WIDTH = M_HEADS * M_DH
M_CHUNK = 128
G_PAD = 128

A_HEADS = 8
A_NOPE = 64
A_ROPE = 32
A_VDIM = 64
A_QRANK = 384
A_KVRANK = 256
A_WIDTH = A_HEADS * A_VDIM
A_PAD = 128
A_HPS = 4
ATT_SCALE = (A_NOPE + A_ROPE) ** -0.5
LOG2E = 1.4426950408889634

G_GROUPS = 4
G_CHUNK = 128
G_WIDTH = 512
G_DG = G_WIDTH // G_GROUPS

N_GROUPS = 4
EXP_PER_GROUP = 4
N_EXPERTS = N_GROUPS * EXP_PER_GROUP
D_EXPERT = 512
R_PAD = 128
R_SEG = 32
MOE_CHUNK = 256

VMEM_LIMIT = 56 * 1024 * 1024


def _dot(a, b):
    return jnp.dot(a, b, preferred_element_type=F32)


def _dot_nt(a, b):
    return lax.dot_general(a, b, (((1,), (1,)), ((), ())), preferred_element_type=F32)


def _dot_tn(a, b):
    return lax.dot_general(a, b, (((0,), (0,)), ((), ())), preferred_element_type=F32)


def _split3(x):
    hi = x.astype(BF16)
    r = x - hi.astype(F32)
    mid = r.astype(BF16)
    lo = (r - mid.astype(F32)).astype(BF16)
    return hi, mid, lo


def _sigmoid(x):
    return 1.0 / (1.0 + jnp.exp(-x))


def _silu(x):
    return x * _sigmoid(x)


def _log_sigmoid(x):
    return jnp.minimum(x, 0.0) - jnp.log1p(jnp.exp(-jnp.abs(x)))


def _gelu(x):
    return 0.5 * x * (1.0 + lax.erf(x * (2.0 ** -0.5)))


def _rms(x, g):
    return x * lax.rsqrt(jnp.mean(x * x, axis=-1, keepdims=True) + EPS) * g


def _params(*sem):
    return pltpu.CompilerParams(dimension_semantics=sem, vmem_limit_bytes=VMEM_LIMIT)


def _const_spec(shape):
    nd = len(shape)
    return pl.BlockSpec(shape, lambda *_: (0,) * nd, pipeline_mode=pl.Buffered(1))


def _ada_kernel(cv_ref, w_ref, b_ref, o_ref):
    s = _silu(cv_ref[...])
    o_ref[0] = _dot(s.astype(BF16), w_ref[0].astype(BF16)) + b_ref[0]


def _ada_call(cv, w_ada, b_ada):
    depth, d, n6 = w_ada.shape
    rows = cv.shape[0]
    tn = n6 // 4
    return pl.pallas_call(
        _ada_kernel,
        grid=(depth, n6 // tn),
        in_specs=[pl.BlockSpec((rows, d), lambda l, j: (0, 0)),
                  pl.BlockSpec((1, d, tn), lambda l, j: (l, 0, j)),
                  pl.BlockSpec((1, 1, tn), lambda l, j: (l, 0, j))],
        out_specs=pl.BlockSpec((1, rows, tn), lambda l, j: (l, 0, j)),
        out_shape=jax.ShapeDtypeStruct((depth, rows, n6), F32),
        compiler_params=_params("parallel", "parallel"),
        name="ada",
    )(cv, w_ada, b_ada.reshape(depth, 1, n6))


def _inproj_kernel(*refs, has_f):
    za_s, zg_s, zbr_s = refs[-3:]
    refs = refs[:-3]
    if has_f:
        f_ref, modp_ref, x_o = refs[0], refs[1], refs[-1]
        refs = refs[2:-1]
    (x_ref, mod_ref, n1_ref, wqk_ref, wvo_ref, wgt_ref, gbt_ref,
     wa_ref, wg_ref, wbr_ref, aqn_ref, akvn_ref, wuq_ref, wuk_ref, wuv_ref, vone_ref,
     cos_ref, sina_ref, sinb_ref, gvn_ref, gws_ref, gbs_ref,
     qk_o, vo_o, gi_o, gf_o, gr_o, q_o, k_o, v_o, yc_o, br_o) = refs
    tm = x_ref.shape[1]
    mod = mod_ref[0]
    x = x_ref[0]
    if has_f:
        x = x + modp_ref[0][5:6] * f_ref[0].astype(F32)
        x_o[0] = x
    h = _rms(x, n1_ref[...]) * (1.0 + mod[1:2]) + mod[0:1]
    hb = h.astype(BF16)

    qk_o[0] = _dot(hb, wqk_ref[...])
    vo_o[0] = _dot(hb, wvo_ref[...]).astype(BF16)
    ng = gi_o.shape[2]
    gates = _dot(hb, wgt_ref[...]) + gbt_ref[...]
    gi_o[0] = gates[:, :ng]
    gf_o[0] = pltpu.roll(gates, gates.shape[1] - ng, 1)[:, :ng]
    gr_o[0] = gates.T[:2 * ng, :]

    za_s[...] = _dot(hb, wa_ref[...])
    zg_s[...] = _dot(hb, wg_ref[...])
    zbr_s[...] = _dot(hb, wbr_ref[...])

    aqn = _rms(za_s[:, :A_QRANK], aqn_ref[...]).astype(BF16)
    akvn = _rms(za_s[:, A_QRANK:A_QRANK + A_KVRANK], akvn_ref[...]).astype(BF16)
    cos = cos_ref[...]
    sina = sina_ref[...]
    sinb = sinb_ref[...]
    half = A_ROPE // 2

    def rope(t):
        return t * cos + pltpu.roll(t, half, 1) * sina + pltpu.roll(t, A_PAD - half, 1) * sinb

    kr = rope(za_s[:, A_QRANK + A_KVRANK:])
    qp = _dot(aqn, wuq_ref[...])
    kp = _dot(akvn, wuk_ref[...])
    for hh in range(A_HEADS):
        sl = slice(hh * A_PAD, (hh + 1) * A_PAD)
        q_o[0, :, sl] = (rope(qp[:, sl]) * (ATT_SCALE * LOG2E)).astype(BF16)
        k_o[0, :, sl] = (kp[:, sl] + kr).astype(BF16)
    v_o[0] = (_dot(akvn, wuv_ref[...]) + vone_ref[...]).astype(BF16)

    gu = _gelu(zg_s[:, :G_WIDTH])
    gv = _gelu(zg_s[:, G_WIDTH:])
    gvn = gvn_ref[...]
    bias = gbs_ref[...]
    for g in range(G_GROUPS):
        sl = slice(g * G_DG, (g + 1) * G_DG)
        xn = _rms(gv[:, sl], gvn[:, sl]).astype(BF16)
        ws = gws_ref[g]
        for ci in range(tm // G_CHUNK):
            r = slice(ci * G_CHUNK, (ci + 1) * G_CHUNK)
            sg = _dot(ws, xn[r]) + bias[:, sl]
            yc_o[0, r, sl] = (gu[r, sl] * sg).astype(BF16)

    br_o[0] = _sigmoid(zbr_s[...]).astype(BF16)


def _inproj_call(xs, mod, n1, w, tabs, lc, tm, f=None, mod_prev=None):
    b, l, d = xs.shape
    nct = lc // tm
    has_f = f is not None
    tok = lambda width: pl.BlockSpec((1, tm, width), lambda bi, j: (bi, j, 0))
    modspec = pl.BlockSpec((1, 6, d), lambda bi, j: (jnp.where(j < nct, b, bi), 0, 0))
    tab = pl.BlockSpec((tm, A_PAD), lambda bi, j: (j, 0))
    consts = [n1, w["wqk"], w["wvo"], w["wgt"], w["gbt"], w["wa"], w["wg"], w["wbr"],
              w["aqn"], w["akvn"], w["wuq"], w["wuk"], w["wuv"], w["vone"]]
    consts2 = [w["gvn"], w["gws"], w["gbs"]]
    in_specs = ([tok(d), modspec] + [_const_spec(a.shape) for a in consts] + [tab, tab, tab]
                + [_const_spec(a.shape) for a in consts2])
    args = [xs, mod, *consts, *tabs, *consts2]
    ng = 4 * M_HEADS
    out_shape = [jax.ShapeDtypeStruct((b, l, 2 * M_WIDTH), F32),
                 jax.ShapeDtypeStruct((b, l, 2 * M_WIDTH), BF16),
                 jax.ShapeDtypeStruct((b, l, ng // 2), F32),
                 jax.ShapeDtypeStruct((b, l, ng // 2), F32),
                 jax.ShapeDtypeStruct((b, ng, l), F32),
                 jax.ShapeDtypeStruct((b, l, A_HEADS * A_PAD), BF16),
                 jax.ShapeDtypeStruct((b, l, A_HEADS * A_PAD), BF16),
                 jax.ShapeDtypeStruct((b, l, A_HEADS * A_PAD), BF16),
                 jax.ShapeDtypeStruct((b, l, G_WIDTH), BF16),
                 jax.ShapeDtypeStruct((b, l, 3 * d), BF16)]
    out_specs = [tok(2 * M_WIDTH), tok(2 * M_WIDTH), tok(ng // 2), tok(ng // 2),
                 pl.BlockSpec((1, ng, tm), lambda bi, j: (bi, 0, j)),
                 tok(A_HEADS * A_PAD), tok(A_HEADS * A_PAD), tok(A_HEADS * A_PAD), tok(G_WIDTH), tok(3 * d)]
    if has_f:
        in_specs = [tok(d), modspec] + in_specs
        args = [f, mod_prev] + args
        out_shape.append(jax.ShapeDtypeStruct((b, l, d), F32))
        out_specs.append(tok(d))
    return pl.pallas_call(
        functools.partial(_inproj_kernel, has_f=has_f), grid=(b, l // tm), in_specs=in_specs,
        out_specs=out_specs, out_shape=out_shape,
        scratch_shapes=[pltpu.VMEM((tm, w["wa"].shape[1]), F32), pltpu.VMEM((tm, 2 * G_WIDTH), F32),
                        pltpu.VMEM((tm, 3 * d), F32)],
        compiler_params=_params("parallel", "parallel"), name="inproj",
    )(*args)


def _scan(x, op, fill, axis, reverse):
    n = x.shape[axis]
    idx = lax.broadcasted_iota(jnp.int32, x.shape, axis)
    k = 1
    while k < n:
        if reverse:
            x = op(x, jnp.where(idx >= n - k, fill, pltpu.roll(x, n - k, axis)))
        else:
            x = op(x, jnp.where(idx < k, fill, pltpu.roll(x, k, axis)))
        k *= 2
    return x


def _mlstm_kernel(qk_ref, vo_ref, gi_ref, gf_ref, gr_ref, conv_ref, mnorm_ref, ya_ref,
                  q_s, kt_s, h_s, bc_s, ml_s, dl_s, br_s, cn_s, m_s, s_s, p_s, qcn_s, u_s, *, lc):
    l = qk_ref.shape[1]
    ch = M_CHUNK
    nc = l // ch
    ncc = lc // ch
    nh = M_HEADS
    ng = 2 * nh
    w = conv_ref[...]
    row = lax.broadcasted_iota(jnp.int32, (ch, 1), 0)

    def conv_chunk(j):
        r0 = pl.multiple_of(j * ch, ch)
        cur = qk_ref[0, pl.ds(r0, ch), :]
        prev8 = qk_ref[0, pl.ds(pl.multiple_of(jnp.maximum(r0 - 8, 0), 8), 8), :]
        next8 = qk_ref[0, pl.ds(pl.multiple_of(jnp.minimum(r0 + ch, l - 8), 8), 8), :]
        seg_start = jnp.logical_or(j == 0, j == ncc)
        seg_end = jnp.logical_or(j == ncc - 1, j == nc - 1)
        pe = jnp.where(seg_start, 0.0, prev8[7:8, :])
        ne = jnp.where(seg_end, 0.0, next8[0:1, :])
        xp = jnp.where(row == 0, pe, pltpu.roll(cur, 1, 0))
        xn = jnp.where(row == ch - 1, ne, pltpu.roll(cur, ch - 1, 0))
        y = _silu(xp * w[0:1] + cur * w[1:2] + xn * w[2:3])
        q_s[pl.ds(r0, ch), :] = (y[:, :M_WIDTH] * (M_DH ** -0.5)).astype(BF16)
        kt_s[:, pl.ds(r0, ch)] = y[:, M_WIDTH:].T.astype(BF16)

    ri = lax.broadcasted_iota(jnp.int32, (ch, ch), 0)
    ci = lax.broadcasted_iota(jnp.int32, (ch, ch), 1)
    lower = ri >= ci
    upper = ri <= ci
    ones_blk = jnp.ones((ch, M_DH), BF16)
    fwd_c = lax.broadcasted_iota(jnp.int32, (ch, ng), 1) < nh
    fwd_r = lax.broadcasted_iota(jnp.int32, (ng, ch), 0) < nh
    lane_c = lax.broadcasted_iota(jnp.int32, (ch, ng), 1)

    def local_chunk(j):
        r0 = pl.multiple_of(j * ch, ch)
        rows = pl.ds(r0, ch)
        lfc = _log_sigmoid(gf_ref[0, rows, :])
        gr = gr_ref[0, :, rows]
        lfr = _log_sigmoid(gr[ng:])
        pre_c = _scan(lfc, jnp.add, 0.0, 0, False)
        pre_r = _scan(lfr, jnp.add, 0.0, 1, False)
        b_c = jnp.where(fwd_c, pre_c, jnp.sum(lfc, axis=0, keepdims=True) + lfc - pre_c)
        b_r = jnp.where(fwd_r, pre_r, jnp.sum(lfr, axis=1, keepdims=True) + lfr - pre_r)
        g_c = gi_ref[0, rows, :] - b_c
        g_r = gr[:ng] - b_r
        cg_c = jnp.where(fwd_c, _scan(g_c, jnp.maximum, -jnp.inf, 0, False),
                         _scan(g_c, jnp.maximum, -jnp.inf, 0, True))
        bc_s[rows, :] = b_c
        br_s[:, rows] = b_r
        ml_s[rows, :] = b_c + cg_c
        dl = jnp.zeros((ch, ng), F32)
        for hh in range(nh):
            sl = slice(hh * M_DH, (hh + 1) * M_DH)
            s_s[hh] = _dot(q_s[rows, sl], kt_s[sl, rows])
        for hh in range(nh):
            s = s_s[hh]
            for d in range(2):
                jj = d * nh + hh
                wgt = jnp.exp(jnp.where(upper if d else lower, g_r[jj:jj + 1, :] - cg_c[:, jj:jj + 1], -jnp.inf))
                p_s[hh, d * ch:(d + 1) * ch] = (s * wgt).astype(BF16)
        for hh in range(nh):
            sl = slice(hh * M_DH, (hh + 1) * M_DH)
            v1 = jnp.concatenate([vo_ref[0, rows, sl], ones_blk], axis=1)
            nd2 = _dot(p_s[hh], v1)
            for d in range(2):
                jj = d * nh + hh
                nd = nd2[d * ch:(d + 1) * ch]
                h_s[d, rows, sl] = nd[:, :M_DH]
                dl = jnp.where(lane_c == jj, nd[:, M_DH:M_DH + ng], dl)
        dl_s[rows, :] = dl

    def conv_local_body(j, carry):
        conv_chunk(j + 1)
        local_chunk(j)
        return carry

    conv_chunk(jnp.int32(0))
    lax.fori_loop(0, nc - 1, conv_local_body, 0)
    local_chunk(jnp.int32(nc - 1))

    cn_s[...] = jnp.zeros_like(cn_s)
    m_s[...] = jnp.zeros_like(m_s)
    lane_r = lax.broadcasted_iota(jnp.int32, (1, ng), 1)

    def scan_issue(r0, d):
        rows = pl.ds(r0, ch)
        gr = gr_ref[0, :, rows]
        br = br_s[:, rows]
        tot = jnp.sum(_log_sigmoid(gr[ng:]), axis=1, keepdims=True)
        scal = []
        for hh in range(nh):
            fi = d * nh + hh
            sl = slice(hh * M_DH, (hh + 1) * M_DH)
            qcn_s[fi] = _dot(q_s[rows, sl], cn_s[fi].astype(BF16))
            m_old = m_s[fi][:, 0:1]
            b_e = tot[fi:fi + 1, :]
            d_end = b_e - br[fi:fi + 1, :] + gr[fi:fi + 1, :]
            m_end = jnp.max(d_end, axis=-1, keepdims=True)
            m_new = jnp.maximum(b_e + m_old, m_end)
            ktw = (kt_s[sl, rows].astype(F32) * jnp.exp(d_end - m_end)).astype(BF16)
            v1 = jnp.concatenate([vo_ref[0, rows, sl], ones_blk], axis=1)
            u_s[fi] = _dot(ktw, v1)
            scal.append((m_old, m_new, jnp.exp(b_e + m_old - m_new), jnp.exp(m_end - m_new)))
        return scal

    def scan_finish(r0, d, scal):
        rows = pl.ds(r0, ch)
        m_row = jnp.zeros((1, ng), F32)
        for hh in range(nh):
            m_row = jnp.where(lane_r == d * nh + hh, scal[hh][0], m_row)

        inter = bc_s[rows, :] + m_row
        ml = ml_s[rows, :]
        mt = jnp.maximum(inter, ml)
        a = jnp.exp(ml - mt)
        wi = jnp.exp(inter - mt)
        qn = jnp.zeros((ch, ng), F32)
        for hh in range(nh):
            fi = d * nh + hh
            qn = jnp.where(lane_c == fi, qcn_s[fi, :, M_DH:M_DH + ng], qn)
        den = a * dl_s[rows, :] + wi * qn
        rinv = 1.0 / jnp.maximum(jnp.abs(den), jnp.exp(-mt))
        c_loc = a * rinv
        c_int = wi * rinv

        for hh in range(nh):
            fi = d * nh + hh
            sl = slice(hh * M_DH, (hh + 1) * M_DH)
            h_s[d, rows, sl] = (c_loc[:, fi:fi + 1] * h_s[d, rows, sl]
                                + c_int[:, fi:fi + 1] * qcn_s[fi, :, :M_DH])
            cn_s[fi] = scal[hh][2] * cn_s[fi] + scal[hh][3] * u_s[fi]
            m_s[fi] = jnp.broadcast_to(scal[hh][1], (1, M_DH))

    def scan_body(s, carry):
        rf = pl.multiple_of(s * ch, ch)
        rb = pl.multiple_of(jnp.where(s < ncc, ncc - 1 - s, nc - 1 - s + ncc) * ch, ch)
        sf = scan_issue(rf, 0)
        sb = scan_issue(rb, 1)
        scan_finish(rf, 0, sf)
        scan_finish(rb, 1, sb)
        return carry

    lax.fori_loop(0, nc, scan_body, 0)

    mnorm = mnorm_ref[...]

    def out_body(j, carry):
        r0 = pl.multiple_of(j * ch, ch)
        hsum = h_s[0, pl.ds(r0, ch), :] + h_s[1, pl.ds(r0, ch), :]
        og = _sigmoid(vo_ref[0, pl.ds(r0, ch), M_WIDTH:].astype(F32))
        for hh in range(M_HEADS):
            sl = slice(hh * M_DH, (hh + 1) * M_DH)
            ya_ref[0, pl.ds(r0, ch), sl] = (_rms(hsum[:, sl], mnorm[:, sl]) * og[:, sl]).astype(BF16)
        return carry

    lax.fori_loop(0, nc, out_body, 0)


def _mlstm_call(qk, vo, gi, gf, gr, conv, mnorm, lc):
    b, l, _ = qk.shape
    ng = 2 * M_HEADS
    return pl.pallas_call(
        functools.partial(_mlstm_kernel, lc=lc),
        grid=(b,),
        in_specs=[pl.BlockSpec((1, l, 2 * M_WIDTH), lambda bi: (bi, 0, 0), pipeline_mode=pl.Buffered(1)),
                  pl.BlockSpec((1, l, 2 * M_WIDTH), lambda bi: (bi, 0, 0)),
                  pl.BlockSpec((1, l, ng), lambda bi: (bi, 0, 0)),
                  pl.BlockSpec((1, l, ng), lambda bi: (bi, 0, 0)),
                  pl.BlockSpec((1, 2 * ng, l), lambda bi: (bi, 0, 0)),
                  _const_spec(conv.shape), _const_spec(mnorm.shape)],
        out_specs=pl.BlockSpec((1, l, M_WIDTH), lambda bi: (bi, 0, 0)),
        out_shape=jax.ShapeDtypeStruct((b, l, M_WIDTH), BF16),
        scratch_shapes=[pltpu.VMEM((l, M_WIDTH), BF16),
                        pltpu.VMEM((M_WIDTH, l), BF16),
                        pltpu.VMEM((2, l, M_WIDTH), F32),
                        pltpu.VMEM((l, ng), F32),
                        pltpu.VMEM((l, ng), F32),
                        pltpu.VMEM((l, ng), F32),
                        pltpu.VMEM((ng, l), F32),
                        pltpu.VMEM((2 * M_HEADS, M_DH, 2 * M_DH), F32),
                        pltpu.VMEM((2 * M_HEADS, 1, M_DH), F32),
                        pltpu.VMEM((M_HEADS, M_CHUNK, M_CHUNK), F32),
                        pltpu.VMEM((M_HEADS, 2 * M_CHUNK, M_CHUNK), BF16),
                        pltpu.VMEM((2 * M_HEADS, M_CHUNK, 2 * M_DH), F32),
                        pltpu.VMEM((2 * M_HEADS, M_DH, 2 * M_DH), F32)],
        compiler_params=_params("parallel"), name="mlstm",
    )(qk, vo, gi, gf, gr, conv, mnorm)


def _attn_kernel(q_ref, k_ref, v_ref, o_ref, s_s, p_s, *, lc, ctx_out):
    tq = q_ref.shape[1]
    l = k_ref.shape[1]
    qi = pl.program_id(2)
    nct = lc // tq
    lane = lax.broadcasted_iota(jnp.int32, (tq, 2 * A_VDIM), 1)

    def run(klen):
        outs = []
        for hh in range(A_HPS):
            sl = slice(hh * A_PAD, (hh + 1) * A_PAD)
            s_s[hh, :, :klen] = _dot_nt(q_ref[0, :, sl], k_ref[0, :klen, sl])
        row_max = [jnp.max(s_s[hh, :, :klen], axis=-1, keepdims=True) for hh in range(A_HPS)]
        for hh in range(A_HPS):
            p_s[hh, :, :klen] = jnp.exp2((s_s[hh, :, :klen] - row_max[hh]).astype(BF16))
        for hh in range(A_HPS):
            sl = slice(hh * A_PAD, (hh + 1) * A_PAD)
            nd = _dot(p_s[hh, :, :klen], v_ref[0, :klen, sl])
            outs.append(nd / pltpu.roll(nd, A_VDIM, 1))
        for pp in range(A_HPS // 2):
            o_ref[0, :, pp * A_PAD:(pp + 1) * A_PAD] = jnp.where(
                lane < A_VDIM, outs[2 * pp], pltpu.roll(outs[2 * pp + 1], A_VDIM, 1)).astype(BF16)

    @pl.when(qi >= nct)
    def _():
        run(l)

    @pl.when(qi < nct)
    def _():
        if ctx_out:
            run(lc)
        else:
            o_ref[...] = jnp.zeros_like(o_ref)


def _attn_call(q, k, v, lc, tq, ctx_out):
    b, l, _ = q.shape
    return pl.pallas_call(
        functools.partial(_attn_kernel, lc=lc, ctx_out=ctx_out),
        grid=(b, A_HEADS // A_HPS, l // tq),
        in_specs=[pl.BlockSpec((1, tq, A_HPS * A_PAD), lambda bi, p, qi: (bi, qi, p)),
                  pl.BlockSpec((1, l, A_HPS * A_PAD), lambda bi, p, qi: (bi, 0, p)),
                  pl.BlockSpec((1, l, A_HPS * A_PAD), lambda bi, p, qi: (bi, 0, p))],
        out_specs=pl.BlockSpec((1, tq, A_HPS * A_VDIM), lambda bi, p, qi: (bi, qi, p)),
        out_shape=jax.ShapeDtypeStruct((b, l, A_WIDTH), BF16),
        scratch_shapes=[pltpu.VMEM((A_HPS, tq, l), F32), pltpu.VMEM((A_HPS, tq, l), BF16)],
        compiler_params=_params("parallel", "parallel", "arbitrary"), name="attn",
    )(q, k, v)


def _merge_kernel(x_ref, mod_ref, modc_ref, ya_ref, yb_ref, yc_ref, br_ref, wpa_ref, wpb_ref, wpc_ref, wout_ref,
                  n2_ref, r_ref, rb_ref, o_ref, h2_o, idx_o, idxt_o, cnt_o, y_s, *, tm, nct, skip_ctx):
    d = x_ref.shape[2]
    nsub = x_ref.shape[1] // tm
    modb = mod_ref[0]
    modc = modc_ref[0]
    tiles = [slice(s * tm, (s + 1) * tm) for s in range(nsub)]
    ctx = [pl.program_id(1) * nsub + s < nct for s in range(nsub)]
    mods = [jnp.where(c, modc, modb) for c in ctx]
    for s, rows in enumerate(tiles):
        br = br_ref[0, rows, :]
        y = (br[:, :d].astype(F32) * _dot(ya_ref[0, rows, :], wpa_ref[...])
             + br[:, d:2 * d].astype(F32) * _dot(yb_ref[0, rows, :], wpb_ref[...])
             + br[:, 2 * d:].astype(F32) * _dot(yc_ref[0, rows, :], wpc_ref[...]))
        y_s[s] = y.astype(BF16)
    for s, rows in enumerate(tiles):
        o_ref[0, rows, :] = x_ref[0, rows, :] + mods[s][2:3] * _dot(y_s[s], wout_ref[...])
    for s, rows in enumerate(tiles):
        logits = _route_logits(o_ref[0, rows, :], mods[s], n2_ref, r_ref, rb_ref, h2_o, rows)
        left_out = ctx[s] if skip_ctx else None
        _route_assign(logits, left_out, d, h2_o, idx_o, idxt_o, cnt_o, rows, s)


def _merge_call(xs, mod, ya, yb, yc, br, w, n2, r3, rb, lc, tm, skip_ctx):
    b, l, d = xs.shape
    nct = lc // tm
    nsub = next(n for n in (3, 2, 1) if l % (n * tm) == 0)
    tg = nsub * tm
    tok = lambda width: pl.BlockSpec((1, tg, width), lambda bi, j: (bi, j, 0))
    consts = [w["wpa"], w["wpb"], w["wpc"], w["wout"], n2, r3, rb]
    return pl.pallas_call(
        functools.partial(_merge_kernel, tm=tm, nct=nct, skip_ctx=skip_ctx), grid=(b, l // tg),
        in_specs=[tok(d), pl.BlockSpec((1, 6, d), lambda bi, j: (bi, 0, 0)),
                  pl.BlockSpec((1, 6, d), lambda bi, j: (b, 0, 0)),
                  tok(M_WIDTH), tok(A_WIDTH), tok(G_WIDTH), tok(3 * d)] + [_const_spec(a.shape) for a in consts],
        out_specs=[tok(d), tok(d + R_PAD), tok(8), pl.BlockSpec((1, 8, tg), lambda bi, j: (bi, 0, j)),
                   pl.BlockSpec((1, nsub, 8, R_PAD), lambda bi, j: (bi, j, 0, 0))],
        out_shape=[jax.ShapeDtypeStruct((b, l, d), F32),
                   jax.ShapeDtypeStruct((b, l, d + R_PAD), BF16),
                   jax.ShapeDtypeStruct((b, l, 8), jnp.int32), jax.ShapeDtypeStruct((b, 8, l), jnp.int32),
                   jax.ShapeDtypeStruct((b, l // tm, 8, R_PAD), F32)],
        scratch_shapes=[pltpu.VMEM((nsub, tm, d), BF16)],
        compiler_params=_params("parallel", "parallel"), name="merge",
    )(xs, mod, mod, ya, yb, yc, br, *consts)


def _route_logits(x, mod, n2_ref, r_ref, rb_ref, h2_o, rows):
    d = x.shape[1]
    h2 = _rms(x, n2_ref[...]) * (1.0 + mod[4:5]) + mod[3:4]
    h2_o[0, rows, :d] = h2.astype(BF16)
    r = r_ref[...]
    pp = sum(_dot(piece, r) for piece in _split3(h2))
    return pp + pltpu.roll(pp, R_PAD - R_SEG, 1) + pltpu.roll(pp, R_PAD - 2 * R_SEG, 1) + rb_ref[...]


def _route_assign(logits, left_out, d, h2_o, idx_o, idxt_o, cnt_o, rows, s):
    tm = logits.shape[0]
    el = logits[:, :N_EXPERTS]
    gl = logits[:, N_EXPERTS:N_EXPERTS + N_GROUPS]
    big = 1e9

    lane_g = lax.broadcasted_iota(jnp.int32, (tm, N_GROUPS), 1).astype(F32)
    gmax = jnp.max(gl, axis=-1, keepdims=True)
    g_sel = jnp.min(jnp.where(gl == gmax, lane_g, big), axis=-1, keepdims=True)
    g_prob = 1.0 / jnp.sum(jnp.exp(gl - gmax), axis=-1, keepdims=True)
    if left_out is not None:
        g_sel = jnp.where(left_out, -1.0, g_sel)

    lane_i = lax.broadcasted_iota(jnp.int32, (tm, N_EXPERTS), 1)
    lane_e = lane_i.astype(F32)
    lane_grp = (lane_i // EXP_PER_GROUP).astype(F32)
    v1 = jnp.where(lane_grp == g_sel, el, -jnp.inf)
    t1 = jnp.max(v1, axis=-1, keepdims=True)
    i1 = jnp.min(jnp.where(v1 == t1, lane_e, big), axis=-1, keepdims=True)
    v2 = jnp.where(lane_e == i1, -jnp.inf, v1)
    t2 = jnp.max(v2, axis=-1, keepdims=True)
    i2 = jnp.min(jnp.where(v2 == t2, lane_e, big), axis=-1, keepdims=True)
    e21 = jnp.exp(t2 - t1)
    w1 = 1.0 / (1.0 + e21)
    w2 = e21 * w1
    comb = (jnp.where(lane_e == i1, w1, 0.0) + jnp.where(lane_e == i2, w2, 0.0)) * g_prob
    tail = jnp.zeros((tm, R_PAD - 3 * N_EXPERTS), BF16)
    h2_o[0, rows, d:] = jnp.concatenate(list(_split3(comb)) + [tail], axis=1)

    lane_p = lax.broadcasted_iota(jnp.int32, (tm, R_PAD), 1)
    onehot = jnp.where(lane_p.astype(F32) == g_sel, 1.0, 0.0)
    ri = lax.broadcasted_iota(jnp.int32, (tm, tm), 0)
    ci = lax.broadcasted_iota(jnp.int32, (tm, tm), 1)
    before = jnp.where(ri > ci, 1.0, 0.0).astype(BF16)
    rank = jnp.sum(_dot(before, onehot.astype(BF16)) * onehot, axis=-1, keepdims=True)
    cnt_o[0, s] = jnp.broadcast_to(jnp.sum(onehot, axis=0, keepdims=True), (8, R_PAD))
    fields = jnp.where(lane_p == 0, g_sel, jnp.where(lane_p == 1, rank, 0.0))
    idx_o[0, rows, :] = fields[:, :8].astype(jnp.int32)
    idxt_o[0, :, rows] = fields.T[:8, :].astype(jnp.int32)


def _experts_kernel(st_ref, h2_ref, idx_ref, idxt_ref, w1_ref, w3_ref, w2_ref, o_ref, hs_s, ys_s, a_s, hid_s, *, tb):
    l = h2_ref.shape[1]
    d = o_ref.shape[2]
    nblk = l // tb
    ch = MOE_CHUNK
    bi = pl.program_id(0)
    g = pl.program_id(1)

    @pl.when(g == 0)
    def _():
        ys_s[...] = jnp.zeros_like(ys_s)

    def group_base(gg):
        return (bi * N_GROUPS + gg) * (nblk + 1)

    def group_offset(upto):
        off = 0
        for gg in range(N_GROUPS - 1):
            padded = ((st_ref[group_base(gg) + nblk] + ch - 1) // ch) * ch
            off = off + jnp.where(gg < upto, padded, 0)
        return off

    base = group_base(g)
    cnt = st_ref[base + nblk]
    goff = group_offset(g)

    def chunk(lo, ch):
        sub_iota = lax.broadcasted_iota(jnp.int32, (ch, tb), 0)
        lane_e = lax.broadcasted_iota(jnp.int32, (ch, N_EXPERTS), 1)
        hs_s[:ch] = jnp.zeros((ch, hs_s.shape[1]), F32)
        for k in range(nblk):
            s_k = st_ref[base + k]
            e_k = st_ref[base + k + 1]
            rows = slice(k * tb, (k + 1) * tb)

            @pl.when(jnp.logical_and(s_k < lo + ch, e_k > lo))
            def _(s_k=s_k, rows=rows):
                it = idxt_ref[0, :, rows]
                pos = jnp.where(it[0:1] == g, it[1:2] + (s_k - lo), -1)
                p = jnp.where(sub_iota == pos, 1.0, 0.0).astype(BF16)
                hs_s[:ch] += _dot(p, h2_ref[0, rows, :])

        hsb = hs_s[:ch, :d].astype(BF16)
        cs = (hs_s[:ch, d:d + N_EXPERTS] + hs_s[:ch, d + N_EXPERTS:d + 2 * N_EXPERTS]
              + hs_s[:ch, d + 2 * N_EXPERTS:d + 3 * N_EXPERTS])
        for e in range(EXP_PER_GROUP):
            a_s[2 * e, :ch] = _dot(hsb, w1_ref[e])
            a_s[2 * e + 1, :ch] = _dot(hsb, w3_ref[e])
        for e in range(EXP_PER_GROUP):
            ce = jnp.sum(jnp.where(lane_e == g * EXP_PER_GROUP + e, cs, 0.0), axis=-1, keepdims=True)
            hid_s[e, :ch] = (_silu(a_s[2 * e, :ch]) * a_s[2 * e + 1, :ch] * ce).astype(BF16)
        y = jnp.zeros((ch, d), F32)
        for e in range(EXP_PER_GROUP):
            y = y + _dot(hid_s[e, :ch], w2_ref[e])
        ys_s[pl.ds(pl.multiple_of(goff + lo, ch), ch), :] = y.astype(BF16)

    half = ch // 2
    nfull = cnt // ch
    rem = cnt - nfull * ch
    nloop = nfull + jnp.where(rem > half, 1, 0)

    def chunk_body(c, carry):
        chunk(c * ch, ch)
        return carry

    lax.fori_loop(0, nloop, chunk_body, 0)

    @pl.when(jnp.logical_and(rem > 0, rem <= half))
    def _():
        chunk(nfull * ch, half)

    @pl.when(g == N_GROUPS - 1)
    def _():
        lane_w = lax.broadcasted_iota(jnp.int32, (tb, ch), 1)
        for k in range(nblk):
            rows = slice(k * tb, (k + 1) * tb)
            ic = idx_ref[0, rows, :]
            acc = jnp.zeros((tb, d), F32)
            spill = []
            for gg in range(N_GROUPS):
                s_k = st_ref[group_base(gg) + k]
                e_k = st_ref[group_base(gg) + k + 1]
                win = (s_k // half) * half
                pos = jnp.where(ic[:, 0:1] == gg, ic[:, 1:2] + (s_k - win), -1)
                start = pl.multiple_of(group_offset(gg) + win, half)
                q = jnp.where(lane_w == pos, 1.0, 0.0).astype(BF16)
                acc = acc + _dot(q, ys_s[pl.ds(start, ch), :])
                spill.append((e_k - win > ch, pos, start))
            o_ref[0, rows, :] = acc.astype(BF16)
            for more, pos, start in spill:
                @pl.when(more)
                def _(pos=pos, start=start):
                    q = jnp.where(lane_w == pos - ch, 1.0, 0.0).astype(BF16)
                    extra = _dot(q, ys_s[pl.ds(start + ch, ch), :])
                    o_ref[0, rows, :] = (o_ref[0, rows, :].astype(F32) + extra).astype(BF16)


def _experts_call(starts, h2, idx, idxt, w1, w3, w2, tb):
    b, l, de = h2.shape
    d = de - R_PAD
    whole = lambda width: pl.BlockSpec((1, l, width), lambda bi, g, st: (bi, 0, 0))
    grid_spec = pltpu.PrefetchScalarGridSpec(
        num_scalar_prefetch=1, grid=(b, N_GROUPS),
        in_specs=[pl.BlockSpec((1, l, de), lambda bi, g, st: (bi, 0, 0), pipeline_mode=pl.Buffered(1)),
                  whole(8), pl.BlockSpec((1, 8, l), lambda bi, g, st: (bi, 0, 0)),
                  pl.BlockSpec((EXP_PER_GROUP, d, D_EXPERT), lambda bi, g, st: (g, 0, 0)),
                  pl.BlockSpec((EXP_PER_GROUP, d, D_EXPERT), lambda bi, g, st: (g, 0, 0)),
                  pl.BlockSpec((EXP_PER_GROUP, D_EXPERT, d), lambda bi, g, st: (g, 0, 0))],
        out_specs=whole(d),
        scratch_shapes=[pltpu.VMEM((MOE_CHUNK, de), F32),
                        pltpu.VMEM((l + (N_GROUPS + 1) * MOE_CHUNK, d), BF16),
                        pltpu.VMEM((2 * EXP_PER_GROUP, MOE_CHUNK, D_EXPERT), F32),
                        pltpu.VMEM((EXP_PER_GROUP, MOE_CHUNK, D_EXPERT), BF16)])
    return pl.pallas_call(
        functools.partial(_experts_kernel, tb=tb), grid_spec=grid_spec,
        out_shape=jax.ShapeDtypeStruct((b, l, d), BF16),
        compiler_params=_params("parallel", "arbitrary"), name="experts",
    )(starts, h2, idx, idxt, w1, w3, w2)


def _group_starts(cnt):
    c = cnt[:, :, 0, :N_GROUPS].astype(jnp.int32)
    s = jnp.cumsum(c, axis=1)
    s = jnp.concatenate([jnp.zeros_like(s[:, :1]), s], axis=1)
    return jnp.transpose(s, (0, 2, 1)).reshape(-1)


def _final_kernel(x_ref, mod_ref, f_ref, g_ref, o_ref):
    o_ref[0] = _rms(x_ref[0] + mod_ref[0][5:6] * f_ref[0].astype(F32), g_ref[...])


def _final_call(xs, mod, f, g, lc, tm):
    b, l, d = xs.shape
    off = lc // tm
    lat = pl.BlockSpec((1, tm, d), lambda bi, j: (bi, j + off, 0))
    return pl.pallas_call(
        _final_kernel, grid=(b, (l - lc) // tm),
        in_specs=[lat, pl.BlockSpec((1, 6, d), lambda bi, j: (bi, 0, 0)), lat, _const_spec(g.shape)],
        out_specs=pl.BlockSpec((1, tm, d), lambda bi, j: (bi, j, 0)),
        out_shape=jax.ShapeDtypeStruct((b, l - lc, d), F32),
        compiler_params=_params("parallel", "parallel"), name="final_norm",
    )(xs, mod, f, g)


def _rope_tables(t_len, lc):
    half = A_ROPE // 2
    rows = t_len // GRID_W
    r = jnp.repeat(jnp.arange(rows, dtype=F32), GRID_W)
    col = jnp.tile(jnp.arange(GRID_W, dtype=F32), rows)
    inv = ROPE_THETA ** (-jnp.arange(0, half, 2, dtype=F32) / half)
    ang = jnp.concatenate([r[:, None] * inv, col[:, None] * inv], axis=-1)
    cos = jnp.concatenate([jnp.ones((lc, half), F32), jnp.cos(ang)], axis=0)
    sin = jnp.concatenate([jnp.zeros((lc, half), F32), jnp.sin(ang)], axis=0)
    l = lc + t_len
    ones = jnp.ones((l, A_NOPE), F32)
    zeros = jnp.zeros((l, A_NOPE), F32)
    tail1 = jnp.ones((l, A_PAD - A_NOPE - A_ROPE), F32)
    tail0 = jnp.zeros((l, A_PAD - A_NOPE - A_ROPE), F32)
    zh = jnp.zeros((l, half), F32)
    cos_t = jnp.concatenate([ones, cos, cos, tail1], axis=-1)
    sina_t = jnp.concatenate([zeros, zh, sin, tail0], axis=-1)
    sinb_t = jnp.concatenate([zeros, -sin, zh, tail0], axis=-1)
    return cos_t, sina_t, sinb_t


def _layer_weights(l, w_in, m_gate_b, a_qnorm, a_wuq, a_kvnorm, a_wukv, g_ws, g_bs, g_vnorm,
                   w_pa, w_pb, w_pc, w_out):
    d = w_in.shape[1]
    wi = w_in[l]
    o = 0

    def take(n):
        nonlocal o
        s = wi[:, o:o + n]
        o += n
        return s

    mq, mk, mv, mo, mg = take(M_WIDTH), take(M_WIDTH), take(M_WIDTH), take(M_WIDTH), take(4 * M_HEADS)
    aq, akv, akr = take(A_QRANK), take(A_KVRANK), take(A_ROPE)
    gu, gv = take(G_WIDTH), take(G_WIDTH)
    br = take(3 * d)
    nh = M_HEADS
    gb = m_gate_b[l]
    mgo = jnp.concatenate([mg[:, :nh], mg[:, 2 * nh:3 * nh], mg[:, nh:2 * nh], mg[:, 3 * nh:]], axis=1)
    gbo = jnp.concatenate([gb[:nh], gb[2 * nh:3 * nh], gb[nh:2 * nh], gb[3 * nh:]])
    akr_pad = jnp.concatenate([jnp.zeros((d, A_NOPE), F32), akr,
                               jnp.zeros((d, A_PAD - A_NOPE - A_ROPE), F32)], axis=1)
    wuq = a_wuq[l].reshape(A_QRANK, A_HEADS, A_NOPE + A_ROPE)
    wuq = jnp.pad(wuq, ((0, 0), (0, 0), (0, A_PAD - A_NOPE - A_ROPE))).reshape(A_QRANK, A_HEADS * A_PAD)
    wukv = a_wukv[l].reshape(A_KVRANK, A_HEADS, A_NOPE + A_VDIM)
    wuk = jnp.pad(wukv[:, :, :A_NOPE], ((0, 0), (0, 0), (0, A_PAD - A_NOPE))).reshape(A_KVRANK, A_HEADS * A_PAD)
    wuv = jnp.pad(wukv[:, :, A_NOPE:], ((0, 0), (0, 0), (0, A_PAD - A_VDIM))).reshape(A_KVRANK, A_HEADS * A_PAD)
    vone = jnp.tile(jnp.concatenate([jnp.zeros((A_VDIM,), F32), jnp.ones((A_PAD - A_VDIM,), F32)]),
                    A_HEADS).reshape(1, A_HEADS * A_PAD)
    gbs = jnp.repeat(g_bs[l].T, G_DG, axis=1)
    return dict(
        wqk=jnp.concatenate([mq, mk], 1).astype(BF16), wvo=jnp.concatenate([mv, mo], 1).astype(BF16),
        wgt=jnp.pad(mgo, ((0, 0), (0, G_PAD - 4 * nh))).astype(BF16),
        gbt=jnp.pad(gbo, (0, G_PAD - 4 * nh)).reshape(1, G_PAD),
        wa=jnp.concatenate([aq, akv, akr_pad], 1).astype(BF16),
        wg=jnp.concatenate([gu, gv], 1).astype(BF16), wbr=br.astype(BF16),
        aqn=a_qnorm[l].reshape(1, -1), akvn=a_kvnorm[l].reshape(1, -1),
        wuq=wuq.astype(BF16), wuk=wuk.astype(BF16), wuv=wuv.astype(BF16), vone=vone,
        gvn=g_vnorm[l].reshape(1, -1), gws=g_ws[l].astype(BF16), gbs=gbs,
        wpa=w_pa[l].astype(BF16), wpb=w_pb[l].astype(BF16), wpc=w_pc[l].astype(BF16),
        wout=w_out[l].astype(BF16))


def _router_weights(r_group, r_group_b, r_expert, r_expert_b):
    d = r_group.shape[0]
    pad = R_SEG - N_EXPERTS - N_GROUPS
    r = jnp.concatenate([r_expert, r_group, jnp.zeros((d, pad), F32)], axis=1)
    r3 = jnp.concatenate(list(_split3(r)) + [jnp.zeros((d, R_PAD - 3 * R_SEG), BF16)], axis=1)
    rb = jnp.concatenate([r_expert_b, r_group_b, jnp.zeros((R_PAD - N_EXPERTS - N_GROUPS,), F32)])
    return r3, rb.reshape(1, R_PAD)


def _tile(n, lc, candidates):
    for t in candidates:
        if n % t == 0 and lc % t == 0:
            return t
    raise ValueError("sequence lengths must be multiples of 128")


def kernel(x, c, ctx, c_ctx, w_ada, b_ada, norm1, norm2, final_norm, w_in, m_conv, m_gate_b, m_norm, a_qnorm, a_wuq, a_kvnorm, a_wukv, g_ws, g_bs, g_vnorm, w_pa, w_pb, w_pc, w_out, r_group, r_group_b, r_expert, r_expert_b, e_w1, e_w3, e_w2):
    b, t_len, d = x.shape
    lc = ctx.shape[1]
    l = lc + t_len
    depth = w_in.shape[0]
    tm = _tile(l, lc, (256, 128))

    xs = jnp.concatenate([ctx, x], axis=1)
    cv = jnp.concatenate([c, c_ctx[None, :]], axis=0)
    mod_all = _ada_call(cv, w_ada, b_ada).reshape(depth, b + 1, 6, d)
    tabs = _rope_tables(t_len, lc)

    f = None
    mod_prev = None
    for li in range(depth):
        last = li == depth - 1
        mod = mod_all[li]
        w = _layer_weights(li, w_in, m_gate_b, a_qnorm, a_wuq, a_kvnorm, a_wukv, g_ws, g_bs, g_vnorm,
                           w_pa, w_pb, w_pc, w_out)
        outs = _inproj_call(xs, mod, norm1[li].reshape(1, d), w, tabs, lc, tm, f, mod_prev)
        qk, vo, gi, gf, gr, q, k, v, yc, br = outs[:10]
        if f is not None:
            xs = outs[10]
        ya = _mlstm_call(qk, vo, gi, gf, gr, m_conv[li], m_norm[li].reshape(1, -1), lc)
        yb = _attn_call(q, k, v, lc, tm, not last)
        r3, rb = _router_weights(r_group[li], r_group_b[li], r_expert[li], r_expert_b[li])
        xs, h2, idx, idxt, cnt = _merge_call(xs, mod, ya, yb, yc, br, w, norm2[li].reshape(1, d),
                                             r3, rb, lc, tm, last)
        f = _experts_call(_group_starts(cnt), h2, idx, idxt, e_w1[li].astype(BF16),
                          e_w3[li].astype(BF16), e_w2[li].astype(BF16), tm)
        mod_prev = mod
    return _final_call(xs, mod_prev, f, final_norm.reshape(1, d), lc, tm)
```

```python
import functools

import jax
import jax.numpy as jnp
from jax import lax
from jax.experimental import pallas as pl
from jax.experimental.pallas import tpu as pltpu

F32 = jnp.float32
BF16 = jnp.bfloat16

EPS = 1e-6
GRID_W = 64
ROPE_THETA = 10000.0

M_HEADS = 4
M_DH = 128
M_WIDTH = M_HEADS * M_DH
M_CHUNK = 128
G_PAD = 128

A_HEADS = 8
A_NOPE = 64
A_ROPE = 32
A_VDIM = 64
A_QRANK = 384
A_KVRANK = 256
A_WIDTH = A_HEADS * A_VDIM
A_PAD = 128
A_HPS = 4
ATT_SCALE = (A_NOPE + A_ROPE) ** -0.5
LOG2E = 1.4426950408889634

G_GROUPS = 4
G_CHUNK = 128
G_WIDTH = 512
G_DG = G_WIDTH // G_GROUPS

N_GROUPS = 4
EXP_PER_GROUP = 4
N_EXPERTS = N_GROUPS * EXP_PER_GROUP
D_EXPERT = 512
R_PAD = 128
R_SEG = 32
MOE_CHUNK = 256

VMEM_LIMIT = 56 * 1024 * 1024


def _dot(a, b):
    return jnp.dot(a, b, preferred_element_type=F32)


def _dot_nt(a, b):
    return lax.dot_general(a, b, (((1,), (1,)), ((), ())), preferred_element_type=F32)


def _dot_tn(a, b):
    return lax.dot_general(a, b, (((0,), (0,)), ((), ())), preferred_element_type=F32)


def _split3(x):
    hi = x.astype(BF16)
    r = x - hi.astype(F32)
    mid = r.astype(BF16)
    lo = (r - mid.astype(F32)).astype(BF16)
    return hi, mid, lo


def _sigmoid(x):
    return 1.0 / (1.0 + jnp.exp(-x))


def _silu(x):
    return x * _sigmoid(x)


def _log_sigmoid(x):
    return jnp.minimum(x, 0.0) - jnp.log1p(jnp.exp(-jnp.abs(x)))


def _gelu(x):
    return 0.5 * x * (1.0 + lax.erf(x * (2.0 ** -0.5)))


def _rms(x, g):
    return x * lax.rsqrt(jnp.mean(x * x, axis=-1, keepdims=True) + EPS) * g


def _params(*sem):
    return pltpu.CompilerParams(dimension_semantics=sem, vmem_limit_bytes=VMEM_LIMIT)


def _const_spec(shape):
    nd = len(shape)
    return pl.BlockSpec(shape, lambda *_: (0,) * nd, pipeline_mode=pl.Buffered(1))


def _ada_kernel(cv_ref, w_ref, b_ref, o_ref):
    s = _silu(cv_ref[...])
    o_ref[0] = _dot(s.astype(BF16), w_ref[0].astype(BF16)) + b_ref[0]


def _ada_call(cv, w_ada, b_ada):
    depth, d, n6 = w_ada.shape
    rows = cv.shape[0]
    tn = n6 // 4
    return pl.pallas_call(
        _ada_kernel,
        grid=(depth, n6 // tn),
        in_specs=[pl.BlockSpec((rows, d), lambda l, j: (0, 0)),
                  pl.BlockSpec((1, d, tn), lambda l, j: (l, 0, j)),
                  pl.BlockSpec((1, 1, tn), lambda l, j: (l, 0, j))],
        out_specs=pl.BlockSpec((1, rows, tn), lambda l, j: (l, 0, j)),
        out_shape=jax.ShapeDtypeStruct((depth, rows, n6), F32),
        compiler_params=_params("parallel", "parallel"),
        name="ada",
    )(cv, w_ada, b_ada.reshape(depth, 1, n6))


def _inproj_kernel(*refs, has_f):
    za_s, zg_s, zbr_s = refs[-3:]
    refs = refs[:-3]
    if has_f:
        f_ref, modp_ref, x_o = refs[0], refs[1], refs[-1]
        refs = refs[2:-1]
    (x_ref, mod_ref, n1_ref, wqk_ref, wvo_ref, wgt_ref, gbt_ref,
     wa_ref, wg_ref, wbr_ref, aqn_ref, akvn_ref, wuq_ref, wuk_ref, wuv_ref, vone_ref,
     cos_ref, sina_ref, sinb_ref, gvn_ref, gws_ref, gbs_ref,
     qk_o, vo_o, gi_o, gf_o, gr_o, q_o, k_o, v_o, yc_o, br_o) = refs
    tm = x_ref.shape[1]
    mod = mod_ref[0]
    x = x_ref[0]
    if has_f:
        x = x + modp_ref[0][5:6] * f_ref[0].astype(F32)
        x_o[0] = x
    h = _rms(x, n1_ref[...]) * (1.0 + mod[1:2]) + mod[0:1]
    hb = h.astype(BF16)

    qk_o[0] = _dot(hb, wqk_ref[...])
    vo_o[0] = _dot(hb, wvo_ref[...]).astype(BF16)
    ng = gi_o.shape[2]
    gates = _dot(hb, wgt_ref[...]) + gbt_ref[...]
    gi_o[0] = gates[:, :ng]
    gf_o[0] = pltpu.roll(gates, gates.shape[1] - ng, 1)[:, :ng]
    gr_o[0] = gates.T[:2 * ng, :]

    za_s[...] = _dot(hb, wa_ref[...])
    zg_s[...] = _dot(hb, wg_ref[...])
    zbr_s[...] = _dot(hb, wbr_ref[...])

    aqn = _rms(za_s[:, :A_QRANK], aqn_ref[...]).astype(BF16)
    akvn = _rms(za_s[:, A_QRANK:A_QRANK + A_KVRANK], akvn_ref[...]).astype(BF16)
    cos = cos_ref[...]
    sina = sina_ref[...]
    sinb = sinb_ref[...]
    half = A_ROPE // 2

    def rope(t):
        return t * cos + pltpu.roll(t, half, 1) * sina + pltpu.roll(t, A_PAD - half, 1) * sinb

    kr = rope(za_s[:, A_QRANK + A_KVRANK:])
    qp = _dot(aqn, wuq_ref[...])
    kp = _dot(akvn, wuk_ref[...])
    for hh in range(A_HEADS):
        sl = slice(hh * A_PAD, (hh + 1) * A_PAD)
        q_o[0, :, sl] = (rope(qp[:, sl]) * (ATT_SCALE * LOG2E)).astype(BF16)
        k_o[0, :, sl] = (kp[:, sl] + kr).astype(BF16)
    v_o[0] = (_dot(akvn, wuv_ref[...]) + vone_ref[...]).astype(BF16)

    gu = _gelu(zg_s[:, :G_WIDTH])
    gv = _gelu(zg_s[:, G_WIDTH:])
    gvn = gvn_ref[...]
    bias = gbs_ref[...]
    for g in range(G_GROUPS):
        sl = slice(g * G_DG, (g + 1) * G_DG)
        xn = _rms(gv[:, sl], gvn[:, sl]).astype(BF16)
        ws = gws_ref[g]
        for ci in range(tm // G_CHUNK):
            r = slice(ci * G_CHUNK, (ci + 1) * G_CHUNK)
            sg = _dot(ws, xn[r]) + bias[:, sl]
            yc_o[0, r, sl] = (gu[r, sl] * sg).astype(BF16)

    br_o[0] = _sigmoid(zbr_s[...]).astype(BF16)


def _inproj_call(xs, mod, n1, w, tabs, lc, tm, f=None, mod_prev=None):
    b, l, d = xs.shape
    nct = lc // tm
    has_f = f is not None
    tok = lambda width: pl.BlockSpec((1, tm, width), lambda bi, j: (bi, j, 0))
    modspec = pl.BlockSpec((1, 6, d), lambda bi, j: (jnp.where(j < nct, b, bi), 0, 0))
    tab = pl.BlockSpec((tm, A_PAD), lambda bi, j: (j, 0))
    consts = [n1, w["wqk"], w["wvo"], w["wgt"], w["gbt"], w["wa"], w["wg"], w["wbr"],
              w["aqn"], w["akvn"], w["wuq"], w["wuk"], w["wuv"], w["vone"]]
    consts2 = [w["gvn"], w["gws"], w["gbs"]]
    in_specs = ([tok(d), modspec] + [_const_spec(a.shape) for a in consts] + [tab, tab, tab]
                + [_const_spec(a.shape) for a in consts2])
    args = [xs, mod, *consts, *tabs, *consts2]
    ng = 4 * M_HEADS
    out_shape = [jax.ShapeDtypeStruct((b, l, 2 * M_WIDTH), F32),
                 jax.ShapeDtypeStruct((b, l, 2 * M_WIDTH), BF16),
                 jax.ShapeDtypeStruct((b, l, ng // 2), F32),
                 jax.ShapeDtypeStruct((b, l, ng // 2), F32),
                 jax.ShapeDtypeStruct((b, ng, l), F32),
                 jax.ShapeDtypeStruct((b, l, A_HEADS * A_PAD), BF16),
                 jax.ShapeDtypeStruct((b, l, A_HEADS * A_PAD), BF16),
                 jax.ShapeDtypeStruct((b, l, A_HEADS * A_PAD), BF16),
                 jax.ShapeDtypeStruct((b, l, G_WIDTH), BF16),
                 jax.ShapeDtypeStruct((b, l, 3 * d), BF16)]
    out_specs = [tok(2 * M_WIDTH), tok(2 * M_WIDTH), tok(ng // 2), tok(ng // 2),
                 pl.BlockSpec((1, ng, tm), lambda bi, j: (bi, 0, j)),
                 tok(A_HEADS * A_PAD), tok(A_HEADS * A_PAD), tok(A_HEADS * A_PAD), tok(G_WIDTH), tok(3 * d)]
    if has_f:
        in_specs = [tok(d), modspec] + in_specs
        args = [f, mod_prev] + args
        out_shape.append(jax.ShapeDtypeStruct((b, l, d), F32))
        out_specs.append(tok(d))
    return pl.pallas_call(
        functools.partial(_inproj_kernel, has_f=has_f), grid=(b, l // tm), in_specs=in_specs,
        out_specs=out_specs, out_shape=out_shape,
        scratch_shapes=[pltpu.VMEM((tm, w["wa"].shape[1]), F32), pltpu.VMEM((tm, 2 * G_WIDTH), F32),
                        pltpu.VMEM((tm, 3 * d), F32)],
        compiler_params=_params("parallel", "parallel"), name="inproj",
    )(*args)


def _scan(x, op, fill, axis, reverse):
    n = x.shape[axis]
    idx = lax.broadcasted_iota(jnp.int32, x.shape, axis)
    k = 1
    while k < n:
        if reverse:
            x = op(x, jnp.where(idx >= n - k, fill, pltpu.roll(x, n - k, axis)))
        else:
            x = op(x, jnp.where(idx < k, fill, pltpu.roll(x, k, axis)))
        k *= 2
    return x


def _mlstm_kernel(qk_ref, vo_ref, gi_ref, gf_ref, gr_ref, conv_ref, mnorm_ref, ya_ref,
                  q_s, kt_s, h_s, bc_s, ml_s, dl_s, br_s, cn_s, m_s, s_s, p_s, qcn_s, u_s, *, lc):
    l = qk_ref.shape[1]
    ch = M_CHUNK
    nc = l // ch
    ncc = lc // ch
    nh = M_HEADS
    ng = 2 * nh
    w = conv_ref[...]
    row = lax.broadcasted_iota(jnp.int32, (ch, 1), 0)

    def conv_chunk(j):
        r0 = pl.multiple_of(j * ch, ch)
        cur = qk_ref[0, pl.ds(r0, ch), :]
        prev8 = qk_ref[0, pl.ds(pl.multiple_of(jnp.maximum(r0 - 8, 0), 8), 8), :]
        next8 = qk_ref[0, pl.ds(pl.multiple_of(jnp.minimum(r0 + ch, l - 8), 8), 8), :]
        seg_start = jnp.logical_or(j == 0, j == ncc)
        seg_end = jnp.logical_or(j == ncc - 1, j == nc - 1)
        pe = jnp.where(seg_start, 0.0, prev8[7:8, :])
        ne = jnp.where(seg_end, 0.0, next8[0:1, :])
        xp = jnp.where(row == 0, pe, pltpu.roll(cur, 1, 0))
        xn = jnp.where(row == ch - 1, ne, pltpu.roll(cur, ch - 1, 0))
        y = _silu(xp * w[0:1] + cur * w[1:2] + xn * w[2:3])
        q_s[pl.ds(r0, ch), :] = (y[:, :M_WIDTH] * (M_DH ** -0.5)).astype(BF16)
        kt_s[:, pl.ds(r0, ch)] = y[:, M_WIDTH:].T.astype(BF16)

    ri = lax.broadcasted_iota(jnp.int32, (ch, ch), 0)
    ci = lax.broadcasted_iota(jnp.int32, (ch, ch), 1)
    lower = ri >= ci
    upper = ri <= ci
    ones_blk = jnp.ones((ch, M_DH), BF16)
    fwd_c = lax.broadcasted_iota(jnp.int32, (ch, ng), 1) < nh
    fwd_r = lax.broadcasted_iota(jnp.int32, (ng, ch), 0) < nh
    lane_c = lax.broadcasted_iota(jnp.int32, (ch, ng), 1)

    def local_chunk(j):
        r0 = pl.multiple_of(j * ch, ch)
        rows = pl.ds(r0, ch)
        lfc = _log_sigmoid(gf_ref[0, rows, :])
        gr = gr_ref[0, :, rows]
        lfr = _log_sigmoid(gr[ng:])
        pre_c = _scan(lfc, jnp.add, 0.0, 0, False)
        pre_r = _scan(lfr, jnp.add, 0.0, 1, False)
        b_c = jnp.where(fwd_c, pre_c, jnp.sum(lfc, axis=0, keepdims=True) + lfc - pre_c)
        b_r = jnp.where(fwd_r, pre_r, jnp.sum(lfr, axis=1, keepdims=True) + lfr - pre_r)
        g_c = gi_ref[0, rows, :] - b_c
        g_r = gr[:ng] - b_r
        cg_c = jnp.where(fwd_c, _scan(g_c, jnp.maximum, -jnp.inf, 0, False),
                         _scan(g_c, jnp.maximum, -jnp.inf, 0, True))
        bc_s[rows, :] = b_c
        br_s[:, rows] = b_r
        ml_s[rows, :] = b_c + cg_c
        dl = jnp.zeros((ch, ng), F32)
        for hh in range(nh):
            sl = slice(hh * M_DH, (hh + 1) * M_DH)
            s_s[hh] = _dot(q_s[rows, sl], kt_s[sl, rows])
        for hh in range(nh):
            s = s_s[hh]
            for d in range(2):
                jj = d * nh + hh
                wgt = jnp.exp(jnp.where(upper if d else lower, g_r[jj:jj + 1, :] - cg_c[:, jj:jj + 1], -jnp.inf))
                p_s[hh, d * ch:(d + 1) * ch] = (s * wgt).astype(BF16)
        for hh in range(nh):
            sl = slice(hh * M_DH, (hh + 1) * M_DH)
            v1 = jnp.concatenate([vo_ref[0, rows, sl], ones_blk], axis=1)
            nd2 = _dot(p_s[hh], v1)
            for d in range(2):
                jj = d * nh + hh
                nd = nd2[d * ch:(d + 1) * ch]
                h_s[d, rows, sl] = nd[:, :M_DH]
                dl = jnp.where(lane_c == jj, nd[:, M_DH:M_DH + ng], dl)
        dl_s[rows, :] = dl

    def conv_local_body(j, carry):
        conv_chunk(j + 1)
        local_chunk(j)
        return carry

    conv_chunk(jnp.int32(0))
    lax.fori_loop(0, nc - 1, conv_local_body, 0)
    local_chunk(jnp.int32(nc - 1))

    cn_s[...] = jnp.zeros_like(cn_s)
    m_s[...] = jnp.zeros_like(m_s)
    lane_r = lax.broadcasted_iota(jnp.int32, (1, ng), 1)

    def scan_issue(r0, d):
        rows = pl.ds(r0, ch)
        gr = gr_ref[0, :, rows]
        br = br_s[:, rows]
        tot = jnp.sum(_log_sigmoid(gr[ng:]), axis=1, keepdims=True)
        scal = []
        for hh in range(nh):
            fi = d * nh + hh
            sl = slice(hh * M_DH, (hh + 1) * M_DH)
            qcn_s[fi] = _dot(q_s[rows, sl], cn_s[fi].astype(BF16))
            m_old = m_s[fi][:, 0:1]
            b_e = tot[fi:fi + 1, :]
            d_end = b_e - br[fi:fi + 1, :] + gr[fi:fi + 1, :]
            m_end = jnp.max(d_end, axis=-1, keepdims=True)
            m_new = jnp.maximum(b_e + m_old, m_end)
            ktw = (kt_s[sl, rows].astype(F32) * jnp.exp(d_end - m_end)).astype(BF16)
            v1 = jnp.concatenate([vo_ref[0, rows, sl], ones_blk], axis=1)
            u_s[fi] = _dot(ktw, v1)
            scal.append((m_old, m_new, jnp.exp(b_e + m_old - m_new), jnp.exp(m_end - m_new)))
        return scal

    def scan_finish(r0, d, scal):
        rows = pl.ds(r0, ch)
        m_row = jnp.zeros((1, ng), F32)
        for hh in range(nh):
            m_row = jnp.where(lane_r == d * nh + hh, scal[hh][0], m_row)

        inter = bc_s[rows, :] + m_row
        ml = ml_s[rows, :]
        mt = jnp.maximum(inter, ml)
        a = jnp.exp(ml - mt)
        wi = jnp.exp(inter - mt)
        qn = jnp.zeros((ch, ng), F32)
        for hh in range(nh):
            fi = d * nh + hh
            qn = jnp.where(lane_c == fi, qcn_s[fi, :, M_DH:M_DH + ng], qn)
        den = a * dl_s[rows, :] + wi * qn
        rinv = 1.0 / jnp.maximum(jnp.abs(den), jnp.exp(-mt))
        c_loc = a * rinv
        c_int = wi * rinv

        for hh in range(nh):
            fi = d * nh + hh
            sl = slice(hh * M_DH, (hh + 1) * M_DH)
            h_s[d, rows, sl] = (c_loc[:, fi:fi + 1] * h_s[d, rows, sl]
                                + c_int[:, fi:fi + 1] * qcn_s[fi, :, :M_DH])
            cn_s[fi] = scal[hh][2] * cn_s[fi] + scal[hh][3] * u_s[fi]
            m_s[fi] = jnp.broadcast_to(scal[hh][1], (1, M_DH))

    def scan_body(s, carry):
        rf = pl.multiple_of(s * ch, ch)
        rb = pl.multiple_of(jnp.where(s < ncc, ncc - 1 - s, nc - 1 - s + ncc) * ch, ch)
        sf = scan_issue(rf, 0)
        sb = scan_issue(rb, 1)
        scan_finish(rf, 0, sf)
        scan_finish(rb, 1, sb)
        return carry

    lax.fori_loop(0, nc, scan_body, 0)

    mnorm = mnorm_ref[...]

    def out_body(j, carry):
        r0 = pl.multiple_of(j * ch, ch)
        hsum = h_s[0, pl.ds(r0, ch), :] + h_s[1, pl.ds(r0, ch), :]
        og = _sigmoid(vo_ref[0, pl.ds(r0, ch), M_WIDTH:].astype(F32))
        for hh in range(M_HEADS):
            sl = slice(hh * M_DH, (hh + 1) * M_DH)
            ya_ref[0, pl.ds(r0, ch), sl] = (_rms(hsum[:, sl], mnorm[:, sl]) * og[:, sl]).astype(BF16)
        return carry

    lax.fori_loop(0, nc, out_body, 0)


def _mlstm_call(qk, vo, gi, gf, gr, conv, mnorm, lc):
    b, l, _ = qk.shape
    ng = 2 * M_HEADS
    return pl.pallas_call(
        functools.partial(_mlstm_kernel, lc=lc),
        grid=(b,),
        in_specs=[pl.BlockSpec((1, l, 2 * M_WIDTH), lambda bi: (bi, 0, 0), pipeline_mode=pl.Buffered(1)),
                  pl.BlockSpec((1, l, 2 * M_WIDTH), lambda bi: (bi, 0, 0)),
                  pl.BlockSpec((1, l, ng), lambda bi: (bi, 0, 0)),
                  pl.BlockSpec((1, l, ng), lambda bi: (bi, 0, 0)),
                  pl.BlockSpec((1, 2 * ng, l), lambda bi: (bi, 0, 0)),
                  _const_spec(conv.shape), _const_spec(mnorm.shape)],
        out_specs=pl.BlockSpec((1, l, M_WIDTH), lambda bi: (bi, 0, 0)),
        out_shape=jax.ShapeDtypeStruct((b, l, M_WIDTH), BF16),
        scratch_shapes=[pltpu.VMEM((l, M_WIDTH), BF16),
                        pltpu.VMEM((M_WIDTH, l), BF16),
                        pltpu.VMEM((2, l, M_WIDTH), F32),
                        pltpu.VMEM((l, ng), F32),
                        pltpu.VMEM((l, ng), F32),
                        pltpu.VMEM((l, ng), F32),
                        pltpu.VMEM((ng, l), F32),
                        pltpu.VMEM((2 * M_HEADS, M_DH, 2 * M_DH), F32),
                        pltpu.VMEM((2 * M_HEADS, 1, M_DH), F32),
                        pltpu.VMEM((M_HEADS, M_CHUNK, M_CHUNK), F32),
                        pltpu.VMEM((M_HEADS, 2 * M_CHUNK, M_CHUNK), BF16),
                        pltpu.VMEM((2 * M_HEADS, M_CHUNK, 2 * M_DH), F32),
                        pltpu.VMEM((2 * M_HEADS, M_DH, 2 * M_DH), F32)],
        compiler_params=_params("parallel"), name="mlstm",
    )(qk, vo, gi, gf, gr, conv, mnorm)


def _attn_kernel(q_ref, k_ref, v_ref, o_ref, s_s, p_s, *, lc, ctx_out):
    tq = q_ref.shape[1]
    l = k_ref.shape[1]
    qi = pl.program_id(2)
    nct = lc // tq
    lane = lax.broadcasted_iota(jnp.int32, (tq, 2 * A_VDIM), 1)

    def run(klen):
        outs = []
        for hh in range(A_HPS):
            sl = slice(hh * A_PAD, (hh + 1) * A_PAD)
            s_s[hh, :, :klen] = _dot_nt(q_ref[0, :, sl], k_ref[0, :klen, sl])
        row_max = [jnp.max(s_s[hh, :, :klen], axis=-1, keepdims=True) for hh in range(A_HPS)]
        for hh in range(A_HPS):
            p_s[hh, :, :klen] = jnp.exp2((s_s[hh, :, :klen] - row_max[hh]).astype(BF16))
        for hh in range(A_HPS):
            sl = slice(hh * A_PAD, (hh + 1) * A_PAD)
            nd = _dot(p_s[hh, :, :klen], v_ref[0, :klen, sl])
            outs.append(nd / pltpu.roll(nd, A_VDIM, 1))
        for pp in range(A_HPS // 2):
            o_ref[0, :, pp * A_PAD:(pp + 1) * A_PAD] = jnp.where(
                lane < A_VDIM, outs[2 * pp], pltpu.roll(outs[2 * pp + 1], A_VDIM, 1)).astype(BF16)

    @pl.when(qi >= nct)
    def _():
        run(l)

    @pl.when(qi < nct)
    def _():
        if ctx_out:
            run(lc)
        else:
            o_ref[...] = jnp.zeros_like(o_ref)


def _attn_call(q, k, v, lc, tq, ctx_out):
    b, l, _ = q.shape
    return pl.pallas_call(
        functools.partial(_attn_kernel, lc=lc, ctx_out=ctx_out),
        grid=(b, A_HEADS // A_HPS, l // tq),
        in_specs=[pl.BlockSpec((1, tq, A_HPS * A_PAD), lambda bi, p, qi: (bi, qi, p)),
                  pl.BlockSpec((1, l, A_HPS * A_PAD), lambda bi, p, qi: (bi, 0, p)),
                  pl.BlockSpec((1, l, A_HPS * A_PAD), lambda bi, p, qi: (bi, 0, p))],
        out_specs=pl.BlockSpec((1, tq, A_HPS * A_VDIM), lambda bi, p, qi: (bi, qi, p)),
        out_shape=jax.ShapeDtypeStruct((b, l, A_WIDTH), BF16),
        scratch_shapes=[pltpu.VMEM((A_HPS, tq, l), F32), pltpu.VMEM((A_HPS, tq, l), BF16)],
        compiler_params=_params("parallel", "parallel", "arbitrary"), name="attn",
    )(q, k, v)


def _merge_kernel(x_ref, mod_ref, modc_ref, ya_ref, yb_ref, yc_ref, br_ref, wpa_ref, wpb_ref, wpc_ref, wout_ref,
                  n2_ref, r_ref, rb_ref, o_ref, h2_o, idx_o, idxt_o, cnt_o, y_s, *, tm, nct, skip_ctx):
    d = x_ref.shape[2]
    nsub = x_ref.shape[1] // tm
    modb = mod_ref[0]
    modc = modc_ref[0]
    tiles = [slice(s * tm, (s + 1) * tm) for s in range(nsub)]
    ctx = [pl.program_id(1) * nsub + s < nct for s in range(nsub)]
    mods = [jnp.where(c, modc, modb) for c in ctx]
    for s, rows in enumerate(tiles):
        br = br_ref[0, rows, :]
        y = (br[:, :d].astype(F32) * _dot(ya_ref[0, rows, :], wpa_ref[...])
             + br[:, d:2 * d].astype(F32) * _dot(yb_ref[0, rows, :], wpb_ref[...])
             + br[:, 2 * d:].astype(F32) * _dot(yc_ref[0, rows, :], wpc_ref[...]))
        y_s[s] = y.astype(BF16)
    for s, rows in enumerate(tiles):
        o_ref[0, rows, :] = x_ref[0, rows, :] + mods[s][2:3] * _dot(y_s[s], wout_ref[...])
    for s, rows in enumerate(tiles):
        logits = _route_logits(o_ref[0, rows, :], mods[s], n2_ref, r_ref, rb_ref, h2_o, rows)
        left_out = ctx[s] if skip_ctx else None
        _route_assign(logits, left_out, d, h2_o, idx_o, idxt_o, cnt_o, rows, s)


def _merge_call(xs, mod, ya, yb, yc, br, w, n2, r3, rb, lc, tm, skip_ctx):
    b, l, d = xs.shape
    nct = lc // tm
    nsub = next(n for n in (3, 2, 1) if l % (n * tm) == 0)
    tg = nsub * tm
    tok = lambda width: pl.BlockSpec((1, tg, width), lambda bi, j: (bi, j, 0))
    consts = [w["wpa"], w["wpb"], w["wpc"], w["wout"], n2, r3, rb]
    return pl.pallas_call(
        functools.partial(_merge_kernel, tm=tm, nct=nct, skip_ctx=skip_ctx), grid=(b, l // tg),
        in_specs=[tok(d), pl.BlockSpec((1, 6, d), lambda bi, j: (bi, 0, 0)),
                  pl.BlockSpec((1, 6, d), lambda bi, j: (b, 0, 0)),
                  tok(M_WIDTH), tok(A_WIDTH), tok(G_WIDTH), tok(3 * d)] + [_const_spec(a.shape) for a in consts],
        out_specs=[tok(d), tok(d + R_PAD), tok(8), pl.BlockSpec((1, 8, tg), lambda bi, j: (bi, 0, j)),
                   pl.BlockSpec((1, nsub, 8, R_PAD), lambda bi, j: (bi, j, 0, 0))],
        out_shape=[jax.ShapeDtypeStruct((b, l, d), F32),
                   jax.ShapeDtypeStruct((b, l, d + R_PAD), BF16),
                   jax.ShapeDtypeStruct((b, l, 8), jnp.int32), jax.ShapeDtypeStruct((b, 8, l), jnp.int32),
                   jax.ShapeDtypeStruct((b, l // tm, 8, R_PAD), F32)],
        scratch_shapes=[pltpu.VMEM((nsub, tm, d), BF16)],
        compiler_params=_params("parallel", "parallel"), name="merge",
    )(xs, mod, mod, ya, yb, yc, br, *consts)


def _route_logits(x, mod, n2_ref, r_ref, rb_ref, h2_o, rows):
    d = x.shape[1]
    h2 = _rms(x, n2_ref[...]) * (1.0 + mod[4:5]) + mod[3:4]
    h2_o[0, rows, :d] = h2.astype(BF16)
    r = r_ref[...]
    pp = sum(_dot(piece, r) for piece in _split3(h2)[:2])
    return pp + pltpu.roll(pp, R_PAD - R_SEG, 1) + pltpu.roll(pp, R_PAD - 2 * R_SEG, 1) + rb_ref[...]


def _route_assign(logits, left_out, d, h2_o, idx_o, idxt_o, cnt_o, rows, s):
    tm = logits.shape[0]
    el = logits[:, :N_EXPERTS]
    gl = logits[:, N_EXPERTS:N_EXPERTS + N_GROUPS]
    big = 1e9

    lane_g = lax.broadcasted_iota(jnp.int32, (tm, N_GROUPS), 1).astype(F32)
    gmax = jnp.max(gl, axis=-1, keepdims=True)
    g_sel = jnp.min(jnp.where(gl == gmax, lane_g, big), axis=-1, keepdims=True)
    g_prob = 1.0 / jnp.sum(jnp.exp(gl - gmax), axis=-1, keepdims=True)
    if left_out is not None:
        g_sel = jnp.where(left_out, -1.0, g_sel)

    lane_i = lax.broadcasted_iota(jnp.int32, (tm, N_EXPERTS), 1)
    lane_e = lane_i.astype(F32)
    lane_grp = (lane_i // EXP_PER_GROUP).astype(F32)
    v1 = jnp.where(lane_grp == g_sel, el, -jnp.inf)
    t1 = jnp.max(v1, axis=-1, keepdims=True)
    i1 = jnp.min(jnp.where(v1 == t1, lane_e, big), axis=-1, keepdims=True)
    v2 = jnp.where(lane_e == i1, -jnp.inf, v1)
    t2 = jnp.max(v2, axis=-1, keepdims=True)
    i2 = jnp.min(jnp.where(v2 == t2, lane_e, big), axis=-1, keepdims=True)
    e21 = jnp.exp(t2 - t1)
    w1 = 1.0 / (1.0 + e21)
    w2 = e21 * w1
    comb = (jnp.where(lane_e == i1, w1, 0.0) + jnp.where(lane_e == i2, w2, 0.0)) * g_prob
    tail = jnp.zeros((tm, R_PAD - 3 * N_EXPERTS), BF16)
    h2_o[0, rows, d:] = jnp.concatenate(list(_split3(comb)) + [tail], axis=1)

    lane_p = lax.broadcasted_iota(jnp.int32, (tm, R_PAD), 1)
    onehot = jnp.where(lane_p.astype(F32) == g_sel, 1.0, 0.0)
    ri = lax.broadcasted_iota(jnp.int32, (tm, tm), 0)
    ci = lax.broadcasted_iota(jnp.int32, (tm, tm), 1)
    before = jnp.where(ri > ci, 1.0, 0.0).astype(BF16)
    rank = jnp.sum(_dot(before, onehot.astype(BF16)) * onehot, axis=-1, keepdims=True)
    cnt_o[0, s] = jnp.broadcast_to(jnp.sum(onehot, axis=0, keepdims=True), (8, R_PAD))
    fields = jnp.where(lane_p == 0, g_sel, jnp.where(lane_p == 1, rank, 0.0))
    idx_o[0, rows, :] = fields[:, :8].astype(jnp.int32)
    idxt_o[0, :, rows] = fields.T[:8, :].astype(jnp.int32)


def _experts_kernel(st_ref, h2_ref, idx_ref, idxt_ref, w1_ref, w3_ref, w2_ref, o_ref, hs_s, ys_s, a_s, hid_s, *, tb):
    l = h2_ref.shape[1]
    d = o_ref.shape[2]
    nblk = l // tb
    ch = MOE_CHUNK
    bi = pl.program_id(0)
    g = pl.program_id(1)

    @pl.when(g == 0)
    def _():
        ys_s[...] = jnp.zeros_like(ys_s)

    def group_base(gg):
        return (bi * N_GROUPS + gg) * (nblk + 1)

    def group_offset(upto):
        off = 0
        for gg in range(N_GROUPS - 1):
            padded = ((st_ref[group_base(gg) + nblk] + ch - 1) // ch) * ch
            off = off + jnp.where(gg < upto, padded, 0)
        return off

    base = group_base(g)
    cnt = st_ref[base + nblk]
    goff = group_offset(g)

    def chunk(lo, ch):
        sub_iota = lax.broadcasted_iota(jnp.int32, (ch, tb), 0)
        lane_e = lax.broadcasted_iota(jnp.int32, (ch, N_EXPERTS), 1)
        hs_s[:ch] = jnp.zeros((ch, hs_s.shape[1]), F32)
        for k in range(nblk):
            s_k = st_ref[base + k]
            e_k = st_ref[base + k + 1]
            rows = slice(k * tb, (k + 1) * tb)

            @pl.when(jnp.logical_and(s_k < lo + ch, e_k > lo))
            def _(s_k=s_k, rows=rows):
                it = idxt_ref[0, :, rows]
                pos = jnp.where(it[0:1] == g, it[1:2] + (s_k - lo), -1)
                p = jnp.where(sub_iota == pos, 1.0, 0.0).astype(BF16)
                hs_s[:ch] += _dot(p, h2_ref[0, rows, :])

        hsb = hs_s[:ch, :d].astype(BF16)
        cs = (hs_s[:ch, d:d + N_EXPERTS] + hs_s[:ch, d + N_EXPERTS:d + 2 * N_EXPERTS]
              + hs_s[:ch, d + 2 * N_EXPERTS:d + 3 * N_EXPERTS])
        for e in range(EXP_PER_GROUP):
            a_s[2 * e, :ch] = _dot(hsb, w1_ref[e])
            a_s[2 * e + 1, :ch] = _dot(hsb, w3_ref[e])
        for e in range(EXP_PER_GROUP):
            ce = jnp.sum(jnp.where(lane_e == g * EXP_PER_GROUP + e, cs, 0.0), axis=-1, keepdims=True)
            hid_s[e, :ch] = (_silu(a_s[2 * e, :ch]) * a_s[2 * e + 1, :ch] * ce).astype(BF16)
        y = jnp.zeros((ch, d), F32)
        for e in range(EXP_PER_GROUP):
            y = y + _dot(hid_s[e, :ch], w2_ref[e])
        ys_s[pl.ds(pl.multiple_of(goff + lo, ch), ch), :] = y.astype(BF16)

    half = ch // 2
    nfull = cnt // ch
    rem = cnt - nfull * ch
    nloop = nfull + jnp.where(rem > half, 1, 0)

    def chunk_body(c, carry):
        chunk(c * ch, ch)
        return carry

    lax.fori_loop(0, nloop, chunk_body, 0)

    @pl.when(jnp.logical_and(rem > 0, rem <= half))
    def _():
        chunk(nfull * ch, half)

    @pl.when(g == N_GROUPS - 1)
    def _():
        lane_w = lax.broadcasted_iota(jnp.int32, (tb, ch), 1)
        goffs = [group_offset(gg) for gg in range(N_GROUPS)]

        def window(k, gg, shift):
            s_k = st_ref[group_base(gg) + k]
            win = (s_k // half) * half
            ic = idx_ref[0, k * tb:(k + 1) * tb, :]
            pos = jnp.where(ic[:, 0:1] == gg, ic[:, 1:2] + (s_k - win - shift), -1)
            q = jnp.where(lane_w == pos, 1.0, 0.0).astype(BF16)
            start = pl.multiple_of(goffs[gg] + win + shift, half)
            return _dot(q, ys_s[pl.ds(start, ch), :])

        for k in range(nblk):
            acc = window(k, 0, 0)
            for gg in range(1, N_GROUPS):
                acc = acc + window(k, gg, 0)
            o_ref[0, k * tb:(k + 1) * tb, :] = acc.astype(BF16)
        for k in range(nblk):
            rows = slice(k * tb, (k + 1) * tb)
            for gg in range(N_GROUPS):
                s_k = st_ref[group_base(gg) + k]
                e_k = st_ref[group_base(gg) + k + 1]

                @pl.when(e_k - (s_k // half) * half > ch)
                def _(k=k, gg=gg, rows=rows):
                    o_ref[0, rows, :] = (o_ref[0, rows, :].astype(F32) + window(k, gg, ch)).astype(BF16)


def _experts_call(starts, h2, idx, idxt, w1, w3, w2, tb):
    b, l, de = h2.shape
    d = de - R_PAD
    whole = lambda width: pl.BlockSpec((1, l, width), lambda bi, g, st: (bi, 0, 0))
    grid_spec = pltpu.PrefetchScalarGridSpec(
        num_scalar_prefetch=1, grid=(b, N_GROUPS),
        in_specs=[pl.BlockSpec((1, l, de), lambda bi, g, st: (bi, 0, 0), pipeline_mode=pl.Buffered(1)),
                  whole(8), pl.BlockSpec((1, 8, l), lambda bi, g, st: (bi, 0, 0)),
                  pl.BlockSpec((EXP_PER_GROUP, d, D_EXPERT), lambda bi, g, st: (g, 0, 0)),
                  pl.BlockSpec((EXP_PER_GROUP, d, D_EXPERT), lambda bi, g, st: (g, 0, 0)),
                  pl.BlockSpec((EXP_PER_GROUP, D_EXPERT, d), lambda bi, g, st: (g, 0, 0))],
        out_specs=whole(d),
        scratch_shapes=[pltpu.VMEM((MOE_CHUNK, de), F32),
                        pltpu.VMEM((l + (N_GROUPS + 1) * MOE_CHUNK, d), BF16),
                        pltpu.VMEM((2 * EXP_PER_GROUP, MOE_CHUNK, D_EXPERT), F32),
                        pltpu.VMEM((EXP_PER_GROUP, MOE_CHUNK, D_EXPERT), BF16)])
    return pl.pallas_call(
        functools.partial(_experts_kernel, tb=tb), grid_spec=grid_spec,
        out_shape=jax.ShapeDtypeStruct((b, l, d), BF16),
        compiler_params=_params("parallel", "arbitrary"), name="experts",
    )(starts, h2, idx, idxt, w1, w3, w2)


def _group_starts(cnt):
    c = cnt[:, :, 0, :N_GROUPS].astype(jnp.int32)
    s = jnp.cumsum(c, axis=1)
    s = jnp.concatenate([jnp.zeros_like(s[:, :1]), s], axis=1)
    return jnp.transpose(s, (0, 2, 1)).reshape(-1)


def _final_kernel(x_ref, mod_ref, f_ref, g_ref, o_ref):
    o_ref[0] = _rms(x_ref[0] + mod_ref[0][5:6] * f_ref[0].astype(F32), g_ref[...])


def _final_call(xs, mod, f, g, lc, tm):
    b, l, d = xs.shape
    off = lc // tm
    lat = pl.BlockSpec((1, tm, d), lambda bi, j: (bi, j + off, 0))
    return pl.pallas_call(
        _final_kernel, grid=(b, (l - lc) // tm),
        in_specs=[lat, pl.BlockSpec((1, 6, d), lambda bi, j: (bi, 0, 0)), lat, _const_spec(g.shape)],
        out_specs=pl.BlockSpec((1, tm, d), lambda bi, j: (bi, j, 0)),
        out_shape=jax.ShapeDtypeStruct((b, l - lc, d), F32),
        compiler_params=_params("parallel", "parallel"), name="final_norm",
    )(xs, mod, f, g)


def _rope_tables(t_len, lc):
    half = A_ROPE // 2
    rows = t_len // GRID_W
    r = jnp.repeat(jnp.arange(rows, dtype=F32), GRID_W)
    col = jnp.tile(jnp.arange(GRID_W, dtype=F32), rows)
    inv = ROPE_THETA ** (-jnp.arange(0, half, 2, dtype=F32) / half)
    ang = jnp.concatenate([r[:, None] * inv, col[:, None] * inv], axis=-1)
    cos = jnp.concatenate([jnp.ones((lc, half), F32), jnp.cos(ang)], axis=0)
    sin = jnp.concatenate([jnp.zeros((lc, half), F32), jnp.sin(ang)], axis=0)
    l = lc + t_len
    ones = jnp.ones((l, A_NOPE), F32)
    zeros = jnp.zeros((l, A_NOPE), F32)
    tail1 = jnp.ones((l, A_PAD - A_NOPE - A_ROPE), F32)
    tail0 = jnp.zeros((l, A_PAD - A_NOPE - A_ROPE), F32)
    zh = jnp.zeros((l, half), F32)
    cos_t = jnp.concatenate([ones, cos, cos, tail1], axis=-1)
    sina_t = jnp.concatenate([zeros, zh, sin, tail0], axis=-1)
    sinb_t = jnp.concatenate([zeros, -sin, zh, tail0], axis=-1)
    return cos_t, sina_t, sinb_t


def _layer_weights(l, w_in, m_gate_b, a_qnorm, a_wuq, a_kvnorm, a_wukv, g_ws, g_bs, g_vnorm,
                   w_pa, w_pb, w_pc, w_out):
    d = w_in.shape[1]
    wi = w_in[l]
    o = 0

    def take(n):
        nonlocal o
        s = wi[:, o:o + n]
        o += n
        return s

    mq, mk, mv, mo, mg = take(M_WIDTH), take(M_WIDTH), take(M_WIDTH), take(M_WIDTH), take(4 * M_HEADS)
    aq, akv, akr = take(A_QRANK), take(A_KVRANK), take(A_ROPE)
    gu, gv = take(G_WIDTH), take(G_WIDTH)
    br = take(3 * d)
    nh = M_HEADS
    gb = m_gate_b[l]
    mgo = jnp.concatenate([mg[:, :nh], mg[:, 2 * nh:3 * nh], mg[:, nh:2 * nh], mg[:, 3 * nh:]], axis=1)
    gbo = jnp.concatenate([gb[:nh], gb[2 * nh:3 * nh], gb[nh:2 * nh], gb[3 * nh:]])
    akr_pad = jnp.concatenate([jnp.zeros((d, A_NOPE), F32), akr,
                               jnp.zeros((d, A_PAD - A_NOPE - A_ROPE), F32)], axis=1)
    wuq = a_wuq[l].reshape(A_QRANK, A_HEADS, A_NOPE + A_ROPE)
    wuq = jnp.pad(wuq, ((0, 0), (0, 0), (0, A_PAD - A_NOPE - A_ROPE))).reshape(A_QRANK, A_HEADS * A_PAD)
    wukv = a_wukv[l].reshape(A_KVRANK, A_HEADS, A_NOPE + A_VDIM)
    wuk = jnp.pad(wukv[:, :, :A_NOPE], ((0, 0), (0, 0), (0, A_PAD - A_NOPE))).reshape(A_KVRANK, A_HEADS * A_PAD)
    wuv = jnp.pad(wukv[:, :, A_NOPE:], ((0, 0), (0, 0), (0, A_PAD - A_VDIM))).reshape(A_KVRANK, A_HEADS * A_PAD)
    vone = jnp.tile(jnp.concatenate([jnp.zeros((A_VDIM,), F32), jnp.ones((A_PAD - A_VDIM,), F32)]),
                    A_HEADS).reshape(1, A_HEADS * A_PAD)
    gbs = jnp.repeat(g_bs[l].T, G_DG, axis=1)
    return dict(
        wqk=jnp.concatenate([mq, mk], 1).astype(BF16), wvo=jnp.concatenate([mv, mo], 1).astype(BF16),
        wgt=jnp.pad(mgo, ((0, 0), (0, G_PAD - 4 * nh))).astype(BF16),
        gbt=jnp.pad(gbo, (0, G_PAD - 4 * nh)).reshape(1, G_PAD),
        wa=jnp.concatenate([aq, akv, akr_pad], 1).astype(BF16),
        wg=jnp.concatenate([gu, gv], 1).astype(BF16), wbr=br.astype(BF16),
        aqn=a_qnorm[l].reshape(1, -1), akvn=a_kvnorm[l].reshape(1, -1),
        wuq=wuq.astype(BF16), wuk=wuk.astype(BF16), wuv=wuv.astype(BF16), vone=vone,
        gvn=g_vnorm[l].reshape(1, -1), gws=g_ws[l].astype(BF16), gbs=gbs,
        wpa=w_pa[l].astype(BF16), wpb=w_pb[l].astype(BF16), wpc=w_pc[l].astype(BF16),
        wout=w_out[l].astype(BF16))


def _router_weights(r_group, r_group_b, r_expert, r_expert_b):
    d = r_group.shape[0]
    pad = R_SEG - N_EXPERTS - N_GROUPS
    r = jnp.concatenate([r_expert, r_group, jnp.zeros((d, pad), F32)], axis=1)
    r3 = jnp.concatenate(list(_split3(r)) + [jnp.zeros((d, R_PAD - 3 * R_SEG), BF16)], axis=1)
    rb = jnp.concatenate([r_expert_b, r_group_b, jnp.zeros((R_PAD - N_EXPERTS - N_GROUPS,), F32)])
    return r3, rb.reshape(1, R_PAD)


def _tile(n, lc, candidates):
    for t in candidates:
        if n % t == 0 and lc % t == 0:
            return t
    raise ValueError("sequence lengths must be multiples of 128")


def kernel(x, c, ctx, c_ctx, w_ada, b_ada, norm1, norm2, final_norm, w_in, m_conv, m_gate_b, m_norm, a_qnorm, a_wuq, a_kvnorm, a_wukv, g_ws, g_bs, g_vnorm, w_pa, w_pb, w_pc, w_out, r_group, r_group_b, r_expert, r_expert_b, e_w1, e_w3, e_w2):
    b, t_len, d = x.shape
    lc = ctx.shape[1]
    l = lc + t_len
    depth = w_in.shape[0]
    tm = _tile(l, lc, (256, 128))

    xs = jnp.concatenate([ctx, x], axis=1)
    cv = jnp.concatenate([c, c_ctx[None, :]], axis=0)
    mod_all = _ada_call(cv, w_ada, b_ada).reshape(depth, b + 1, 6, d)
    tabs = _rope_tables(t_len, lc)

    f = None
    mod_prev = None
    for li in range(depth):
        last = li == depth - 1
        mod = mod_all[li]
        w = _layer_weights(li, w_in, m_gate_b, a_qnorm, a_wuq, a_kvnorm, a_wukv, g_ws, g_bs, g_vnorm,
                           w_pa, w_pb, w_pc, w_out)
        outs = _inproj_call(xs, mod, norm1[li].reshape(1, d), w, tabs, lc, tm, f, mod_prev)
        qk, vo, gi, gf, gr, q, k, v, yc, br = outs[:10]
        if f is not None:
            xs = outs[10]
        ya = _mlstm_call(qk, vo, gi, gf, gr, m_conv[li], m_norm[li].reshape(1, -1), lc)
        yb = _attn_call(q, k, v, lc, tm, not last)
        r3, rb = _router_weights(r_group[li], r_group_b[li], r_expert[li], r_expert_b[li])
        xs, h2, idx, idxt, cnt = _merge_call(xs, mod, ya, yb, yc, br, w, norm2[li].reshape(1, d),
                                             r3, rb, lc, tm, last)
        f = _experts_call(_group_starts(cnt), h2, idx, idxt, e_w1[li].astype(BF16),
                          e_w3[li].astype(BF16), e_w2[li].astype(BF16), tm)
        mod_prev = mod
    return _final_call(xs, mod_prev, f, final_norm.reshape(1, d), lc, tm)
```

```python
import functools

import jax
import jax.numpy as jnp
from jax import lax
from jax.experimental import pallas as pl
from jax.experimental.pallas import tpu as pltpu

F32 = jnp.float32
BF16 = jnp.bfloat16

EPS = 1e-6
GRID_W = 64
ROPE_THETA = 10000.0

M_HEADS = 4
M_DH = 128
M_WIDTH = M_HEADS * M_DH
M_CHUNK = 128
G_PAD = 128

A_HEADS = 8
A_NOPE = 64
A_ROPE = 32
A_VDIM = 64
A_QRANK = 384
A_KVRANK = 256
A_WIDTH = A_HEADS * A_VDIM
A_PAD = 128
A_HPS = 4
ATT_SCALE = (A_NOPE + A_ROPE) ** -0.5
LOG2E = 1.4426950408889634

G_GROUPS = 4
G_CHUNK = 128
G_WIDTH = 512
G_DG = G_WIDTH // G_GROUPS

N_GROUPS = 4
EXP_PER_GROUP = 4
N_EXPERTS = N_GROUPS * EXP_PER_GROUP
D_EXPERT = 512
R_PAD = 128
R_SEG = 32
MOE_CHUNK = 256
MOE_GATHER_BLOCKS = 5

VMEM_LIMIT = 56 * 1024 * 1024


def _dot(a, b):
    return jnp.dot(a, b, preferred_element_type=F32)


def _dot_nt(a, b):
    return lax.dot_general(a, b, (((1,), (1,)), ((), ())), preferred_element_type=F32)


def _dot_tn(a, b):
    return lax.dot_general(a, b, (((0,), (0,)), ((), ())), preferred_element_type=F32)


def _split3(x):
    hi = x.astype(BF16)
    r = x - hi.astype(F32)
    mid = r.astype(BF16)
    lo = (r - mid.astype(F32)).astype(BF16)
    return hi, mid, lo


def _sigmoid(x):
    return 1.0 / (1.0 + jnp.exp(-x))


def _silu(x):
    return x * _sigmoid(x)


def _log_sigmoid(x):
    return jnp.minimum(x, 0.0) - jnp.log1p(jnp.exp(-jnp.abs(x)))


def _gelu(x):
    return 0.5 * x * (1.0 + lax.erf(x * (2.0 ** -0.5)))


def _rms(x, g):
    return x * lax.rsqrt(jnp.mean(x * x, axis=-1, keepdims=True) + EPS) * g


def _params(*sem):
    return pltpu.CompilerParams(dimension_semantics=sem, vmem_limit_bytes=VMEM_LIMIT)


def _const_spec(shape):
    nd = len(shape)
    return pl.BlockSpec(shape, lambda *_: (0,) * nd, pipeline_mode=pl.Buffered(1))


def _ada_kernel(cv_ref, w_ref, b_ref, o_ref):
    s = _silu(cv_ref[...])
    o_ref[0] = _dot(s.astype(BF16), w_ref[0].astype(BF16)) + b_ref[0]


def _ada_call(cv, w_ada, b_ada):
    depth, d, n6 = w_ada.shape
    rows = cv.shape[0]
    tn = n6 // 4
    return pl.pallas_call(
        _ada_kernel,
        grid=(depth, n6 // tn),
        in_specs=[pl.BlockSpec((rows, d), lambda l, j: (0, 0)),
                  pl.BlockSpec((1, d, tn), lambda l, j: (l, 0, j)),
                  pl.BlockSpec((1, 1, tn), lambda l, j: (l, 0, j))],
        out_specs=pl.BlockSpec((1, rows, tn), lambda l, j: (l, 0, j)),
        out_shape=jax.ShapeDtypeStruct((depth, rows, n6), F32),
        compiler_params=_params("parallel", "parallel"),
        name="ada",
    )(cv, w_ada, b_ada.reshape(depth, 1, n6))


def _inproj_kernel(*refs, has_f):
    za_s, zg_s, zbr_s = refs[-3:]
    refs = refs[:-3]
    if has_f:
        f_ref, modp_ref, x_o = refs[0], refs[1], refs[-1]
        refs = refs[2:-1]
    (x_ref, mod_ref, n1_ref, wqk_ref, wvo_ref, wgt_ref, gbt_ref,
     wa_ref, wg_ref, wbr_ref, aqn_ref, akvn_ref, wuq_ref, wuk_ref, wuv_ref, vone_ref,
     cos_ref, sina_ref, sinb_ref, gvn_ref, gws_ref, gbs_ref,
     qk_o, vo_o, gi_o, gf_o, gr_o, q_o, k_o, v_o, yc_o, br_o) = refs
    tm = x_ref.shape[1]
    mod = mod_ref[0]
    x = x_ref[0]
    if has_f:
        x = x + modp_ref[0][5:6] * f_ref[0].astype(F32)
        x_o[0] = x
    h = _rms(x, n1_ref[...]) * (1.0 + mod[1:2]) + mod[0:1]
    hb = h.astype(BF16)

    qk_o[0] = _dot(hb, wqk_ref[...])
    vo_o[0] = _dot(hb, wvo_ref[...]).astype(BF16)
    ng = gi_o.shape[2]
    gates = _dot(hb, wgt_ref[...]) + gbt_ref[...]
    gi_o[0] = gates[:, :ng]
    gf_o[0] = pltpu.roll(gates, gates.shape[1] - ng, 1)[:, :ng]
    gr_o[0] = gates.T[:2 * ng, :]

    za_s[...] = _dot(hb, wa_ref[...])
    zg_s[...] = _dot(hb, wg_ref[...])
    zbr_s[...] = _dot(hb, wbr_ref[...])

    aqn = _rms(za_s[:, :A_QRANK], aqn_ref[...]).astype(BF16)
    akvn = _rms(za_s[:, A_QRANK:A_QRANK + A_KVRANK], akvn_ref[...]).astype(BF16)
    cos = cos_ref[...]
    sina = sina_ref[...]
    sinb = sinb_ref[...]
    half = A_ROPE // 2

    def rope(t):
        return t * cos + pltpu.roll(t, half, 1) * sina + pltpu.roll(t, A_PAD - half, 1) * sinb

    kr = rope(za_s[:, A_QRANK + A_KVRANK:])
    qp = _dot(aqn, wuq_ref[...])
    kp = _dot(akvn, wuk_ref[...])
    for hh in range(A_HEADS):
        sl = slice(hh * A_PAD, (hh + 1) * A_PAD)
        q_o[0, :, sl] = (rope(qp[:, sl]) * (ATT_SCALE * LOG2E)).astype(BF16)
        k_o[0, :, sl] = (kp[:, sl] + kr).astype(BF16)
    v_o[0] = (_dot(akvn, wuv_ref[...]) + vone_ref[...]).astype(BF16)

    gu = _gelu(zg_s[:, :G_WIDTH])
    gv = _gelu(zg_s[:, G_WIDTH:])
    gvn = gvn_ref[...]
    bias = gbs_ref[...]
    for g in range(G_GROUPS):
        sl = slice(g * G_DG, (g + 1) * G_DG)
        xn = _rms(gv[:, sl], gvn[:, sl]).astype(BF16)
        ws = gws_ref[g]
        for ci in range(tm // G_CHUNK):
            r = slice(ci * G_CHUNK, (ci + 1) * G_CHUNK)
            sg = _dot(ws, xn[r]) + bias[:, sl]
            yc_o[0, r, sl] = (gu[r, sl] * sg).astype(BF16)

    br_o[0] = _sigmoid(zbr_s[...]).astype(BF16)


def _inproj_call(xs, mod, n1, w, tabs, lc, tm, f=None, mod_prev=None):
    b, l, d = xs.shape
    nct = lc // tm
    has_f = f is not None
    tok = lambda width: pl.BlockSpec((1, tm, width), lambda bi, j: (bi, j, 0))
    modspec = pl.BlockSpec((1, 6, d), lambda bi, j: (jnp.where(j < nct, b, bi), 0, 0))
    tab = pl.BlockSpec((tm, A_PAD), lambda bi, j: (j, 0))
    consts = [n1, w["wqk"], w["wvo"], w["wgt"], w["gbt"], w["wa"], w["wg"], w["wbr"],
              w["aqn"], w["akvn"], w["wuq"], w["wuk"], w["wuv"], w["vone"]]
    consts2 = [w["gvn"], w["gws"], w["gbs"]]
    in_specs = ([tok(d), modspec] + [_const_spec(a.shape) for a in consts] + [tab, tab, tab]
                + [_const_spec(a.shape) for a in consts2])
    args = [xs, mod, *consts, *tabs, *consts2]
    ng = 4 * M_HEADS
    out_shape = [jax.ShapeDtypeStruct((b, l, 2 * M_WIDTH), F32),
                 jax.ShapeDtypeStruct((b, l, 2 * M_WIDTH), BF16),
                 jax.ShapeDtypeStruct((b, l, ng // 2), F32),
                 jax.ShapeDtypeStruct((b, l, ng // 2), F32),
                 jax.ShapeDtypeStruct((b, ng, l), F32),
                 jax.ShapeDtypeStruct((b, l, A_HEADS * A_PAD), BF16),
                 jax.ShapeDtypeStruct((b, l, A_HEADS * A_PAD), BF16),
                 jax.ShapeDtypeStruct((b, l, A_HEADS * A_PAD), BF16),
                 jax.ShapeDtypeStruct((b, l, G_WIDTH), BF16),
                 jax.ShapeDtypeStruct((b, l, 3 * d), BF16)]
    out_specs = [tok(2 * M_WIDTH), tok(2 * M_WIDTH), tok(ng // 2), tok(ng // 2),
                 pl.BlockSpec((1, ng, tm), lambda bi, j: (bi, 0, j)),
                 tok(A_HEADS * A_PAD), tok(A_HEADS * A_PAD), tok(A_HEADS * A_PAD), tok(G_WIDTH), tok(3 * d)]
    if has_f:
        in_specs = [tok(d), modspec] + in_specs
        args = [f, mod_prev] + args
        out_shape.append(jax.ShapeDtypeStruct((b, l, d), F32))
        out_specs.append(tok(d))
    return pl.pallas_call(
        functools.partial(_inproj_kernel, has_f=has_f), grid=(b, l // tm), in_specs=in_specs,
        out_specs=out_specs, out_shape=out_shape,
        scratch_shapes=[pltpu.VMEM((tm, w["wa"].shape[1]), F32), pltpu.VMEM((tm, 2 * G_WIDTH), F32),
                        pltpu.VMEM((tm, 3 * d), F32)],
        compiler_params=_params("parallel", "parallel"), name="inproj",
    )(*args)


def _scan(x, op, fill, axis, reverse):
    n = x.shape[axis]
    idx = lax.broadcasted_iota(jnp.int32, x.shape, axis)
    k = 1
    while k < n:
        if reverse:
            x = op(x, jnp.where(idx >= n - k, fill, pltpu.roll(x, n - k, axis)))
        else:
            x = op(x, jnp.where(idx < k, fill, pltpu.roll(x, k, axis)))
        k *= 2
    return x


def _mlstm_kernel(qk_ref, vo_ref, gi_ref, gf_ref, gr_ref, conv_ref, mnorm_ref, ya_ref,
                  q_s, kt_s, h_s, bc_s, ml_s, dl_s, br_s, cn_s, m_s, s_s, p_s, qcn_s, u_s, *, lc):
    l = qk_ref.shape[1]
    ch = M_CHUNK
    nc = l // ch
    ncc = lc // ch
    nh = M_HEADS
    ng = 2 * nh
    w = conv_ref[...]
    row = lax.broadcasted_iota(jnp.int32, (ch, 1), 0)

    def conv_chunk(j):
        r0 = pl.multiple_of(j * ch, ch)
        cur = qk_ref[0, pl.ds(r0, ch), :]
        prev8 = qk_ref[0, pl.ds(pl.multiple_of(jnp.maximum(r0 - 8, 0), 8), 8), :]
        next8 = qk_ref[0, pl.ds(pl.multiple_of(jnp.minimum(r0 + ch, l - 8), 8), 8), :]
        seg_start = jnp.logical_or(j == 0, j == ncc)
        seg_end = jnp.logical_or(j == ncc - 1, j == nc - 1)
        pe = jnp.where(seg_start, 0.0, prev8[7:8, :])
        ne = jnp.where(seg_end, 0.0, next8[0:1, :])
        xp = jnp.where(row == 0, pe, pltpu.roll(cur, 1, 0))
        xn = jnp.where(row == ch - 1, ne, pltpu.roll(cur, ch - 1, 0))
        y = _silu(xp * w[0:1] + cur * w[1:2] + xn * w[2:3])
        q_s[pl.ds(r0, ch), :] = (y[:, :M_WIDTH] * (M_DH ** -0.5)).astype(BF16)
        kt_s[:, pl.ds(r0, ch)] = y[:, M_WIDTH:].T.astype(BF16)

    ri = lax.broadcasted_iota(jnp.int32, (ch, ch), 0)
    ci = lax.broadcasted_iota(jnp.int32, (ch, ch), 1)
    lower = ri >= ci
    upper = ri <= ci
    ones_blk = jnp.ones((ch, M_DH), BF16)
    fwd_c = lax.broadcasted_iota(jnp.int32, (ch, ng), 1) < nh
    fwd_r = lax.broadcasted_iota(jnp.int32, (ng, ch), 0) < nh
    lane_c = lax.broadcasted_iota(jnp.int32, (ch, ng), 1)

    def local_chunk(j):
        r0 = pl.multiple_of(j * ch, ch)
        rows = pl.ds(r0, ch)
        lfc = _log_sigmoid(gf_ref[0, rows, :])
        gr = gr_ref[0, :, rows]
        lfr = _log_sigmoid(gr[ng:])
        pre_c = _scan(lfc, jnp.add, 0.0, 0, False)
        pre_r = _scan(lfr, jnp.add, 0.0, 1, False)
        b_c = jnp.where(fwd_c, pre_c, jnp.sum(lfc, axis=0, keepdims=True) + lfc - pre_c)
        b_r = jnp.where(fwd_r, pre_r, jnp.sum(lfr, axis=1, keepdims=True) + lfr - pre_r)
        g_c = gi_ref[0, rows, :] - b_c
        g_r = gr[:ng] - b_r
        cg_c = jnp.where(fwd_c, _scan(g_c, jnp.maximum, -jnp.inf, 0, False),
                         _scan(g_c, jnp.maximum, -jnp.inf, 0, True))
        bc_s[rows, :] = b_c
        br_s[:, rows] = b_r
        ml_s[rows, :] = b_c + cg_c
        dl = jnp.zeros((ch, ng), F32)
        for hh in range(nh):
            sl = slice(hh * M_DH, (hh + 1) * M_DH)
            s_s[hh] = _dot(q_s[rows, sl], kt_s[sl, rows])
        for hh in range(nh):
            s = s_s[hh]
            for d in range(2):
                jj = d * nh + hh
                wgt = jnp.exp(jnp.where(upper if d else lower, g_r[jj:jj + 1, :] - cg_c[:, jj:jj + 1], -jnp.inf))
                p_s[hh, d * ch:(d + 1) * ch] = (s * wgt).astype(BF16)
        for hh in range(nh):
            sl = slice(hh * M_DH, (hh + 1) * M_DH)
            v1 = jnp.concatenate([vo_ref[0, rows, sl], ones_blk], axis=1)
            nd2 = _dot(p_s[hh], v1)
            for d in range(2):
                jj = d * nh + hh
                nd = nd2[d * ch:(d + 1) * ch]
                h_s[d, rows, sl] = nd[:, :M_DH]
                dl = jnp.where(lane_c == jj, nd[:, M_DH:M_DH + ng], dl)
        dl_s[rows, :] = dl

    def conv_local_body(j, carry):
        conv_chunk(j + 1)
        local_chunk(j)
        return carry

    conv_chunk(jnp.int32(0))
    lax.fori_loop(0, nc - 1, conv_local_body, 0)
    local_chunk(jnp.int32(nc - 1))

    cn_s[...] = jnp.zeros_like(cn_s)
    m_s[...] = jnp.zeros_like(m_s)
    lane_r = lax.broadcasted_iota(jnp.int32, (1, ng), 1)

    def scan_issue(r0, d):
        rows = pl.ds(r0, ch)
        gr = gr_ref[0, :, rows]
        br = br_s[:, rows]
        tot = jnp.sum(_log_sigmoid(gr[ng:]), axis=1, keepdims=True)
        scal = []
        for hh in range(nh):
            fi = d * nh + hh
            sl = slice(hh * M_DH, (hh + 1) * M_DH)
            qcn_s[fi] = _dot(q_s[rows, sl], cn_s[fi].astype(BF16))
            m_old = m_s[fi][:, 0:1]
            b_e = tot[fi:fi + 1, :]
            d_end = b_e - br[fi:fi + 1, :] + gr[fi:fi + 1, :]
            m_end = jnp.max(d_end, axis=-1, keepdims=True)
            m_new = jnp.maximum(b_e + m_old, m_end)
            ktw = (kt_s[sl, rows].astype(F32) * jnp.exp(d_end - m_end)).astype(BF16)
            v1 = jnp.concatenate([vo_ref[0, rows, sl], ones_blk], axis=1)
            u_s[fi] = _dot(ktw, v1)
            scal.append((m_old, m_new, jnp.exp(b_e + m_old - m_new), jnp.exp(m_end - m_new)))
        return scal

    def scan_finish(r0, d, scal):
        rows = pl.ds(r0, ch)
        m_row = jnp.zeros((1, ng), F32)
        for hh in range(nh):
            m_row = jnp.where(lane_r == d * nh + hh, scal[hh][0], m_row)

        inter = bc_s[rows, :] + m_row
        ml = ml_s[rows, :]
        mt = jnp.maximum(inter, ml)
        a = jnp.exp(ml - mt)
        wi = jnp.exp(inter - mt)
        qn = jnp.zeros((ch, ng), F32)
        for hh in range(nh):
            fi = d * nh + hh
            qn = jnp.where(lane_c == fi, qcn_s[fi, :, M_DH:M_DH + ng], qn)
        den = a * dl_s[rows, :] + wi * qn
        rinv = 1.0 / jnp.maximum(jnp.abs(den), jnp.exp(-mt))
        c_loc = a * rinv
        c_int = wi * rinv

        for hh in range(nh):
            fi = d * nh + hh
            sl = slice(hh * M_DH, (hh + 1) * M_DH)
            h_s[d, rows, sl] = (c_loc[:, fi:fi + 1] * h_s[d, rows, sl]
                                + c_int[:, fi:fi + 1] * qcn_s[fi, :, :M_DH])
            cn_s[fi] = scal[hh][2] * cn_s[fi] + scal[hh][3] * u_s[fi]
            m_s[fi] = jnp.broadcast_to(scal[hh][1], (1, M_DH))

    def scan_body(s, carry):
        rf = pl.multiple_of(s * ch, ch)
        rb = pl.multiple_of(jnp.where(s < ncc, ncc - 1 - s, nc - 1 - s + ncc) * ch, ch)
        sf = scan_issue(rf, 0)
        sb = scan_issue(rb, 1)
        scan_finish(rf, 0, sf)
        scan_finish(rb, 1, sb)
        return carry

    lax.fori_loop(0, nc, scan_body, 0)

    mnorm = mnorm_ref[...]

    def out_body(j, carry):
        r0 = pl.multiple_of(j * ch, ch)
        hsum = h_s[0, pl.ds(r0, ch), :] + h_s[1, pl.ds(r0, ch), :]
        og = _sigmoid(vo_ref[0, pl.ds(r0, ch), M_WIDTH:].astype(F32))
        for hh in range(M_HEADS):
            sl = slice(hh * M_DH, (hh + 1) * M_DH)
            ya_ref[0, pl.ds(r0, ch), sl] = (_rms(hsum[:, sl], mnorm[:, sl]) * og[:, sl]).astype(BF16)
        return carry

    lax.fori_loop(0, nc, out_body, 0)


def _mlstm_call(qk, vo, gi, gf, gr, conv, mnorm, lc):
    b, l, _ = qk.shape
    ng = 2 * M_HEADS
    return pl.pallas_call(
        functools.partial(_mlstm_kernel, lc=lc),
        grid=(b,),
        in_specs=[pl.BlockSpec((1, l, 2 * M_WIDTH), lambda bi: (bi, 0, 0), pipeline_mode=pl.Buffered(1)),
                  pl.BlockSpec((1, l, 2 * M_WIDTH), lambda bi: (bi, 0, 0)),
                  pl.BlockSpec((1, l, ng), lambda bi: (bi, 0, 0)),
                  pl.BlockSpec((1, l, ng), lambda bi: (bi, 0, 0)),
                  pl.BlockSpec((1, 2 * ng, l), lambda bi: (bi, 0, 0)),
                  _const_spec(conv.shape), _const_spec(mnorm.shape)],
        out_specs=pl.BlockSpec((1, l, M_WIDTH), lambda bi: (bi, 0, 0)),
        out_shape=jax.ShapeDtypeStruct((b, l, M_WIDTH), BF16),
        scratch_shapes=[pltpu.VMEM((l, M_WIDTH), BF16),
                        pltpu.VMEM((M_WIDTH, l), BF16),
                        pltpu.VMEM((2, l, M_WIDTH), F32),
                        pltpu.VMEM((l, ng), F32),
                        pltpu.VMEM((l, ng), F32),
                        pltpu.VMEM((l, ng), F32),
                        pltpu.VMEM((ng, l), F32),
                        pltpu.VMEM((2 * M_HEADS, M_DH, 2 * M_DH), F32),
                        pltpu.VMEM((2 * M_HEADS, 1, M_DH), F32),
                        pltpu.VMEM((M_HEADS, M_CHUNK, M_CHUNK), F32),
                        pltpu.VMEM((M_HEADS, 2 * M_CHUNK, M_CHUNK), BF16),
                        pltpu.VMEM((2 * M_HEADS, M_CHUNK, 2 * M_DH), F32),
                        pltpu.VMEM((2 * M_HEADS, M_DH, 2 * M_DH), F32)],
        compiler_params=_params("parallel"), name="mlstm",
    )(qk, vo, gi, gf, gr, conv, mnorm)


def _attn_kernel(q_ref, k_ref, v_ref, o_ref, s_s, p_s, *, lc, ctx_out):
    tq = q_ref.shape[1]
    l = k_ref.shape[1]
    qi = pl.program_id(2)
    nct = lc // tq
    lane = lax.broadcasted_iota(jnp.int32, (tq, 2 * A_VDIM), 1)

    def run(klen):
        outs = []
        for hh in range(A_HPS):
            sl = slice(hh * A_PAD, (hh + 1) * A_PAD)
            s_s[hh, :, :klen] = _dot_nt(q_ref[0, :, sl], k_ref[0, :klen, sl])
        row_max = [jnp.max(s_s[hh, :, :klen], axis=-1, keepdims=True) for hh in range(A_HPS)]
        for hh in range(A_HPS):
            p_s[hh, :, :klen] = jnp.exp2((s_s[hh, :, :klen] - row_max[hh]).astype(BF16))
        for hh in range(A_HPS):
            sl = slice(hh * A_PAD, (hh + 1) * A_PAD)
            nd = _dot(p_s[hh, :, :klen], v_ref[0, :klen, sl])
            outs.append(nd / pltpu.roll(nd, A_VDIM, 1))
        for pp in range(A_HPS // 2):
            o_ref[0, :, pp * A_PAD:(pp + 1) * A_PAD] = jnp.where(
                lane < A_VDIM, outs[2 * pp], pltpu.roll(outs[2 * pp + 1], A_VDIM, 1)).astype(BF16)

    @pl.when(qi >= nct)
    def _():
        run(l)

    @pl.when(qi < nct)
    def _():
        if ctx_out:
            run(lc)
        else:
            o_ref[...] = jnp.zeros_like(o_ref)


def _attn_call(q, k, v, lc, tq, ctx_out):
    b, l, _ = q.shape
    return pl.pallas_call(
        functools.partial(_attn_kernel, lc=lc, ctx_out=ctx_out),
        grid=(b, A_HEADS // A_HPS, l // tq),
        in_specs=[pl.BlockSpec((1, tq, A_HPS * A_PAD), lambda bi, p, qi: (bi, qi, p)),
                  pl.BlockSpec((1, l, A_HPS * A_PAD), lambda bi, p, qi: (bi, 0, p)),
                  pl.BlockSpec((1, l, A_HPS * A_PAD), lambda bi, p, qi: (bi, 0, p))],
        out_specs=pl.BlockSpec((1, tq, A_HPS * A_VDIM), lambda bi, p, qi: (bi, qi, p)),
        out_shape=jax.ShapeDtypeStruct((b, l, A_WIDTH), BF16),
        scratch_shapes=[pltpu.VMEM((A_HPS, tq, l), F32), pltpu.VMEM((A_HPS, tq, l), BF16)],
        compiler_params=_params("parallel", "parallel", "arbitrary"), name="attn",
    )(q, k, v)


def _merge_kernel(x_ref, mod_ref, modc_ref, ya_ref, yb_ref, yc_ref, br_ref, wpa_ref, wpb_ref, wpc_ref, wout_ref,
                  n2_ref, r_ref, rb_ref, o_ref, h2_o, idx_o, idxt_o, cnt_o, y_s, *, tm, nct, skip_ctx):
    d = x_ref.shape[2]
    nsub = x_ref.shape[1] // tm
    modb = mod_ref[0]
    modc = modc_ref[0]
    tiles = [slice(s * tm, (s + 1) * tm) for s in range(nsub)]
    ctx = [pl.program_id(1) * nsub + s < nct for s in range(nsub)]
    mods = [jnp.where(c, modc, modb) for c in ctx]
    for s, rows in enumerate(tiles):
        br = br_ref[0, rows, :]
        y = (br[:, :d].astype(F32) * _dot(ya_ref[0, rows, :], wpa_ref[...])
             + br[:, d:2 * d].astype(F32) * _dot(yb_ref[0, rows, :], wpb_ref[...])
             + br[:, 2 * d:].astype(F32) * _dot(yc_ref[0, rows, :], wpc_ref[...]))
        y_s[s] = y.astype(BF16)
    for s, rows in enumerate(tiles):
        o_ref[0, rows, :] = x_ref[0, rows, :] + mods[s][2:3] * _dot(y_s[s], wout_ref[...])
    for s, rows in enumerate(tiles):
        logits = _route_logits(o_ref[0, rows, :], mods[s], n2_ref, r_ref, rb_ref, h2_o, rows)
        left_out = ctx[s] if skip_ctx else None
        _route_assign(logits, left_out, d, h2_o, idx_o, idxt_o, cnt_o, rows, s)


def _merge_call(xs, mod, ya, yb, yc, br, w, n2, r3, rb, lc, tm, skip_ctx):
    b, l, d = xs.shape
    nct = lc // tm
    nsub = next(n for n in (3, 2, 1) if l % (n * tm) == 0)
    tg = nsub * tm
    tok = lambda width: pl.BlockSpec((1, tg, width), lambda bi, j: (bi, j, 0))
    consts = [w["wpa"], w["wpb"], w["wpc"], w["wout"], n2, r3, rb]
    return pl.pallas_call(
        functools.partial(_merge_kernel, tm=tm, nct=nct, skip_ctx=skip_ctx), grid=(b, l // tg),
        in_specs=[tok(d), pl.BlockSpec((1, 6, d), lambda bi, j: (bi, 0, 0)),
                  pl.BlockSpec((1, 6, d), lambda bi, j: (b, 0, 0)),
                  tok(M_WIDTH), tok(A_WIDTH), tok(G_WIDTH), tok(3 * d)] + [_const_spec(a.shape) for a in consts],
        out_specs=[tok(d), tok(d + R_PAD), tok(8), pl.BlockSpec((1, 8, tg), lambda bi, j: (bi, 0, j)),
                   pl.BlockSpec((1, nsub, 8, R_PAD), lambda bi, j: (bi, j, 0, 0))],
        out_shape=[jax.ShapeDtypeStruct((b, l, d), F32),
                   jax.ShapeDtypeStruct((b, l, d + R_PAD), BF16),
                   jax.ShapeDtypeStruct((b, l, 8), jnp.int32), jax.ShapeDtypeStruct((b, 8, l), jnp.int32),
                   jax.ShapeDtypeStruct((b, l // tm, 8, R_PAD), F32)],
        scratch_shapes=[pltpu.VMEM((nsub, tm, d), BF16)],
        compiler_params=_params("parallel", "parallel"), name="merge",
    )(xs, mod, mod, ya, yb, yc, br, *consts)


def _route_logits(x, mod, n2_ref, r_ref, rb_ref, h2_o, rows):
    d = x.shape[1]
    h2 = _rms(x, n2_ref[...]) * (1.0 + mod[4:5]) + mod[3:4]
    h2_o[0, rows, :d] = h2.astype(BF16)
    r = r_ref[...]
    pp = sum(_dot(piece, r) for piece in _split3(h2)[:2])
    return pp + pltpu.roll(pp, R_PAD - R_SEG, 1) + pltpu.roll(pp, R_PAD - 2 * R_SEG, 1) + rb_ref[...]


def _route_assign(logits, left_out, d, h2_o, idx_o, idxt_o, cnt_o, rows, s):
    tm = logits.shape[0]
    el = logits[:, :N_EXPERTS]
    gl = logits[:, N_EXPERTS:N_EXPERTS + N_GROUPS]
    big = 1e9

    lane_g = lax.broadcasted_iota(jnp.int32, (tm, N_GROUPS), 1).astype(F32)
    gmax = jnp.max(gl, axis=-1, keepdims=True)
    g_sel = jnp.min(jnp.where(gl == gmax, lane_g, big), axis=-1, keepdims=True)
    g_prob = 1.0 / jnp.sum(jnp.exp(gl - gmax), axis=-1, keepdims=True)

    lane_i = lax.broadcasted_iota(jnp.int32, (tm, N_EXPERTS), 1)
    lane_e = lane_i.astype(F32)
    lane_grp = (lane_i // EXP_PER_GROUP).astype(F32)
    v1 = jnp.where(lane_grp == g_sel, el, -jnp.inf)
    t1 = jnp.max(v1, axis=-1, keepdims=True)
    i1 = jnp.min(jnp.where(v1 == t1, lane_e, big), axis=-1, keepdims=True)
    v2 = jnp.where(lane_e == i1, -jnp.inf, v1)
    t2 = jnp.max(v2, axis=-1, keepdims=True)
    i2 = jnp.min(jnp.where(v2 == t2, lane_e, big), axis=-1, keepdims=True)
    e21 = jnp.exp(t2 - t1)
    w1 = 1.0 / (1.0 + e21)
    w2 = e21 * w1
    comb = (jnp.where(lane_e == i1, w1, 0.0) + jnp.where(lane_e == i2, w2, 0.0)) * g_prob
    tail = jnp.zeros((tm, R_PAD - 3 * N_EXPERTS), BF16)
    h2_o[0, rows, d:] = jnp.concatenate(list(_split3(comb)) + [tail], axis=1)

    if left_out is not None:
        g_sel = jnp.where(left_out, -1.0, g_sel)
    lane_p = lax.broadcasted_iota(jnp.int32, (tm, R_PAD), 1)
    onehot = jnp.where(lane_p.astype(F32) == g_sel, 1.0, 0.0)
    ri = lax.broadcasted_iota(jnp.int32, (tm, tm), 0)
    ci = lax.broadcasted_iota(jnp.int32, (tm, tm), 1)
    before = jnp.where(ri > ci, 1.0, 0.0).astype(BF16)
    rank = jnp.sum(_dot(before, onehot.astype(BF16)) * onehot, axis=-1, keepdims=True)
    cnt_o[0, s] = jnp.broadcast_to(jnp.sum(onehot, axis=0, keepdims=True), (8, R_PAD))
    fields = jnp.where(lane_p == 0, g_sel, jnp.where(lane_p == 1, rank, 0.0))
    idx_o[0, rows, :] = fields[:, :8].astype(jnp.int32)
    idxt_o[0, :, rows] = fields.T[:8, :].astype(jnp.int32)


def _experts_kernel(st_ref, h2_ref, idx_ref, idxt_ref, w1_ref, w3_ref, w2_ref, o_ref, hs_s, ys_s, a_s, hid_s, *, tb):
    l = h2_ref.shape[1]
    d = o_ref.shape[2]
    nblk = l // tb
    ch = MOE_CHUNK
    bi = pl.program_id(0)
    g = pl.program_id(1)

    @pl.when(g == 0)
    def _():
        ys_s[...] = jnp.zeros_like(ys_s)

    def group_base(gg):
        return (bi * N_GROUPS + gg) * (nblk + 1)

    def group_offset(upto):
        off = 0
        for gg in range(N_GROUPS - 1):
            padded = ((st_ref[group_base(gg) + nblk] + ch - 1) // ch) * ch
            off = off + jnp.where(gg < upto, padded, 0)
        return off

    base = group_base(g)
    cnt = st_ref[base + nblk]
    goff = group_offset(g)
    gw = min(MOE_GATHER_BLOCKS, nblk)
    nwin = -(-nblk // gw)

    def chunk(lo, ch):
        lane_e = lax.broadcasted_iota(jnp.int32, (ch, N_EXPERTS), 1)
        sub_iota = lax.broadcasted_iota(jnp.int32, (ch, gw * tb), 0)
        lane_blk = lax.broadcasted_iota(jnp.int32, (1, gw * tb), 1) // tb
        k0 = 0
        for k in range(nblk):
            k0 = k0 + jnp.where(st_ref[base + k + 1] <= lo, 1, 0)

        def window(j):
            want = k0 + j * gw
            kj = jnp.minimum(want, nblk - gw)
            tok0 = pl.multiple_of(kj * tb, tb)
            it = idxt_ref[0, :, pl.ds(tok0, gw * tb)]
            offs = jnp.concatenate([jnp.full((1, tb), st_ref[base + kj + i] - lo, jnp.int32) for i in range(gw)],
                                   axis=1)
            mine = jnp.where(lane_blk >= want - kj, it[0:1], -1)
            pos = jnp.where(mine == g, it[1:2] + offs, -1)
            p = jnp.where(sub_iota == pos, 1.0, 0.0).astype(BF16)
            return _dot(p, h2_ref[0, pl.ds(tok0, gw * tb), :])

        hs_s[:ch] = window(0)
        for j in range(1, nwin):
            first = k0 + j * gw

            @pl.when(jnp.logical_and(first < nblk, st_ref[base + jnp.minimum(first, nblk)] < lo + ch))
            def _(j=j):
                hs_s[:ch] += window(j)

        hsb = hs_s[:ch, :d].astype(BF16)
        cs = (hs_s[:ch, d:d + N_EXPERTS] + hs_s[:ch, d + N_EXPERTS:d + 2 * N_EXPERTS]
              + hs_s[:ch, d + 2 * N_EXPERTS:d + 3 * N_EXPERTS])
        for e in range(EXP_PER_GROUP):
            a_s[2 * e, :ch] = _dot(hsb, w1_ref[e])
            a_s[2 * e + 1, :ch] = _dot(hsb, w3_ref[e])
        for e in range(EXP_PER_GROUP):
            ce = jnp.sum(jnp.where(lane_e == g * EXP_PER_GROUP + e, cs, 0.0), axis=-1, keepdims=True)
            hid_s[e, :ch] = (_silu(a_s[2 * e, :ch]) * a_s[2 * e + 1, :ch] * ce).astype(BF16)
        y = jnp.zeros((ch, d), F32)
        for e in range(EXP_PER_GROUP):
            y = y + _dot(hid_s[e, :ch], w2_ref[e])
        ys_s[pl.ds(pl.multiple_of(goff + lo, ch), ch), :] = y.astype(BF16)

    half = ch // 2
    nfull = cnt // ch
    rem = cnt - nfull * ch
    nloop = nfull + jnp.where(rem > half, 1, 0)

    def chunk_body(c, carry):
        chunk(c * ch, ch)
        return carry

    lax.fori_loop(0, nloop, chunk_body, 0)

    @pl.when(jnp.logical_and(rem > 0, rem <= half))
    def _():
        chunk(nfull * ch, half)

    @pl.when(g == N_GROUPS - 1)
    def _():
        lane_w = lax.broadcasted_iota(jnp.int32, (tb, ch), 1)
        goffs = [group_offset(gg) for gg in range(N_GROUPS)]

        def window(k, gg, shift):
            s_k = st_ref[group_base(gg) + k]
            win = (s_k // half) * half
            ic = idx_ref[0, k * tb:(k + 1) * tb, :]
            pos = jnp.where(ic[:, 0:1] == gg, ic[:, 1:2] + (s_k - win - shift), -1)
            q = jnp.where(lane_w == pos, 1.0, 0.0).astype(BF16)
            start = pl.multiple_of(goffs[gg] + win + shift, half)
            return _dot(q, ys_s[pl.ds(start, ch), :])

        for k in range(nblk):
            acc = window(k, 0, 0)
            for gg in range(1, N_GROUPS):
                acc = acc + window(k, gg, 0)
            o_ref[0, k * tb:(k + 1) * tb, :] = acc.astype(BF16)
        for k in range(nblk):
            rows = slice(k * tb, (k + 1) * tb)
            for gg in range(N_GROUPS):
                s_k = st_ref[group_base(gg) + k]
                e_k = st_ref[group_base(gg) + k + 1]

                @pl.when(e_k - (s_k // half) * half > ch)
                def _(k=k, gg=gg, rows=rows):
                    o_ref[0, rows, :] = (o_ref[0, rows, :].astype(F32) + window(k, gg, ch)).astype(BF16)


def _experts_call(starts, h2, idx, idxt, w1, w3, w2, tb):
    b, l, de = h2.shape
    d = de - R_PAD
    whole = lambda width: pl.BlockSpec((1, l, width), lambda bi, g, st: (bi, 0, 0))
    grid_spec = pltpu.PrefetchScalarGridSpec(
        num_scalar_prefetch=1, grid=(b, N_GROUPS),
        in_specs=[pl.BlockSpec((1, l, de), lambda bi, g, st: (bi, 0, 0), pipeline_mode=pl.Buffered(1)),
                  whole(8), pl.BlockSpec((1, 8, l), lambda bi, g, st: (bi, 0, 0)),
                  pl.BlockSpec((EXP_PER_GROUP, d, D_EXPERT), lambda bi, g, st: (g, 0, 0)),
                  pl.BlockSpec((EXP_PER_GROUP, d, D_EXPERT), lambda bi, g, st: (g, 0, 0)),
                  pl.BlockSpec((EXP_PER_GROUP, D_EXPERT, d), lambda bi, g, st: (g, 0, 0))],
        out_specs=whole(d),
        scratch_shapes=[pltpu.VMEM((MOE_CHUNK, de), F32),
                        pltpu.VMEM((l + (N_GROUPS + 1) * MOE_CHUNK, d), BF16),
                        pltpu.VMEM((2 * EXP_PER_GROUP, MOE_CHUNK, D_EXPERT), F32),
                        pltpu.VMEM((EXP_PER_GROUP, MOE_CHUNK, D_EXPERT), BF16)])
    return pl.pallas_call(
        functools.partial(_experts_kernel, tb=tb), grid_spec=grid_spec,
        out_shape=jax.ShapeDtypeStruct((b, l, d), BF16),
        compiler_params=_params("parallel", "arbitrary"), name="experts",
    )(starts, h2, idx, idxt, w1, w3, w2)


def _group_starts(cnt):
    c = cnt[:, :, 0, :N_GROUPS].astype(jnp.int32)
    s = jnp.cumsum(c, axis=1)
    s = jnp.concatenate([jnp.zeros_like(s[:, :1]), s], axis=1)
    return jnp.transpose(s, (0, 2, 1)).reshape(-1)


def _final_kernel(x_ref, mod_ref, f_ref, g_ref, o_ref):
    o_ref[0] = _rms(x_ref[0] + mod_ref[0][5:6] * f_ref[0].astype(F32), g_ref[...])


def _final_call(xs, mod, f, g, lc, tm):
    b, l, d = xs.shape
    off = lc // tm
    lat = pl.BlockSpec((1, tm, d), lambda bi, j: (bi, j + off, 0))
    return pl.pallas_call(
        _final_kernel, grid=(b, (l - lc) // tm),
        in_specs=[lat, pl.BlockSpec((1, 6, d), lambda bi, j: (bi, 0, 0)), lat, _const_spec(g.shape)],
        out_specs=pl.BlockSpec((1, tm, d), lambda bi, j: (bi, j, 0)),
        out_shape=jax.ShapeDtypeStruct((b, l - lc, d), F32),
        compiler_params=_params("parallel", "parallel"), name="final_norm",
    )(xs, mod, f, g)


def _rope_tables(t_len, lc):
    half = A_ROPE // 2
    rows = t_len // GRID_W
    r = jnp.repeat(jnp.arange(rows, dtype=F32), GRID_W)
    col = jnp.tile(jnp.arange(GRID_W, dtype=F32), rows)
    inv = ROPE_THETA ** (-jnp.arange(0, half, 2, dtype=F32) / half)
    ang = jnp.concatenate([r[:, None] * inv, col[:, None] * inv], axis=-1)
    cos = jnp.concatenate([jnp.ones((lc, half), F32), jnp.cos(ang)], axis=0)
    sin = jnp.concatenate([jnp.zeros((lc, half), F32), jnp.sin(ang)], axis=0)
    l = lc + t_len
    ones = jnp.ones((l, A_NOPE), F32)
    zeros = jnp.zeros((l, A_NOPE), F32)
    tail1 = jnp.ones((l, A_PAD - A_NOPE - A_ROPE), F32)
    tail0 = jnp.zeros((l, A_PAD - A_NOPE - A_ROPE), F32)
    zh = jnp.zeros((l, half), F32)
    cos_t = jnp.concatenate([ones, cos, cos, tail1], axis=-1)
    sina_t = jnp.concatenate([zeros, zh, sin, tail0], axis=-1)
    sinb_t = jnp.concatenate([zeros, -sin, zh, tail0], axis=-1)
    return cos_t, sina_t, sinb_t


def _layer_weights(l, w_in, m_gate_b, a_qnorm, a_wuq, a_kvnorm, a_wukv, g_ws, g_bs, g_vnorm,
                   w_pa, w_pb, w_pc, w_out):
    d = w_in.shape[1]
    wi = w_in[l]
    o = 0

    def take(n):
        nonlocal o
        s = wi[:, o:o + n]
        o += n
        return s

    mq, mk, mv, mo, mg = take(M_WIDTH), take(M_WIDTH), take(M_WIDTH), take(M_WIDTH), take(4 * M_HEADS)
    aq, akv, akr = take(A_QRANK), take(A_KVRANK), take(A_ROPE)
    gu, gv = take(G_WIDTH), take(G_WIDTH)
    br = take(3 * d)
    nh = M_HEADS
    gb = m_gate_b[l]
    mgo = jnp.concatenate([mg[:, :nh], mg[:, 2 * nh:3 * nh], mg[:, nh:2 * nh], mg[:, 3 * nh:]], axis=1)
    gbo = jnp.concatenate([gb[:nh], gb[2 * nh:3 * nh], gb[nh:2 * nh], gb[3 * nh:]])
    akr_pad = jnp.concatenate([jnp.zeros((d, A_NOPE), F32), akr,
                               jnp.zeros((d, A_PAD - A_NOPE - A_ROPE), F32)], axis=1)
    wuq = a_wuq[l].reshape(A_QRANK, A_HEADS, A_NOPE + A_ROPE)
    wuq = jnp.pad(wuq, ((0, 0), (0, 0), (0, A_PAD - A_NOPE - A_ROPE))).reshape(A_QRANK, A_HEADS * A_PAD)
    wukv = a_wukv[l].reshape(A_KVRANK, A_HEADS, A_NOPE + A_VDIM)
    wuk = jnp.pad(wukv[:, :, :A_NOPE], ((0, 0), (0, 0), (0, A_PAD - A_NOPE))).reshape(A_KVRANK, A_HEADS * A_PAD)
    wuv = jnp.pad(wukv[:, :, A_NOPE:], ((0, 0), (0, 0), (0, A_PAD - A_VDIM))).reshape(A_KVRANK, A_HEADS * A_PAD)
    vone = jnp.tile(jnp.concatenate([jnp.zeros((A_VDIM,), F32), jnp.ones((A_PAD - A_VDIM,), F32)]),
                    A_HEADS).reshape(1, A_HEADS * A_PAD)
    gbs = jnp.repeat(g_bs[l].T, G_DG, axis=1)
    return dict(
        wqk=jnp.concatenate([mq, mk], 1).astype(BF16), wvo=jnp.concatenate([mv, mo], 1).astype(BF16),
        wgt=jnp.pad(mgo, ((0, 0), (0, G_PAD - 4 * nh))).astype(BF16),
        gbt=jnp.pad(gbo, (0, G_PAD - 4 * nh)).reshape(1, G_PAD),
        wa=jnp.concatenate([aq, akv, akr_pad], 1).astype(BF16),
        wg=jnp.concatenate([gu, gv], 1).astype(BF16), wbr=br.astype(BF16),
        aqn=a_qnorm[l].reshape(1, -1), akvn=a_kvnorm[l].reshape(1, -1),
        wuq=wuq.astype(BF16), wuk=wuk.astype(BF16), wuv=wuv.astype(BF16), vone=vone,
        gvn=g_vnorm[l].reshape(1, -1), gws=g_ws[l].astype(BF16), gbs=gbs,
        wpa=w_pa[l].astype(BF16), wpb=w_pb[l].astype(BF16), wpc=w_pc[l].astype(BF16),
        wout=w_out[l].astype(BF16))


def _router_weights(r_group, r_group_b, r_expert, r_expert_b):
    d = r_group.shape[0]
    pad = R_SEG - N_EXPERTS - N_GROUPS
    r = jnp.concatenate([r_expert, r_group, jnp.zeros((d, pad), F32)], axis=1)
    r3 = jnp.concatenate(list(_split3(r)) + [jnp.zeros((d, R_PAD - 3 * R_SEG), BF16)], axis=1)
    rb = jnp.concatenate([r_expert_b, r_group_b, jnp.zeros((R_PAD - N_EXPERTS - N_GROUPS,), F32)])
    return r3, rb.reshape(1, R_PAD)


def _tile(n, lc, candidates):
    for t in candidates:
        if n % t == 0 and lc % t == 0:
            return t
    raise ValueError("sequence lengths must be multiples of 128")


def kernel(x, c, ctx, c_ctx, w_ada, b_ada, norm1, norm2, final_norm, w_in, m_conv, m_gate_b, m_norm, a_qnorm, a_wuq, a_kvnorm, a_wukv, g_ws, g_bs, g_vnorm, w_pa, w_pb, w_pc, w_out, r_group, r_group_b, r_expert, r_expert_b, e_w1, e_w3, e_w2):
    b, t_len, d = x.shape
    lc = ctx.shape[1]
    l = lc + t_len
    depth = w_in.shape[0]
    tm = _tile(l, lc, (256, 128))

    xs = jnp.concatenate([ctx, x], axis=1)
    cv = jnp.concatenate([c, c_ctx[None, :]], axis=0)
    mod_all = _ada_call(cv, w_ada, b_ada).reshape(depth, b + 1, 6, d)
    tabs = _rope_tables(t_len, lc)

    f = None
    mod_prev = None
    for li in range(depth):
        last = li == depth - 1
        mod = mod_all[li]
        w = _layer_weights(li, w_in, m_gate_b, a_qnorm, a_wuq, a_kvnorm, a_wukv, g_ws, g_bs, g_vnorm,
                           w_pa, w_pb, w_pc, w_out)
        outs = _inproj_call(xs, mod, norm1[li].reshape(1, d), w, tabs, lc, tm, f, mod_prev)
        qk, vo, gi, gf, gr, q, k, v, yc, br = outs[:10]
        if f is not None:
            xs = outs[10]
        ya = _mlstm_call(qk, vo, gi, gf, gr, m_conv[li], m_norm[li].reshape(1, -1), lc)
        yb = _attn_call(q, k, v, lc, tm, not last)
        r3, rb = _router_weights(r_group[li], r_group_b[li], r_expert[li], r_expert_b[li])
        xs, h2, idx, idxt, cnt = _merge_call(xs, mod, ya, yb, yc, br, w, norm2[li].reshape(1, d),
                                             r3, rb, lc, tm, last)
        f = _experts_call(_group_starts(cnt), h2, idx, idxt, e_w1[li].astype(BF16),
                          e_w3[li].astype(BF16), e_w2[li].astype(BF16), tm)
        mod_prev = mod
    return _final_call(xs, mod_prev, f, final_norm.reshape(1, d), lc, tm)
```

```python
import functools

import jax
import jax.numpy as jnp
from jax import lax
from jax.experimental import pallas as pl
from jax.experimental.pallas import tpu as pltpu

F32 = jnp.float32
BF16 = jnp.bfloat16

EPS = 1e-6
GRID_W = 64
ROPE_THETA = 10000.0

M_HEADS = 4
M_DH = 128
M_WIDTH = M_HEADS * M_DH
M_CHUNK = 128
G_PAD = 128

A_HEADS = 8
A_NOPE = 64
A_ROPE = 32
A_VDIM = 64
A_QRANK = 384
A_KVRANK = 256
A_WIDTH = A_HEADS * A_VDIM
A_PAD = 128
A_HPS = 4
ATT_SCALE = (A_NOPE + A_ROPE) ** -0.5
LOG2E = 1.4426950408889634

G_GROUPS = 4
G_CHUNK = 128
G_WIDTH = 512
G_DG = G_WIDTH // G_GROUPS

N_GROUPS = 4
EXP_PER_GROUP = 4
N_EXPERTS = N_GROUPS * EXP_PER_GROUP
D_EXPERT = 512
R_PAD = 128
R_SEG = 32
MOE_CHUNK = 256
MOE_GATHER_BLOCKS = 5

VMEM_LIMIT = 56 * 1024 * 1024


def _dot(a, b):
    return jnp.dot(a, b, preferred_element_type=F32)


def _dot_nt(a, b):
    return lax.dot_general(a, b, (((1,), (1,)), ((), ())), preferred_element_type=F32)


def _dot_tn(a, b):
    return lax.dot_general(a, b, (((0,), (0,)), ((), ())), preferred_element_type=F32)


def _split3(x):
    hi = x.astype(BF16)
    r = x - hi.astype(F32)
    mid = r.astype(BF16)
    lo = (r - mid.astype(F32)).astype(BF16)
    return hi, mid, lo


def _sigmoid(x):
    return 1.0 / (1.0 + jnp.exp(-x))


def _silu(x):
    return x * _sigmoid(x)


def _log_sigmoid(x):
    return jnp.minimum(x, 0.0) - jnp.log1p(jnp.exp(-jnp.abs(x)))


def _gelu(x):
    return 0.5 * x * (1.0 + lax.erf(x * (2.0 ** -0.5)))


def _rms(x, g):
    return x * lax.rsqrt(jnp.mean(x * x, axis=-1, keepdims=True) + EPS) * g


def _params(*sem):
    return pltpu.CompilerParams(dimension_semantics=sem, vmem_limit_bytes=VMEM_LIMIT)


def _const_spec(shape):
    nd = len(shape)
    return pl.BlockSpec(shape, lambda *_: (0,) * nd, pipeline_mode=pl.Buffered(1))


def _ada_kernel(cv_ref, w_ref, b_ref, o_ref):
    s = _silu(cv_ref[...])
    o_ref[0] = _dot(s.astype(BF16), w_ref[0].astype(BF16)) + b_ref[0]


def _ada_call(cv, w_ada, b_ada):
    depth, d, n6 = w_ada.shape
    rows = cv.shape[0]
    tn = n6 // 4
    return pl.pallas_call(
        _ada_kernel,
        grid=(depth, n6 // tn),
        in_specs=[pl.BlockSpec((rows, d), lambda l, j: (0, 0)),
                  pl.BlockSpec((1, d, tn), lambda l, j: (l, 0, j)),
                  pl.BlockSpec((1, 1, tn), lambda l, j: (l, 0, j))],
        out_specs=pl.BlockSpec((1, rows, tn), lambda l, j: (l, 0, j)),
        out_shape=jax.ShapeDtypeStruct((depth, rows, n6), F32),
        compiler_params=_params("parallel", "parallel"),
        name="ada",
    )(cv, w_ada, b_ada.reshape(depth, 1, n6))


def _inproj_kernel(*refs, has_f):
    za_s, zg_s, zbr_s = refs[-3:]
    refs = refs[:-3]
    if has_f:
        f_ref, modp_ref, x_o = refs[0], refs[1], refs[-1]
        refs = refs[2:-1]
    (x_ref, mod_ref, n1_ref, wqk_ref, wvo_ref, wgt_ref, gbt_ref,
     wa_ref, wg_ref, wbr_ref, aqn_ref, akvn_ref, wuq_ref, wuk_ref, wuv_ref, vone_ref,
     cos_ref, sina_ref, sinb_ref, gvn_ref, gws_ref, gbs_ref,
     qk_o, vo_o, gi_o, gf_o, gr_o, q_o, k_o, v_o, yc_o, br_o) = refs
    tm = x_ref.shape[1]
    mod = mod_ref[0]
    x = x_ref[0]
    if has_f:
        x = x + modp_ref[0][5:6] * f_ref[0].astype(F32)
        x_o[0] = x
    h = _rms(x, n1_ref[...]) * (1.0 + mod[1:2]) + mod[0:1]
    hb = h.astype(BF16)

    qk_o[0] = _dot(hb, wqk_ref[...])
    vo_o[0] = _dot(hb, wvo_ref[...]).astype(BF16)
    ng = gi_o.shape[2]
    gates = _dot(hb, wgt_ref[...]) + gbt_ref[...]
    gi_o[0] = gates[:, :ng]
    gf_o[0] = pltpu.roll(gates, gates.shape[1] - ng, 1)[:, :ng]
    gr_o[0] = gates.T[:2 * ng, :]

    za_s[...] = _dot(hb, wa_ref[...])
    zg_s[...] = _dot(hb, wg_ref[...])
    zbr_s[...] = _dot(hb, wbr_ref[...])

    aqn = _rms(za_s[:, :A_QRANK], aqn_ref[...]).astype(BF16)
    akvn = _rms(za_s[:, A_QRANK:A_QRANK + A_KVRANK], akvn_ref[...]).astype(BF16)
    cos = cos_ref[...]
    sina = sina_ref[...]
    sinb = sinb_ref[...]
    half = A_ROPE // 2

    def rope(t):
        return t * cos + pltpu.roll(t, half, 1) * sina + pltpu.roll(t, A_PAD - half, 1) * sinb

    kr = rope(za_s[:, A_QRANK + A_KVRANK:])
    qp = _dot(aqn, wuq_ref[...])
    kp = _dot(akvn, wuk_ref[...])
    for hh in range(A_HEADS):
        sl = slice(hh * A_PAD, (hh + 1) * A_PAD)
        q_o[0, :, sl] = (rope(qp[:, sl]) * (ATT_SCALE * LOG2E)).astype(BF16)
        k_o[0, :, sl] = (kp[:, sl] + kr).astype(BF16)
    v_o[0] = (_dot(akvn, wuv_ref[...]) + vone_ref[...]).astype(BF16)

    gu = _gelu(zg_s[:, :G_WIDTH])
    gv = _gelu(zg_s[:, G_WIDTH:])
    gvn = gvn_ref[...]
    bias = gbs_ref[...]
    for g in range(G_GROUPS):
        sl = slice(g * G_DG, (g + 1) * G_DG)
        xn = _rms(gv[:, sl], gvn[:, sl]).astype(BF16)
        ws = gws_ref[g]
        for ci in range(tm // G_CHUNK):
            r = slice(ci * G_CHUNK, (ci + 1) * G_CHUNK)
            sg = _dot(ws, xn[r]) + bias[:, sl]
            yc_o[0, r, sl] = (gu[r, sl] * sg).astype(BF16)

    br_o[0] = _sigmoid(zbr_s[...]).astype(BF16)


def _inproj_call(xs, mod, n1, w, tabs, lc, tm, f=None, mod_prev=None):
    b, l, d = xs.shape
    nct = lc // tm
    has_f = f is not None
    tok = lambda width: pl.BlockSpec((1, tm, width), lambda bi, j: (bi, j, 0))
    modspec = pl.BlockSpec((1, 6, d), lambda bi, j: (jnp.where(j < nct, b, bi), 0, 0))
    tab = pl.BlockSpec((tm, A_PAD), lambda bi, j: (j, 0))
    consts = [n1, w["wqk"], w["wvo"], w["wgt"], w["gbt"], w["wa"], w["wg"], w["wbr"],
              w["aqn"], w["akvn"], w["wuq"], w["wuk"], w["wuv"], w["vone"]]
    consts2 = [w["gvn"], w["gws"], w["gbs"]]
    in_specs = ([tok(d), modspec] + [_const_spec(a.shape) for a in consts] + [tab, tab, tab]
                + [_const_spec(a.shape) for a in consts2])
    args = [xs, mod, *consts, *tabs, *consts2]
    ng = 4 * M_HEADS
    out_shape = [jax.ShapeDtypeStruct((b, l, 2 * M_WIDTH), F32),
                 jax.ShapeDtypeStruct((b, l, 2 * M_WIDTH), BF16),
                 jax.ShapeDtypeStruct((b, l, ng // 2), F32),
                 jax.ShapeDtypeStruct((b, l, ng // 2), F32),
                 jax.ShapeDtypeStruct((b, ng, l), F32),
                 jax.ShapeDtypeStruct((b, l, A_HEADS * A_PAD), BF16),
                 jax.ShapeDtypeStruct((b, l, A_HEADS * A_PAD), BF16),
                 jax.ShapeDtypeStruct((b, l, A_HEADS * A_PAD), BF16),
                 jax.ShapeDtypeStruct((b, l, G_WIDTH), BF16),
                 jax.ShapeDtypeStruct((b, l, 3 * d), BF16)]
    out_specs = [tok(2 * M_WIDTH), tok(2 * M_WIDTH), tok(ng // 2), tok(ng // 2),
                 pl.BlockSpec((1, ng, tm), lambda bi, j: (bi, 0, j)),
                 tok(A_HEADS * A_PAD), tok(A_HEADS * A_PAD), tok(A_HEADS * A_PAD), tok(G_WIDTH), tok(3 * d)]
    if has_f:
        in_specs = [tok(d), modspec] + in_specs
        args = [f, mod_prev] + args
        out_shape.append(jax.ShapeDtypeStruct((b, l, d), F32))
        out_specs.append(tok(d))
    return pl.pallas_call(
        functools.partial(_inproj_kernel, has_f=has_f), grid=(b, l // tm), in_specs=in_specs,
        out_specs=out_specs, out_shape=out_shape,
        scratch_shapes=[pltpu.VMEM((tm, w["wa"].shape[1]), F32), pltpu.VMEM((tm, 2 * G_WIDTH), F32),
                        pltpu.VMEM((tm, 3 * d), F32)],
        compiler_params=_params("parallel", "parallel"), name="inproj",
    )(*args)


def _scan(x, op, fill, axis, reverse):
    n = x.shape[axis]
    idx = lax.broadcasted_iota(jnp.int32, x.shape, axis)
    k = 1
    while k < n:
        if reverse:
            x = op(x, jnp.where(idx >= n - k, fill, pltpu.roll(x, n - k, axis)))
        else:
            x = op(x, jnp.where(idx < k, fill, pltpu.roll(x, k, axis)))
        k *= 2
    return x


def _mlstm_kernel(qk_ref, vo_ref, gi_ref, gf_ref, gr_ref, conv_ref, mnorm_ref, ya_ref,
                  q_s, kt_s, h_s, bc_s, ml_s, dl_s, br_s, cn_s, m_s, s_s, p_s, qcn_s, u_s, *, lc):
    l = qk_ref.shape[1]
    ch = M_CHUNK
    nc = l // ch
    ncc = lc // ch
    nh = M_HEADS
    ng = 2 * nh
    w = conv_ref[...]
    row = lax.broadcasted_iota(jnp.int32, (ch, 1), 0)

    def conv_chunk(j):
        r0 = pl.multiple_of(j * ch, ch)
        cur = qk_ref[0, pl.ds(r0, ch), :]
        prev8 = qk_ref[0, pl.ds(pl.multiple_of(jnp.maximum(r0 - 8, 0), 8), 8), :]
        next8 = qk_ref[0, pl.ds(pl.multiple_of(jnp.minimum(r0 + ch, l - 8), 8), 8), :]
        seg_start = jnp.logical_or(j == 0, j == ncc)
        seg_end = jnp.logical_or(j == ncc - 1, j == nc - 1)
        pe = jnp.where(seg_start, 0.0, prev8[7:8, :])
        ne = jnp.where(seg_end, 0.0, next8[0:1, :])
        xp = jnp.where(row == 0, pe, pltpu.roll(cur, 1, 0))
        xn = jnp.where(row == ch - 1, ne, pltpu.roll(cur, ch - 1, 0))
        y = _silu(xp * w[0:1] + cur * w[1:2] + xn * w[2:3])
        q_s[pl.ds(r0, ch), :] = (y[:, :M_WIDTH] * (M_DH ** -0.5)).astype(BF16)
        kt_s[:, pl.ds(r0, ch)] = y[:, M_WIDTH:].T.astype(BF16)

    ri = lax.broadcasted_iota(jnp.int32, (ch, ch), 0)
    ci = lax.broadcasted_iota(jnp.int32, (ch, ch), 1)
    lower = ri >= ci
    upper = ri <= ci
    ones_blk = jnp.ones((ch, M_DH), BF16)
    fwd_c = lax.broadcasted_iota(jnp.int32, (ch, ng), 1) < nh
    fwd_r = lax.broadcasted_iota(jnp.int32, (ng, ch), 0) < nh
    lane_c = lax.broadcasted_iota(jnp.int32, (ch, ng), 1)

    def local_chunk(j):
        r0 = pl.multiple_of(j * ch, ch)
        rows = pl.ds(r0, ch)
        lfc = _log_sigmoid(gf_ref[0, rows, :])
        gr = gr_ref[0, :, rows]
        lfr = _log_sigmoid(gr[ng:])
        pre_c = _scan(lfc, jnp.add, 0.0, 0, False)
        pre_r = _scan(lfr, jnp.add, 0.0, 1, False)
        b_c = jnp.where(fwd_c, pre_c, jnp.sum(lfc, axis=0, keepdims=True) + lfc - pre_c)
        b_r = jnp.where(fwd_r, pre_r, jnp.sum(lfr, axis=1, keepdims=True) + lfr - pre_r)
        g_c = gi_ref[0, rows, :] - b_c
        g_r = gr[:ng] - b_r
        cg_c = jnp.where(fwd_c, _scan(g_c, jnp.maximum, -jnp.inf, 0, False),
                         _scan(g_c, jnp.maximum, -jnp.inf, 0, True))
        bc_s[rows, :] = b_c
        br_s[:, rows] = b_r
        ml_s[rows, :] = b_c + cg_c
        dl = jnp.zeros((ch, ng), F32)
        for hh in range(nh):
            sl = slice(hh * M_DH, (hh + 1) * M_DH)
            s_s[hh] = _dot(q_s[rows, sl], kt_s[sl, rows])
        for hh in range(nh):
            s = s_s[hh]
            for d in range(2):
                jj = d * nh + hh
                wgt = jnp.exp(jnp.where(upper if d else lower, g_r[jj:jj + 1, :] - cg_c[:, jj:jj + 1], -jnp.inf))
                p_s[hh, d * ch:(d + 1) * ch] = (s * wgt).astype(BF16)
        for hh in range(nh):
            sl = slice(hh * M_DH, (hh + 1) * M_DH)
            v1 = jnp.concatenate([vo_ref[0, rows, sl], ones_blk], axis=1)
            nd2 = _dot(p_s[hh], v1)
            for d in range(2):
                jj = d * nh + hh
                nd = nd2[d * ch:(d + 1) * ch]
                h_s[d, rows, sl] = nd[:, :M_DH]
                dl = jnp.where(lane_c == jj, nd[:, M_DH:M_DH + ng], dl)
        dl_s[rows, :] = dl

    def conv_local_body(j, carry):
        conv_chunk(j + 1)
        local_chunk(j)
        return carry

    conv_chunk(jnp.int32(0))
    lax.fori_loop(0, nc - 1, conv_local_body, 0)
    local_chunk(jnp.int32(nc - 1))

    cn_s[...] = jnp.zeros_like(cn_s)
    m_s[...] = jnp.zeros_like(m_s)
    lane_r = lax.broadcasted_iota(jnp.int32, (1, ng), 1)

    def scan_issue(r0, d):
        rows = pl.ds(r0, ch)
        gr = gr_ref[0, :, rows]
        br = br_s[:, rows]
        tot = jnp.sum(_log_sigmoid(gr[ng:]), axis=1, keepdims=True)
        scal = []
        for hh in range(nh):
            fi = d * nh + hh
            sl = slice(hh * M_DH, (hh + 1) * M_DH)
            qcn_s[fi] = _dot(q_s[rows, sl], cn_s[fi].astype(BF16))
            m_old = m_s[fi][:, 0:1]
            b_e = tot[fi:fi + 1, :]
            d_end = b_e - br[fi:fi + 1, :] + gr[fi:fi + 1, :]
            m_end = jnp.max(d_end, axis=-1, keepdims=True)
            m_new = jnp.maximum(b_e + m_old, m_end)
            ktw = (kt_s[sl, rows].astype(F32) * jnp.exp(d_end - m_end)).astype(BF16)
            v1 = jnp.concatenate([vo_ref[0, rows, sl], ones_blk], axis=1)
            u_s[fi] = _dot(ktw, v1)
            scal.append((m_old, m_new, jnp.exp(b_e + m_old - m_new), jnp.exp(m_end - m_new)))
        return scal

    def scan_finish(r0, d, scal):
        rows = pl.ds(r0, ch)
        m_row = jnp.zeros((1, ng), F32)
        for hh in range(nh):
            m_row = jnp.where(lane_r == d * nh + hh, scal[hh][0], m_row)

        inter = bc_s[rows, :] + m_row
        ml = ml_s[rows, :]
        mt = jnp.maximum(inter, ml)
        a = jnp.exp(ml - mt)
        wi = jnp.exp(inter - mt)
        qn = jnp.zeros((ch, ng), F32)
        for hh in range(nh):
            fi = d * nh + hh
            qn = jnp.where(lane_c == fi, qcn_s[fi, :, M_DH:M_DH + ng], qn)
        den = a * dl_s[rows, :] + wi * qn
        rinv = 1.0 / jnp.maximum(jnp.abs(den), jnp.exp(-mt))
        c_loc = a * rinv
        c_int = wi * rinv

        for hh in range(nh):
            fi = d * nh + hh
            sl = slice(hh * M_DH, (hh + 1) * M_DH)
            h_s[d, rows, sl] = (c_loc[:, fi:fi + 1] * h_s[d, rows, sl]
                                + c_int[:, fi:fi + 1] * qcn_s[fi, :, :M_DH])
            cn_s[fi] = scal[hh][2] * cn_s[fi] + scal[hh][3] * u_s[fi]
            m_s[fi] = jnp.broadcast_to(scal[hh][1], (1, M_DH))

    def scan_body(s, carry):
        rf = pl.multiple_of(s * ch, ch)
        rb = pl.multiple_of(jnp.where(s < ncc, ncc - 1 - s, nc - 1 - s + ncc) * ch, ch)
        sf = scan_issue(rf, 0)
        sb = scan_issue(rb, 1)
        scan_finish(rf, 0, sf)
        scan_finish(rb, 1, sb)
        return carry

    lax.fori_loop(0, nc, scan_body, 0)

    mnorm = mnorm_ref[...]

    def out_body(j, carry):
        r0 = pl.multiple_of(j * ch, ch)
        hsum = h_s[0, pl.ds(r0, ch), :] + h_s[1, pl.ds(r0, ch), :]
        og = _sigmoid(vo_ref[0, pl.ds(r0, ch), M_WIDTH:].astype(F32))
        for hh in range(M_HEADS):
            sl = slice(hh * M_DH, (hh + 1) * M_DH)
            ya_ref[0, pl.ds(r0, ch), sl] = (_rms(hsum[:, sl], mnorm[:, sl]) * og[:, sl]).astype(BF16)
        return carry

    lax.fori_loop(0, nc, out_body, 0)


def _mlstm_call(qk, vo, gi, gf, gr, conv, mnorm, lc):
    b, l, _ = qk.shape
    ng = 2 * M_HEADS
    return pl.pallas_call(
        functools.partial(_mlstm_kernel, lc=lc),
        grid=(b,),
        in_specs=[pl.BlockSpec((1, l, 2 * M_WIDTH), lambda bi: (bi, 0, 0), pipeline_mode=pl.Buffered(1)),
                  pl.BlockSpec((1, l, 2 * M_WIDTH), lambda bi: (bi, 0, 0)),
                  pl.BlockSpec((1, l, ng), lambda bi: (bi, 0, 0)),
                  pl.BlockSpec((1, l, ng), lambda bi: (bi, 0, 0)),
                  pl.BlockSpec((1, 2 * ng, l), lambda bi: (bi, 0, 0)),
                  _const_spec(conv.shape), _const_spec(mnorm.shape)],
        out_specs=pl.BlockSpec((1, l, M_WIDTH), lambda bi: (bi, 0, 0)),
        out_shape=jax.ShapeDtypeStruct((b, l, M_WIDTH), BF16),
        scratch_shapes=[pltpu.VMEM((l, M_WIDTH), BF16),
                        pltpu.VMEM((M_WIDTH, l), BF16),
                        pltpu.VMEM((2, l, M_WIDTH), F32),
                        pltpu.VMEM((l, ng), F32),
                        pltpu.VMEM((l, ng), F32),
                        pltpu.VMEM((l, ng), F32),
                        pltpu.VMEM((ng, l), F32),
                        pltpu.VMEM((2 * M_HEADS, M_DH, 2 * M_DH), F32),
                        pltpu.VMEM((2 * M_HEADS, 1, M_DH), F32),
                        pltpu.VMEM((M_HEADS, M_CHUNK, M_CHUNK), F32),
                        pltpu.VMEM((M_HEADS, 2 * M_CHUNK, M_CHUNK), BF16),
                        pltpu.VMEM((2 * M_HEADS, M_CHUNK, 2 * M_DH), F32),
                        pltpu.VMEM((2 * M_HEADS, M_DH, 2 * M_DH), F32)],
        compiler_params=_params("parallel"), name="mlstm",
    )(qk, vo, gi, gf, gr, conv, mnorm)


def _attn_kernel(q_ref, k_ref, v_ref, o_ref, s_s, p_s, *, lc, ctx_out):
    tq = q_ref.shape[1]
    l = k_ref.shape[1]
    qi = pl.program_id(2)
    nct = lc // tq
    lane = lax.broadcasted_iota(jnp.int32, (tq, 2 * A_VDIM), 1)

    def run(klen):
        outs = []
        for hh in range(A_HPS):
            sl = slice(hh * A_PAD, (hh + 1) * A_PAD)
            s_s[hh, :, :klen] = _dot_nt(q_ref[0, :, sl], k_ref[0, :klen, sl])
        row_max = [jnp.max(s_s[hh, :, :klen], axis=-1, keepdims=True) for hh in range(A_HPS)]
        for hh in range(A_HPS):
            p_s[hh, :, :klen] = jnp.exp2((s_s[hh, :, :klen] - row_max[hh]).astype(BF16))
        for hh in range(A_HPS):
            sl = slice(hh * A_PAD, (hh + 1) * A_PAD)
            nd = _dot(p_s[hh, :, :klen], v_ref[0, :klen, sl])
            outs.append(nd / pltpu.roll(nd, A_VDIM, 1))
        for pp in range(A_HPS // 2):
            o_ref[0, :, pp * A_PAD:(pp + 1) * A_PAD] = jnp.where(
                lane < A_VDIM, outs[2 * pp], pltpu.roll(outs[2 * pp + 1], A_VDIM, 1)).astype(BF16)

    @pl.when(qi >= nct)
    def _():
        run(l)

    @pl.when(qi < nct)
    def _():
        if ctx_out:
            run(lc)
        else:
            o_ref[...] = jnp.zeros_like(o_ref)


def _attn_call(q, k, v, lc, tq, ctx_out):
    b, l, _ = q.shape
    return pl.pallas_call(
        functools.partial(_attn_kernel, lc=lc, ctx_out=ctx_out),
        grid=(b, A_HEADS // A_HPS, l // tq),
        in_specs=[pl.BlockSpec((1, tq, A_HPS * A_PAD), lambda bi, p, qi: (bi, qi, p)),
                  pl.BlockSpec((1, l, A_HPS * A_PAD), lambda bi, p, qi: (bi, 0, p)),
                  pl.BlockSpec((1, l, A_HPS * A_PAD), lambda bi, p, qi: (bi, 0, p))],
        out_specs=pl.BlockSpec((1, tq, A_HPS * A_VDIM), lambda bi, p, qi: (bi, qi, p)),
        out_shape=jax.ShapeDtypeStruct((b, l, A_WIDTH), BF16),
        scratch_shapes=[pltpu.VMEM((A_HPS, tq, l), F32), pltpu.VMEM((A_HPS, tq, l), BF16)],
        compiler_params=_params("parallel", "parallel", "arbitrary"), name="attn",
    )(q, k, v)


def _merge_kernel(x_ref, mod_ref, modc_ref, ya_ref, yb_ref, yc_ref, br_ref, wpa_ref, wpb_ref, wpc_ref, wout_ref,
                  n2_ref, r_ref, rb_ref, o_ref, h2_o, idx_o, idxt_o, cnt_o, y_s, *, tm, nct, skip_ctx):
    d = x_ref.shape[2]
    nsub = x_ref.shape[1] // tm
    modb = mod_ref[0]
    modc = modc_ref[0]
    tiles = [slice(s * tm, (s + 1) * tm) for s in range(nsub)]
    ctx = [pl.program_id(1) * nsub + s < nct for s in range(nsub)]
    mods = [jnp.where(c, modc, modb) for c in ctx]
    for s, rows in enumerate(tiles):
        br = br_ref[0, rows, :]
        y = (br[:, :d].astype(F32) * _dot(ya_ref[0, rows, :], wpa_ref[...])
             + br[:, d:2 * d].astype(F32) * _dot(yb_ref[0, rows, :], wpb_ref[...])
             + br[:, 2 * d:].astype(F32) * _dot(yc_ref[0, rows, :], wpc_ref[...]))
        y_s[s] = y.astype(BF16)
    for s, rows in enumerate(tiles):
        o_ref[0, rows, :] = x_ref[0, rows, :] + mods[s][2:3] * _dot(y_s[s], wout_ref[...])
    for s, rows in enumerate(tiles):
        logits = _route_logits(o_ref[0, rows, :], mods[s], n2_ref, r_ref, rb_ref, h2_o, rows)
        left_out = ctx[s] if skip_ctx else None
        _route_assign(logits, left_out, d, h2_o, idx_o, idxt_o, cnt_o, rows, s)


def _merge_call(xs, mod, ya, yb, yc, br, w, n2, r3, rb, lc, tm, skip_ctx):
    b, l, d = xs.shape
    nct = lc // tm
    nsub = next(n for n in (3, 2, 1) if l % (n * tm) == 0)
    tg = nsub * tm
    tok = lambda width: pl.BlockSpec((1, tg, width), lambda bi, j: (bi, j, 0))
    consts = [w["wpa"], w["wpb"], w["wpc"], w["wout"], n2, r3, rb]
    return pl.pallas_call(
        functools.partial(_merge_kernel, tm=tm, nct=nct, skip_ctx=skip_ctx), grid=(b, l // tg),
        in_specs=[tok(d), pl.BlockSpec((1, 6, d), lambda bi, j: (bi, 0, 0)),
                  pl.BlockSpec((1, 6, d), lambda bi, j: (b, 0, 0)),
                  tok(M_WIDTH), tok(A_WIDTH), tok(G_WIDTH), tok(3 * d)] + [_const_spec(a.shape) for a in consts],
        out_specs=[tok(d), tok(d + R_PAD), tok(8), pl.BlockSpec((1, 8, tg), lambda bi, j: (bi, 0, j)),
                   pl.BlockSpec((1, nsub, 8, R_PAD), lambda bi, j: (bi, j, 0, 0))],
        out_shape=[jax.ShapeDtypeStruct((b, l, d), F32),
                   jax.ShapeDtypeStruct((b, l, d + R_PAD), BF16),
                   jax.ShapeDtypeStruct((b, l, 8), jnp.int32), jax.ShapeDtypeStruct((b, 8, l), jnp.int32),
                   jax.ShapeDtypeStruct((b, l // tm, 8, R_PAD), F32)],
        scratch_shapes=[pltpu.VMEM((nsub, tm, d), BF16)],
        compiler_params=_params("parallel", "parallel"), name="merge",
    )(xs, mod, mod, ya, yb, yc, br, *consts)


def _route_logits(x, mod, n2_ref, r_ref, rb_ref, h2_o, rows):
    d = x.shape[1]
    h2 = _rms(x, n2_ref[...]) * (1.0 + mod[4:5]) + mod[3:4]
    h2_o[0, rows, :d] = h2.astype(BF16)
    r = r_ref[...]
    pp = sum(_dot(piece, r) for piece in _split3(h2)[:2])
    return pp + pltpu.roll(pp, R_PAD - R_SEG, 1) + pltpu.roll(pp, R_PAD - 2 * R_SEG, 1) + rb_ref[...]


def _route_assign(logits, left_out, d, h2_o, idx_o, idxt_o, cnt_o, rows, s):
    tm = logits.shape[0]
    el = logits[:, :N_EXPERTS]
    gl = logits[:, N_EXPERTS:N_EXPERTS + N_GROUPS]
    big = 1e9

    lane_g = lax.broadcasted_iota(jnp.int32, (tm, N_GROUPS), 1).astype(F32)
    gmax = jnp.max(gl, axis=-1, keepdims=True)
    g_sel = jnp.min(jnp.where(gl == gmax, lane_g, big), axis=-1, keepdims=True)
    g_prob = 1.0 / jnp.sum(jnp.exp(gl - gmax), axis=-1, keepdims=True)

    lane_i = lax.broadcasted_iota(jnp.int32, (tm, N_EXPERTS), 1)
    lane_e = lane_i.astype(F32)
    lane_grp = (lane_i // EXP_PER_GROUP).astype(F32)
    v1 = jnp.where(lane_grp == g_sel, el, -jnp.inf)
    t1 = jnp.max(v1, axis=-1, keepdims=True)
    i1 = jnp.min(jnp.where(v1 == t1, lane_e, big), axis=-1, keepdims=True)
    v2 = jnp.where(lane_e == i1, -jnp.inf, v1)
    t2 = jnp.max(v2, axis=-1, keepdims=True)
    i2 = jnp.min(jnp.where(v2 == t2, lane_e, big), axis=-1, keepdims=True)
    e21 = jnp.exp(t2 - t1)
    w1 = 1.0 / (1.0 + e21)
    w2 = e21 * w1
    comb = (jnp.where(lane_e == i1, w1, 0.0) + jnp.where(lane_e == i2, w2, 0.0)) * g_prob
    tail = jnp.zeros((tm, R_PAD - 3 * N_EXPERTS), BF16)
    h2_o[0, rows, d:] = jnp.concatenate(list(_split3(comb)) + [tail], axis=1)

    if left_out is not None:
        g_sel = jnp.where(left_out, -1.0, g_sel)
    lane_p = lax.broadcasted_iota(jnp.int32, (tm, R_PAD), 1)
    onehot = jnp.where(lane_p.astype(F32) == g_sel, 1.0, 0.0)
    ri = lax.broadcasted_iota(jnp.int32, (tm, tm), 0)
    ci = lax.broadcasted_iota(jnp.int32, (tm, tm), 1)
    before = jnp.where(ri > ci, 1.0, 0.0).astype(BF16)
    rank = jnp.sum(_dot(before, onehot.astype(BF16)) * onehot, axis=-1, keepdims=True)
    cnt_o[0, s] = jnp.broadcast_to(jnp.sum(onehot, axis=0, keepdims=True), (8, R_PAD))
    fields = jnp.where(lane_p == 0, g_sel, jnp.where(lane_p == 1, rank, 0.0))
    idx_o[0, rows, :] = fields[:, :8].astype(jnp.int32)
    idxt_o[0, :, rows] = fields.T[:8, :].astype(jnp.int32)


def _experts_kernel(st_ref, h2_ref, idx_ref, idxt_ref, w1_ref, w3_ref, w2_ref, o_ref, hs_s, ys_s, a_s, hid_s, *, tb):
    l = h2_ref.shape[1]
    d = o_ref.shape[2]
    nblk = l // tb
    ch = MOE_CHUNK
    bi = pl.program_id(0)
    g = pl.program_id(1)

    @pl.when(g == 0)
    def _():
        ys_s[...] = jnp.zeros_like(ys_s)

    def group_base(gg):
        return (bi * N_GROUPS + gg) * (nblk + 1)

    def group_offset(upto):
        off = 0
        for gg in range(N_GROUPS - 1):
            padded = ((st_ref[group_base(gg) + nblk] + ch - 1) // ch) * ch
            off = off + jnp.where(gg < upto, padded, 0)
        return off

    base = group_base(g)
    cnt = st_ref[base + nblk]
    goff = group_offset(g)
    gw = min(MOE_GATHER_BLOCKS, nblk)
    nwin = -(-nblk // gw)

    def chunk(lo, ch):
        lane_e = lax.broadcasted_iota(jnp.int32, (ch, N_EXPERTS), 1)
        sub_iota = lax.broadcasted_iota(jnp.int32, (ch, gw * tb), 0)
        lane_blk = lax.broadcasted_iota(jnp.int32, (1, gw * tb), 1) // tb
        k0 = 0
        for k in range(nblk):
            k0 = k0 + jnp.where(st_ref[base + k + 1] <= lo, 1, 0)

        def window(j):
            want = k0 + j * gw
            kj = jnp.minimum(want, nblk - gw)
            tok0 = pl.multiple_of(kj * tb, tb)
            it = idxt_ref[0, :, pl.ds(tok0, gw * tb)]
            offs = jnp.concatenate([jnp.full((1, tb), st_ref[base + kj + i] - lo, jnp.int32) for i in range(gw)],
                                   axis=1)
            mine = jnp.where(lane_blk >= want - kj, it[0:1], -1)
            pos = jnp.where(mine == g, it[1:2] + offs, -1)
            p = jnp.where(sub_iota == pos, 1.0, 0.0).astype(BF16)
            return _dot(p, h2_ref[0, pl.ds(tok0, gw * tb), :])

        hs_s[:ch] = window(0)
        for j in range(1, nwin):
            first = k0 + j * gw

            @pl.when(jnp.logical_and(first < nblk, st_ref[base + jnp.minimum(first, nblk)] < lo + ch))
            def _(j=j):
                hs_s[:ch] += window(j)

        hsb = hs_s[:ch, :d].astype(BF16)
        cs = (hs_s[:ch, d:d + N_EXPERTS] + hs_s[:ch, d + N_EXPERTS:d + 2 * N_EXPERTS]
              + hs_s[:ch, d + 2 * N_EXPERTS:d + 3 * N_EXPERTS])
        for e in range(EXP_PER_GROUP):
            a_s[2 * e, :ch] = _dot(hsb, w1_ref[e])
            a_s[2 * e + 1, :ch] = _dot(hsb, w3_ref[e])
        for e in range(EXP_PER_GROUP):
            ce = jnp.sum(jnp.where(lane_e == g * EXP_PER_GROUP + e, cs, 0.0), axis=-1, keepdims=True)
            hid_s[e, :ch] = (_silu(a_s[2 * e, :ch]) * a_s[2 * e + 1, :ch] * ce).astype(BF16)
        y = jnp.zeros((ch, d), F32)
        for e in range(EXP_PER_GROUP):
            y = y + _dot(hid_s[e, :ch], w2_ref[e])
        ys_s[pl.ds(pl.multiple_of(goff + lo, ch), ch), :] = y.astype(BF16)

    half = ch // 2
    nfull = cnt // ch
    rem = cnt - nfull * ch
    nloop = nfull + jnp.where(rem > half, 1, 0)

    def chunk_body(c, carry):
        chunk(c * ch, ch)
        return carry

    lax.fori_loop(0, nloop, chunk_body, 0)

    @pl.when(jnp.logical_and(rem > 0, rem <= half))
    def _():
        chunk(nfull * ch, half)

    @pl.when(g == N_GROUPS - 1)
    def _():
        lane_w = lax.broadcasted_iota(jnp.int32, (tb, ch), 1)
        goffs = [group_offset(gg) for gg in range(N_GROUPS)]

        def window(k, gg, shift):
            s_k = st_ref[group_base(gg) + k]
            win = (s_k // half) * half
            ic = idx_ref[0, k * tb:(k + 1) * tb, :]
            pos = jnp.where(ic[:, 0:1] == gg, ic[:, 1:2] + (s_k - win - shift), -1)
            q = jnp.where(lane_w == pos, 1.0, 0.0).astype(BF16)
            start = pl.multiple_of(goffs[gg] + win + shift, half)
            return _dot(q, ys_s[pl.ds(start, ch), :])

        for k in range(nblk):
            acc = window(k, 0, 0)
            for gg in range(1, N_GROUPS):
                acc = acc + window(k, gg, 0)
            o_ref[0, k * tb:(k + 1) * tb, :] = acc.astype(BF16)
        for k in range(nblk):
            rows = slice(k * tb, (k + 1) * tb)
            for gg in range(N_GROUPS):
                s_k = st_ref[group_base(gg) + k]
                e_k = st_ref[group_base(gg) + k + 1]

                @pl.when(e_k - (s_k // half) * half > ch)
                def _(k=k, gg=gg, rows=rows):
                    o_ref[0, rows, :] = (o_ref[0, rows, :].astype(F32) + window(k, gg, ch)).astype(BF16)


def _experts_call(starts, h2, idx, idxt, w1, w3, w2, tb):
    b, l, de = h2.shape
    d = de - R_PAD
    whole = lambda width: pl.BlockSpec((1, l, width), lambda bi, g, st: (bi, 0, 0))
    grid_spec = pltpu.PrefetchScalarGridSpec(
        num_scalar_prefetch=1, grid=(b, N_GROUPS),
        in_specs=[pl.BlockSpec((1, l, de), lambda bi, g, st: (bi, 0, 0), pipeline_mode=pl.Buffered(1)),
                  whole(8), pl.BlockSpec((1, 8, l), lambda bi, g, st: (bi, 0, 0)),
                  pl.BlockSpec((EXP_PER_GROUP, d, D_EXPERT), lambda bi, g, st: (g, 0, 0)),
                  pl.BlockSpec((EXP_PER_GROUP, d, D_EXPERT), lambda bi, g, st: (g, 0, 0)),
                  pl.BlockSpec((EXP_PER_GROUP, D_EXPERT, d), lambda bi, g, st: (g, 0, 0))],
        out_specs=whole(d),
        scratch_shapes=[pltpu.VMEM((MOE_CHUNK, de), F32),
                        pltpu.VMEM((l + (N_GROUPS + 1) * MOE_CHUNK, d), BF16),
                        pltpu.VMEM((2 * EXP_PER_GROUP, MOE_CHUNK, D_EXPERT), F32),
                        pltpu.VMEM((EXP_PER_GROUP, MOE_CHUNK, D_EXPERT), BF16)])
    return pl.pallas_call(
        functools.partial(_experts_kernel, tb=tb), grid_spec=grid_spec,
        out_shape=jax.ShapeDtypeStruct((b, l, d), BF16),
        compiler_params=_params("parallel", "arbitrary"), name="experts",
    )(starts, h2, idx, idxt, w1, w3, w2)


def _group_starts(cnt):
    c = cnt[:, :, 0, :N_GROUPS].astype(jnp.int32)
    s = jnp.cumsum(c, axis=1)
    s = jnp.concatenate([jnp.zeros_like(s[:, :1]), s], axis=1)
    return jnp.transpose(s, (0, 2, 1)).reshape(-1)


def _final_kernel(x_ref, mod_ref, f_ref, g_ref, o_ref):
    o_ref[0] = _rms(x_ref[0] + mod_ref[0][5:6] * f_ref[0].astype(F32), g_ref[...])


def _final_call(xs, mod, f, g, lc, tm):
    b, l, d = xs.shape
    off = lc // tm
    lat = pl.BlockSpec((1, tm, d), lambda bi, j: (bi, j + off, 0))
    return pl.pallas_call(
        _final_kernel, grid=(b, (l - lc) // tm),
        in_specs=[lat, pl.BlockSpec((1, 6, d), lambda bi, j: (bi, 0, 0)), lat, _const_spec(g.shape)],
        out_specs=pl.BlockSpec((1, tm, d), lambda bi, j: (bi, j, 0)),
        out_shape=jax.ShapeDtypeStruct((b, l - lc, d), F32),
        compiler_params=_params("parallel", "parallel"), name="final_norm",
    )(xs, mod, f, g)


def _rope_tables(t_len, lc):
    half = A_ROPE // 2
    rows = t_len // GRID_W
    r = jnp.repeat(jnp.arange(rows, dtype=F32), GRID_W)
    col = jnp.tile(jnp.arange(GRID_W, dtype=F32), rows)
    inv = ROPE_THETA ** (-jnp.arange(0, half, 2, dtype=F32) / half)
    ang = jnp.concatenate([r[:, None] * inv, col[:, None] * inv], axis=-1)
    cos = jnp.concatenate([jnp.ones((lc, half), F32), jnp.cos(ang)], axis=0)
    sin = jnp.concatenate([jnp.zeros((lc, half), F32), jnp.sin(ang)], axis=0)
    l = lc + t_len
    ones = jnp.ones((l, A_NOPE), F32)
    zeros = jnp.zeros((l, A_NOPE), F32)
    tail1 = jnp.ones((l, A_PAD - A_NOPE - A_ROPE), F32)
    tail0 = jnp.zeros((l, A_PAD - A_NOPE - A_ROPE), F32)
    zh = jnp.zeros((l, half), F32)
    cos_t = jnp.concatenate([ones, cos, cos, tail1], axis=-1)
    sina_t = jnp.concatenate([zeros, zh, sin, tail0], axis=-1)
    sinb_t = jnp.concatenate([zeros, -sin, zh, tail0], axis=-1)
    return cos_t, sina_t, sinb_t


def _layer_weights(l, w_in, m_gate_b, a_qnorm, a_wuq, a_kvnorm, a_wukv, g_ws, g_bs, g_vnorm,
                   w_pa, w_pb, w_pc, w_out):
    d = w_in.shape[1]
    wi = w_in[l]
    o = 0

    def take(n):
        nonlocal o
        s = wi[:, o:o + n]
        o += n
        return s

    mq, mk, mv, mo, mg = take(M_WIDTH), take(M_WIDTH), take(M_WIDTH), take(M_WIDTH), take(4 * M_HEADS)
    aq, akv, akr = take(A_QRANK), take(A_KVRANK), take(A_ROPE)
    gu, gv = take(G_WIDTH), take(G_WIDTH)
    br = take(3 * d)
    nh = M_HEADS
    gb = m_gate_b[l]
    mgo = jnp.concatenate([mg[:, :nh], mg[:, 2 * nh:3 * nh], mg[:, nh:2 * nh], mg[:, 3 * nh:]], axis=1)
    gbo = jnp.concatenate([gb[:nh], gb[2 * nh:3 * nh], gb[nh:2 * nh], gb[3 * nh:]])
    akr_pad = jnp.concatenate([jnp.zeros((d, A_NOPE), F32), akr,
                               jnp.zeros((d, A_PAD - A_NOPE - A_ROPE), F32)], axis=1)
    wuq = a_wuq[l].reshape(A_QRANK, A_HEADS, A_NOPE + A_ROPE)
    wuq = jnp.pad(wuq, ((0, 0), (0, 0), (0, A_PAD - A_NOPE - A_ROPE))).reshape(A_QRANK, A_HEADS * A_PAD)
    wukv = a_wukv[l].reshape(A_KVRANK, A_HEADS, A_NOPE + A_VDIM)
    wuk = jnp.pad(wukv[:, :, :A_NOPE], ((0, 0), (0, 0), (0, A_PAD - A_NOPE))).reshape(A_KVRANK, A_HEADS * A_PAD)
    wuv = jnp.pad(wukv[:, :, A_NOPE:], ((0, 0), (0, 0), (0, A_PAD - A_VDIM))).reshape(A_KVRANK, A_HEADS * A_PAD)
    vone = jnp.tile(jnp.concatenate([jnp.zeros((A_VDIM,), F32), jnp.ones((A_PAD - A_VDIM,), F32)]),
                    A_HEADS).reshape(1, A_HEADS * A_PAD)
    gbs = jnp.repeat(g_bs[l].T, G_DG, axis=1)
    return dict(
        wqk=jnp.concatenate([mq, mk], 1).astype(BF16), wvo=jnp.concatenate([mv, mo], 1).astype(BF16),
        wgt=jnp.pad(mgo, ((0, 0), (0, G_PAD - 4 * nh))).astype(BF16),
        gbt=jnp.pad(gbo, (0, G_PAD - 4 * nh)).reshape(1, G_PAD),
        wa=jnp.concatenate([aq, akv, akr_pad], 1).astype(BF16),
        wg=jnp.concatenate([gu, gv], 1).astype(BF16), wbr=br.astype(BF16),
        aqn=a_qnorm[l].reshape(1, -1), akvn=a_kvnorm[l].reshape(1, -1),
        wuq=wuq.astype(BF16), wuk=wuk.astype(BF16), wuv=wuv.astype(BF16), vone=vone,
        gvn=g_vnorm[l].reshape(1, -1), gws=g_ws[l].astype(BF16), gbs=gbs,
        wpa=w_pa[l].astype(BF16), wpb=w_pb[l].astype(BF16), wpc=w_pc[l].astype(BF16),
        wout=w_out[l].astype(BF16))


def _router_weights(r_group, r_group_b, r_expert, r_expert_b):
    d = r_group.shape[0]
    pad = R_SEG - N_EXPERTS - N_GROUPS
    r = jnp.concatenate([r_expert, r_group, jnp.zeros((d, pad), F32)], axis=1)
    r3 = jnp.concatenate(list(_split3(r)) + [jnp.zeros((d, R_PAD - 3 * R_SEG), BF16)], axis=1)
    rb = jnp.concatenate([r_expert_b, r_group_b, jnp.zeros((R_PAD - N_EXPERTS - N_GROUPS,), F32)])
    return r3, rb.reshape(1, R_PAD)


def _tile(n, lc, candidates):
    for t in candidates:
        if n % t == 0 and lc % t == 0:
            return t
    raise ValueError("sequence lengths must be multiples of 128")


def kernel(x, c, ctx, c_ctx, w_ada, b_ada, norm1, norm2, final_norm, w_in, m_conv, m_gate_b, m_norm, a_qnorm, a_wuq, a_kvnorm, a_wukv, g_ws, g_bs, g_vnorm, w_pa, w_pb, w_pc, w_out, r_group, r_group_b, r_expert, r_expert_b, e_w1, e_w3, e_w2):
    b, t_len, d = x.shape
    lc = ctx.shape[1]
    l = lc + t_len
    depth = w_in.shape[0]
    tm = _tile(l, lc, (256, 128))

    xs = jnp.concatenate([ctx, x], axis=1)
    cv = jnp.concatenate([c, c_ctx[None, :]], axis=0)
    mod_all = _ada_call(cv, w_ada, b_ada).reshape(depth, b + 1, 6, d)
    tabs = _rope_tables(t_len, lc)

    w_in_b = w_in.astype(BF16)
    f = None
    mod_prev = None
    for li in range(depth):
        last = li == depth - 1
        mod = mod_all[li]
        w = _layer_weights(li, w_in_b, m_gate_b, a_qnorm, a_wuq, a_kvnorm, a_wukv, g_ws, g_bs, g_vnorm,
                           w_pa, w_pb, w_pc, w_out)
        outs = _inproj_call(xs, mod, norm1[li].reshape(1, d), w, tabs, lc, tm, f, mod_prev)
        qk, vo, gi, gf, gr, q, k, v, yc, br = outs[:10]
        if f is not None:
            xs = outs[10]
        ya = _mlstm_call(qk, vo, gi, gf, gr, m_conv[li], m_norm[li].reshape(1, -1), lc)
        yb = _attn_call(q, k, v, lc, tm, not last)
        r3, rb = _router_weights(r_group[li], r_group_b[li], r_expert[li], r_expert_b[li])
        xs, h2, idx, idxt, cnt = _merge_call(xs, mod, ya, yb, yc, br, w, norm2[li].reshape(1, d),
                                             r3, rb, lc, tm, last)
        f = _experts_call(_group_starts(cnt), h2, idx, idxt, e_w1[li].astype(BF16),
                          e_w3[li].astype(BF16), e_w2[li].astype(BF16), tm)
        mod_prev = mod
    return _final_call(xs, mod_prev, f, final_norm.reshape(1, d), lc, tm)
```

```python
import functools

import jax
import jax.numpy as jnp
from jax import lax
from jax.experimental import pallas as pl
from jax.experimental.pallas import tpu as pltpu

F32 = jnp.float32
BF16 = jnp.bfloat16

EPS = 1e-6
GRID_W = 64
ROPE_THETA = 10000.0

M_HEADS = 4
M_DH = 128
M_WIDTH = M_HEADS * M_DH
M_CHUNK = 128
G_PAD = 128

A_HEADS = 8
A_NOPE = 64
A_ROPE = 32
A_VDIM = 64
A_QRANK = 384
A_KVRANK = 256
A_WIDTH = A_HEADS * A_VDIM
A_PAD = 128
A_HPS = 8
ATT_SCALE = (A_NOPE + A_ROPE) ** -0.5
LOG2E = 1.4426950408889634

G_GROUPS = 4
G_CHUNK = 128
G_WIDTH = 512
G_DG = G_WIDTH // G_GROUPS

N_GROUPS = 4
EXP_PER_GROUP = 4
N_EXPERTS = N_GROUPS * EXP_PER_GROUP
D_EXPERT = 512
R_PAD = 128
R_SEG = 32
MOE_CHUNK = 256
MOE_GATHER_BLOCKS = 5

VMEM_LIMIT = 56 * 1024 * 1024


def _dot(a, b):
    return jnp.dot(a, b, preferred_element_type=F32)


def _dot_nt(a, b):
    return lax.dot_general(a, b, (((1,), (1,)), ((), ())), preferred_element_type=F32)


def _dot_tn(a, b):
    return lax.dot_general(a, b, (((0,), (0,)), ((), ())), preferred_element_type=F32)


def _split3(x):
    hi = x.astype(BF16)
    r = x - hi.astype(F32)
    mid = r.astype(BF16)
    lo = (r - mid.astype(F32)).astype(BF16)
    return hi, mid, lo


def _sigmoid(x):
    return 1.0 / (1.0 + jnp.exp(-x))


def _silu(x):
    return x * _sigmoid(x)


def _log_sigmoid(x):
    return jnp.minimum(x, 0.0) - jnp.log1p(jnp.exp(-jnp.abs(x)))


def _gelu(x):
    return 0.5 * x * (1.0 + lax.erf(x * (2.0 ** -0.5)))


def _rms(x, g):
    return x * lax.rsqrt(jnp.mean(x * x, axis=-1, keepdims=True) + EPS) * g


def _params(*sem):
    return pltpu.CompilerParams(dimension_semantics=sem, vmem_limit_bytes=VMEM_LIMIT)


def _const_spec(shape):
    nd = len(shape)
    return pl.BlockSpec(shape, lambda *_: (0,) * nd, pipeline_mode=pl.Buffered(1))


def _ada_kernel(cv_ref, w_ref, b_ref, o_ref):
    s = _silu(cv_ref[...])
    o_ref[0] = _dot(s.astype(BF16), w_ref[0].astype(BF16)) + b_ref[0]


def _ada_call(cv, w_ada, b_ada):
    depth, d, n6 = w_ada.shape
    rows = cv.shape[0]
    tn = n6 // 4
    return pl.pallas_call(
        _ada_kernel,
        grid=(depth, n6 // tn),
        in_specs=[pl.BlockSpec((rows, d), lambda l, j: (0, 0)),
                  pl.BlockSpec((1, d, tn), lambda l, j: (l, 0, j)),
                  pl.BlockSpec((1, 1, tn), lambda l, j: (l, 0, j))],
        out_specs=pl.BlockSpec((1, rows, tn), lambda l, j: (l, 0, j)),
        out_shape=jax.ShapeDtypeStruct((depth, rows, n6), F32),
        compiler_params=_params("parallel", "parallel"),
        name="ada",
    )(cv, w_ada, b_ada.reshape(depth, 1, n6))


def _inproj_kernel(*refs, has_f):
    za_s, zg_s, zbr_s = refs[-3:]
    refs = refs[:-3]
    if has_f:
        f_ref, modp_ref, x_o = refs[0], refs[1], refs[-1]
        refs = refs[2:-1]
    (x_ref, mod_ref, n1_ref, wqk_ref, wvo_ref, wgt_ref, gbt_ref,
     wa_ref, wg_ref, wbr_ref, aqn_ref, akvn_ref, wuq_ref, wuk_ref, wuv_ref, vone_ref,
     cos_ref, sina_ref, sinb_ref, gvn_ref, gws_ref, gbs_ref,
     qk_o, vo_o, gi_o, gf_o, gr_o, q_o, k_o, v_o, yc_o, br_o) = refs
    tm = x_ref.shape[1]
    mod = mod_ref[0]
    x = x_ref[0]
    if has_f:
        x = x + modp_ref[0][5:6] * f_ref[0].astype(F32)
        x_o[0] = x
    h = _rms(x, n1_ref[...]) * (1.0 + mod[1:2]) + mod[0:1]
    hb = h.astype(BF16)

    qk_o[0] = _dot(hb, wqk_ref[...])
    vo_o[0] = _dot(hb, wvo_ref[...]).astype(BF16)
    ng = gi_o.shape[2]
    gates = _dot(hb, wgt_ref[...]) + gbt_ref[...]
    gi_o[0] = gates[:, :ng]
    gf_o[0] = pltpu.roll(gates, gates.shape[1] - ng, 1)[:, :ng]
    gr_o[0] = gates.T[:2 * ng, :]

    za_s[...] = _dot(hb, wa_ref[...])
    zg_s[...] = _dot(hb, wg_ref[...])
    zbr_s[...] = _dot(hb, wbr_ref[...])

    aqn = _rms(za_s[:, :A_QRANK], aqn_ref[...]).astype(BF16)
    akvn = _rms(za_s[:, A_QRANK:A_QRANK + A_KVRANK], akvn_ref[...]).astype(BF16)
    cos = cos_ref[...]
    sina = sina_ref[...]
    sinb = sinb_ref[...]
    half = A_ROPE // 2

    def rope(t):
        return t * cos + pltpu.roll(t, half, 1) * sina + pltpu.roll(t, A_PAD - half, 1) * sinb

    kr = rope(za_s[:, A_QRANK + A_KVRANK:])
    qp = _dot(aqn, wuq_ref[...])
    kp = _dot(akvn, wuk_ref[...])
    for hh in range(A_HEADS):
        sl = slice(hh * A_PAD, (hh + 1) * A_PAD)
        q_o[0, :, sl] = (rope(qp[:, sl]) * (ATT_SCALE * LOG2E)).astype(BF16)
        k_o[0, :, sl] = (kp[:, sl] + kr).astype(BF16)
    v_o[0] = (_dot(akvn, wuv_ref[...]) + vone_ref[...]).astype(BF16)

    gu = _gelu(zg_s[:, :G_WIDTH])
    gv = _gelu(zg_s[:, G_WIDTH:])
    gvn = gvn_ref[...]
    bias = gbs_ref[...]
    for g in range(G_GROUPS):
        sl = slice(g * G_DG, (g + 1) * G_DG)
        xn = _rms(gv[:, sl], gvn[:, sl]).astype(BF16)
        ws = gws_ref[g]
        for ci in range(tm // G_CHUNK):
            r = slice(ci * G_CHUNK, (ci + 1) * G_CHUNK)
            sg = _dot(ws, xn[r]) + bias[:, sl]
            yc_o[0, r, sl] = (gu[r, sl] * sg).astype(BF16)

    br_o[0] = _sigmoid(zbr_s[...]).astype(BF16)


def _inproj_call(xs, mod, n1, w, tabs, lc, tm, f=None, mod_prev=None):
    b, l, d = xs.shape
    nct = lc // tm
    has_f = f is not None
    tok = lambda width: pl.BlockSpec((1, tm, width), lambda bi, j: (bi, j, 0))
    modspec = pl.BlockSpec((1, 6, d), lambda bi, j: (jnp.where(j < nct, b, bi), 0, 0))
    tab = pl.BlockSpec((tm, A_PAD), lambda bi, j: (j, 0))
    consts = [n1, w["wqk"], w["wvo"], w["wgt"], w["gbt"], w["wa"], w["wg"], w["wbr"],
              w["aqn"], w["akvn"], w["wuq"], w["wuk"], w["wuv"], w["vone"]]
    consts2 = [w["gvn"], w["gws"], w["gbs"]]
    in_specs = ([tok(d), modspec] + [_const_spec(a.shape) for a in consts] + [tab, tab, tab]
                + [_const_spec(a.shape) for a in consts2])
    args = [xs, mod, *consts, *tabs, *consts2]
    ng = 4 * M_HEADS
    out_shape = [jax.ShapeDtypeStruct((b, l, 2 * M_WIDTH), F32),
                 jax.ShapeDtypeStruct((b, l, 2 * M_WIDTH), BF16),
                 jax.ShapeDtypeStruct((b, l, ng // 2), F32),
                 jax.ShapeDtypeStruct((b, l, ng // 2), F32),
                 jax.ShapeDtypeStruct((b, ng, l), F32),
                 jax.ShapeDtypeStruct((b, l, A_HEADS * A_PAD), BF16),
                 jax.ShapeDtypeStruct((b, l, A_HEADS * A_PAD), BF16),
                 jax.ShapeDtypeStruct((b, l, A_HEADS * A_PAD), BF16),
                 jax.ShapeDtypeStruct((b, l, G_WIDTH), BF16),
                 jax.ShapeDtypeStruct((b, l, 3 * d), BF16)]
    out_specs = [tok(2 * M_WIDTH), tok(2 * M_WIDTH), tok(ng // 2), tok(ng // 2),
                 pl.BlockSpec((1, ng, tm), lambda bi, j: (bi, 0, j)),
                 tok(A_HEADS * A_PAD), tok(A_HEADS * A_PAD), tok(A_HEADS * A_PAD), tok(G_WIDTH), tok(3 * d)]
    if has_f:
        in_specs = [tok(d), modspec] + in_specs
        args = [f, mod_prev] + args
        out_shape.append(jax.ShapeDtypeStruct((b, l, d), F32))
        out_specs.append(tok(d))
    return pl.pallas_call(
        functools.partial(_inproj_kernel, has_f=has_f), grid=(b, l // tm), in_specs=in_specs,
        out_specs=out_specs, out_shape=out_shape,
        scratch_shapes=[pltpu.VMEM((tm, w["wa"].shape[1]), F32), pltpu.VMEM((tm, 2 * G_WIDTH), F32),
                        pltpu.VMEM((tm, 3 * d), F32)],
        compiler_params=_params("parallel", "parallel"), name="inproj",
    )(*args)


def _scan(x, op, fill, axis, reverse):
    n = x.shape[axis]
    idx = lax.broadcasted_iota(jnp.int32, x.shape, axis)
    k = 1
    while k < n:
        if reverse:
            x = op(x, jnp.where(idx >= n - k, fill, pltpu.roll(x, n - k, axis)))
        else:
            x = op(x, jnp.where(idx < k, fill, pltpu.roll(x, k, axis)))
        k *= 2
    return x


def _mlstm_kernel(qk_ref, vo_ref, gi_ref, gf_ref, gr_ref, conv_ref, mnorm_ref, ya_ref,
                  q_s, kt_s, h_s, bc_s, ml_s, dl_s, br_s, cn_s, m_s, s_s, p_s, qcn_s, u_s, *, lc):
    l = qk_ref.shape[1]
    ch = M_CHUNK
    nc = l // ch
    ncc = lc // ch
    nh = M_HEADS
    ng = 2 * nh
    w = conv_ref[...]
    row = lax.broadcasted_iota(jnp.int32, (ch, 1), 0)

    def conv_chunk(j):
        r0 = pl.multiple_of(j * ch, ch)
        cur = qk_ref[0, pl.ds(r0, ch), :]
        prev8 = qk_ref[0, pl.ds(pl.multiple_of(jnp.maximum(r0 - 8, 0), 8), 8), :]
        next8 = qk_ref[0, pl.ds(pl.multiple_of(jnp.minimum(r0 + ch, l - 8), 8), 8), :]
        seg_start = jnp.logical_or(j == 0, j == ncc)
        seg_end = jnp.logical_or(j == ncc - 1, j == nc - 1)
        pe = jnp.where(seg_start, 0.0, prev8[7:8, :])
        ne = jnp.where(seg_end, 0.0, next8[0:1, :])
        xp = jnp.where(row == 0, pe, pltpu.roll(cur, 1, 0))
        xn = jnp.where(row == ch - 1, ne, pltpu.roll(cur, ch - 1, 0))
        y = _silu(xp * w[0:1] + cur * w[1:2] + xn * w[2:3])
        q_s[pl.ds(r0, ch), :] = (y[:, :M_WIDTH] * (M_DH ** -0.5)).astype(BF16)
        kt_s[:, pl.ds(r0, ch)] = y[:, M_WIDTH:].T.astype(BF16)

    ri = lax.broadcasted_iota(jnp.int32, (ch, ch), 0)
    ci = lax.broadcasted_iota(jnp.int32, (ch, ch), 1)
    lower = ri >= ci
    upper = ri <= ci
    ones_blk = jnp.ones((ch, M_DH), BF16)
    fwd_c = lax.broadcasted_iota(jnp.int32, (ch, ng), 1) < nh
    fwd_r = lax.broadcasted_iota(jnp.int32, (ng, ch), 0) < nh
    lane_c = lax.broadcasted_iota(jnp.int32, (ch, ng), 1)

    def local_chunk(j):
        r0 = pl.multiple_of(j * ch, ch)
        rows = pl.ds(r0, ch)
        lfc = _log_sigmoid(gf_ref[0, rows, :])
        gr = gr_ref[0, :, rows]
        lfr = _log_sigmoid(gr[ng:])
        pre_c = _scan(lfc, jnp.add, 0.0, 0, False)
        pre_r = _scan(lfr, jnp.add, 0.0, 1, False)
        b_c = jnp.where(fwd_c, pre_c, jnp.sum(lfc, axis=0, keepdims=True) + lfc - pre_c)
        b_r = jnp.where(fwd_r, pre_r, jnp.sum(lfr, axis=1, keepdims=True) + lfr - pre_r)
        g_c = gi_ref[0, rows, :] - b_c
        g_r = gr[:ng] - b_r
        cg_c = jnp.where(fwd_c, _scan(g_c, jnp.maximum, -jnp.inf, 0, False),
                         _scan(g_c, jnp.maximum, -jnp.inf, 0, True))
        bc_s[rows, :] = b_c
        br_s[:, rows] = b_r
        ml_s[rows, :] = b_c + cg_c
        dl = jnp.zeros((ch, ng), F32)
        for hh in range(nh):
            sl = slice(hh * M_DH, (hh + 1) * M_DH)
            s_s[hh] = _dot(q_s[rows, sl], kt_s[sl, rows])
        for hh in range(nh):
            s = s_s[hh]
            for d in range(2):
                jj = d * nh + hh
                wgt = jnp.exp(jnp.where(upper if d else lower, g_r[jj:jj + 1, :] - cg_c[:, jj:jj + 1], -jnp.inf))
                p_s[hh, d * ch:(d + 1) * ch] = (s * wgt).astype(BF16)
        for hh in range(nh):
            sl = slice(hh * M_DH, (hh + 1) * M_DH)
            v1 = jnp.concatenate([vo_ref[0, rows, sl], ones_blk], axis=1)
            nd2 = _dot(p_s[hh], v1)
            for d in range(2):
                jj = d * nh + hh
                nd = nd2[d * ch:(d + 1) * ch]
                h_s[d, rows, sl] = nd[:, :M_DH]
                dl = jnp.where(lane_c == jj, nd[:, M_DH:M_DH + ng], dl)
        dl_s[rows, :] = dl

    def conv_local_body(j, carry):
        conv_chunk(j + 1)
        local_chunk(j)
        return carry

    conv_chunk(jnp.int32(0))
    lax.fori_loop(0, nc - 1, conv_local_body, 0)
    local_chunk(jnp.int32(nc - 1))

    cn_s[...] = jnp.zeros_like(cn_s)
    m_s[...] = jnp.zeros_like(m_s)
    lane_r = lax.broadcasted_iota(jnp.int32, (1, ng), 1)

    def scan_issue(r0, d):
        rows = pl.ds(r0, ch)
        gr = gr_ref[0, :, rows]
        br = br_s[:, rows]
        tot = jnp.sum(_log_sigmoid(gr[ng:]), axis=1, keepdims=True)
        scal = []
        for hh in range(nh):
            fi = d * nh + hh
            sl = slice(hh * M_DH, (hh + 1) * M_DH)
            qcn_s[fi] = _dot(q_s[rows, sl], cn_s[fi].astype(BF16))
            m_old = m_s[fi][:, 0:1]
            b_e = tot[fi:fi + 1, :]
            d_end = b_e - br[fi:fi + 1, :] + gr[fi:fi + 1, :]
            m_end = jnp.max(d_end, axis=-1, keepdims=True)
            m_new = jnp.maximum(b_e + m_old, m_end)
            ktw = (kt_s[sl, rows].astype(F32) * jnp.exp(d_end - m_end)).astype(BF16)
            v1 = jnp.concatenate([vo_ref[0, rows, sl], ones_blk], axis=1)
            u_s[fi] = _dot(ktw, v1)
            scal.append((m_old, m_new, jnp.exp(b_e + m_old - m_new), jnp.exp(m_end - m_new)))
        return scal

    def scan_finish(r0, d, scal):
        rows = pl.ds(r0, ch)
        m_row = jnp.zeros((1, ng), F32)
        for hh in range(nh):
            m_row = jnp.where(lane_r == d * nh + hh, scal[hh][0], m_row)

        inter = bc_s[rows, :] + m_row
        ml = ml_s[rows, :]
        mt = jnp.maximum(inter, ml)
        a = jnp.exp(ml - mt)
        wi = jnp.exp(inter - mt)
        qn = jnp.zeros((ch, ng), F32)
        for hh in range(nh):
            fi = d * nh + hh
            qn = jnp.where(lane_c == fi, qcn_s[fi, :, M_DH:M_DH + ng], qn)
        den = a * dl_s[rows, :] + wi * qn
        rinv = 1.0 / jnp.maximum(jnp.abs(den), jnp.exp(-mt))
        c_loc = a * rinv
        c_int = wi * rinv

        for hh in range(nh):
            fi = d * nh + hh
            sl = slice(hh * M_DH, (hh + 1) * M_DH)
            h_s[d, rows, sl] = (c_loc[:, fi:fi + 1] * h_s[d, rows, sl]
                                + c_int[:, fi:fi + 1] * qcn_s[fi, :, :M_DH])
            cn_s[fi] = scal[hh][2] * cn_s[fi] + scal[hh][3] * u_s[fi]
            m_s[fi] = jnp.broadcast_to(scal[hh][1], (1, M_DH))

    def scan_body(s, carry):
        rf = pl.multiple_of(s * ch, ch)
        rb = pl.multiple_of(jnp.where(s < ncc, ncc - 1 - s, nc - 1 - s + ncc) * ch, ch)
        sf = scan_issue(rf, 0)
        sb = scan_issue(rb, 1)
        scan_finish(rf, 0, sf)
        scan_finish(rb, 1, sb)
        return carry

    lax.fori_loop(0, nc, scan_body, 0)

    mnorm = mnorm_ref[...]

    def out_body(j, carry):
        r0 = pl.multiple_of(j * ch, ch)
        hsum = h_s[0, pl.ds(r0, ch), :] + h_s[1, pl.ds(r0, ch), :]
        og = _sigmoid(vo_ref[0, pl.ds(r0, ch), M_WIDTH:].astype(F32))
        for hh in range(M_HEADS):
            sl = slice(hh * M_DH, (hh + 1) * M_DH)
            ya_ref[0, pl.ds(r0, ch), sl] = (_rms(hsum[:, sl], mnorm[:, sl]) * og[:, sl]).astype(BF16)
        return carry

    lax.fori_loop(0, nc, out_body, 0)


def _mlstm_call(qk, vo, gi, gf, gr, conv, mnorm, lc):
    b, l, _ = qk.shape
    ng = 2 * M_HEADS
    return pl.pallas_call(
        functools.partial(_mlstm_kernel, lc=lc),
        grid=(b,),
        in_specs=[pl.BlockSpec((1, l, 2 * M_WIDTH), lambda bi: (bi, 0, 0), pipeline_mode=pl.Buffered(1)),
                  pl.BlockSpec((1, l, 2 * M_WIDTH), lambda bi: (bi, 0, 0)),
                  pl.BlockSpec((1, l, ng), lambda bi: (bi, 0, 0)),
                  pl.BlockSpec((1, l, ng), lambda bi: (bi, 0, 0)),
                  pl.BlockSpec((1, 2 * ng, l), lambda bi: (bi, 0, 0)),
                  _const_spec(conv.shape), _const_spec(mnorm.shape)],
        out_specs=pl.BlockSpec((1, l, M_WIDTH), lambda bi: (bi, 0, 0)),
        out_shape=jax.ShapeDtypeStruct((b, l, M_WIDTH), BF16),
        scratch_shapes=[pltpu.VMEM((l, M_WIDTH), BF16),
                        pltpu.VMEM((M_WIDTH, l), BF16),
                        pltpu.VMEM((2, l, M_WIDTH), F32),
                        pltpu.VMEM((l, ng), F32),
                        pltpu.VMEM((l, ng), F32),
                        pltpu.VMEM((l, ng), F32),
                        pltpu.VMEM((ng, l), F32),
                        pltpu.VMEM((2 * M_HEADS, M_DH, 2 * M_DH), F32),
                        pltpu.VMEM((2 * M_HEADS, 1, M_DH), F32),
                        pltpu.VMEM((M_HEADS, M_CHUNK, M_CHUNK), F32),
                        pltpu.VMEM((M_HEADS, 2 * M_CHUNK, M_CHUNK), BF16),
                        pltpu.VMEM((2 * M_HEADS, M_CHUNK, 2 * M_DH), F32),
                        pltpu.VMEM((2 * M_HEADS, M_DH, 2 * M_DH), F32)],
        compiler_params=_params("parallel"), name="mlstm",
    )(qk, vo, gi, gf, gr, conv, mnorm)


def _attn_kernel(q_ref, k_ref, v_ref, o_ref, s_s, p_s, *, lc, ctx_out):
    tq = q_ref.shape[1]
    l = k_ref.shape[1]
    qi = pl.program_id(2)
    nct = lc // tq
    lane = lax.broadcasted_iota(jnp.int32, (tq, 2 * A_VDIM), 1)

    def run(klen):
        outs = []
        for hh in range(A_HPS):
            sl = slice(hh * A_PAD, (hh + 1) * A_PAD)
            s_s[hh, :, :klen] = _dot_nt(q_ref[0, :, sl], k_ref[0, :klen, sl])
        row_max = [jnp.max(s_s[hh, :, :klen], axis=-1, keepdims=True) for hh in range(A_HPS)]
        for hh in range(A_HPS):
            p_s[hh, :, :klen] = jnp.exp2((s_s[hh, :, :klen] - row_max[hh]).astype(BF16))
        for hh in range(A_HPS):
            sl = slice(hh * A_PAD, (hh + 1) * A_PAD)
            nd = _dot(p_s[hh, :, :klen], v_ref[0, :klen, sl])
            outs.append(nd / pltpu.roll(nd, A_VDIM, 1))
        for pp in range(A_HPS // 2):
            o_ref[0, :, pp * A_PAD:(pp + 1) * A_PAD] = jnp.where(
                lane < A_VDIM, outs[2 * pp], pltpu.roll(outs[2 * pp + 1], A_VDIM, 1)).astype(BF16)

    @pl.when(qi >= nct)
    def _():
        run(l)

    @pl.when(qi < nct)
    def _():
        if ctx_out:
            run(lc)
        else:
            o_ref[...] = jnp.zeros_like(o_ref)


def _attn_call(q, k, v, lc, tq, ctx_out):
    b, l, _ = q.shape
    return pl.pallas_call(
        functools.partial(_attn_kernel, lc=lc, ctx_out=ctx_out),
        grid=(b, A_HEADS // A_HPS, l // tq),
        in_specs=[pl.BlockSpec((1, tq, A_HPS * A_PAD), lambda bi, p, qi: (bi, qi, p)),
                  pl.BlockSpec((1, l, A_HPS * A_PAD), lambda bi, p, qi: (bi, 0, p)),
                  pl.BlockSpec((1, l, A_HPS * A_PAD), lambda bi, p, qi: (bi, 0, p))],
        out_specs=pl.BlockSpec((1, tq, A_HPS * A_VDIM), lambda bi, p, qi: (bi, qi, p)),
        out_shape=jax.ShapeDtypeStruct((b, l, A_WIDTH), BF16),
        scratch_shapes=[pltpu.VMEM((A_HPS, tq, l), F32), pltpu.VMEM((A_HPS, tq, l), BF16)],
        compiler_params=_params("parallel", "parallel", "arbitrary"), name="attn",
    )(q, k, v)


def _merge_kernel(x_ref, mod_ref, modc_ref, ya_ref, yb_ref, yc_ref, br_ref, wpa_ref, wpb_ref, wpc_ref, wout_ref,
                  n2_ref, r_ref, rb_ref, o_ref, h2_o, idx_o, idxt_o, cnt_o, y_s, *, tm, nct, skip_ctx):
    d = x_ref.shape[2]
    nsub = x_ref.shape[1] // tm
    modb = mod_ref[0]
    modc = modc_ref[0]
    tiles = [slice(s * tm, (s + 1) * tm) for s in range(nsub)]
    ctx = [pl.program_id(1) * nsub + s < nct for s in range(nsub)]
    mods = [jnp.where(c, modc, modb) for c in ctx]
    for s, rows in enumerate(tiles):
        br = br_ref[0, rows, :]
        y = (br[:, :d].astype(F32) * _dot(ya_ref[0, rows, :], wpa_ref[...])
             + br[:, d:2 * d].astype(F32) * _dot(yb_ref[0, rows, :], wpb_ref[...])
             + br[:, 2 * d:].astype(F32) * _dot(yc_ref[0, rows, :], wpc_ref[...]))
        y_s[s] = y.astype(BF16)
    for s, rows in enumerate(tiles):
        o_ref[0, rows, :] = x_ref[0, rows, :] + mods[s][2:3] * _dot(y_s[s], wout_ref[...])
    for s, rows in enumerate(tiles):
        logits = _route_logits(o_ref[0, rows, :], mods[s], n2_ref, r_ref, rb_ref, h2_o, rows)
        left_out = ctx[s] if skip_ctx else None
        _route_assign(logits, left_out, d, h2_o, idx_o, idxt_o, cnt_o, rows, s)


def _merge_call(xs, mod, ya, yb, yc, br, w, n2, r3, rb, lc, tm, skip_ctx):
    b, l, d = xs.shape
    nct = lc // tm
    nsub = next(n for n in (3, 2, 1) if l % (n * tm) == 0)
    tg = nsub * tm
    tok = lambda width: pl.BlockSpec((1, tg, width), lambda bi, j: (bi, j, 0))
    consts = [w["wpa"], w["wpb"], w["wpc"], w["wout"], n2, r3, rb]
    return pl.pallas_call(
        functools.partial(_merge_kernel, tm=tm, nct=nct, skip_ctx=skip_ctx), grid=(b, l // tg),
        in_specs=[tok(d), pl.BlockSpec((1, 6, d), lambda bi, j: (bi, 0, 0)),
                  pl.BlockSpec((1, 6, d), lambda bi, j: (b, 0, 0)),
                  tok(M_WIDTH), tok(A_WIDTH), tok(G_WIDTH), tok(3 * d)] + [_const_spec(a.shape) for a in consts],
        out_specs=[tok(d), tok(d + R_PAD), tok(8), pl.BlockSpec((1, 8, tg), lambda bi, j: (bi, 0, j)),
                   pl.BlockSpec((1, nsub, 8, R_PAD), lambda bi, j: (bi, j, 0, 0))],
        out_shape=[jax.ShapeDtypeStruct((b, l, d), F32),
                   jax.ShapeDtypeStruct((b, l, d + R_PAD), BF16),
                   jax.ShapeDtypeStruct((b, l, 8), jnp.int32), jax.ShapeDtypeStruct((b, 8, l), jnp.int32),
                   jax.ShapeDtypeStruct((b, l // tm, 8, R_PAD), F32)],
        scratch_shapes=[pltpu.VMEM((nsub, tm, d), BF16)],
        compiler_params=_params("parallel", "parallel"), name="merge",
    )(xs, mod, mod, ya, yb, yc, br, *consts)


def _route_logits(x, mod, n2_ref, r_ref, rb_ref, h2_o, rows):
    d = x.shape[1]
    h2 = _rms(x, n2_ref[...]) * (1.0 + mod[4:5]) + mod[3:4]
    h2_o[0, rows, :d] = h2.astype(BF16)
    r = r_ref[...]
    pp = sum(_dot(piece, r) for piece in _split3(h2)[:2])
    return pp + pltpu.roll(pp, R_PAD - R_SEG, 1) + pltpu.roll(pp, R_PAD - 2 * R_SEG, 1) + rb_ref[...]


def _route_assign(logits, left_out, d, h2_o, idx_o, idxt_o, cnt_o, rows, s):
    tm = logits.shape[0]
    el = logits[:, :N_EXPERTS]
    gl = logits[:, N_EXPERTS:N_EXPERTS + N_GROUPS]
    big = 1e9

    lane_g = lax.broadcasted_iota(jnp.int32, (tm, N_GROUPS), 1).astype(F32)
    gmax = jnp.max(gl, axis=-1, keepdims=True)
    g_sel = jnp.min(jnp.where(gl == gmax, lane_g, big), axis=-1, keepdims=True)
    g_prob = 1.0 / jnp.sum(jnp.exp(gl - gmax), axis=-1, keepdims=True)

    lane_i = lax.broadcasted_iota(jnp.int32, (tm, N_EXPERTS), 1)
    lane_e = lane_i.astype(F32)
    lane_grp = (lane_i // EXP_PER_GROUP).astype(F32)
    v1 = jnp.where(lane_grp == g_sel, el, -jnp.inf)
    t1 = jnp.max(v1, axis=-1, keepdims=True)
    i1 = jnp.min(jnp.where(v1 == t1, lane_e, big), axis=-1, keepdims=True)
    v2 = jnp.where(lane_e == i1, -jnp.inf, v1)
    t2 = jnp.max(v2, axis=-1, keepdims=True)
    i2 = jnp.min(jnp.where(v2 == t2, lane_e, big), axis=-1, keepdims=True)
    e21 = jnp.exp(t2 - t1)
    w1 = 1.0 / (1.0 + e21)
    w2 = e21 * w1
    comb = (jnp.where(lane_e == i1, w1, 0.0) + jnp.where(lane_e == i2, w2, 0.0)) * g_prob
    tail = jnp.zeros((tm, R_PAD - 3 * N_EXPERTS), BF16)
    h2_o[0, rows, d:] = jnp.concatenate(list(_split3(comb)) + [tail], axis=1)

    if left_out is not None:
        g_sel = jnp.where(left_out, -1.0, g_sel)
    lane_p = lax.broadcasted_iota(jnp.int32, (tm, R_PAD), 1)
    onehot = jnp.where(lane_p.astype(F32) == g_sel, 1.0, 0.0)
    ri = lax.broadcasted_iota(jnp.int32, (tm, tm), 0)
    ci = lax.broadcasted_iota(jnp.int32, (tm, tm), 1)
    before = jnp.where(ri > ci, 1.0, 0.0).astype(BF16)
    rank = jnp.sum(_dot(before, onehot.astype(BF16)) * onehot, axis=-1, keepdims=True)
    cnt_o[0, s] = jnp.broadcast_to(jnp.sum(onehot, axis=0, keepdims=True), (8, R_PAD))
    fields = jnp.where(lane_p == 0, g_sel, jnp.where(lane_p == 1, rank, 0.0))
    idx_o[0, rows, :] = fields[:, :8].astype(jnp.int32)
    idxt_o[0, :, rows] = fields.T[:8, :].astype(jnp.int32)


def _experts_kernel(st_ref, h2_ref, idx_ref, idxt_ref, w1_ref, w3_ref, w2_ref, o_ref, hs_s, ys_s, a_s, hid_s, *, tb):
    l = h2_ref.shape[1]
    d = o_ref.shape[2]
    nblk = l // tb
    ch = MOE_CHUNK
    bi = pl.program_id(0)
    g = pl.program_id(1)

    @pl.when(g == 0)
    def _():
        ys_s[...] = jnp.zeros_like(ys_s)

    def group_base(gg):
        return (bi * N_GROUPS + gg) * (nblk + 1)

    def group_offset(upto):
        off = 0
        for gg in range(N_GROUPS - 1):
            padded = ((st_ref[group_base(gg) + nblk] + ch - 1) // ch) * ch
            off = off + jnp.where(gg < upto, padded, 0)
        return off

    base = group_base(g)
    cnt = st_ref[base + nblk]
    goff = group_offset(g)
    gw = min(MOE_GATHER_BLOCKS, nblk)
    nwin = -(-nblk // gw)

    def chunk(lo, ch):
        lane_e = lax.broadcasted_iota(jnp.int32, (ch, N_EXPERTS), 1)
        sub_iota = lax.broadcasted_iota(jnp.int32, (ch, gw * tb), 0)
        lane_blk = lax.broadcasted_iota(jnp.int32, (1, gw * tb), 1) // tb
        k0 = 0
        for k in range(nblk):
            k0 = k0 + jnp.where(st_ref[base + k + 1] <= lo, 1, 0)

        def window(j):
            want = k0 + j * gw
            kj = jnp.minimum(want, nblk - gw)
            tok0 = pl.multiple_of(kj * tb, tb)
            it = idxt_ref[0, :, pl.ds(tok0, gw * tb)]
            offs = jnp.concatenate([jnp.full((1, tb), st_ref[base + kj + i] - lo, jnp.int32) for i in range(gw)],
                                   axis=1)
            mine = jnp.where(lane_blk >= want - kj, it[0:1], -1)
            pos = jnp.where(mine == g, it[1:2] + offs, -1)
            p = jnp.where(sub_iota == pos, 1.0, 0.0).astype(BF16)
            return _dot(p, h2_ref[0, pl.ds(tok0, gw * tb), :])

        hs_s[:ch] = window(0)
        for j in range(1, nwin):
            first = k0 + j * gw

            @pl.when(jnp.logical_and(first < nblk, st_ref[base + jnp.minimum(first, nblk)] < lo + ch))
            def _(j=j):
                hs_s[:ch] += window(j)

        hsb = hs_s[:ch, :d].astype(BF16)
        cs = (hs_s[:ch, d:d + N_EXPERTS] + hs_s[:ch, d + N_EXPERTS:d + 2 * N_EXPERTS]
              + hs_s[:ch, d + 2 * N_EXPERTS:d + 3 * N_EXPERTS])
        for e in range(EXP_PER_GROUP):
            a_s[2 * e, :ch] = _dot(hsb, w1_ref[e])
            a_s[2 * e + 1, :ch] = _dot(hsb, w3_ref[e])
        for e in range(EXP_PER_GROUP):
            ce = jnp.sum(jnp.where(lane_e == g * EXP_PER_GROUP + e, cs, 0.0), axis=-1, keepdims=True)
            hid_s[e, :ch] = (_silu(a_s[2 * e, :ch]) * a_s[2 * e + 1, :ch] * ce).astype(BF16)
        y = jnp.zeros((ch, d), F32)
        for e in range(EXP_PER_GROUP):
            y = y + _dot(hid_s[e, :ch], w2_ref[e])
        ys_s[pl.ds(pl.multiple_of(goff + lo, ch), ch), :] = y.astype(BF16)

    half = ch // 2
    nfull = cnt // ch
    rem = cnt - nfull * ch
    nloop = nfull + jnp.where(rem > half, 1, 0)

    def chunk_body(c, carry):
        chunk(c * ch, ch)
        return carry

    lax.fori_loop(0, nloop, chunk_body, 0)

    @pl.when(jnp.logical_and(rem > 0, rem <= half))
    def _():
        chunk(nfull * ch, half)

    @pl.when(g == N_GROUPS - 1)
    def _():
        lane_w = lax.broadcasted_iota(jnp.int32, (tb, ch), 1)
        goffs = [group_offset(gg) for gg in range(N_GROUPS)]

        def window(k, gg, shift):
            s_k = st_ref[group_base(gg) + k]
            win = (s_k // half) * half
            ic = idx_ref[0, k * tb:(k + 1) * tb, :]
            pos = jnp.where(ic[:, 0:1] == gg, ic[:, 1:2] + (s_k - win - shift), -1)
            q = jnp.where(lane_w == pos, 1.0, 0.0).astype(BF16)
            start = pl.multiple_of(goffs[gg] + win + shift, half)
            return _dot(q, ys_s[pl.ds(start, ch), :])

        for k in range(nblk):
            acc = window(k, 0, 0)
            for gg in range(1, N_GROUPS):
                acc = acc + window(k, gg, 0)
            o_ref[0, k * tb:(k + 1) * tb, :] = acc.astype(BF16)
        for k in range(nblk):
            rows = slice(k * tb, (k + 1) * tb)
            for gg in range(N_GROUPS):
                s_k = st_ref[group_base(gg) + k]
                e_k = st_ref[group_base(gg) + k + 1]

                @pl.when(e_k - (s_k // half) * half > ch)
                def _(k=k, gg=gg, rows=rows):
                    o_ref[0, rows, :] = (o_ref[0, rows, :].astype(F32) + window(k, gg, ch)).astype(BF16)


def _experts_call(starts, h2, idx, idxt, w1, w3, w2, tb):
    b, l, de = h2.shape
    d = de - R_PAD
    whole = lambda width: pl.BlockSpec((1, l, width), lambda bi, g, st: (bi, 0, 0))
    grid_spec = pltpu.PrefetchScalarGridSpec(
        num_scalar_prefetch=1, grid=(b, N_GROUPS),
        in_specs=[pl.BlockSpec((1, l, de), lambda bi, g, st: (bi, 0, 0), pipeline_mode=pl.Buffered(1)),
                  whole(8), pl.BlockSpec((1, 8, l), lambda bi, g, st: (bi, 0, 0)),
                  pl.BlockSpec((EXP_PER_GROUP, d, D_EXPERT), lambda bi, g, st: (g, 0, 0)),
                  pl.BlockSpec((EXP_PER_GROUP, d, D_EXPERT), lambda bi, g, st: (g, 0, 0)),
                  pl.BlockSpec((EXP_PER_GROUP, D_EXPERT, d), lambda bi, g, st: (g, 0, 0))],
        out_specs=whole(d),
        scratch_shapes=[pltpu.VMEM((MOE_CHUNK, de), F32),
                        pltpu.VMEM((l + (N_GROUPS + 1) * MOE_CHUNK, d), BF16),
                        pltpu.VMEM((2 * EXP_PER_GROUP, MOE_CHUNK, D_EXPERT), F32),
                        pltpu.VMEM((EXP_PER_GROUP, MOE_CHUNK, D_EXPERT), BF16)])
    return pl.pallas_call(
        functools.partial(_experts_kernel, tb=tb), grid_spec=grid_spec,
        out_shape=jax.ShapeDtypeStruct((b, l, d), BF16),
        compiler_params=_params("parallel", "arbitrary"), name="experts",
    )(starts, h2, idx, idxt, w1, w3, w2)


def _group_starts(cnt):
    c = cnt[:, :, 0, :N_GROUPS].astype(jnp.int32)
    s = jnp.cumsum(c, axis=1)
    s = jnp.concatenate([jnp.zeros_like(s[:, :1]), s], axis=1)
    return jnp.transpose(s, (0, 2, 1)).reshape(-1)


def _final_kernel(x_ref, mod_ref, f_ref, g_ref, o_ref):
    o_ref[0] = _rms(x_ref[0] + mod_ref[0][5:6] * f_ref[0].astype(F32), g_ref[...])


def _final_call(xs, mod, f, g, lc, tm):
    b, l, d = xs.shape
    off = lc // tm
    lat = pl.BlockSpec((1, tm, d), lambda bi, j: (bi, j + off, 0))
    return pl.pallas_call(
        _final_kernel, grid=(b, (l - lc) // tm),
        in_specs=[lat, pl.BlockSpec((1, 6, d), lambda bi, j: (bi, 0, 0)), lat, _const_spec(g.shape)],
        out_specs=pl.BlockSpec((1, tm, d), lambda bi, j: (bi, j, 0)),
        out_shape=jax.ShapeDtypeStruct((b, l - lc, d), F32),
        compiler_params=_params("parallel", "parallel"), name="final_norm",
    )(xs, mod, f, g)


def _rope_tables(t_len, lc):
    half = A_ROPE // 2
    rows = t_len // GRID_W
    r = jnp.repeat(jnp.arange(rows, dtype=F32), GRID_W)
    col = jnp.tile(jnp.arange(GRID_W, dtype=F32), rows)
    inv = ROPE_THETA ** (-jnp.arange(0, half, 2, dtype=F32) / half)
    ang = jnp.concatenate([r[:, None] * inv, col[:, None] * inv], axis=-1)
    cos = jnp.concatenate([jnp.ones((lc, half), F32), jnp.cos(ang)], axis=0)
    sin = jnp.concatenate([jnp.zeros((lc, half), F32), jnp.sin(ang)], axis=0)
    l = lc + t_len
    ones = jnp.ones((l, A_NOPE), F32)
    zeros = jnp.zeros((l, A_NOPE), F32)
    tail1 = jnp.ones((l, A_PAD - A_NOPE - A_ROPE), F32)
    tail0 = jnp.zeros((l, A_PAD - A_NOPE - A_ROPE), F32)
    zh = jnp.zeros((l, half), F32)
    cos_t = jnp.concatenate([ones, cos, cos, tail1], axis=-1)
    sina_t = jnp.concatenate([zeros, zh, sin, tail0], axis=-1)
    sinb_t = jnp.concatenate([zeros, -sin, zh, tail0], axis=-1)
    return cos_t, sina_t, sinb_t


def _layer_weights(l, w_in, m_gate_b, a_qnorm, a_wuq, a_kvnorm, a_wukv, g_ws, g_bs, g_vnorm,
                   w_pa, w_pb, w_pc, w_out):
    d = w_in.shape[1]
    wi = w_in[l]
    o = 0

    def take(n):
        nonlocal o
        s = wi[:, o:o + n]
        o += n
        return s

    mq, mk, mv, mo, mg = take(M_WIDTH), take(M_WIDTH), take(M_WIDTH), take(M_WIDTH), take(4 * M_HEADS)
    aq, akv, akr = take(A_QRANK), take(A_KVRANK), take(A_ROPE)
    gu, gv = take(G_WIDTH), take(G_WIDTH)
    br = take(3 * d)
    nh = M_HEADS
    gb = m_gate_b[l]
    mgo = jnp.concatenate([mg[:, :nh], mg[:, 2 * nh:3 * nh], mg[:, nh:2 * nh], mg[:, 3 * nh:]], axis=1)
    gbo = jnp.concatenate([gb[:nh], gb[2 * nh:3 * nh], gb[nh:2 * nh], gb[3 * nh:]])
    akr_pad = jnp.concatenate([jnp.zeros((d, A_NOPE), F32), akr,
                               jnp.zeros((d, A_PAD - A_NOPE - A_ROPE), F32)], axis=1)
    wuq = a_wuq[l].reshape(A_QRANK, A_HEADS, A_NOPE + A_ROPE)
    wuq = jnp.pad(wuq, ((0, 0), (0, 0), (0, A_PAD - A_NOPE - A_ROPE))).reshape(A_QRANK, A_HEADS * A_PAD)
    wukv = a_wukv[l].reshape(A_KVRANK, A_HEADS, A_NOPE + A_VDIM)
    wuk = jnp.pad(wukv[:, :, :A_NOPE], ((0, 0), (0, 0), (0, A_PAD - A_NOPE))).reshape(A_KVRANK, A_HEADS * A_PAD)
    wuv = jnp.pad(wukv[:, :, A_NOPE:], ((0, 0), (0, 0), (0, A_PAD - A_VDIM))).reshape(A_KVRANK, A_HEADS * A_PAD)
    vone = jnp.tile(jnp.concatenate([jnp.zeros((A_VDIM,), F32), jnp.ones((A_PAD - A_VDIM,), F32)]),
                    A_HEADS).reshape(1, A_HEADS * A_PAD)
    gbs = jnp.repeat(g_bs[l].T, G_DG, axis=1)
    return dict(
        wqk=jnp.concatenate([mq, mk], 1).astype(BF16), wvo=jnp.concatenate([mv, mo], 1).astype(BF16),
        wgt=jnp.pad(mgo, ((0, 0), (0, G_PAD - 4 * nh))).astype(BF16),
        gbt=jnp.pad(gbo, (0, G_PAD - 4 * nh)).reshape(1, G_PAD),
        wa=jnp.concatenate([aq, akv, akr_pad], 1).astype(BF16),
        wg=jnp.concatenate([gu, gv], 1).astype(BF16), wbr=br.astype(BF16),
        aqn=a_qnorm[l].reshape(1, -1), akvn=a_kvnorm[l].reshape(1, -1),
        wuq=wuq.astype(BF16), wuk=wuk.astype(BF16), wuv=wuv.astype(BF16), vone=vone,
        gvn=g_vnorm[l].reshape(1, -1), gws=g_ws[l].astype(BF16), gbs=gbs,
        wpa=w_pa[l].astype(BF16), wpb=w_pb[l].astype(BF16), wpc=w_pc[l].astype(BF16),
        wout=w_out[l].astype(BF16))


def _router_weights(r_group, r_group_b, r_expert, r_expert_b):
    d = r_group.shape[0]
    pad = R_SEG - N_EXPERTS - N_GROUPS
    r = jnp.concatenate([r_expert, r_group, jnp.zeros((d, pad), F32)], axis=1)
    r3 = jnp.concatenate(list(_split3(r)) + [jnp.zeros((d, R_PAD - 3 * R_SEG), BF16)], axis=1)
    rb = jnp.concatenate([r_expert_b, r_group_b, jnp.zeros((R_PAD - N_EXPERTS - N_GROUPS,), F32)])
    return r3, rb.reshape(1, R_PAD)


def _tile(n, lc, candidates):
    for t in candidates:
        if n % t == 0 and lc % t == 0:
            return t
    raise ValueError("sequence lengths must be multiples of 128")


def kernel(x, c, ctx, c_ctx, w_ada, b_ada, norm1, norm2, final_norm, w_in, m_conv, m_gate_b, m_norm, a_qnorm, a_wuq, a_kvnorm, a_wukv, g_ws, g_bs, g_vnorm, w_pa, w_pb, w_pc, w_out, r_group, r_group_b, r_expert, r_expert_b, e_w1, e_w3, e_w2):
    b, t_len, d = x.shape
    lc = ctx.shape[1]
    l = lc + t_len
    depth = w_in.shape[0]
    tm = _tile(l, lc, (256, 128))

    xs = jnp.concatenate([ctx, x], axis=1)
    cv = jnp.concatenate([c, c_ctx[None, :]], axis=0)
    mod_all = _ada_call(cv, w_ada, b_ada).reshape(depth, b + 1, 6, d)
    tabs = _rope_tables(t_len, lc)

    w_in_b = w_in.astype(BF16)
    f = None
    mod_prev = None
    for li in range(depth):
        last = li == depth - 1
        mod = mod_all[li]
        w = _layer_weights(li, w_in_b, m_gate_b, a_qnorm, a_wuq, a_kvnorm, a_wukv, g_ws, g_bs, g_vnorm,
                           w_pa, w_pb, w_pc, w_out)
        outs = _inproj_call(xs, mod, norm1[li].reshape(1, d), w, tabs, lc, tm, f, mod_prev)
        qk, vo, gi, gf, gr, q, k, v, yc, br = outs[:10]
        if f is not None:
            xs = outs[10]
        ya = _mlstm_call(qk, vo, gi, gf, gr, m_conv[li], m_norm[li].reshape(1, -1), lc)
        yb = _attn_call(q, k, v, lc, tm, not last)
        r3, rb = _router_weights(r_group[li], r_group_b[li], r_expert[li], r_expert_b[li])
        xs, h2, idx, idxt, cnt = _merge_call(xs, mod, ya, yb, yc, br, w, norm2[li].reshape(1, d),
                                             r3, rb, lc, tm, last)
        f = _experts_call(_group_starts(cnt), h2, idx, idxt, e_w1[li].astype(BF16),
                          e_w3[li].astype(BF16), e_w2[li].astype(BF16), tm)
        mod_prev = mod
    return _final_call(xs, mod_prev, f, final_norm.reshape(1, d), lc, tm)
```

```python
import functools

import jax
import jax.numpy as jnp
from jax import lax
from jax.experimental import pallas as pl
from jax.experimental.pallas import tpu as pltpu

F32 = jnp.float32
BF16 = jnp.bfloat16

EPS = 1e-6
GRID_W = 64
ROPE_THETA = 10000.0

M_HEADS = 4
M_DH = 128
M_WIDTH = M_HEADS * M_DH
M_CHUNK = 128
G_PAD = 128

A_HEADS = 8
A_NOPE = 64
A_ROPE = 32
A_VDIM = 64
A_QRANK = 384
A_KVRANK = 256
A_WIDTH = A_HEADS * A_VDIM
A_PAD = 128
A_HPS = 4
ATT_SCALE = (A_NOPE + A_ROPE) ** -0.5
LOG2E = 1.4426950408889634

G_GROUPS = 4
G_CHUNK = 128
G_WIDTH = 512
G_DG = G_WIDTH // G_GROUPS

N_GROUPS = 4
EXP_PER_GROUP = 4
N_EXPERTS = N_GROUPS * EXP_PER_GROUP
D_EXPERT = 512
R_PAD = 128
R_SEG = 32
MOE_CHUNK = 256
MOE_GATHER_BLOCKS = 5

VMEM_LIMIT = 56 * 1024 * 1024


def _dot(a, b):
    return jnp.dot(a, b, preferred_element_type=F32)


def _dot_nt(a, b):
    return lax.dot_general(a, b, (((1,), (1,)), ((), ())), preferred_element_type=F32)


def _split3(x):
    hi = x.astype(BF16)
    r = x - hi.astype(F32)
    mid = r.astype(BF16)
    lo = (r - mid.astype(F32)).astype(BF16)
    return hi, mid, lo


def _sigmoid(x):
    return 1.0 / (1.0 + jnp.exp(-x))


def _silu(x):
    return x * _sigmoid(x)


def _log_sigmoid(x):
    return jnp.minimum(x, 0.0) - jnp.log1p(jnp.exp(-jnp.abs(x)))


def _gelu(x):
    return 0.5 * x * (1.0 + lax.erf(x * (2.0 ** -0.5)))


def _rms(x, g):
    return x * lax.rsqrt(jnp.mean(x * x, axis=-1, keepdims=True) + EPS) * g


def _params(*sem):
    return pltpu.CompilerParams(dimension_semantics=sem, vmem_limit_bytes=VMEM_LIMIT)


def _const_spec(shape):
    nd = len(shape)
    return pl.BlockSpec(shape, lambda *_: (0,) * nd, pipeline_mode=pl.Buffered(1))


def _ada_kernel(cv_ref, w_ref, b_ref, o_ref):
    s = _silu(cv_ref[...])
    o_ref[0] = _dot(s.astype(BF16), w_ref[0].astype(BF16)) + b_ref[0]


def _ada_call(cv, w_ada, b_ada):
    depth, d, n6 = w_ada.shape
    rows = cv.shape[0]
    tn = n6 // 4
    return pl.pallas_call(
        _ada_kernel,
        grid=(depth, n6 // tn),
        in_specs=[pl.BlockSpec((rows, d), lambda l, j: (0, 0)),
                  pl.BlockSpec((1, d, tn), lambda l, j: (l, 0, j)),
                  pl.BlockSpec((1, 1, tn), lambda l, j: (l, 0, j))],
        out_specs=pl.BlockSpec((1, rows, tn), lambda l, j: (l, 0, j)),
        out_shape=jax.ShapeDtypeStruct((depth, rows, n6), F32),
        compiler_params=_params("parallel", "parallel"),
        name="ada",
    )(cv, w_ada, b_ada.reshape(depth, 1, n6))


def _inproj_kernel(*refs, has_f):
    za_s, zg_s, zbr_s = refs[-3:]
    refs = refs[:-3]
    if has_f:
        f_ref, modp_ref, x_o = refs[0], refs[1], refs[-1]
        refs = refs[2:-1]
    (x_ref, mod_ref, n1_ref, wqk_ref, wvo_ref, wgt_ref, gbt_ref,
     wa_ref, wg_ref, wbr_ref, aqn_ref, akvn_ref, wuq_ref, wuk_ref, wuv_ref, vone_ref,
     cos_ref, sina_ref, sinb_ref, gvn_ref, gws_ref, gbs_ref,
     qk_o, vo_o, gi_o, gf_o, gr_o, q_o, k_o, v_o, yc_o, br_o) = refs
    tm = x_ref.shape[1]
    mod = mod_ref[0]
    x = x_ref[0]
    if has_f:
        x = x + modp_ref[0][5:6] * f_ref[0].astype(F32)
        x_o[0] = x
    h = _rms(x, n1_ref[...]) * (1.0 + mod[1:2]) + mod[0:1]
    hb = h.astype(BF16)

    qk_o[0] = _dot(hb, wqk_ref[...])
    vo_o[0] = _dot(hb, wvo_ref[...]).astype(BF16)
    ng = gi_o.shape[2]
    gates = _dot(hb, wgt_ref[...]) + gbt_ref[...]
    gi_o[0] = gates[:, :ng]
    gf_o[0] = pltpu.roll(gates, gates.shape[1] - ng, 1)[:, :ng]
    gr_o[0] = gates.T[:2 * ng, :]

    za_s[...] = _dot(hb, wa_ref[...])
    zg_s[...] = _dot(hb, wg_ref[...])
    zbr_s[...] = _dot(hb, wbr_ref[...])

    aqn = _rms(za_s[:, :A_QRANK], aqn_ref[...]).astype(BF16)
    akvn = _rms(za_s[:, A_QRANK:A_QRANK + A_KVRANK], akvn_ref[...]).astype(BF16)
    cos = cos_ref[...]
    sina = sina_ref[...]
    sinb = sinb_ref[...]
    half = A_ROPE // 2

    def rope(t):
        return t * cos + pltpu.roll(t, half, 1) * sina + pltpu.roll(t, A_PAD - half, 1) * sinb

    kr = rope(za_s[:, A_QRANK + A_KVRANK:])
    qp = _dot(aqn, wuq_ref[...])
    kp = _dot(akvn, wuk_ref[...])
    for hh in range(A_HEADS):
        sl = slice(hh * A_PAD, (hh + 1) * A_PAD)
        q_o[0, :, sl] = (rope(qp[:, sl]) * (ATT_SCALE * LOG2E)).astype(BF16)
        k_o[0, :, sl] = (kp[:, sl] + kr).astype(BF16)
    v_o[0] = (_dot(akvn, wuv_ref[...]) + vone_ref[...]).astype(BF16)

    gu = _gelu(zg_s[:, :G_WIDTH])
    gv = _gelu(zg_s[:, G_WIDTH:])
    gvn = gvn_ref[...]
    bias = gbs_ref[...]
    for g in range(G_GROUPS):
        sl = slice(g * G_DG, (g + 1) * G_DG)
        xn = _rms(gv[:, sl], gvn[:, sl]).astype(BF16)
        ws = gws_ref[g]
        for ci in range(tm // G_CHUNK):
            r = slice(ci * G_CHUNK, (ci + 1) * G_CHUNK)
            sg = _dot(ws, xn[r]) + bias[:, sl]
            yc_o[0, r, sl] = (gu[r, sl] * sg).astype(BF16)

    br_o[0] = _sigmoid(zbr_s[...]).astype(BF16)


def _inproj_call(xs, mod, n1, w, tabs, lc, tm, f=None, mod_prev=None):
    b, l, d = xs.shape
    nct = lc // tm
    has_f = f is not None
    tok = lambda width: pl.BlockSpec((1, tm, width), lambda bi, j: (bi, j, 0))
    modspec = pl.BlockSpec((1, 6, d), lambda bi, j: (jnp.where(j < nct, b, bi), 0, 0))
    tab = pl.BlockSpec((tm, A_PAD), lambda bi, j: (j, 0))
    consts = [n1, w["wqk"], w["wvo"], w["wgt"], w["gbt"], w["wa"], w["wg"], w["wbr"],
              w["aqn"], w["akvn"], w["wuq"], w["wuk"], w["wuv"], w["vone"]]
    consts2 = [w["gvn"], w["gws"], w["gbs"]]
    in_specs = ([tok(d), modspec] + [_const_spec(a.shape) for a in consts] + [tab, tab, tab]
                + [_const_spec(a.shape) for a in consts2])
    args = [xs, mod, *consts, *tabs, *consts2]
    ng = 4 * M_HEADS
    out_shape = [jax.ShapeDtypeStruct((b, l, 2 * M_WIDTH), F32),
                 jax.ShapeDtypeStruct((b, l, 2 * M_WIDTH), BF16),
                 jax.ShapeDtypeStruct((b, l, ng // 2), F32),
                 jax.ShapeDtypeStruct((b, l, ng // 2), F32),
                 jax.ShapeDtypeStruct((b, ng, l), F32),
                 jax.ShapeDtypeStruct((b, l, A_HEADS * A_PAD), BF16),
                 jax.ShapeDtypeStruct((b, l, A_HEADS * A_PAD), BF16),
                 jax.ShapeDtypeStruct((b, l, A_HEADS * A_PAD), BF16),
                 jax.ShapeDtypeStruct((b, l, G_WIDTH), BF16),
                 jax.ShapeDtypeStruct((b, l, 3 * d), BF16)]
    out_specs = [tok(2 * M_WIDTH), tok(2 * M_WIDTH), tok(ng // 2), tok(ng // 2),
                 pl.BlockSpec((1, ng, tm), lambda bi, j: (bi, 0, j)),
                 tok(A_HEADS * A_PAD), tok(A_HEADS * A_PAD), tok(A_HEADS * A_PAD), tok(G_WIDTH), tok(3 * d)]
    if has_f:
        in_specs = [tok(d), modspec] + in_specs
        args = [f, mod_prev] + args
        out_shape.append(jax.ShapeDtypeStruct((b, l, d), F32))
        out_specs.append(tok(d))
    return pl.pallas_call(
        functools.partial(_inproj_kernel, has_f=has_f), grid=(b, l // tm), in_specs=in_specs,
        out_specs=out_specs, out_shape=out_shape,
        scratch_shapes=[pltpu.VMEM((tm, w["wa"].shape[1]), F32), pltpu.VMEM((tm, 2 * G_WIDTH), F32),
                        pltpu.VMEM((tm, 3 * d), F32)],
        compiler_params=_params("parallel", "parallel"), name="inproj",
    )(*args)


def _scan(x, op, fill, axis, reverse):
    n = x.shape[axis]
    idx = lax.broadcasted_iota(jnp.int32, x.shape, axis)
    k = 1
    while k < n:
        if reverse:
            x = op(x, jnp.where(idx >= n - k, fill, pltpu.roll(x, n - k, axis)))
        else:
            x = op(x, jnp.where(idx < k, fill, pltpu.roll(x, k, axis)))
        k *= 2
    return x


def _mlstm_kernel(qk_ref, vo_ref, gi_ref, gf_ref, gr_ref, conv_ref, mnorm_ref, ya_ref,
                  q_s, kt_s, h_s, bc_s, ml_s, dl_s, br_s, cn_s, m_s, s_s, p_s, qcn_s, u_s, *, lc):
    l = qk_ref.shape[1]
    ch = M_CHUNK
    nc = l // ch
    ncc = lc // ch
    nh = M_HEADS
    ng = 2 * nh
    w = conv_ref[...]
    row = lax.broadcasted_iota(jnp.int32, (ch, 1), 0)

    def conv_chunk(j):
        r0 = pl.multiple_of(j * ch, ch)
        cur = qk_ref[0, pl.ds(r0, ch), :]
        prev8 = qk_ref[0, pl.ds(pl.multiple_of(jnp.maximum(r0 - 8, 0), 8), 8), :]
        next8 = qk_ref[0, pl.ds(pl.multiple_of(jnp.minimum(r0 + ch, l - 8), 8), 8), :]
        seg_start = jnp.logical_or(j == 0, j == ncc)
        seg_end = jnp.logical_or(j == ncc - 1, j == nc - 1)
        pe = jnp.where(seg_start, 0.0, prev8[7:8, :])
        ne = jnp.where(seg_end, 0.0, next8[0:1, :])
        xp = jnp.where(row == 0, pe, pltpu.roll(cur, 1, 0))
        xn = jnp.where(row == ch - 1, ne, pltpu.roll(cur, ch - 1, 0))
        y = _silu(xp * w[0:1] + cur * w[1:2] + xn * w[2:3])
        q_s[pl.ds(r0, ch), :] = (y[:, :M_WIDTH] * (M_DH ** -0.5)).astype(BF16)
        kt_s[:, pl.ds(r0, ch)] = y[:, M_WIDTH:].T.astype(BF16)

    ri = lax.broadcasted_iota(jnp.int32, (ch, ch), 0)
    ci = lax.broadcasted_iota(jnp.int32, (ch, ch), 1)
    lower = ri >= ci
    upper = ri <= ci
    ones_blk = jnp.ones((ch, M_DH), BF16)
    fwd_c = lax.broadcasted_iota(jnp.int32, (ch, ng), 1) < nh
    fwd_r = lax.broadcasted_iota(jnp.int32, (ng, ch), 0) < nh
    lane_c = lax.broadcasted_iota(jnp.int32, (ch, ng), 1)

    def local_chunk(j):
        r0 = pl.multiple_of(j * ch, ch)
        rows = pl.ds(r0, ch)
        lfc = _log_sigmoid(gf_ref[0, rows, :])
        gr = gr_ref[0, :, rows]
        lfr = _log_sigmoid(gr[ng:])
        pre_c = _scan(lfc, jnp.add, 0.0, 0, False)
        pre_r = _scan(lfr, jnp.add, 0.0, 1, False)
        b_c = jnp.where(fwd_c, pre_c, jnp.sum(lfc, axis=0, keepdims=True) + lfc - pre_c)
        b_r = jnp.where(fwd_r, pre_r, jnp.sum(lfr, axis=1, keepdims=True) + lfr - pre_r)
        g_c = gi_ref[0, rows, :] - b_c
        g_r = gr[:ng] - b_r
        cg_c = jnp.where(fwd_c, _scan(g_c, jnp.maximum, -jnp.inf, 0, False),
                         _scan(g_c, jnp.maximum, -jnp.inf, 0, True))
        bc_s[rows, :] = b_c
        br_s[:, rows] = b_r
        ml_s[rows, :] = b_c + cg_c
        dl = jnp.zeros((ch, ng), F32)
        for hh in range(nh):
            sl = slice(hh * M_DH, (hh + 1) * M_DH)
            s_s[hh] = _dot(q_s[rows, sl], kt_s[sl, rows])
        for hh in range(nh):
            s = s_s[hh]
            for d in range(2):
                jj = d * nh + hh
                wgt = jnp.exp(jnp.where(upper if d else lower, g_r[jj:jj + 1, :] - cg_c[:, jj:jj + 1], -jnp.inf))
                p_s[hh, d * ch:(d + 1) * ch] = (s * wgt).astype(BF16)
        for hh in range(nh):
            sl = slice(hh * M_DH, (hh + 1) * M_DH)
            v1 = jnp.concatenate([vo_ref[0, rows, sl], ones_blk], axis=1)
            nd2 = _dot(p_s[hh], v1)
            for d in range(2):
                jj = d * nh + hh
                nd = nd2[d * ch:(d + 1) * ch]
                h_s[d, rows, sl] = nd[:, :M_DH]
                dl = jnp.where(lane_c == jj, nd[:, M_DH:M_DH + ng], dl)
        dl_s[rows, :] = dl

    def conv_local_body(j, carry):
        conv_chunk(j + 1)
        local_chunk(j)
        return carry

    conv_chunk(jnp.int32(0))
    lax.fori_loop(0, nc - 1, conv_local_body, 0)
    local_chunk(jnp.int32(nc - 1))

    cn_s[...] = jnp.zeros_like(cn_s)
    m_s[...] = jnp.zeros_like(m_s)
    lane_r = lax.broadcasted_iota(jnp.int32, (1, ng), 1)

    def scan_issue(r0, d):
        rows = pl.ds(r0, ch)
        gr = gr_ref[0, :, rows]
        br = br_s[:, rows]
        tot = jnp.sum(_log_sigmoid(gr[ng:]), axis=1, keepdims=True)
        scal = []
        for hh in range(nh):
            fi = d * nh + hh
            sl = slice(hh * M_DH, (hh + 1) * M_DH)
            qcn_s[fi] = _dot(q_s[rows, sl], cn_s[fi].astype(BF16))
            m_old = m_s[fi][:, 0:1]
            b_e = tot[fi:fi + 1, :]
            d_end = b_e - br[fi:fi + 1, :] + gr[fi:fi + 1, :]
            m_end = jnp.max(d_end, axis=-1, keepdims=True)
            m_new = jnp.maximum(b_e + m_old, m_end)
            ktw = (kt_s[sl, rows].astype(F32) * jnp.exp(d_end - m_end)).astype(BF16)
            v1 = jnp.concatenate([vo_ref[0, rows, sl], ones_blk], axis=1)
            u_s[fi] = _dot(ktw, v1)
            scal.append((m_old, m_new, jnp.exp(b_e + m_old - m_new), jnp.exp(m_end - m_new)))
        return scal

    def scan_finish(r0, d, scal):
        rows = pl.ds(r0, ch)
        m_row = jnp.zeros((1, ng), F32)
        for hh in range(nh):
            m_row = jnp.where(lane_r == d * nh + hh, scal[hh][0], m_row)

        inter = bc_s[rows, :] + m_row
        ml = ml_s[rows, :]
        mt = jnp.maximum(inter, ml)
        a = jnp.exp(ml - mt)
        wi = jnp.exp(inter - mt)
        qn = jnp.zeros((ch, ng), F32)
        for hh in range(nh):
            fi = d * nh + hh
            qn = jnp.where(lane_c == fi, qcn_s[fi, :, M_DH:M_DH + ng], qn)
        den = a * dl_s[rows, :] + wi * qn
        rinv = 1.0 / jnp.maximum(jnp.abs(den), jnp.exp(-mt))
        c_loc = a * rinv
        c_int = wi * rinv

        for hh in range(nh):
            fi = d * nh + hh
            sl = slice(hh * M_DH, (hh + 1) * M_DH)
            h_s[d, rows, sl] = (c_loc[:, fi:fi + 1] * h_s[d, rows, sl]
                                + c_int[:, fi:fi + 1] * qcn_s[fi, :, :M_DH])
            cn_s[fi] = scal[hh][2] * cn_s[fi] + scal[hh][3] * u_s[fi]
            m_s[fi] = jnp.broadcast_to(scal[hh][1], (1, M_DH))

    def scan_body(s, carry):
        rf = pl.multiple_of(s * ch, ch)
        rb = pl.multiple_of(jnp.where(s < ncc, ncc - 1 - s, nc - 1 - s + ncc) * ch, ch)
        sf = scan_issue(rf, 0)
        sb = scan_issue(rb, 1)
        scan_finish(rf, 0, sf)
        scan_finish(rb, 1, sb)
        return carry

    lax.fori_loop(0, nc, scan_body, 0)

    mnorm = mnorm_ref[...]

    def out_body(j, carry):
        r0 = pl.multiple_of(j * ch, ch)
        hsum = h_s[0, pl.ds(r0, ch), :] + h_s[1, pl.ds(r0, ch), :]
        og = _sigmoid(vo_ref[0, pl.ds(r0, ch), M_WIDTH:].astype(F32))
        for hh in range(M_HEADS):
            sl = slice(hh * M_DH, (hh + 1) * M_DH)
            ya_ref[0, pl.ds(r0, ch), sl] = (_rms(hsum[:, sl], mnorm[:, sl]) * og[:, sl]).astype(BF16)
        return carry

    lax.fori_loop(0, nc, out_body, 0)


def _mlstm_call(qk, vo, gi, gf, gr, conv, mnorm, lc):
    b, l, _ = qk.shape
    ng = 2 * M_HEADS
    return pl.pallas_call(
        functools.partial(_mlstm_kernel, lc=lc),
        grid=(b,),
        in_specs=[pl.BlockSpec((1, l, 2 * M_WIDTH), lambda bi: (bi, 0, 0), pipeline_mode=pl.Buffered(1)),
                  pl.BlockSpec((1, l, 2 * M_WIDTH), lambda bi: (bi, 0, 0)),
                  pl.BlockSpec((1, l, ng), lambda bi: (bi, 0, 0)),
                  pl.BlockSpec((1, l, ng), lambda bi: (bi, 0, 0)),
                  pl.BlockSpec((1, 2 * ng, l), lambda bi: (bi, 0, 0)),
                  _const_spec(conv.shape), _const_spec(mnorm.shape)],
        out_specs=pl.BlockSpec((1, l, M_WIDTH), lambda bi: (bi, 0, 0)),
        out_shape=jax.ShapeDtypeStruct((b, l, M_WIDTH), BF16),
        scratch_shapes=[pltpu.VMEM((l, M_WIDTH), BF16),
                        pltpu.VMEM((M_WIDTH, l), BF16),
                        pltpu.VMEM((2, l, M_WIDTH), F32),
                        pltpu.VMEM((l, ng), F32),
                        pltpu.VMEM((l, ng), F32),
                        pltpu.VMEM((l, ng), F32),
                        pltpu.VMEM((ng, l), F32),
                        pltpu.VMEM((2 * M_HEADS, M_DH, 2 * M_DH), F32),
                        pltpu.VMEM((2 * M_HEADS, 1, M_DH), F32),
                        pltpu.VMEM((M_HEADS, M_CHUNK, M_CHUNK), F32),
                        pltpu.VMEM((M_HEADS, 2 * M_CHUNK, M_CHUNK), BF16),
                        pltpu.VMEM((2 * M_HEADS, M_CHUNK, 2 * M_DH), F32),
                        pltpu.VMEM((2 * M_HEADS, M_DH, 2 * M_DH), F32)],
        compiler_params=_params("parallel"), name="mlstm",
    )(qk, vo, gi, gf, gr, conv, mnorm)


def _attn_kernel(q_ref, k_ref, v_ref, o_ref, s_s, p_s, *, lc, ctx_out):
    tq = q_ref.shape[1]
    l = k_ref.shape[1]
    qi = pl.program_id(2)
    nct = lc // tq
    lane = lax.broadcasted_iota(jnp.int32, (tq, 2 * A_VDIM), 1)

    def run(klen):
        outs = []
        for hh in range(A_HPS):
            sl = slice(hh * A_PAD, (hh + 1) * A_PAD)
            s_s[hh, :, :klen] = _dot_nt(q_ref[0, :, sl], k_ref[0, :klen, sl])
        row_max = [jnp.max(s_s[hh, :, :klen], axis=-1, keepdims=True) for hh in range(A_HPS)]
        for hh in range(A_HPS):
            p_s[hh, :, :klen] = jnp.exp2((s_s[hh, :, :klen] - row_max[hh]).astype(BF16))
        for hh in range(A_HPS):
            sl = slice(hh * A_PAD, (hh + 1) * A_PAD)
            nd = _dot(p_s[hh, :, :klen], v_ref[0, :klen, sl])
            outs.append(nd / pltpu.roll(nd, A_VDIM, 1))
        for pp in range(A_HPS // 2):
            o_ref[0, :, pp * A_PAD:(pp + 1) * A_PAD] = jnp.where(
                lane < A_VDIM, outs[2 * pp], pltpu.roll(outs[2 * pp + 1], A_VDIM, 1)).astype(BF16)

    @pl.when(qi >= nct)
    def _():
        run(l)

    @pl.when(qi < nct)
    def _():
        if ctx_out:
            run(lc)
        else:
            o_ref[...] = jnp.zeros_like(o_ref)


def _attn_call(q, k, v, lc, tq, ctx_out):
    b, l, _ = q.shape
    return pl.pallas_call(
        functools.partial(_attn_kernel, lc=lc, ctx_out=ctx_out),
        grid=(b, A_HEADS // A_HPS, l // tq),
        in_specs=[pl.BlockSpec((1, tq, A_HPS * A_PAD), lambda bi, p, qi: (bi, qi, p)),
                  pl.BlockSpec((1, l, A_HPS * A_PAD), lambda bi, p, qi: (bi, 0, p)),
                  pl.BlockSpec((1, l, A_HPS * A_PAD), lambda bi, p, qi: (bi, 0, p))],
        out_specs=pl.BlockSpec((1, tq, A_HPS * A_VDIM), lambda bi, p, qi: (bi, qi, p)),
        out_shape=jax.ShapeDtypeStruct((b, l, A_WIDTH), BF16),
        scratch_shapes=[pltpu.VMEM((A_HPS, tq, l), F32), pltpu.VMEM((A_HPS, tq, l), BF16)],
        compiler_params=_params("parallel", "parallel", "arbitrary"), name="attn",
    )(q, k, v)


def _merge_kernel(x_ref, mod_ref, modc_ref, ya_ref, yb_ref, yc_ref, br_ref, wpa_ref, wpb_ref, wpc_ref, wout_ref,
                  n2_ref, r_ref, rb_ref, o_ref, h2_o, idx_o, idxt_o, cnt_o, y_s, *, tm, nct, skip_ctx):
    d = x_ref.shape[2]
    nsub = x_ref.shape[1] // tm
    modb = mod_ref[0]
    modc = modc_ref[0]
    tiles = [slice(s * tm, (s + 1) * tm) for s in range(nsub)]
    ctx = [pl.program_id(1) * nsub + s < nct for s in range(nsub)]
    mods = [jnp.where(c, modc, modb) for c in ctx]
    for s, rows in enumerate(tiles):
        br = br_ref[0, rows, :]
        y = (br[:, :d].astype(F32) * _dot(ya_ref[0, rows, :], wpa_ref[...])
             + br[:, d:2 * d].astype(F32) * _dot(yb_ref[0, rows, :], wpb_ref[...])
             + br[:, 2 * d:].astype(F32) * _dot(yc_ref[0, rows, :], wpc_ref[...]))
        y_s[s] = y.astype(BF16)
    for s, rows in enumerate(tiles):
        o_ref[0, rows, :] = x_ref[0, rows, :] + mods[s][2:3] * _dot(y_s[s], wout_ref[...])
    for s, rows in enumerate(tiles):
        logits = _route_logits(o_ref[0, rows, :], mods[s], n2_ref, r_ref, rb_ref, h2_o, rows)
        left_out = ctx[s] if skip_ctx else None
        _route_assign(logits, left_out, d, h2_o, idx_o, idxt_o, cnt_o, rows, s)


def _merge_call(xs, mod, ya, yb, yc, br, w, n2, r3, rb, lc, tm, skip_ctx):
    b, l, d = xs.shape
    nct = lc // tm
    nsub = next(n for n in (3, 2, 1) if l % (n * tm) == 0)
    tg = nsub * tm
    tok = lambda width: pl.BlockSpec((1, tg, width), lambda bi, j: (bi, j, 0))
    consts = [w["wpa"], w["wpb"], w["wpc"], w["wout"], n2, r3, rb]
    return pl.pallas_call(
        functools.partial(_merge_kernel, tm=tm, nct=nct, skip_ctx=skip_ctx), grid=(b, l // tg),
        in_specs=[tok(d), pl.BlockSpec((1, 6, d), lambda bi, j: (bi, 0, 0)),
                  pl.BlockSpec((1, 6, d), lambda bi, j: (b, 0, 0)),
                  tok(M_WIDTH), tok(A_WIDTH), tok(G_WIDTH), tok(3 * d)] + [_const_spec(a.shape) for a in consts],
        out_specs=[tok(d), tok(d + R_PAD), tok(8), pl.BlockSpec((1, 8, tg), lambda bi, j: (bi, 0, j)),
                   pl.BlockSpec((1, nsub, 8, R_PAD), lambda bi, j: (bi, j, 0, 0))],
        out_shape=[jax.ShapeDtypeStruct((b, l, d), F32),
                   jax.ShapeDtypeStruct((b, l, d + R_PAD), BF16),
                   jax.ShapeDtypeStruct((b, l, 8), jnp.int32), jax.ShapeDtypeStruct((b, 8, l), jnp.int32),
                   jax.ShapeDtypeStruct((b, l // tm, 8, R_PAD), F32)],
        scratch_shapes=[pltpu.VMEM((nsub, tm, d), BF16)],
        compiler_params=_params("parallel", "parallel"), name="merge",
    )(xs, mod, mod, ya, yb, yc, br, *consts)


def _route_logits(x, mod, n2_ref, r_ref, rb_ref, h2_o, rows):
    d = x.shape[1]
    h2 = _rms(x, n2_ref[...]) * (1.0 + mod[4:5]) + mod[3:4]
    h2_o[0, rows, :d] = h2.astype(BF16)
    r = r_ref[...]
    pp = sum(_dot(piece, r) for piece in _split3(h2)[:2])
    return pp + pltpu.roll(pp, R_PAD - R_SEG, 1) + pltpu.roll(pp, R_PAD - 2 * R_SEG, 1) + rb_ref[...]


def _route_assign(logits, left_out, d, h2_o, idx_o, idxt_o, cnt_o, rows, s):
    tm = logits.shape[0]
    el = logits[:, :N_EXPERTS]
    gl = logits[:, N_EXPERTS:N_EXPERTS + N_GROUPS]
    big = 1e9

    lane_g = lax.broadcasted_iota(jnp.int32, (tm, N_GROUPS), 1).astype(F32)
    gmax = jnp.max(gl, axis=-1, keepdims=True)
    g_sel = jnp.min(jnp.where(gl == gmax, lane_g, big), axis=-1, keepdims=True)
    g_prob = 1.0 / jnp.sum(jnp.exp(gl - gmax), axis=-1, keepdims=True)

    lane_i = lax.broadcasted_iota(jnp.int32, (tm, N_EXPERTS), 1)
    lane_e = lane_i.astype(F32)
    lane_grp = (lane_i // EXP_PER_GROUP).astype(F32)
    v1 = jnp.where(lane_grp == g_sel, el, -jnp.inf)
    t1 = jnp.max(v1, axis=-1, keepdims=True)
    i1 = jnp.min(jnp.where(v1 == t1, lane_e, big), axis=-1, keepdims=True)
    v2 = jnp.where(lane_e == i1, -jnp.inf, v1)
    t2 = jnp.max(v2, axis=-1, keepdims=True)
    i2 = jnp.min(jnp.where(v2 == t2, lane_e, big), axis=-1, keepdims=True)
    e21 = jnp.exp(t2 - t1)
    w1 = 1.0 / (1.0 + e21)
    w2 = e21 * w1
    comb = (jnp.where(lane_e == i1, w1, 0.0) + jnp.where(lane_e == i2, w2, 0.0)) * g_prob
    tail = jnp.zeros((tm, R_PAD - 3 * N_EXPERTS), BF16)
    h2_o[0, rows, d:] = jnp.concatenate(list(_split3(comb)) + [tail], axis=1)

    if left_out is not None:
        g_sel = jnp.where(left_out, -1.0, g_sel)
    lane_p = lax.broadcasted_iota(jnp.int32, (tm, R_PAD), 1)
    onehot = jnp.where(lane_p.astype(F32) == g_sel, 1.0, 0.0)
    ri = lax.broadcasted_iota(jnp.int32, (tm, tm), 0)
    ci = lax.broadcasted_iota(jnp.int32, (tm, tm), 1)
    before = jnp.where(ri > ci, 1.0, 0.0).astype(BF16)
    rank = jnp.sum(_dot(before, onehot.astype(BF16)) * onehot, axis=-1, keepdims=True)
    cnt_o[0, s] = jnp.broadcast_to(jnp.sum(onehot, axis=0, keepdims=True), (8, R_PAD))
    fields = jnp.where(lane_p == 0, g_sel, jnp.where(lane_p == 1, rank, 0.0))
    idx_o[0, rows, :] = fields[:, :8].astype(jnp.int32)
    idxt_o[0, :, rows] = fields.T[:8, :].astype(jnp.int32)


def _experts_kernel(st_ref, h2_ref, idx_ref, idxt_ref, w1_ref, w3_ref, w2_ref, o_ref, hs_s, ys_s, a_s, hid_s, *, tb):
    l = h2_ref.shape[1]
    d = o_ref.shape[2]
    nblk = l // tb
    ch = MOE_CHUNK
    bi = pl.program_id(0)
    g = pl.program_id(1)

    @pl.when(g == 0)
    def _():
        ys_s[...] = jnp.zeros_like(ys_s)

    def group_base(gg):
        return (bi * N_GROUPS + gg) * (nblk + 1)

    def group_offset(upto):
        off = 0
        for gg in range(N_GROUPS - 1):
            padded = ((st_ref[group_base(gg) + nblk] + ch - 1) // ch) * ch
            off = off + jnp.where(gg < upto, padded, 0)
        return off

    base = group_base(g)
    cnt = st_ref[base + nblk]
    goff = group_offset(g)
    gw = min(MOE_GATHER_BLOCKS, nblk)
    nwin = -(-nblk // gw)

    def chunk(lo, ch):
        lane_e = lax.broadcasted_iota(jnp.int32, (ch, N_EXPERTS), 1)
        sub_iota = lax.broadcasted_iota(jnp.int32, (ch, gw * tb), 0)
        lane_blk = lax.broadcasted_iota(jnp.int32, (1, gw * tb), 1) // tb
        k0 = 0
        for k in range(nblk):
            k0 = k0 + jnp.where(st_ref[base + k + 1] <= lo, 1, 0)

        def window(j):
            want = k0 + j * gw
            kj = jnp.minimum(want, nblk - gw)
            tok0 = pl.multiple_of(kj * tb, tb)
            it = idxt_ref[0, :, pl.ds(tok0, gw * tb)]
            offs = jnp.concatenate([jnp.full((1, tb), st_ref[base + kj + i] - lo, jnp.int32) for i in range(gw)],
                                   axis=1)
            mine = jnp.where(lane_blk >= want - kj, it[0:1], -1)
            pos = jnp.where(mine == g, it[1:2] + offs, -1)
            p = jnp.where(sub_iota == pos, 1.0, 0.0).astype(BF16)
            return _dot(p, h2_ref[0, pl.ds(tok0, gw * tb), :])

        hs_s[:ch] = window(0)
        for j in range(1, nwin):
            first = k0 + j * gw

            @pl.when(jnp.logical_and(first < nblk, st_ref[base + jnp.minimum(first, nblk)] < lo + ch))
            def _(j=j):
                hs_s[:ch] += window(j)

        hsb = hs_s[:ch, :d].astype(BF16)
        cs = (hs_s[:ch, d:d + N_EXPERTS] + hs_s[:ch, d + N_EXPERTS:d + 2 * N_EXPERTS]
              + hs_s[:ch, d + 2 * N_EXPERTS:d + 3 * N_EXPERTS])
        for e in range(EXP_PER_GROUP):
            a_s[2 * e, :ch] = _dot(hsb, w1_ref[e])
            a_s[2 * e + 1, :ch] = _dot(hsb, w3_ref[e])
        for e in range(EXP_PER_GROUP):
            ce = jnp.sum(jnp.where(lane_e == g * EXP_PER_GROUP + e, cs, 0.0), axis=-1, keepdims=True)
            hid_s[e, :ch] = (_silu(a_s[2 * e, :ch]) * a_s[2 * e + 1, :ch] * ce).astype(BF16)
        y = jnp.zeros((ch, d), F32)
        for e in range(EXP_PER_GROUP):
            y = y + _dot(hid_s[e, :ch], w2_ref[e])
        ys_s[pl.ds(pl.multiple_of(goff + lo, ch), ch), :] = y.astype(BF16)

    half = ch // 2
    nfull = cnt // ch
    rem = cnt - nfull * ch
    nloop = nfull + jnp.where(rem > half, 1, 0)

    def chunk_body(c, carry):
        chunk(c * ch, ch)
        return carry

    lax.fori_loop(0, nloop, chunk_body, 0)

    @pl.when(jnp.logical_and(rem > 0, rem <= half))
    def _():
        chunk(nfull * ch, half)

    @pl.when(g == N_GROUPS - 1)
    def _():
        lane_w = lax.broadcasted_iota(jnp.int32, (tb, ch), 1)
        goffs = [group_offset(gg) for gg in range(N_GROUPS)]

        def window(k, gg, shift):
            s_k = st_ref[group_base(gg) + k]
            win = (s_k // half) * half
            ic = idx_ref[0, k * tb:(k + 1) * tb, :]
            pos = jnp.where(ic[:, 0:1] == gg, ic[:, 1:2] + (s_k - win - shift), -1)
            q = jnp.where(lane_w == pos, 1.0, 0.0).astype(BF16)
            start = pl.multiple_of(goffs[gg] + win + shift, half)
            return _dot(q, ys_s[pl.ds(start, ch), :])

        for k in range(nblk):
            acc = window(k, 0, 0)
            for gg in range(1, N_GROUPS):
                acc = acc + window(k, gg, 0)
            o_ref[0, k * tb:(k + 1) * tb, :] = acc.astype(BF16)
        for k in range(nblk):
            rows = slice(k * tb, (k + 1) * tb)
            for gg in range(N_GROUPS):
                s_k = st_ref[group_base(gg) + k]
                e_k = st_ref[group_base(gg) + k + 1]

                @pl.when(e_k - (s_k // half) * half > ch)
                def _(k=k, gg=gg, rows=rows):
                    o_ref[0, rows, :] = (o_ref[0, rows, :].astype(F32) + window(k, gg, ch)).astype(BF16)


def _experts_call(starts, h2, idx, idxt, w1, w3, w2, layer, tb):
    b, l, de = h2.shape
    d = de - R_PAD
    whole = lambda width: pl.BlockSpec((1, l, width), lambda bi, g, st: (bi, 0, 0))
    grid_spec = pltpu.PrefetchScalarGridSpec(
        num_scalar_prefetch=1, grid=(b, N_GROUPS),
        in_specs=[pl.BlockSpec((1, l, de), lambda bi, g, st: (bi, 0, 0), pipeline_mode=pl.Buffered(1)),
                  whole(8), pl.BlockSpec((1, 8, l), lambda bi, g, st: (bi, 0, 0)),
                  pl.BlockSpec((None, EXP_PER_GROUP, d, D_EXPERT), lambda bi, g, st: (layer, g, 0, 0)),
                  pl.BlockSpec((None, EXP_PER_GROUP, d, D_EXPERT), lambda bi, g, st: (layer, g, 0, 0)),
                  pl.BlockSpec((None, EXP_PER_GROUP, D_EXPERT, d), lambda bi, g, st: (layer, g, 0, 0))],
        out_specs=whole(d),
        scratch_shapes=[pltpu.VMEM((MOE_CHUNK, de), F32),
                        pltpu.VMEM((l + (N_GROUPS + 1) * MOE_CHUNK, d), BF16),
                        pltpu.VMEM((2 * EXP_PER_GROUP, MOE_CHUNK, D_EXPERT), F32),
                        pltpu.VMEM((EXP_PER_GROUP, MOE_CHUNK, D_EXPERT), BF16)])
    return pl.pallas_call(
        functools.partial(_experts_kernel, tb=tb), grid_spec=grid_spec,
        out_shape=jax.ShapeDtypeStruct((b, l, d), BF16),
        compiler_params=_params("parallel", "arbitrary"), name="experts",
    )(starts, h2, idx, idxt, w1, w3, w2)


def _group_starts(cnt):
    c = cnt[:, :, 0, :N_GROUPS].astype(jnp.int32)
    s = jnp.cumsum(c, axis=1)
    s = jnp.concatenate([jnp.zeros_like(s[:, :1]), s], axis=1)
    return jnp.transpose(s, (0, 2, 1)).reshape(-1)


def _final_kernel(x_ref, mod_ref, f_ref, g_ref, o_ref):
    o_ref[0] = _rms(x_ref[0] + mod_ref[0][5:6] * f_ref[0].astype(F32), g_ref[...])


def _final_call(xs, mod, f, g, lc, tm):
    b, l, d = xs.shape
    off = lc // tm
    lat = pl.BlockSpec((1, tm, d), lambda bi, j: (bi, j + off, 0))
    return pl.pallas_call(
        _final_kernel, grid=(b, (l - lc) // tm),
        in_specs=[lat, pl.BlockSpec((1, 6, d), lambda bi, j: (bi, 0, 0)), lat, _const_spec(g.shape)],
        out_specs=pl.BlockSpec((1, tm, d), lambda bi, j: (bi, j, 0)),
        out_shape=jax.ShapeDtypeStruct((b, l - lc, d), F32),
        compiler_params=_params("parallel", "parallel"), name="final_norm",
    )(xs, mod, f, g)


def _rope_tables(t_len, lc):
    half = A_ROPE // 2
    rows = t_len // GRID_W
    r = jnp.repeat(jnp.arange(rows, dtype=F32), GRID_W)
    col = jnp.tile(jnp.arange(GRID_W, dtype=F32), rows)
    inv = ROPE_THETA ** (-jnp.arange(0, half, 2, dtype=F32) / half)
    ang = jnp.concatenate([r[:, None] * inv, col[:, None] * inv], axis=-1)
    cos = jnp.concatenate([jnp.ones((lc, half), F32), jnp.cos(ang)], axis=0)
    sin = jnp.concatenate([jnp.zeros((lc, half), F32), jnp.sin(ang)], axis=0)
    l = lc + t_len
    ones = jnp.ones((l, A_NOPE), F32)
    zeros = jnp.zeros((l, A_NOPE), F32)
    tail1 = jnp.ones((l, A_PAD - A_NOPE - A_ROPE), F32)
    tail0 = jnp.zeros((l, A_PAD - A_NOPE - A_ROPE), F32)
    zh = jnp.zeros((l, half), F32)
    cos_t = jnp.concatenate([ones, cos, cos, tail1], axis=-1)
    sina_t = jnp.concatenate([zeros, zh, sin, tail0], axis=-1)
    sinb_t = jnp.concatenate([zeros, -sin, zh, tail0], axis=-1)
    return cos_t, sina_t, sinb_t


def _layer_weights(l, w_in, m_gate_b, a_qnorm, a_wuq, a_kvnorm, a_wukv, g_ws, g_bs, g_vnorm,
                   w_pa, w_pb, w_pc, w_out):
    d = w_in.shape[1]
    wi = w_in[l]
    o = 0

    def take(n):
        nonlocal o
        s = wi[:, o:o + n]
        o += n
        return s

    mq, mk, mv, mo, mg = take(M_WIDTH), take(M_WIDTH), take(M_WIDTH), take(M_WIDTH), take(4 * M_HEADS)
    aq, akv, akr = take(A_QRANK), take(A_KVRANK), take(A_ROPE)
    gu, gv = take(G_WIDTH), take(G_WIDTH)
    br = take(3 * d)
    nh = M_HEADS
    gb = m_gate_b[l]
    mgo = jnp.concatenate([mg[:, :nh], mg[:, 2 * nh:3 * nh], mg[:, nh:2 * nh], mg[:, 3 * nh:]], axis=1)
    gbo = jnp.concatenate([gb[:nh], gb[2 * nh:3 * nh], gb[nh:2 * nh], gb[3 * nh:]])
    akr_pad = jnp.concatenate([jnp.zeros((d, A_NOPE), F32), akr,
                               jnp.zeros((d, A_PAD - A_NOPE - A_ROPE), F32)], axis=1)
    wuq = a_wuq[l].reshape(A_QRANK, A_HEADS, A_NOPE + A_ROPE)
    wuq = jnp.pad(wuq, ((0, 0), (0, 0), (0, A_PAD - A_NOPE - A_ROPE))).reshape(A_QRANK, A_HEADS * A_PAD)
    wukv = a_wukv[l].reshape(A_KVRANK, A_HEADS, A_NOPE + A_VDIM)
    wuk = jnp.pad(wukv[:, :, :A_NOPE], ((0, 0), (0, 0), (0, A_PAD - A_NOPE))).reshape(A_KVRANK, A_HEADS * A_PAD)
    wuv = jnp.pad(wukv[:, :, A_NOPE:], ((0, 0), (0, 0), (0, A_PAD - A_VDIM))).reshape(A_KVRANK, A_HEADS * A_PAD)
    vone = jnp.tile(jnp.concatenate([jnp.zeros((A_VDIM,), F32), jnp.ones((A_PAD - A_VDIM,), F32)]),
                    A_HEADS).reshape(1, A_HEADS * A_PAD)
    gbs = jnp.repeat(g_bs[l].T, G_DG, axis=1)
    return dict(
        wqk=jnp.concatenate([mq, mk], 1).astype(BF16), wvo=jnp.concatenate([mv, mo], 1).astype(BF16),
        wgt=jnp.pad(mgo, ((0, 0), (0, G_PAD - 4 * nh))).astype(BF16),
        gbt=jnp.pad(gbo, (0, G_PAD - 4 * nh)).reshape(1, G_PAD),
        wa=jnp.concatenate([aq, akv, akr_pad], 1).astype(BF16),
        wg=jnp.concatenate([gu, gv], 1).astype(BF16), wbr=br.astype(BF16),
        aqn=a_qnorm[l].reshape(1, -1), akvn=a_kvnorm[l].reshape(1, -1),
        wuq=wuq.astype(BF16), wuk=wuk.astype(BF16), wuv=wuv.astype(BF16), vone=vone,
        gvn=g_vnorm[l].reshape(1, -1), gws=g_ws[l].astype(BF16), gbs=gbs,
        wpa=w_pa[l].astype(BF16), wpb=w_pb[l].astype(BF16), wpc=w_pc[l].astype(BF16),
        wout=w_out[l].astype(BF16))


def _router_weights(r_group, r_group_b, r_expert, r_expert_b):
    d = r_group.shape[0]
    pad = R_SEG - N_EXPERTS - N_GROUPS
    r = jnp.concatenate([r_expert, r_group, jnp.zeros((d, pad), F32)], axis=1)
    r3 = jnp.concatenate(list(_split3(r)) + [jnp.zeros((d, R_PAD - 3 * R_SEG), BF16)], axis=1)
    rb = jnp.concatenate([r_expert_b, r_group_b, jnp.zeros((R_PAD - N_EXPERTS - N_GROUPS,), F32)])
    return r3, rb.reshape(1, R_PAD)


def _tile(n, lc, candidates):
    for t in candidates:
        if n % t == 0 and lc % t == 0:
            return t
    raise ValueError("sequence lengths must be multiples of 128")


def kernel(x, c, ctx, c_ctx, w_ada, b_ada, norm1, norm2, final_norm, w_in, m_conv, m_gate_b, m_norm, a_qnorm, a_wuq, a_kvnorm, a_wukv, g_ws, g_bs, g_vnorm, w_pa, w_pb, w_pc, w_out, r_group, r_group_b, r_expert, r_expert_b, e_w1, e_w3, e_w2):
    b, t_len, d = x.shape
    lc = ctx.shape[1]
    l = lc + t_len
    depth = w_in.shape[0]
    tm = _tile(l, lc, (256, 128))

    xs = jnp.concatenate([ctx, x], axis=1)
    cv = jnp.concatenate([c, c_ctx[None, :]], axis=0)
    mod_all = _ada_call(cv, w_ada, b_ada).reshape(depth, b + 1, 6, d)
    tabs = _rope_tables(t_len, lc)

    w_in_b = w_in.astype(BF16)
    e_w1_b, e_w3_b, e_w2_b = e_w1.astype(BF16), e_w3.astype(BF16), e_w2.astype(BF16)
    f = None
    mod_prev = None
    for li in range(depth):
        last = li == depth - 1
        mod = mod_all[li]
        w = _layer_weights(li, w_in_b, m_gate_b, a_qnorm, a_wuq, a_kvnorm, a_wukv, g_ws, g_bs, g_vnorm,
                           w_pa, w_pb, w_pc, w_out)
        outs = _inproj_call(xs, mod, norm1[li].reshape(1, d), w, tabs, lc, tm, f, mod_prev)
        qk, vo, gi, gf, gr, q, k, v, yc, br = outs[:10]
        if f is not None:
            xs = outs[10]
        ya = _mlstm_call(qk, vo, gi, gf, gr, m_conv[li], m_norm[li].reshape(1, -1), lc)
        yb = _attn_call(q, k, v, lc, tm, not last)
        r3, rb = _router_weights(r_group[li], r_group_b[li], r_expert[li], r_expert_b[li])
        xs, h2, idx, idxt, cnt = _merge_call(xs, mod, ya, yb, yc, br, w, norm2[li].reshape(1, d),
                                             r3, rb, lc, tm, last)
        f = _experts_call(_group_starts(cnt), h2, idx, idxt, e_w1_b, e_w3_b, e_w2_b, li, tm)
        mod_prev = mod
    return _final_call(xs, mod_prev, f, final_norm.reshape(1, d), lc, tm)
```

```python
import functools

import jax
import jax.numpy as jnp
from jax import lax
from jax.experimental import pallas as pl
from jax.experimental.pallas import tpu as pltpu

F32 = jnp.float32
BF16 = jnp.bfloat16

EPS = 1e-6
GRID_W = 64
ROPE_THETA = 10000.0

M_HEADS = 4
M_DH = 128
M_WIDTH = M_HEADS * M_DH
M_CHUNK = 128
G_PAD = 128

A_HEADS = 8
A_NOPE = 64
A_ROPE = 32
A_VDIM = 64
A_QRANK = 384
A_KVRANK = 256
A_WIDTH = A_HEADS * A_VDIM
A_PAD = 128
A_HPS = 4
ATT_SCALE = (A_NOPE + A_ROPE) ** -0.5
LOG2E = 1.4426950408889634

G_GROUPS = 4
G_CHUNK = 128
G_WIDTH = 512
G_DG = G_WIDTH // G_GROUPS

N_GROUPS = 4
EXP_PER_GROUP = 4
N_EXPERTS = N_GROUPS * EXP_PER_GROUP
D_EXPERT = 512
R_PAD = 128
R_SEG = 32
MOE_CHUNK = 256
MOE_GATHER_BLOCKS = 5

VMEM_LIMIT = 56 * 1024 * 1024


def _dot(a, b):
    return jnp.dot(a, b, preferred_element_type=F32)


def _dot_nt(a, b):
    return lax.dot_general(a, b, (((1,), (1,)), ((), ())), preferred_element_type=F32)


def _split3(x):
    hi = x.astype(BF16)
    r = x - hi.astype(F32)
    mid = r.astype(BF16)
    lo = (r - mid.astype(F32)).astype(BF16)
    return hi, mid, lo


def _sigmoid(x):
    return 1.0 / (1.0 + jnp.exp(-x))


def _silu(x):
    return x * _sigmoid(x)


def _log_sigmoid(x):
    return jnp.minimum(x, 0.0) - jnp.log1p(jnp.exp(-jnp.abs(x)))


def _gelu(x):
    return 0.5 * x * (1.0 + lax.erf(x * (2.0 ** -0.5)))


def _rms(x, g):
    return x * lax.rsqrt(jnp.mean(x * x, axis=-1, keepdims=True) + EPS) * g


def _params(*sem):
    return pltpu.CompilerParams(dimension_semantics=sem, vmem_limit_bytes=VMEM_LIMIT)


def _const_spec(shape):
    nd = len(shape)
    return pl.BlockSpec(shape, lambda *_: (0,) * nd, pipeline_mode=pl.Buffered(1))


def _ada_kernel(cv_ref, w_ref, b_ref, o_ref):
    s = _silu(cv_ref[...])
    o_ref[0] = _dot(s.astype(BF16), w_ref[0].astype(BF16)) + b_ref[0]


def _ada_call(cv, w_ada, b_ada):
    depth, d, n6 = w_ada.shape
    rows = cv.shape[0]
    tn = n6 // 4
    return pl.pallas_call(
        _ada_kernel,
        grid=(depth, n6 // tn),
        in_specs=[pl.BlockSpec((rows, d), lambda l, j: (0, 0)),
                  pl.BlockSpec((1, d, tn), lambda l, j: (l, 0, j)),
                  pl.BlockSpec((1, 1, tn), lambda l, j: (l, 0, j))],
        out_specs=pl.BlockSpec((1, rows, tn), lambda l, j: (l, 0, j)),
        out_shape=jax.ShapeDtypeStruct((depth, rows, n6), F32),
        compiler_params=_params("parallel", "parallel"),
        name="ada",
    )(cv, w_ada, b_ada.reshape(depth, 1, n6))


def _inproj_kernel(*refs, has_f, nct_first):
    za_s, zg_s, zbr_s = refs[-3:]
    refs = refs[:-3]
    if nct_first is not None:
        ctx_ref, x_o = refs[0], refs[-1]
        refs = refs[1:-1]
    if has_f:
        f_ref, modp_ref, x_o = refs[0], refs[1], refs[-1]
        refs = refs[2:-1]
    (x_ref, mod_ref, n1_ref, wqk_ref, wvo_ref, wgt_ref, gbt_ref,
     wa_ref, wg_ref, wbr_ref, aqn_ref, akvn_ref, wuq_ref, wuk_ref, wuv_ref, vone_ref,
     cos_ref, sina_ref, sinb_ref, gvn_ref, gws_ref, gbs_ref,
     qk_o, vo_o, gi_o, gf_o, gr_o, q_o, k_o, v_o, yc_o, br_o) = refs
    tm = x_ref.shape[1]
    mod = mod_ref[0]
    x = x_ref[0]
    if nct_first is not None:
        x = jnp.where(pl.program_id(1) < nct_first, ctx_ref[0], x)
        x_o[0] = x
    if has_f:
        x = x + modp_ref[0][5:6] * f_ref[0].astype(F32)
        x_o[0] = x
    h = _rms(x, n1_ref[...]) * (1.0 + mod[1:2]) + mod[0:1]
    hb = h.astype(BF16)

    qk_o[0] = _dot(hb, wqk_ref[...])
    vo_o[0] = _dot(hb, wvo_ref[...]).astype(BF16)
    ng = gi_o.shape[2]
    gates = _dot(hb, wgt_ref[...]) + gbt_ref[...]
    gi_o[0] = gates[:, :ng]
    gf_o[0] = pltpu.roll(gates, gates.shape[1] - ng, 1)[:, :ng]
    gr_o[0] = gates.T[:2 * ng, :]

    za_s[...] = _dot(hb, wa_ref[...])
    zg_s[...] = _dot(hb, wg_ref[...])
    zbr_s[...] = _dot(hb, wbr_ref[...])

    aqn = _rms(za_s[:, :A_QRANK], aqn_ref[...]).astype(BF16)
    akvn = _rms(za_s[:, A_QRANK:A_QRANK + A_KVRANK], akvn_ref[...]).astype(BF16)
    cos = cos_ref[...]
    sina = sina_ref[...]
    sinb = sinb_ref[...]
    half = A_ROPE // 2

    def rope(t):
        return t * cos + pltpu.roll(t, half, 1) * sina + pltpu.roll(t, A_PAD - half, 1) * sinb

    kr = rope(za_s[:, A_QRANK + A_KVRANK:])
    qp = _dot(aqn, wuq_ref[...])
    kp = _dot(akvn, wuk_ref[...])
    for hh in range(A_HEADS):
        sl = slice(hh * A_PAD, (hh + 1) * A_PAD)
        q_o[0, :, sl] = (rope(qp[:, sl]) * (ATT_SCALE * LOG2E)).astype(BF16)
        k_o[0, :, sl] = (kp[:, sl] + kr).astype(BF16)
    v_o[0] = (_dot(akvn, wuv_ref[...]) + vone_ref[...]).astype(BF16)

    gu = _gelu(zg_s[:, :G_WIDTH])
    gv = _gelu(zg_s[:, G_WIDTH:])
    gvn = gvn_ref[...]
    bias = gbs_ref[...]
    for g in range(G_GROUPS):
        sl = slice(g * G_DG, (g + 1) * G_DG)
        xn = _rms(gv[:, sl], gvn[:, sl]).astype(BF16)
        ws = gws_ref[g]
        for ci in range(tm // G_CHUNK):
            r = slice(ci * G_CHUNK, (ci + 1) * G_CHUNK)
            sg = _dot(ws, xn[r]) + bias[:, sl]
            yc_o[0, r, sl] = (gu[r, sl] * sg).astype(BF16)

    br_o[0] = _sigmoid(zbr_s[...]).astype(BF16)


def _inproj_call(xs, mod, n1, w, tabs, lc, tm, f=None, mod_prev=None):
    first = isinstance(xs, tuple)
    if first:
        ctx, xs = xs
        b, t_len, d = xs.shape
        l = lc + t_len
    else:
        b, l, d = xs.shape
    nct = lc // tm
    has_f = f is not None
    tok = lambda width: pl.BlockSpec((1, tm, width), lambda bi, j: (bi, j, 0))
    modspec = pl.BlockSpec((1, 6, d), lambda bi, j: (jnp.where(j < nct, b, bi), 0, 0))
    tab = pl.BlockSpec((tm, A_PAD), lambda bi, j: (j, 0))
    consts = [n1, w["wqk"], w["wvo"], w["wgt"], w["gbt"], w["wa"], w["wg"], w["wbr"],
              w["aqn"], w["akvn"], w["wuq"], w["wuk"], w["wuv"], w["vone"]]
    consts2 = [w["gvn"], w["gws"], w["gbs"]]
    in_specs = ([tok(d), modspec] + [_const_spec(a.shape) for a in consts] + [tab, tab, tab]
                + [_const_spec(a.shape) for a in consts2])
    args = [xs, mod, *consts, *tabs, *consts2]
    ng = 4 * M_HEADS
    out_shape = [jax.ShapeDtypeStruct((b, l, 2 * M_WIDTH), F32),
                 jax.ShapeDtypeStruct((b, l, 2 * M_WIDTH), BF16),
                 jax.ShapeDtypeStruct((b, l, ng // 2), F32),
                 jax.ShapeDtypeStruct((b, l, ng // 2), F32),
                 jax.ShapeDtypeStruct((b, ng, l), F32),
                 jax.ShapeDtypeStruct((b, l, A_HEADS * A_PAD), BF16),
                 jax.ShapeDtypeStruct((b, l, A_HEADS * A_PAD), BF16),
                 jax.ShapeDtypeStruct((b, l, A_HEADS * A_PAD), BF16),
                 jax.ShapeDtypeStruct((b, l, G_WIDTH), BF16),
                 jax.ShapeDtypeStruct((b, l, 3 * d), BF16)]
    out_specs = [tok(2 * M_WIDTH), tok(2 * M_WIDTH), tok(ng // 2), tok(ng // 2),
                 pl.BlockSpec((1, ng, tm), lambda bi, j: (bi, 0, j)),
                 tok(A_HEADS * A_PAD), tok(A_HEADS * A_PAD), tok(A_HEADS * A_PAD), tok(G_WIDTH), tok(3 * d)]
    if has_f:
        in_specs = [tok(d), modspec] + in_specs
        args = [f, mod_prev] + args
        out_shape.append(jax.ShapeDtypeStruct((b, l, d), F32))
        out_specs.append(tok(d))
    if first:
        in_specs[0] = pl.BlockSpec((1, tm, d), lambda bi, j: (bi, jnp.maximum(j - nct, 0), 0))
        in_specs = [pl.BlockSpec((1, tm, d), lambda bi, j: (bi, jnp.minimum(j, nct - 1), 0))] + in_specs
        args = [ctx] + args
        out_shape.append(jax.ShapeDtypeStruct((b, l, d), F32))
        out_specs.append(tok(d))
    return pl.pallas_call(
        functools.partial(_inproj_kernel, has_f=has_f, nct_first=nct if first else None),
        grid=(b, l // tm), in_specs=in_specs,
        out_specs=out_specs, out_shape=out_shape,
        scratch_shapes=[pltpu.VMEM((tm, w["wa"].shape[1]), F32), pltpu.VMEM((tm, 2 * G_WIDTH), F32),
                        pltpu.VMEM((tm, 3 * d), F32)],
        compiler_params=_params("parallel", "parallel"), name="inproj",
    )(*args)


def _scan(x, op, fill, axis, reverse):
    n = x.shape[axis]
    idx = lax.broadcasted_iota(jnp.int32, x.shape, axis)
    k = 1
    while k < n:
        if reverse:
            x = op(x, jnp.where(idx >= n - k, fill, pltpu.roll(x, n - k, axis)))
        else:
            x = op(x, jnp.where(idx < k, fill, pltpu.roll(x, k, axis)))
        k *= 2
    return x


def _mlstm_kernel(qk_ref, vo_ref, gi_ref, gf_ref, gr_ref, conv_ref, mnorm_ref, ya_ref,
                  q_s, kt_s, h_s, bc_s, ml_s, dl_s, br_s, cn_s, m_s, s_s, p_s, qcn_s, u_s, *, lc):
    l = qk_ref.shape[1]
    ch = M_CHUNK
    nc = l // ch
    ncc = lc // ch
    nh = M_HEADS
    ng = 2 * nh
    w = conv_ref[...]
    row = lax.broadcasted_iota(jnp.int32, (ch, 1), 0)

    def conv_chunk(j):
        r0 = pl.multiple_of(j * ch, ch)
        cur = qk_ref[0, pl.ds(r0, ch), :]
        prev8 = qk_ref[0, pl.ds(pl.multiple_of(jnp.maximum(r0 - 8, 0), 8), 8), :]
        next8 = qk_ref[0, pl.ds(pl.multiple_of(jnp.minimum(r0 + ch, l - 8), 8), 8), :]
        seg_start = jnp.logical_or(j == 0, j == ncc)
        seg_end = jnp.logical_or(j == ncc - 1, j == nc - 1)
        pe = jnp.where(seg_start, 0.0, prev8[7:8, :])
        ne = jnp.where(seg_end, 0.0, next8[0:1, :])
        xp = jnp.where(row == 0, pe, pltpu.roll(cur, 1, 0))
        xn = jnp.where(row == ch - 1, ne, pltpu.roll(cur, ch - 1, 0))
        y = _silu(xp * w[0:1] + cur * w[1:2] + xn * w[2:3])
        q_s[pl.ds(r0, ch), :] = (y[:, :M_WIDTH] * (M_DH ** -0.5)).astype(BF16)
        kt_s[:, pl.ds(r0, ch)] = y[:, M_WIDTH:].T.astype(BF16)

    ri = lax.broadcasted_iota(jnp.int32, (ch, ch), 0)
    ci = lax.broadcasted_iota(jnp.int32, (ch, ch), 1)
    lower = ri >= ci
    upper = ri <= ci
    ones_blk = jnp.ones((ch, M_DH), BF16)
    fwd_c = lax.broadcasted_iota(jnp.int32, (ch, ng), 1) < nh
    fwd_r = lax.broadcasted_iota(jnp.int32, (ng, ch), 0) < nh
    lane_c = lax.broadcasted_iota(jnp.int32, (ch, ng), 1)

    def local_chunk(j):
        r0 = pl.multiple_of(j * ch, ch)
        rows = pl.ds(r0, ch)
        lfc = _log_sigmoid(gf_ref[0, rows, :])
        gr = gr_ref[0, :, rows]
        lfr = _log_sigmoid(gr[ng:])
        pre_c = _scan(lfc, jnp.add, 0.0, 0, False)
        pre_r = _scan(lfr, jnp.add, 0.0, 1, False)
        b_c = jnp.where(fwd_c, pre_c, jnp.sum(lfc, axis=0, keepdims=True) + lfc - pre_c)
        b_r = jnp.where(fwd_r, pre_r, jnp.sum(lfr, axis=1, keepdims=True) + lfr - pre_r)
        g_c = gi_ref[0, rows, :] - b_c
        g_r = gr[:ng] - b_r
        cg_c = jnp.where(fwd_c, _scan(g_c, jnp.maximum, -jnp.inf, 0, False),
                         _scan(g_c, jnp.maximum, -jnp.inf, 0, True))
        bc_s[rows, :] = b_c
        br_s[:, rows] = b_r
        ml_s[rows, :] = b_c + cg_c
        dl = jnp.zeros((ch, ng), F32)
        for hh in range(nh):
            sl = slice(hh * M_DH, (hh + 1) * M_DH)
            s_s[hh] = _dot(q_s[rows, sl], kt_s[sl, rows])
        for hh in range(nh):
            s = s_s[hh]
            for d in range(2):
                jj = d * nh + hh
                wgt = jnp.exp(jnp.where(upper if d else lower, g_r[jj:jj + 1, :] - cg_c[:, jj:jj + 1], -jnp.inf))
                p_s[hh, d * ch:(d + 1) * ch] = (s * wgt).astype(BF16)
        for hh in range(nh):
            sl = slice(hh * M_DH, (hh + 1) * M_DH)
            v1 = jnp.concatenate([vo_ref[0, rows, sl], ones_blk], axis=1)
            nd2 = _dot(p_s[hh], v1)
            for d in range(2):
                jj = d * nh + hh
                nd = nd2[d * ch:(d + 1) * ch]
                h_s[d, rows, sl] = nd[:, :M_DH]
                dl = jnp.where(lane_c == jj, nd[:, M_DH:M_DH + ng], dl)
        dl_s[rows, :] = dl

    def conv_local_body(j, carry):
        conv_chunk(j + 1)
        local_chunk(j)
        return carry

    conv_chunk(jnp.int32(0))
    lax.fori_loop(0, nc - 1, conv_local_body, 0)
    local_chunk(jnp.int32(nc - 1))

    cn_s[...] = jnp.zeros_like(cn_s)
    m_s[...] = jnp.zeros_like(m_s)
    lane_r = lax.broadcasted_iota(jnp.int32, (1, ng), 1)

    def scan_issue(r0, d):
        rows = pl.ds(r0, ch)
        gr = gr_ref[0, :, rows]
        br = br_s[:, rows]
        tot = jnp.sum(_log_sigmoid(gr[ng:]), axis=1, keepdims=True)
        scal = []
        for hh in range(nh):
            fi = d * nh + hh
            sl = slice(hh * M_DH, (hh + 1) * M_DH)
            qcn_s[fi] = _dot(q_s[rows, sl], cn_s[fi].astype(BF16))
            m_old = m_s[fi][:, 0:1]
            b_e = tot[fi:fi + 1, :]
            d_end = b_e - br[fi:fi + 1, :] + gr[fi:fi + 1, :]
            m_end = jnp.max(d_end, axis=-1, keepdims=True)
            m_new = jnp.maximum(b_e + m_old, m_end)
            ktw = (kt_s[sl, rows].astype(F32) * jnp.exp(d_end - m_end)).astype(BF16)
            v1 = jnp.concatenate([vo_ref[0, rows, sl], ones_blk], axis=1)
            u_s[fi] = _dot(ktw, v1)
            scal.append((m_old, m_new, jnp.exp(b_e + m_old - m_new), jnp.exp(m_end - m_new)))
        return scal

    def scan_finish(r0, d, scal):
        rows = pl.ds(r0, ch)
        m_row = jnp.zeros((1, ng), F32)
        for hh in range(nh):
            m_row = jnp.where(lane_r == d * nh + hh, scal[hh][0], m_row)

        inter = bc_s[rows, :] + m_row
        ml = ml_s[rows, :]
        mt = jnp.maximum(inter, ml)
        a = jnp.exp(ml - mt)
        wi = jnp.exp(inter - mt)
        qn = jnp.zeros((ch, ng), F32)
        for hh in range(nh):
            fi = d * nh + hh
            qn = jnp.where(lane_c == fi, qcn_s[fi, :, M_DH:M_DH + ng], qn)
        den = a * dl_s[rows, :] + wi * qn
        rinv = 1.0 / jnp.maximum(jnp.abs(den), jnp.exp(-mt))
        c_loc = a * rinv
        c_int = wi * rinv

        for hh in range(nh):
            fi = d * nh + hh
            sl = slice(hh * M_DH, (hh + 1) * M_DH)
            h_s[d, rows, sl] = (c_loc[:, fi:fi + 1] * h_s[d, rows, sl]
                                + c_int[:, fi:fi + 1] * qcn_s[fi, :, :M_DH])
            cn_s[fi] = scal[hh][2] * cn_s[fi] + scal[hh][3] * u_s[fi]
            m_s[fi] = jnp.broadcast_to(scal[hh][1], (1, M_DH))

    def scan_body(s, carry):
        rf = pl.multiple_of(s * ch, ch)
        rb = pl.multiple_of(jnp.where(s < ncc, ncc - 1 - s, nc - 1 - s + ncc) * ch, ch)
        sf = scan_issue(rf, 0)
        sb = scan_issue(rb, 1)
        scan_finish(rf, 0, sf)
        scan_finish(rb, 1, sb)
        return carry

    lax.fori_loop(0, nc, scan_body, 0)

    mnorm = mnorm_ref[...]

    def out_body(j, carry):
        r0 = pl.multiple_of(j * ch, ch)
        hsum = h_s[0, pl.ds(r0, ch), :] + h_s[1, pl.ds(r0, ch), :]
        og = _sigmoid(vo_ref[0, pl.ds(r0, ch), M_WIDTH:].astype(F32))
        for hh in range(M_HEADS):
            sl = slice(hh * M_DH, (hh + 1) * M_DH)
            ya_ref[0, pl.ds(r0, ch), sl] = (_rms(hsum[:, sl], mnorm[:, sl]) * og[:, sl]).astype(BF16)
        return carry

    lax.fori_loop(0, nc, out_body, 0)


def _mlstm_call(qk, vo, gi, gf, gr, conv, mnorm, lc):
    b, l, _ = qk.shape
    ng = 2 * M_HEADS
    return pl.pallas_call(
        functools.partial(_mlstm_kernel, lc=lc),
        grid=(b,),
        in_specs=[pl.BlockSpec((1, l, 2 * M_WIDTH), lambda bi: (bi, 0, 0), pipeline_mode=pl.Buffered(1)),
                  pl.BlockSpec((1, l, 2 * M_WIDTH), lambda bi: (bi, 0, 0)),
                  pl.BlockSpec((1, l, ng), lambda bi: (bi, 0, 0)),
                  pl.BlockSpec((1, l, ng), lambda bi: (bi, 0, 0)),
                  pl.BlockSpec((1, 2 * ng, l), lambda bi: (bi, 0, 0)),
                  _const_spec(conv.shape), _const_spec(mnorm.shape)],
        out_specs=pl.BlockSpec((1, l, M_WIDTH), lambda bi: (bi, 0, 0)),
        out_shape=jax.ShapeDtypeStruct((b, l, M_WIDTH), BF16),
        scratch_shapes=[pltpu.VMEM((l, M_WIDTH), BF16),
                        pltpu.VMEM((M_WIDTH, l), BF16),
                        pltpu.VMEM((2, l, M_WIDTH), F32),
                        pltpu.VMEM((l, ng), F32),
                        pltpu.VMEM((l, ng), F32),
                        pltpu.VMEM((l, ng), F32),
                        pltpu.VMEM((ng, l), F32),
                        pltpu.VMEM((2 * M_HEADS, M_DH, 2 * M_DH), F32),
                        pltpu.VMEM((2 * M_HEADS, 1, M_DH), F32),
                        pltpu.VMEM((M_HEADS, M_CHUNK, M_CHUNK), F32),
                        pltpu.VMEM((M_HEADS, 2 * M_CHUNK, M_CHUNK), BF16),
                        pltpu.VMEM((2 * M_HEADS, M_CHUNK, 2 * M_DH), F32),
                        pltpu.VMEM((2 * M_HEADS, M_DH, 2 * M_DH), F32)],
        compiler_params=_params("parallel"), name="mlstm",
    )(qk, vo, gi, gf, gr, conv, mnorm)


def _attn_kernel(q_ref, k_ref, v_ref, o_ref, s_s, p_s, *, lc, ctx_out):
    tq = q_ref.shape[1]
    l = k_ref.shape[1]
    qi = pl.program_id(2)
    nct = lc // tq
    lane = lax.broadcasted_iota(jnp.int32, (tq, 2 * A_VDIM), 1)

    def run(klen):
        outs = []
        for hh in range(A_HPS):
            sl = slice(hh * A_PAD, (hh + 1) * A_PAD)
            s_s[hh, :, :klen] = _dot_nt(q_ref[0, :, sl], k_ref[0, :klen, sl])
        row_max = [jnp.max(s_s[hh, :, :klen], axis=-1, keepdims=True) for hh in range(A_HPS)]
        for hh in range(A_HPS):
            p_s[hh, :, :klen] = jnp.exp2((s_s[hh, :, :klen] - row_max[hh]).astype(BF16))
        for hh in range(A_HPS):
            sl = slice(hh * A_PAD, (hh + 1) * A_PAD)
            nd = _dot(p_s[hh, :, :klen], v_ref[0, :klen, sl])
            outs.append(nd / pltpu.roll(nd, A_VDIM, 1))
        for pp in range(A_HPS // 2):
            o_ref[0, :, pp * A_PAD:(pp + 1) * A_PAD] = jnp.where(
                lane < A_VDIM, outs[2 * pp], pltpu.roll(outs[2 * pp + 1], A_VDIM, 1)).astype(BF16)

    @pl.when(qi >= nct)
    def _():
        run(l)

    @pl.when(qi < nct)
    def _():
        if ctx_out:
            run(lc)
        else:
            o_ref[...] = jnp.zeros_like(o_ref)


def _attn_call(q, k, v, lc, tq, ctx_out):
    b, l, _ = q.shape
    return pl.pallas_call(
        functools.partial(_attn_kernel, lc=lc, ctx_out=ctx_out),
        grid=(b, A_HEADS // A_HPS, l // tq),
        in_specs=[pl.BlockSpec((1, tq, A_HPS * A_PAD), lambda bi, p, qi: (bi, qi, p)),
                  pl.BlockSpec((1, l, A_HPS * A_PAD), lambda bi, p, qi: (bi, 0, p)),
                  pl.BlockSpec((1, l, A_HPS * A_PAD), lambda bi, p, qi: (bi, 0, p))],
        out_specs=pl.BlockSpec((1, tq, A_HPS * A_VDIM), lambda bi, p, qi: (bi, qi, p)),
        out_shape=jax.ShapeDtypeStruct((b, l, A_WIDTH), BF16),
        scratch_shapes=[pltpu.VMEM((A_HPS, tq, l), F32), pltpu.VMEM((A_HPS, tq, l), BF16)],
        compiler_params=_params("parallel", "parallel", "arbitrary"), name="attn",
    )(q, k, v)


def _merge_kernel(x_ref, mod_ref, modc_ref, ya_ref, yb_ref, yc_ref, br_ref, wpa_ref, wpb_ref, wpc_ref, wout_ref,
                  n2_ref, r_ref, rb_ref, o_ref, h2_o, idx_o, idxt_o, cnt_o, y_s, *, tm, nct, skip_ctx):
    d = x_ref.shape[2]
    nsub = x_ref.shape[1] // tm
    modb = mod_ref[0]
    modc = modc_ref[0]
    tiles = [slice(s * tm, (s + 1) * tm) for s in range(nsub)]
    ctx = [pl.program_id(1) * nsub + s < nct for s in range(nsub)]
    mods = [jnp.where(c, modc, modb) for c in ctx]
    for s, rows in enumerate(tiles):
        br = br_ref[0, rows, :]
        y = (br[:, :d].astype(F32) * _dot(ya_ref[0, rows, :], wpa_ref[...])
             + br[:, d:2 * d].astype(F32) * _dot(yb_ref[0, rows, :], wpb_ref[...])
             + br[:, 2 * d:].astype(F32) * _dot(yc_ref[0, rows, :], wpc_ref[...]))
        y_s[s] = y.astype(BF16)
    for s, rows in enumerate(tiles):
        o_ref[0, rows, :] = x_ref[0, rows, :] + mods[s][2:3] * _dot(y_s[s], wout_ref[...])
    for s, rows in enumerate(tiles):
        logits = _route_logits(o_ref[0, rows, :], mods[s], n2_ref, r_ref, rb_ref, h2_o, rows)
        left_out = ctx[s] if skip_ctx else None
        _route_assign(logits, left_out, d, h2_o, idx_o, idxt_o, cnt_o, rows, s)


def _merge_call(xs, mod, ya, yb, yc, br, w, n2, r3, rb, lc, tm, skip_ctx):
    b, l, d = xs.shape
    nct = lc // tm
    nsub = next(n for n in (3, 2, 1) if l % (n * tm) == 0)
    tg = nsub * tm
    tok = lambda width: pl.BlockSpec((1, tg, width), lambda bi, j: (bi, j, 0))
    consts = [w["wpa"], w["wpb"], w["wpc"], w["wout"], n2, r3, rb]
    return pl.pallas_call(
        functools.partial(_merge_kernel, tm=tm, nct=nct, skip_ctx=skip_ctx), grid=(b, l // tg),
        in_specs=[tok(d), pl.BlockSpec((1, 6, d), lambda bi, j: (bi, 0, 0)),
                  pl.BlockSpec((1, 6, d), lambda bi, j: (b, 0, 0)),
                  tok(M_WIDTH), tok(A_WIDTH), tok(G_WIDTH), tok(3 * d)] + [_const_spec(a.shape) for a in consts],
        out_specs=[tok(d), tok(d + R_PAD), tok(8), pl.BlockSpec((1, 8, tg), lambda bi, j: (bi, 0, j)),
                   pl.BlockSpec((1, nsub, 8, R_PAD), lambda bi, j: (bi, j, 0, 0))],
        out_shape=[jax.ShapeDtypeStruct((b, l, d), F32),
                   jax.ShapeDtypeStruct((b, l, d + R_PAD), BF16),
                   jax.ShapeDtypeStruct((b, l, 8), jnp.int32), jax.ShapeDtypeStruct((b, 8, l), jnp.int32),
                   jax.ShapeDtypeStruct((b, l // tm, 8, R_PAD), F32)],
        scratch_shapes=[pltpu.VMEM((nsub, tm, d), BF16)],
        compiler_params=_params("parallel", "parallel"), name="merge",
    )(xs, mod, mod, ya, yb, yc, br, *consts)


def _route_logits(x, mod, n2_ref, r_ref, rb_ref, h2_o, rows):
    d = x.shape[1]
    h2 = _rms(x, n2_ref[...]) * (1.0 + mod[4:5]) + mod[3:4]
    h2_o[0, rows, :d] = h2.astype(BF16)
    r = r_ref[...]
    pp = sum(_dot(piece, r) for piece in _split3(h2)[:2])
    return pp + pltpu.roll(pp, R_PAD - R_SEG, 1) + pltpu.roll(pp, R_PAD - 2 * R_SEG, 1) + rb_ref[...]


def _route_assign(logits, left_out, d, h2_o, idx_o, idxt_o, cnt_o, rows, s):
    tm = logits.shape[0]
    el = logits[:, :N_EXPERTS]
    gl = logits[:, N_EXPERTS:N_EXPERTS + N_GROUPS]
    big = 1e9

    lane_g = lax.broadcasted_iota(jnp.int32, (tm, N_GROUPS), 1).astype(F32)
    gmax = jnp.max(gl, axis=-1, keepdims=True)
    g_sel = jnp.min(jnp.where(gl == gmax, lane_g, big), axis=-1, keepdims=True)
    g_prob = 1.0 / jnp.sum(jnp.exp(gl - gmax), axis=-1, keepdims=True)

    lane_i = lax.broadcasted_iota(jnp.int32, (tm, N_EXPERTS), 1)
    lane_e = lane_i.astype(F32)
    lane_grp = (lane_i // EXP_PER_GROUP).astype(F32)
    v1 = jnp.where(lane_grp == g_sel, el, -jnp.inf)
    t1 = jnp.max(v1, axis=-1, keepdims=True)
    i1 = jnp.min(jnp.where(v1 == t1, lane_e, big), axis=-1, keepdims=True)
    v2 = jnp.where(lane_e == i1, -jnp.inf, v1)
    t2 = jnp.max(v2, axis=-1, keepdims=True)
    i2 = jnp.min(jnp.where(v2 == t2, lane_e, big), axis=-1, keepdims=True)
    e21 = jnp.exp(t2 - t1)
    w1 = 1.0 / (1.0 + e21)
    w2 = e21 * w1
    comb = (jnp.where(lane_e == i1, w1, 0.0) + jnp.where(lane_e == i2, w2, 0.0)) * g_prob
    tail = jnp.zeros((tm, R_PAD - 3 * N_EXPERTS), BF16)
    h2_o[0, rows, d:] = jnp.concatenate(list(_split3(comb)) + [tail], axis=1)

    if left_out is not None:
        g_sel = jnp.where(left_out, -1.0, g_sel)
    lane_p = lax.broadcasted_iota(jnp.int32, (tm, R_PAD), 1)
    onehot = jnp.where(lane_p.astype(F32) == g_sel, 1.0, 0.0)
    ri = lax.broadcasted_iota(jnp.int32, (tm, tm), 0)
    ci = lax.broadcasted_iota(jnp.int32, (tm, tm), 1)
    before = jnp.where(ri > ci, 1.0, 0.0).astype(BF16)
    rank = jnp.sum(_dot(before, onehot.astype(BF16)) * onehot, axis=-1, keepdims=True)
    cnt_o[0, s] = jnp.broadcast_to(jnp.sum(onehot, axis=0, keepdims=True), (8, R_PAD))
    fields = jnp.where(lane_p == 0, g_sel, jnp.where(lane_p == 1, rank, 0.0))
    idx_o[0, rows, :] = fields[:, :8].astype(jnp.int32)
    idxt_o[0, :, rows] = fields.T[:8, :].astype(jnp.int32)


def _experts_kernel(st_ref, h2_ref, idx_ref, idxt_ref, w1_ref, w3_ref, w2_ref, o_ref, hs_s, ys_s, a_s, hid_s, *, tb):
    l = h2_ref.shape[1]
    d = o_ref.shape[2]
    nblk = l // tb
    ch = MOE_CHUNK
    bi = pl.program_id(0)
    g = pl.program_id(1)

    @pl.when(g == 0)
    def _():
        ys_s[...] = jnp.zeros_like(ys_s)

    def group_base(gg):
        return (bi * N_GROUPS + gg) * (nblk + 1)

    def group_offset(upto):
        off = 0
        for gg in range(N_GROUPS - 1):
            padded = ((st_ref[group_base(gg) + nblk] + ch - 1) // ch) * ch
            off = off + jnp.where(gg < upto, padded, 0)
        return off

    base = group_base(g)
    cnt = st_ref[base + nblk]
    goff = group_offset(g)
    gw = min(MOE_GATHER_BLOCKS, nblk)
    nwin = -(-nblk // gw)

    def chunk(lo, ch):
        lane_e = lax.broadcasted_iota(jnp.int32, (ch, N_EXPERTS), 1)
        sub_iota = lax.broadcasted_iota(jnp.int32, (ch, gw * tb), 0)
        lane_blk = lax.broadcasted_iota(jnp.int32, (1, gw * tb), 1) // tb
        k0 = 0
        for k in range(nblk):
            k0 = k0 + jnp.where(st_ref[base + k + 1] <= lo, 1, 0)

        def window(j):
            want = k0 + j * gw
            kj = jnp.minimum(want, nblk - gw)
            tok0 = pl.multiple_of(kj * tb, tb)
            it = idxt_ref[0, :, pl.ds(tok0, gw * tb)]
            offs = jnp.concatenate([jnp.full((1, tb), st_ref[base + kj + i] - lo, jnp.int32) for i in range(gw)],
                                   axis=1)
            mine = jnp.where(lane_blk >= want - kj, it[0:1], -1)
            pos = jnp.where(mine == g, it[1:2] + offs, -1)
            p = jnp.where(sub_iota == pos, 1.0, 0.0).astype(BF16)
            return _dot(p, h2_ref[0, pl.ds(tok0, gw * tb), :])

        hs_s[:ch] = window(0)
        for j in range(1, nwin):
            first = k0 + j * gw

            @pl.when(jnp.logical_and(first < nblk, st_ref[base + jnp.minimum(first, nblk)] < lo + ch))
            def _(j=j):
                hs_s[:ch] += window(j)

        hsb = hs_s[:ch, :d].astype(BF16)
        cs = (hs_s[:ch, d:d + N_EXPERTS] + hs_s[:ch, d + N_EXPERTS:d + 2 * N_EXPERTS]
              + hs_s[:ch, d + 2 * N_EXPERTS:d + 3 * N_EXPERTS])
        for e in range(EXP_PER_GROUP):
            a_s[2 * e, :ch] = _dot(hsb, w1_ref[e])
            a_s[2 * e + 1, :ch] = _dot(hsb, w3_ref[e])
        for e in range(EXP_PER_GROUP):
            ce = jnp.sum(jnp.where(lane_e == g * EXP_PER_GROUP + e, cs, 0.0), axis=-1, keepdims=True)
            hid_s[e, :ch] = (_silu(a_s[2 * e, :ch]) * a_s[2 * e + 1, :ch] * ce).astype(BF16)
        y = jnp.zeros((ch, d), F32)
        for e in range(EXP_PER_GROUP):
            y = y + _dot(hid_s[e, :ch], w2_ref[e])
        ys_s[pl.ds(pl.multiple_of(goff + lo, ch), ch), :] = y.astype(BF16)

    half = ch // 2
    nfull = cnt // ch
    rem = cnt - nfull * ch
    nloop = nfull + jnp.where(rem > half, 1, 0)

    def chunk_body(c, carry):
        chunk(c * ch, ch)
        return carry

    lax.fori_loop(0, nloop, chunk_body, 0)

    @pl.when(jnp.logical_and(rem > 0, rem <= half))
    def _():
        chunk(nfull * ch, half)

    @pl.when(g == N_GROUPS - 1)
    def _():
        lane_w = lax.broadcasted_iota(jnp.int32, (tb, ch), 1)
        goffs = [group_offset(gg) for gg in range(N_GROUPS)]

        def window(k, gg, shift):
            s_k = st_ref[group_base(gg) + k]
            win = (s_k // half) * half
            ic = idx_ref[0, k * tb:(k + 1) * tb, :]
            pos = jnp.where(ic[:, 0:1] == gg, ic[:, 1:2] + (s_k - win - shift), -1)
            q = jnp.where(lane_w == pos, 1.0, 0.0).astype(BF16)
            start = pl.multiple_of(goffs[gg] + win + shift, half)
            return _dot(q, ys_s[pl.ds(start, ch), :])

        for k in range(nblk):
            acc = window(k, 0, 0)
            for gg in range(1, N_GROUPS):
                acc = acc + window(k, gg, 0)
            o_ref[0, k * tb:(k + 1) * tb, :] = acc.astype(BF16)
        for k in range(nblk):
            rows = slice(k * tb, (k + 1) * tb)
            for gg in range(N_GROUPS):
                s_k = st_ref[group_base(gg) + k]
                e_k = st_ref[group_base(gg) + k + 1]

                @pl.when(e_k - (s_k // half) * half > ch)
                def _(k=k, gg=gg, rows=rows):
                    o_ref[0, rows, :] = (o_ref[0, rows, :].astype(F32) + window(k, gg, ch)).astype(BF16)


def _experts_call(starts, h2, idx, idxt, w1, w3, w2, layer, tb):
    b, l, de = h2.shape
    d = de - R_PAD
    whole = lambda width: pl.BlockSpec((1, l, width), lambda bi, g, st: (bi, 0, 0))
    grid_spec = pltpu.PrefetchScalarGridSpec(
        num_scalar_prefetch=1, grid=(b, N_GROUPS),
        in_specs=[pl.BlockSpec((1, l, de), lambda bi, g, st: (bi, 0, 0), pipeline_mode=pl.Buffered(1)),
                  whole(8), pl.BlockSpec((1, 8, l), lambda bi, g, st: (bi, 0, 0)),
                  pl.BlockSpec((None, EXP_PER_GROUP, d, D_EXPERT), lambda bi, g, st: (layer, g, 0, 0)),
                  pl.BlockSpec((None, EXP_PER_GROUP, d, D_EXPERT), lambda bi, g, st: (layer, g, 0, 0)),
                  pl.BlockSpec((None, EXP_PER_GROUP, D_EXPERT, d), lambda bi, g, st: (layer, g, 0, 0))],
        out_specs=whole(d),
        scratch_shapes=[pltpu.VMEM((MOE_CHUNK, de), F32),
                        pltpu.VMEM((l + (N_GROUPS + 1) * MOE_CHUNK, d), BF16),
                        pltpu.VMEM((2 * EXP_PER_GROUP, MOE_CHUNK, D_EXPERT), F32),
                        pltpu.VMEM((EXP_PER_GROUP, MOE_CHUNK, D_EXPERT), BF16)])
    return pl.pallas_call(
        functools.partial(_experts_kernel, tb=tb), grid_spec=grid_spec,
        out_shape=jax.ShapeDtypeStruct((b, l, d), BF16),
        compiler_params=_params("parallel", "arbitrary"), name="experts",
    )(starts, h2, idx, idxt, w1, w3, w2)


def _group_starts(cnt):
    c = cnt[:, :, 0, :N_GROUPS].astype(jnp.int32)
    s = jnp.cumsum(c, axis=1)
    s = jnp.concatenate([jnp.zeros_like(s[:, :1]), s], axis=1)
    return jnp.transpose(s, (0, 2, 1)).reshape(-1)


def _final_kernel(x_ref, mod_ref, f_ref, g_ref, o_ref):
    o_ref[0] = _rms(x_ref[0] + mod_ref[0][5:6] * f_ref[0].astype(F32), g_ref[...])


def _final_call(xs, mod, f, g, lc, tm):
    b, l, d = xs.shape
    off = lc // tm
    lat = pl.BlockSpec((1, tm, d), lambda bi, j: (bi, j + off, 0))
    return pl.pallas_call(
        _final_kernel, grid=(b, (l - lc) // tm),
        in_specs=[lat, pl.BlockSpec((1, 6, d), lambda bi, j: (bi, 0, 0)), lat, _const_spec(g.shape)],
        out_specs=pl.BlockSpec((1, tm, d), lambda bi, j: (bi, j, 0)),
        out_shape=jax.ShapeDtypeStruct((b, l - lc, d), F32),
        compiler_params=_params("parallel", "parallel"), name="final_norm",
    )(xs, mod, f, g)


def _rope_tables(t_len, lc):
    half = A_ROPE // 2
    rows = t_len // GRID_W
    r = jnp.repeat(jnp.arange(rows, dtype=F32), GRID_W)
    col = jnp.tile(jnp.arange(GRID_W, dtype=F32), rows)
    inv = ROPE_THETA ** (-jnp.arange(0, half, 2, dtype=F32) / half)
    ang = jnp.concatenate([r[:, None] * inv, col[:, None] * inv], axis=-1)
    cos = jnp.concatenate([jnp.ones((lc, half), F32), jnp.cos(ang)], axis=0)
    sin = jnp.concatenate([jnp.zeros((lc, half), F32), jnp.sin(ang)], axis=0)
    l = lc + t_len
    ones = jnp.ones((l, A_NOPE), F32)
    zeros = jnp.zeros((l, A_NOPE), F32)
    tail1 = jnp.ones((l, A_PAD - A_NOPE - A_ROPE), F32)
    tail0 = jnp.zeros((l, A_PAD - A_NOPE - A_ROPE), F32)
    zh = jnp.zeros((l, half), F32)
    cos_t = jnp.concatenate([ones, cos, cos, tail1], axis=-1)
    sina_t = jnp.concatenate([zeros, zh, sin, tail0], axis=-1)
    sinb_t = jnp.concatenate([zeros, -sin, zh, tail0], axis=-1)
    return cos_t, sina_t, sinb_t


def _layer_weights(l, w_in, m_gate_b, a_qnorm, a_wuq, a_kvnorm, a_wukv, g_ws, g_bs, g_vnorm,
                   w_pa, w_pb, w_pc, w_out):
    d = w_in.shape[1]
    wi = w_in[l]
    o = 0

    def take(n):
        nonlocal o
        s = wi[:, o:o + n]
        o += n
        return s

    mq, mk, mv, mo, mg = take(M_WIDTH), take(M_WIDTH), take(M_WIDTH), take(M_WIDTH), take(4 * M_HEADS)
    aq, akv, akr = take(A_QRANK), take(A_KVRANK), take(A_ROPE)
    gu, gv = take(G_WIDTH), take(G_WIDTH)
    br = take(3 * d)
    nh = M_HEADS
    gb = m_gate_b[l]
    mgo = jnp.concatenate([mg[:, :nh], mg[:, 2 * nh:3 * nh], mg[:, nh:2 * nh], mg[:, 3 * nh:]], axis=1)
    gbo = jnp.concatenate([gb[:nh], gb[2 * nh:3 * nh], gb[nh:2 * nh], gb[3 * nh:]])
    akr_pad = jnp.concatenate([jnp.zeros((d, A_NOPE), F32), akr,
                               jnp.zeros((d, A_PAD - A_NOPE - A_ROPE), F32)], axis=1)
    wuq = a_wuq[l].reshape(A_QRANK, A_HEADS, A_NOPE + A_ROPE)
    wuq = jnp.pad(wuq, ((0, 0), (0, 0), (0, A_PAD - A_NOPE - A_ROPE))).reshape(A_QRANK, A_HEADS * A_PAD)
    wukv = a_wukv[l].reshape(A_KVRANK, A_HEADS, A_NOPE + A_VDIM)
    wuk = jnp.pad(wukv[:, :, :A_NOPE], ((0, 0), (0, 0), (0, A_PAD - A_NOPE))).reshape(A_KVRANK, A_HEADS * A_PAD)
    wuv = jnp.pad(wukv[:, :, A_NOPE:], ((0, 0), (0, 0), (0, A_PAD - A_VDIM))).reshape(A_KVRANK, A_HEADS * A_PAD)
    vone = jnp.tile(jnp.concatenate([jnp.zeros((A_VDIM,), F32), jnp.ones((A_PAD - A_VDIM,), F32)]),
                    A_HEADS).reshape(1, A_HEADS * A_PAD)
    gbs = jnp.repeat(g_bs[l].T, G_DG, axis=1)
    return dict(
        wqk=jnp.concatenate([mq, mk], 1).astype(BF16), wvo=jnp.concatenate([mv, mo], 1).astype(BF16),
        wgt=jnp.pad(mgo, ((0, 0), (0, G_PAD - 4 * nh))).astype(BF16),
        gbt=jnp.pad(gbo, (0, G_PAD - 4 * nh)).reshape(1, G_PAD),
        wa=jnp.concatenate([aq, akv, akr_pad], 1).astype(BF16),
        wg=jnp.concatenate([gu, gv], 1).astype(BF16), wbr=br.astype(BF16),
        aqn=a_qnorm[l].reshape(1, -1), akvn=a_kvnorm[l].reshape(1, -1),
        wuq=wuq.astype(BF16), wuk=wuk.astype(BF16), wuv=wuv.astype(BF16), vone=vone,
        gvn=g_vnorm[l].reshape(1, -1), gws=g_ws[l].astype(BF16), gbs=gbs,
        wpa=w_pa[l].astype(BF16), wpb=w_pb[l].astype(BF16), wpc=w_pc[l].astype(BF16),
        wout=w_out[l].astype(BF16))


def _router_weights(r_group, r_group_b, r_expert, r_expert_b):
    d = r_group.shape[0]
    pad = R_SEG - N_EXPERTS - N_GROUPS
    r = jnp.concatenate([r_expert, r_group, jnp.zeros((d, pad), F32)], axis=1)
    r3 = jnp.concatenate(list(_split3(r)) + [jnp.zeros((d, R_PAD - 3 * R_SEG), BF16)], axis=1)
    rb = jnp.concatenate([r_expert_b, r_group_b, jnp.zeros((R_PAD - N_EXPERTS - N_GROUPS,), F32)])
    return r3, rb.reshape(1, R_PAD)


def _tile(n, lc, candidates):
    for t in candidates:
        if n % t == 0 and lc % t == 0:
            return t
    raise ValueError("sequence lengths must be multiples of 128")


def kernel(x, c, ctx, c_ctx, w_ada, b_ada, norm1, norm2, final_norm, w_in, m_conv, m_gate_b, m_norm, a_qnorm, a_wuq, a_kvnorm, a_wukv, g_ws, g_bs, g_vnorm, w_pa, w_pb, w_pc, w_out, r_group, r_group_b, r_expert, r_expert_b, e_w1, e_w3, e_w2):
    b, t_len, d = x.shape
    lc = ctx.shape[1]
    l = lc + t_len
    depth = w_in.shape[0]
    tm = _tile(l, lc, (256, 128))

    xs = (ctx, x)
    cv = jnp.concatenate([c, c_ctx[None, :]], axis=0)
    mod_all = _ada_call(cv, w_ada, b_ada).reshape(depth, b + 1, 6, d)
    tabs = _rope_tables(t_len, lc)

    w_in_b = w_in.astype(BF16)
    e_w1_b, e_w3_b, e_w2_b = e_w1.astype(BF16), e_w3.astype(BF16), e_w2.astype(BF16)
    f = None
    mod_prev = None
    for li in range(depth):
        last = li == depth - 1
        mod = mod_all[li]
        w = _layer_weights(li, w_in_b, m_gate_b, a_qnorm, a_wuq, a_kvnorm, a_wukv, g_ws, g_bs, g_vnorm,
                           w_pa, w_pb, w_pc, w_out)
        outs = _inproj_call(xs, mod, norm1[li].reshape(1, d), w, tabs, lc, tm, f, mod_prev)
        qk, vo, gi, gf, gr, q, k, v, yc, br = outs[:10]
        if len(outs) > 10:
            xs = outs[10]
        ya = _mlstm_call(qk, vo, gi, gf, gr, m_conv[li], m_norm[li].reshape(1, -1), lc)
        yb = _attn_call(q, k, v, lc, tm, not last)
        r3, rb = _router_weights(r_group[li], r_group_b[li], r_expert[li], r_expert_b[li])
        xs, h2, idx, idxt, cnt = _merge_call(xs, mod, ya, yb, yc, br, w, norm2[li].reshape(1, d),
                                             r3, rb, lc, tm, last)
        f = _experts_call(_group_starts(cnt), h2, idx, idxt, e_w1_b, e_w3_b, e_w2_b, li, tm)
        mod_prev = mod
    return _final_call(xs, mod_prev, f, final_norm.reshape(1, d), lc, tm)
```

```python
import functools

import jax
import jax.numpy as jnp
from jax import lax
from jax.experimental import pallas as pl
from jax.experimental.pallas import tpu as pltpu

F32 = jnp.float32
BF16 = jnp.bfloat16

EPS = 1e-6
GRID_W = 64
ROPE_THETA = 10000.0

M_HEADS = 4
M_DH = 128
M_WIDTH = M_HEADS * M_DH
M_CHUNK = 128
G_PAD = 128

A_HEADS = 8
A_NOPE = 64
A_ROPE = 32
A_VDIM = 64
A_QRANK = 384
A_KVRANK = 256
A_WIDTH = A_HEADS * A_VDIM
A_PAD = 128
A_HPS = 4
ATT_SCALE = (A_NOPE + A_ROPE) ** -0.5
LOG2E = 1.4426950408889634

G_GROUPS = 4
G_CHUNK = 128
G_WIDTH = 512
G_DG = G_WIDTH // G_GROUPS

N_GROUPS = 4
EXP_PER_GROUP = 4
N_EXPERTS = N_GROUPS * EXP_PER_GROUP
D_EXPERT = 512
R_PAD = 128
R_SEG = 32
MOE_CHUNK = 256
MOE_GATHER_BLOCKS = 5

VMEM_LIMIT = 56 * 1024 * 1024


def _dot(a, b):
    return jnp.dot(a, b, preferred_element_type=F32)


def _dot_nt(a, b):
    return lax.dot_general(a, b, (((1,), (1,)), ((), ())), preferred_element_type=F32)


def _split3(x):
    hi = x.astype(BF16)
    r = x - hi.astype(F32)
    mid = r.astype(BF16)
    lo = (r - mid.astype(F32)).astype(BF16)
    return hi, mid, lo


def _sigmoid(x):
    return 1.0 / (1.0 + jnp.exp(-x))


def _silu(x):
    return x * _sigmoid(x)


def _log_sigmoid(x):
    return jnp.minimum(x, 0.0) - jnp.log1p(jnp.exp(-jnp.abs(x)))


def _gelu(x):
    return 0.5 * x * (1.0 + lax.erf(x * (2.0 ** -0.5)))


def _rms(x, g):
    return x * lax.rsqrt(jnp.mean(x * x, axis=-1, keepdims=True) + EPS) * g


def _params(*sem):
    return pltpu.CompilerParams(dimension_semantics=sem, vmem_limit_bytes=VMEM_LIMIT)


def _const_spec(shape):
    nd = len(shape)
    return pl.BlockSpec(shape, lambda *_: (0,) * nd, pipeline_mode=pl.Buffered(1))


def _ada_kernel(cv_ref, w_ref, b_ref, o_ref):
    s = _silu(cv_ref[...])
    o_ref[0] = _dot(s.astype(BF16), w_ref[0].astype(BF16)) + b_ref[0]


def _ada_call(cv, w_ada, b_ada):
    depth, d, n6 = w_ada.shape
    rows = cv.shape[0]
    tn = n6 // 4
    return pl.pallas_call(
        _ada_kernel,
        grid=(depth, n6 // tn),
        in_specs=[pl.BlockSpec((rows, d), lambda l, j: (0, 0)),
                  pl.BlockSpec((1, d, tn), lambda l, j: (l, 0, j)),
                  pl.BlockSpec((1, 1, tn), lambda l, j: (l, 0, j))],
        out_specs=pl.BlockSpec((1, rows, tn), lambda l, j: (l, 0, j)),
        out_shape=jax.ShapeDtypeStruct((depth, rows, n6), F32),
        compiler_params=_params("parallel", "parallel"),
        name="ada",
    )(cv, w_ada, b_ada.reshape(depth, 1, n6))


def _inproj_kernel(*refs, has_f, nct_first):
    za_s, zg_s, zbr_s = refs[-3:]
    refs = refs[:-3]
    if nct_first is not None:
        ctx_ref, x_o = refs[0], refs[-1]
        refs = refs[1:-1]
    if has_f:
        f_ref, modp_ref, x_o = refs[0], refs[1], refs[-1]
        refs = refs[2:-1]
    (x_ref, mod_ref, n1_ref, wqk_ref, wvo_ref, wgt_ref, gbt_ref,
     wa_ref, wg_ref, wbr_ref, aqn_ref, akvn_ref, wuq_ref, wuk_ref, wuv_ref, vone_ref,
     cos_ref, sina_ref, sinb_ref, gvn_ref, gws_ref, gbs_ref,
     qk_o, vo_o, gi_o, gf_o, gr_o, q_o, k_o, v_o, yc_o, br_o) = refs
    tm = x_ref.shape[1]
    mod = mod_ref[0]
    x = x_ref[0]
    if nct_first is not None:
        x = jnp.where(pl.program_id(1) < nct_first, ctx_ref[0], x)
        x_o[0] = x
    if has_f:
        x = x + modp_ref[0][5:6] * f_ref[0].astype(F32)
        x_o[0] = x
    h = _rms(x, n1_ref[...]) * (1.0 + mod[1:2]) + mod[0:1]
    hb = h.astype(BF16)

    qk_o[0] = _dot(hb, wqk_ref[...])
    vo_o[0] = _dot(hb, wvo_ref[...]).astype(BF16)
    ng = gi_o.shape[2]
    gates = _dot(hb, wgt_ref[...]) + gbt_ref[...]
    gi_o[0] = gates[:, :ng]
    gf_o[0] = pltpu.roll(gates, gates.shape[1] - ng, 1)[:, :ng]
    gr_o[0] = gates.T[:2 * ng, :]

    za_s[...] = _dot(hb, wa_ref[...])
    zg_s[...] = _dot(hb, wg_ref[...])
    zbr_s[...] = _dot(hb, wbr_ref[...])

    aqn = _rms(za_s[:, :A_QRANK], aqn_ref[...]).astype(BF16)
    akvn = _rms(za_s[:, A_QRANK:A_QRANK + A_KVRANK], akvn_ref[...]).astype(BF16)
    cos = cos_ref[...]
    sina = sina_ref[...]
    sinb = sinb_ref[...]
    half = A_ROPE // 2

    def rope(t):
        return t * cos + pltpu.roll(t, half, 1) * sina + pltpu.roll(t, A_PAD - half, 1) * sinb

    kr = rope(za_s[:, A_QRANK + A_KVRANK:])
    qp = _dot(aqn, wuq_ref[...])
    kp = _dot(akvn, wuk_ref[...])
    for hh in range(A_HEADS):
        sl = slice(hh * A_PAD, (hh + 1) * A_PAD)
        q_o[0, :, sl] = (rope(qp[:, sl]) * (ATT_SCALE * LOG2E)).astype(BF16)
        k_o[0, :, sl] = (kp[:, sl] + kr).astype(BF16)
    v_o[0] = (_dot(akvn, wuv_ref[...]) + vone_ref[...]).astype(BF16)

    gu = _gelu(zg_s[:, :G_WIDTH])
    gv = _gelu(zg_s[:, G_WIDTH:])
    gvn = gvn_ref[...]
    bias = gbs_ref[...]
    for g in range(G_GROUPS):
        sl = slice(g * G_DG, (g + 1) * G_DG)
        xn = _rms(gv[:, sl], gvn[:, sl]).astype(BF16)
        ws = gws_ref[g]
        for ci in range(tm // G_CHUNK):
            r = slice(ci * G_CHUNK, (ci + 1) * G_CHUNK)
            sg = _dot(ws, xn[r]) + bias[:, sl]
            yc_o[0, r, sl] = (gu[r, sl] * sg).astype(BF16)

    br_o[0] = _sigmoid(zbr_s[...]).astype(BF16)


def _inproj_call(xs, mod, n1, w, tabs, lc, tm, f=None, mod_prev=None):
    first = isinstance(xs, tuple)
    if first:
        ctx, xs = xs
        b, t_len, d = xs.shape
        l = lc + t_len
    else:
        b, l, d = xs.shape
    nct = lc // tm
    has_f = f is not None
    tok = lambda width: pl.BlockSpec((1, tm, width), lambda bi, j: (bi, j, 0))
    modspec = pl.BlockSpec((1, 6, d), lambda bi, j: (jnp.where(j < nct, b, bi), 0, 0))
    tab = pl.BlockSpec((tm, A_PAD), lambda bi, j: (j, 0))
    consts = [n1, w["wqk"], w["wvo"], w["wgt"], w["gbt"], w["wa"], w["wg"], w["wbr"],
              w["aqn"], w["akvn"], w["wuq"], w["wuk"], w["wuv"], w["vone"]]
    consts2 = [w["gvn"], w["gws"], w["gbs"]]
    in_specs = ([tok(d), modspec] + [_const_spec(a.shape) for a in consts] + [tab, tab, tab]
                + [_const_spec(a.shape) for a in consts2])
    args = [xs, mod, *consts, *tabs, *consts2]
    ng = 4 * M_HEADS
    out_shape = [jax.ShapeDtypeStruct((b, l, 2 * M_WIDTH), F32),
                 jax.ShapeDtypeStruct((b, l, 2 * M_WIDTH), BF16),
                 jax.ShapeDtypeStruct((b, l, ng // 2), F32),
                 jax.ShapeDtypeStruct((b, l, ng // 2), F32),
                 jax.ShapeDtypeStruct((b, ng, l), F32),
                 jax.ShapeDtypeStruct((b, l, A_HEADS * A_PAD), BF16),
                 jax.ShapeDtypeStruct((b, l, A_HEADS * A_PAD), BF16),
                 jax.ShapeDtypeStruct((b, l, A_HEADS * A_PAD), BF16),
                 jax.ShapeDtypeStruct((b, l, G_WIDTH), BF16),
                 jax.ShapeDtypeStruct((b, l, 3 * d), BF16)]
    out_specs = [tok(2 * M_WIDTH), tok(2 * M_WIDTH), tok(ng // 2), tok(ng // 2),
                 pl.BlockSpec((1, ng, tm), lambda bi, j: (bi, 0, j)),
                 tok(A_HEADS * A_PAD), tok(A_HEADS * A_PAD), tok(A_HEADS * A_PAD), tok(G_WIDTH), tok(3 * d)]
    if has_f:
        in_specs = [tok(d), modspec] + in_specs
        args = [f, mod_prev] + args
        out_shape.append(jax.ShapeDtypeStruct((b, l, d), F32))
        out_specs.append(tok(d))
    if first:
        in_specs[0] = pl.BlockSpec((1, tm, d), lambda bi, j: (bi, jnp.maximum(j - nct, 0), 0))
        in_specs = [pl.BlockSpec((1, tm, d), lambda bi, j: (bi, jnp.minimum(j, nct - 1), 0))] + in_specs
        args = [ctx] + args
        out_shape.append(jax.ShapeDtypeStruct((b, l, d), F32))
        out_specs.append(tok(d))
    return pl.pallas_call(
        functools.partial(_inproj_kernel, has_f=has_f, nct_first=nct if first else None),
        grid=(b, l // tm), in_specs=in_specs,
        out_specs=out_specs, out_shape=out_shape,
        scratch_shapes=[pltpu.VMEM((tm, w["wa"].shape[1]), F32), pltpu.VMEM((tm, 2 * G_WIDTH), F32),
                        pltpu.VMEM((tm, 3 * d), F32)],
        compiler_params=_params("parallel", "parallel"), name="inproj",
    )(*args)


def _scan(x, op, fill, axis, reverse):
    n = x.shape[axis]
    idx = lax.broadcasted_iota(jnp.int32, x.shape, axis)
    k = 1
    while k < n:
        if reverse:
            x = op(x, jnp.where(idx >= n - k, fill, pltpu.roll(x, n - k, axis)))
        else:
            x = op(x, jnp.where(idx < k, fill, pltpu.roll(x, k, axis)))
        k *= 2
    return x


def _mlstm_kernel(qk_ref, vo_ref, gi_ref, gf_ref, gr_ref, conv_ref, mnorm_ref, ya_ref,
                  q_s, kt_s, h_s, bc_s, ml_s, dl_s, br_s, cn_s, m_s, s_s, p_s, qcn_s, u_s, *, lc):
    l = qk_ref.shape[1]
    ch = M_CHUNK
    nc = l // ch
    ncc = lc // ch
    nh = M_HEADS
    ng = 2 * nh
    w = conv_ref[...]
    row = lax.broadcasted_iota(jnp.int32, (ch, 1), 0)

    def conv_chunk(j):
        r0 = pl.multiple_of(j * ch, ch)
        cur = qk_ref[0, pl.ds(r0, ch), :]
        prev8 = qk_ref[0, pl.ds(pl.multiple_of(jnp.maximum(r0 - 8, 0), 8), 8), :]
        next8 = qk_ref[0, pl.ds(pl.multiple_of(jnp.minimum(r0 + ch, l - 8), 8), 8), :]
        seg_start = jnp.logical_or(j == 0, j == ncc)
        seg_end = jnp.logical_or(j == ncc - 1, j == nc - 1)
        pe = jnp.where(seg_start, 0.0, prev8[7:8, :])
        ne = jnp.where(seg_end, 0.0, next8[0:1, :])
        xp = jnp.where(row == 0, pe, pltpu.roll(cur, 1, 0))
        xn = jnp.where(row == ch - 1, ne, pltpu.roll(cur, ch - 1, 0))
        y = _silu(xp * w[0:1] + cur * w[1:2] + xn * w[2:3])
        q_s[pl.ds(r0, ch), :] = (y[:, :M_WIDTH] * (M_DH ** -0.5)).astype(BF16)
        kt_s[:, pl.ds(r0, ch)] = y[:, M_WIDTH:].T.astype(BF16)

    ri = lax.broadcasted_iota(jnp.int32, (ch, ch), 0)
    ci = lax.broadcasted_iota(jnp.int32, (ch, ch), 1)
    lower = ri >= ci
    upper = ri <= ci
    ones_blk = jnp.ones((ch, M_DH), BF16)
    fwd_c = lax.broadcasted_iota(jnp.int32, (ch, ng), 1) < nh
    fwd_r = lax.broadcasted_iota(jnp.int32, (ng, ch), 0) < nh
    lane_c = lax.broadcasted_iota(jnp.int32, (ch, ng), 1)

    def local_chunk(j):
        r0 = pl.multiple_of(j * ch, ch)
        rows = pl.ds(r0, ch)
        lfc = _log_sigmoid(gf_ref[0, rows, :])
        gr = gr_ref[0, :, rows]
        lfr = _log_sigmoid(gr[ng:])
        pre_c = _scan(lfc, jnp.add, 0.0, 0, False)
        pre_r = _scan(lfr, jnp.add, 0.0, 1, False)
        b_c = jnp.where(fwd_c, pre_c, jnp.sum(lfc, axis=0, keepdims=True) + lfc - pre_c)
        b_r = jnp.where(fwd_r, pre_r, jnp.sum(lfr, axis=1, keepdims=True) + lfr - pre_r)
        g_c = gi_ref[0, rows, :] - b_c
        g_r = gr[:ng] - b_r
        cg_c = jnp.where(fwd_c, _scan(g_c, jnp.maximum, -jnp.inf, 0, False),
                         _scan(g_c, jnp.maximum, -jnp.inf, 0, True))
        bc_s[rows, :] = b_c
        br_s[:, rows] = b_r
        ml_s[rows, :] = b_c + cg_c
        dl = jnp.zeros((ch, ng), F32)
        for hh in range(nh):
            sl = slice(hh * M_DH, (hh + 1) * M_DH)
            s_s[hh] = _dot(q_s[rows, sl], kt_s[sl, rows])
        for hh in range(nh):
            s = s_s[hh]
            for d in range(2):
                jj = d * nh + hh
                wgt = jnp.exp(jnp.where(upper if d else lower, g_r[jj:jj + 1, :] - cg_c[:, jj:jj + 1], -jnp.inf))
                p_s[hh, d * ch:(d + 1) * ch] = (s * wgt).astype(BF16)
        for hh in range(nh):
            sl = slice(hh * M_DH, (hh + 1) * M_DH)
            v1 = jnp.concatenate([vo_ref[0, rows, sl], ones_blk], axis=1)
            nd2 = _dot(p_s[hh], v1)
            for d in range(2):
                jj = d * nh + hh
                nd = nd2[d * ch:(d + 1) * ch]
                h_s[d, rows, sl] = nd[:, :M_DH]
                dl = jnp.where(lane_c == jj, nd[:, M_DH:M_DH + ng], dl)
        dl_s[rows, :] = dl

    def conv_local_body(j, carry):
        conv_chunk(j + 1)
        local_chunk(j)
        return carry

    conv_chunk(jnp.int32(0))
    lax.fori_loop(0, nc - 1, conv_local_body, 0)
    local_chunk(jnp.int32(nc - 1))

    cn_s[...] = jnp.zeros_like(cn_s)
    m_s[...] = jnp.zeros_like(m_s)
    lane_r = lax.broadcasted_iota(jnp.int32, (1, ng), 1)

    def scan_issue(r0, d):
        rows = pl.ds(r0, ch)
        gr = gr_ref[0, :, rows]
        br = br_s[:, rows]
        tot = jnp.sum(_log_sigmoid(gr[ng:]), axis=1, keepdims=True)
        scal = []
        for hh in range(nh):
            fi = d * nh + hh
            sl = slice(hh * M_DH, (hh + 1) * M_DH)
            qcn_s[fi] = _dot(q_s[rows, sl], cn_s[fi].astype(BF16))
            m_old = m_s[fi][:, 0:1]
            b_e = tot[fi:fi + 1, :]
            d_end = b_e - br[fi:fi + 1, :] + gr[fi:fi + 1, :]
            m_end = jnp.max(d_end, axis=-1, keepdims=True)
            m_new = jnp.maximum(b_e + m_old, m_end)
            ktw = (kt_s[sl, rows].astype(F32) * jnp.exp(d_end - m_end)).astype(BF16)
            v1 = jnp.concatenate([vo_ref[0, rows, sl], ones_blk], axis=1)
            u_s[fi] = _dot(ktw, v1)
            scal.append((m_old, m_new, jnp.exp(b_e + m_old - m_new), jnp.exp(m_end - m_new)))
        return scal

    def scan_finish(r0, d, scal):
        rows = pl.ds(r0, ch)
        m_row = jnp.zeros((1, ng), F32)
        for hh in range(nh):
            m_row = jnp.where(lane_r == d * nh + hh, scal[hh][0], m_row)

        inter = bc_s[rows, :] + m_row
        ml = ml_s[rows, :]
        mt = jnp.maximum(inter, ml)
        a = jnp.exp(ml - mt)
        wi = jnp.exp(inter - mt)
        qn = jnp.zeros((ch, ng), F32)
        for hh in range(nh):
            fi = d * nh + hh
            qn = jnp.where(lane_c == fi, qcn_s[fi, :, M_DH:M_DH + ng], qn)
        den = a * dl_s[rows, :] + wi * qn
        rinv = 1.0 / jnp.maximum(jnp.abs(den), jnp.exp(-mt))
        c_loc = a * rinv
        c_int = wi * rinv

        for hh in range(nh):
            fi = d * nh + hh
            sl = slice(hh * M_DH, (hh + 1) * M_DH)
            h_s[d, rows, sl] = (c_loc[:, fi:fi + 1] * h_s[d, rows, sl]
                                + c_int[:, fi:fi + 1] * qcn_s[fi, :, :M_DH])
            cn_s[fi] = scal[hh][2] * cn_s[fi] + scal[hh][3] * u_s[fi]
            m_s[fi] = jnp.broadcast_to(scal[hh][1], (1, M_DH))

    def scan_body(s, carry):
        rf = pl.multiple_of(s * ch, ch)
        rb = pl.multiple_of(jnp.where(s < ncc, ncc - 1 - s, nc - 1 - s + ncc) * ch, ch)
        sf = scan_issue(rf, 0)
        sb = scan_issue(rb, 1)
        scan_finish(rf, 0, sf)
        scan_finish(rb, 1, sb)
        return carry

    lax.fori_loop(0, nc, scan_body, 0)

    mnorm = mnorm_ref[...]

    def out_body(j, carry):
        r0 = pl.multiple_of(j * ch, ch)
        hsum = h_s[0, pl.ds(r0, ch), :] + h_s[1, pl.ds(r0, ch), :]
        og = _sigmoid(vo_ref[0, pl.ds(r0, ch), M_WIDTH:].astype(F32))
        for hh in range(M_HEADS):
            sl = slice(hh * M_DH, (hh + 1) * M_DH)
            ya_ref[0, pl.ds(r0, ch), sl] = (_rms(hsum[:, sl], mnorm[:, sl]) * og[:, sl]).astype(BF16)
        return carry

    lax.fori_loop(0, nc, out_body, 0)


def _mlstm_call(qk, vo, gi, gf, gr, conv, mnorm, lc):
    b, l, _ = qk.shape
    ng = 2 * M_HEADS
    return pl.pallas_call(
        functools.partial(_mlstm_kernel, lc=lc),
        grid=(b,),
        in_specs=[pl.BlockSpec((1, l, 2 * M_WIDTH), lambda bi: (bi, 0, 0), pipeline_mode=pl.Buffered(1)),
                  pl.BlockSpec((1, l, 2 * M_WIDTH), lambda bi: (bi, 0, 0)),
                  pl.BlockSpec((1, l, ng), lambda bi: (bi, 0, 0)),
                  pl.BlockSpec((1, l, ng), lambda bi: (bi, 0, 0)),
                  pl.BlockSpec((1, 2 * ng, l), lambda bi: (bi, 0, 0)),
                  _const_spec(conv.shape), _const_spec(mnorm.shape)],
        out_specs=pl.BlockSpec((1, l, M_WIDTH), lambda bi: (bi, 0, 0)),
        out_shape=jax.ShapeDtypeStruct((b, l, M_WIDTH), BF16),
        scratch_shapes=[pltpu.VMEM((l, M_WIDTH), BF16),
                        pltpu.VMEM((M_WIDTH, l), BF16),
                        pltpu.VMEM((2, l, M_WIDTH), F32),
                        pltpu.VMEM((l, ng), F32),
                        pltpu.VMEM((l, ng), F32),
                        pltpu.VMEM((l, ng), F32),
                        pltpu.VMEM((ng, l), F32),
                        pltpu.VMEM((2 * M_HEADS, M_DH, 2 * M_DH), F32),
                        pltpu.VMEM((2 * M_HEADS, 1, M_DH), F32),
                        pltpu.VMEM((M_HEADS, M_CHUNK, M_CHUNK), F32),
                        pltpu.VMEM((M_HEADS, 2 * M_CHUNK, M_CHUNK), BF16),
                        pltpu.VMEM((2 * M_HEADS, M_CHUNK, 2 * M_DH), F32),
                        pltpu.VMEM((2 * M_HEADS, M_DH, 2 * M_DH), F32)],
        compiler_params=_params("parallel"), name="mlstm",
    )(qk, vo, gi, gf, gr, conv, mnorm)


def _attn_kernel(q_ref, k_ref, v_ref, o_ref, s_s, p_s, *, lc, ctx_out):
    tq = q_ref.shape[1]
    l = k_ref.shape[1]
    qi = pl.program_id(2)
    nct = lc // tq
    lane = lax.broadcasted_iota(jnp.int32, (tq, 2 * A_VDIM), 1)

    def run(klen):
        outs = []
        for hh in range(A_HPS):
            sl = slice(hh * A_PAD, (hh + 1) * A_PAD)
            s_s[hh, :, :klen] = _dot_nt(q_ref[0, :, sl], k_ref[0, :klen, sl])
        row_max = [jnp.max(s_s[hh, :, :klen], axis=-1, keepdims=True) for hh in range(A_HPS)]
        for hh in range(A_HPS):
            p_s[hh, :, :klen] = jnp.exp2((s_s[hh, :, :klen] - row_max[hh]).astype(BF16))
        for hh in range(A_HPS):
            sl = slice(hh * A_PAD, (hh + 1) * A_PAD)
            nd = _dot(p_s[hh, :, :klen], v_ref[0, :klen, sl])
            outs.append(nd / pltpu.roll(nd, A_VDIM, 1))
        for pp in range(A_HPS // 2):
            o_ref[0, :, pp * A_PAD:(pp + 1) * A_PAD] = jnp.where(
                lane < A_VDIM, outs[2 * pp], pltpu.roll(outs[2 * pp + 1], A_VDIM, 1)).astype(BF16)

    @pl.when(qi >= nct)
    def _():
        run(l)

    @pl.when(qi < nct)
    def _():
        if ctx_out:
            run(lc)
        else:
            o_ref[...] = jnp.zeros_like(o_ref)


def _attn_call(q, k, v, lc, tq, ctx_out):
    b, l, _ = q.shape
    return pl.pallas_call(
        functools.partial(_attn_kernel, lc=lc, ctx_out=ctx_out),
        grid=(b, A_HEADS // A_HPS, l // tq),
        in_specs=[pl.BlockSpec((1, tq, A_HPS * A_PAD), lambda bi, p, qi: (bi, qi, p)),
                  pl.BlockSpec((1, l, A_HPS * A_PAD), lambda bi, p, qi: (bi, 0, p)),
                  pl.BlockSpec((1, l, A_HPS * A_PAD), lambda bi, p, qi: (bi, 0, p))],
        out_specs=pl.BlockSpec((1, tq, A_HPS * A_VDIM), lambda bi, p, qi: (bi, qi, p)),
        out_shape=jax.ShapeDtypeStruct((b, l, A_WIDTH), BF16),
        scratch_shapes=[pltpu.VMEM((A_HPS, tq, l), F32), pltpu.VMEM((A_HPS, tq, l), BF16)],
        compiler_params=_params("parallel", "parallel", "arbitrary"), name="attn",
    )(q, k, v)


def _merge_kernel(x_ref, mod_ref, modc_ref, ya_ref, yb_ref, yc_ref, br_ref, wpa_ref, wpb_ref, wpc_ref, wout_ref,
                  n2_ref, r_ref, rb_ref, o_ref, h2_o, idx_o, idxt_o, cnt_o, y_s, *, tm, nct, skip_ctx):
    d = x_ref.shape[2]
    nsub = x_ref.shape[1] // tm
    modb = mod_ref[0]
    modc = modc_ref[0]
    tiles = [slice(s * tm, (s + 1) * tm) for s in range(nsub)]
    ctx = [pl.program_id(1) * nsub + s < nct for s in range(nsub)]
    mods = [jnp.where(c, modc, modb) for c in ctx]
    for s, rows in enumerate(tiles):
        br = br_ref[0, rows, :]
        y = (br[:, :d].astype(F32) * _dot(ya_ref[0, rows, :], wpa_ref[...])
             + br[:, d:2 * d].astype(F32) * _dot(yb_ref[0, rows, :], wpb_ref[...])
             + br[:, 2 * d:].astype(F32) * _dot(yc_ref[0, rows, :], wpc_ref[...]))
        y_s[s] = y.astype(BF16)
    for s, rows in enumerate(tiles):
        o_ref[0, rows, :] = x_ref[0, rows, :] + mods[s][2:3] * _dot(y_s[s], wout_ref[...])
    for s, rows in enumerate(tiles):
        logits = _route_logits(o_ref[0, rows, :], mods[s], n2_ref, r_ref, rb_ref, h2_o, rows)
        left_out = ctx[s] if skip_ctx else None
        _route_assign(logits, left_out, d, h2_o, idx_o, idxt_o, cnt_o, rows, s)


def _merge_call(xs, mod, ya, yb, yc, br, w, n2, r3, rb, lc, tm, skip_ctx):
    b, l, d = xs.shape
    nct = lc // tm
    nsub = next(n for n in (3, 2, 1) if l % (n * tm) == 0)
    tg = nsub * tm
    tok = lambda width: pl.BlockSpec((1, tg, width), lambda bi, j: (bi, j, 0))
    consts = [w["wpa"], w["wpb"], w["wpc"], w["wout"], n2, r3, rb]
    return pl.pallas_call(
        functools.partial(_merge_kernel, tm=tm, nct=nct, skip_ctx=skip_ctx), grid=(b, l // tg),
        in_specs=[tok(d), pl.BlockSpec((1, 6, d), lambda bi, j: (bi, 0, 0)),
                  pl.BlockSpec((1, 6, d), lambda bi, j: (b, 0, 0)),
                  tok(M_WIDTH), tok(A_WIDTH), tok(G_WIDTH), tok(3 * d)] + [_const_spec(a.shape) for a in consts],
        out_specs=[tok(d), tok(d + R_PAD), tok(8), pl.BlockSpec((1, 8, tg), lambda bi, j: (bi, 0, j)),
                   pl.BlockSpec((1, nsub, 8, R_PAD), lambda bi, j: (bi, j, 0, 0))],
        out_shape=[jax.ShapeDtypeStruct((b, l, d), F32),
                   jax.ShapeDtypeStruct((b, l, d + R_PAD), BF16),
                   jax.ShapeDtypeStruct((b, l, 8), jnp.int32), jax.ShapeDtypeStruct((b, 8, l), jnp.int32),
                   jax.ShapeDtypeStruct((b, l // tm, 8, R_PAD), F32)],
        scratch_shapes=[pltpu.VMEM((nsub, tm, d), BF16)],
        compiler_params=_params("parallel", "parallel"), name="merge",
    )(xs, mod, mod, ya, yb, yc, br, *consts)


def _route_logits(x, mod, n2_ref, r_ref, rb_ref, h2_o, rows):
    d = x.shape[1]
    h2 = _rms(x, n2_ref[...]) * (1.0 + mod[4:5]) + mod[3:4]
    h2_o[0, rows, :d] = h2.astype(BF16)
    r = r_ref[...]
    pp = sum(_dot(piece, r) for piece in _split3(h2)[:2])
    return pp + pltpu.roll(pp, R_PAD - R_SEG, 1) + pltpu.roll(pp, R_PAD - 2 * R_SEG, 1) + rb_ref[...]


def _route_assign(logits, left_out, d, h2_o, idx_o, idxt_o, cnt_o, rows, s):
    tm = logits.shape[0]
    el = logits[:, :N_EXPERTS]
    gl = logits[:, N_EXPERTS:N_EXPERTS + N_GROUPS]
    big = 1e9

    lane_g = lax.broadcasted_iota(jnp.int32, (tm, N_GROUPS), 1).astype(F32)
    gmax = jnp.max(gl, axis=-1, keepdims=True)
    g_sel = jnp.min(jnp.where(gl == gmax, lane_g, big), axis=-1, keepdims=True)
    g_prob = 1.0 / jnp.sum(jnp.exp(gl - gmax), axis=-1, keepdims=True)

    lane_i = lax.broadcasted_iota(jnp.int32, (tm, N_EXPERTS), 1)
    lane_e = lane_i.astype(F32)
    lane_grp = (lane_i // EXP_PER_GROUP).astype(F32)
    v1 = jnp.where(lane_grp == g_sel, el, -jnp.inf)
    t1 = jnp.max(v1, axis=-1, keepdims=True)
    i1 = jnp.min(jnp.where(v1 == t1, lane_e, big), axis=-1, keepdims=True)
    v2 = jnp.where(lane_e == i1, -jnp.inf, v1)
    t2 = jnp.max(v2, axis=-1, keepdims=True)
    i2 = jnp.min(jnp.where(v2 == t2, lane_e, big), axis=-1, keepdims=True)
    e21 = jnp.exp(t2 - t1)
    w1 = 1.0 / (1.0 + e21)
    w2 = e21 * w1
    comb = (jnp.where(lane_e == i1, w1, 0.0) + jnp.where(lane_e == i2, w2, 0.0)) * g_prob
    tail = jnp.zeros((tm, R_PAD - 3 * N_EXPERTS), BF16)
    h2_o[0, rows, d:] = jnp.concatenate(list(_split3(comb)) + [tail], axis=1)

    if left_out is not None:
        g_sel = jnp.where(left_out, -1.0, g_sel)
    lane_p = lax.broadcasted_iota(jnp.int32, (tm, R_PAD), 1)
    onehot = jnp.where(lane_p.astype(F32) == g_sel, 1.0, 0.0)
    ri = lax.broadcasted_iota(jnp.int32, (tm, tm), 0)
    ci = lax.broadcasted_iota(jnp.int32, (tm, tm), 1)
    before = jnp.where(ri > ci, 1.0, 0.0).astype(BF16)
    rank = jnp.sum(_dot(before, onehot.astype(BF16)) * onehot, axis=-1, keepdims=True)
    cnt_o[0, s] = jnp.broadcast_to(jnp.sum(onehot, axis=0, keepdims=True), (8, R_PAD))
    fields = jnp.where(lane_p == 0, g_sel, jnp.where(lane_p == 1, rank, 0.0))
    idx_o[0, rows, :] = fields[:, :8].astype(jnp.int32)
    idxt_o[0, :, rows] = fields.T[:8, :].astype(jnp.int32)


def _experts_kernel(st_ref, h2_ref, idx_ref, idxt_ref, w1_ref, w3_ref, w2_ref, o_ref, hs_s, ys_s, a_s, hid_s, *, tb):
    l = h2_ref.shape[1]
    d = o_ref.shape[2]
    nblk = l // tb
    ch = MOE_CHUNK
    bi = pl.program_id(0)
    g = pl.program_id(1)

    @pl.when(g == 0)
    def _():
        ys_s[...] = jnp.zeros_like(ys_s)

    def group_base(gg):
        return (bi * N_GROUPS + gg) * (nblk + 1)

    def group_offset(upto):
        off = 0
        for gg in range(N_GROUPS - 1):
            padded = ((st_ref[group_base(gg) + nblk] + ch - 1) // ch) * ch
            off = off + jnp.where(gg < upto, padded, 0)
        return off

    base = group_base(g)
    cnt = st_ref[base + nblk]
    goff = group_offset(g)
    gw = min(MOE_GATHER_BLOCKS, nblk)
    nwin = -(-nblk // gw)

    def chunk(lo, ch):
        lane_e = lax.broadcasted_iota(jnp.int32, (ch, N_EXPERTS), 1)
        sub_iota = lax.broadcasted_iota(jnp.int32, (ch, gw * tb), 0)
        lane_blk = lax.broadcasted_iota(jnp.int32, (1, gw * tb), 1) // tb
        k0 = 0
        for k in range(nblk):
            k0 = k0 + jnp.where(st_ref[base + k + 1] <= lo, 1, 0)

        def window(j):
            want = k0 + j * gw
            kj = jnp.minimum(want, nblk - gw)
            tok0 = pl.multiple_of(kj * tb, tb)
            it = idxt_ref[0, :, pl.ds(tok0, gw * tb)]
            offs = jnp.concatenate([jnp.full((1, tb), st_ref[base + kj + i] - lo, jnp.int32) for i in range(gw)],
                                   axis=1)
            mine = jnp.where(lane_blk >= want - kj, it[0:1], -1)
            pos = jnp.where(mine == g, it[1:2] + offs, -1)
            p = jnp.where(sub_iota == pos, 1.0, 0.0).astype(BF16)
            return _dot(p, h2_ref[0, pl.ds(tok0, gw * tb), :])

        hs_s[:ch] = window(0)
        for j in range(1, nwin):
            first = k0 + j * gw

            @pl.when(jnp.logical_and(first < nblk, st_ref[base + jnp.minimum(first, nblk)] < lo + ch))
            def _(j=j):
                hs_s[:ch] += window(j)

        hsb = hs_s[:ch, :d].astype(BF16)
        cs = (hs_s[:ch, d:d + N_EXPERTS] + hs_s[:ch, d + N_EXPERTS:d + 2 * N_EXPERTS]
              + hs_s[:ch, d + 2 * N_EXPERTS:d + 3 * N_EXPERTS])
        for e in range(EXP_PER_GROUP):
            a_s[2 * e, :ch] = _dot(hsb, w1_ref[e])
            a_s[2 * e + 1, :ch] = _dot(hsb, w3_ref[e])
        for e in range(EXP_PER_GROUP):
            ce = jnp.sum(jnp.where(lane_e == g * EXP_PER_GROUP + e, cs, 0.0), axis=-1, keepdims=True)
            hid_s[e, :ch] = (_silu(a_s[2 * e, :ch]) * a_s[2 * e + 1, :ch] * ce).astype(BF16)
        y = jnp.zeros((ch, d), F32)
        for e in range(EXP_PER_GROUP):
            y = y + _dot(hid_s[e, :ch], w2_ref[e])
        ys_s[pl.ds(pl.multiple_of(goff + lo, ch), ch), :] = y.astype(BF16)

    half = ch // 2
    nfull = cnt // ch
    rem = cnt - nfull * ch
    nloop = nfull + jnp.where(rem > half, 1, 0)

    def chunk_body(c, carry):
        chunk(c * ch, ch)
        return carry

    lax.fori_loop(0, nloop, chunk_body, 0)

    @pl.when(jnp.logical_and(rem > 0, rem <= half))
    def _():
        chunk(nfull * ch, half)

    @pl.when(g == N_GROUPS - 1)
    def _():
        lane_w = lax.broadcasted_iota(jnp.int32, (tb, ch), 1)
        goffs = [group_offset(gg) for gg in range(N_GROUPS)]

        def window(k, gg, shift):
            s_k = st_ref[group_base(gg) + k]
            win = (s_k // half) * half
            ic = idx_ref[0, k * tb:(k + 1) * tb, :]
            pos = jnp.where(ic[:, 0:1] == gg, ic[:, 1:2] + (s_k - win - shift), -1)
            q = jnp.where(lane_w == pos, 1.0, 0.0).astype(BF16)
            start = pl.multiple_of(goffs[gg] + win + shift, half)
            return _dot(q, ys_s[pl.ds(start, ch), :])

        for k in range(nblk):
            acc = window(k, 0, 0)
            for gg in range(1, N_GROUPS):
                acc = acc + window(k, gg, 0)
            o_ref[0, k * tb:(k + 1) * tb, :] = acc.astype(BF16)
        for k in range(nblk):
            rows = slice(k * tb, (k + 1) * tb)
            for gg in range(N_GROUPS):
                s_k = st_ref[group_base(gg) + k]
                e_k = st_ref[group_base(gg) + k + 1]

                @pl.when(e_k - (s_k // half) * half > ch)
                def _(k=k, gg=gg, rows=rows):
                    o_ref[0, rows, :] = (o_ref[0, rows, :].astype(F32) + window(k, gg, ch)).astype(BF16)


def _experts_call(starts, h2, idx, idxt, w1, w3, w2, layer, tb):
    b, l, de = h2.shape
    d = de - R_PAD
    whole = lambda width: pl.BlockSpec((1, l, width), lambda bi, g, st: (bi, 0, 0))
    grid_spec = pltpu.PrefetchScalarGridSpec(
        num_scalar_prefetch=1, grid=(b, N_GROUPS),
        in_specs=[pl.BlockSpec((1, l, de), lambda bi, g, st: (bi, 0, 0), pipeline_mode=pl.Buffered(1)),
                  whole(8), pl.BlockSpec((1, 8, l), lambda bi, g, st: (bi, 0, 0)),
                  pl.BlockSpec((None, EXP_PER_GROUP, d, D_EXPERT), lambda bi, g, st: (layer, g, 0, 0)),
                  pl.BlockSpec((None, EXP_PER_GROUP, d, D_EXPERT), lambda bi, g, st: (layer, g, 0, 0)),
                  pl.BlockSpec((None, EXP_PER_GROUP, D_EXPERT, d), lambda bi, g, st: (layer, g, 0, 0))],
        out_specs=whole(d),
        scratch_shapes=[pltpu.VMEM((MOE_CHUNK, de), F32),
                        pltpu.VMEM((l + (N_GROUPS + 1) * MOE_CHUNK, d), BF16),
                        pltpu.VMEM((2 * EXP_PER_GROUP, MOE_CHUNK, D_EXPERT), F32),
                        pltpu.VMEM((EXP_PER_GROUP, MOE_CHUNK, D_EXPERT), BF16)])
    return pl.pallas_call(
        functools.partial(_experts_kernel, tb=tb), grid_spec=grid_spec,
        out_shape=jax.ShapeDtypeStruct((b, l, d), BF16),
        compiler_params=_params("parallel", "arbitrary"), name="experts",
    )(starts, h2, idx, idxt, w1, w3, w2)


def _group_starts(cnt):
    c = cnt[:, :, 0, :N_GROUPS].astype(jnp.int32)
    s = jnp.cumsum(c, axis=1)
    s = jnp.concatenate([jnp.zeros_like(s[:, :1]), s], axis=1)
    return jnp.transpose(s, (0, 2, 1)).reshape(-1)


def _final_kernel(x_ref, mod_ref, f_ref, g_ref, o_ref):
    o_ref[0] = _rms(x_ref[0] + mod_ref[0][5:6] * f_ref[0].astype(F32), g_ref[...])


def _final_call(xs, mod, f, g, lc, tm):
    b, l, d = xs.shape
    tf = next(t for t in (4 * tm, 2 * tm, tm) if (l - lc) % t == 0)
    lat = pl.BlockSpec((pl.Element(1), pl.Element(tf), pl.Element(d)),
                       lambda bi, j: (bi, pl.multiple_of(lc + j * tf, 128), 0))
    return pl.pallas_call(
        _final_kernel, grid=(b, (l - lc) // tf),
        in_specs=[lat, pl.BlockSpec((1, 6, d), lambda bi, j: (bi, 0, 0)), lat, _const_spec(g.shape)],
        out_specs=pl.BlockSpec((1, tf, d), lambda bi, j: (bi, j, 0)),
        out_shape=jax.ShapeDtypeStruct((b, l - lc, d), F32),
        compiler_params=_params("parallel", "parallel"), name="final_norm",
    )(xs, mod, f, g)


def _rope_tables(t_len, lc):
    half = A_ROPE // 2
    rows = t_len // GRID_W
    r = jnp.repeat(jnp.arange(rows, dtype=F32), GRID_W)
    col = jnp.tile(jnp.arange(GRID_W, dtype=F32), rows)
    inv = ROPE_THETA ** (-jnp.arange(0, half, 2, dtype=F32) / half)
    ang = jnp.concatenate([r[:, None] * inv, col[:, None] * inv], axis=-1)
    cos = jnp.concatenate([jnp.ones((lc, half), F32), jnp.cos(ang)], axis=0)
    sin = jnp.concatenate([jnp.zeros((lc, half), F32), jnp.sin(ang)], axis=0)
    l = lc + t_len
    ones = jnp.ones((l, A_NOPE), F32)
    zeros = jnp.zeros((l, A_NOPE), F32)
    tail1 = jnp.ones((l, A_PAD - A_NOPE - A_ROPE), F32)
    tail0 = jnp.zeros((l, A_PAD - A_NOPE - A_ROPE), F32)
    zh = jnp.zeros((l, half), F32)
    cos_t = jnp.concatenate([ones, cos, cos, tail1], axis=-1)
    sina_t = jnp.concatenate([zeros, zh, sin, tail0], axis=-1)
    sinb_t = jnp.concatenate([zeros, -sin, zh, tail0], axis=-1)
    return cos_t, sina_t, sinb_t


def _layer_weights(l, w_in, m_gate_b, a_qnorm, a_wuq, a_kvnorm, a_wukv, g_ws, g_bs, g_vnorm,
                   w_pa, w_pb, w_pc, w_out):
    d = w_in.shape[1]
    wi = w_in[l]
    o = 0

    def take(n):
        nonlocal o
        s = wi[:, o:o + n]
        o += n
        return s

    mq, mk, mv, mo, mg = take(M_WIDTH), take(M_WIDTH), take(M_WIDTH), take(M_WIDTH), take(4 * M_HEADS)
    aq, akv, akr = take(A_QRANK), take(A_KVRANK), take(A_ROPE)
    gu, gv = take(G_WIDTH), take(G_WIDTH)
    br = take(3 * d)
    nh = M_HEADS
    gb = m_gate_b[l]
    mgo = jnp.concatenate([mg[:, :nh], mg[:, 2 * nh:3 * nh], mg[:, nh:2 * nh], mg[:, 3 * nh:]], axis=1)
    gbo = jnp.concatenate([gb[:nh], gb[2 * nh:3 * nh], gb[nh:2 * nh], gb[3 * nh:]])
    akr_pad = jnp.concatenate([jnp.zeros((d, A_NOPE), F32), akr,
                               jnp.zeros((d, A_PAD - A_NOPE - A_ROPE), F32)], axis=1)
    wuq = a_wuq[l].reshape(A_QRANK, A_HEADS, A_NOPE + A_ROPE)
    wuq = jnp.pad(wuq, ((0, 0), (0, 0), (0, A_PAD - A_NOPE - A_ROPE))).reshape(A_QRANK, A_HEADS * A_PAD)
    wukv = a_wukv[l].reshape(A_KVRANK, A_HEADS, A_NOPE + A_VDIM)
    wuk = jnp.pad(wukv[:, :, :A_NOPE], ((0, 0), (0, 0), (0, A_PAD - A_NOPE))).reshape(A_KVRANK, A_HEADS * A_PAD)
    wuv = jnp.pad(wukv[:, :, A_NOPE:], ((0, 0), (0, 0), (0, A_PAD - A_VDIM))).reshape(A_KVRANK, A_HEADS * A_PAD)
    vone = jnp.tile(jnp.concatenate([jnp.zeros((A_VDIM,), F32), jnp.ones((A_PAD - A_VDIM,), F32)]),
                    A_HEADS).reshape(1, A_HEADS * A_PAD)
    gbs = jnp.repeat(g_bs[l].T, G_DG, axis=1)
    return dict(
        wqk=jnp.concatenate([mq, mk], 1).astype(BF16), wvo=jnp.concatenate([mv, mo], 1).astype(BF16),
        wgt=jnp.pad(mgo, ((0, 0), (0, G_PAD - 4 * nh))).astype(BF16),
        gbt=jnp.pad(gbo, (0, G_PAD - 4 * nh)).reshape(1, G_PAD),
        wa=jnp.concatenate([aq, akv, akr_pad], 1).astype(BF16),
        wg=jnp.concatenate([gu, gv], 1).astype(BF16), wbr=br.astype(BF16),
        aqn=a_qnorm[l].reshape(1, -1), akvn=a_kvnorm[l].reshape(1, -1),
        wuq=wuq.astype(BF16), wuk=wuk.astype(BF16), wuv=wuv.astype(BF16), vone=vone,
        gvn=g_vnorm[l].reshape(1, -1), gws=g_ws[l].astype(BF16), gbs=gbs,
        wpa=w_pa[l].astype(BF16), wpb=w_pb[l].astype(BF16), wpc=w_pc[l].astype(BF16),
        wout=w_out[l].astype(BF16))


def _router_weights(r_group, r_group_b, r_expert, r_expert_b):
    d = r_group.shape[0]
    pad = R_SEG - N_EXPERTS - N_GROUPS
    r = jnp.concatenate([r_expert, r_group, jnp.zeros((d, pad), F32)], axis=1)
    r3 = jnp.concatenate(list(_split3(r)) + [jnp.zeros((d, R_PAD - 3 * R_SEG), BF16)], axis=1)
    rb = jnp.concatenate([r_expert_b, r_group_b, jnp.zeros((R_PAD - N_EXPERTS - N_GROUPS,), F32)])
    return r3, rb.reshape(1, R_PAD)


def _tile(n, lc, candidates):
    for t in candidates:
        if n % t == 0 and lc % t == 0:
            return t
    raise ValueError("sequence lengths must be multiples of 128")


def kernel(x, c, ctx, c_ctx, w_ada, b_ada, norm1, norm2, final_norm, w_in, m_conv, m_gate_b, m_norm, a_qnorm, a_wuq, a_kvnorm, a_wukv, g_ws, g_bs, g_vnorm, w_pa, w_pb, w_pc, w_out, r_group, r_group_b, r_expert, r_expert_b, e_w1, e_w3, e_w2):
    b, t_len, d = x.shape
    lc = ctx.shape[1]
    l = lc + t_len
    depth = w_in.shape[0]
    tm = _tile(l, lc, (256, 128))

    xs = (ctx, x)
    cv = jnp.concatenate([c, c_ctx[None, :]], axis=0)
    mod_all = _ada_call(cv, w_ada, b_ada).reshape(depth, b + 1, 6, d)
    tabs = _rope_tables(t_len, lc)

    w_in_b = w_in.astype(BF16)
    e_w1_b, e_w3_b, e_w2_b = e_w1.astype(BF16), e_w3.astype(BF16), e_w2.astype(BF16)
    f = None
    mod_prev = None
    for li in range(depth):
        last = li == depth - 1
        mod = mod_all[li]
        w = _layer_weights(li, w_in_b, m_gate_b, a_qnorm, a_wuq, a_kvnorm, a_wukv, g_ws, g_bs, g_vnorm,
                           w_pa, w_pb, w_pc, w_out)
        outs = _inproj_call(xs, mod, norm1[li].reshape(1, d), w, tabs, lc, tm, f, mod_prev)
        qk, vo, gi, gf, gr, q, k, v, yc, br = outs[:10]
        if len(outs) > 10:
            xs = outs[10]
        ya = _mlstm_call(qk, vo, gi, gf, gr, m_conv[li], m_norm[li].reshape(1, -1), lc)
        yb = _attn_call(q, k, v, lc, tm, not last)
        r3, rb = _router_weights(r_group[li], r_group_b[li], r_expert[li], r_expert_b[li])
        xs, h2, idx, idxt, cnt = _merge_call(xs, mod, ya, yb, yc, br, w, norm2[li].reshape(1, d),
                                             r3, rb, lc, tm, last)
        f = _experts_call(_group_starts(cnt), h2, idx, idxt, e_w1_b, e_w3_b, e_w2_b, li, tm)
        mod_prev = mod
    return _final_call(xs, mod_prev, f, final_norm.reshape(1, d), lc, tm)
```

```python
import functools

import jax
import jax.numpy as jnp
from jax import lax
from jax.experimental import pallas as pl
from jax.experimental.pallas import tpu as pltpu

F32 = jnp.float32
BF16 = jnp.bfloat16

EPS = 1e-6
GRID_W = 64
ROPE_THETA = 10000.0

M_HEADS = 4
M_DH = 128
M_WIDTH = M_HEADS * M_DH
M_CHUNK = 128
G_PAD = 128

A_HEADS = 8
A_NOPE = 64
A_ROPE = 32
A_VDIM = 64
A_QRANK = 384
A_KVRANK = 256
A_WIDTH = A_HEADS * A_VDIM
A_PAD = 128
A_HPS = 4
ATT_SCALE = (A_NOPE + A_ROPE) ** -0.5
LOG2E = 1.4426950408889634

G_GROUPS = 4
G_CHUNK = 128
G_WIDTH = 512
G_DG = G_WIDTH // G_GROUPS

N_GROUPS = 4
EXP_PER_GROUP = 4
N_EXPERTS = N_GROUPS * EXP_PER_GROUP
D_EXPERT = 512
R_PAD = 128
R_SEG = 32
MOE_CHUNK = 256
MOE_GATHER_BLOCKS = 5

VMEM_LIMIT = 56 * 1024 * 1024


def _dot(a, b):
    return jnp.dot(a, b, preferred_element_type=F32)


def _dot_nt(a, b):
    return lax.dot_general(a, b, (((1,), (1,)), ((), ())), preferred_element_type=F32)


def _split3(x):
    hi = x.astype(BF16)
    r = x - hi.astype(F32)
    mid = r.astype(BF16)
    lo = (r - mid.astype(F32)).astype(BF16)
    return hi, mid, lo


def _sigmoid(x):
    return 1.0 / (1.0 + jnp.exp(-x))


def _silu(x):
    return x * _sigmoid(x)


def _log_sigmoid(x):
    return jnp.minimum(x, 0.0) - jnp.log1p(jnp.exp(-jnp.abs(x)))


def _gelu(x):
    return 0.5 * x * (1.0 + lax.erf(x * (2.0 ** -0.5)))


def _rms(x, g):
    return x * lax.rsqrt(jnp.mean(x * x, axis=-1, keepdims=True) + EPS) * g


def _params(*sem):
    return pltpu.CompilerParams(dimension_semantics=sem, vmem_limit_bytes=VMEM_LIMIT)


def _const_spec(shape):
    nd = len(shape)
    return pl.BlockSpec(shape, lambda *_: (0,) * nd, pipeline_mode=pl.Buffered(1))


def _ada_kernel(cv_ref, w_ref, b_ref, o_ref):
    s = _silu(cv_ref[...])
    o_ref[0] = _dot(s.astype(BF16), w_ref[0].astype(BF16)) + b_ref[0]


def _ada_call(cv, w_ada, b_ada):
    depth, d, n6 = w_ada.shape
    rows = cv.shape[0]
    tn = n6 // 4
    return pl.pallas_call(
        _ada_kernel,
        grid=(depth, n6 // tn),
        in_specs=[pl.BlockSpec((rows, d), lambda l, j: (0, 0)),
                  pl.BlockSpec((1, d, tn), lambda l, j: (l, 0, j)),
                  pl.BlockSpec((1, 1, tn), lambda l, j: (l, 0, j))],
        out_specs=pl.BlockSpec((1, rows, tn), lambda l, j: (l, 0, j)),
        out_shape=jax.ShapeDtypeStruct((depth, rows, n6), F32),
        compiler_params=_params("parallel", "parallel"),
        name="ada",
    )(cv, w_ada, b_ada.reshape(depth, 1, n6))


def _inproj_kernel(*refs, has_f, nct_first):
    za_s, zg_s, zbr_s = refs[-3:]
    refs = refs[:-3]
    if nct_first is not None:
        ctx_ref, x_o = refs[0], refs[-1]
        refs = refs[1:-1]
    if has_f:
        f_ref, modp_ref, x_o = refs[0], refs[1], refs[-1]
        refs = refs[2:-1]
    (x_ref, mod_ref, n1_ref, wqk_ref, wvo_ref, wgt_ref, gbt_ref,
     wa_ref, wg_ref, wbr_ref, aqn_ref, akvn_ref, wuq_ref, wuk_ref, wuv_ref, vone_ref,
     cos_ref, sina_ref, sinb_ref, gvn_ref, gws_ref, gbs_ref,
     qk_o, vo_o, gi_o, gf_o, gr_o, q_o, k_o, v_o, yc_o, br_o) = refs
    tm = x_ref.shape[1]
    mod = mod_ref[0]
    x = x_ref[0]
    if nct_first is not None:
        x = jnp.where(pl.program_id(1) < nct_first, ctx_ref[0], x)
        x_o[0] = x
    if has_f:
        x = x + modp_ref[0][5:6] * f_ref[0].astype(F32)
        x_o[0] = x
    h = _rms(x, n1_ref[...]) * (1.0 + mod[1:2]) + mod[0:1]
    hb = h.astype(BF16)

    qk_o[0] = _dot(hb, wqk_ref[...])
    vo_o[0] = _dot(hb, wvo_ref[...]).astype(BF16)
    ng = gi_o.shape[2]
    gates = _dot(hb, wgt_ref[...]) + gbt_ref[...]
    gi_o[0] = gates[:, :ng]
    gf_o[0] = pltpu.roll(gates, gates.shape[1] - ng, 1)[:, :ng]
    gr_o[0] = gates.T[:2 * ng, :]

    za_s[...] = _dot(hb, wa_ref[...])
    zg_s[...] = _dot(hb, wg_ref[...])
    zbr_s[...] = _dot(hb, wbr_ref[...])

    aqn = _rms(za_s[:, :A_QRANK], aqn_ref[...]).astype(BF16)
    akvn = _rms(za_s[:, A_QRANK:A_QRANK + A_KVRANK], akvn_ref[...]).astype(BF16)
    cos = cos_ref[...]
    sina = sina_ref[...]
    sinb = sinb_ref[...]
    half = A_ROPE // 2

    def rope(t):
        return t * cos + pltpu.roll(t, half, 1) * sina + pltpu.roll(t, A_PAD - half, 1) * sinb

    kr = rope(za_s[:, A_QRANK + A_KVRANK:])
    qp = _dot(aqn, wuq_ref[...])
    kp = _dot(akvn, wuk_ref[...])
    for hh in range(A_HEADS):
        sl = slice(hh * A_PAD, (hh + 1) * A_PAD)
        q_o[0, :, sl] = (rope(qp[:, sl]) * (ATT_SCALE * LOG2E)).astype(BF16)
        k_o[0, :, sl] = (kp[:, sl] + kr).astype(BF16)
    v_o[0] = (_dot(akvn, wuv_ref[...]) + vone_ref[...]).astype(BF16)

    gu = _gelu(zg_s[:, :G_WIDTH])
    gv = _gelu(zg_s[:, G_WIDTH:])
    gvn = gvn_ref[...]
    bias = gbs_ref[...]
    for g in range(G_GROUPS):
        sl = slice(g * G_DG, (g + 1) * G_DG)
        xn = _rms(gv[:, sl], gvn[:, sl]).astype(BF16)
        ws = gws_ref[g]
        for ci in range(tm // G_CHUNK):
            r = slice(ci * G_CHUNK, (ci + 1) * G_CHUNK)
            sg = _dot(ws, xn[r]) + bias[:, sl]
            yc_o[0, r, sl] = (gu[r, sl] * sg).astype(BF16)

    br_o[0] = _sigmoid(zbr_s[...]).astype(BF16)


def _inproj_call(xs, mod, n1, w, tabs, lc, tm, f=None, mod_prev=None):
    first = isinstance(xs, tuple)
    if first:
        ctx, xs = xs
        b, t_len, d = xs.shape
        l = lc + t_len
    else:
        b, l, d = xs.shape
    nct = lc // tm
    has_f = f is not None
    tok = lambda width: pl.BlockSpec((1, tm, width), lambda bi, j: (bi, j, 0))
    modspec = pl.BlockSpec((1, 6, d), lambda bi, j: (jnp.where(j < nct, b, bi), 0, 0))
    tab = pl.BlockSpec((tm, A_PAD), lambda bi, j: (j, 0))
    consts = [n1, w["wqk"], w["wvo"], w["wgt"], w["gbt"], w["wa"], w["wg"], w["wbr"],
              w["aqn"], w["akvn"], w["wuq"], w["wuk"], w["wuv"], w["vone"]]
    consts2 = [w["gvn"], w["gws"], w["gbs"]]
    in_specs = ([tok(d), modspec] + [_const_spec(a.shape) for a in consts] + [tab, tab, tab]
                + [_const_spec(a.shape) for a in consts2])
    args = [xs, mod, *consts, *tabs, *consts2]
    ng = 4 * M_HEADS
    out_shape = [jax.ShapeDtypeStruct((b, l, 2 * M_WIDTH), F32),
                 jax.ShapeDtypeStruct((b, l, 2 * M_WIDTH), BF16),
                 jax.ShapeDtypeStruct((b, l, ng // 2), F32),
                 jax.ShapeDtypeStruct((b, l, ng // 2), F32),
                 jax.ShapeDtypeStruct((b, ng, l), F32),
                 jax.ShapeDtypeStruct((b, l, A_HEADS * A_PAD), BF16),
                 jax.ShapeDtypeStruct((b, l, A_HEADS * A_PAD), BF16),
                 jax.ShapeDtypeStruct((b, l, A_HEADS * A_PAD), BF16),
                 jax.ShapeDtypeStruct((b, l, G_WIDTH), BF16),
                 jax.ShapeDtypeStruct((b, l, 3 * d), BF16)]
    out_specs = [tok(2 * M_WIDTH), tok(2 * M_WIDTH), tok(ng // 2), tok(ng // 2),
                 pl.BlockSpec((1, ng, tm), lambda bi, j: (bi, 0, j)),
                 tok(A_HEADS * A_PAD), tok(A_HEADS * A_PAD), tok(A_HEADS * A_PAD), tok(G_WIDTH), tok(3 * d)]
    if has_f:
        in_specs = [tok(d), modspec] + in_specs
        args = [f, mod_prev] + args
        out_shape.append(jax.ShapeDtypeStruct((b, l, d), F32))
        out_specs.append(tok(d))
    if first:
        in_specs[0] = pl.BlockSpec((1, tm, d), lambda bi, j: (bi, jnp.maximum(j - nct, 0), 0))
        in_specs = [pl.BlockSpec((1, tm, d), lambda bi, j: (bi, jnp.minimum(j, nct - 1), 0))] + in_specs
        args = [ctx] + args
        out_shape.append(jax.ShapeDtypeStruct((b, l, d), F32))
        out_specs.append(tok(d))
    return pl.pallas_call(
        functools.partial(_inproj_kernel, has_f=has_f, nct_first=nct if first else None),
        grid=(b, l // tm), in_specs=in_specs,
        out_specs=out_specs, out_shape=out_shape,
        scratch_shapes=[pltpu.VMEM((tm, w["wa"].shape[1]), F32), pltpu.VMEM((tm, 2 * G_WIDTH), F32),
                        pltpu.VMEM((tm, 3 * d), F32)],
        compiler_params=_params("parallel", "parallel"), name="inproj",
    )(*args)


def _scan(x, op, fill, axis, reverse):
    n = x.shape[axis]
    idx = lax.broadcasted_iota(jnp.int32, x.shape, axis)
    k = 1
    while k < n:
        if reverse:
            x = op(x, jnp.where(idx >= n - k, fill, pltpu.roll(x, n - k, axis)))
        else:
            x = op(x, jnp.where(idx < k, fill, pltpu.roll(x, k, axis)))
        k *= 2
    return x


def _mlstm_kernel(qk_ref, vo_ref, gi_ref, gf_ref, gr_ref, conv_ref, mnorm_ref, ya_ref,
                  q_s, kt_s, h_s, bc_s, ml_s, dl_s, br_s, cn_s, m_s, s_s, p_s, qcn_s, u_s, *, lc):
    l = qk_ref.shape[1]
    ch = M_CHUNK
    nc = l // ch
    ncc = lc // ch
    nh = M_HEADS
    ng = 2 * nh
    w = conv_ref[...]
    row = lax.broadcasted_iota(jnp.int32, (ch, 1), 0)

    def conv_chunk(j):
        r0 = pl.multiple_of(j * ch, ch)
        cur = qk_ref[0, pl.ds(r0, ch), :]
        prev8 = qk_ref[0, pl.ds(pl.multiple_of(jnp.maximum(r0 - 8, 0), 8), 8), :]
        next8 = qk_ref[0, pl.ds(pl.multiple_of(jnp.minimum(r0 + ch, l - 8), 8), 8), :]
        seg_start = jnp.logical_or(j == 0, j == ncc)
        seg_end = jnp.logical_or(j == ncc - 1, j == nc - 1)
        pe = jnp.where(seg_start, 0.0, prev8[7:8, :])
        ne = jnp.where(seg_end, 0.0, next8[0:1, :])
        xp = jnp.where(row == 0, pe, pltpu.roll(cur, 1, 0))
        xn = jnp.where(row == ch - 1, ne, pltpu.roll(cur, ch - 1, 0))
        y = _silu(xp * w[0:1] + cur * w[1:2] + xn * w[2:3])
        q_s[pl.ds(r0, ch), :] = (y[:, :M_WIDTH] * (M_DH ** -0.5)).astype(BF16)
        kt_s[:, pl.ds(r0, ch)] = y[:, M_WIDTH:].T.astype(BF16)

    ri = lax.broadcasted_iota(jnp.int32, (ch, ch), 0)
    ci = lax.broadcasted_iota(jnp.int32, (ch, ch), 1)
    lower = ri >= ci
    upper = ri <= ci
    ones_blk = jnp.ones((ch, M_DH), BF16)
    fwd_c = lax.broadcasted_iota(jnp.int32, (ch, ng), 1) < nh
    fwd_r = lax.broadcasted_iota(jnp.int32, (ng, ch), 0) < nh
    lane_c = lax.broadcasted_iota(jnp.int32, (ch, ng), 1)

    def local_chunk(j):
        r0 = pl.multiple_of(j * ch, ch)
        rows = pl.ds(r0, ch)
        lfc = _log_sigmoid(gf_ref[0, rows, :])
        gr = gr_ref[0, :, rows]
        lfr = _log_sigmoid(gr[ng:])
        pre_c = _scan(lfc, jnp.add, 0.0, 0, False)
        pre_r = _scan(lfr, jnp.add, 0.0, 1, False)
        b_c = jnp.where(fwd_c, pre_c, jnp.sum(lfc, axis=0, keepdims=True) + lfc - pre_c)
        b_r = jnp.where(fwd_r, pre_r, jnp.sum(lfr, axis=1, keepdims=True) + lfr - pre_r)
        g_c = gi_ref[0, rows, :] - b_c
        g_r = gr[:ng] - b_r
        cg_c = jnp.where(fwd_c, _scan(g_c, jnp.maximum, -jnp.inf, 0, False),
                         _scan(g_c, jnp.maximum, -jnp.inf, 0, True))
        bc_s[rows, :] = b_c
        br_s[:, rows] = b_r
        ml_s[rows, :] = b_c + cg_c
        dl = jnp.zeros((ch, ng), F32)
        for hh in range(nh):
            sl = slice(hh * M_DH, (hh + 1) * M_DH)
            s_s[hh] = _dot(q_s[rows, sl], kt_s[sl, rows])
        for hh in range(nh):
            s = s_s[hh]
            for d in range(2):
                jj = d * nh + hh
                wgt = jnp.exp(jnp.where(upper if d else lower, g_r[jj:jj + 1, :] - cg_c[:, jj:jj + 1], -jnp.inf))
                p_s[hh, d * ch:(d + 1) * ch] = (s * wgt).astype(BF16)
        for hh in range(nh):
            sl = slice(hh * M_DH, (hh + 1) * M_DH)
            v1 = jnp.concatenate([vo_ref[0, rows, sl], ones_blk], axis=1)
            nd2 = _dot(p_s[hh], v1)
            for d in range(2):
                jj = d * nh + hh
                nd = nd2[d * ch:(d + 1) * ch]
                h_s[d, rows, sl] = nd[:, :M_DH]
                dl = jnp.where(lane_c == jj, nd[:, M_DH:M_DH + ng], dl)
        dl_s[rows, :] = dl

    def conv_local_body(j, carry):
        conv_chunk(j + 1)
        local_chunk(j)
        return carry

    conv_chunk(jnp.int32(0))
    lax.fori_loop(0, nc - 1, conv_local_body, 0)
    local_chunk(jnp.int32(nc - 1))

    cn_s[...] = jnp.zeros_like(cn_s)
    m_s[...] = jnp.zeros_like(m_s)
    lane_r = lax.broadcasted_iota(jnp.int32, (1, ng), 1)

    def scan_issue(r0, d):
        rows = pl.ds(r0, ch)
        gr = gr_ref[0, :, rows]
        br = br_s[:, rows]
        tot = jnp.sum(_log_sigmoid(gr[ng:]), axis=1, keepdims=True)
        scal = []
        for hh in range(nh):
            fi = d * nh + hh
            sl = slice(hh * M_DH, (hh + 1) * M_DH)
            qcn_s[fi] = _dot(q_s[rows, sl], cn_s[fi].astype(BF16))
            m_old = m_s[fi][:, 0:1]
            b_e = tot[fi:fi + 1, :]
            d_end = b_e - br[fi:fi + 1, :] + gr[fi:fi + 1, :]
            m_end = jnp.max(d_end, axis=-1, keepdims=True)
            m_new = jnp.maximum(b_e + m_old, m_end)
            ktw = (kt_s[sl, rows].astype(F32) * jnp.exp(d_end - m_end)).astype(BF16)
            v1 = jnp.concatenate([vo_ref[0, rows, sl], ones_blk], axis=1)
            u_s[fi] = _dot(ktw, v1)
            scal.append((m_old, m_new, jnp.exp(b_e + m_old - m_new), jnp.exp(m_end - m_new)))
        return scal

    def scan_finish(r0, d, scal):
        rows = pl.ds(r0, ch)
        m_row = jnp.zeros((1, ng), F32)
        for hh in range(nh):
            m_row = jnp.where(lane_r == d * nh + hh, scal[hh][0], m_row)

        inter = bc_s[rows, :] + m_row
        ml = ml_s[rows, :]
        mt = jnp.maximum(inter, ml)
        a = jnp.exp(ml - mt)
        wi = jnp.exp(inter - mt)
        qn = jnp.zeros((ch, ng), F32)
        for hh in range(nh):
            fi = d * nh + hh
            qn = jnp.where(lane_c == fi, qcn_s[fi, :, M_DH:M_DH + ng], qn)
        den = a * dl_s[rows, :] + wi * qn
        rinv = 1.0 / jnp.maximum(jnp.abs(den), jnp.exp(-mt))
        c_loc = a * rinv
        c_int = wi * rinv

        for hh in range(nh):
            fi = d * nh + hh
            sl = slice(hh * M_DH, (hh + 1) * M_DH)
            h_s[d, rows, sl] = (c_loc[:, fi:fi + 1] * h_s[d, rows, sl]
                                + c_int[:, fi:fi + 1] * qcn_s[fi, :, :M_DH])
            cn_s[fi] = scal[hh][2] * cn_s[fi] + scal[hh][3] * u_s[fi]
            m_s[fi] = jnp.broadcast_to(scal[hh][1], (1, M_DH))

    def scan_body(s, carry):
        rf = pl.multiple_of(s * ch, ch)
        rb = pl.multiple_of(jnp.where(s < ncc, ncc - 1 - s, nc - 1 - s + ncc) * ch, ch)
        sf = scan_issue(rf, 0)
        sb = scan_issue(rb, 1)
        scan_finish(rf, 0, sf)
        scan_finish(rb, 1, sb)
        return carry

    lax.fori_loop(0, nc, scan_body, 0)

    mnorm = mnorm_ref[...]

    def out_body(j, carry):
        r0 = pl.multiple_of(j * ch, ch)
        hsum = h_s[0, pl.ds(r0, ch), :] + h_s[1, pl.ds(r0, ch), :]
        og = _sigmoid(vo_ref[0, pl.ds(r0, ch), M_WIDTH:].astype(F32))
        for hh in range(M_HEADS):
            sl = slice(hh * M_DH, (hh + 1) * M_DH)
            ya_ref[0, pl.ds(r0, ch), sl] = (_rms(hsum[:, sl], mnorm[:, sl]) * og[:, sl]).astype(BF16)
        return carry

    lax.fori_loop(0, nc, out_body, 0)


def _mlstm_call(qk, vo, gi, gf, gr, conv, mnorm, lc):
    b, l, _ = qk.shape
    ng = 2 * M_HEADS
    return pl.pallas_call(
        functools.partial(_mlstm_kernel, lc=lc),
        grid=(b,),
        in_specs=[pl.BlockSpec((1, l, 2 * M_WIDTH), lambda bi: (bi, 0, 0)),
                  pl.BlockSpec((1, l, 2 * M_WIDTH), lambda bi: (bi, 0, 0)),
                  pl.BlockSpec((1, l, ng), lambda bi: (bi, 0, 0)),
                  pl.BlockSpec((1, l, ng), lambda bi: (bi, 0, 0)),
                  pl.BlockSpec((1, 2 * ng, l), lambda bi: (bi, 0, 0)),
                  _const_spec(conv.shape), _const_spec(mnorm.shape)],
        out_specs=pl.BlockSpec((1, l, M_WIDTH), lambda bi: (bi, 0, 0)),
        out_shape=jax.ShapeDtypeStruct((b, l, M_WIDTH), BF16),
        scratch_shapes=[pltpu.VMEM((l, M_WIDTH), BF16),
                        pltpu.VMEM((M_WIDTH, l), BF16),
                        pltpu.VMEM((2, l, M_WIDTH), F32),
                        pltpu.VMEM((l, ng), F32),
                        pltpu.VMEM((l, ng), F32),
                        pltpu.VMEM((l, ng), F32),
                        pltpu.VMEM((ng, l), F32),
                        pltpu.VMEM((2 * M_HEADS, M_DH, 2 * M_DH), F32),
                        pltpu.VMEM((2 * M_HEADS, 1, M_DH), F32),
                        pltpu.VMEM((M_HEADS, M_CHUNK, M_CHUNK), F32),
                        pltpu.VMEM((M_HEADS, 2 * M_CHUNK, M_CHUNK), BF16),
                        pltpu.VMEM((2 * M_HEADS, M_CHUNK, 2 * M_DH), F32),
                        pltpu.VMEM((2 * M_HEADS, M_DH, 2 * M_DH), F32)],
        compiler_params=_params("parallel"), name="mlstm",
    )(qk, vo, gi, gf, gr, conv, mnorm)


def _attn_kernel(q_ref, k_ref, v_ref, o_ref, s_s, p_s, *, lc, ctx_out):
    tq = q_ref.shape[1]
    l = k_ref.shape[1]
    qi = pl.program_id(2)
    nct = lc // tq
    lane = lax.broadcasted_iota(jnp.int32, (tq, 2 * A_VDIM), 1)

    def run(klen):
        outs = []
        for hh in range(A_HPS):
            sl = slice(hh * A_PAD, (hh + 1) * A_PAD)
            s_s[hh, :, :klen] = _dot_nt(q_ref[0, :, sl], k_ref[0, :klen, sl])
        row_max = [jnp.max(s_s[hh, :, :klen], axis=-1, keepdims=True) for hh in range(A_HPS)]
        for hh in range(A_HPS):
            p_s[hh, :, :klen] = jnp.exp2((s_s[hh, :, :klen] - row_max[hh]).astype(BF16))
        for hh in range(A_HPS):
            sl = slice(hh * A_PAD, (hh + 1) * A_PAD)
            nd = _dot(p_s[hh, :, :klen], v_ref[0, :klen, sl])
            outs.append(nd / pltpu.roll(nd, A_VDIM, 1))
        for pp in range(A_HPS // 2):
            o_ref[0, :, pp * A_PAD:(pp + 1) * A_PAD] = jnp.where(
                lane < A_VDIM, outs[2 * pp], pltpu.roll(outs[2 * pp + 1], A_VDIM, 1)).astype(BF16)

    @pl.when(qi >= nct)
    def _():
        run(l)

    @pl.when(qi < nct)
    def _():
        if ctx_out:
            run(lc)
        else:
            o_ref[...] = jnp.zeros_like(o_ref)


def _attn_call(q, k, v, lc, tq, ctx_out):
    b, l, _ = q.shape
    return pl.pallas_call(
        functools.partial(_attn_kernel, lc=lc, ctx_out=ctx_out),
        grid=(b, A_HEADS // A_HPS, l // tq),
        in_specs=[pl.BlockSpec((1, tq, A_HPS * A_PAD), lambda bi, p, qi: (bi, qi, p)),
                  pl.BlockSpec((1, l, A_HPS * A_PAD), lambda bi, p, qi: (bi, 0, p)),
                  pl.BlockSpec((1, l, A_HPS * A_PAD), lambda bi, p, qi: (bi, 0, p))],
        out_specs=pl.BlockSpec((1, tq, A_HPS * A_VDIM), lambda bi, p, qi: (bi, qi, p)),
        out_shape=jax.ShapeDtypeStruct((b, l, A_WIDTH), BF16),
        scratch_shapes=[pltpu.VMEM((A_HPS, tq, l), F32), pltpu.VMEM((A_HPS, tq, l), BF16)],
        compiler_params=_params("parallel", "parallel", "arbitrary"), name="attn",
    )(q, k, v)


def _merge_kernel(x_ref, mod_ref, modc_ref, ya_ref, yb_ref, yc_ref, br_ref, wpa_ref, wpb_ref, wpc_ref, wout_ref,
                  n2_ref, r_ref, rb_ref, o_ref, h2_o, idx_o, idxt_o, cnt_o, y_s, *, tm, nct, skip_ctx):
    d = x_ref.shape[2]
    nsub = x_ref.shape[1] // tm
    modb = mod_ref[0]
    modc = modc_ref[0]
    tiles = [slice(s * tm, (s + 1) * tm) for s in range(nsub)]
    ctx = [pl.program_id(1) * nsub + s < nct for s in range(nsub)]
    mods = [jnp.where(c, modc, modb) for c in ctx]
    for s, rows in enumerate(tiles):
        br = br_ref[0, rows, :]
        y = (br[:, :d].astype(F32) * _dot(ya_ref[0, rows, :], wpa_ref[...])
             + br[:, d:2 * d].astype(F32) * _dot(yb_ref[0, rows, :], wpb_ref[...])
             + br[:, 2 * d:].astype(F32) * _dot(yc_ref[0, rows, :], wpc_ref[...]))
        y_s[s] = y.astype(BF16)
    for s, rows in enumerate(tiles):
        o_ref[0, rows, :] = x_ref[0, rows, :] + mods[s][2:3] * _dot(y_s[s], wout_ref[...])
    for s, rows in enumerate(tiles):
        logits = _route_logits(o_ref[0, rows, :], mods[s], n2_ref, r_ref, rb_ref, h2_o, rows)
        left_out = ctx[s] if skip_ctx else None
        _route_assign(logits, left_out, d, h2_o, idx_o, idxt_o, cnt_o, rows, s)


def _merge_call(xs, mod, ya, yb, yc, br, w, n2, r3, rb, lc, tm, skip_ctx):
    b, l, d = xs.shape
    nct = lc // tm
    nsub = next(n for n in (3, 2, 1) if l % (n * tm) == 0)
    tg = nsub * tm
    tok = lambda width: pl.BlockSpec((1, tg, width), lambda bi, j: (bi, j, 0))
    consts = [w["wpa"], w["wpb"], w["wpc"], w["wout"], n2, r3, rb]
    return pl.pallas_call(
        functools.partial(_merge_kernel, tm=tm, nct=nct, skip_ctx=skip_ctx), grid=(b, l // tg),
        in_specs=[tok(d), pl.BlockSpec((1, 6, d), lambda bi, j: (bi, 0, 0)),
                  pl.BlockSpec((1, 6, d), lambda bi, j: (b, 0, 0)),
                  tok(M_WIDTH), tok(A_WIDTH), tok(G_WIDTH), tok(3 * d)] + [_const_spec(a.shape) for a in consts],
        out_specs=[tok(d), tok(d + R_PAD), tok(8), pl.BlockSpec((1, 8, tg), lambda bi, j: (bi, 0, j)),
                   pl.BlockSpec((1, nsub, 8, R_PAD), lambda bi, j: (bi, j, 0, 0))],
        out_shape=[jax.ShapeDtypeStruct((b, l, d), F32),
                   jax.ShapeDtypeStruct((b, l, d + R_PAD), BF16),
                   jax.ShapeDtypeStruct((b, l, 8), jnp.int32), jax.ShapeDtypeStruct((b, 8, l), jnp.int32),
                   jax.ShapeDtypeStruct((b, l // tm, 8, R_PAD), F32)],
        scratch_shapes=[pltpu.VMEM((nsub, tm, d), BF16)],
        compiler_params=_params("parallel", "parallel"), name="merge",
    )(xs, mod, mod, ya, yb, yc, br, *consts)


def _route_logits(x, mod, n2_ref, r_ref, rb_ref, h2_o, rows):
    d = x.shape[1]
    h2 = _rms(x, n2_ref[...]) * (1.0 + mod[4:5]) + mod[3:4]
    h2_o[0, rows, :d] = h2.astype(BF16)
    r = r_ref[...]
    pp = sum(_dot(piece, r) for piece in _split3(h2)[:2])
    return pp + pltpu.roll(pp, R_PAD - R_SEG, 1) + pltpu.roll(pp, R_PAD - 2 * R_SEG, 1) + rb_ref[...]


def _route_assign(logits, left_out, d, h2_o, idx_o, idxt_o, cnt_o, rows, s):
    tm = logits.shape[0]
    el = logits[:, :N_EXPERTS]
    gl = logits[:, N_EXPERTS:N_EXPERTS + N_GROUPS]
    big = 1e9

    lane_g = lax.broadcasted_iota(jnp.int32, (tm, N_GROUPS), 1).astype(F32)
    gmax = jnp.max(gl, axis=-1, keepdims=True)
    g_sel = jnp.min(jnp.where(gl == gmax, lane_g, big), axis=-1, keepdims=True)
    g_prob = 1.0 / jnp.sum(jnp.exp(gl - gmax), axis=-1, keepdims=True)

    lane_i = lax.broadcasted_iota(jnp.int32, (tm, N_EXPERTS), 1)
    lane_e = lane_i.astype(F32)
    lane_grp = (lane_i // EXP_PER_GROUP).astype(F32)
    v1 = jnp.where(lane_grp == g_sel, el, -jnp.inf)
    t1 = jnp.max(v1, axis=-1, keepdims=True)
    i1 = jnp.min(jnp.where(v1 == t1, lane_e, big), axis=-1, keepdims=True)
    v2 = jnp.where(lane_e == i1, -jnp.inf, v1)
    t2 = jnp.max(v2, axis=-1, keepdims=True)
    i2 = jnp.min(jnp.where(v2 == t2, lane_e, big), axis=-1, keepdims=True)
    e21 = jnp.exp(t2 - t1)
    w1 = 1.0 / (1.0 + e21)
    w2 = e21 * w1
    comb = (jnp.where(lane_e == i1, w1, 0.0) + jnp.where(lane_e == i2, w2, 0.0)) * g_prob
    tail = jnp.zeros((tm, R_PAD - 3 * N_EXPERTS), BF16)
    h2_o[0, rows, d:] = jnp.concatenate(list(_split3(comb)) + [tail], axis=1)

    if left_out is not None:
        g_sel = jnp.where(left_out, -1.0, g_sel)
    lane_p = lax.broadcasted_iota(jnp.int32, (tm, R_PAD), 1)
    onehot = jnp.where(lane_p.astype(F32) == g_sel, 1.0, 0.0)
    ri = lax.broadcasted_iota(jnp.int32, (tm, tm), 0)
    ci = lax.broadcasted_iota(jnp.int32, (tm, tm), 1)
    before = jnp.where(ri > ci, 1.0, 0.0).astype(BF16)
    rank = jnp.sum(_dot(before, onehot.astype(BF16)) * onehot, axis=-1, keepdims=True)
    cnt_o[0, s] = jnp.broadcast_to(jnp.sum(onehot, axis=0, keepdims=True), (8, R_PAD))
    fields = jnp.where(lane_p == 0, g_sel, jnp.where(lane_p == 1, rank, 0.0))
    idx_o[0, rows, :] = fields[:, :8].astype(jnp.int32)
    idxt_o[0, :, rows] = fields.T[:8, :].astype(jnp.int32)


def _experts_kernel(st_ref, h2_ref, idx_ref, idxt_ref, w1_ref, w3_ref, w2_ref, o_ref, hs_s, ys_s, a_s, hid_s, *, tb):
    l = h2_ref.shape[1]
    d = o_ref.shape[2]
    nblk = l // tb
    ch = MOE_CHUNK
    bi = pl.program_id(0)
    g = pl.program_id(1)

    @pl.when(g == 0)
    def _():
        ys_s[...] = jnp.zeros_like(ys_s)

    def group_base(gg):
        return (bi * N_GROUPS + gg) * (nblk + 1)

    def group_offset(upto):
        off = 0
        for gg in range(N_GROUPS - 1):
            padded = ((st_ref[group_base(gg) + nblk] + ch - 1) // ch) * ch
            off = off + jnp.where(gg < upto, padded, 0)
        return off

    base = group_base(g)
    cnt = st_ref[base + nblk]
    goff = group_offset(g)
    gw = min(MOE_GATHER_BLOCKS, nblk)
    nwin = -(-nblk // gw)

    def chunk(lo, ch):
        lane_e = lax.broadcasted_iota(jnp.int32, (ch, N_EXPERTS), 1)
        sub_iota = lax.broadcasted_iota(jnp.int32, (ch, gw * tb), 0)
        lane_blk = lax.broadcasted_iota(jnp.int32, (1, gw * tb), 1) // tb
        k0 = 0
        for k in range(nblk):
            k0 = k0 + jnp.where(st_ref[base + k + 1] <= lo, 1, 0)

        def window(j):
            want = k0 + j * gw
            kj = jnp.minimum(want, nblk - gw)
            tok0 = pl.multiple_of(kj * tb, tb)
            it = idxt_ref[0, :, pl.ds(tok0, gw * tb)]
            offs = jnp.concatenate([jnp.full((1, tb), st_ref[base + kj + i] - lo, jnp.int32) for i in range(gw)],
                                   axis=1)
            mine = jnp.where(lane_blk >= want - kj, it[0:1], -1)
            pos = jnp.where(mine == g, it[1:2] + offs, -1)
            p = jnp.where(sub_iota == pos, 1.0, 0.0).astype(BF16)
            return _dot(p, h2_ref[0, pl.ds(tok0, gw * tb), :])

        hs_s[:ch] = window(0)
        for j in range(1, nwin):
            first = k0 + j * gw

            @pl.when(jnp.logical_and(first < nblk, st_ref[base + jnp.minimum(first, nblk)] < lo + ch))
            def _(j=j):
                hs_s[:ch] += window(j)

        hsb = hs_s[:ch, :d].astype(BF16)
        cs = (hs_s[:ch, d:d + N_EXPERTS] + hs_s[:ch, d + N_EXPERTS:d + 2 * N_EXPERTS]
              + hs_s[:ch, d + 2 * N_EXPERTS:d + 3 * N_EXPERTS])
        for e in range(EXP_PER_GROUP):
            a_s[2 * e, :ch] = _dot(hsb, w1_ref[e])
            a_s[2 * e + 1, :ch] = _dot(hsb, w3_ref[e])
        for e in range(EXP_PER_GROUP):
            ce = jnp.sum(jnp.where(lane_e == g * EXP_PER_GROUP + e, cs, 0.0), axis=-1, keepdims=True)
            hid_s[e, :ch] = (_silu(a_s[2 * e, :ch]) * a_s[2 * e + 1, :ch] * ce).astype(BF16)
        y = jnp.zeros((ch, d), F32)
        for e in range(EXP_PER_GROUP):
            y = y + _dot(hid_s[e, :ch], w2_ref[e])
        ys_s[pl.ds(pl.multiple_of(goff + lo, ch), ch), :] = y.astype(BF16)

    half = ch // 2
    nfull = cnt // ch
    rem = cnt - nfull * ch
    nloop = nfull + jnp.where(rem > half, 1, 0)

    def chunk_body(c, carry):
        chunk(c * ch, ch)
        return carry

    lax.fori_loop(0, nloop, chunk_body, 0)

    @pl.when(jnp.logical_and(rem > 0, rem <= half))
    def _():
        chunk(nfull * ch, half)

    @pl.when(g == N_GROUPS - 1)
    def _():
        lane_w = lax.broadcasted_iota(jnp.int32, (tb, ch), 1)
        goffs = [group_offset(gg) for gg in range(N_GROUPS)]

        def window(k, gg, shift):
            s_k = st_ref[group_base(gg) + k]
            win = (s_k // half) * half
            ic = idx_ref[0, k * tb:(k + 1) * tb, :]
            pos = jnp.where(ic[:, 0:1] == gg, ic[:, 1:2] + (s_k - win - shift), -1)
            q = jnp.where(lane_w == pos, 1.0, 0.0).astype(BF16)
            start = pl.multiple_of(goffs[gg] + win + shift, half)
            return _dot(q, ys_s[pl.ds(start, ch), :])

        for k in range(nblk):
            acc = window(k, 0, 0)
            for gg in range(1, N_GROUPS):
                acc = acc + window(k, gg, 0)
            o_ref[0, k * tb:(k + 1) * tb, :] = acc.astype(BF16)
        for k in range(nblk):
            rows = slice(k * tb, (k + 1) * tb)
            for gg in range(N_GROUPS):
                s_k = st_ref[group_base(gg) + k]
                e_k = st_ref[group_base(gg) + k + 1]

                @pl.when(e_k - (s_k // half) * half > ch)
                def _(k=k, gg=gg, rows=rows):
                    o_ref[0, rows, :] = (o_ref[0, rows, :].astype(F32) + window(k, gg, ch)).astype(BF16)


def _experts_call(starts, h2, idx, idxt, w1, w3, w2, layer, tb):
    b, l, de = h2.shape
    d = de - R_PAD
    whole = lambda width: pl.BlockSpec((1, l, width), lambda bi, g, st: (bi, 0, 0))
    grid_spec = pltpu.PrefetchScalarGridSpec(
        num_scalar_prefetch=1, grid=(b, N_GROUPS),
        in_specs=[pl.BlockSpec((1, l, de), lambda bi, g, st: (bi, 0, 0), pipeline_mode=pl.Buffered(1)),
                  whole(8), pl.BlockSpec((1, 8, l), lambda bi, g, st: (bi, 0, 0)),
                  pl.BlockSpec((None, EXP_PER_GROUP, d, D_EXPERT), lambda bi, g, st: (layer, g, 0, 0)),
                  pl.BlockSpec((None, EXP_PER_GROUP, d, D_EXPERT), lambda bi, g, st: (layer, g, 0, 0)),
                  pl.BlockSpec((None, EXP_PER_GROUP, D_EXPERT, d), lambda bi, g, st: (layer, g, 0, 0))],
        out_specs=whole(d),
        scratch_shapes=[pltpu.VMEM((MOE_CHUNK, de), F32),
                        pltpu.VMEM((l + (N_GROUPS + 1) * MOE_CHUNK, d), BF16),
                        pltpu.VMEM((2 * EXP_PER_GROUP, MOE_CHUNK, D_EXPERT), F32),
                        pltpu.VMEM((EXP_PER_GROUP, MOE_CHUNK, D_EXPERT), BF16)])
    return pl.pallas_call(
        functools.partial(_experts_kernel, tb=tb), grid_spec=grid_spec,
        out_shape=jax.ShapeDtypeStruct((b, l, d), BF16),
        compiler_params=_params("parallel", "arbitrary"), name="experts",
    )(starts, h2, idx, idxt, w1, w3, w2)


def _group_starts(cnt):
    c = cnt[:, :, 0, :N_GROUPS].astype(jnp.int32)
    s = jnp.cumsum(c, axis=1)
    s = jnp.concatenate([jnp.zeros_like(s[:, :1]), s], axis=1)
    return jnp.transpose(s, (0, 2, 1)).reshape(-1)


def _final_kernel(x_ref, mod_ref, f_ref, g_ref, o_ref):
    o_ref[0] = _rms(x_ref[0] + mod_ref[0][5:6] * f_ref[0].astype(F32), g_ref[...])


def _final_call(xs, mod, f, g, lc, tm):
    b, l, d = xs.shape
    tf = next(t for t in (4 * tm, 2 * tm, tm) if (l - lc) % t == 0)
    lat = pl.BlockSpec((pl.Element(1), pl.Element(tf), pl.Element(d)),
                       lambda bi, j: (bi, pl.multiple_of(lc + j * tf, 128), 0))
    return pl.pallas_call(
        _final_kernel, grid=(b, (l - lc) // tf),
        in_specs=[lat, pl.BlockSpec((1, 6, d), lambda bi, j: (bi, 0, 0)), lat, _const_spec(g.shape)],
        out_specs=pl.BlockSpec((1, tf, d), lambda bi, j: (bi, j, 0)),
        out_shape=jax.ShapeDtypeStruct((b, l - lc, d), F32),
        compiler_params=_params("parallel", "parallel"), name="final_norm",
    )(xs, mod, f, g)


def _rope_tables(t_len, lc):
    half = A_ROPE // 2
    rows = t_len // GRID_W
    r = jnp.repeat(jnp.arange(rows, dtype=F32), GRID_W)
    col = jnp.tile(jnp.arange(GRID_W, dtype=F32), rows)
    inv = ROPE_THETA ** (-jnp.arange(0, half, 2, dtype=F32) / half)
    ang = jnp.concatenate([r[:, None] * inv, col[:, None] * inv], axis=-1)
    cos = jnp.concatenate([jnp.ones((lc, half), F32), jnp.cos(ang)], axis=0)
    sin = jnp.concatenate([jnp.zeros((lc, half), F32), jnp.sin(ang)], axis=0)
    l = lc + t_len
    ones = jnp.ones((l, A_NOPE), F32)
    zeros = jnp.zeros((l, A_NOPE), F32)
    tail1 = jnp.ones((l, A_PAD - A_NOPE - A_ROPE), F32)
    tail0 = jnp.zeros((l, A_PAD - A_NOPE - A_ROPE), F32)
    zh = jnp.zeros((l, half), F32)
    cos_t = jnp.concatenate([ones, cos, cos, tail1], axis=-1)
    sina_t = jnp.concatenate([zeros, zh, sin, tail0], axis=-1)
    sinb_t = jnp.concatenate([zeros, -sin, zh, tail0], axis=-1)
    return cos_t, sina_t, sinb_t


def _layer_weights(l, w_in, m_gate_b, a_qnorm, a_wuq, a_kvnorm, a_wukv, g_ws, g_bs, g_vnorm,
                   w_pa, w_pb, w_pc, w_out):
    d = w_in.shape[1]
    wi = w_in[l]
    o = 0

    def take(n):
        nonlocal o
        s = wi[:, o:o + n]
        o += n
        return s

    mq, mk, mv, mo, mg = take(M_WIDTH), take(M_WIDTH), take(M_WIDTH), take(M_WIDTH), take(4 * M_HEADS)
    aq, akv, akr = take(A_QRANK), take(A_KVRANK), take(A_ROPE)
    gu, gv = take(G_WIDTH), take(G_WIDTH)
    br = take(3 * d)
    nh = M_HEADS
    gb = m_gate_b[l]
    mgo = jnp.concatenate([mg[:, :nh], mg[:, 2 * nh:3 * nh], mg[:, nh:2 * nh], mg[:, 3 * nh:]], axis=1)
    gbo = jnp.concatenate([gb[:nh], gb[2 * nh:3 * nh], gb[nh:2 * nh], gb[3 * nh:]])
    akr_pad = jnp.concatenate([jnp.zeros((d, A_NOPE), F32), akr,
                               jnp.zeros((d, A_PAD - A_NOPE - A_ROPE), F32)], axis=1)
    wuq = a_wuq[l].reshape(A_QRANK, A_HEADS, A_NOPE + A_ROPE)
    wuq = jnp.pad(wuq, ((0, 0), (0, 0), (0, A_PAD - A_NOPE - A_ROPE))).reshape(A_QRANK, A_HEADS * A_PAD)
    wukv = a_wukv[l].reshape(A_KVRANK, A_HEADS, A_NOPE + A_VDIM)
    wuk = jnp.pad(wukv[:, :, :A_NOPE], ((0, 0), (0, 0), (0, A_PAD - A_NOPE))).reshape(A_KVRANK, A_HEADS * A_PAD)
    wuv = jnp.pad(wukv[:, :, A_NOPE:], ((0, 0), (0, 0), (0, A_PAD - A_VDIM))).reshape(A_KVRANK, A_HEADS * A_PAD)
    vone = jnp.tile(jnp.concatenate([jnp.zeros((A_VDIM,), F32), jnp.ones((A_PAD - A_VDIM,), F32)]),
                    A_HEADS).reshape(1, A_HEADS * A_PAD)
    gbs = jnp.repeat(g_bs[l].T, G_DG, axis=1)
    return dict(
        wqk=jnp.concatenate([mq, mk], 1).astype(BF16), wvo=jnp.concatenate([mv, mo], 1).astype(BF16),
        wgt=jnp.pad(mgo, ((0, 0), (0, G_PAD - 4 * nh))).astype(BF16),
        gbt=jnp.pad(gbo, (0, G_PAD - 4 * nh)).reshape(1, G_PAD),
        wa=jnp.concatenate([aq, akv, akr_pad], 1).astype(BF16),
        wg=jnp.concatenate([gu, gv], 1).astype(BF16), wbr=br.astype(BF16),
        aqn=a_qnorm[l].reshape(1, -1), akvn=a_kvnorm[l].reshape(1, -1),
        wuq=wuq.astype(BF16), wuk=wuk.astype(BF16), wuv=wuv.astype(BF16), vone=vone,
        gvn=g_vnorm[l].reshape(1, -1), gws=g_ws[l].astype(BF16), gbs=gbs,
        wpa=w_pa[l].astype(BF16), wpb=w_pb[l].astype(BF16), wpc=w_pc[l].astype(BF16),
        wout=w_out[l].astype(BF16))


def _router_weights(r_group, r_group_b, r_expert, r_expert_b):
    d = r_group.shape[0]
    pad = R_SEG - N_EXPERTS - N_GROUPS
    r = jnp.concatenate([r_expert, r_group, jnp.zeros((d, pad), F32)], axis=1)
    r3 = jnp.concatenate(list(_split3(r)) + [jnp.zeros((d, R_PAD - 3 * R_SEG), BF16)], axis=1)
    rb = jnp.concatenate([r_expert_b, r_group_b, jnp.zeros((R_PAD - N_EXPERTS - N_GROUPS,), F32)])
    return r3, rb.reshape(1, R_PAD)


def _tile(n, lc, candidates):
    for t in candidates:
        if n % t == 0 and lc % t == 0:
            return t
    raise ValueError("sequence lengths must be multiples of 128")


def kernel(x, c, ctx, c_ctx, w_ada, b_ada, norm1, norm2, final_norm, w_in, m_conv, m_gate_b, m_norm, a_qnorm, a_wuq, a_kvnorm, a_wukv, g_ws, g_bs, g_vnorm, w_pa, w_pb, w_pc, w_out, r_group, r_group_b, r_expert, r_expert_b, e_w1, e_w3, e_w2):
    b, t_len, d = x.shape
    lc = ctx.shape[1]
    l = lc + t_len
    depth = w_in.shape[0]
    tm = _tile(l, lc, (256, 128))

    xs = (ctx, x)
    cv = jnp.concatenate([c, c_ctx[None, :]], axis=0)
    mod_all = _ada_call(cv, w_ada, b_ada).reshape(depth, b + 1, 6, d)
    tabs = _rope_tables(t_len, lc)

    w_in_b = w_in.astype(BF16)
    e_w1_b, e_w3_b, e_w2_b = e_w1.astype(BF16), e_w3.astype(BF16), e_w2.astype(BF16)
    f = None
    mod_prev = None
    for li in range(depth):
        last = li == depth - 1
        mod = mod_all[li]
        w = _layer_weights(li, w_in_b, m_gate_b, a_qnorm, a_wuq, a_kvnorm, a_wukv, g_ws, g_bs, g_vnorm,
                           w_pa, w_pb, w_pc, w_out)
        outs = _inproj_call(xs, mod, norm1[li].reshape(1, d), w, tabs, lc, tm, f, mod_prev)
        qk, vo, gi, gf, gr, q, k, v, yc, br = outs[:10]
        if len(outs) > 10:
            xs = outs[10]
        ya = _mlstm_call(qk, vo, gi, gf, gr, m_conv[li], m_norm[li].reshape(1, -1), lc)
        yb = _attn_call(q, k, v, lc, tm, not last)
        r3, rb = _router_weights(r_group[li], r_group_b[li], r_expert[li], r_expert_b[li])
        xs, h2, idx, idxt, cnt = _merge_call(xs, mod, ya, yb, yc, br, w, norm2[li].reshape(1, d),
                                             r3, rb, lc, tm, last)
        f = _experts_call(_group_starts(cnt), h2, idx, idxt, e_w1_b, e_w3_b, e_w2_b, li, tm)
        mod_prev = mod
    return _final_call(xs, mod_prev, f, final_norm.reshape(1, d), lc, tm)
```
